```python
import jax, jax.numpy as jnp
from jax import lax
import numpy as np

D_MODEL = 1024
BATCH = 8
SEQ = 8192
DEPTH = 2

HEAD_DIM = 64
W_A = D_MODEL // 4
W_B = D_MODEL // 2
W_C = D_MODEL // 4
NH_A = W_A // HEAD_DIM
NH_B = W_B // HEAD_DIM
NG_C = 4
G_C = W_C // NG_C
N_HEADS_TOTAL = NH_A + NH_B + W_C // HEAD_DIM
MIX_WIDTH = W_A + W_B + W_C
IN_COLS = 2 * W_A + 3 * W_B + W_C
CHUNK = 128
SB_BLOCK = 128
POOL_WINDOWS = (2, 4, 8, 16)
D_FF = ((8 * D_MODEL // 3) + 127) // 128 * 128
CONV_WIDTH = 3
EPS = 1e-6

kernel_name = "hybrid_sgu_stickbreak_pool_convffn"


def rmsnorm(x, g):
    xf = x.astype(jnp.float32)
    y = xf * lax.rsqrt(jnp.mean(xf * xf, axis=-1, keepdims=True) + EPS)
    return (y * g.astype(jnp.float32)).astype(x.dtype)


def chunked_sgu(a, sgu_norm_g, sgu_w, sgu_b):
    B, S, _ = a.shape
    u, v = jnp.split(jax.nn.gelu(a, approximate=False), 2, axis=-1)
    v = rmsnorm(v.reshape(B, S, NH_A, HEAD_DIM), sgu_norm_g.reshape(NH_A, HEAD_DIM))
    v = v.reshape(B, S // CHUNK, CHUNK, NH_A, HEAD_DIM)
    tril = jnp.tril(jnp.ones((CHUNK, CHUNK), dtype=bool))
    wm = jnp.where(tril[None], sgu_w, jnp.zeros_like(sgu_w)).astype(v.dtype)
    s = jnp.einsum('hts,bcshd->bcthd', wm, v)
    s = s + jnp.transpose(sgu_b)[None, None, :, :, None].astype(v.dtype)
    return u * s.reshape(B, S, W_A)


def _stick_breaking_block(qb, k, v, q_start):
    T = qb.shape[2]
    S = k.shape[2]
    z = jnp.einsum('bhtd,bhsd->bhts', qb, k).astype(jnp.float32) * (HEAD_DIM ** -0.5)
    t_idx = q_start + jnp.arange(T, dtype=jnp.int32)
    s_idx = jnp.arange(S, dtype=jnp.int32)
    mask = (s_idx[None, :] < t_idx[:, None])[None, None]
    log_1m = jnp.where(mask, jax.nn.log_sigmoid(-z), 0.0)
    tail = lax.cumsum(log_1m, axis=3, reverse=True) - log_1m
    log_a = jax.nn.log_sigmoid(z) + tail
    att = jnp.where(mask, jnp.exp(log_a), 0.0)
    out = jnp.einsum('bhts,bhsd->bhtd', att, v.astype(jnp.float32))
    return out.astype(qb.dtype)


def stick_breaking_attention(q, k, v):
    B, S, H, d = q.shape
    nb = S // SB_BLOCK
    qt = jnp.transpose(q, (0, 2, 1, 3)).reshape(B, H, nb, SB_BLOCK, d)
    qt = jnp.transpose(qt, (2, 0, 1, 3, 4))
    kt = jnp.transpose(k, (0, 2, 1, 3))
    vt = jnp.transpose(v, (0, 2, 1, 3))
    starts = jnp.arange(nb, dtype=jnp.int32) * SB_BLOCK
    out = lax.map(lambda args: _stick_breaking_block(args[0], kt, vt, args[1]), (qt, starts))
    out = jnp.transpose(out, (1, 0, 3, 2, 4))
    return out.reshape(B, S, H * d)


def causal_pool_mixer(p, pool_w, pool_scale):
    B, S, _ = p.shape
    pf = p.astype(jnp.float32)
    csum = jnp.cumsum(pf, axis=1)
    pos = jnp.arange(1, S + 1, dtype=jnp.float32)
    outs = []
    for g, w in enumerate(POOL_WINDOWS):
        c = csum[..., g * G_C:(g + 1) * G_C]
        shifted = jnp.pad(c, ((0, 0), (w, 0), (0, 0)))[:, :S]
        cnt = jnp.minimum(pos, float(w))[None, :, None]
        d = (c - shifted) / cnt - pf[..., g * G_C:(g + 1) * G_C]
        outs.append(jnp.einsum('bsc,cd->bsd', d, pool_w[g].astype(jnp.float32)))
    y = jnp.concatenate(outs, axis=-1) * pool_scale.astype(jnp.float32)
    return y.astype(p.dtype)


def conv_gated_ffn(h, w_up, conv_w, conv_b, w_down):
    S = h.shape[1]
    z = h @ w_up
    zp = jnp.pad(z, ((0, 0), (CONV_WIDTH - 1, 0), (0, 0)))
    zc = conv_b
    for k in range(CONV_WIDTH):
        zc = zc + zp[:, k:k + S] * conv_w[k]
    g, u = jnp.split(zc, 2, axis=-1)
    return (jax.nn.silu(g) * u) @ w_down


def _fwd_setup_inputs(seed: int = 0) -> dict:
    key = jax.random.key(seed)
    ks = jax.random.split(key, 16)
    f32 = jnp.float32

    def nrm(k, shape, scale):
        return jax.random.normal(k, shape, f32) * scale

    return {
        "x": nrm(ks[0], (BATCH, SEQ, D_MODEL), 1.0),
        "norm1_g": 1.0 + nrm(ks[1], (DEPTH, D_MODEL), 0.05),
        "w_in": nrm(ks[2], (DEPTH, D_MODEL, IN_COLS), D_MODEL ** -0.5),
        "sgu_norm_g": 1.0 + nrm(ks[3], (DEPTH, W_A), 0.05),
        "sgu_w": nrm(ks[4], (DEPTH, NH_A, CHUNK, CHUNK), 0.05),
        "sgu_b": 1.0 + nrm(ks[5], (DEPTH, NH_A, CHUNK), 0.05),
        "pool_w": nrm(ks[6], (DEPTH, NG_C, G_C, G_C), G_C ** -0.5),
        "pool_scale": 1.0 + nrm(ks[7], (DEPTH, W_C), 0.05),
        "mix_norm_g": 1.0 + nrm(ks[8], (DEPTH, MIX_WIDTH), 0.05),
        "w_o": nrm(ks[9], (DEPTH, MIX_WIDTH, D_MODEL), MIX_WIDTH ** -0.5),
        "norm2_g": 1.0 + nrm(ks[10], (DEPTH, D_MODEL), 0.05),
        "w_up": nrm(ks[11], (DEPTH, D_MODEL, 2 * D_FF), D_MODEL ** -0.5),
        "conv_w": nrm(ks[12], (DEPTH, CONV_WIDTH, 2 * D_FF), CONV_WIDTH ** -0.5),
        "conv_b": nrm(ks[13], (DEPTH, 2 * D_FF), 0.01),
        "w_down": nrm(ks[14], (DEPTH, D_FF, D_MODEL), D_FF ** -0.5),
        "final_g": 1.0 + nrm(ks[15], (D_MODEL,), 0.05),
    }


def _fwd_reference(x, norm1_g, w_in, sgu_norm_g, sgu_w, sgu_b, pool_w, pool_scale,
              mix_norm_g, w_o, norm2_g, w_up, conv_w, conv_b, w_down, final_g):
    B, S, D = x.shape
    for l in range(DEPTH):
        h = rmsnorm(x, norm1_g[l])
        proj = h @ w_in[l]
        a_in = proj[..., :2 * W_A]
        qkv = proj[..., 2 * W_A:2 * W_A + 3 * W_B]
        p_in = proj[..., 2 * W_A + 3 * W_B:]
        q, k, v = jnp.split(qkv.reshape(B, S, 3, NH_B, HEAD_DIM), 3, axis=2)
        y_a = chunked_sgu(a_in, sgu_norm_g[l], sgu_w[l], sgu_b[l])
        y_b = stick_breaking_attention(q[:, :, 0], k[:, :, 0], v[:, :, 0])
        y_c = causal_pool_mixer(p_in, pool_w[l], pool_scale[l])
        y = jnp.concatenate([y_a, y_b, y_c], axis=-1).reshape(B, S, N_HEADS_TOTAL, HEAD_DIM)
        y = rmsnorm(y, mix_norm_g[l].reshape(N_HEADS_TOTAL, HEAD_DIM)).reshape(B, S, MIX_WIDTH)
        x = x + y @ w_o[l]
        h = rmsnorm(x, norm2_g[l])
        x = x + conv_gated_ffn(h, w_up[l], conv_w[l], conv_b[l], w_down[l])
    return rmsnorm(x, final_g)


import jax as _jax
import jax.numpy as _jnp

TWIN_FORMAT = 'train_step'
FWD_PARAMS = ['x', 'norm1_g', 'w_in', 'sgu_norm_g', 'sgu_w', 'sgu_b', 'pool_w', 'pool_scale', 'mix_norm_g', 'w_o', 'norm2_g', 'w_up', 'conv_w', 'conv_b', 'w_down', 'final_g']
TWIN_WEIGHTS = ['norm1_g', 'w_in', 'sgu_norm_g', 'sgu_w', 'sgu_b', 'pool_w', 'pool_scale', 'mix_norm_g', 'w_o', 'norm2_g', 'w_up', 'conv_w', 'conv_b', 'w_down', 'final_g']
TWIN_DIFF_INPUT = 'x'
TWIN_INPUTS = ['x', 'norm1_g', 'w_in', 'sgu_norm_g', 'sgu_w', 'sgu_b', 'pool_w', 'pool_scale', 'mix_norm_g', 'w_o', 'norm2_g', 'w_up', 'conv_w', 'conv_b', 'w_down', 'final_g', 'loss_target', 'm_norm1_g', 'm_w_in', 'm_sgu_norm_g', 'm_sgu_w', 'm_sgu_b', 'm_pool_w', 'm_pool_scale', 'm_mix_norm_g', 'm_w_o', 'm_norm2_g', 'm_w_up', 'm_conv_w', 'm_conv_b', 'm_w_down', 'm_final_g', 'v_norm1_g', 'v_w_in', 'v_sgu_norm_g', 'v_sgu_w', 'v_sgu_b', 'v_pool_w', 'v_pool_scale', 'v_mix_norm_g', 'v_w_o', 'v_norm2_g', 'v_w_up', 'v_conv_w', 'v_conv_b', 'v_w_down', 'v_final_g']
TWIN_OUTPUTS = ['loss', 'grad_x', 'grad_norm1_g', 'grad_w_in', 'grad_sgu_norm_g', 'grad_sgu_w', 'grad_sgu_b', 'grad_pool_w', 'grad_pool_scale', 'grad_mix_norm_g', 'grad_w_o', 'grad_norm2_g', 'grad_w_up', 'grad_conv_w', 'grad_conv_b', 'grad_w_down', 'grad_final_g', 'delta_norm1_g', 'delta_w_in', 'delta_sgu_norm_g', 'delta_sgu_w', 'delta_sgu_b', 'delta_pool_w', 'delta_pool_scale', 'delta_mix_norm_g', 'delta_w_o', 'delta_norm2_g', 'delta_w_up', 'delta_conv_w', 'delta_conv_b', 'delta_w_down', 'delta_final_g', 'new_m_norm1_g', 'new_m_w_in', 'new_m_sgu_norm_g', 'new_m_sgu_w', 'new_m_sgu_b', 'new_m_pool_w', 'new_m_pool_scale', 'new_m_mix_norm_g', 'new_m_w_o', 'new_m_norm2_g', 'new_m_w_up', 'new_m_conv_w', 'new_m_conv_b', 'new_m_w_down', 'new_m_final_g', 'new_v_norm1_g', 'new_v_w_in', 'new_v_sgu_norm_g', 'new_v_sgu_w', 'new_v_sgu_b', 'new_v_pool_w', 'new_v_pool_scale', 'new_v_mix_norm_g', 'new_v_w_o', 'new_v_norm2_g', 'new_v_w_up', 'new_v_conv_w', 'new_v_conv_b', 'new_v_w_down', 'new_v_final_g']
TWIN_LEAF_KINDS = {'loss': 'loss', 'grad_x': 'grad_x', 'grad_norm1_g': 'grad_w', 'grad_w_in': 'grad_w', 'grad_sgu_norm_g': 'grad_w', 'grad_sgu_w': 'grad_w', 'grad_sgu_b': 'grad_w', 'grad_pool_w': 'grad_w', 'grad_pool_scale': 'grad_w', 'grad_mix_norm_g': 'grad_w', 'grad_w_o': 'grad_w', 'grad_norm2_g': 'grad_w', 'grad_w_up': 'grad_w', 'grad_conv_w': 'grad_w', 'grad_conv_b': 'grad_w', 'grad_w_down': 'grad_w', 'grad_final_g': 'grad_w', 'delta_norm1_g': 'delta_w', 'delta_w_in': 'delta_w', 'delta_sgu_norm_g': 'delta_w', 'delta_sgu_w': 'delta_w', 'delta_sgu_b': 'delta_w', 'delta_pool_w': 'delta_w', 'delta_pool_scale': 'delta_w', 'delta_mix_norm_g': 'delta_w', 'delta_w_o': 'delta_w', 'delta_norm2_g': 'delta_w', 'delta_w_up': 'delta_w', 'delta_conv_w': 'delta_w', 'delta_conv_b': 'delta_w', 'delta_w_down': 'delta_w', 'delta_final_g': 'delta_w', 'new_m_norm1_g': 'new_m', 'new_m_w_in': 'new_m', 'new_m_sgu_norm_g': 'new_m', 'new_m_sgu_w': 'new_m', 'new_m_sgu_b': 'new_m', 'new_m_pool_w': 'new_m', 'new_m_pool_scale': 'new_m', 'new_m_mix_norm_g': 'new_m', 'new_m_w_o': 'new_m', 'new_m_norm2_g': 'new_m', 'new_m_w_up': 'new_m', 'new_m_conv_w': 'new_m', 'new_m_conv_b': 'new_m', 'new_m_w_down': 'new_m', 'new_m_final_g': 'new_m', 'new_v_norm1_g': 'new_v', 'new_v_w_in': 'new_v', 'new_v_sgu_norm_g': 'new_v', 'new_v_sgu_w': 'new_v', 'new_v_sgu_b': 'new_v', 'new_v_pool_w': 'new_v', 'new_v_pool_scale': 'new_v', 'new_v_mix_norm_g': 'new_v', 'new_v_w_o': 'new_v', 'new_v_norm2_g': 'new_v', 'new_v_w_up': 'new_v', 'new_v_conv_w': 'new_v', 'new_v_conv_b': 'new_v', 'new_v_w_down': 'new_v', 'new_v_final_g': 'new_v'}


def _forward(args):
    return _fwd_reference(*[args[k] for k in FWD_PARAMS])


def _output_shape():
    out = _jax.eval_shape(lambda: _forward(_fwd_setup_inputs(0)))
    return out.shape, out.dtype

N_MICROBATCH = 1
ADAM_LR = 0.001
ADAM_B1 = 0.9
ADAM_B2 = 0.999
ADAM_EPS = 1e-08
ADAM_WD = 0.01
ADAM_STEP = 10
PER_EXAMPLE_BATCH_AXIS = {'x': 0, 'loss_target': 0}
SHARED_INPUTS = []
_WEIGHT_DTYPES = {'norm1_g': _jnp.float32, 'w_in': _jnp.float32, 'sgu_norm_g': _jnp.float32, 'sgu_w': _jnp.float32, 'sgu_b': _jnp.float32, 'pool_w': _jnp.float32, 'pool_scale': _jnp.float32, 'mix_norm_g': _jnp.float32, 'w_o': _jnp.float32, 'norm2_g': _jnp.float32, 'w_up': _jnp.float32, 'conv_w': _jnp.float32, 'conv_b': _jnp.float32, 'w_down': _jnp.float32, 'final_g': _jnp.float32}
MOMENT_SCALE = {'norm1_g': 2.123510e-01, 'w_in': 1.404927e-01, 'sgu_norm_g': 7.288944e-02, 'sgu_w': 7.898951e-02, 'sgu_b': 4.994058e-02, 'pool_w': 1.891487e-01, 'pool_scale': 2.012790e-01, 'mix_norm_g': 1.976821e-01, 'w_o': 1.949353e-01, 'norm2_g': 1.494008e-01, 'w_up': 5.986381e-02, 'conv_w': 6.067005e-02, 'conv_b': 6.527453e-02, 'w_down': 9.881153e-02, 'final_g': 6.404483e+01}


def _to_microbatches(a, axis):
    t = _jnp.moveaxis(a, axis, 0)
    t = t.reshape((N_MICROBATCH, t.shape[0] // N_MICROBATCH) + t.shape[1:])
    return _jnp.moveaxis(t, 1, axis + 1)


def setup_inputs(seed: int = 0) -> dict:
    inp = _fwd_setup_inputs(seed)
    key = _jax.random.fold_in(_jax.random.key(seed), 7919)
    shape, _ = _output_shape()
    out = dict(inp)
    out["loss_target"] = _jax.random.normal(_jax.random.fold_in(key, 0), shape, _jnp.float32)
    for i, name in enumerate(TWIN_WEIGHTS):
        w = inp[name].astype(_jnp.float32)
        if MOMENT_SCALE is None:
            s = _jnp.sqrt(_jnp.mean(_jnp.square(w)) + 1e-30)
        else:
            s = MOMENT_SCALE[name]
        km, kv = _jax.random.split(_jax.random.fold_in(key, i + 1))
        out[name] = w
        out["m_" + name] = s * _jax.random.normal(km, w.shape, _jnp.float32)
        out["v_" + name] = (s * s) * _jax.random.uniform(kv, w.shape, _jnp.float32, 0.5, 1.5)
    if N_MICROBATCH > 1:
        for name, axis in PER_EXAMPLE_BATCH_AXIS.items():
            out[name] = _to_microbatches(out[name], axis)
    return {'x': out['x'], 'norm1_g': out['norm1_g'], 'w_in': out['w_in'], 'sgu_norm_g': out['sgu_norm_g'], 'sgu_w': out['sgu_w'], 'sgu_b': out['sgu_b'], 'pool_w': out['pool_w'], 'pool_scale': out['pool_scale'], 'mix_norm_g': out['mix_norm_g'], 'w_o': out['w_o'], 'norm2_g': out['norm2_g'], 'w_up': out['w_up'], 'conv_w': out['conv_w'], 'conv_b': out['conv_b'], 'w_down': out['w_down'], 'final_g': out['final_g'], 'loss_target': out['loss_target'], 'm_norm1_g': out['m_norm1_g'], 'm_w_in': out['m_w_in'], 'm_sgu_norm_g': out['m_sgu_norm_g'], 'm_sgu_w': out['m_sgu_w'], 'm_sgu_b': out['m_sgu_b'], 'm_pool_w': out['m_pool_w'], 'm_pool_scale': out['m_pool_scale'], 'm_mix_norm_g': out['m_mix_norm_g'], 'm_w_o': out['m_w_o'], 'm_norm2_g': out['m_norm2_g'], 'm_w_up': out['m_w_up'], 'm_conv_w': out['m_conv_w'], 'm_conv_b': out['m_conv_b'], 'm_w_down': out['m_w_down'], 'm_final_g': out['m_final_g'], 'v_norm1_g': out['v_norm1_g'], 'v_w_in': out['v_w_in'], 'v_sgu_norm_g': out['v_sgu_norm_g'], 'v_sgu_w': out['v_sgu_w'], 'v_sgu_b': out['v_sgu_b'], 'v_pool_w': out['v_pool_w'], 'v_pool_scale': out['v_pool_scale'], 'v_mix_norm_g': out['v_mix_norm_g'], 'v_w_o': out['v_w_o'], 'v_norm2_g': out['v_norm2_g'], 'v_w_up': out['v_w_up'], 'v_conv_w': out['v_conv_w'], 'v_conv_b': out['v_conv_b'], 'v_w_down': out['v_w_down'], 'v_final_g': out['v_final_g']}


def _loss(weights, diff, rest, loss_target):
    with _jax.named_scope("forward"):
        args = {**rest, TWIN_DIFF_INPUT: diff, **{k: w.astype(_WEIGHT_DTYPES[k]) for k, w in weights.items()}}
        y = _forward(args)
    with _jax.named_scope("loss_head"):
        err = _jnp.square(y.astype(_jnp.float32) - loss_target)
        return 0.5 * _jnp.sum(_jnp.mean(err, axis=-1)) if err.ndim else 0.5 * err


def _adamw(w, g, m, v):
    m = ADAM_B1 * m + (1.0 - ADAM_B1) * g
    v = ADAM_B2 * v + (1.0 - ADAM_B2) * _jnp.square(g)
    m_hat = m / (1.0 - ADAM_B1 ** ADAM_STEP)
    v_hat = v / (1.0 - ADAM_B2 ** ADAM_STEP)
    delta = -ADAM_LR * (m_hat / (_jnp.sqrt(v_hat) + ADAM_EPS) + ADAM_WD * w)
    return delta, m, v


def reference(x, norm1_g, w_in, sgu_norm_g, sgu_w, sgu_b, pool_w, pool_scale, mix_norm_g, w_o, norm2_g, w_up, conv_w, conv_b, w_down, final_g, loss_target, m_norm1_g, m_w_in, m_sgu_norm_g, m_sgu_w, m_sgu_b, m_pool_w, m_pool_scale, m_mix_norm_g, m_w_o, m_norm2_g, m_w_up, m_conv_w, m_conv_b, m_w_down, m_final_g, v_norm1_g, v_w_in, v_sgu_norm_g, v_sgu_w, v_sgu_b, v_pool_w, v_pool_scale, v_mix_norm_g, v_w_o, v_norm2_g, v_w_up, v_conv_w, v_conv_b, v_w_down, v_final_g):
    given = dict(x=x, norm1_g=norm1_g, w_in=w_in, sgu_norm_g=sgu_norm_g, sgu_w=sgu_w, sgu_b=sgu_b, pool_w=pool_w, pool_scale=pool_scale, mix_norm_g=mix_norm_g, w_o=w_o, norm2_g=norm2_g, w_up=w_up, conv_w=conv_w, conv_b=conv_b, w_down=w_down, final_g=final_g, loss_target=loss_target, m_norm1_g=m_norm1_g, m_w_in=m_w_in, m_sgu_norm_g=m_sgu_norm_g, m_sgu_w=m_sgu_w, m_sgu_b=m_sgu_b, m_pool_w=m_pool_w, m_pool_scale=m_pool_scale, m_mix_norm_g=m_mix_norm_g, m_w_o=m_w_o, m_norm2_g=m_norm2_g, m_w_up=m_w_up, m_conv_w=m_conv_w, m_conv_b=m_conv_b, m_w_down=m_w_down, m_final_g=m_final_g, v_norm1_g=v_norm1_g, v_w_in=v_w_in, v_sgu_norm_g=v_sgu_norm_g, v_sgu_w=v_sgu_w, v_sgu_b=v_sgu_b, v_pool_w=v_pool_w, v_pool_scale=v_pool_scale, v_mix_norm_g=v_mix_norm_g, v_w_o=v_w_o, v_norm2_g=v_norm2_g, v_w_up=v_w_up, v_conv_w=v_conv_w, v_conv_b=v_conv_b, v_w_down=v_w_down, v_final_g=v_final_g)
    weights = {n: given[n] for n in TWIN_WEIGHTS}
    shared = {n: given[n] for n in SHARED_INPUTS}
    per_example = {n: given[n] for n in ['x']}
    grad_fn = _jax.value_and_grad(_loss, argnums=(0, 1))

    def one_microbatch(ex, loss_target):
        ex = dict(ex)
        diff = ex.pop(TWIN_DIFF_INPUT)
        return grad_fn(weights, diff, {**shared, **ex}, loss_target)

    if N_MICROBATCH == 1:
        loss, (grad_w, grad_x) = one_microbatch(per_example, given["loss_target"])
    else:
        def body(carry, xs):
            loss_sum, grad_sum = carry
            l_k, (gw_k, gx_k) = one_microbatch(xs[0], xs[1])
            with _jax.named_scope("update"):
                return (loss_sum + l_k, _jax.tree.map(_jnp.add, grad_sum, gw_k)), gx_k

        init = (_jnp.zeros((), _jnp.float32), _jax.tree.map(_jnp.zeros_like, weights))
        (loss, grad_w), grad_x = _jax.lax.scan(body, init, (per_example, given["loss_target"]))
    with _jax.named_scope("update"):
        delta_w, new_m, new_v = {}, {}, {}
        for n in TWIN_WEIGHTS:
            delta_w[n], new_m[n], new_v[n] = _adamw(weights[n], grad_w[n], given["m_" + n], given["v_" + n])
    return (loss, grad_x, *[grad_w[n] for n in TWIN_WEIGHTS], *[delta_w[n] for n in TWIN_WEIGHTS],
            *[new_m[n] for n in TWIN_WEIGHTS], *[new_v[n] for n in TWIN_WEIGHTS])
```

```python
import functools

import jax
import jax.numpy as jnp
from jax import lax
from jax.experimental import pallas as pl
from jax.experimental.pallas import tpu as pltpu

F32 = jnp.float32
MXU_DT = jnp.bfloat16

D_MODEL = 1024
DEPTH = 2
HEAD_DIM = 64
W_A = 256
W_B = 512
W_C = 256
IN_COLS = 2 * W_A + 3 * W_B + W_C
CHUNK = 128
POOL_WINDOWS = (2, 4, 8, 16)
D_FF = 2816
EPS = 1e-6
N_CHIPS = 4

ADAM_LR = 0.001
ADAM_B1 = 0.9
ADAM_B2 = 0.999
ADAM_EPS = 1e-08
ADAM_WD = 0.01
ADAM_STEP = 10

LANES = 128
TQ = 256
TM = 256
TM_MM = 512
HALO = 16
VMEM_LIMIT = 56 * 1024 * 1024

ROWS_IN = DEPTH * D_MODEL * (IN_COLS // N_CHIPS) // D_MODEL
ROWS_O = DEPTH * (D_MODEL // N_CHIPS)
ROWS_UP = DEPTH * D_MODEL * (2 * D_FF // N_CHIPS) // D_MODEL
ROWS_DOWN = DEPTH * (D_FF // N_CHIPS)
ROWS_BIG = ROWS_IN + ROWS_O + ROWS_UP + ROWS_DOWN
ROWS_CONV = 16
ROWS_SMALL = 240
ROWS_PACK = ROWS_BIG + ROWS_CONV + ROWS_SMALL
ROWS_GATHER = ROWS_BIG + 2 * ROWS_CONV

SMALL_NAMES = ("norm1_g", "sgu_norm_g", "sgu_w", "sgu_b", "pool_w", "pool_scale",
               "mix_norm_g", "norm2_g", "conv_b", "final_g")
SMALL_SHAPES = {
    "norm1_g": (DEPTH, D_MODEL), "sgu_norm_g": (DEPTH, W_A), "sgu_w": (DEPTH, 4, CHUNK, CHUNK),
    "sgu_b": (DEPTH, 4, CHUNK), "pool_w": (DEPTH, 4, 64, 64), "pool_scale": (DEPTH, W_C),
    "mix_norm_g": (DEPTH, D_MODEL), "norm2_g": (DEPTH, D_MODEL), "conv_b": (DEPTH, 2 * D_FF),
    "final_g": (D_MODEL,),
}


def _call(body, **kw):
    return pl.pallas_call(body, **kw)


def _params(*sem):
    return pltpu.CompilerParams(dimension_semantics=sem, vmem_limit_bytes=VMEM_LIMIT)


def _dot(a, b):
    return jnp.dot(a, b, preferred_element_type=F32)


def _dot_nt(a, b):
    return lax.dot_general(a, b, (((1,), (1,)), ((), ())), preferred_element_type=F32)


def _dot_tn(a, b):
    return lax.dot_general(a, b, (((0,), (0,)), ((), ())), preferred_element_type=F32)


def _split(a):
    hi = a.astype(MXU_DT)
    lo = (a - hi.astype(F32)).astype(MXU_DT)
    return hi, lo


def _dot_split(a, b):
    hi, lo = _split(a)
    return _dot(hi, b) + _dot(lo, b)


def _group_mean(sq, gmat):
    cols = [_dot_split(sq[:, b * LANES:(b + 1) * LANES], gmat) for b in range(sq.shape[1] // LANES)]
    return cols[0] if len(cols) == 1 else jnp.concatenate(cols, axis=-1)


def _group_matrix():
    r = jnp.arange(LANES)
    return jnp.where((r[:, None] // HEAD_DIM) == (r[None, :] // HEAD_DIM), 1.0 / HEAD_DIM, 0.0).astype(MXU_DT)


def _tile(n):
    return max(t for t in range(LANES, 1536 + 1, LANES) if n % t == 0)


def _row_spec(tm, cols, col_block=0):
    return pl.BlockSpec((tm, cols), lambda i, cb=col_block: (i, cb))


def _full_spec(shape):
    nd = len(shape)
    return pl.BlockSpec(shape, lambda *_: (0,) * nd)


def _rms_mm(x, g, w, tn, name):
    S, D = x.shape
    N = w.shape[1]
    tm = TM_MM

    def body(x_ref, g_ref, w_ref, o_ref, ob_ref, h_ref):
        @pl.when(pl.program_id(1) == 0)
        def _():
            xv = x_ref[...]
            r = lax.rsqrt(jnp.mean(xv * xv, axis=-1, keepdims=True) + EPS)
            h_ref[...] = (xv * r * g_ref[...]).astype(h_ref.dtype)

        acc = _dot(h_ref[...], w_ref[...])
        o_ref[...] = acc
        ob_ref[...] = acc.astype(ob_ref.dtype)

    return _call(
        body, name=name, grid=(S // tm, N // tn),
        in_specs=[pl.BlockSpec((tm, D), lambda i, j: (i, 0)),
                  pl.BlockSpec((1, D), lambda i, j: (0, 0)),
                  pl.BlockSpec((D, tn), lambda i, j: (0, j))],
        out_specs=[pl.BlockSpec((tm, tn), lambda i, j: (i, j)),
                   pl.BlockSpec((tm, tn), lambda i, j: (i, j)),
                   pl.BlockSpec((tm, D), lambda i, j: (i, 0))],
        out_shape=[jax.ShapeDtypeStruct((S, N), F32), jax.ShapeDtypeStruct((S, N), MXU_DT),
                   jax.ShapeDtypeStruct((S, D), MXU_DT)],
        compiler_params=_params("parallel", "arbitrary"),
    )(x, g, w)


def _mm_res(a, w, res, name):
    S, K = a.shape
    N = w.shape[1]
    tm = TM_MM

    def body(a_ref, w_ref, r_ref, o_ref):
        o_ref[...] = r_ref[...] + _dot(a_ref[...], w_ref[...])

    return _call(
        body, name=name, grid=(S // tm,),
        in_specs=[_row_spec(tm, K), _full_spec((K, N)), _row_spec(tm, N)],
        out_specs=_row_spec(tm, N),
        out_shape=jax.ShapeDtypeStruct((S, N), F32),
        compiler_params=_params("parallel"),
    )(a, w, res)


def _mm_nt(a, w, name):
    S, K = a.shape
    N = w.shape[0]
    tm = TM_MM

    def body(a_ref, w_ref, o_ref):
        o_ref[...] = _dot_nt(a_ref[...].astype(MXU_DT), w_ref[...])

    return _call(
        body, name=name, grid=(S // tm,),
        in_specs=[_row_spec(tm, K), _full_spec((N, K))],
        out_specs=_row_spec(tm, N),
        out_shape=jax.ShapeDtypeStruct((S, N), F32),
        compiler_params=_params("parallel"),
    )(a, w)


def _mm_tn(a, b, name):
    S, K1 = a.shape
    N = b.shape[1]
    ts = TM_MM
    tk = _tile(K1)
    tn = _tile(N)

    def body(a_ref, b_ref, o_ref):
        @pl.when(pl.program_id(2) == 0)
        def _():
            o_ref[...] = jnp.zeros_like(o_ref)

        o_ref[...] += _dot_tn(a_ref[...], b_ref[...].astype(MXU_DT))

    return _call(
        body, name=name, grid=(K1 // tk, N // tn, S // ts),
        in_specs=[pl.BlockSpec((ts, tk), lambda m, n, s: (s, m)),
                  pl.BlockSpec((ts, tn), lambda m, n, s: (s, n))],
        out_specs=pl.BlockSpec((tk, tn), lambda m, n, s: (m, n)),
        out_shape=jax.ShapeDtypeStruct((K1, N), F32),
        compiler_params=_params("parallel", "parallel", "arbitrary"),
    )(a, b)


def _mm_nt_rmsbwd(pairs, x, g, dres, name):
    S, D = x.shape
    tm = TM
    n = len(pairs)

    def body(*refs):
        a_refs = refs[:n]
        w_refs = refs[n:2 * n]
        x_ref, g_ref, r_ref, dx_ref, dg_ref = refs[2 * n:]
        dh = _dot_nt(a_refs[0][...], w_refs[0][...])
        for k in range(1, n):
            dh += _dot_nt(a_refs[k][...], w_refs[k][...])
        xv = x_ref[...]
        r = lax.rsqrt(jnp.mean(xv * xv, axis=-1, keepdims=True) + EPS)
        xhat = xv * r

        @pl.when(pl.program_id(0) == 0)
        def _():
            dg_ref[...] = jnp.zeros_like(dg_ref)

        dg_ref[...] += jnp.sum(dh * xhat, axis=0, keepdims=True)
        dxh = dh * g_ref[...]
        dx_ref[...] = r_ref[...] + r * (dxh - xhat * jnp.mean(dxh * xhat, axis=-1, keepdims=True))

    in_specs = ([_row_spec(tm, a.shape[1]) for a, _ in pairs] + [_full_spec(w.shape) for _, w in pairs]
                + [_row_spec(tm, D), _full_spec((1, D)), _row_spec(tm, D)])
    return _call(
        body, name=name, grid=(S // tm,), in_specs=in_specs,
        out_specs=[_row_spec(tm, D), _full_spec((1, D))],
        out_shape=[jax.ShapeDtypeStruct((S, D), F32), jax.ShapeDtypeStruct((1, D), F32)],
        compiler_params=_params("arbitrary"),
    )(*[a for a, _ in pairs], *[w for _, w in pairs], x, g, dres)


def _loss_head(x, g, tgt):
    S, D = x.shape
    tm = TM

    def body(x_ref, g_ref, t_ref, dx_ref, dg_ref, l_ref):
        xv = x_ref[...]
        r = lax.rsqrt(jnp.mean(xv * xv, axis=-1, keepdims=True) + EPS)
        xhat = xv * r
        diff = xhat * g_ref[...] - t_ref[...]

        @pl.when(pl.program_id(0) == 0)
        def _():
            dg_ref[...] = jnp.zeros_like(dg_ref)
            l_ref[...] = jnp.zeros_like(l_ref)

        l_ref[...] += jnp.full(l_ref.shape, 0.5 * jnp.sum(jnp.mean(diff * diff, axis=-1, keepdims=True)), F32)
        dout = diff * (1.0 / D)
        dg_ref[...] += jnp.sum(dout * xhat, axis=0, keepdims=True)
        dxh = dout * g_ref[...]
        dx_ref[...] = r * (dxh - xhat * jnp.mean(dxh * xhat, axis=-1, keepdims=True))

    return _call(
        body, name="loss_head", grid=(S // tm,),
        in_specs=[_row_spec(tm, D), _full_spec((1, D)), _row_spec(tm, D)],
        out_specs=[_row_spec(tm, D), _full_spec((1, D)), _full_spec((8, LANES))],
        out_shape=[jax.ShapeDtypeStruct((S, D), F32), jax.ShapeDtypeStruct((1, D), F32),
                   jax.ShapeDtypeStruct((8, LANES), F32)],
        compiler_params=_params("arbitrary"),
    )(x, g, tgt)


def _mix_out(ya, yb, yc, gm, wo, x, gmat):
    S = x.shape[0]
    tm = TM

    def body(ya_ref, yb_ref, yc_ref, gm_ref, wo_ref, x_ref, gmat_ref, x2_ref, yn_ref):
        y = jnp.concatenate([ya_ref[...], yb_ref[...], yc_ref[...]], axis=-1)
        r = lax.rsqrt(_group_mean(y * y, gmat_ref[...]) + EPS)
        yn = (y * r * gm_ref[...]).astype(MXU_DT)
        yn_ref[...] = yn
        x2_ref[...] = x_ref[...] + _dot(yn, wo_ref[...])

    return _call(
        body, name="mix_out", grid=(S // tm,),
        in_specs=[_row_spec(tm, W_A), _row_spec(tm, W_B), _row_spec(tm, W_C), _full_spec((1, D_MODEL)),
                  _full_spec((D_MODEL, D_MODEL)), _row_spec(tm, D_MODEL), _full_spec((LANES, LANES))],
        out_specs=[_row_spec(tm, D_MODEL), _row_spec(tm, D_MODEL)],
        out_shape=[jax.ShapeDtypeStruct((S, D_MODEL), F32), jax.ShapeDtypeStruct((S, D_MODEL), MXU_DT)],
        compiler_params=_params("parallel"),
    )(ya, yb, yc, gm, wo, x, gmat)


def _mix_out_bwd(dx2, wo, ya, yb, yc, gm, gmat):
    S = dx2.shape[0]
    tm = TM

    def body(dx2_ref, wo_ref, ya_ref, yb_ref, yc_ref, gm_ref, gmat_ref, dya_ref, dyb_ref, dyc_ref, dgm_ref):
        dyn = _dot_nt(dx2_ref[...].astype(MXU_DT), wo_ref[...])
        y = jnp.concatenate([ya_ref[...], yb_ref[...], yc_ref[...]], axis=-1)
        r = lax.rsqrt(_group_mean(y * y, gmat_ref[...]) + EPS)
        yhat = y * r

        @pl.when(pl.program_id(0) == 0)
        def _():
            dgm_ref[...] = jnp.zeros_like(dgm_ref)

        dgm_ref[...] += jnp.sum(dyn * yhat, axis=0, keepdims=True)
        dyh = dyn * gm_ref[...]
        dy = r * (dyh - yhat * _group_mean(dyh * yhat, gmat_ref[...]))
        dya_ref[...] = dy[:, :W_A]
        dyb_ref[...] = dy[:, W_A:W_A + W_B]
        dyc_ref[...] = dy[:, W_A + W_B:]

    return _call(
        body, name="mix_out_bwd", grid=(S // tm,),
        in_specs=[_row_spec(tm, D_MODEL), _full_spec((D_MODEL, D_MODEL)), _row_spec(tm, W_A), _row_spec(tm, W_B),
                  _row_spec(tm, W_C), _full_spec((1, D_MODEL)), _full_spec((LANES, LANES))],
        out_specs=[_row_spec(tm, W_A), _row_spec(tm, W_B), _row_spec(tm, W_C), _full_spec((1, D_MODEL))],
        out_shape=[jax.ShapeDtypeStruct((S, W_A), F32), jax.ShapeDtypeStruct((S, W_B), F32),
                   jax.ShapeDtypeStruct((S, W_C), F32), jax.ShapeDtypeStruct((1, D_MODEL), F32)],
        compiler_params=_params("arbitrary"),
    )(dx2, wo, ya, yb, yc, gm, gmat)


_SQRT_HALF = 0.7071067811865476
_INV_SQRT_2PI = 0.3989422804014327


def _sgu_common(a, sng, wm_ref, bias, gmat):
    phi = 0.5 * (1.0 + lax.erf(a * _SQRT_HALF))
    ga = a * phi
    u = ga[:, :W_A]
    v = ga[:, W_A:]
    r = lax.rsqrt(_group_mean(v * v, gmat) + EPS)
    vhat = v * r
    vn = (vhat * sng).astype(MXU_DT)
    head = lax.broadcasted_iota(jnp.int32, (CHUNK, W_A), 1) // HEAD_DIM
    rows = []
    for c in range(a.shape[0] // CHUNK):
        vc = vn[c * CHUNK:(c + 1) * CHUNK]
        s = bias
        for h in range(4):
            s = s + jnp.where(head == h, _dot(wm_ref[h], vc), 0.0)
        rows.append(s)
    s = jnp.concatenate(rows, axis=0)
    return phi, u, r, vhat, vn, s


def _tril_weights(sgu_w_l):
    t = jnp.arange(CHUNK)
    return jnp.where((t[None, :] <= t[:, None])[None], sgu_w_l, 0.0)


def _sgu_fwd(proj, sng, wm, bias, gmat):
    S = proj.shape[0]
    tm = TM

    def body(a_ref, sng_ref, wm_ref, b_ref, gmat_ref, y_ref):
        _, u, _, _, _, s = _sgu_common(a_ref[...], sng_ref[...], wm_ref, b_ref[...], gmat_ref[...])
        y_ref[...] = u * s

    return _call(
        body, name="sgu_fwd", grid=(S // tm,),
        in_specs=[_row_spec(tm, 2 * W_A), _full_spec((1, W_A)), _full_spec((4, CHUNK, CHUNK)),
                  _full_spec((CHUNK, W_A)), _full_spec((LANES, LANES))],
        out_specs=_row_spec(tm, W_A),
        out_shape=jax.ShapeDtypeStruct((S, W_A), F32),
        compiler_params=_params("parallel"),
    )(proj, sng, wm, bias, gmat)


def _sgu_bwd(proj, dy, sng, wm, wmt, bias, gmat):
    S = proj.shape[0]
    tm = TM

    def body(a_ref, dy_ref, sng_ref, wm_ref, wmt_ref, b_ref, gmat_ref, da_ref, dw_ref, db_ref, dsng_ref):
        a = a_ref[...]
        dy = dy_ref[...]
        gmat = gmat_ref[...]
        sng = sng_ref[...]
        phi, u, r, vhat, vn, s = _sgu_common(a, sng, wm_ref, b_ref[...], gmat)
        du = dy * s
        ds = dy * u

        @pl.when(pl.program_id(0) == 0)
        def _():
            dw_ref[...] = jnp.zeros_like(dw_ref)
            db_ref[...] = jnp.zeros_like(db_ref)
            dsng_ref[...] = jnp.zeros_like(dsng_ref)

        head = lax.broadcasted_iota(jnp.int32, (CHUNK, W_A), 1) // HEAD_DIM
        tt = lax.broadcasted_iota(jnp.int32, (CHUNK, CHUNK), 0)
        ss = lax.broadcasted_iota(jnp.int32, (CHUNK, CHUNK), 1)
        rows = []
        for c in range(tm // CHUNK):
            dsc = ds[c * CHUNK:(c + 1) * CHUNK]
            vc = vn[c * CHUNK:(c + 1) * CHUNK]
            db_ref[...] += dsc
            dsb = dsc.astype(MXU_DT)
            dvn = jnp.zeros((CHUNK, W_A), F32)
            for h in range(4):
                dvn = dvn + jnp.where(head == h, _dot(wmt_ref[h], dsb), 0.0)
                dsh = jnp.where(head == h, dsc, 0.0).astype(MXU_DT)
                dw_ref[h] += jnp.where(ss <= tt, _dot_nt(dsh, vc), 0.0)
            rows.append(dvn)
        dvn = jnp.concatenate(rows, axis=0)
        dsng_ref[...] += jnp.sum(dvn * vhat, axis=0, keepdims=True)
        dvh = dvn * sng
        dv = r * (dvh - vhat * _group_mean(dvh * vhat, gmat))
        dga = jnp.concatenate([du, dv], axis=-1)
        dgelu = phi + a * (_INV_SQRT_2PI * jnp.exp(-0.5 * a * a))
        da_ref[...] = (dga * dgelu).astype(da_ref.dtype)

    return _call(
        body, name="sgu_bwd", grid=(S // tm,),
        in_specs=[_row_spec(tm, 2 * W_A), _row_spec(tm, W_A), _full_spec((1, W_A)), _full_spec((4, CHUNK, CHUNK)),
                  _full_spec((4, CHUNK, CHUNK)), _full_spec((CHUNK, W_A)), _full_spec((LANES, LANES))],
        out_specs=[_row_spec(tm, 2 * W_A), _full_spec((4, CHUNK, CHUNK)), _full_spec((CHUNK, W_A)),
                   _full_spec((1, W_A))],
        out_shape=[jax.ShapeDtypeStruct((S, 2 * W_A), MXU_DT), jax.ShapeDtypeStruct((4, CHUNK, CHUNK), F32),
                   jax.ShapeDtypeStruct((CHUNK, W_A), F32), jax.ShapeDtypeStruct((1, W_A), F32)],
        compiler_params=_params("arbitrary"),
    )(proj, dy, sng, wm, wmt, bias, gmat)


Q_BLK0 = (2 * W_A) // LANES
K_BLK0 = Q_BLK0 + W_B // LANES
V_BLK0 = K_BLK0 + W_B // LANES
N_PAIRS = W_B // LANES


def _tri_matrix():
    r = jnp.arange(TQ)
    return (r[:, None] > r[None, :]).astype(MXU_DT)


def _sb_scores(qh, kj, tri, masked):
    z = _dot_nt(qh, kj)
    sp = jnp.log(1.0 + jnp.exp(-jnp.abs(z)))
    lsp = jnp.minimum(z, 0.0) - sp
    lsm = lsp - z
    msk = None
    if masked:
        row = lax.broadcasted_iota(jnp.int32, z.shape, 0)
        col = lax.broadcasted_iota(jnp.int32, z.shape, 1)
        msk = col < row
        lsm = jnp.where(msk, lsm, 0.0)
    tail = _dot_split(lsm, tri)
    return lsp, lsm, tail, msk


def _sb_fwd(proj_b, tri):
    S = proj_b.shape[0]
    nq = S // TQ
    assert nq <= LANES

    def body(q_ref, k_ref, v_ref, tri_ref, o_ref, rb_ref, acc_ref):
        i = pl.program_id(1)
        lane = lax.broadcasted_iota(jnp.int32, (TQ, LANES), 1)
        qs = q_ref[...].astype(F32) * (HEAD_DIM ** -0.5)
        tri = tri_ref[...]
        for h in range(2):
            qh = jnp.where(lane // HEAD_DIM == h, qs, 0.0).astype(MXU_DT)
            rb_ref[h] = jnp.zeros((TQ, LANES), F32)

            def block(j, run, masked, h=h, qh=qh):
                start = pl.multiple_of(j * TQ, TQ)
                kj = k_ref[pl.ds(start, TQ), :]
                vj = v_ref[pl.ds(start, TQ), :]
                lsp, lsm, tail, msk = _sb_scores(qh, kj, tri, masked)
                rb_ref[h] = jnp.where(lane == j, run, rb_ref[h])
                att = jnp.exp(lsp + tail + run)
                if masked:
                    att = jnp.where(msk, att, 0.0)
                pv = _dot(att.astype(MXU_DT), vj)
                if masked:
                    acc_ref[h] = pv
                else:
                    acc_ref[h] += pv
                return run + tail[:, :1] + lsm[:, :1]

            run = block(i, jnp.zeros((TQ, 1), F32), True)
            lax.fori_loop(0, i, lambda n, run: block(i - 1 - n, run, False), run)
        o_ref[...] = jnp.where(lane < HEAD_DIM, acc_ref[0], acc_ref[1])

    return _call(
        body, name="sb_fwd", grid=(N_PAIRS, nq),
        in_specs=[pl.BlockSpec((TQ, LANES), lambda p, i: (i, Q_BLK0 + p)),
                  pl.BlockSpec((S, LANES), lambda p, i: (0, K_BLK0 + p)),
                  pl.BlockSpec((S, LANES), lambda p, i: (0, V_BLK0 + p)),
                  pl.BlockSpec((TQ, TQ), lambda p, i: (0, 0))],
        out_specs=[pl.BlockSpec((TQ, LANES), lambda p, i: (i, p)),
                   pl.BlockSpec((2, TQ, LANES), lambda p, i: (p, i, 0))],
        out_shape=[jax.ShapeDtypeStruct((S, W_B), F32), jax.ShapeDtypeStruct((2 * N_PAIRS, S, LANES), F32)],
        scratch_shapes=[pltpu.VMEM((2, TQ, LANES), F32)],
        compiler_params=_params("parallel", "arbitrary"),
    )(proj_b, proj_b, proj_b, tri)


def _sb_bwd(proj_b, dyb, rb, tri, trit):
    S = proj_b.shape[0]
    nq = S // TQ

    def body(q_ref, k_ref, v_ref, do_ref, rb_ref, tri_ref, trit_ref, dq_ref, dk_ref, dv_ref,
             dq_acc, dk_acc, dv_acc):
        i = pl.program_id(1)
        lane = lax.broadcasted_iota(jnp.int32, (TQ, LANES), 1)
        scale = HEAD_DIM ** -0.5
        qs = q_ref[...].astype(F32) * scale
        do = do_ref[...]
        tri = tri_ref[...]
        trit = trit_ref[...]

        @pl.when(i == 0)
        def _():
            dk_acc[...] = jnp.zeros_like(dk_acc)
            dv_acc[...] = jnp.zeros_like(dv_acc)

        for h in range(2):
            hm = lane // HEAD_DIM == h
            qh = jnp.where(hm, qs, 0.0).astype(MXU_DT)
            doh = jnp.where(hm, do, 0.0).astype(MXU_DT)
            dq_acc[h] = jnp.zeros((TQ, LANES), F32)

            def block(j, pre, masked, h=h, qh=qh, doh=doh):
                start = pl.multiple_of(j * TQ, TQ)
                kj = k_ref[pl.ds(start, TQ), :]
                vj = v_ref[pl.ds(start, TQ), :]
                lsp, lsm, tail, msk = _sb_scores(qh, kj, tri, masked)
                run = jnp.sum(jnp.where(lane == j, rb_ref[h], 0.0), axis=-1, keepdims=True)
                att = jnp.exp(lsp + tail + run)
                if masked:
                    att = jnp.where(msk, att, 0.0)
                beta = jnp.exp(lsp)
                dl = _dot_nt(doh, vj) * att
                cin = _dot_split(dl, trit)
                dz = dl * (1.0 - beta) - beta * (pre + cin)
                if masked:
                    dz = jnp.where(msk, dz, 0.0)
                dzb = dz.astype(MXU_DT)
                dq_acc[h] += _dot(dzb, kj)
                dk_acc[pl.ds(start, TQ), :] += _dot_tn(dzb, qh)
                dv_acc[pl.ds(start, TQ), :] += _dot_tn(att.astype(MXU_DT), doh)
                return pre + cin[:, TQ - 1:] + dl[:, TQ - 1:]

            pre = lax.fori_loop(0, i, lambda j, pre: block(j, pre, False), jnp.zeros((TQ, 1), F32))
            block(i, pre, True)
        dq_ref[...] = (jnp.where(lane < HEAD_DIM, dq_acc[0], dq_acc[1]) * scale).astype(dq_ref.dtype)

        @pl.when(i == nq - 1)
        def _():
            dk_ref[...] = dk_acc[...].astype(dk_ref.dtype)
            dv_ref[...] = dv_acc[...].astype(dv_ref.dtype)

    return _call(
        body, name="sb_bwd", grid=(N_PAIRS, nq),
        in_specs=[pl.BlockSpec((TQ, LANES), lambda p, i: (i, Q_BLK0 + p)),
                  pl.BlockSpec((S, LANES), lambda p, i: (0, K_BLK0 + p)),
                  pl.BlockSpec((S, LANES), lambda p, i: (0, V_BLK0 + p)),
                  pl.BlockSpec((TQ, LANES), lambda p, i: (i, p)),
                  pl.BlockSpec((2, TQ, LANES), lambda p, i: (p, i, 0)),
                  pl.BlockSpec((TQ, TQ), lambda p, i: (0, 0)),
                  pl.BlockSpec((TQ, TQ), lambda p, i: (0, 0))],
        out_specs=[pl.BlockSpec((TQ, LANES), lambda p, i: (i, p)),
                   pl.BlockSpec((S, LANES), lambda p, i: (0, p)),
                   pl.BlockSpec((S, LANES), lambda p, i: (0, p))],
        out_shape=[jax.ShapeDtypeStruct((S, W_B), MXU_DT)] * 3,
        scratch_shapes=[pltpu.VMEM((2, TQ, LANES), F32), pltpu.VMEM((S, LANES), F32), pltpu.VMEM((S, LANES), F32)],
        compiler_params=_params("parallel", "arbitrary"),
    )(proj_b, proj_b, proj_b, dyb, rb, tri, trit)


P_BLK = (2 * W_A + 3 * W_B) // W_C


def _window_lanes():
    g = lax.broadcasted_iota(jnp.int32, (1, W_C), 1) // (W_C // 4)
    w = jnp.where(g == 0, POOL_WINDOWS[0], jnp.where(g == 1, POOL_WINDOWS[1],
                  jnp.where(g == 2, POOL_WINDOWS[2], POOL_WINDOWS[3])))
    return g, w


def _shift_rows(ext, k, tm, lead):
    n = ext.shape[0]
    return pltpu.roll(ext, shift=k % n, axis=0)[lead:lead + tm]


def _pool_diff(p_cur, p_halo, row0, tm):
    ext = jnp.concatenate([p_halo, p_cur], axis=0)
    g, w = _window_lanes()
    acc = ext
    sums = []
    for sh in (1, 2, 4, 8):
        acc = acc + pltpu.roll(acc, shift=sh, axis=0)
        sums.append(acc[HALO:HALO + tm])
    wsum = jnp.where(g == 0, sums[0], jnp.where(g == 1, sums[1], jnp.where(g == 2, sums[2], sums[3])))
    pos = (row0 + 1 + lax.broadcasted_iota(jnp.int32, (tm, W_C), 0)).astype(F32)
    cnt = jnp.minimum(pos, w.astype(F32))
    return wsum / cnt - p_cur, cnt


def _pool_specs(tm, nrow_blocks_halo):
    cur = pl.BlockSpec((tm, W_C), lambda i: (i, P_BLK))
    prev = pl.BlockSpec((HALO, W_C), lambda i: (jnp.maximum(i * (tm // HALO) - 1, 0), P_BLK))
    return cur, prev


def _pool_fwd(proj, wbd, scale):
    S = proj.shape[0]
    tm = TM

    def body(p_ref, ph_ref, w_ref, sc_ref, y_ref):
        i = pl.program_id(0)
        halo = jnp.where(i > 0, ph_ref[...], 0.0)
        d, _ = _pool_diff(p_ref[...], halo, i * tm, tm)
        y_ref[...] = _dot(d.astype(MXU_DT), w_ref[...]) * sc_ref[...]

    cur, prev = _pool_specs(tm, S // HALO)
    return _call(
        body, name="pool_fwd", grid=(S // tm,),
        in_specs=[cur, prev, _full_spec((W_C, W_C)), _full_spec((1, W_C))],
        out_specs=_row_spec(tm, W_C),
        out_shape=jax.ShapeDtypeStruct((S, W_C), F32),
        compiler_params=_params("parallel"),
    )(proj, proj, wbd, scale)


def _pool_bwd_a(proj, dy, wbd, scale):
    S = proj.shape[0]
    tm = TM

    def body(p_ref, ph_ref, dy_ref, w_ref, sc_ref, dd_ref, e_ref, dw_ref, dsc_ref):
        i = pl.program_id(0)
        halo = jnp.where(i > 0, ph_ref[...], 0.0)
        d, cnt = _pool_diff(p_ref[...], halo, i * tm, tm)
        db = d.astype(MXU_DT)
        dy = dy_ref[...]

        @pl.when(i == 0)
        def _():
            dw_ref[...] = jnp.zeros_like(dw_ref)
            dsc_ref[...] = jnp.zeros_like(dsc_ref)

        dsc_ref[...] += jnp.sum(dy * _dot(db, w_ref[...]), axis=0, keepdims=True)
        dys = (dy * sc_ref[...]).astype(MXU_DT)
        dw_ref[...] += _dot_tn(db, dys)
        dd = _dot_nt(dys, w_ref[...])
        dd_ref[...] = dd
        e_ref[...] = dd / cnt

    cur, prev = _pool_specs(tm, S // HALO)
    return _call(
        body, name="pool_bwd_a", grid=(S // tm,),
        in_specs=[cur, prev, _row_spec(tm, W_C), _full_spec((W_C, W_C)), _full_spec((1, W_C))],
        out_specs=[_row_spec(tm, W_C), _row_spec(tm, W_C), _full_spec((W_C, W_C)), _full_spec((1, W_C))],
        out_shape=[jax.ShapeDtypeStruct((S, W_C), F32), jax.ShapeDtypeStruct((S, W_C), F32),
                   jax.ShapeDtypeStruct((W_C, W_C), F32), jax.ShapeDtypeStruct((1, W_C), F32)],
        compiler_params=_params("arbitrary"),
    )(proj, proj, dy, wbd, scale)


def _pool_bwd_b(dd, e):
    S = dd.shape[0]
    tm = TM
    nb = S // tm

    def body(dd_ref, e_ref, en_ref, dp_ref):
        i = pl.program_id(0)
        halo = jnp.where(i < nb - 1, en_ref[...], 0.0)
        ext = jnp.concatenate([e_ref[...], halo], axis=0)
        n = ext.shape[0]
        g, _ = _window_lanes()
        acc = ext
        sums = []
        for sh in (1, 2, 4, 8):
            acc = acc + pltpu.roll(acc, shift=n - sh, axis=0)
            sums.append(acc[:tm])
        wsum = jnp.where(g == 0, sums[0], jnp.where(g == 1, sums[1], jnp.where(g == 2, sums[2], sums[3])))
        dp_ref[...] = (wsum - dd_ref[...]).astype(dp_ref.dtype)

    nxt = pl.BlockSpec((HALO, W_C), lambda i: (jnp.minimum((i + 1) * (tm // HALO), S // HALO - 1), 0))
    return _call(
        body, name="pool_bwd_b", grid=(nb,),
        in_specs=[_row_spec(tm, W_C), _row_spec(tm, W_C), nxt],
        out_specs=_row_spec(tm, W_C),
        out_shape=jax.ShapeDtypeStruct((S, W_C), MXU_DT),
        compiler_params=_params("parallel"),
    )(dd, e, e)


TN_FF = 1408
NB_FF = D_FF // TN_FF
CONV_ROWS = 8


def _conv(z_cur, z_halo, cwb, tm):
    ext = jnp.concatenate([z_halo, z_cur], axis=0)
    z2 = _shift_rows(ext, 2, tm, HALO)
    z1 = _shift_rows(ext, 1, tm, HALO)
    zc = cwb[3:4] + z2 * cwb[0:1] + z1 * cwb[1:2] + z_cur * cwb[2:3]
    return zc, z2, z1


def _ffn_specs(tm, order):
    def mk(f):
        return (lambda i, j: f(i, j)) if order == "ij" else (lambda j, i: f(i, j))
    hb = tm // HALO
    return [
        pl.BlockSpec((tm, TN_FF), mk(lambda i, j: (i, j))),
        pl.BlockSpec((tm, TN_FF), mk(lambda i, j: (i, j + NB_FF))),
        pl.BlockSpec((HALO, TN_FF), mk(lambda i, j: (jnp.maximum(i * hb - 1, 0), j))),
        pl.BlockSpec((HALO, TN_FF), mk(lambda i, j: (jnp.maximum(i * hb - 1, 0), j + NB_FF))),
        pl.BlockSpec((CONV_ROWS, TN_FF), mk(lambda i, j: (0, j))),
        pl.BlockSpec((CONV_ROWS, TN_FF), mk(lambda i, j: (0, j + NB_FF))),
    ]


def _conv_gate(z, cwb):
    S = z.shape[0]
    tm = TM

    def body(zg_ref, zu_ref, hg_ref, hu_ref, cg_ref, cu_ref, f_ref):
        first = pl.program_id(0) == 0
        g, _, _ = _conv(zg_ref[...], jnp.where(first, 0.0, hg_ref[...]), cg_ref[...], tm)
        u, _, _ = _conv(zu_ref[...], jnp.where(first, 0.0, hu_ref[...]), cu_ref[...], tm)
        f_ref[...] = (g * jax.nn.sigmoid(g) * u).astype(f_ref.dtype)

    return _call(
        body, name="conv_gate", grid=(S // tm, NB_FF), in_specs=_ffn_specs(tm, "ij"),
        out_specs=pl.BlockSpec((tm, TN_FF), lambda i, j: (i, j)),
        out_shape=jax.ShapeDtypeStruct((S, D_FF), MXU_DT),
        compiler_params=_params("parallel", "parallel"),
    )(z, z, z, z, cwb, cwb)


def _conv_gate_bwd(z, df, cwb):
    S = z.shape[0]
    tm = TM

    def body(zg_ref, zu_ref, hg_ref, hu_ref, cg_ref, cu_ref, df_ref, dg_ref, du_ref, dcg_ref, dcu_ref):
        i = pl.program_id(1)
        first = i == 0
        zg = zg_ref[...]
        zu = zu_ref[...]
        g, g2, g1 = _conv(zg, jnp.where(first, 0.0, hg_ref[...]), cg_ref[...], tm)
        u, u2, u1 = _conv(zu, jnp.where(first, 0.0, hu_ref[...]), cu_ref[...], tm)
        sg = jax.nn.sigmoid(g)
        df = df_ref[...]
        dgv = df * u * (sg * (1.0 + g * (1.0 - sg)))
        duv = df * (g * sg)
        dg_ref[...] = dgv
        du_ref[...] = duv

        @pl.when(first)
        def _():
            dcg_ref[...] = jnp.zeros_like(dcg_ref)
            dcu_ref[...] = jnp.zeros_like(dcu_ref)

        rid = lax.broadcasted_iota(jnp.int32, (CONV_ROWS, TN_FF), 0)

        def taps(dv, s2, s1, s0):
            sums = [jnp.sum(dv * s2, axis=0, keepdims=True), jnp.sum(dv * s1, axis=0, keepdims=True),
                    jnp.sum(dv * s0, axis=0, keepdims=True), jnp.sum(dv, axis=0, keepdims=True)]
            out = jnp.zeros((CONV_ROWS, TN_FF), F32)
            for k, v in enumerate(sums):
                out = jnp.where(rid == k, v, out)
            return out

        dcg_ref[...] += taps(dgv, g2, g1, zg)
        dcu_ref[...] += taps(duv, u2, u1, zu)

    acc = pl.BlockSpec((CONV_ROWS, TN_FF), lambda j, i: (0, j))
    tile = pl.BlockSpec((tm, TN_FF), lambda j, i: (i, j))
    return _call(
        body, name="conv_gate_bwd", grid=(NB_FF, S // tm), in_specs=_ffn_specs(tm, "ji") + [tile],
        out_specs=[tile, tile, acc, acc],
        out_shape=[jax.ShapeDtypeStruct((S, D_FF), F32), jax.ShapeDtypeStruct((S, D_FF), F32),
                   jax.ShapeDtypeStruct((CONV_ROWS, D_FF), F32), jax.ShapeDtypeStruct((CONV_ROWS, D_FF), F32)],
        compiler_params=_params("parallel", "arbitrary"),
    )(z, z, z, z, cwb, cwb, df)


def _conv_transpose(dzc, cwb_half, name):
    S = dzc.shape[0]
    tm = TM
    nb = S // tm

    def body(d_ref, dn_ref, c_ref, o_ref):
        last = pl.program_id(0) == nb - 1
        cur = d_ref[...]
        ext = jnp.concatenate([cur, jnp.where(last, 0.0, dn_ref[...])], axis=0)
        c = c_ref[...]
        o_ref[...] = (cur * c[2:3] + _shift_rows(ext, -1, tm, 0) * c[1:2]
                      + _shift_rows(ext, -2, tm, 0) * c[0:1]).astype(o_ref.dtype)

    hb = tm // HALO
    return _call(
        body, name=name, grid=(nb, NB_FF),
        in_specs=[pl.BlockSpec((tm, TN_FF), lambda i, j: (i, j)),
                  pl.BlockSpec((HALO, TN_FF), lambda i, j: (jnp.minimum((i + 1) * hb, S // HALO - 1), j)),
                  pl.BlockSpec((CONV_ROWS, TN_FF), lambda i, j: (0, j))],
        out_specs=pl.BlockSpec((tm, TN_FF), lambda i, j: (i, j)),
        out_shape=jax.ShapeDtypeStruct((S, D_FF), MXU_DT),
        compiler_params=_params("parallel", "parallel"),
    )(dzc, dzc, cwb_half)


def _layer_consts(w, l):
    wm = _tril_weights(w["sgu_w"][l])
    eye = jnp.eye(4, dtype=F32)
    wbd = (w["pool_w"][l][:, :, None, :] * eye[:, None, :, None]).reshape(W_C, W_C)
    cwb = jnp.concatenate([w["conv_w"][l], w["conv_b"][l][None], jnp.zeros((CONV_ROWS - 4, 2 * D_FF), F32)], axis=0)
    return dict(
        g1=w["norm1_g"][l][None], g2=w["norm2_g"][l][None], gm=w["mix_norm_g"][l][None],
        sng=w["sgu_norm_g"][l][None], wm=wm.astype(MXU_DT), wmt=jnp.swapaxes(wm, 1, 2).astype(MXU_DT),
        bias=jnp.repeat(jnp.transpose(w["sgu_b"][l]), HEAD_DIM, axis=1),
        wbd=wbd.astype(MXU_DT), scale=w["pool_scale"][l][None], cwb=cwb,
        w_in=w["w_in"][l], w_o=w["w_o"][l], w_up=w["w_up"][l], w_down=w["w_down"][l],
    )


def _local_step(x, tgt, w):
    gmat = _group_matrix()
    tri = _tri_matrix()
    trit = jnp.transpose(tri)
    saved = []
    for l in range(DEPTH):
        c = _layer_consts(w, l)
        proj, proj_b, h1 = _rms_mm(x, c["g1"], c["w_in"], IN_COLS // 3, "in_proj")
        ya = _sgu_fwd(proj, c["sng"], c["wm"], c["bias"], gmat)
        yb, rb = _sb_fwd(proj_b, tri)
        yc = _pool_fwd(proj, c["wbd"], c["scale"])
        x2, yn = _mix_out(ya, yb, yc, c["gm"], c["w_o"], x, gmat)
        z, _, h2 = _rms_mm(x2, c["g2"], c["w_up"], TN_FF, "up_proj")
        f = _conv_gate(z, c["cwb"])
        x3 = _mm_res(f, c["w_down"], x2, "down_proj")
        saved.append(dict(c=c, x=x, proj=proj, proj_b=proj_b, h1=h1, ya=ya, yb=yb, yc=yc, rb=rb, x2=x2, yn=yn,
                          z=z, h2=h2, f=f))
        x = x3

    dx, d_final_g, loss8 = _loss_head(x, w["final_g"][None], tgt)
    grads = {n: [None] * DEPTH for n in ("norm1_g", "w_in", "sgu_norm_g", "sgu_w", "sgu_b", "pool_w", "pool_scale",
                                         "mix_norm_g", "w_o", "norm2_g", "w_up", "conv_w", "conv_b", "w_down")}
    for l in reversed(range(DEPTH)):
        s = saved[l]
        c = s["c"]
        df = _mm_nt(dx, c["w_down"], "down_proj_bwd")
        grads["w_down"][l] = _mm_tn(s["f"], dx, "down_proj_wgrad")
        dzg, dzu, dcg, dcu = _conv_gate_bwd(s["z"], df, c["cwb"])
        dz_g = _conv_transpose(dzg, c["cwb"][:, :D_FF], "conv_t_gate")
        dz_u = _conv_transpose(dzu, c["cwb"][:, D_FF:], "conv_t_value")
        dcwb = jnp.concatenate([dcg, dcu], axis=1)
        grads["conv_w"][l] = dcwb[:3]
        grads["conv_b"][l] = dcwb[3]
        grads["w_up"][l] = jnp.concatenate([_mm_tn(s["h2"], dz_g, "up_proj_wgrad_gate"),
                                            _mm_tn(s["h2"], dz_u, "up_proj_wgrad_value")], axis=1)
        dx2, dg2 = _mm_nt_rmsbwd([(dz_g, c["w_up"][:, :D_FF]), (dz_u, c["w_up"][:, D_FF:])],
                                 s["x2"], c["g2"], dx, "up_proj_bwd")
        grads["norm2_g"][l] = dg2[0]
        grads["w_o"][l] = _mm_tn(s["yn"], dx2, "out_proj_wgrad")
        dya, dyb, dyc, dgm = _mix_out_bwd(dx2, c["w_o"], s["ya"], s["yb"], s["yc"], c["gm"], gmat)
        grads["mix_norm_g"][l] = dgm[0]
        dd, e, dwbd, dscale = _pool_bwd_a(s["proj"], dyc, c["wbd"], c["scale"])
        dp = _pool_bwd_b(dd, e)
        grads["pool_w"][l] = jnp.stack([dwbd[g * 64:(g + 1) * 64, g * 64:(g + 1) * 64] for g in range(4)])
        grads["pool_scale"][l] = dscale[0]
        dq, dk, dv = _sb_bwd(s["proj_b"], dyb, s["rb"], tri, trit)
        da, dwm, dbias, dsng = _sgu_bwd(s["proj"], dya, c["sng"], c["wm"], c["wmt"], c["bias"], gmat)
        grads["sgu_w"][l] = dwm
        grads["sgu_b"][l] = jnp.transpose(jnp.sum(dbias.reshape(CHUNK, 4, HEAD_DIM), axis=-1))
        grads["sgu_norm_g"][l] = dsng[0]
        dproj = jnp.concatenate([da, dq, dk, dv, dp], axis=1)
        grads["w_in"][l] = _mm_tn(s["h1"], dproj, "in_proj_wgrad")
        dx, dg1 = _mm_nt_rmsbwd([(dproj, c["w_in"])], s["x"], c["g1"], dx2, "in_proj_bwd")
        grads["norm1_g"][l] = dg1[0]

    out = {n: jnp.stack(v) for n, v in grads.items()}
    out["final_g"] = d_final_g[0]
    return loss8[0, 0], dx, out


MESH = pl.DeviceIdType.MESH
ANY = pl.BlockSpec(memory_space=pl.ANY)


def _all_gather(x_half):
    m_per, n = x_half.shape

    def body(x_ref, out_ref, send_sems, recv_sems, local_sem):
        x, y, c = lax.axis_index("x"), lax.axis_index("y"), lax.axis_index("c")
        me, sibling = (x, y, c), (x, y, 1 - c)
        chips = [(1 - x, y), (x, 1 - y), (1 - x, 1 - y)]

        def rows(px, py, pc):
            return out_ref.at[pl.ds((4 * px + 2 * py + pc) * m_per, m_per), :]

        def copy(k, block, to, src=None):
            return pltpu.make_async_remote_copy(
                src_ref=rows(*block) if src is None else src, dst_ref=rows(*block),
                send_sem=send_sems.at[k], recv_sem=recv_sems.at[k], device_id=to, device_id_type=MESH)

        mine = pltpu.make_async_copy(x_ref, rows(*me), local_sem)
        mine.start()
        first = [copy(0, me, sibling, src=x_ref)]
        first += [copy(1 + j, me, (*chip, c), src=x_ref) for j, chip in enumerate(chips)]
        for cp in first:
            cp.start()
        passed = [copy(4 + j, (*chip, c), sibling) for j, chip in enumerate(chips)]
        for j, chip in enumerate(chips):
            copy(1 + j, (*chip, c), me).wait_recv()
            passed[j].start()
        copy(0, sibling, me).wait_recv()
        for j, chip in enumerate(chips):
            copy(4 + j, (*chip, 1 - c), me).wait_recv()
        for cp in first + passed:
            cp.wait_send()
        mine.wait()

    return _call(
        body, name="weight_all_gather",
        out_shape=jax.ShapeDtypeStruct((8 * m_per, n), x_half.dtype),
        in_specs=[ANY], out_specs=ANY,
        scratch_shapes=[pltpu.SemaphoreType.DMA((7,)), pltpu.SemaphoreType.DMA((7,)), pltpu.SemaphoreType.DMA],
    )(x_half)


def _swap_cores(a, name):
    def body(a_ref, out_ref, send_sem, recv_sem):
        x, y, c = lax.axis_index("x"), lax.axis_index("y"), lax.axis_index("c")
        cp = pltpu.make_async_remote_copy(src_ref=a_ref, dst_ref=out_ref, send_sem=send_sem, recv_sem=recv_sem,
                                          device_id=(x, y, 1 - c), device_id_type=MESH)
        cp.start()
        cp.wait()

    return _call(
        body, name=name, out_shape=jax.ShapeDtypeStruct(a.shape, a.dtype), in_specs=[ANY], out_specs=ANY,
        scratch_shapes=[pltpu.SemaphoreType.DMA, pltpu.SemaphoreType.DMA],
    )(a)


def _exchange_chips(a):
    def body(a_ref, out_ref, send_sems, recv_sems, local_sem):
        x, y, c = lax.axis_index("x"), lax.axis_index("y"), lax.axis_index("c")
        my_chip = 2 * x + y
        chips = [(1 - x, y), (x, 1 - y), (1 - x, 1 - y)]
        mine = pltpu.make_async_copy(a_ref.at[my_chip], out_ref.at[my_chip], local_sem)
        mine.start()
        copies = []
        for k, (px, py) in enumerate(chips):
            copies.append(pltpu.make_async_remote_copy(
                src_ref=a_ref.at[2 * px + py], dst_ref=out_ref.at[my_chip],
                send_sem=send_sems.at[k], recv_sem=recv_sems.at[k], device_id=(px, py, c), device_id_type=MESH))
        for cp in copies:
            cp.start()
        for k, (px, py) in enumerate(chips):
            pltpu.make_async_remote_copy(
                src_ref=a_ref.at[my_chip], dst_ref=out_ref.at[2 * px + py],
                send_sem=send_sems.at[k], recv_sem=recv_sems.at[k], device_id=(px, py, c),
                device_id_type=MESH).wait_recv()
        for cp in copies:
            cp.wait_send()
        mine.wait()

    return _call(
        body, name="grad_exchange_chips", out_shape=jax.ShapeDtypeStruct(a.shape, a.dtype),
        in_specs=[ANY], out_specs=ANY,
        scratch_shapes=[pltpu.SemaphoreType.DMA((3,)), pltpu.SemaphoreType.DMA((3,)), pltpu.SemaphoreType.DMA],
    )(a)


def _add2(a, b, name):
    n, r, c = a.shape
    tr = 512
    spec = pl.BlockSpec((1, tr, c), lambda k, i: (k, i, 0))

    def body(a_ref, b_ref, o_ref):
        o_ref[...] = a_ref[...] + b_ref[...]

    return _call(body, name=name, grid=(n, r // tr), in_specs=[spec, spec], out_specs=spec,
                 out_shape=jax.ShapeDtypeStruct(a.shape, a.dtype), compiler_params=_params("parallel", "parallel"))(a, b)


def _sum_chips(a):
    _, r, c = a.shape
    tr = 512

    def body(a_ref, o_ref):
        o_ref[...] = ((a_ref[0] + a_ref[1]) + a_ref[2]) + a_ref[3]

    return _call(body, name="grad_sum_chips", grid=(r // tr,),
                 in_specs=[pl.BlockSpec((N_CHIPS, tr, c), lambda i: (0, i, 0))],
                 out_specs=pl.BlockSpec((tr, c), lambda i: (i, 0)),
                 out_shape=jax.ShapeDtypeStruct((r, c), a.dtype), compiler_params=_params("parallel"))(a)


def _adamw(g, w, m, v):
    r, c = g.shape
    tr = 256
    spec = pl.BlockSpec((tr, c), lambda i: (i, 0))

    def body(g_ref, w_ref, m_ref, v_ref, d_ref, nm_ref, nv_ref):
        gv = g_ref[...]
        nm = ADAM_B1 * m_ref[...] + (1.0 - ADAM_B1) * gv
        nv = ADAM_B2 * v_ref[...] + (1.0 - ADAM_B2) * (gv * gv)
        m_hat = nm / (1.0 - ADAM_B1 ** ADAM_STEP)
        v_hat = nv / (1.0 - ADAM_B2 ** ADAM_STEP)
        d_ref[...] = -ADAM_LR * (m_hat / (jnp.sqrt(v_hat) + ADAM_EPS) + ADAM_WD * w_ref[...])
        nm_ref[...] = nm
        nv_ref[...] = nv

    shp = jax.ShapeDtypeStruct((r, c), F32)
    return _call(body, name="adamw", grid=(r // tr,), in_specs=[spec] * 4, out_specs=[spec] * 3,
                 out_shape=[shp, shp, shp], compiler_params=_params("parallel"))(g, w, m, v)


def _rows(a, rows):
    flat = a.reshape(-1)
    return jnp.pad(flat, (0, rows * D_MODEL - flat.shape[0])).reshape(rows, D_MODEL)


def _small_rows(p, extra=None):
    parts = [p[n].reshape(-1) for n in SMALL_NAMES]
    if extra is not None:
        parts.append(extra.reshape(-1))
    flat = jnp.concatenate(parts)
    return jnp.pad(flat, (0, ROWS_SMALL * D_MODEL - flat.shape[0])).reshape(ROWS_SMALL, D_MODEL)


def _pack_shard(p, extra=None):
    return jnp.concatenate([
        _rows(p["w_in"], ROWS_IN), _rows(p["w_o"], ROWS_O), _rows(p["w_up"], ROWS_UP), _rows(p["w_down"], ROWS_DOWN),
        _rows(p["conv_w"], ROWS_CONV), _small_rows(p, extra)], axis=0)


def _unpack_shard(pack):
    out = {}
    o = 0
    for name, rows, shape in (("w_in", ROWS_IN, (DEPTH, D_MODEL, IN_COLS // N_CHIPS)),
                              ("w_o", ROWS_O, (DEPTH, D_MODEL // N_CHIPS, D_MODEL)),
                              ("w_up", ROWS_UP, (DEPTH, D_MODEL, 2 * D_FF // N_CHIPS)),
                              ("w_down", ROWS_DOWN, (DEPTH, D_FF // N_CHIPS, D_MODEL)),
                              ("conv_w", ROWS_CONV, (DEPTH, 3, 2 * D_FF // N_CHIPS))):
        n = 1
        for d in shape:
            n *= d
        out[name] = pack[o:o + rows].reshape(-1)[:n].reshape(shape)
        o += rows
    flat = pack[o:].reshape(-1)
    k = 0
    for name in SMALL_NAMES:
        shape = SMALL_SHAPES[name]
        n = 1
        for d in shape:
            n *= d
        out[name] = flat[k:k + n].reshape(shape)
        k += n
    out["extra"] = flat[k]
    return out


def _pack_grads_by_chip(g, loss):
    def cols(a, width):
        d, r, _ = a.shape
        return jnp.transpose(a.reshape(d, r, N_CHIPS, width), (2, 0, 1, 3))

    def rws(a, height):
        d, _, c = a.shape
        return jnp.transpose(a.reshape(d, N_CHIPS, height, c), (1, 0, 2, 3))

    per_chip = dict(w_in=cols(g["w_in"], IN_COLS // N_CHIPS), w_o=rws(g["w_o"], D_MODEL // N_CHIPS),
                    w_up=cols(g["w_up"], 2 * D_FF // N_CHIPS), w_down=rws(g["w_down"], D_FF // N_CHIPS),
                    conv_w=cols(g["conv_w"], 2 * D_FF // N_CHIPS))
    small = _small_rows(g, loss)
    packs = []
    for k in range(N_CHIPS):
        packs.append(jnp.concatenate([
            _rows(per_chip["w_in"][k], ROWS_IN), _rows(per_chip["w_o"][k], ROWS_O), _rows(per_chip["w_up"][k], ROWS_UP),
            _rows(per_chip["w_down"][k], ROWS_DOWN), _rows(per_chip["conv_w"][k], ROWS_CONV), small], axis=0))
    return jnp.stack(packs)


def _gather_weights(w_pack, c):
    conv_bits = lax.bitcast_convert_type(w_pack[ROWS_BIG:ROWS_BIG + ROWS_CONV], jnp.bfloat16)
    conv_bits = conv_bits.reshape(2 * ROWS_CONV, D_MODEL)
    block = jnp.concatenate([w_pack[:ROWS_BIG].astype(jnp.bfloat16), conv_bits], axis=0)
    half = ROWS_GATHER // 2
    mine = lax.dynamic_slice_in_dim(block, c * half, half, axis=0)
    allw = _all_gather(mine).reshape(N_CHIPS, ROWS_GATHER, D_MODEL)
    o = 0

    def take(rows, shape):
        nonlocal o
        a = allw[:, o:o + rows].reshape((N_CHIPS,) + shape)
        o += rows
        return a

    w_in = take(ROWS_IN, (DEPTH, D_MODEL, IN_COLS // N_CHIPS))
    w_o = take(ROWS_O, (DEPTH, D_MODEL // N_CHIPS, D_MODEL))
    w_up = take(ROWS_UP, (DEPTH, D_MODEL, 2 * D_FF // N_CHIPS))
    w_down = take(ROWS_DOWN, (DEPTH, D_FF // N_CHIPS, D_MODEL))
    conv = lax.bitcast_convert_type(allw[:, o:o + 2 * ROWS_CONV].reshape(N_CHIPS, ROWS_CONV, D_MODEL, 2), F32)
    conv = conv.reshape(N_CHIPS, -1)[:, :DEPTH * 3 * (2 * D_FF // N_CHIPS)].reshape(N_CHIPS, DEPTH, 3, 2 * D_FF // N_CHIPS)

    def by_cols(a):
        k, d, r, wd = a.shape
        return jnp.transpose(a, (1, 2, 0, 3)).reshape(d, r, k * wd)

    def by_rows(a):
        k, d, hgt, cc = a.shape
        return jnp.transpose(a, (1, 0, 2, 3)).reshape(d, k * hgt, cc)

    return dict(w_in=by_cols(w_in), w_o=by_rows(w_o), w_up=by_cols(w_up), w_down=by_rows(w_down), conv_w=by_cols(conv))


def _reduce_grads(g_pack, c):
    half = ROWS_PACK // 2
    keep = lax.dynamic_slice_in_dim(g_pack, c * half, half, axis=1)
    give = lax.dynamic_slice_in_dim(g_pack, (1 - c) * half, half, axis=1)
    got = _swap_cores(give, "grad_swap_cores")
    pair = _add2(keep, got, "grad_add_cores")
    total = _sum_chips(_exchange_chips(pair))
    other = _swap_cores(total, "grad_share_cores")
    lo = jnp.where(c == 0, total, other)
    hi = jnp.where(c == 0, other, total)
    return jnp.concatenate([lo, hi], axis=0)


def kernel(x, norm1_g, w_in, sgu_norm_g, sgu_w, sgu_b, pool_w, pool_scale, mix_norm_g, w_o, norm2_g, w_up, conv_w, conv_b, w_down, final_g, loss_target, m_norm1_g, m_w_in, m_sgu_norm_g, m_sgu_w, m_sgu_b, m_pool_w, m_pool_scale, m_mix_norm_g, m_w_o, m_norm2_g, m_w_up, m_conv_w, m_conv_b, m_w_down, m_final_g, v_norm1_g, v_w_in, v_sgu_norm_g, v_sgu_w, v_sgu_b, v_pool_w, v_pool_scale, v_mix_norm_g, v_w_o, v_norm2_g, v_w_up, v_conv_w, v_conv_b, v_w_down, v_final_g):
    names = ("norm1_g", "w_in", "sgu_norm_g", "sgu_w", "sgu_b", "pool_w", "pool_scale", "mix_norm_g", "w_o",
             "norm2_g", "w_up", "conv_w", "conv_b", "w_down", "final_g")
    p = dict(zip(names, (norm1_g, w_in, sgu_norm_g, sgu_w, sgu_b, pool_w, pool_scale, mix_norm_g, w_o, norm2_g,
                         w_up, conv_w, conv_b, w_down, final_g)))
    pm = dict(zip(names, (m_norm1_g, m_w_in, m_sgu_norm_g, m_sgu_w, m_sgu_b, m_pool_w, m_pool_scale, m_mix_norm_g,
                          m_w_o, m_norm2_g, m_w_up, m_conv_w, m_conv_b, m_w_down, m_final_g)))
    pv = dict(zip(names, (v_norm1_g, v_w_in, v_sgu_norm_g, v_sgu_w, v_sgu_b, v_pool_w, v_pool_scale, v_mix_norm_g,
                          v_w_o, v_norm2_g, v_w_up, v_conv_w, v_conv_b, v_w_down, v_final_g)))
    c = lax.axis_index("c")
    zero = jnp.zeros((), F32)
    w_pack = _pack_shard(p, zero)
    full = dict(p)
    full.update(_gather_weights(w_pack, c))

    loss, dx, grads = _local_step(x[0], loss_target[0], full)

    g_pack = _reduce_grads(_pack_grads_by_chip(grads, loss), c)
    d_pack, m_pack, v_pack = _adamw(g_pack, w_pack, _pack_shard(pm, zero), _pack_shard(pv, zero))
    g = _unpack_shard(g_pack)
    d = _unpack_shard(d_pack)
    nm = _unpack_shard(m_pack)
    nv = _unpack_shard(v_pack)
    return (g["extra"], dx[None], *[g[n] for n in names], *[d[n] for n in names], *[nm[n] for n in names],
            *[nv[n] for n in names])
```

```python
import functools

import jax
import jax.numpy as jnp
from jax import lax
from jax.experimental import pallas as pl
from jax.experimental.pallas import tpu as pltpu

F32 = jnp.float32
MXU_DT = jnp.bfloat16

D_MODEL = 1024
DEPTH = 2
HEAD_DIM = 64
W_A = 256
W_B = 512
W_C = 256
IN_COLS = 2 * W_A + 3 * W_B + W_C
CHUNK = 128
POOL_WINDOWS = (2, 4, 8, 16)
D_FF = 2816
EPS = 1e-6
N_CHIPS = 4

ADAM_LR = 0.001
ADAM_B1 = 0.9
ADAM_B2 = 0.999
ADAM_EPS = 1e-08
ADAM_WD = 0.01
ADAM_STEP = 10

LANES = 128
TQ = 512
TK = 256
TM = 256
TM_MM = 512
HALO = 16
VMEM_LIMIT = 56 * 1024 * 1024

ROWS_IN = DEPTH * D_MODEL * (IN_COLS // N_CHIPS) // D_MODEL
ROWS_O = DEPTH * (D_MODEL // N_CHIPS)
ROWS_UP = DEPTH * D_MODEL * (2 * D_FF // N_CHIPS) // D_MODEL
ROWS_DOWN = DEPTH * (D_FF // N_CHIPS)
ROWS_BIG = ROWS_IN + ROWS_O + ROWS_UP + ROWS_DOWN
ROWS_CONV = 16
ROWS_SMALL = 240
ROWS_PACK = ROWS_BIG + ROWS_CONV + ROWS_SMALL
ROWS_GATHER = ROWS_BIG + 2 * ROWS_CONV

SMALL_NAMES = ("norm1_g", "sgu_norm_g", "sgu_w", "sgu_b", "pool_w", "pool_scale",
               "mix_norm_g", "norm2_g", "conv_b", "final_g")
SMALL_SHAPES = {
    "norm1_g": (DEPTH, D_MODEL), "sgu_norm_g": (DEPTH, W_A), "sgu_w": (DEPTH, 4, CHUNK, CHUNK),
    "sgu_b": (DEPTH, 4, CHUNK), "pool_w": (DEPTH, 4, 64, 64), "pool_scale": (DEPTH, W_C),
    "mix_norm_g": (DEPTH, D_MODEL), "norm2_g": (DEPTH, D_MODEL), "conv_b": (DEPTH, 2 * D_FF),
    "final_g": (D_MODEL,),
}


def _call(body, **kw):
    return pl.pallas_call(body, **kw)


def _params(*sem):
    return pltpu.CompilerParams(dimension_semantics=sem, vmem_limit_bytes=VMEM_LIMIT)


def _dot(a, b):
    return jnp.dot(a, b, preferred_element_type=F32)


def _dot_nt(a, b):
    return lax.dot_general(a, b, (((1,), (1,)), ((), ())), preferred_element_type=F32)


def _dot_tn(a, b):
    return lax.dot_general(a, b, (((0,), (0,)), ((), ())), preferred_element_type=F32)


def _split(a):
    hi = a.astype(MXU_DT)
    lo = (a - hi.astype(F32)).astype(MXU_DT)
    return hi, lo


def _dot_split(a, b):
    hi, lo = _split(a)
    return _dot(hi, b) + _dot(lo, b)


def _group_mean(sq, gmat):
    cols = [_dot_split(sq[:, b * LANES:(b + 1) * LANES], gmat) for b in range(sq.shape[1] // LANES)]
    return cols[0] if len(cols) == 1 else jnp.concatenate(cols, axis=-1)


def _group_matrix():
    r = jnp.arange(LANES)
    return jnp.where((r[:, None] // HEAD_DIM) == (r[None, :] // HEAD_DIM), 1.0 / HEAD_DIM, 0.0).astype(MXU_DT)


def _tile(n):
    return max(t for t in range(LANES, 1536 + 1, LANES) if n % t == 0)


def _row_spec(tm, cols, col_block=0):
    return pl.BlockSpec((tm, cols), lambda i, cb=col_block: (i, cb))


def _full_spec(shape):
    nd = len(shape)
    return pl.BlockSpec(shape, lambda *_: (0,) * nd)


def _rms_mm(x, g, w, tn, name):
    S, D = x.shape
    N = w.shape[1]
    tm = TM_MM

    def body(x_ref, g_ref, w_ref, o_ref, ob_ref, h_ref):
        @pl.when(pl.program_id(1) == 0)
        def _():
            xv = x_ref[...]
            r = lax.rsqrt(jnp.mean(xv * xv, axis=-1, keepdims=True) + EPS)
            h_ref[...] = (xv * r * g_ref[...]).astype(h_ref.dtype)

        acc = _dot(h_ref[...], w_ref[...])
        o_ref[...] = acc
        ob_ref[...] = acc.astype(ob_ref.dtype)

    return _call(
        body, name=name, grid=(S // tm, N // tn),
        in_specs=[pl.BlockSpec((tm, D), lambda i, j: (i, 0)),
                  pl.BlockSpec((1, D), lambda i, j: (0, 0)),
                  pl.BlockSpec((D, tn), lambda i, j: (0, j))],
        out_specs=[pl.BlockSpec((tm, tn), lambda i, j: (i, j)),
                   pl.BlockSpec((tm, tn), lambda i, j: (i, j)),
                   pl.BlockSpec((tm, D), lambda i, j: (i, 0))],
        out_shape=[jax.ShapeDtypeStruct((S, N), F32), jax.ShapeDtypeStruct((S, N), MXU_DT),
                   jax.ShapeDtypeStruct((S, D), MXU_DT)],
        compiler_params=_params("parallel", "arbitrary"),
    )(x, g, w)


def _mm_res(a, w, res, name):
    S, K = a.shape
    N = w.shape[1]
    tm = TM_MM

    def body(a_ref, w_ref, r_ref, o_ref):
        o_ref[...] = r_ref[...] + _dot(a_ref[...], w_ref[...])

    return _call(
        body, name=name, grid=(S // tm,),
        in_specs=[_row_spec(tm, K), _full_spec((K, N)), _row_spec(tm, N)],
        out_specs=_row_spec(tm, N),
        out_shape=jax.ShapeDtypeStruct((S, N), F32),
        compiler_params=_params("parallel"),
    )(a, w, res)


def _mm_nt(a, w, name):
    S, K = a.shape
    N = w.shape[0]
    tm = TM_MM

    def body(a_ref, w_ref, o_ref):
        o_ref[...] = _dot_nt(a_ref[...].astype(MXU_DT), w_ref[...])

    return _call(
        body, name=name, grid=(S // tm,),
        in_specs=[_row_spec(tm, K), _full_spec((N, K))],
        out_specs=_row_spec(tm, N),
        out_shape=jax.ShapeDtypeStruct((S, N), F32),
        compiler_params=_params("parallel"),
    )(a, w)


def _mm_tn(a, b, name):
    S, K1 = a.shape
    N = b.shape[1]
    ts = TM_MM
    tk = _tile(K1)
    tn = _tile(N)

    def body(a_ref, b_ref, o_ref):
        @pl.when(pl.program_id(2) == 0)
        def _():
            o_ref[...] = jnp.zeros_like(o_ref)

        o_ref[...] += _dot_tn(a_ref[...], b_ref[...].astype(MXU_DT))

    return _call(
        body, name=name, grid=(K1 // tk, N // tn, S // ts),
        in_specs=[pl.BlockSpec((ts, tk), lambda m, n, s: (s, m)),
                  pl.BlockSpec((ts, tn), lambda m, n, s: (s, n))],
        out_specs=pl.BlockSpec((tk, tn), lambda m, n, s: (m, n)),
        out_shape=jax.ShapeDtypeStruct((K1, N), F32),
        compiler_params=_params("parallel", "parallel", "arbitrary"),
    )(a, b)


def _mm_nt_rmsbwd(pairs, x, g, dres, name):
    S, D = x.shape
    tm = TM
    n = len(pairs)

    def body(*refs):
        a_refs = refs[:n]
        w_refs = refs[n:2 * n]
        x_ref, g_ref, r_ref, dx_ref, dg_ref = refs[2 * n:]
        dh = _dot_nt(a_refs[0][...], w_refs[0][...])
        for k in range(1, n):
            dh += _dot_nt(a_refs[k][...], w_refs[k][...])
        xv = x_ref[...]
        r = lax.rsqrt(jnp.mean(xv * xv, axis=-1, keepdims=True) + EPS)
        xhat = xv * r

        @pl.when(pl.program_id(0) == 0)
        def _():
            dg_ref[...] = jnp.zeros_like(dg_ref)

        dg_ref[...] += jnp.sum(dh * xhat, axis=0, keepdims=True)
        dxh = dh * g_ref[...]
        dx_ref[...] = r_ref[...] + r * (dxh - xhat * jnp.mean(dxh * xhat, axis=-1, keepdims=True))

    in_specs = ([_row_spec(tm, a.shape[1]) for a, _ in pairs] + [_full_spec(w.shape) for _, w in pairs]
                + [_row_spec(tm, D), _full_spec((1, D)), _row_spec(tm, D)])
    return _call(
        body, name=name, grid=(S // tm,), in_specs=in_specs,
        out_specs=[_row_spec(tm, D), _full_spec((1, D))],
        out_shape=[jax.ShapeDtypeStruct((S, D), F32), jax.ShapeDtypeStruct((1, D), F32)],
        compiler_params=_params("arbitrary"),
    )(*[a for a, _ in pairs], *[w for _, w in pairs], x, g, dres)


def _loss_head(x, g, tgt):
    S, D = x.shape
    tm = TM

    def body(x_ref, g_ref, t_ref, dx_ref, dg_ref, l_ref):
        xv = x_ref[...]
        r = lax.rsqrt(jnp.mean(xv * xv, axis=-1, keepdims=True) + EPS)
        xhat = xv * r
        diff = xhat * g_ref[...] - t_ref[...]

        @pl.when(pl.program_id(0) == 0)
        def _():
            dg_ref[...] = jnp.zeros_like(dg_ref)
            l_ref[...] = jnp.zeros_like(l_ref)

        l_ref[...] += jnp.full(l_ref.shape, 0.5 * jnp.sum(jnp.mean(diff * diff, axis=-1, keepdims=True)), F32)
        dout = diff * (1.0 / D)
        dg_ref[...] += jnp.sum(dout * xhat, axis=0, keepdims=True)
        dxh = dout * g_ref[...]
        dx_ref[...] = r * (dxh - xhat * jnp.mean(dxh * xhat, axis=-1, keepdims=True))

    return _call(
        body, name="loss_head", grid=(S // tm,),
        in_specs=[_row_spec(tm, D), _full_spec((1, D)), _row_spec(tm, D)],
        out_specs=[_row_spec(tm, D), _full_spec((1, D)), _full_spec((8, LANES))],
        out_shape=[jax.ShapeDtypeStruct((S, D), F32), jax.ShapeDtypeStruct((1, D), F32),
                   jax.ShapeDtypeStruct((8, LANES), F32)],
        compiler_params=_params("arbitrary"),
    )(x, g, tgt)


def _mix_out(ya, yb, yc, gm, wo, x, gmat):
    S = x.shape[0]
    tm = TM

    def body(ya_ref, yb_ref, yc_ref, gm_ref, wo_ref, x_ref, gmat_ref, x2_ref, yn_ref):
        y = jnp.concatenate([ya_ref[...], yb_ref[...], yc_ref[...]], axis=-1)
        r = lax.rsqrt(_group_mean(y * y, gmat_ref[...]) + EPS)
        yn = (y * r * gm_ref[...]).astype(MXU_DT)
        yn_ref[...] = yn
        x2_ref[...] = x_ref[...] + _dot(yn, wo_ref[...])

    return _call(
        body, name="mix_out", grid=(S // tm,),
        in_specs=[_row_spec(tm, W_A), _row_spec(tm, W_B), _row_spec(tm, W_C), _full_spec((1, D_MODEL)),
                  _full_spec((D_MODEL, D_MODEL)), _row_spec(tm, D_MODEL), _full_spec((LANES, LANES))],
        out_specs=[_row_spec(tm, D_MODEL), _row_spec(tm, D_MODEL)],
        out_shape=[jax.ShapeDtypeStruct((S, D_MODEL), F32), jax.ShapeDtypeStruct((S, D_MODEL), MXU_DT)],
        compiler_params=_params("parallel"),
    )(ya, yb, yc, gm, wo, x, gmat)


def _mix_out_bwd(dx2, wo, ya, yb, yc, gm, gmat):
    S = dx2.shape[0]
    tm = TM

    def body(dx2_ref, wo_ref, ya_ref, yb_ref, yc_ref, gm_ref, gmat_ref, dya_ref, dyb_ref, dyc_ref, dgm_ref):
        dyn = _dot_nt(dx2_ref[...].astype(MXU_DT), wo_ref[...])
        y = jnp.concatenate([ya_ref[...], yb_ref[...], yc_ref[...]], axis=-1)
        r = lax.rsqrt(_group_mean(y * y, gmat_ref[...]) + EPS)
        yhat = y * r

        @pl.when(pl.program_id(0) == 0)
        def _():
            dgm_ref[...] = jnp.zeros_like(dgm_ref)

        dgm_ref[...] += jnp.sum(dyn * yhat, axis=0, keepdims=True)
        dyh = dyn * gm_ref[...]
        dy = r * (dyh - yhat * _group_mean(dyh * yhat, gmat_ref[...]))
        dya_ref[...] = dy[:, :W_A]
        dyb_ref[...] = dy[:, W_A:W_A + W_B]
        dyc_ref[...] = dy[:, W_A + W_B:]

    return _call(
        body, name="mix_out_bwd", grid=(S // tm,),
        in_specs=[_row_spec(tm, D_MODEL), _full_spec((D_MODEL, D_MODEL)), _row_spec(tm, W_A), _row_spec(tm, W_B),
                  _row_spec(tm, W_C), _full_spec((1, D_MODEL)), _full_spec((LANES, LANES))],
        out_specs=[_row_spec(tm, W_A), _row_spec(tm, W_B), _row_spec(tm, W_C), _full_spec((1, D_MODEL))],
        out_shape=[jax.ShapeDtypeStruct((S, W_A), F32), jax.ShapeDtypeStruct((S, W_B), F32),
                   jax.ShapeDtypeStruct((S, W_C), F32), jax.ShapeDtypeStruct((1, D_MODEL), F32)],
        compiler_params=_params("arbitrary"),
    )(dx2, wo, ya, yb, yc, gm, gmat)


_SQRT_HALF = 0.7071067811865476
_INV_SQRT_2PI = 0.3989422804014327


def _sgu_common(a, sng, wm_ref, bias, gmat):
    phi = 0.5 * (1.0 + lax.erf(a * _SQRT_HALF))
    ga = a * phi
    u = ga[:, :W_A]
    v = ga[:, W_A:]
    r = lax.rsqrt(_group_mean(v * v, gmat) + EPS)
    vhat = v * r
    vn = (vhat * sng).astype(MXU_DT)
    head = lax.broadcasted_iota(jnp.int32, (CHUNK, W_A), 1) // HEAD_DIM
    rows = []
    for c in range(a.shape[0] // CHUNK):
        vc = vn[c * CHUNK:(c + 1) * CHUNK]
        s = bias
        for h in range(4):
            s = s + jnp.where(head == h, _dot(wm_ref[h], vc), 0.0)
        rows.append(s)
    s = jnp.concatenate(rows, axis=0)
    return phi, u, r, vhat, vn, s


def _tril_weights(sgu_w_l):
    t = jnp.arange(CHUNK)
    return jnp.where((t[None, :] <= t[:, None])[None], sgu_w_l, 0.0)


def _sgu_fwd(proj, sng, wm, bias, gmat):
    S = proj.shape[0]
    tm = TM

    def body(a_ref, sng_ref, wm_ref, b_ref, gmat_ref, y_ref):
        _, u, _, _, _, s = _sgu_common(a_ref[...], sng_ref[...], wm_ref, b_ref[...], gmat_ref[...])
        y_ref[...] = u * s

    return _call(
        body, name="sgu_fwd", grid=(S // tm,),
        in_specs=[_row_spec(tm, 2 * W_A), _full_spec((1, W_A)), _full_spec((4, CHUNK, CHUNK)),
                  _full_spec((CHUNK, W_A)), _full_spec((LANES, LANES))],
        out_specs=_row_spec(tm, W_A),
        out_shape=jax.ShapeDtypeStruct((S, W_A), F32),
        compiler_params=_params("parallel"),
    )(proj, sng, wm, bias, gmat)


def _sgu_bwd(proj, dy, sng, wm, wmt, bias, gmat):
    S = proj.shape[0]
    tm = TM

    def body(a_ref, dy_ref, sng_ref, wm_ref, wmt_ref, b_ref, gmat_ref, da_ref, dw_ref, db_ref, dsng_ref):
        a = a_ref[...]
        dy = dy_ref[...]
        gmat = gmat_ref[...]
        sng = sng_ref[...]
        phi, u, r, vhat, vn, s = _sgu_common(a, sng, wm_ref, b_ref[...], gmat)
        du = dy * s
        ds = dy * u

        @pl.when(pl.program_id(0) == 0)
        def _():
            dw_ref[...] = jnp.zeros_like(dw_ref)
            db_ref[...] = jnp.zeros_like(db_ref)
            dsng_ref[...] = jnp.zeros_like(dsng_ref)

        head = lax.broadcasted_iota(jnp.int32, (CHUNK, W_A), 1) // HEAD_DIM
        tt = lax.broadcasted_iota(jnp.int32, (CHUNK, CHUNK), 0)
        ss = lax.broadcasted_iota(jnp.int32, (CHUNK, CHUNK), 1)
        rows = []
        for c in range(tm // CHUNK):
            dsc = ds[c * CHUNK:(c + 1) * CHUNK]
            vc = vn[c * CHUNK:(c + 1) * CHUNK]
            db_ref[...] += dsc
            dsb = dsc.astype(MXU_DT)
            dvn = jnp.zeros((CHUNK, W_A), F32)
            for h in range(4):
                dvn = dvn + jnp.where(head == h, _dot(wmt_ref[h], dsb), 0.0)
                dsh = jnp.where(head == h, dsc, 0.0).astype(MXU_DT)
                dw_ref[h] += jnp.where(ss <= tt, _dot_nt(dsh, vc), 0.0)
            rows.append(dvn)
        dvn = jnp.concatenate(rows, axis=0)
        dsng_ref[...] += jnp.sum(dvn * vhat, axis=0, keepdims=True)
        dvh = dvn * sng
        dv = r * (dvh - vhat * _group_mean(dvh * vhat, gmat))
        dga = jnp.concatenate([du, dv], axis=-1)
        dgelu = phi + a * (_INV_SQRT_2PI * jnp.exp(-0.5 * a * a))
        da_ref[...] = (dga * dgelu).astype(da_ref.dtype)

    return _call(
        body, name="sgu_bwd", grid=(S // tm,),
        in_specs=[_row_spec(tm, 2 * W_A), _row_spec(tm, W_A), _full_spec((1, W_A)), _full_spec((4, CHUNK, CHUNK)),
                  _full_spec((4, CHUNK, CHUNK)), _full_spec((CHUNK, W_A)), _full_spec((LANES, LANES))],
        out_specs=[_row_spec(tm, 2 * W_A), _full_spec((4, CHUNK, CHUNK)), _full_spec((CHUNK, W_A)),
                   _full_spec((1, W_A))],
        out_shape=[jax.ShapeDtypeStruct((S, 2 * W_A), MXU_DT), jax.ShapeDtypeStruct((4, CHUNK, CHUNK), F32),
                   jax.ShapeDtypeStruct((CHUNK, W_A), F32), jax.ShapeDtypeStruct((1, W_A), F32)],
        compiler_params=_params("arbitrary"),
    )(proj, dy, sng, wm, wmt, bias, gmat)


Q_BLK0 = (2 * W_A) // LANES
K_BLK0 = Q_BLK0 + W_B // LANES
V_BLK0 = K_BLK0 + W_B // LANES
N_PAIRS = W_B // LANES


def _tri_matrix():
    r = jnp.arange(TK)
    return (r[:, None] > r[None, :]).astype(MXU_DT)


def _stack_heads(a):
    lane = lax.broadcasted_iota(jnp.int32, a.shape, 1)
    return jnp.concatenate([jnp.where(lane < HEAD_DIM, a, 0.0), jnp.where(lane >= HEAD_DIM, a, 0.0)],
                           axis=0).astype(MXU_DT)


def _unstack_heads(a):
    lane = lax.broadcasted_iota(jnp.int32, (TQ, LANES), 1)
    return jnp.where(lane < HEAD_DIM, a[:TQ], a[TQ:])


def _sb_scores(q2, kj, tri, key_offset):
    z = _dot_nt(q2, kj)
    sp = jnp.log(1.0 + jnp.exp(-jnp.abs(z)))
    lsp = jnp.minimum(z, 0.0) - sp
    lsm = lsp - z
    msk = None
    if key_offset is not None:
        row = lax.broadcasted_iota(jnp.int32, z.shape, 0) & (TQ - 1)
        col = lax.broadcasted_iota(jnp.int32, z.shape, 1) + key_offset
        msk = col < row
        lsm = jnp.where(msk, lsm, 0.0)
    tail = _dot_split(lsm, tri)
    return lsp, lsm, tail, msk


def _sb_fwd(proj_b, tri):
    S = proj_b.shape[0]
    nq = S // TQ
    kpq = TQ // TK
    assert S // TK <= LANES

    def body(q_ref, k_ref, v_ref, tri_ref, o_ref, rb_ref, acc_ref):
        i = pl.program_id(1)
        lane2 = lax.broadcasted_iota(jnp.int32, (2 * TQ, LANES), 1)
        q2 = _stack_heads(q_ref[...].astype(F32) * (HEAD_DIM ** -0.5))
        tri = tri_ref[...]
        rb_ref[...] = jnp.zeros_like(rb_ref)

        def block(j, run, key_offset=None, first=False):
            start = pl.multiple_of(j * TK, TK)
            kj = k_ref[pl.ds(start, TK), :]
            vj = v_ref[pl.ds(start, TK), :]
            lsp, lsm, tail, msk = _sb_scores(q2, kj, tri, key_offset)
            rb_ref[...] = jnp.where(lane2 == j, run, rb_ref[...])
            att = jnp.exp(lsp + tail + run)
            if msk is not None:
                att = jnp.where(msk, att, 0.0)
            pv = _dot(att.astype(MXU_DT), vj)
            if first:
                acc_ref[...] = pv
            else:
                acc_ref[...] += pv
            return run + tail[:, :1] + lsm[:, :1]

        run = jnp.zeros((2 * TQ, 1), F32)
        for d in reversed(range(kpq)):
            run = block(i * kpq + d, run, key_offset=d * TK, first=(d == kpq - 1))
        lax.fori_loop(0, i * kpq, lambda n, run: block(i * kpq - 1 - n, run), run)
        o_ref[...] = _unstack_heads(acc_ref[...])

    return _call(
        body, name="sb_fwd", grid=(N_PAIRS, nq),
        in_specs=[pl.BlockSpec((TQ, LANES), lambda p, i: (i, Q_BLK0 + p)),
                  pl.BlockSpec((S, LANES), lambda p, i: (0, K_BLK0 + p)),
                  pl.BlockSpec((S, LANES), lambda p, i: (0, V_BLK0 + p)),
                  pl.BlockSpec((TK, TK), lambda p, i: (0, 0))],
        out_specs=[pl.BlockSpec((TQ, LANES), lambda p, i: (i, p)),
                   pl.BlockSpec((None, None, 2 * TQ, LANES), lambda p, i: (p, i, 0, 0))],
        out_shape=[jax.ShapeDtypeStruct((S, W_B), F32), jax.ShapeDtypeStruct((N_PAIRS, nq, 2 * TQ, LANES), F32)],
        scratch_shapes=[pltpu.VMEM((2 * TQ, LANES), F32)],
        compiler_params=_params("parallel", "arbitrary"),
    )(proj_b, proj_b, proj_b, tri)


def _sb_bwd(proj_b, dyb, rb, tri, trit):
    S = proj_b.shape[0]
    nq = S // TQ

    kpq = TQ // TK

    def body(q_ref, k_ref, v_ref, do_ref, rb_ref, tri_ref, trit_ref, dq_ref, dk_ref, dv_ref,
             dq_acc, dk_acc, dv_acc):
        i = pl.program_id(1)
        lane2 = lax.broadcasted_iota(jnp.int32, (2 * TQ, LANES), 1)
        scale = HEAD_DIM ** -0.5
        q2 = _stack_heads(q_ref[...].astype(F32) * scale)
        do2 = _stack_heads(do_ref[...])
        tri = tri_ref[...]
        trit = trit_ref[...]

        @pl.when(i == 0)
        def _():
            dk_acc[...] = jnp.zeros_like(dk_acc)
            dv_acc[...] = jnp.zeros_like(dv_acc)

        dq_acc[...] = jnp.zeros_like(dq_acc)

        def block(j, pre, key_offset=None):
            start = pl.multiple_of(j * TK, TK)
            kj = k_ref[pl.ds(start, TK), :]
            vj = v_ref[pl.ds(start, TK), :]
            lsp, lsm, tail, msk = _sb_scores(q2, kj, tri, key_offset)
            run = jnp.sum(jnp.where(lane2 == j, rb_ref[...], 0.0), axis=-1, keepdims=True)
            att = jnp.exp(lsp + tail + run)
            if msk is not None:
                att = jnp.where(msk, att, 0.0)
            beta = jnp.exp(lsp)
            dl = _dot_nt(do2, vj) * att
            cin = _dot_split(dl, trit)
            dz = dl * (1.0 - beta) - beta * (pre + cin)
            if msk is not None:
                dz = jnp.where(msk, dz, 0.0)
            dzb = dz.astype(MXU_DT)
            dq_acc[...] += _dot(dzb, kj)
            dk_acc[pl.ds(start, TK), :] += _dot_tn(dzb, q2)
            dv_acc[pl.ds(start, TK), :] += _dot_tn(att.astype(MXU_DT), do2)
            return pre + cin[:, TK - 1:] + dl[:, TK - 1:]

        pre = lax.fori_loop(0, i * kpq, lambda j, pre: block(j, pre), jnp.zeros((2 * TQ, 1), F32))
        for d in range(kpq):
            pre = block(i * kpq + d, pre, key_offset=d * TK)
        dq_ref[...] = (_unstack_heads(dq_acc[...]) * scale).astype(dq_ref.dtype)

        @pl.when(i == nq - 1)
        def _():
            dk_ref[...] = dk_acc[...].astype(dk_ref.dtype)
            dv_ref[...] = dv_acc[...].astype(dv_ref.dtype)

    return _call(
        body, name="sb_bwd", grid=(N_PAIRS, nq),
        in_specs=[pl.BlockSpec((TQ, LANES), lambda p, i: (i, Q_BLK0 + p)),
                  pl.BlockSpec((S, LANES), lambda p, i: (0, K_BLK0 + p)),
                  pl.BlockSpec((S, LANES), lambda p, i: (0, V_BLK0 + p)),
                  pl.BlockSpec((TQ, LANES), lambda p, i: (i, p)),
                  pl.BlockSpec((None, None, 2 * TQ, LANES), lambda p, i: (p, i, 0, 0)),
                  pl.BlockSpec((TK, TK), lambda p, i: (0, 0)),
                  pl.BlockSpec((TK, TK), lambda p, i: (0, 0))],
        out_specs=[pl.BlockSpec((TQ, LANES), lambda p, i: (i, p)),
                   pl.BlockSpec((S, LANES), lambda p, i: (0, p)),
                   pl.BlockSpec((S, LANES), lambda p, i: (0, p))],
        out_shape=[jax.ShapeDtypeStruct((S, W_B), MXU_DT)] * 3,
        scratch_shapes=[pltpu.VMEM((2 * TQ, LANES), F32), pltpu.VMEM((S, LANES), F32), pltpu.VMEM((S, LANES), F32)],
        compiler_params=_params("parallel", "arbitrary"),
    )(proj_b, proj_b, proj_b, dyb, rb, tri, trit)


P_BLK = (2 * W_A + 3 * W_B) // W_C


def _window_lanes():
    g = lax.broadcasted_iota(jnp.int32, (1, W_C), 1) // (W_C // 4)
    w = jnp.where(g == 0, POOL_WINDOWS[0], jnp.where(g == 1, POOL_WINDOWS[1],
                  jnp.where(g == 2, POOL_WINDOWS[2], POOL_WINDOWS[3])))
    return g, w


def _shift_rows(ext, k, tm, lead):
    n = ext.shape[0]
    return pltpu.roll(ext, shift=k % n, axis=0)[lead:lead + tm]


def _pool_diff(p_cur, p_halo, row0, tm):
    ext = jnp.concatenate([p_halo, p_cur], axis=0)
    g, w = _window_lanes()
    acc = ext
    sums = []
    for sh in (1, 2, 4, 8):
        acc = acc + pltpu.roll(acc, shift=sh, axis=0)
        sums.append(acc[HALO:HALO + tm])
    wsum = jnp.where(g == 0, sums[0], jnp.where(g == 1, sums[1], jnp.where(g == 2, sums[2], sums[3])))
    pos = (row0 + 1 + lax.broadcasted_iota(jnp.int32, (tm, W_C), 0)).astype(F32)
    cnt = jnp.minimum(pos, w.astype(F32))
    return wsum / cnt - p_cur, cnt


def _pool_specs(tm, nrow_blocks_halo):
    cur = pl.BlockSpec((tm, W_C), lambda i: (i, P_BLK))
    prev = pl.BlockSpec((HALO, W_C), lambda i: (jnp.maximum(i * (tm // HALO) - 1, 0), P_BLK))
    return cur, prev


def _pool_fwd(proj, wbd, scale):
    S = proj.shape[0]
    tm = TM

    def body(p_ref, ph_ref, w_ref, sc_ref, y_ref):
        i = pl.program_id(0)
        halo = jnp.where(i > 0, ph_ref[...], 0.0)
        d, _ = _pool_diff(p_ref[...], halo, i * tm, tm)
        y_ref[...] = _dot(d.astype(MXU_DT), w_ref[...]) * sc_ref[...]

    cur, prev = _pool_specs(tm, S // HALO)
    return _call(
        body, name="pool_fwd", grid=(S // tm,),
        in_specs=[cur, prev, _full_spec((W_C, W_C)), _full_spec((1, W_C))],
        out_specs=_row_spec(tm, W_C),
        out_shape=jax.ShapeDtypeStruct((S, W_C), F32),
        compiler_params=_params("parallel"),
    )(proj, proj, wbd, scale)


def _pool_bwd_a(proj, dy, wbd, scale):
    S = proj.shape[0]
    tm = TM

    def body(p_ref, ph_ref, dy_ref, w_ref, sc_ref, dd_ref, e_ref, dw_ref, dsc_ref):
        i = pl.program_id(0)
        halo = jnp.where(i > 0, ph_ref[...], 0.0)
        d, cnt = _pool_diff(p_ref[...], halo, i * tm, tm)
        db = d.astype(MXU_DT)
        dy = dy_ref[...]

        @pl.when(i == 0)
        def _():
            dw_ref[...] = jnp.zeros_like(dw_ref)
            dsc_ref[...] = jnp.zeros_like(dsc_ref)

        dsc_ref[...] += jnp.sum(dy * _dot(db, w_ref[...]), axis=0, keepdims=True)
        dys = (dy * sc_ref[...]).astype(MXU_DT)
        dw_ref[...] += _dot_tn(db, dys)
        dd = _dot_nt(dys, w_ref[...])
        dd_ref[...] = dd
        e_ref[...] = dd / cnt

    cur, prev = _pool_specs(tm, S // HALO)
    return _call(
        body, name="pool_bwd_a", grid=(S // tm,),
        in_specs=[cur, prev, _row_spec(tm, W_C), _full_spec((W_C, W_C)), _full_spec((1, W_C))],
        out_specs=[_row_spec(tm, W_C), _row_spec(tm, W_C), _full_spec((W_C, W_C)), _full_spec((1, W_C))],
        out_shape=[jax.ShapeDtypeStruct((S, W_C), F32), jax.ShapeDtypeStruct((S, W_C), F32),
                   jax.ShapeDtypeStruct((W_C, W_C), F32), jax.ShapeDtypeStruct((1, W_C), F32)],
        compiler_params=_params("arbitrary"),
    )(proj, proj, dy, wbd, scale)


def _pool_bwd_b(dd, e):
    S = dd.shape[0]
    tm = TM
    nb = S // tm

    def body(dd_ref, e_ref, en_ref, dp_ref):
        i = pl.program_id(0)
        halo = jnp.where(i < nb - 1, en_ref[...], 0.0)
        ext = jnp.concatenate([e_ref[...], halo], axis=0)
        n = ext.shape[0]
        g, _ = _window_lanes()
        acc = ext
        sums = []
        for sh in (1, 2, 4, 8):
            acc = acc + pltpu.roll(acc, shift=n - sh, axis=0)
            sums.append(acc[:tm])
        wsum = jnp.where(g == 0, sums[0], jnp.where(g == 1, sums[1], jnp.where(g == 2, sums[2], sums[3])))
        dp_ref[...] = (wsum - dd_ref[...]).astype(dp_ref.dtype)

    nxt = pl.BlockSpec((HALO, W_C), lambda i: (jnp.minimum((i + 1) * (tm // HALO), S // HALO - 1), 0))
    return _call(
        body, name="pool_bwd_b", grid=(nb,),
        in_specs=[_row_spec(tm, W_C), _row_spec(tm, W_C), nxt],
        out_specs=_row_spec(tm, W_C),
        out_shape=jax.ShapeDtypeStruct((S, W_C), MXU_DT),
        compiler_params=_params("parallel"),
    )(dd, e, e)


TN_FF = 1408
NB_FF = D_FF // TN_FF
CONV_ROWS = 8


def _conv(z_cur, z_halo, cwb, tm):
    ext = jnp.concatenate([z_halo, z_cur], axis=0)
    z2 = _shift_rows(ext, 2, tm, HALO)
    z1 = _shift_rows(ext, 1, tm, HALO)
    zc = cwb[3:4] + z2 * cwb[0:1] + z1 * cwb[1:2] + z_cur * cwb[2:3]
    return zc, z2, z1


def _ffn_specs(tm, order):
    def mk(f):
        return (lambda i, j: f(i, j)) if order == "ij" else (lambda j, i: f(i, j))
    hb = tm // HALO
    return [
        pl.BlockSpec((tm, TN_FF), mk(lambda i, j: (i, j))),
        pl.BlockSpec((tm, TN_FF), mk(lambda i, j: (i, j + NB_FF))),
        pl.BlockSpec((HALO, TN_FF), mk(lambda i, j: (jnp.maximum(i * hb - 1, 0), j))),
        pl.BlockSpec((HALO, TN_FF), mk(lambda i, j: (jnp.maximum(i * hb - 1, 0), j + NB_FF))),
        pl.BlockSpec((CONV_ROWS, TN_FF), mk(lambda i, j: (0, j))),
        pl.BlockSpec((CONV_ROWS, TN_FF), mk(lambda i, j: (0, j + NB_FF))),
    ]


def _conv_gate(z, cwb):
    S = z.shape[0]
    tm = TM

    def body(zg_ref, zu_ref, hg_ref, hu_ref, cg_ref, cu_ref, f_ref):
        first = pl.program_id(0) == 0
        g, _, _ = _conv(zg_ref[...], jnp.where(first, 0.0, hg_ref[...]), cg_ref[...], tm)
        u, _, _ = _conv(zu_ref[...], jnp.where(first, 0.0, hu_ref[...]), cu_ref[...], tm)
        f_ref[...] = (g * jax.nn.sigmoid(g) * u).astype(f_ref.dtype)

    return _call(
        body, name="conv_gate", grid=(S // tm, NB_FF), in_specs=_ffn_specs(tm, "ij"),
        out_specs=pl.BlockSpec((tm, TN_FF), lambda i, j: (i, j)),
        out_shape=jax.ShapeDtypeStruct((S, D_FF), MXU_DT),
        compiler_params=_params("parallel", "parallel"),
    )(z, z, z, z, cwb, cwb)


def _conv_gate_bwd(z, df, cwb):
    S = z.shape[0]
    tm = TM

    def body(zg_ref, zu_ref, hg_ref, hu_ref, cg_ref, cu_ref, df_ref, dg_ref, du_ref, dcg_ref, dcu_ref):
        i = pl.program_id(1)
        first = i == 0
        zg = zg_ref[...]
        zu = zu_ref[...]
        g, g2, g1 = _conv(zg, jnp.where(first, 0.0, hg_ref[...]), cg_ref[...], tm)
        u, u2, u1 = _conv(zu, jnp.where(first, 0.0, hu_ref[...]), cu_ref[...], tm)
        sg = jax.nn.sigmoid(g)
        df = df_ref[...]
        dgv = df * u * (sg * (1.0 + g * (1.0 - sg)))
        duv = df * (g * sg)
        dg_ref[...] = dgv
        du_ref[...] = duv

        @pl.when(first)
        def _():
            dcg_ref[...] = jnp.zeros_like(dcg_ref)
            dcu_ref[...] = jnp.zeros_like(dcu_ref)

        rid = lax.broadcasted_iota(jnp.int32, (CONV_ROWS, TN_FF), 0)

        def taps(dv, s2, s1, s0):
            sums = [jnp.sum(dv * s2, axis=0, keepdims=True), jnp.sum(dv * s1, axis=0, keepdims=True),
                    jnp.sum(dv * s0, axis=0, keepdims=True), jnp.sum(dv, axis=0, keepdims=True)]
            out = jnp.zeros((CONV_ROWS, TN_FF), F32)
            for k, v in enumerate(sums):
                out = jnp.where(rid == k, v, out)
            return out

        dcg_ref[...] += taps(dgv, g2, g1, zg)
        dcu_ref[...] += taps(duv, u2, u1, zu)

    acc = pl.BlockSpec((CONV_ROWS, TN_FF), lambda j, i: (0, j))
    tile = pl.BlockSpec((tm, TN_FF), lambda j, i: (i, j))
    return _call(
        body, name="conv_gate_bwd", grid=(NB_FF, S // tm), in_specs=_ffn_specs(tm, "ji") + [tile],
        out_specs=[tile, tile, acc, acc],
        out_shape=[jax.ShapeDtypeStruct((S, D_FF), F32), jax.ShapeDtypeStruct((S, D_FF), F32),
                   jax.ShapeDtypeStruct((CONV_ROWS, D_FF), F32), jax.ShapeDtypeStruct((CONV_ROWS, D_FF), F32)],
        compiler_params=_params("parallel", "arbitrary"),
    )(z, z, z, z, cwb, cwb, df)


def _conv_transpose(dzc, cwb_half, name):
    S = dzc.shape[0]
    tm = TM
    nb = S // tm

    def body(d_ref, dn_ref, c_ref, o_ref):
        last = pl.program_id(0) == nb - 1
        cur = d_ref[...]
        ext = jnp.concatenate([cur, jnp.where(last, 0.0, dn_ref[...])], axis=0)
        c = c_ref[...]
        o_ref[...] = (cur * c[2:3] + _shift_rows(ext, -1, tm, 0) * c[1:2]
                      + _shift_rows(ext, -2, tm, 0) * c[0:1]).astype(o_ref.dtype)

    hb = tm // HALO
    return _call(
        body, name=name, grid=(nb, NB_FF),
        in_specs=[pl.BlockSpec((tm, TN_FF), lambda i, j: (i, j)),
                  pl.BlockSpec((HALO, TN_FF), lambda i, j: (jnp.minimum((i + 1) * hb, S // HALO - 1), j)),
                  pl.BlockSpec((CONV_ROWS, TN_FF), lambda i, j: (0, j))],
        out_specs=pl.BlockSpec((tm, TN_FF), lambda i, j: (i, j)),
        out_shape=jax.ShapeDtypeStruct((S, D_FF), MXU_DT),
        compiler_params=_params("parallel", "parallel"),
    )(dzc, dzc, cwb_half)


def _layer_consts(w, l):
    wm = _tril_weights(w["sgu_w"][l])
    eye = jnp.eye(4, dtype=F32)
    wbd = (w["pool_w"][l][:, :, None, :] * eye[:, None, :, None]).reshape(W_C, W_C)
    cwb = jnp.concatenate([w["conv_w"][l], w["conv_b"][l][None], jnp.zeros((CONV_ROWS - 4, 2 * D_FF), F32)], axis=0)
    return dict(
        g1=w["norm1_g"][l][None], g2=w["norm2_g"][l][None], gm=w["mix_norm_g"][l][None],
        sng=w["sgu_norm_g"][l][None], wm=wm.astype(MXU_DT), wmt=jnp.swapaxes(wm, 1, 2).astype(MXU_DT),
        bias=jnp.repeat(jnp.transpose(w["sgu_b"][l]), HEAD_DIM, axis=1),
        wbd=wbd.astype(MXU_DT), scale=w["pool_scale"][l][None], cwb=cwb,
        w_in=w["w_in"][l], w_o=w["w_o"][l], w_up=w["w_up"][l], w_down=w["w_down"][l],
    )


def _local_step(x, tgt, w):
    gmat = _group_matrix()
    tri = _tri_matrix()
    trit = jnp.transpose(tri)
    saved = []
    for l in range(DEPTH):
        c = _layer_consts(w, l)
        proj, proj_b, h1 = _rms_mm(x, c["g1"], c["w_in"], IN_COLS // 3, "in_proj")
        ya = _sgu_fwd(proj, c["sng"], c["wm"], c["bias"], gmat)
        yb, rb = _sb_fwd(proj_b, tri)
        yc = _pool_fwd(proj, c["wbd"], c["scale"])
        x2, yn = _mix_out(ya, yb, yc, c["gm"], c["w_o"], x, gmat)
        z, _, h2 = _rms_mm(x2, c["g2"], c["w_up"], TN_FF, "up_proj")
        f = _conv_gate(z, c["cwb"])
        x3 = _mm_res(f, c["w_down"], x2, "down_proj")
        saved.append(dict(c=c, x=x, proj=proj, proj_b=proj_b, h1=h1, ya=ya, yb=yb, yc=yc, rb=rb, x2=x2, yn=yn,
                          z=z, h2=h2, f=f))
        x = x3

    dx, d_final_g, loss8 = _loss_head(x, w["final_g"][None], tgt)
    grads = {n: [None] * DEPTH for n in ("norm1_g", "w_in", "sgu_norm_g", "sgu_w", "sgu_b", "pool_w", "pool_scale",
                                         "mix_norm_g", "w_o", "norm2_g", "w_up", "conv_w", "conv_b", "w_down")}
    for l in reversed(range(DEPTH)):
        s = saved[l]
        c = s["c"]
        df = _mm_nt(dx, c["w_down"], "down_proj_bwd")
        grads["w_down"][l] = _mm_tn(s["f"], dx, "down_proj_wgrad")
        dzg, dzu, dcg, dcu = _conv_gate_bwd(s["z"], df, c["cwb"])
        dz_g = _conv_transpose(dzg, c["cwb"][:, :D_FF], "conv_t_gate")
        dz_u = _conv_transpose(dzu, c["cwb"][:, D_FF:], "conv_t_value")
        dcwb = jnp.concatenate([dcg, dcu], axis=1)
        grads["conv_w"][l] = dcwb[:3]
        grads["conv_b"][l] = dcwb[3]
        grads["w_up"][l] = jnp.concatenate([_mm_tn(s["h2"], dz_g, "up_proj_wgrad_gate"),
                                            _mm_tn(s["h2"], dz_u, "up_proj_wgrad_value")], axis=1)
        dx2, dg2 = _mm_nt_rmsbwd([(dz_g, c["w_up"][:, :D_FF]), (dz_u, c["w_up"][:, D_FF:])],
                                 s["x2"], c["g2"], dx, "up_proj_bwd")
        grads["norm2_g"][l] = dg2[0]
        grads["w_o"][l] = _mm_tn(s["yn"], dx2, "out_proj_wgrad")
        dya, dyb, dyc, dgm = _mix_out_bwd(dx2, c["w_o"], s["ya"], s["yb"], s["yc"], c["gm"], gmat)
        grads["mix_norm_g"][l] = dgm[0]
        dd, e, dwbd, dscale = _pool_bwd_a(s["proj"], dyc, c["wbd"], c["scale"])
        dp = _pool_bwd_b(dd, e)
        grads["pool_w"][l] = jnp.stack([dwbd[g * 64:(g + 1) * 64, g * 64:(g + 1) * 64] for g in range(4)])
        grads["pool_scale"][l] = dscale[0]
        dq, dk, dv = _sb_bwd(s["proj_b"], dyb, s["rb"], tri, trit)
        da, dwm, dbias, dsng = _sgu_bwd(s["proj"], dya, c["sng"], c["wm"], c["wmt"], c["bias"], gmat)
        grads["sgu_w"][l] = dwm
        grads["sgu_b"][l] = jnp.transpose(jnp.sum(dbias.reshape(CHUNK, 4, HEAD_DIM), axis=-1))
        grads["sgu_norm_g"][l] = dsng[0]
        dproj = jnp.concatenate([da, dq, dk, dv, dp], axis=1)
        grads["w_in"][l] = _mm_tn(s["h1"], dproj, "in_proj_wgrad")
        dx, dg1 = _mm_nt_rmsbwd([(dproj, c["w_in"])], s["x"], c["g1"], dx2, "in_proj_bwd")
        grads["norm1_g"][l] = dg1[0]

    out = {n: jnp.stack(v) for n, v in grads.items()}
    out["final_g"] = d_final_g[0]
    return loss8[0, 0], dx, out


MESH = pl.DeviceIdType.MESH
ANY = pl.BlockSpec(memory_space=pl.ANY)


def _all_gather(x_half):
    m_per, n = x_half.shape

    def body(x_ref, out_ref, send_sems, recv_sems, local_sem):
        x, y, c = lax.axis_index("x"), lax.axis_index("y"), lax.axis_index("c")
        me, sibling = (x, y, c), (x, y, 1 - c)
        chips = [(1 - x, y), (x, 1 - y), (1 - x, 1 - y)]

        def rows(px, py, pc):
            return out_ref.at[pl.ds((4 * px + 2 * py + pc) * m_per, m_per), :]

        def copy(k, block, to, src=None):
            return pltpu.make_async_remote_copy(
                src_ref=rows(*block) if src is None else src, dst_ref=rows(*block),
                send_sem=send_sems.at[k], recv_sem=recv_sems.at[k], device_id=to, device_id_type=MESH)

        mine = pltpu.make_async_copy(x_ref, rows(*me), local_sem)
        mine.start()
        first = [copy(0, me, sibling, src=x_ref)]
        first += [copy(1 + j, me, (*chip, c), src=x_ref) for j, chip in enumerate(chips)]
        for cp in first:
            cp.start()
        passed = [copy(4 + j, (*chip, c), sibling) for j, chip in enumerate(chips)]
        for j, chip in enumerate(chips):
            copy(1 + j, (*chip, c), me).wait_recv()
            passed[j].start()
        copy(0, sibling, me).wait_recv()
        for j, chip in enumerate(chips):
            copy(4 + j, (*chip, 1 - c), me).wait_recv()
        for cp in first + passed:
            cp.wait_send()
        mine.wait()

    return _call(
        body, name="weight_all_gather",
        out_shape=jax.ShapeDtypeStruct((8 * m_per, n), x_half.dtype),
        in_specs=[ANY], out_specs=ANY,
        scratch_shapes=[pltpu.SemaphoreType.DMA((7,)), pltpu.SemaphoreType.DMA((7,)), pltpu.SemaphoreType.DMA],
    )(x_half)


def _swap_cores(a, name):
    def body(a_ref, out_ref, send_sem, recv_sem):
        x, y, c = lax.axis_index("x"), lax.axis_index("y"), lax.axis_index("c")
        cp = pltpu.make_async_remote_copy(src_ref=a_ref, dst_ref=out_ref, send_sem=send_sem, recv_sem=recv_sem,
                                          device_id=(x, y, 1 - c), device_id_type=MESH)
        cp.start()
        cp.wait()

    return _call(
        body, name=name, out_shape=jax.ShapeDtypeStruct(a.shape, a.dtype), in_specs=[ANY], out_specs=ANY,
        scratch_shapes=[pltpu.SemaphoreType.DMA, pltpu.SemaphoreType.DMA],
    )(a)


def _exchange_chips(a):
    def body(a_ref, out_ref, send_sems, recv_sems, local_sem):
        x, y, c = lax.axis_index("x"), lax.axis_index("y"), lax.axis_index("c")
        my_chip = 2 * x + y
        chips = [(1 - x, y), (x, 1 - y), (1 - x, 1 - y)]
        mine = pltpu.make_async_copy(a_ref.at[my_chip], out_ref.at[my_chip], local_sem)
        mine.start()
        copies = []
        for k, (px, py) in enumerate(chips):
            copies.append(pltpu.make_async_remote_copy(
                src_ref=a_ref.at[2 * px + py], dst_ref=out_ref.at[my_chip],
                send_sem=send_sems.at[k], recv_sem=recv_sems.at[k], device_id=(px, py, c), device_id_type=MESH))
        for cp in copies:
            cp.start()
        for k, (px, py) in enumerate(chips):
            pltpu.make_async_remote_copy(
                src_ref=a_ref.at[my_chip], dst_ref=out_ref.at[2 * px + py],
                send_sem=send_sems.at[k], recv_sem=recv_sems.at[k], device_id=(px, py, c),
                device_id_type=MESH).wait_recv()
        for cp in copies:
            cp.wait_send()
        mine.wait()

    return _call(
        body, name="grad_exchange_chips", out_shape=jax.ShapeDtypeStruct(a.shape, a.dtype),
        in_specs=[ANY], out_specs=ANY,
        scratch_shapes=[pltpu.SemaphoreType.DMA((3,)), pltpu.SemaphoreType.DMA((3,)), pltpu.SemaphoreType.DMA],
    )(a)


def _add2(a, b, name):
    n, r, c = a.shape
    tr = 512
    spec = pl.BlockSpec((1, tr, c), lambda k, i: (k, i, 0))

    def body(a_ref, b_ref, o_ref):
        o_ref[...] = a_ref[...] + b_ref[...]

    return _call(body, name=name, grid=(n, r // tr), in_specs=[spec, spec], out_specs=spec,
                 out_shape=jax.ShapeDtypeStruct(a.shape, a.dtype), compiler_params=_params("parallel", "parallel"))(a, b)


def _sum_chips(a):
    _, r, c = a.shape
    tr = 512

    def body(a_ref, o_ref):
        o_ref[...] = ((a_ref[0] + a_ref[1]) + a_ref[2]) + a_ref[3]

    return _call(body, name="grad_sum_chips", grid=(r // tr,),
                 in_specs=[pl.BlockSpec((N_CHIPS, tr, c), lambda i: (0, i, 0))],
                 out_specs=pl.BlockSpec((tr, c), lambda i: (i, 0)),
                 out_shape=jax.ShapeDtypeStruct((r, c), a.dtype), compiler_params=_params("parallel"))(a)


def _adamw(g, w, m, v):
    r, c = g.shape
    tr = 256
    spec = pl.BlockSpec((tr, c), lambda i: (i, 0))

    def body(g_ref, w_ref, m_ref, v_ref, d_ref, nm_ref, nv_ref):
        gv = g_ref[...]
        nm = ADAM_B1 * m_ref[...] + (1.0 - ADAM_B1) * gv
        nv = ADAM_B2 * v_ref[...] + (1.0 - ADAM_B2) * (gv * gv)
        m_hat = nm / (1.0 - ADAM_B1 ** ADAM_STEP)
        v_hat = nv / (1.0 - ADAM_B2 ** ADAM_STEP)
        d_ref[...] = -ADAM_LR * (m_hat / (jnp.sqrt(v_hat) + ADAM_EPS) + ADAM_WD * w_ref[...])
        nm_ref[...] = nm
        nv_ref[...] = nv

    shp = jax.ShapeDtypeStruct((r, c), F32)
    return _call(body, name="adamw", grid=(r // tr,), in_specs=[spec] * 4, out_specs=[spec] * 3,
                 out_shape=[shp, shp, shp], compiler_params=_params("parallel"))(g, w, m, v)


def _rows(a, rows):
    flat = a.reshape(-1)
    return jnp.pad(flat, (0, rows * D_MODEL - flat.shape[0])).reshape(rows, D_MODEL)


def _small_rows(p, extra=None):
    parts = [p[n].reshape(-1) for n in SMALL_NAMES]
    if extra is not None:
        parts.append(extra.reshape(-1))
    flat = jnp.concatenate(parts)
    return jnp.pad(flat, (0, ROWS_SMALL * D_MODEL - flat.shape[0])).reshape(ROWS_SMALL, D_MODEL)


def _pack_shard(p, extra=None):
    return jnp.concatenate([
        _rows(p["w_in"], ROWS_IN), _rows(p["w_o"], ROWS_O), _rows(p["w_up"], ROWS_UP), _rows(p["w_down"], ROWS_DOWN),
        _rows(p["conv_w"], ROWS_CONV), _small_rows(p, extra)], axis=0)


def _unpack_shard(pack):
    out = {}
    o = 0
    for name, rows, shape in (("w_in", ROWS_IN, (DEPTH, D_MODEL, IN_COLS // N_CHIPS)),
                              ("w_o", ROWS_O, (DEPTH, D_MODEL // N_CHIPS, D_MODEL)),
                              ("w_up", ROWS_UP, (DEPTH, D_MODEL, 2 * D_FF // N_CHIPS)),
                              ("w_down", ROWS_DOWN, (DEPTH, D_FF // N_CHIPS, D_MODEL)),
                              ("conv_w", ROWS_CONV, (DEPTH, 3, 2 * D_FF // N_CHIPS))):
        n = 1
        for d in shape:
            n *= d
        out[name] = pack[o:o + rows].reshape(-1)[:n].reshape(shape)
        o += rows
    flat = pack[o:].reshape(-1)
    k = 0
    for name in SMALL_NAMES:
        shape = SMALL_SHAPES[name]
        n = 1
        for d in shape:
            n *= d
        out[name] = flat[k:k + n].reshape(shape)
        k += n
    out["extra"] = flat[k]
    return out


def _pack_grads_by_chip(g, loss):
    def cols(a, width):
        d, r, _ = a.shape
        return jnp.transpose(a.reshape(d, r, N_CHIPS, width), (2, 0, 1, 3))

    def rws(a, height):
        d, _, c = a.shape
        return jnp.transpose(a.reshape(d, N_CHIPS, height, c), (1, 0, 2, 3))

    per_chip = dict(w_in=cols(g["w_in"], IN_COLS // N_CHIPS), w_o=rws(g["w_o"], D_MODEL // N_CHIPS),
                    w_up=cols(g["w_up"], 2 * D_FF // N_CHIPS), w_down=rws(g["w_down"], D_FF // N_CHIPS),
                    conv_w=cols(g["conv_w"], 2 * D_FF // N_CHIPS))
    small = _small_rows(g, loss)
    packs = []
    for k in range(N_CHIPS):
        packs.append(jnp.concatenate([
            _rows(per_chip["w_in"][k], ROWS_IN), _rows(per_chip["w_o"][k], ROWS_O), _rows(per_chip["w_up"][k], ROWS_UP),
            _rows(per_chip["w_down"][k], ROWS_DOWN), _rows(per_chip["conv_w"][k], ROWS_CONV), small], axis=0))
    return jnp.stack(packs)


def _gather_weights(w_pack, c):
    conv_bits = lax.bitcast_convert_type(w_pack[ROWS_BIG:ROWS_BIG + ROWS_CONV], jnp.bfloat16)
    conv_bits = conv_bits.reshape(2 * ROWS_CONV, D_MODEL)
    block = jnp.concatenate([w_pack[:ROWS_BIG].astype(jnp.bfloat16), conv_bits], axis=0)
    half = ROWS_GATHER // 2
    mine = lax.dynamic_slice_in_dim(block, c * half, half, axis=0)
    allw = _all_gather(mine).reshape(N_CHIPS, ROWS_GATHER, D_MODEL)
    o = 0

    def take(rows, shape):
        nonlocal o
        a = allw[:, o:o + rows].reshape((N_CHIPS,) + shape)
        o += rows
        return a

    w_in = take(ROWS_IN, (DEPTH, D_MODEL, IN_COLS // N_CHIPS))
    w_o = take(ROWS_O, (DEPTH, D_MODEL // N_CHIPS, D_MODEL))
    w_up = take(ROWS_UP, (DEPTH, D_MODEL, 2 * D_FF // N_CHIPS))
    w_down = take(ROWS_DOWN, (DEPTH, D_FF // N_CHIPS, D_MODEL))
    conv = lax.bitcast_convert_type(allw[:, o:o + 2 * ROWS_CONV].reshape(N_CHIPS, ROWS_CONV, D_MODEL, 2), F32)
    conv = conv.reshape(N_CHIPS, -1)[:, :DEPTH * 3 * (2 * D_FF // N_CHIPS)].reshape(N_CHIPS, DEPTH, 3, 2 * D_FF // N_CHIPS)

    def by_cols(a):
        k, d, r, wd = a.shape
        return jnp.transpose(a, (1, 2, 0, 3)).reshape(d, r, k * wd)

    def by_rows(a):
        k, d, hgt, cc = a.shape
        return jnp.transpose(a, (1, 0, 2, 3)).reshape(d, k * hgt, cc)

    return dict(w_in=by_cols(w_in), w_o=by_rows(w_o), w_up=by_cols(w_up), w_down=by_rows(w_down), conv_w=by_cols(conv))


def _reduce_grads(g_pack, c):
    half = ROWS_PACK // 2
    keep = lax.dynamic_slice_in_dim(g_pack, c * half, half, axis=1)
    give = lax.dynamic_slice_in_dim(g_pack, (1 - c) * half, half, axis=1)
    got = _swap_cores(give, "grad_swap_cores")
    pair = _add2(keep, got, "grad_add_cores")
    total = _sum_chips(_exchange_chips(pair))
    other = _swap_cores(total, "grad_share_cores")
    lo = jnp.where(c == 0, total, other)
    hi = jnp.where(c == 0, other, total)
    return jnp.concatenate([lo, hi], axis=0)


def kernel(x, norm1_g, w_in, sgu_norm_g, sgu_w, sgu_b, pool_w, pool_scale, mix_norm_g, w_o, norm2_g, w_up, conv_w, conv_b, w_down, final_g, loss_target, m_norm1_g, m_w_in, m_sgu_norm_g, m_sgu_w, m_sgu_b, m_pool_w, m_pool_scale, m_mix_norm_g, m_w_o, m_norm2_g, m_w_up, m_conv_w, m_conv_b, m_w_down, m_final_g, v_norm1_g, v_w_in, v_sgu_norm_g, v_sgu_w, v_sgu_b, v_pool_w, v_pool_scale, v_mix_norm_g, v_w_o, v_norm2_g, v_w_up, v_conv_w, v_conv_b, v_w_down, v_final_g):
    names = ("norm1_g", "w_in", "sgu_norm_g", "sgu_w", "sgu_b", "pool_w", "pool_scale", "mix_norm_g", "w_o",
             "norm2_g", "w_up", "conv_w", "conv_b", "w_down", "final_g")
    p = dict(zip(names, (norm1_g, w_in, sgu_norm_g, sgu_w, sgu_b, pool_w, pool_scale, mix_norm_g, w_o, norm2_g,
                         w_up, conv_w, conv_b, w_down, final_g)))
    pm = dict(zip(names, (m_norm1_g, m_w_in, m_sgu_norm_g, m_sgu_w, m_sgu_b, m_pool_w, m_pool_scale, m_mix_norm_g,
                          m_w_o, m_norm2_g, m_w_up, m_conv_w, m_conv_b, m_w_down, m_final_g)))
    pv = dict(zip(names, (v_norm1_g, v_w_in, v_sgu_norm_g, v_sgu_w, v_sgu_b, v_pool_w, v_pool_scale, v_mix_norm_g,
                          v_w_o, v_norm2_g, v_w_up, v_conv_w, v_conv_b, v_w_down, v_final_g)))
    c = lax.axis_index("c")
    zero = jnp.zeros((), F32)
    w_pack = _pack_shard(p, zero)
    full = dict(p)
    full.update(_gather_weights(w_pack, c))

    loss, dx, grads = _local_step(x[0], loss_target[0], full)

    g_pack = _reduce_grads(_pack_grads_by_chip(grads, loss), c)
    d_pack, m_pack, v_pack = _adamw(g_pack, w_pack, _pack_shard(pm, zero), _pack_shard(pv, zero))
    g = _unpack_shard(g_pack)
    d = _unpack_shard(d_pack)
    nm = _unpack_shard(m_pack)
    nv = _unpack_shard(v_pack)
    return (g["extra"], dx[None], *[g[n] for n in names], *[d[n] for n in names], *[nm[n] for n in names],
            *[nv[n] for n in names])
```

```python
import functools

import jax
import jax.numpy as jnp
from jax import lax
from jax.experimental import pallas as pl
from jax.experimental.pallas import tpu as pltpu

F32 = jnp.float32
MXU_DT = jnp.bfloat16

D_MODEL = 1024
DEPTH = 2
HEAD_DIM = 64
W_A = 256
W_B = 512
W_C = 256
IN_COLS = 2 * W_A + 3 * W_B + W_C
CHUNK = 128
POOL_WINDOWS = (2, 4, 8, 16)
D_FF = 2816
EPS = 1e-6
N_CHIPS = 4

ADAM_LR = 0.001
ADAM_B1 = 0.9
ADAM_B2 = 0.999
ADAM_EPS = 1e-08
ADAM_WD = 0.01
ADAM_STEP = 10

LANES = 128
TQ = 512
TK = 256
TM = 256
TM_MM = 512
HALO = 16
VMEM_LIMIT = 56 * 1024 * 1024

ROWS_IN = DEPTH * D_MODEL * (IN_COLS // N_CHIPS) // D_MODEL
ROWS_O = DEPTH * (D_MODEL // N_CHIPS)
ROWS_UP = DEPTH * D_MODEL * (2 * D_FF // N_CHIPS) // D_MODEL
ROWS_DOWN = DEPTH * (D_FF // N_CHIPS)
ROWS_BIG = ROWS_IN + ROWS_O + ROWS_UP + ROWS_DOWN
ROWS_CONV = 16
ROWS_SMALL = 240
ROWS_PACK = ROWS_BIG + ROWS_CONV + ROWS_SMALL
ROWS_GATHER = ROWS_BIG + 2 * ROWS_CONV

BIG_NAMES = ("w_in", "w_o", "w_up", "w_down", "conv_w")
SMALL_NAMES = ("norm1_g", "sgu_norm_g", "sgu_w", "sgu_b", "pool_w", "pool_scale",
               "mix_norm_g", "norm2_g", "conv_b", "final_g")
SMALL_SHAPES = {
    "norm1_g": (DEPTH, D_MODEL), "sgu_norm_g": (DEPTH, W_A), "sgu_w": (DEPTH, 4, CHUNK, CHUNK),
    "sgu_b": (DEPTH, 4, CHUNK), "pool_w": (DEPTH, 4, 64, 64), "pool_scale": (DEPTH, W_C),
    "mix_norm_g": (DEPTH, D_MODEL), "norm2_g": (DEPTH, D_MODEL), "conv_b": (DEPTH, 2 * D_FF),
    "final_g": (D_MODEL,),
}


def _call(body, **kw):
    return pl.pallas_call(body, **kw)


def _params(*sem):
    return pltpu.CompilerParams(dimension_semantics=sem, vmem_limit_bytes=VMEM_LIMIT)


def _dot(a, b):
    return jnp.dot(a, b, preferred_element_type=F32)


def _dot_nt(a, b):
    return lax.dot_general(a, b, (((1,), (1,)), ((), ())), preferred_element_type=F32)


def _dot_tn(a, b):
    return lax.dot_general(a, b, (((0,), (0,)), ((), ())), preferred_element_type=F32)


def _split(a):
    hi = a.astype(MXU_DT)
    lo = (a - hi.astype(F32)).astype(MXU_DT)
    return hi, lo


def _dot_split(a, b):
    hi, lo = _split(a)
    return _dot(hi, b) + _dot(lo, b)


def _group_mean(sq, gmat):
    cols = [_dot_split(sq[:, b * LANES:(b + 1) * LANES], gmat) for b in range(sq.shape[1] // LANES)]
    return cols[0] if len(cols) == 1 else jnp.concatenate(cols, axis=-1)


def _group_matrix():
    r = jnp.arange(LANES)
    return jnp.where((r[:, None] // HEAD_DIM) == (r[None, :] // HEAD_DIM), 1.0 / HEAD_DIM, 0.0).astype(MXU_DT)


def _tile(n):
    return max(t for t in range(LANES, 1536 + 1, LANES) if n % t == 0)


def _row_spec(tm, cols, col_block=0):
    return pl.BlockSpec((tm, cols), lambda i, cb=col_block: (i, cb))


def _full_spec(shape):
    nd = len(shape)
    return pl.BlockSpec(shape, lambda *_: (0,) * nd)


def _rms_mm(x, g, w, tn, name):
    S, D = x.shape
    N = w.shape[1]
    tm = TM_MM

    def body(x_ref, g_ref, w_ref, o_ref, ob_ref, h_ref):
        @pl.when(pl.program_id(1) == 0)
        def _():
            xv = x_ref[...]
            r = lax.rsqrt(jnp.mean(xv * xv, axis=-1, keepdims=True) + EPS)
            h_ref[...] = (xv * r * g_ref[...]).astype(h_ref.dtype)

        acc = _dot(h_ref[...], w_ref[...])
        o_ref[...] = acc
        ob_ref[...] = acc.astype(ob_ref.dtype)

    return _call(
        body, name=name, grid=(S // tm, N // tn),
        in_specs=[pl.BlockSpec((tm, D), lambda i, j: (i, 0)),
                  pl.BlockSpec((1, D), lambda i, j: (0, 0)),
                  pl.BlockSpec((D, tn), lambda i, j: (0, j))],
        out_specs=[pl.BlockSpec((tm, tn), lambda i, j: (i, j)),
                   pl.BlockSpec((tm, tn), lambda i, j: (i, j)),
                   pl.BlockSpec((tm, D), lambda i, j: (i, 0))],
        out_shape=[jax.ShapeDtypeStruct((S, N), F32), jax.ShapeDtypeStruct((S, N), MXU_DT),
                   jax.ShapeDtypeStruct((S, D), MXU_DT)],
        compiler_params=_params("parallel", "arbitrary"),
    )(x, g, w)


def _mm_res(a, w, res, name):
    S, K = a.shape
    N = w.shape[1]
    tm = TM_MM

    def body(a_ref, w_ref, r_ref, o_ref):
        o_ref[...] = r_ref[...] + _dot(a_ref[...], w_ref[...])

    return _call(
        body, name=name, grid=(S // tm,),
        in_specs=[_row_spec(tm, K), _full_spec((K, N)), _row_spec(tm, N)],
        out_specs=_row_spec(tm, N),
        out_shape=jax.ShapeDtypeStruct((S, N), F32),
        compiler_params=_params("parallel"),
    )(a, w, res)


def _mm_nt(a, w, name):
    S, K = a.shape
    N = w.shape[0]
    tm = TM_MM

    def body(a_ref, w_ref, o_ref):
        o_ref[...] = _dot_nt(a_ref[...].astype(MXU_DT), w_ref[...])

    return _call(
        body, name=name, grid=(S // tm,),
        in_specs=[_row_spec(tm, K), _full_spec((N, K))],
        out_specs=_row_spec(tm, N),
        out_shape=jax.ShapeDtypeStruct((S, N), F32),
        compiler_params=_params("parallel"),
    )(a, w)


def _mm_tn(a, b, name):
    S, K1 = a.shape
    N = b.shape[1]
    ts = TM_MM
    tk = _tile(K1)
    tn = _tile(N)

    def body(a_ref, b_ref, o_ref):
        @pl.when(pl.program_id(2) == 0)
        def _():
            o_ref[...] = jnp.zeros_like(o_ref)

        o_ref[...] += _dot_tn(a_ref[...], b_ref[...].astype(MXU_DT))

    return _call(
        body, name=name, grid=(K1 // tk, N // tn, S // ts),
        in_specs=[pl.BlockSpec((ts, tk), lambda m, n, s: (s, m)),
                  pl.BlockSpec((ts, tn), lambda m, n, s: (s, n))],
        out_specs=pl.BlockSpec((tk, tn), lambda m, n, s: (m, n)),
        out_shape=jax.ShapeDtypeStruct((K1, N), F32),
        compiler_params=_params("parallel", "parallel", "arbitrary"),
    )(a, b)


def _mm_nt_rmsbwd(pairs, x, g, dres, name):
    S, D = x.shape
    tm = TM
    n = len(pairs)

    def body(*refs):
        a_refs = refs[:n]
        w_refs = refs[n:2 * n]
        x_ref, g_ref, r_ref, dx_ref, dg_ref = refs[2 * n:]
        dh = _dot_nt(a_refs[0][...], w_refs[0][...])
        for k in range(1, n):
            dh += _dot_nt(a_refs[k][...], w_refs[k][...])
        xv = x_ref[...]
        r = lax.rsqrt(jnp.mean(xv * xv, axis=-1, keepdims=True) + EPS)
        xhat = xv * r

        @pl.when(pl.program_id(0) == 0)
        def _():
            dg_ref[...] = jnp.zeros_like(dg_ref)

        dg_ref[...] += jnp.sum(dh * xhat, axis=0, keepdims=True)
        dxh = dh * g_ref[...]
        dx_ref[...] = r_ref[...] + r * (dxh - xhat * jnp.mean(dxh * xhat, axis=-1, keepdims=True))

    in_specs = ([_row_spec(tm, a.shape[1]) for a, _ in pairs] + [_full_spec(w.shape) for _, w in pairs]
                + [_row_spec(tm, D), _full_spec((1, D)), _row_spec(tm, D)])
    return _call(
        body, name=name, grid=(S // tm,), in_specs=in_specs,
        out_specs=[_row_spec(tm, D), _full_spec((1, D))],
        out_shape=[jax.ShapeDtypeStruct((S, D), F32), jax.ShapeDtypeStruct((1, D), F32)],
        compiler_params=_params("arbitrary"),
    )(*[a for a, _ in pairs], *[w for _, w in pairs], x, g, dres)


def _loss_head(x, g, tgt):
    S, D = x.shape
    tm = TM

    def body(x_ref, g_ref, t_ref, dx_ref, dg_ref, l_ref):
        xv = x_ref[...]
        r = lax.rsqrt(jnp.mean(xv * xv, axis=-1, keepdims=True) + EPS)
        xhat = xv * r
        diff = xhat * g_ref[...] - t_ref[...]

        @pl.when(pl.program_id(0) == 0)
        def _():
            dg_ref[...] = jnp.zeros_like(dg_ref)
            l_ref[...] = jnp.zeros_like(l_ref)

        l_ref[...] += jnp.full(l_ref.shape, 0.5 * jnp.sum(jnp.mean(diff * diff, axis=-1, keepdims=True)), F32)
        dout = diff * (1.0 / D)
        dg_ref[...] += jnp.sum(dout * xhat, axis=0, keepdims=True)
        dxh = dout * g_ref[...]
        dx_ref[...] = r * (dxh - xhat * jnp.mean(dxh * xhat, axis=-1, keepdims=True))

    return _call(
        body, name="loss_head", grid=(S // tm,),
        in_specs=[_row_spec(tm, D), _full_spec((1, D)), _row_spec(tm, D)],
        out_specs=[_row_spec(tm, D), _full_spec((1, D)), _full_spec((8, LANES))],
        out_shape=[jax.ShapeDtypeStruct((S, D), F32), jax.ShapeDtypeStruct((1, D), F32),
                   jax.ShapeDtypeStruct((8, LANES), F32)],
        compiler_params=_params("arbitrary"),
    )(x, g, tgt)


def _mix_out(ya, yb, yc, gm, wo, x, gmat):
    S = x.shape[0]
    tm = TM

    def body(ya_ref, yb_ref, yc_ref, gm_ref, wo_ref, x_ref, gmat_ref, x2_ref, yn_ref):
        y = jnp.concatenate([ya_ref[...], yb_ref[...], yc_ref[...]], axis=-1)
        r = lax.rsqrt(_group_mean(y * y, gmat_ref[...]) + EPS)
        yn = (y * r * gm_ref[...]).astype(MXU_DT)
        yn_ref[...] = yn
        x2_ref[...] = x_ref[...] + _dot(yn, wo_ref[...])

    return _call(
        body, name="mix_out", grid=(S // tm,),
        in_specs=[_row_spec(tm, W_A), _row_spec(tm, W_B), _row_spec(tm, W_C), _full_spec((1, D_MODEL)),
                  _full_spec((D_MODEL, D_MODEL)), _row_spec(tm, D_MODEL), _full_spec((LANES, LANES))],
        out_specs=[_row_spec(tm, D_MODEL), _row_spec(tm, D_MODEL)],
        out_shape=[jax.ShapeDtypeStruct((S, D_MODEL), F32), jax.ShapeDtypeStruct((S, D_MODEL), MXU_DT)],
        compiler_params=_params("parallel"),
    )(ya, yb, yc, gm, wo, x, gmat)


def _mix_out_bwd(dx2, wo, ya, yb, yc, gm, gmat):
    S = dx2.shape[0]
    tm = TM

    def body(dx2_ref, wo_ref, ya_ref, yb_ref, yc_ref, gm_ref, gmat_ref, dya_ref, dyb_ref, dyc_ref, dgm_ref):
        dyn = _dot_nt(dx2_ref[...].astype(MXU_DT), wo_ref[...])
        y = jnp.concatenate([ya_ref[...], yb_ref[...], yc_ref[...]], axis=-1)
        r = lax.rsqrt(_group_mean(y * y, gmat_ref[...]) + EPS)
        yhat = y * r

        @pl.when(pl.program_id(0) == 0)
        def _():
            dgm_ref[...] = jnp.zeros_like(dgm_ref)

        dgm_ref[...] += jnp.sum(dyn * yhat, axis=0, keepdims=True)
        dyh = dyn * gm_ref[...]
        dy = r * (dyh - yhat * _group_mean(dyh * yhat, gmat_ref[...]))
        dya_ref[...] = dy[:, :W_A]
        dyb_ref[...] = dy[:, W_A:W_A + W_B]
        dyc_ref[...] = dy[:, W_A + W_B:]

    return _call(
        body, name="mix_out_bwd", grid=(S // tm,),
        in_specs=[_row_spec(tm, D_MODEL), _full_spec((D_MODEL, D_MODEL)), _row_spec(tm, W_A), _row_spec(tm, W_B),
                  _row_spec(tm, W_C), _full_spec((1, D_MODEL)), _full_spec((LANES, LANES))],
        out_specs=[_row_spec(tm, W_A), _row_spec(tm, W_B), _row_spec(tm, W_C), _full_spec((1, D_MODEL))],
        out_shape=[jax.ShapeDtypeStruct((S, W_A), F32), jax.ShapeDtypeStruct((S, W_B), F32),
                   jax.ShapeDtypeStruct((S, W_C), F32), jax.ShapeDtypeStruct((1, D_MODEL), F32)],
        compiler_params=_params("arbitrary"),
    )(dx2, wo, ya, yb, yc, gm, gmat)


_SQRT_HALF = 0.7071067811865476
_INV_SQRT_2PI = 0.3989422804014327


def _sgu_common(a, sng, wm_ref, bias, gmat):
    phi = 0.5 * (1.0 + lax.erf(a * _SQRT_HALF))
    ga = a * phi
    u = ga[:, :W_A]
    v = ga[:, W_A:]
    r = lax.rsqrt(_group_mean(v * v, gmat) + EPS)
    vhat = v * r
    vn = (vhat * sng).astype(MXU_DT)
    head = lax.broadcasted_iota(jnp.int32, (CHUNK, W_A), 1) // HEAD_DIM
    rows = []
    for c in range(a.shape[0] // CHUNK):
        vc = vn[c * CHUNK:(c + 1) * CHUNK]
        s = bias
        for h in range(4):
            s = s + jnp.where(head == h, _dot(wm_ref[h], vc), 0.0)
        rows.append(s)
    s = jnp.concatenate(rows, axis=0)
    return phi, u, r, vhat, vn, s


def _tril_weights(sgu_w_l):
    t = jnp.arange(CHUNK)
    return jnp.where((t[None, :] <= t[:, None])[None], sgu_w_l, 0.0)


def _sgu_fwd(proj, sng, wm, bias, gmat):
    S = proj.shape[0]
    tm = TM

    def body(a_ref, sng_ref, wm_ref, b_ref, gmat_ref, y_ref):
        _, u, _, _, _, s = _sgu_common(a_ref[...], sng_ref[...], wm_ref, b_ref[...], gmat_ref[...])
        y_ref[...] = u * s

    return _call(
        body, name="sgu_fwd", grid=(S // tm,),
        in_specs=[_row_spec(tm, 2 * W_A), _full_spec((1, W_A)), _full_spec((4, CHUNK, CHUNK)),
                  _full_spec((CHUNK, W_A)), _full_spec((LANES, LANES))],
        out_specs=_row_spec(tm, W_A),
        out_shape=jax.ShapeDtypeStruct((S, W_A), F32),
        compiler_params=_params("parallel"),
    )(proj, sng, wm, bias, gmat)


def _sgu_bwd(proj, dy, sng, wm, wmt, bias, gmat):
    S = proj.shape[0]
    tm = TM

    def body(a_ref, dy_ref, sng_ref, wm_ref, wmt_ref, b_ref, gmat_ref, da_ref, dw_ref, db_ref, dsng_ref):
        a = a_ref[...]
        dy = dy_ref[...]
        gmat = gmat_ref[...]
        sng = sng_ref[...]
        phi, u, r, vhat, vn, s = _sgu_common(a, sng, wm_ref, b_ref[...], gmat)
        du = dy * s
        ds = dy * u

        @pl.when(pl.program_id(0) == 0)
        def _():
            dw_ref[...] = jnp.zeros_like(dw_ref)
            db_ref[...] = jnp.zeros_like(db_ref)
            dsng_ref[...] = jnp.zeros_like(dsng_ref)

        head = lax.broadcasted_iota(jnp.int32, (CHUNK, W_A), 1) // HEAD_DIM
        tt = lax.broadcasted_iota(jnp.int32, (CHUNK, CHUNK), 0)
        ss = lax.broadcasted_iota(jnp.int32, (CHUNK, CHUNK), 1)
        rows = []
        for c in range(tm // CHUNK):
            dsc = ds[c * CHUNK:(c + 1) * CHUNK]
            vc = vn[c * CHUNK:(c + 1) * CHUNK]
            db_ref[...] += dsc
            dsb = dsc.astype(MXU_DT)
            dvn = jnp.zeros((CHUNK, W_A), F32)
            for h in range(4):
                dvn = dvn + jnp.where(head == h, _dot(wmt_ref[h], dsb), 0.0)
                dsh = jnp.where(head == h, dsc, 0.0).astype(MXU_DT)
                dw_ref[h] += jnp.where(ss <= tt, _dot_nt(dsh, vc), 0.0)
            rows.append(dvn)
        dvn = jnp.concatenate(rows, axis=0)
        dsng_ref[...] += jnp.sum(dvn * vhat, axis=0, keepdims=True)
        dvh = dvn * sng
        dv = r * (dvh - vhat * _group_mean(dvh * vhat, gmat))
        dga = jnp.concatenate([du, dv], axis=-1)
        dgelu = phi + a * (_INV_SQRT_2PI * jnp.exp(-0.5 * a * a))
        da_ref[...] = (dga * dgelu).astype(da_ref.dtype)

    return _call(
        body, name="sgu_bwd", grid=(S // tm,),
        in_specs=[_row_spec(tm, 2 * W_A), _row_spec(tm, W_A), _full_spec((1, W_A)), _full_spec((4, CHUNK, CHUNK)),
                  _full_spec((4, CHUNK, CHUNK)), _full_spec((CHUNK, W_A)), _full_spec((LANES, LANES))],
        out_specs=[_row_spec(tm, 2 * W_A), _full_spec((4, CHUNK, CHUNK)), _full_spec((CHUNK, W_A)),
                   _full_spec((1, W_A))],
        out_shape=[jax.ShapeDtypeStruct((S, 2 * W_A), MXU_DT), jax.ShapeDtypeStruct((4, CHUNK, CHUNK), F32),
                   jax.ShapeDtypeStruct((CHUNK, W_A), F32), jax.ShapeDtypeStruct((1, W_A), F32)],
        compiler_params=_params("arbitrary"),
    )(proj, dy, sng, wm, wmt, bias, gmat)


Q_BLK0 = (2 * W_A) // LANES
K_BLK0 = Q_BLK0 + W_B // LANES
V_BLK0 = K_BLK0 + W_B // LANES
N_PAIRS = W_B // LANES


def _tri_matrix():
    r = jnp.arange(TK)
    return (r[:, None] > r[None, :]).astype(MXU_DT)


def _stack_heads(a):
    lane = lax.broadcasted_iota(jnp.int32, a.shape, 1)
    return jnp.concatenate([jnp.where(lane < HEAD_DIM, a, 0.0), jnp.where(lane >= HEAD_DIM, a, 0.0)],
                           axis=0).astype(MXU_DT)


def _unstack_heads(a):
    lane = lax.broadcasted_iota(jnp.int32, (TQ, LANES), 1)
    return jnp.where(lane < HEAD_DIM, a[:TQ], a[TQ:])


def _sb_scores(q2, kj, tri, key_offset):
    z = _dot_nt(q2, kj)
    sp = jnp.log(1.0 + jnp.exp(-jnp.abs(z)))
    lsp = jnp.minimum(z, 0.0) - sp
    lsm = lsp - z
    msk = None
    if key_offset is not None:
        row = lax.broadcasted_iota(jnp.int32, z.shape, 0) & (TQ - 1)
        col = lax.broadcasted_iota(jnp.int32, z.shape, 1) + key_offset
        msk = col < row
        lsm = jnp.where(msk, lsm, 0.0)
    tail = _dot(lsm.astype(MXU_DT), tri)
    return lsp, lsm, tail, msk


def _sb_fwd(proj_b, tri):
    S = proj_b.shape[0]
    nq = S // TQ
    kpq = TQ // TK
    assert S // TK <= LANES

    def body(q_ref, k_ref, v_ref, tri_ref, o_ref, rb_ref, acc_ref):
        i = pl.program_id(1)
        lane2 = lax.broadcasted_iota(jnp.int32, (2 * TQ, LANES), 1)
        q2 = _stack_heads(q_ref[...].astype(F32) * (HEAD_DIM ** -0.5))
        tri = tri_ref[...]
        rb_ref[...] = jnp.zeros_like(rb_ref)

        def block(j, run, key_offset=None, first=False):
            start = pl.multiple_of(j * TK, TK)
            kj = k_ref[pl.ds(start, TK), :]
            vj = v_ref[pl.ds(start, TK), :]
            lsp, lsm, tail, msk = _sb_scores(q2, kj, tri, key_offset)
            rb_ref[...] = jnp.where(lane2 == j, run, rb_ref[...])
            att = jnp.exp(lsp + tail + run)
            if msk is not None:
                att = jnp.where(msk, att, 0.0)
            pv = _dot(att.astype(MXU_DT), vj)
            if first:
                acc_ref[...] = pv
            else:
                acc_ref[...] += pv
            return run + tail[:, :1] + lsm[:, :1]

        run = jnp.zeros((2 * TQ, 1), F32)
        for d in reversed(range(kpq)):
            run = block(i * kpq + d, run, key_offset=d * TK, first=(d == kpq - 1))
        lax.fori_loop(0, i * kpq, lambda n, run: block(i * kpq - 1 - n, run), run)
        o_ref[...] = _unstack_heads(acc_ref[...])

    return _call(
        body, name="sb_fwd", grid=(N_PAIRS, nq),
        in_specs=[pl.BlockSpec((TQ, LANES), lambda p, i: (i, Q_BLK0 + p)),
                  pl.BlockSpec((S, LANES), lambda p, i: (0, K_BLK0 + p)),
                  pl.BlockSpec((S, LANES), lambda p, i: (0, V_BLK0 + p)),
                  pl.BlockSpec((TK, TK), lambda p, i: (0, 0))],
        out_specs=[pl.BlockSpec((TQ, LANES), lambda p, i: (i, p)),
                   pl.BlockSpec((None, None, 2 * TQ, LANES), lambda p, i: (p, i, 0, 0))],
        out_shape=[jax.ShapeDtypeStruct((S, W_B), F32), jax.ShapeDtypeStruct((N_PAIRS, nq, 2 * TQ, LANES), F32)],
        scratch_shapes=[pltpu.VMEM((2 * TQ, LANES), F32)],
        compiler_params=_params("parallel", "arbitrary"),
    )(proj_b, proj_b, proj_b, tri)


def _sb_bwd(proj_b, dyb, rb, tri, trit):
    S = proj_b.shape[0]
    nq = S // TQ

    kpq = TQ // TK

    def body(q_ref, k_ref, v_ref, do_ref, rb_ref, tri_ref, trit_ref, dq_ref, dk_ref, dv_ref,
             dq_acc, dk_acc, dv_acc):
        i = pl.program_id(1)
        lane2 = lax.broadcasted_iota(jnp.int32, (2 * TQ, LANES), 1)
        scale = HEAD_DIM ** -0.5
        q2 = _stack_heads(q_ref[...].astype(F32) * scale)
        do2 = _stack_heads(do_ref[...])
        tri = tri_ref[...]
        trit = trit_ref[...]

        @pl.when(i == 0)
        def _():
            dk_acc[...] = jnp.zeros_like(dk_acc)
            dv_acc[...] = jnp.zeros_like(dv_acc)

        dq_acc[...] = jnp.zeros_like(dq_acc)

        def block(j, pre, key_offset=None):
            start = pl.multiple_of(j * TK, TK)
            kj = k_ref[pl.ds(start, TK), :]
            vj = v_ref[pl.ds(start, TK), :]
            lsp, lsm, tail, msk = _sb_scores(q2, kj, tri, key_offset)
            run = jnp.sum(jnp.where(lane2 == j, rb_ref[...], 0.0), axis=-1, keepdims=True)
            att = jnp.exp(lsp + tail + run)
            if msk is not None:
                att = jnp.where(msk, att, 0.0)
            beta = jnp.exp(lsp)
            dl = _dot_nt(do2, vj) * att
            cin = _dot(dl.astype(MXU_DT), trit)
            dz = dl * (1.0 - beta) - beta * (pre + cin)
            if msk is not None:
                dz = jnp.where(msk, dz, 0.0)
            dzb = dz.astype(MXU_DT)
            dq_acc[...] += _dot(dzb, kj)
            dk_acc[pl.ds(start, TK), :] += _dot_tn(dzb, q2)
            dv_acc[pl.ds(start, TK), :] += _dot_tn(att.astype(MXU_DT), do2)
            return pre + cin[:, TK - 1:] + dl[:, TK - 1:]

        pre = lax.fori_loop(0, i * kpq, lambda j, pre: block(j, pre), jnp.zeros((2 * TQ, 1), F32))
        for d in range(kpq):
            pre = block(i * kpq + d, pre, key_offset=d * TK)
        dq_ref[...] = (_unstack_heads(dq_acc[...]) * scale).astype(dq_ref.dtype)

        @pl.when(i == nq - 1)
        def _():
            dk_ref[...] = dk_acc[...].astype(dk_ref.dtype)
            dv_ref[...] = dv_acc[...].astype(dv_ref.dtype)

    return _call(
        body, name="sb_bwd", grid=(N_PAIRS, nq),
        in_specs=[pl.BlockSpec((TQ, LANES), lambda p, i: (i, Q_BLK0 + p)),
                  pl.BlockSpec((S, LANES), lambda p, i: (0, K_BLK0 + p)),
                  pl.BlockSpec((S, LANES), lambda p, i: (0, V_BLK0 + p)),
                  pl.BlockSpec((TQ, LANES), lambda p, i: (i, p)),
                  pl.BlockSpec((None, None, 2 * TQ, LANES), lambda p, i: (p, i, 0, 0)),
                  pl.BlockSpec((TK, TK), lambda p, i: (0, 0)),
                  pl.BlockSpec((TK, TK), lambda p, i: (0, 0))],
        out_specs=[pl.BlockSpec((TQ, LANES), lambda p, i: (i, p)),
                   pl.BlockSpec((S, LANES), lambda p, i: (0, p)),
                   pl.BlockSpec((S, LANES), lambda p, i: (0, p))],
        out_shape=[jax.ShapeDtypeStruct((S, W_B), MXU_DT)] * 3,
        scratch_shapes=[pltpu.VMEM((2 * TQ, LANES), F32), pltpu.VMEM((S, LANES), F32), pltpu.VMEM((S, LANES), F32)],
        compiler_params=_params("parallel", "arbitrary"),
    )(proj_b, proj_b, proj_b, dyb, rb, tri, trit)


P_BLK = (2 * W_A + 3 * W_B) // W_C


def _window_lanes():
    g = lax.broadcasted_iota(jnp.int32, (1, W_C), 1) // (W_C // 4)
    w = jnp.where(g == 0, POOL_WINDOWS[0], jnp.where(g == 1, POOL_WINDOWS[1],
                  jnp.where(g == 2, POOL_WINDOWS[2], POOL_WINDOWS[3])))
    return g, w


def _shift_rows(ext, k, tm, lead):
    n = ext.shape[0]
    return pltpu.roll(ext, shift=k % n, axis=0)[lead:lead + tm]


def _pool_diff(p_cur, p_halo, row0, tm):
    ext = jnp.concatenate([p_halo, p_cur], axis=0)
    g, w = _window_lanes()
    acc = ext
    sums = []
    for sh in (1, 2, 4, 8):
        acc = acc + pltpu.roll(acc, shift=sh, axis=0)
        sums.append(acc[HALO:HALO + tm])
    wsum = jnp.where(g == 0, sums[0], jnp.where(g == 1, sums[1], jnp.where(g == 2, sums[2], sums[3])))
    pos = (row0 + 1 + lax.broadcasted_iota(jnp.int32, (tm, W_C), 0)).astype(F32)
    cnt = jnp.minimum(pos, w.astype(F32))
    return wsum / cnt - p_cur, cnt


def _pool_specs(tm, nrow_blocks_halo):
    cur = pl.BlockSpec((tm, W_C), lambda i: (i, P_BLK))
    prev = pl.BlockSpec((HALO, W_C), lambda i: (jnp.maximum(i * (tm // HALO) - 1, 0), P_BLK))
    return cur, prev


def _pool_fwd(proj, wbd, scale):
    S = proj.shape[0]
    tm = TM

    def body(p_ref, ph_ref, w_ref, sc_ref, y_ref):
        i = pl.program_id(0)
        halo = jnp.where(i > 0, ph_ref[...], 0.0)
        d, _ = _pool_diff(p_ref[...], halo, i * tm, tm)
        y_ref[...] = _dot(d.astype(MXU_DT), w_ref[...]) * sc_ref[...]

    cur, prev = _pool_specs(tm, S // HALO)
    return _call(
        body, name="pool_fwd", grid=(S // tm,),
        in_specs=[cur, prev, _full_spec((W_C, W_C)), _full_spec((1, W_C))],
        out_specs=_row_spec(tm, W_C),
        out_shape=jax.ShapeDtypeStruct((S, W_C), F32),
        compiler_params=_params("parallel"),
    )(proj, proj, wbd, scale)


def _pool_bwd_a(proj, dy, wbd, scale):
    S = proj.shape[0]
    tm = TM

    def body(p_ref, ph_ref, dy_ref, w_ref, sc_ref, dd_ref, e_ref, dw_ref, dsc_ref):
        i = pl.program_id(0)
        halo = jnp.where(i > 0, ph_ref[...], 0.0)
        d, cnt = _pool_diff(p_ref[...], halo, i * tm, tm)
        db = d.astype(MXU_DT)
        dy = dy_ref[...]

        @pl.when(i == 0)
        def _():
            dw_ref[...] = jnp.zeros_like(dw_ref)
            dsc_ref[...] = jnp.zeros_like(dsc_ref)

        dsc_ref[...] += jnp.sum(dy * _dot(db, w_ref[...]), axis=0, keepdims=True)
        dys = (dy * sc_ref[...]).astype(MXU_DT)
        dw_ref[...] += _dot_tn(db, dys)
        dd = _dot_nt(dys, w_ref[...])
        dd_ref[...] = dd
        e_ref[...] = dd / cnt

    cur, prev = _pool_specs(tm, S // HALO)
    return _call(
        body, name="pool_bwd_a", grid=(S // tm,),
        in_specs=[cur, prev, _row_spec(tm, W_C), _full_spec((W_C, W_C)), _full_spec((1, W_C))],
        out_specs=[_row_spec(tm, W_C), _row_spec(tm, W_C), _full_spec((W_C, W_C)), _full_spec((1, W_C))],
        out_shape=[jax.ShapeDtypeStruct((S, W_C), F32), jax.ShapeDtypeStruct((S, W_C), F32),
                   jax.ShapeDtypeStruct((W_C, W_C), F32), jax.ShapeDtypeStruct((1, W_C), F32)],
        compiler_params=_params("arbitrary"),
    )(proj, proj, dy, wbd, scale)


def _pool_bwd_b(dd, e):
    S = dd.shape[0]
    tm = TM
    nb = S // tm

    def body(dd_ref, e_ref, en_ref, dp_ref):
        i = pl.program_id(0)
        halo = jnp.where(i < nb - 1, en_ref[...], 0.0)
        ext = jnp.concatenate([e_ref[...], halo], axis=0)
        n = ext.shape[0]
        g, _ = _window_lanes()
        acc = ext
        sums = []
        for sh in (1, 2, 4, 8):
            acc = acc + pltpu.roll(acc, shift=n - sh, axis=0)
            sums.append(acc[:tm])
        wsum = jnp.where(g == 0, sums[0], jnp.where(g == 1, sums[1], jnp.where(g == 2, sums[2], sums[3])))
        dp_ref[...] = (wsum - dd_ref[...]).astype(dp_ref.dtype)

    nxt = pl.BlockSpec((HALO, W_C), lambda i: (jnp.minimum((i + 1) * (tm // HALO), S // HALO - 1), 0))
    return _call(
        body, name="pool_bwd_b", grid=(nb,),
        in_specs=[_row_spec(tm, W_C), _row_spec(tm, W_C), nxt],
        out_specs=_row_spec(tm, W_C),
        out_shape=jax.ShapeDtypeStruct((S, W_C), MXU_DT),
        compiler_params=_params("parallel"),
    )(dd, e, e)


TN_FF = 1408
NB_FF = D_FF // TN_FF
CONV_ROWS = 8


def _conv(z_cur, z_halo, cwb, tm):
    ext = jnp.concatenate([z_halo, z_cur], axis=0)
    z2 = _shift_rows(ext, 2, tm, HALO)
    z1 = _shift_rows(ext, 1, tm, HALO)
    zc = cwb[3:4] + z2 * cwb[0:1] + z1 * cwb[1:2] + z_cur * cwb[2:3]
    return zc, z2, z1


def _ffn_specs(tm, order):
    def mk(f):
        return (lambda i, j: f(i, j)) if order == "ij" else (lambda j, i: f(i, j))
    hb = tm // HALO
    return [
        pl.BlockSpec((tm, TN_FF), mk(lambda i, j: (i, j))),
        pl.BlockSpec((tm, TN_FF), mk(lambda i, j: (i, j + NB_FF))),
        pl.BlockSpec((HALO, TN_FF), mk(lambda i, j: (jnp.maximum(i * hb - 1, 0), j))),
        pl.BlockSpec((HALO, TN_FF), mk(lambda i, j: (jnp.maximum(i * hb - 1, 0), j + NB_FF))),
        pl.BlockSpec((CONV_ROWS, TN_FF), mk(lambda i, j: (0, j))),
        pl.BlockSpec((CONV_ROWS, TN_FF), mk(lambda i, j: (0, j + NB_FF))),
    ]


def _conv_gate(z, cwb):
    S = z.shape[0]
    tm = TM

    def body(zg_ref, zu_ref, hg_ref, hu_ref, cg_ref, cu_ref, f_ref):
        first = pl.program_id(0) == 0
        g, _, _ = _conv(zg_ref[...], jnp.where(first, 0.0, hg_ref[...]), cg_ref[...], tm)
        u, _, _ = _conv(zu_ref[...], jnp.where(first, 0.0, hu_ref[...]), cu_ref[...], tm)
        f_ref[...] = (g * jax.nn.sigmoid(g) * u).astype(f_ref.dtype)

    return _call(
        body, name="conv_gate", grid=(S // tm, NB_FF), in_specs=_ffn_specs(tm, "ij"),
        out_specs=pl.BlockSpec((tm, TN_FF), lambda i, j: (i, j)),
        out_shape=jax.ShapeDtypeStruct((S, D_FF), MXU_DT),
        compiler_params=_params("parallel", "parallel"),
    )(z, z, z, z, cwb, cwb)


def _conv_gate_bwd(z, df, cwb):
    S = z.shape[0]
    tm = TM

    def body(zg_ref, zu_ref, hg_ref, hu_ref, cg_ref, cu_ref, df_ref, dg_ref, du_ref, dcg_ref, dcu_ref):
        i = pl.program_id(1)
        first = i == 0
        zg = zg_ref[...]
        zu = zu_ref[...]
        g, g2, g1 = _conv(zg, jnp.where(first, 0.0, hg_ref[...]), cg_ref[...], tm)
        u, u2, u1 = _conv(zu, jnp.where(first, 0.0, hu_ref[...]), cu_ref[...], tm)
        sg = jax.nn.sigmoid(g)
        df = df_ref[...]
        dgv = df * u * (sg * (1.0 + g * (1.0 - sg)))
        duv = df * (g * sg)
        dg_ref[...] = dgv
        du_ref[...] = duv

        @pl.when(first)
        def _():
            dcg_ref[...] = jnp.zeros_like(dcg_ref)
            dcu_ref[...] = jnp.zeros_like(dcu_ref)

        rid = lax.broadcasted_iota(jnp.int32, (CONV_ROWS, TN_FF), 0)

        def taps(dv, s2, s1, s0):
            sums = [jnp.sum(dv * s2, axis=0, keepdims=True), jnp.sum(dv * s1, axis=0, keepdims=True),
                    jnp.sum(dv * s0, axis=0, keepdims=True), jnp.sum(dv, axis=0, keepdims=True)]
            out = jnp.zeros((CONV_ROWS, TN_FF), F32)
            for k, v in enumerate(sums):
                out = jnp.where(rid == k, v, out)
            return out

        dcg_ref[...] += taps(dgv, g2, g1, zg)
        dcu_ref[...] += taps(duv, u2, u1, zu)

    acc = pl.BlockSpec((CONV_ROWS, TN_FF), lambda j, i: (0, j))
    tile = pl.BlockSpec((tm, TN_FF), lambda j, i: (i, j))
    return _call(
        body, name="conv_gate_bwd", grid=(NB_FF, S // tm), in_specs=_ffn_specs(tm, "ji") + [tile],
        out_specs=[tile, tile, acc, acc],
        out_shape=[jax.ShapeDtypeStruct((S, D_FF), F32), jax.ShapeDtypeStruct((S, D_FF), F32),
                   jax.ShapeDtypeStruct((CONV_ROWS, D_FF), F32), jax.ShapeDtypeStruct((CONV_ROWS, D_FF), F32)],
        compiler_params=_params("parallel", "arbitrary"),
    )(z, z, z, z, cwb, cwb, df)


def _conv_transpose(dzc, cwb_half, name):
    S = dzc.shape[0]
    tm = TM
    nb = S // tm

    def body(d_ref, dn_ref, c_ref, o_ref):
        last = pl.program_id(0) == nb - 1
        cur = d_ref[...]
        ext = jnp.concatenate([cur, jnp.where(last, 0.0, dn_ref[...])], axis=0)
        c = c_ref[...]
        o_ref[...] = (cur * c[2:3] + _shift_rows(ext, -1, tm, 0) * c[1:2]
                      + _shift_rows(ext, -2, tm, 0) * c[0:1]).astype(o_ref.dtype)

    hb = tm // HALO
    return _call(
        body, name=name, grid=(nb, NB_FF),
        in_specs=[pl.BlockSpec((tm, TN_FF), lambda i, j: (i, j)),
                  pl.BlockSpec((HALO, TN_FF), lambda i, j: (jnp.minimum((i + 1) * hb, S // HALO - 1), j)),
                  pl.BlockSpec((CONV_ROWS, TN_FF), lambda i, j: (0, j))],
        out_specs=pl.BlockSpec((tm, TN_FF), lambda i, j: (i, j)),
        out_shape=jax.ShapeDtypeStruct((S, D_FF), MXU_DT),
        compiler_params=_params("parallel", "parallel"),
    )(dzc, dzc, cwb_half)


def _layer_consts(w, l):
    wm = _tril_weights(w["sgu_w"][l])
    eye = jnp.eye(4, dtype=F32)
    wbd = (w["pool_w"][l][:, :, None, :] * eye[:, None, :, None]).reshape(W_C, W_C)
    cwb = jnp.concatenate([w["conv_w"][l], w["conv_b"][l][None], jnp.zeros((CONV_ROWS - 4, 2 * D_FF), F32)], axis=0)
    return dict(
        g1=w["norm1_g"][l][None], g2=w["norm2_g"][l][None], gm=w["mix_norm_g"][l][None],
        sng=w["sgu_norm_g"][l][None], wm=wm.astype(MXU_DT), wmt=jnp.swapaxes(wm, 1, 2).astype(MXU_DT),
        bias=jnp.repeat(jnp.transpose(w["sgu_b"][l]), HEAD_DIM, axis=1),
        wbd=wbd.astype(MXU_DT), scale=w["pool_scale"][l][None], cwb=cwb,
        w_in=w["w_in"][l], w_o=w["w_o"][l], w_up=w["w_up"][l], w_down=w["w_down"][l],
    )


def _local_step(x, tgt, w):
    gmat = _group_matrix()
    tri = _tri_matrix()
    trit = jnp.transpose(tri)
    saved = []
    for l in range(DEPTH):
        c = _layer_consts(w, l)
        proj, proj_b, h1 = _rms_mm(x, c["g1"], c["w_in"], IN_COLS // 3, "in_proj")
        ya = _sgu_fwd(proj, c["sng"], c["wm"], c["bias"], gmat)
        yb, rb = _sb_fwd(proj_b, tri)
        yc = _pool_fwd(proj, c["wbd"], c["scale"])
        x2, yn = _mix_out(ya, yb, yc, c["gm"], c["w_o"], x, gmat)
        z, _, h2 = _rms_mm(x2, c["g2"], c["w_up"], TN_FF, "up_proj")
        f = _conv_gate(z, c["cwb"])
        x3 = _mm_res(f, c["w_down"], x2, "down_proj")
        saved.append(dict(c=c, x=x, proj=proj, proj_b=proj_b, h1=h1, ya=ya, yb=yb, yc=yc, rb=rb, x2=x2, yn=yn,
                          z=z, h2=h2, f=f))
        x = x3

    dx, d_final_g, loss8 = _loss_head(x, w["final_g"][None], tgt)
    grads = {n: [None] * DEPTH for n in ("norm1_g", "w_in", "sgu_norm_g", "sgu_w", "sgu_b", "pool_w", "pool_scale",
                                         "mix_norm_g", "w_o", "norm2_g", "w_up", "conv_w", "conv_b", "w_down")}
    for l in reversed(range(DEPTH)):
        s = saved[l]
        c = s["c"]
        df = _mm_nt(dx, c["w_down"], "down_proj_bwd")
        grads["w_down"][l] = _mm_tn(s["f"], dx, "down_proj_wgrad")
        dzg, dzu, dcg, dcu = _conv_gate_bwd(s["z"], df, c["cwb"])
        dz_g = _conv_transpose(dzg, c["cwb"][:, :D_FF], "conv_t_gate")
        dz_u = _conv_transpose(dzu, c["cwb"][:, D_FF:], "conv_t_value")
        dcwb = jnp.concatenate([dcg, dcu], axis=1)
        grads["conv_w"][l] = dcwb[:3]
        grads["conv_b"][l] = dcwb[3]
        grads["w_up"][l] = jnp.concatenate([_mm_tn(s["h2"], dz_g, "up_proj_wgrad_gate"),
                                            _mm_tn(s["h2"], dz_u, "up_proj_wgrad_value")], axis=1)
        dx2, dg2 = _mm_nt_rmsbwd([(dz_g, c["w_up"][:, :D_FF]), (dz_u, c["w_up"][:, D_FF:])],
                                 s["x2"], c["g2"], dx, "up_proj_bwd")
        grads["norm2_g"][l] = dg2[0]
        grads["w_o"][l] = _mm_tn(s["yn"], dx2, "out_proj_wgrad")
        dya, dyb, dyc, dgm = _mix_out_bwd(dx2, c["w_o"], s["ya"], s["yb"], s["yc"], c["gm"], gmat)
        grads["mix_norm_g"][l] = dgm[0]
        dd, e, dwbd, dscale = _pool_bwd_a(s["proj"], dyc, c["wbd"], c["scale"])
        dp = _pool_bwd_b(dd, e)
        grads["pool_w"][l] = jnp.stack([dwbd[g * 64:(g + 1) * 64, g * 64:(g + 1) * 64] for g in range(4)])
        grads["pool_scale"][l] = dscale[0]
        dq, dk, dv = _sb_bwd(s["proj_b"], dyb, s["rb"], tri, trit)
        da, dwm, dbias, dsng = _sgu_bwd(s["proj"], dya, c["sng"], c["wm"], c["wmt"], c["bias"], gmat)
        grads["sgu_w"][l] = dwm
        grads["sgu_b"][l] = jnp.transpose(jnp.sum(dbias.reshape(CHUNK, 4, HEAD_DIM), axis=-1))
        grads["sgu_norm_g"][l] = dsng[0]
        dproj = jnp.concatenate([da, dq, dk, dv, dp], axis=1)
        grads["w_in"][l] = _mm_tn(s["h1"], dproj, "in_proj_wgrad")
        dx, dg1 = _mm_nt_rmsbwd([(dproj, c["w_in"])], s["x"], c["g1"], dx2, "in_proj_bwd")
        grads["norm1_g"][l] = dg1[0]

    out = {n: jnp.stack(v) for n, v in grads.items()}
    out["final_g"] = d_final_g[0]
    return loss8[0, 0], dx, out


MESH = pl.DeviceIdType.MESH
ANY = pl.BlockSpec(memory_space=pl.ANY)


def _all_gather(x_half):
    m_per, n = x_half.shape

    def body(x_ref, out_ref, send_sems, recv_sems, local_sem):
        x, y, c = lax.axis_index("x"), lax.axis_index("y"), lax.axis_index("c")
        me, sibling = (x, y, c), (x, y, 1 - c)
        chips = [(1 - x, y), (x, 1 - y), (1 - x, 1 - y)]

        def rows(px, py, pc):
            return out_ref.at[pl.ds((4 * px + 2 * py + pc) * m_per, m_per), :]

        def copy(k, block, to, src=None):
            return pltpu.make_async_remote_copy(
                src_ref=rows(*block) if src is None else src, dst_ref=rows(*block),
                send_sem=send_sems.at[k], recv_sem=recv_sems.at[k], device_id=to, device_id_type=MESH)

        mine = pltpu.make_async_copy(x_ref, rows(*me), local_sem)
        mine.start()
        first = [copy(0, me, sibling, src=x_ref)]
        first += [copy(1 + j, me, (*chip, c), src=x_ref) for j, chip in enumerate(chips)]
        for cp in first:
            cp.start()
        passed = [copy(4 + j, (*chip, c), sibling) for j, chip in enumerate(chips)]
        for j, chip in enumerate(chips):
            copy(1 + j, (*chip, c), me).wait_recv()
            passed[j].start()
        copy(0, sibling, me).wait_recv()
        for j, chip in enumerate(chips):
            copy(4 + j, (*chip, 1 - c), me).wait_recv()
        for cp in first + passed:
            cp.wait_send()
        mine.wait()

    return _call(
        body, name="weight_all_gather",
        out_shape=jax.ShapeDtypeStruct((8 * m_per, n), x_half.dtype),
        in_specs=[ANY], out_specs=ANY,
        scratch_shapes=[pltpu.SemaphoreType.DMA((7,)), pltpu.SemaphoreType.DMA((7,)), pltpu.SemaphoreType.DMA],
    )(x_half)


def _swap_cores(a, name):
    def body(a_ref, out_ref, send_sem, recv_sem):
        x, y, c = lax.axis_index("x"), lax.axis_index("y"), lax.axis_index("c")
        cp = pltpu.make_async_remote_copy(src_ref=a_ref, dst_ref=out_ref, send_sem=send_sem, recv_sem=recv_sem,
                                          device_id=(x, y, 1 - c), device_id_type=MESH)
        cp.start()
        cp.wait()

    return _call(
        body, name=name, out_shape=jax.ShapeDtypeStruct(a.shape, a.dtype), in_specs=[ANY], out_specs=ANY,
        scratch_shapes=[pltpu.SemaphoreType.DMA, pltpu.SemaphoreType.DMA],
    )(a)


def _exchange_chips(a):
    def body(a_ref, out_ref, send_sems, recv_sems, local_sem):
        x, y, c = lax.axis_index("x"), lax.axis_index("y"), lax.axis_index("c")
        my_chip = 2 * x + y
        chips = [(1 - x, y), (x, 1 - y), (1 - x, 1 - y)]
        mine = pltpu.make_async_copy(a_ref.at[my_chip], out_ref.at[my_chip], local_sem)
        mine.start()
        copies = []
        for k, (px, py) in enumerate(chips):
            copies.append(pltpu.make_async_remote_copy(
                src_ref=a_ref.at[2 * px + py], dst_ref=out_ref.at[my_chip],
                send_sem=send_sems.at[k], recv_sem=recv_sems.at[k], device_id=(px, py, c), device_id_type=MESH))
        for cp in copies:
            cp.start()
        for k, (px, py) in enumerate(chips):
            pltpu.make_async_remote_copy(
                src_ref=a_ref.at[my_chip], dst_ref=out_ref.at[2 * px + py],
                send_sem=send_sems.at[k], recv_sem=recv_sems.at[k], device_id=(px, py, c),
                device_id_type=MESH).wait_recv()
        for cp in copies:
            cp.wait_send()
        mine.wait()

    return _call(
        body, name="grad_exchange_chips", out_shape=jax.ShapeDtypeStruct(a.shape, a.dtype),
        in_specs=[ANY], out_specs=ANY,
        scratch_shapes=[pltpu.SemaphoreType.DMA((3,)), pltpu.SemaphoreType.DMA((3,)), pltpu.SemaphoreType.DMA],
    )(a)


def _add2(a, b, name):
    n, r, c = a.shape
    tr = 512
    spec = pl.BlockSpec((1, tr, c), lambda k, i: (k, i, 0))

    def body(a_ref, b_ref, o_ref):
        o_ref[...] = a_ref[...] + b_ref[...]

    return _call(body, name=name, grid=(n, r // tr), in_specs=[spec, spec], out_specs=spec,
                 out_shape=jax.ShapeDtypeStruct(a.shape, a.dtype), compiler_params=_params("parallel", "parallel"))(a, b)


def _sum_chips(a):
    _, r, c = a.shape
    tr = 512

    def body(a_ref, o_ref):
        o_ref[...] = ((a_ref[0] + a_ref[1]) + a_ref[2]) + a_ref[3]

    return _call(body, name="grad_sum_chips", grid=(r // tr,),
                 in_specs=[pl.BlockSpec((N_CHIPS, tr, c), lambda i: (0, i, 0))],
                 out_specs=pl.BlockSpec((tr, c), lambda i: (i, 0)),
                 out_shape=jax.ShapeDtypeStruct((r, c), a.dtype), compiler_params=_params("parallel"))(a)


def _adamw_math(g_ref, w_ref, m_ref, v_ref, d_ref, nm_ref, nv_ref):
    gv = g_ref[...]
    nm = ADAM_B1 * m_ref[...] + (1.0 - ADAM_B1) * gv
    nv = ADAM_B2 * v_ref[...] + (1.0 - ADAM_B2) * (gv * gv)
    m_hat = nm / (1.0 - ADAM_B1 ** ADAM_STEP)
    v_hat = nv / (1.0 - ADAM_B2 ** ADAM_STEP)
    d_ref[...] = -ADAM_LR * (m_hat / (jnp.sqrt(v_hat) + ADAM_EPS) + ADAM_WD * w_ref[...])
    nm_ref[...] = nm
    nv_ref[...] = nv


def _adamw_big(g, w, m, v, name):
    d, r, c = g.shape
    tr = r if r <= 704 else 256
    spec = pl.BlockSpec((1, tr, c), lambda l, i: (l, i, 0))

    def body(*refs):
        _adamw_math(*refs)

    shp = jax.ShapeDtypeStruct(g.shape, F32)
    return _call(body, name=name, grid=(d, r // tr), in_specs=[spec] * 4, out_specs=[spec] * 3,
                 out_shape=[shp, shp, shp], compiler_params=_params("parallel", "parallel"))(g, w, m, v)


def _adamw_small(gs, ws, ms, vs):
    n = len(gs)

    def body(*refs):
        ins, outs = refs[:4 * n], refs[4 * n:]
        for k in range(n):
            _adamw_math(ins[k], ins[n + k], ins[2 * n + k], ins[3 * n + k], outs[k], outs[n + k], outs[2 * n + k])

    shp = [jax.ShapeDtypeStruct(g.shape, F32) for g in gs]
    res = _call(body, name="adamw_small", out_shape=shp * 3)(*gs, *ws, *ms, *vs)
    return res[:n], res[n:2 * n], res[2 * n:]


def _rows(a, rows):
    flat = a.reshape(-1)
    return jnp.pad(flat, (0, rows * D_MODEL - flat.shape[0])).reshape(rows, D_MODEL)


def _small_rows(p, extra=None):
    parts = [p[n].reshape(-1) for n in SMALL_NAMES]
    if extra is not None:
        parts.append(extra.reshape(-1))
    flat = jnp.concatenate(parts)
    return jnp.pad(flat, (0, ROWS_SMALL * D_MODEL - flat.shape[0])).reshape(ROWS_SMALL, D_MODEL)


def _unpack_shard(pack):
    out = {}
    o = 0
    for name, rows, shape in (("w_in", ROWS_IN, (DEPTH, D_MODEL, IN_COLS // N_CHIPS)),
                              ("w_o", ROWS_O, (DEPTH, D_MODEL // N_CHIPS, D_MODEL)),
                              ("w_up", ROWS_UP, (DEPTH, D_MODEL, 2 * D_FF // N_CHIPS)),
                              ("w_down", ROWS_DOWN, (DEPTH, D_FF // N_CHIPS, D_MODEL)),
                              ("conv_w", ROWS_CONV, (DEPTH, 3, 2 * D_FF // N_CHIPS))):
        n = 1
        for d in shape:
            n *= d
        out[name] = pack[o:o + rows].reshape(-1)[:n].reshape(shape)
        o += rows
    flat = pack[o:].reshape(-1)
    k = 0
    for name in SMALL_NAMES:
        shape = SMALL_SHAPES[name]
        n = 1
        for d in shape:
            n *= d
        out[name] = flat[k:k + n].reshape(shape)
        k += n
    out["extra"] = flat[k]
    return out


def _pack_grads_by_chip(g, loss):
    def cols(a, width):
        d, r, _ = a.shape
        return jnp.transpose(a.reshape(d, r, N_CHIPS, width), (2, 0, 1, 3))

    def rws(a, height):
        d, _, c = a.shape
        return jnp.transpose(a.reshape(d, N_CHIPS, height, c), (1, 0, 2, 3))

    per_chip = dict(w_in=cols(g["w_in"], IN_COLS // N_CHIPS), w_o=rws(g["w_o"], D_MODEL // N_CHIPS),
                    w_up=cols(g["w_up"], 2 * D_FF // N_CHIPS), w_down=rws(g["w_down"], D_FF // N_CHIPS),
                    conv_w=cols(g["conv_w"], 2 * D_FF // N_CHIPS))
    small = _small_rows(g, loss)
    packs = []
    for k in range(N_CHIPS):
        packs.append(jnp.concatenate([
            _rows(per_chip["w_in"][k], ROWS_IN), _rows(per_chip["w_o"][k], ROWS_O), _rows(per_chip["w_up"][k], ROWS_UP),
            _rows(per_chip["w_down"][k], ROWS_DOWN), _rows(per_chip["conv_w"][k], ROWS_CONV), small], axis=0))
    return jnp.stack(packs)


def _gather_weights(p, c):
    conv_bits = lax.bitcast_convert_type(_rows(p["conv_w"], ROWS_CONV), jnp.bfloat16)
    conv_bits = conv_bits.reshape(2 * ROWS_CONV, D_MODEL)
    bf = jnp.bfloat16
    block = jnp.concatenate([_rows(p["w_in"].astype(bf), ROWS_IN), _rows(p["w_o"].astype(bf), ROWS_O),
                             _rows(p["w_up"].astype(bf), ROWS_UP), _rows(p["w_down"].astype(bf), ROWS_DOWN),
                             conv_bits], axis=0)
    half = ROWS_GATHER // 2
    mine = lax.dynamic_slice_in_dim(block, c * half, half, axis=0)
    allw = _all_gather(mine).reshape(N_CHIPS, ROWS_GATHER, D_MODEL)
    o = 0

    def take(rows, shape):
        nonlocal o
        a = allw[:, o:o + rows].reshape((N_CHIPS,) + shape)
        o += rows
        return a

    w_in = take(ROWS_IN, (DEPTH, D_MODEL, IN_COLS // N_CHIPS))
    w_o = take(ROWS_O, (DEPTH, D_MODEL // N_CHIPS, D_MODEL))
    w_up = take(ROWS_UP, (DEPTH, D_MODEL, 2 * D_FF // N_CHIPS))
    w_down = take(ROWS_DOWN, (DEPTH, D_FF // N_CHIPS, D_MODEL))
    conv = lax.bitcast_convert_type(allw[:, o:o + 2 * ROWS_CONV].reshape(N_CHIPS, ROWS_CONV, D_MODEL, 2), F32)
    conv = conv.reshape(N_CHIPS, -1)[:, :DEPTH * 3 * (2 * D_FF // N_CHIPS)].reshape(N_CHIPS, DEPTH, 3, 2 * D_FF // N_CHIPS)

    def by_cols(a):
        k, d, r, wd = a.shape
        return jnp.transpose(a, (1, 2, 0, 3)).reshape(d, r, k * wd)

    def by_rows(a):
        k, d, hgt, cc = a.shape
        return jnp.transpose(a, (1, 0, 2, 3)).reshape(d, k * hgt, cc)

    return dict(w_in=by_cols(w_in), w_o=by_rows(w_o), w_up=by_cols(w_up), w_down=by_rows(w_down), conv_w=by_cols(conv))


def _reduce_grads(g_pack, c):
    half = ROWS_PACK // 2
    keep = lax.dynamic_slice_in_dim(g_pack, c * half, half, axis=1)
    give = lax.dynamic_slice_in_dim(g_pack, (1 - c) * half, half, axis=1)
    got = _swap_cores(give, "grad_swap_cores")
    pair = _add2(keep, got, "grad_add_cores")
    total = _sum_chips(_exchange_chips(pair))
    other = _swap_cores(total, "grad_share_cores")
    lo = jnp.where(c == 0, total, other)
    hi = jnp.where(c == 0, other, total)
    return jnp.concatenate([lo, hi], axis=0)


def kernel(x, norm1_g, w_in, sgu_norm_g, sgu_w, sgu_b, pool_w, pool_scale, mix_norm_g, w_o, norm2_g, w_up, conv_w, conv_b, w_down, final_g, loss_target, m_norm1_g, m_w_in, m_sgu_norm_g, m_sgu_w, m_sgu_b, m_pool_w, m_pool_scale, m_mix_norm_g, m_w_o, m_norm2_g, m_w_up, m_conv_w, m_conv_b, m_w_down, m_final_g, v_norm1_g, v_w_in, v_sgu_norm_g, v_sgu_w, v_sgu_b, v_pool_w, v_pool_scale, v_mix_norm_g, v_w_o, v_norm2_g, v_w_up, v_conv_w, v_conv_b, v_w_down, v_final_g):
    names = ("norm1_g", "w_in", "sgu_norm_g", "sgu_w", "sgu_b", "pool_w", "pool_scale", "mix_norm_g", "w_o",
             "norm2_g", "w_up", "conv_w", "conv_b", "w_down", "final_g")
    p = dict(zip(names, (norm1_g, w_in, sgu_norm_g, sgu_w, sgu_b, pool_w, pool_scale, mix_norm_g, w_o, norm2_g,
                         w_up, conv_w, conv_b, w_down, final_g)))
    pm = dict(zip(names, (m_norm1_g, m_w_in, m_sgu_norm_g, m_sgu_w, m_sgu_b, m_pool_w, m_pool_scale, m_mix_norm_g,
                          m_w_o, m_norm2_g, m_w_up, m_conv_w, m_conv_b, m_w_down, m_final_g)))
    pv = dict(zip(names, (v_norm1_g, v_w_in, v_sgu_norm_g, v_sgu_w, v_sgu_b, v_pool_w, v_pool_scale, v_mix_norm_g,
                          v_w_o, v_norm2_g, v_w_up, v_conv_w, v_conv_b, v_w_down, v_final_g)))
    c = lax.axis_index("c")
    full = dict(p)
    full.update(_gather_weights(p, c))

    loss, dx, grads = _local_step(x[0], loss_target[0], full)

    g = _unpack_shard(_reduce_grads(_pack_grads_by_chip(grads, loss), c))
    d, nm, nv = {}, {}, {}
    for n in BIG_NAMES:
        d[n], nm[n], nv[n] = _adamw_big(g[n], p[n], pm[n], pv[n], "adamw_" + n)

    def two_d(a):
        return a.reshape(1, -1) if a.ndim == 1 else a

    ds, ms, vs = _adamw_small([two_d(g[n]) for n in SMALL_NAMES], [two_d(p[n]) for n in SMALL_NAMES],
                              [two_d(pm[n]) for n in SMALL_NAMES], [two_d(pv[n]) for n in SMALL_NAMES])
    for k, n in enumerate(SMALL_NAMES):
        d[n], nm[n], nv[n] = (a.reshape(p[n].shape) for a in (ds[k], ms[k], vs[k]))
    return (g["extra"], dx[None], *[g[n] for n in names], *[d[n] for n in names], *[nm[n] for n in names],
            *[nv[n] for n in names])
```

```python
import functools

import jax
import jax.numpy as jnp
from jax import lax
from jax.experimental import pallas as pl
from jax.experimental.pallas import tpu as pltpu

F32 = jnp.float32
MXU_DT = jnp.bfloat16

D_MODEL = 1024
DEPTH = 2
HEAD_DIM = 64
W_A = 256
W_B = 512
W_C = 256
IN_COLS = 2 * W_A + 3 * W_B + W_C
CHUNK = 128
POOL_WINDOWS = (2, 4, 8, 16)
D_FF = 2816
EPS = 1e-6
N_CHIPS = 4

ADAM_LR = 0.001
ADAM_B1 = 0.9
ADAM_B2 = 0.999
ADAM_EPS = 1e-08
ADAM_WD = 0.01
ADAM_STEP = 10

LANES = 128
TQ = 512
TK = 256
TM = 256
TM_MM = 512
HALO = 16
VMEM_LIMIT = 56 * 1024 * 1024

ROWS_IN = DEPTH * D_MODEL * (IN_COLS // N_CHIPS) // D_MODEL
ROWS_O = DEPTH * (D_MODEL // N_CHIPS)
ROWS_UP = DEPTH * D_MODEL * (2 * D_FF // N_CHIPS) // D_MODEL
ROWS_DOWN = DEPTH * (D_FF // N_CHIPS)
ROWS_BIG = ROWS_IN + ROWS_O + ROWS_UP + ROWS_DOWN
ROWS_CONV = 16
ROWS_SMALL = 240
ROWS_PACK = ROWS_BIG + ROWS_CONV + ROWS_SMALL
ROWS_GATHER = ROWS_BIG + 2 * ROWS_CONV

BIG_NAMES = ("w_in", "w_o", "w_up", "w_down", "conv_w")
SMALL_NAMES = ("norm1_g", "sgu_norm_g", "sgu_w", "sgu_b", "pool_w", "pool_scale",
               "mix_norm_g", "norm2_g", "conv_b", "final_g")
SMALL_SHAPES = {
    "norm1_g": (DEPTH, D_MODEL), "sgu_norm_g": (DEPTH, W_A), "sgu_w": (DEPTH, 4, CHUNK, CHUNK),
    "sgu_b": (DEPTH, 4, CHUNK), "pool_w": (DEPTH, 4, 64, 64), "pool_scale": (DEPTH, W_C),
    "mix_norm_g": (DEPTH, D_MODEL), "norm2_g": (DEPTH, D_MODEL), "conv_b": (DEPTH, 2 * D_FF),
    "final_g": (D_MODEL,),
}


def _call(body, **kw):
    return pl.pallas_call(body, **kw)


def _params(*sem):
    return pltpu.CompilerParams(dimension_semantics=sem, vmem_limit_bytes=VMEM_LIMIT)


def _dot(a, b):
    return jnp.dot(a, b, preferred_element_type=F32)


def _dot_nt(a, b):
    return lax.dot_general(a, b, (((1,), (1,)), ((), ())), preferred_element_type=F32)


def _dot_tn(a, b):
    return lax.dot_general(a, b, (((0,), (0,)), ((), ())), preferred_element_type=F32)


def _split(a):
    hi = a.astype(MXU_DT)
    lo = (a - hi.astype(F32)).astype(MXU_DT)
    return hi, lo


def _dot_split(a, b):
    hi, lo = _split(a)
    return _dot(hi, b) + _dot(lo, b)


def _group_mean(sq, gmat):
    cols = [_dot_split(sq[:, b * LANES:(b + 1) * LANES], gmat) for b in range(sq.shape[1] // LANES)]
    return cols[0] if len(cols) == 1 else jnp.concatenate(cols, axis=-1)


def _group_matrix():
    r = jnp.arange(LANES)
    return jnp.where((r[:, None] // HEAD_DIM) == (r[None, :] // HEAD_DIM), 1.0 / HEAD_DIM, 0.0).astype(MXU_DT)


def _tile(n):
    return max(t for t in range(LANES, 1536 + 1, LANES) if n % t == 0)


def _row_spec(tm, cols, col_block=0):
    return pl.BlockSpec((tm, cols), lambda i, cb=col_block: (i, cb))


def _full_spec(shape):
    nd = len(shape)
    return pl.BlockSpec(shape, lambda *_: (0,) * nd)


def _rms_mm(x, g, w, tn, name, out_dtypes):
    S, D = x.shape
    N = w.shape[1]
    tm = TM_MM

    def body(x_ref, g_ref, w_ref, h_ref, *o_refs):
        @pl.when(pl.program_id(1) == 0)
        def _():
            xv = x_ref[...]
            r = lax.rsqrt(jnp.mean(xv * xv, axis=-1, keepdims=True) + EPS)
            h_ref[...] = (xv * r * g_ref[...]).astype(h_ref.dtype)

        acc = _dot(h_ref[...], w_ref[...])
        for o_ref in o_refs:
            o_ref[...] = acc.astype(o_ref.dtype)

    return _call(
        body, name=name, grid=(S // tm, N // tn),
        in_specs=[pl.BlockSpec((tm, D), lambda i, j: (i, 0)),
                  pl.BlockSpec((1, D), lambda i, j: (0, 0)),
                  pl.BlockSpec((D, tn), lambda i, j: (0, j))],
        out_specs=[pl.BlockSpec((tm, D), lambda i, j: (i, 0))]
        + [pl.BlockSpec((tm, tn), lambda i, j: (i, j)) for _ in out_dtypes],
        out_shape=[jax.ShapeDtypeStruct((S, D), MXU_DT)] + [jax.ShapeDtypeStruct((S, N), dt) for dt in out_dtypes],
        compiler_params=_params("parallel", "arbitrary"),
    )(x, g, w)


def _mm_res(a, w, res, name):
    S, K = a.shape
    N = w.shape[1]
    tm = TM_MM

    def body(a_ref, w_ref, r_ref, o_ref):
        o_ref[...] = r_ref[...] + _dot(a_ref[...], w_ref[...])

    return _call(
        body, name=name, grid=(S // tm,),
        in_specs=[_row_spec(tm, K), _full_spec((K, N)), _row_spec(tm, N)],
        out_specs=_row_spec(tm, N),
        out_shape=jax.ShapeDtypeStruct((S, N), F32),
        compiler_params=_params("parallel"),
    )(a, w, res)


def _mm_nt(a, w, name):
    S, K = a.shape
    N = w.shape[0]
    tm = TM_MM

    def body(a_ref, w_ref, o_ref):
        o_ref[...] = _dot_nt(a_ref[...].astype(MXU_DT), w_ref[...])

    return _call(
        body, name=name, grid=(S // tm,),
        in_specs=[_row_spec(tm, K), _full_spec((N, K))],
        out_specs=_row_spec(tm, N),
        out_shape=jax.ShapeDtypeStruct((S, N), F32),
        compiler_params=_params("parallel"),
    )(a, w)


def _mm_tn(a, b, name):
    S, K1 = a.shape
    N = b.shape[1]
    ts = TM_MM
    tk = _tile(K1)
    tn = _tile(N)

    def body(a_ref, b_ref, o_ref):
        @pl.when(pl.program_id(2) == 0)
        def _():
            o_ref[...] = jnp.zeros_like(o_ref)

        o_ref[...] += _dot_tn(a_ref[...], b_ref[...].astype(MXU_DT))

    return _call(
        body, name=name, grid=(K1 // tk, N // tn, S // ts),
        in_specs=[pl.BlockSpec((ts, tk), lambda m, n, s: (s, m)),
                  pl.BlockSpec((ts, tn), lambda m, n, s: (s, n))],
        out_specs=pl.BlockSpec((tk, tn), lambda m, n, s: (m, n)),
        out_shape=jax.ShapeDtypeStruct((K1, N), F32),
        compiler_params=_params("parallel", "parallel", "arbitrary"),
    )(a, b)


def _mm_nt_rmsbwd(pairs, x, g, dres, name):
    S, D = x.shape
    tm = TM
    n = len(pairs)

    def body(*refs):
        a_refs = refs[:n]
        w_refs = refs[n:2 * n]
        x_ref, g_ref, r_ref, dx_ref, dg_ref = refs[2 * n:]
        dh = _dot_nt(a_refs[0][...], w_refs[0][...])
        for k in range(1, n):
            dh += _dot_nt(a_refs[k][...], w_refs[k][...])
        xv = x_ref[...]
        r = lax.rsqrt(jnp.mean(xv * xv, axis=-1, keepdims=True) + EPS)
        xhat = xv * r

        @pl.when(pl.program_id(0) == 0)
        def _():
            dg_ref[...] = jnp.zeros_like(dg_ref)

        dg_ref[...] += jnp.sum(dh * xhat, axis=0, keepdims=True)
        dxh = dh * g_ref[...]
        dx_ref[...] = r_ref[...] + r * (dxh - xhat * jnp.mean(dxh * xhat, axis=-1, keepdims=True))

    in_specs = ([_row_spec(tm, a.shape[1]) for a, _ in pairs] + [_full_spec(w.shape) for _, w in pairs]
                + [_row_spec(tm, D), _full_spec((1, D)), _row_spec(tm, D)])
    return _call(
        body, name=name, grid=(S // tm,), in_specs=in_specs,
        out_specs=[_row_spec(tm, D), _full_spec((1, D))],
        out_shape=[jax.ShapeDtypeStruct((S, D), F32), jax.ShapeDtypeStruct((1, D), F32)],
        compiler_params=_params("arbitrary"),
    )(*[a for a, _ in pairs], *[w for _, w in pairs], x, g, dres)


def _loss_head(x, g, tgt):
    S, D = x.shape
    tm = TM

    def body(x_ref, g_ref, t_ref, dx_ref, dg_ref, l_ref):
        xv = x_ref[...]
        r = lax.rsqrt(jnp.mean(xv * xv, axis=-1, keepdims=True) + EPS)
        xhat = xv * r
        diff = xhat * g_ref[...] - t_ref[...]

        @pl.when(pl.program_id(0) == 0)
        def _():
            dg_ref[...] = jnp.zeros_like(dg_ref)
            l_ref[...] = jnp.zeros_like(l_ref)

        l_ref[...] += jnp.full(l_ref.shape, 0.5 * jnp.sum(jnp.mean(diff * diff, axis=-1, keepdims=True)), F32)
        dout = diff * (1.0 / D)
        dg_ref[...] += jnp.sum(dout * xhat, axis=0, keepdims=True)
        dxh = dout * g_ref[...]
        dx_ref[...] = r * (dxh - xhat * jnp.mean(dxh * xhat, axis=-1, keepdims=True))

    return _call(
        body, name="loss_head", grid=(S // tm,),
        in_specs=[_row_spec(tm, D), _full_spec((1, D)), _row_spec(tm, D)],
        out_specs=[_row_spec(tm, D), _full_spec((1, D)), _full_spec((8, LANES))],
        out_shape=[jax.ShapeDtypeStruct((S, D), F32), jax.ShapeDtypeStruct((1, D), F32),
                   jax.ShapeDtypeStruct((8, LANES), F32)],
        compiler_params=_params("arbitrary"),
    )(x, g, tgt)


def _mix_out(ya, yb, yc, gm, wo, x, gmat):
    S = x.shape[0]
    tm = TM

    def body(ya_ref, yb_ref, yc_ref, gm_ref, wo_ref, x_ref, gmat_ref, x2_ref, yn_ref):
        y = jnp.concatenate([ya_ref[...], yb_ref[...], yc_ref[...]], axis=-1)
        r = lax.rsqrt(_group_mean(y * y, gmat_ref[...]) + EPS)
        yn = (y * r * gm_ref[...]).astype(MXU_DT)
        yn_ref[...] = yn
        x2_ref[...] = x_ref[...] + _dot(yn, wo_ref[...])

    return _call(
        body, name="mix_out", grid=(S // tm,),
        in_specs=[_row_spec(tm, W_A), _row_spec(tm, W_B), _row_spec(tm, W_C), _full_spec((1, D_MODEL)),
                  _full_spec((D_MODEL, D_MODEL)), _row_spec(tm, D_MODEL), _full_spec((LANES, LANES))],
        out_specs=[_row_spec(tm, D_MODEL), _row_spec(tm, D_MODEL)],
        out_shape=[jax.ShapeDtypeStruct((S, D_MODEL), F32), jax.ShapeDtypeStruct((S, D_MODEL), MXU_DT)],
        compiler_params=_params("parallel"),
    )(ya, yb, yc, gm, wo, x, gmat)


def _mix_out_bwd(dx2, wo, ya, yb, yc, gm, gmat):
    S = dx2.shape[0]
    tm = TM

    def body(dx2_ref, wo_ref, ya_ref, yb_ref, yc_ref, gm_ref, gmat_ref, dya_ref, dyb_ref, dyc_ref, dgm_ref):
        dyn = _dot_nt(dx2_ref[...].astype(MXU_DT), wo_ref[...])
        y = jnp.concatenate([ya_ref[...], yb_ref[...], yc_ref[...]], axis=-1)
        r = lax.rsqrt(_group_mean(y * y, gmat_ref[...]) + EPS)
        yhat = y * r

        @pl.when(pl.program_id(0) == 0)
        def _():
            dgm_ref[...] = jnp.zeros_like(dgm_ref)

        dgm_ref[...] += jnp.sum(dyn * yhat, axis=0, keepdims=True)
        dyh = dyn * gm_ref[...]
        dy = r * (dyh - yhat * _group_mean(dyh * yhat, gmat_ref[...]))
        dya_ref[...] = dy[:, :W_A]
        dyb_ref[...] = dy[:, W_A:W_A + W_B]
        dyc_ref[...] = dy[:, W_A + W_B:]

    return _call(
        body, name="mix_out_bwd", grid=(S // tm,),
        in_specs=[_row_spec(tm, D_MODEL), _full_spec((D_MODEL, D_MODEL)), _row_spec(tm, W_A), _row_spec(tm, W_B),
                  _row_spec(tm, W_C), _full_spec((1, D_MODEL)), _full_spec((LANES, LANES))],
        out_specs=[_row_spec(tm, W_A), _row_spec(tm, W_B), _row_spec(tm, W_C), _full_spec((1, D_MODEL))],
        out_shape=[jax.ShapeDtypeStruct((S, W_A), F32), jax.ShapeDtypeStruct((S, W_B), F32),
                   jax.ShapeDtypeStruct((S, W_C), F32), jax.ShapeDtypeStruct((1, D_MODEL), F32)],
        compiler_params=_params("arbitrary"),
    )(dx2, wo, ya, yb, yc, gm, gmat)


_SQRT_HALF = 0.7071067811865476
_INV_SQRT_2PI = 0.3989422804014327


def _sgu_common(a, sng, wm_ref, bias, gmat):
    phi = 0.5 * (1.0 + lax.erf(a * _SQRT_HALF))
    ga = a * phi
    u = ga[:, :W_A]
    v = ga[:, W_A:]
    r = lax.rsqrt(_group_mean(v * v, gmat) + EPS)
    vhat = v * r
    vn = (vhat * sng).astype(MXU_DT)
    head = lax.broadcasted_iota(jnp.int32, (CHUNK, W_A), 1) // HEAD_DIM
    rows = []
    for c in range(a.shape[0] // CHUNK):
        vc = vn[c * CHUNK:(c + 1) * CHUNK]
        s = bias
        for h in range(4):
            s = s + jnp.where(head == h, _dot(wm_ref[h], vc), 0.0)
        rows.append(s)
    s = jnp.concatenate(rows, axis=0)
    return phi, u, r, vhat, vn, s


def _tril_weights(sgu_w_l):
    t = jnp.arange(CHUNK)
    return jnp.where((t[None, :] <= t[:, None])[None], sgu_w_l, 0.0)


def _sgu_fwd(proj, sng, wm, bias, gmat):
    S = proj.shape[0]
    tm = TM

    def body(a_ref, sng_ref, wm_ref, b_ref, gmat_ref, y_ref):
        _, u, _, _, _, s = _sgu_common(a_ref[...], sng_ref[...], wm_ref, b_ref[...], gmat_ref[...])
        y_ref[...] = u * s

    return _call(
        body, name="sgu_fwd", grid=(S // tm,),
        in_specs=[_row_spec(tm, 2 * W_A), _full_spec((1, W_A)), _full_spec((4, CHUNK, CHUNK)),
                  _full_spec((CHUNK, W_A)), _full_spec((LANES, LANES))],
        out_specs=_row_spec(tm, W_A),
        out_shape=jax.ShapeDtypeStruct((S, W_A), F32),
        compiler_params=_params("parallel"),
    )(proj, sng, wm, bias, gmat)


def _sgu_bwd(proj, dy, sng, wm, wmt, bias, gmat):
    S = proj.shape[0]
    tm = TM

    def body(a_ref, dy_ref, sng_ref, wm_ref, wmt_ref, b_ref, gmat_ref, da_ref, dw_ref, db_ref, dsng_ref):
        a = a_ref[...]
        dy = dy_ref[...]
        gmat = gmat_ref[...]
        sng = sng_ref[...]
        phi, u, r, vhat, vn, s = _sgu_common(a, sng, wm_ref, b_ref[...], gmat)
        du = dy * s
        ds = dy * u

        @pl.when(pl.program_id(0) == 0)
        def _():
            dw_ref[...] = jnp.zeros_like(dw_ref)
            db_ref[...] = jnp.zeros_like(db_ref)
            dsng_ref[...] = jnp.zeros_like(dsng_ref)

        head = lax.broadcasted_iota(jnp.int32, (CHUNK, W_A), 1) // HEAD_DIM
        tt = lax.broadcasted_iota(jnp.int32, (CHUNK, CHUNK), 0)
        ss = lax.broadcasted_iota(jnp.int32, (CHUNK, CHUNK), 1)
        rows = []
        for c in range(tm // CHUNK):
            dsc = ds[c * CHUNK:(c + 1) * CHUNK]
            vc = vn[c * CHUNK:(c + 1) * CHUNK]
            db_ref[...] += dsc
            dsb = dsc.astype(MXU_DT)
            dvn = jnp.zeros((CHUNK, W_A), F32)
            for h in range(4):
                dvn = dvn + jnp.where(head == h, _dot(wmt_ref[h], dsb), 0.0)
                dsh = jnp.where(head == h, dsc, 0.0).astype(MXU_DT)
                dw_ref[h] += jnp.where(ss <= tt, _dot_nt(dsh, vc), 0.0)
            rows.append(dvn)
        dvn = jnp.concatenate(rows, axis=0)
        dsng_ref[...] += jnp.sum(dvn * vhat, axis=0, keepdims=True)
        dvh = dvn * sng
        dv = r * (dvh - vhat * _group_mean(dvh * vhat, gmat))
        dga = jnp.concatenate([du, dv], axis=-1)
        dgelu = phi + a * (_INV_SQRT_2PI * jnp.exp(-0.5 * a * a))
        da_ref[...] = (dga * dgelu).astype(da_ref.dtype)

    return _call(
        body, name="sgu_bwd", grid=(S // tm,),
        in_specs=[_row_spec(tm, 2 * W_A), _row_spec(tm, W_A), _full_spec((1, W_A)), _full_spec((4, CHUNK, CHUNK)),
                  _full_spec((4, CHUNK, CHUNK)), _full_spec((CHUNK, W_A)), _full_spec((LANES, LANES))],
        out_specs=[_row_spec(tm, 2 * W_A), _full_spec((4, CHUNK, CHUNK)), _full_spec((CHUNK, W_A)),
                   _full_spec((1, W_A))],
        out_shape=[jax.ShapeDtypeStruct((S, 2 * W_A), MXU_DT), jax.ShapeDtypeStruct((4, CHUNK, CHUNK), F32),
                   jax.ShapeDtypeStruct((CHUNK, W_A), F32), jax.ShapeDtypeStruct((1, W_A), F32)],
        compiler_params=_params("arbitrary"),
    )(proj, dy, sng, wm, wmt, bias, gmat)


Q_BLK0 = (2 * W_A) // LANES
K_BLK0 = Q_BLK0 + W_B // LANES
V_BLK0 = K_BLK0 + W_B // LANES
N_PAIRS = W_B // LANES
EXP_IS_ZERO_BELOW = -120.0


def _tri_matrix():
    r = jnp.arange(TK)
    return (r[:, None] > r[None, :]).astype(MXU_DT)


def _stack_heads(a):
    lane = lax.broadcasted_iota(jnp.int32, a.shape, 1)
    return jnp.concatenate([jnp.where(lane < HEAD_DIM, a, 0.0), jnp.where(lane >= HEAD_DIM, a, 0.0)],
                           axis=0).astype(MXU_DT)


def _unstack_heads(a):
    lane = lax.broadcasted_iota(jnp.int32, (TQ, LANES), 1)
    return jnp.where(lane < HEAD_DIM, a[:TQ], a[TQ:])


def _sb_scores(q2, kj, tri, key_offset):
    z = _dot_nt(q2, kj)
    sp = jnp.log(1.0 + jnp.exp(-jnp.abs(z)))
    lsp = jnp.minimum(z, 0.0) - sp
    lsm = lsp - z
    msk = None
    if key_offset is not None:
        row = lax.broadcasted_iota(jnp.int32, z.shape, 0) & (TQ - 1)
        col = lax.broadcasted_iota(jnp.int32, z.shape, 1) + key_offset
        msk = col < row
        lsm = jnp.where(msk, lsm, 0.0)
    tail = _dot(lsm.astype(MXU_DT), tri)
    return lsp, lsm, tail, msk


def _sb_fwd(proj_b, tri):
    S = proj_b.shape[0]
    nq = S // TQ
    kpq = TQ // TK
    assert S // TK < LANES

    def body(q_ref, k_ref, v_ref, tri_ref, o_ref, rb_ref, acc_ref):
        i = pl.program_id(1)
        lane2 = lax.broadcasted_iota(jnp.int32, (2 * TQ, LANES), 1)
        q2 = _stack_heads(q_ref[...].astype(F32) * (HEAD_DIM ** -0.5))
        tri = tri_ref[...]
        rb_ref[...] = jnp.zeros_like(rb_ref)

        def block(j, run, key_offset=None, first=False):
            start = pl.multiple_of(j * TK, TK)
            kj = k_ref[pl.ds(start, TK), :]
            vj = v_ref[pl.ds(start, TK), :]
            lsp, lsm, tail, msk = _sb_scores(q2, kj, tri, key_offset)
            rb_ref[...] = jnp.where(lane2 == j, run, rb_ref[...])
            att = jnp.exp(lsp + tail + run)
            if msk is not None:
                att = jnp.where(msk, att, 0.0)
            pv = _dot(att.astype(MXU_DT), vj)
            if first:
                acc_ref[...] = pv
            else:
                acc_ref[...] += pv
            return run + tail[:, :1] + lsm[:, :1]

        run = jnp.zeros((2 * TQ, 1), F32)
        for d in reversed(range(kpq)):
            run = block(i * kpq + d, run, key_offset=d * TK, first=(d == kpq - 1))
        past = i * kpq

        def alive(run):
            return (jnp.max(run) > EXP_IS_ZERO_BELOW).astype(jnp.int32)

        def step(carry):
            n, run, _ = carry
            run = block(past - 1 - n, run)
            return n + 1, run, alive(run)

        n, _, _ = lax.while_loop(lambda c: jnp.logical_and(c[0] < past, c[2] > 0), step,
                                 (jnp.int32(0), run, alive(run)))
        rb_ref[...] = jnp.where(lane2 == LANES - 1, n.astype(F32), rb_ref[...])
        o_ref[...] = _unstack_heads(acc_ref[...])

    return _call(
        body, name="sb_fwd", grid=(N_PAIRS, nq),
        in_specs=[pl.BlockSpec((TQ, LANES), lambda p, i: (i, Q_BLK0 + p)),
                  pl.BlockSpec((S, LANES), lambda p, i: (0, K_BLK0 + p)),
                  pl.BlockSpec((S, LANES), lambda p, i: (0, V_BLK0 + p)),
                  pl.BlockSpec((TK, TK), lambda p, i: (0, 0))],
        out_specs=[pl.BlockSpec((TQ, LANES), lambda p, i: (i, p)),
                   pl.BlockSpec((None, None, 2 * TQ, LANES), lambda p, i: (p, i, 0, 0))],
        out_shape=[jax.ShapeDtypeStruct((S, W_B), F32), jax.ShapeDtypeStruct((N_PAIRS, nq, 2 * TQ, LANES), F32)],
        scratch_shapes=[pltpu.VMEM((2 * TQ, LANES), F32)],
        compiler_params=_params("parallel", "arbitrary"),
    )(proj_b, proj_b, proj_b, tri)


def _sb_bwd(proj_b, dyb, rb, tri, trit):
    S = proj_b.shape[0]
    nq = S // TQ

    kpq = TQ // TK

    def body(q_ref, k_ref, v_ref, do_ref, rb_ref, tri_ref, trit_ref, dq_ref, dk_ref, dv_ref,
             dq_acc, dk_acc, dv_acc):
        i = pl.program_id(1)
        lane2 = lax.broadcasted_iota(jnp.int32, (2 * TQ, LANES), 1)
        scale = HEAD_DIM ** -0.5
        q2 = _stack_heads(q_ref[...].astype(F32) * scale)
        do2 = _stack_heads(do_ref[...])
        tri = tri_ref[...]
        trit = trit_ref[...]

        @pl.when(i == 0)
        def _():
            dk_acc[...] = jnp.zeros_like(dk_acc)
            dv_acc[...] = jnp.zeros_like(dv_acc)

        dq_acc[...] = jnp.zeros_like(dq_acc)

        def block(j, pre, key_offset=None):
            start = pl.multiple_of(j * TK, TK)
            kj = k_ref[pl.ds(start, TK), :]
            vj = v_ref[pl.ds(start, TK), :]
            lsp, lsm, tail, msk = _sb_scores(q2, kj, tri, key_offset)
            run = jnp.sum(jnp.where(lane2 == j, rb_ref[...], 0.0), axis=-1, keepdims=True)
            att = jnp.exp(lsp + tail + run)
            if msk is not None:
                att = jnp.where(msk, att, 0.0)
            beta = jnp.exp(lsp)
            dl = _dot_nt(do2, vj) * att
            cin = _dot(dl.astype(MXU_DT), trit)
            dz = dl * (1.0 - beta) - beta * (pre + cin)
            if msk is not None:
                dz = jnp.where(msk, dz, 0.0)
            dzb = dz.astype(MXU_DT)
            dq_acc[...] += _dot(dzb, kj)
            dk_acc[pl.ds(start, TK), :] += _dot_tn(dzb, q2)
            dv_acc[pl.ds(start, TK), :] += _dot_tn(att.astype(MXU_DT), do2)
            return pre + cin[:, TK - 1:] + dl[:, TK - 1:]

        past = i * kpq
        walked = jnp.max(jnp.where(lane2[:8] == LANES - 1, rb_ref[pl.ds(0, 8), :], 0.0)).astype(jnp.int32)
        walked = jnp.clip(walked, 0, past)
        pre = lax.fori_loop(past - walked, past, lambda j, pre: block(j, pre), jnp.zeros((2 * TQ, 1), F32))
        for d in range(kpq):
            pre = block(i * kpq + d, pre, key_offset=d * TK)
        dq_ref[...] = (_unstack_heads(dq_acc[...]) * scale).astype(dq_ref.dtype)

        @pl.when(i == nq - 1)
        def _():
            dk_ref[...] = dk_acc[...].astype(dk_ref.dtype)
            dv_ref[...] = dv_acc[...].astype(dv_ref.dtype)

    return _call(
        body, name="sb_bwd", grid=(N_PAIRS, nq),
        in_specs=[pl.BlockSpec((TQ, LANES), lambda p, i: (i, Q_BLK0 + p)),
                  pl.BlockSpec((S, LANES), lambda p, i: (0, K_BLK0 + p)),
                  pl.BlockSpec((S, LANES), lambda p, i: (0, V_BLK0 + p)),
                  pl.BlockSpec((TQ, LANES), lambda p, i: (i, p)),
                  pl.BlockSpec((None, None, 2 * TQ, LANES), lambda p, i: (p, i, 0, 0)),
                  pl.BlockSpec((TK, TK), lambda p, i: (0, 0)),
                  pl.BlockSpec((TK, TK), lambda p, i: (0, 0))],
        out_specs=[pl.BlockSpec((TQ, LANES), lambda p, i: (i, p)),
                   pl.BlockSpec((S, LANES), lambda p, i: (0, p)),
                   pl.BlockSpec((S, LANES), lambda p, i: (0, p))],
        out_shape=[jax.ShapeDtypeStruct((S, W_B), MXU_DT)] * 3,
        scratch_shapes=[pltpu.VMEM((2 * TQ, LANES), F32), pltpu.VMEM((S, LANES), F32), pltpu.VMEM((S, LANES), F32)],
        compiler_params=_params("parallel", "arbitrary"),
    )(proj_b, proj_b, proj_b, dyb, rb, tri, trit)


P_BLK = (2 * W_A + 3 * W_B) // W_C


def _window_lanes():
    g = lax.broadcasted_iota(jnp.int32, (1, W_C), 1) // (W_C // 4)
    w = jnp.where(g == 0, POOL_WINDOWS[0], jnp.where(g == 1, POOL_WINDOWS[1],
                  jnp.where(g == 2, POOL_WINDOWS[2], POOL_WINDOWS[3])))
    return g, w


def _shift_rows(ext, k, tm, lead):
    n = ext.shape[0]
    return pltpu.roll(ext, shift=k % n, axis=0)[lead:lead + tm]


def _pool_diff(p_cur, p_halo, row0, tm):
    ext = jnp.concatenate([p_halo, p_cur], axis=0)
    g, w = _window_lanes()
    acc = ext
    sums = []
    for sh in (1, 2, 4, 8):
        acc = acc + pltpu.roll(acc, shift=sh, axis=0)
        sums.append(acc[HALO:HALO + tm])
    wsum = jnp.where(g == 0, sums[0], jnp.where(g == 1, sums[1], jnp.where(g == 2, sums[2], sums[3])))
    pos = (row0 + 1 + lax.broadcasted_iota(jnp.int32, (tm, W_C), 0)).astype(F32)
    cnt = jnp.minimum(pos, w.astype(F32))
    return wsum / cnt - p_cur, cnt


def _pool_specs(tm, nrow_blocks_halo):
    cur = pl.BlockSpec((tm, W_C), lambda i: (i, P_BLK))
    prev = pl.BlockSpec((HALO, W_C), lambda i: (jnp.maximum(i * (tm // HALO) - 1, 0), P_BLK))
    return cur, prev


def _pool_fwd(proj, wbd, scale):
    S = proj.shape[0]
    tm = TM

    def body(p_ref, ph_ref, w_ref, sc_ref, y_ref):
        i = pl.program_id(0)
        halo = jnp.where(i > 0, ph_ref[...], 0.0)
        d, _ = _pool_diff(p_ref[...], halo, i * tm, tm)
        y_ref[...] = _dot(d.astype(MXU_DT), w_ref[...]) * sc_ref[...]

    cur, prev = _pool_specs(tm, S // HALO)
    return _call(
        body, name="pool_fwd", grid=(S // tm,),
        in_specs=[cur, prev, _full_spec((W_C, W_C)), _full_spec((1, W_C))],
        out_specs=_row_spec(tm, W_C),
        out_shape=jax.ShapeDtypeStruct((S, W_C), F32),
        compiler_params=_params("parallel"),
    )(proj, proj, wbd, scale)


def _pool_bwd_a(proj, dy, wbd, scale):
    S = proj.shape[0]
    tm = TM

    def body(p_ref, ph_ref, dy_ref, w_ref, sc_ref, dd_ref, e_ref, dw_ref, dsc_ref):
        i = pl.program_id(0)
        halo = jnp.where(i > 0, ph_ref[...], 0.0)
        d, cnt = _pool_diff(p_ref[...], halo, i * tm, tm)
        db = d.astype(MXU_DT)
        dy = dy_ref[...]

        @pl.when(i == 0)
        def _():
            dw_ref[...] = jnp.zeros_like(dw_ref)
            dsc_ref[...] = jnp.zeros_like(dsc_ref)

        dsc_ref[...] += jnp.sum(dy * _dot(db, w_ref[...]), axis=0, keepdims=True)
        dys = (dy * sc_ref[...]).astype(MXU_DT)
        dw_ref[...] += _dot_tn(db, dys)
        dd = _dot_nt(dys, w_ref[...])
        dd_ref[...] = dd
        e_ref[...] = dd / cnt

    cur, prev = _pool_specs(tm, S // HALO)
    return _call(
        body, name="pool_bwd_a", grid=(S // tm,),
        in_specs=[cur, prev, _row_spec(tm, W_C), _full_spec((W_C, W_C)), _full_spec((1, W_C))],
        out_specs=[_row_spec(tm, W_C), _row_spec(tm, W_C), _full_spec((W_C, W_C)), _full_spec((1, W_C))],
        out_shape=[jax.ShapeDtypeStruct((S, W_C), F32), jax.ShapeDtypeStruct((S, W_C), F32),
                   jax.ShapeDtypeStruct((W_C, W_C), F32), jax.ShapeDtypeStruct((1, W_C), F32)],
        compiler_params=_params("arbitrary"),
    )(proj, proj, dy, wbd, scale)


def _pool_bwd_b(dd, e):
    S = dd.shape[0]
    tm = TM
    nb = S // tm

    def body(dd_ref, e_ref, en_ref, dp_ref):
        i = pl.program_id(0)
        halo = jnp.where(i < nb - 1, en_ref[...], 0.0)
        ext = jnp.concatenate([e_ref[...], halo], axis=0)
        n = ext.shape[0]
        g, _ = _window_lanes()
        acc = ext
        sums = []
        for sh in (1, 2, 4, 8):
            acc = acc + pltpu.roll(acc, shift=n - sh, axis=0)
            sums.append(acc[:tm])
        wsum = jnp.where(g == 0, sums[0], jnp.where(g == 1, sums[1], jnp.where(g == 2, sums[2], sums[3])))
        dp_ref[...] = (wsum - dd_ref[...]).astype(dp_ref.dtype)

    nxt = pl.BlockSpec((HALO, W_C), lambda i: (jnp.minimum((i + 1) * (tm // HALO), S // HALO - 1), 0))
    return _call(
        body, name="pool_bwd_b", grid=(nb,),
        in_specs=[_row_spec(tm, W_C), _row_spec(tm, W_C), nxt],
        out_specs=_row_spec(tm, W_C),
        out_shape=jax.ShapeDtypeStruct((S, W_C), MXU_DT),
        compiler_params=_params("parallel"),
    )(dd, e, e)


TN_FF = 1408
NB_FF = D_FF // TN_FF
CONV_ROWS = 8


def _conv(z_cur, z_halo, cwb, tm):
    ext = jnp.concatenate([z_halo, z_cur], axis=0)
    z2 = _shift_rows(ext, 2, tm, HALO)
    z1 = _shift_rows(ext, 1, tm, HALO)
    zc = cwb[3:4] + z2 * cwb[0:1] + z1 * cwb[1:2] + z_cur * cwb[2:3]
    return zc, z2, z1


def _ffn_specs(tm, order):
    def mk(f):
        return (lambda i, j: f(i, j)) if order == "ij" else (lambda j, i: f(i, j))
    hb = tm // HALO
    return [
        pl.BlockSpec((tm, TN_FF), mk(lambda i, j: (i, j))),
        pl.BlockSpec((tm, TN_FF), mk(lambda i, j: (i, j + NB_FF))),
        pl.BlockSpec((HALO, TN_FF), mk(lambda i, j: (jnp.maximum(i * hb - 1, 0), j))),
        pl.BlockSpec((HALO, TN_FF), mk(lambda i, j: (jnp.maximum(i * hb - 1, 0), j + NB_FF))),
        pl.BlockSpec((CONV_ROWS, TN_FF), mk(lambda i, j: (0, j))),
        pl.BlockSpec((CONV_ROWS, TN_FF), mk(lambda i, j: (0, j + NB_FF))),
    ]


def _conv_gate(z, cwb):
    S = z.shape[0]
    tm = TM

    def body(zg_ref, zu_ref, hg_ref, hu_ref, cg_ref, cu_ref, f_ref):
        first = pl.program_id(0) == 0
        g, _, _ = _conv(zg_ref[...], jnp.where(first, 0.0, hg_ref[...]), cg_ref[...], tm)
        u, _, _ = _conv(zu_ref[...], jnp.where(first, 0.0, hu_ref[...]), cu_ref[...], tm)
        f_ref[...] = (g * jax.nn.sigmoid(g) * u).astype(f_ref.dtype)

    return _call(
        body, name="conv_gate", grid=(S // tm, NB_FF), in_specs=_ffn_specs(tm, "ij"),
        out_specs=pl.BlockSpec((tm, TN_FF), lambda i, j: (i, j)),
        out_shape=jax.ShapeDtypeStruct((S, D_FF), MXU_DT),
        compiler_params=_params("parallel", "parallel"),
    )(z, z, z, z, cwb, cwb)


def _conv_gate_bwd(z, df, cwb):
    S = z.shape[0]
    tm = TM

    def body(zg_ref, zu_ref, hg_ref, hu_ref, cg_ref, cu_ref, df_ref, dg_ref, du_ref, dcg_ref, dcu_ref):
        i = pl.program_id(1)
        first = i == 0
        zg = zg_ref[...]
        zu = zu_ref[...]
        g, g2, g1 = _conv(zg, jnp.where(first, 0.0, hg_ref[...]), cg_ref[...], tm)
        u, u2, u1 = _conv(zu, jnp.where(first, 0.0, hu_ref[...]), cu_ref[...], tm)
        sg = jax.nn.sigmoid(g)
        df = df_ref[...]
        dgv = df * u * (sg * (1.0 + g * (1.0 - sg)))
        duv = df * (g * sg)
        dg_ref[...] = dgv
        du_ref[...] = duv

        @pl.when(first)
        def _():
            dcg_ref[...] = jnp.zeros_like(dcg_ref)
            dcu_ref[...] = jnp.zeros_like(dcu_ref)

        rid = lax.broadcasted_iota(jnp.int32, (CONV_ROWS, TN_FF), 0)

        def taps(dv, s2, s1, s0):
            sums = [jnp.sum(dv * s2, axis=0, keepdims=True), jnp.sum(dv * s1, axis=0, keepdims=True),
                    jnp.sum(dv * s0, axis=0, keepdims=True), jnp.sum(dv, axis=0, keepdims=True)]
            out = jnp.zeros((CONV_ROWS, TN_FF), F32)
            for k, v in enumerate(sums):
                out = jnp.where(rid == k, v, out)
            return out

        dcg_ref[...] += taps(dgv, g2, g1, zg)
        dcu_ref[...] += taps(duv, u2, u1, zu)

    acc = pl.BlockSpec((CONV_ROWS, TN_FF), lambda j, i: (0, j))
    tile = pl.BlockSpec((tm, TN_FF), lambda j, i: (i, j))
    return _call(
        body, name="conv_gate_bwd", grid=(NB_FF, S // tm), in_specs=_ffn_specs(tm, "ji") + [tile],
        out_specs=[tile, tile, acc, acc],
        out_shape=[jax.ShapeDtypeStruct((S, D_FF), F32), jax.ShapeDtypeStruct((S, D_FF), F32),
                   jax.ShapeDtypeStruct((CONV_ROWS, D_FF), F32), jax.ShapeDtypeStruct((CONV_ROWS, D_FF), F32)],
        compiler_params=_params("parallel", "arbitrary"),
    )(z, z, z, z, cwb, cwb, df)


def _conv_transpose(dzc, cwb_half, name):
    S = dzc.shape[0]
    tm = TM
    nb = S // tm

    def body(d_ref, dn_ref, c_ref, o_ref):
        last = pl.program_id(0) == nb - 1
        cur = d_ref[...]
        ext = jnp.concatenate([cur, jnp.where(last, 0.0, dn_ref[...])], axis=0)
        c = c_ref[...]
        o_ref[...] = (cur * c[2:3] + _shift_rows(ext, -1, tm, 0) * c[1:2]
                      + _shift_rows(ext, -2, tm, 0) * c[0:1]).astype(o_ref.dtype)

    hb = tm // HALO
    return _call(
        body, name=name, grid=(nb, NB_FF),
        in_specs=[pl.BlockSpec((tm, TN_FF), lambda i, j: (i, j)),
                  pl.BlockSpec((HALO, TN_FF), lambda i, j: (jnp.minimum((i + 1) * hb, S // HALO - 1), j)),
                  pl.BlockSpec((CONV_ROWS, TN_FF), lambda i, j: (0, j))],
        out_specs=pl.BlockSpec((tm, TN_FF), lambda i, j: (i, j)),
        out_shape=jax.ShapeDtypeStruct((S, D_FF), MXU_DT),
        compiler_params=_params("parallel", "parallel"),
    )(dzc, dzc, cwb_half)


def _layer_consts(w, l):
    wm = _tril_weights(w["sgu_w"][l])
    eye = jnp.eye(4, dtype=F32)
    wbd = (w["pool_w"][l][:, :, None, :] * eye[:, None, :, None]).reshape(W_C, W_C)
    cwb = jnp.concatenate([w["conv_w"][l], w["conv_b"][l][None], jnp.zeros((CONV_ROWS - 4, 2 * D_FF), F32)], axis=0)
    return dict(
        g1=w["norm1_g"][l][None], g2=w["norm2_g"][l][None], gm=w["mix_norm_g"][l][None],
        sng=w["sgu_norm_g"][l][None], wm=wm.astype(MXU_DT), wmt=jnp.swapaxes(wm, 1, 2).astype(MXU_DT),
        bias=jnp.repeat(jnp.transpose(w["sgu_b"][l]), HEAD_DIM, axis=1),
        wbd=wbd.astype(MXU_DT), scale=w["pool_scale"][l][None], cwb=cwb,
        w_in=w["w_in"][l], w_o=w["w_o"][l], w_up=w["w_up"][l], w_down=w["w_down"][l],
    )


def _local_step(x, tgt, w):
    gmat = _group_matrix()
    tri = _tri_matrix()
    trit = jnp.transpose(tri)
    saved = []
    for l in range(DEPTH):
        c = _layer_consts(w, l)
        h1, proj, proj_b = _rms_mm(x, c["g1"], c["w_in"], IN_COLS // 3, "in_proj", (F32, MXU_DT))
        ya = _sgu_fwd(proj, c["sng"], c["wm"], c["bias"], gmat)
        yb, rb = _sb_fwd(proj_b, tri)
        yc = _pool_fwd(proj, c["wbd"], c["scale"])
        x2, yn = _mix_out(ya, yb, yc, c["gm"], c["w_o"], x, gmat)
        h2, z = _rms_mm(x2, c["g2"], c["w_up"], TN_FF, "up_proj", (F32,))
        f = _conv_gate(z, c["cwb"])
        x3 = _mm_res(f, c["w_down"], x2, "down_proj")
        saved.append(dict(c=c, x=x, proj=proj, proj_b=proj_b, h1=h1, ya=ya, yb=yb, yc=yc, rb=rb, x2=x2, yn=yn,
                          z=z, h2=h2, f=f))
        x = x3

    dx, d_final_g, loss8 = _loss_head(x, w["final_g"][None], tgt)
    grads = {n: [None] * DEPTH for n in ("norm1_g", "w_in", "sgu_norm_g", "sgu_w", "sgu_b", "pool_w", "pool_scale",
                                         "mix_norm_g", "w_o", "norm2_g", "w_up", "conv_w", "conv_b", "w_down")}
    for l in reversed(range(DEPTH)):
        s = saved[l]
        c = s["c"]
        df = _mm_nt(dx, c["w_down"], "down_proj_bwd")
        grads["w_down"][l] = _mm_tn(s["f"], dx, "down_proj_wgrad")
        dzg, dzu, dcg, dcu = _conv_gate_bwd(s["z"], df, c["cwb"])
        dz_g = _conv_transpose(dzg, c["cwb"][:, :D_FF], "conv_t_gate")
        dz_u = _conv_transpose(dzu, c["cwb"][:, D_FF:], "conv_t_value")
        dcwb = jnp.concatenate([dcg, dcu], axis=1)
        grads["conv_w"][l] = dcwb[:3]
        grads["conv_b"][l] = dcwb[3]
        grads["w_up"][l] = jnp.concatenate([_mm_tn(s["h2"], dz_g, "up_proj_wgrad_gate"),
                                            _mm_tn(s["h2"], dz_u, "up_proj_wgrad_value")], axis=1)
        dx2, dg2 = _mm_nt_rmsbwd([(dz_g, c["w_up"][:, :D_FF]), (dz_u, c["w_up"][:, D_FF:])],
                                 s["x2"], c["g2"], dx, "up_proj_bwd")
        grads["norm2_g"][l] = dg2[0]
        grads["w_o"][l] = _mm_tn(s["yn"], dx2, "out_proj_wgrad")
        dya, dyb, dyc, dgm = _mix_out_bwd(dx2, c["w_o"], s["ya"], s["yb"], s["yc"], c["gm"], gmat)
        grads["mix_norm_g"][l] = dgm[0]
        dd, e, dwbd, dscale = _pool_bwd_a(s["proj"], dyc, c["wbd"], c["scale"])
        dp = _pool_bwd_b(dd, e)
        grads["pool_w"][l] = jnp.stack([dwbd[g * 64:(g + 1) * 64, g * 64:(g + 1) * 64] for g in range(4)])
        grads["pool_scale"][l] = dscale[0]
        dq, dk, dv = _sb_bwd(s["proj_b"], dyb, s["rb"], tri, trit)
        da, dwm, dbias, dsng = _sgu_bwd(s["proj"], dya, c["sng"], c["wm"], c["wmt"], c["bias"], gmat)
        grads["sgu_w"][l] = dwm
        grads["sgu_b"][l] = jnp.transpose(jnp.sum(dbias.reshape(CHUNK, 4, HEAD_DIM), axis=-1))
        grads["sgu_norm_g"][l] = dsng[0]
        dproj = jnp.concatenate([da, dq, dk, dv, dp], axis=1)
        grads["w_in"][l] = _mm_tn(s["h1"], dproj, "in_proj_wgrad")
        dx, dg1 = _mm_nt_rmsbwd([(dproj, c["w_in"])], s["x"], c["g1"], dx2, "in_proj_bwd")
        grads["norm1_g"][l] = dg1[0]

    out = {n: jnp.stack(v) for n, v in grads.items()}
    out["final_g"] = d_final_g[0]
    return loss8[0, 0], dx, out


MESH = pl.DeviceIdType.MESH
ANY = pl.BlockSpec(memory_space=pl.ANY)


def _all_gather(x_half):
    m_per, n = x_half.shape

    def body(x_ref, out_ref, send_sems, recv_sems, local_sem):
        x, y, c = lax.axis_index("x"), lax.axis_index("y"), lax.axis_index("c")
        me, sibling = (x, y, c), (x, y, 1 - c)
        chips = [(1 - x, y), (x, 1 - y), (1 - x, 1 - y)]

        def rows(px, py, pc):
            return out_ref.at[pl.ds((4 * px + 2 * py + pc) * m_per, m_per), :]

        def copy(k, block, to, src=None):
            return pltpu.make_async_remote_copy(
                src_ref=rows(*block) if src is None else src, dst_ref=rows(*block),
                send_sem=send_sems.at[k], recv_sem=recv_sems.at[k], device_id=to, device_id_type=MESH)

        mine = pltpu.make_async_copy(x_ref, rows(*me), local_sem)
        mine.start()
        first = [copy(0, me, sibling, src=x_ref)]
        first += [copy(1 + j, me, (*chip, c), src=x_ref) for j, chip in enumerate(chips)]
        for cp in first:
            cp.start()
        passed = [copy(4 + j, (*chip, c), sibling) for j, chip in enumerate(chips)]
        for j, chip in enumerate(chips):
            copy(1 + j, (*chip, c), me).wait_recv()
            passed[j].start()
        copy(0, sibling, me).wait_recv()
        for j, chip in enumerate(chips):
            copy(4 + j, (*chip, 1 - c), me).wait_recv()
        for cp in first + passed:
            cp.wait_send()
        mine.wait()

    return _call(
        body, name="weight_all_gather",
        out_shape=jax.ShapeDtypeStruct((8 * m_per, n), x_half.dtype),
        in_specs=[ANY], out_specs=ANY,
        scratch_shapes=[pltpu.SemaphoreType.DMA((7,)), pltpu.SemaphoreType.DMA((7,)), pltpu.SemaphoreType.DMA],
    )(x_half)


def _swap_cores(a, name):
    def body(a_ref, out_ref, send_sem, recv_sem):
        x, y, c = lax.axis_index("x"), lax.axis_index("y"), lax.axis_index("c")
        cp = pltpu.make_async_remote_copy(src_ref=a_ref, dst_ref=out_ref, send_sem=send_sem, recv_sem=recv_sem,
                                          device_id=(x, y, 1 - c), device_id_type=MESH)
        cp.start()
        cp.wait()

    return _call(
        body, name=name, out_shape=jax.ShapeDtypeStruct(a.shape, a.dtype), in_specs=[ANY], out_specs=ANY,
        scratch_shapes=[pltpu.SemaphoreType.DMA, pltpu.SemaphoreType.DMA],
    )(a)


def _exchange_chips(a):
    def body(a_ref, out_ref, send_sems, recv_sems, local_sem):
        x, y, c = lax.axis_index("x"), lax.axis_index("y"), lax.axis_index("c")
        my_chip = 2 * x + y
        chips = [(1 - x, y), (x, 1 - y), (1 - x, 1 - y)]
        mine = pltpu.make_async_copy(a_ref.at[my_chip], out_ref.at[my_chip], local_sem)
        mine.start()
        copies = []
        for k, (px, py) in enumerate(chips):
            copies.append(pltpu.make_async_remote_copy(
                src_ref=a_ref.at[2 * px + py], dst_ref=out_ref.at[my_chip],
                send_sem=send_sems.at[k], recv_sem=recv_sems.at[k], device_id=(px, py, c), device_id_type=MESH))
        for cp in copies:
            cp.start()
        for k, (px, py) in enumerate(chips):
            pltpu.make_async_remote_copy(
                src_ref=a_ref.at[my_chip], dst_ref=out_ref.at[2 * px + py],
                send_sem=send_sems.at[k], recv_sem=recv_sems.at[k], device_id=(px, py, c),
                device_id_type=MESH).wait_recv()
        for cp in copies:
            cp.wait_send()
        mine.wait()

    return _call(
        body, name="grad_exchange_chips", out_shape=jax.ShapeDtypeStruct(a.shape, a.dtype),
        in_specs=[ANY], out_specs=ANY,
        scratch_shapes=[pltpu.SemaphoreType.DMA((3,)), pltpu.SemaphoreType.DMA((3,)), pltpu.SemaphoreType.DMA],
    )(a)


def _add2(a, b, name):
    n, r, c = a.shape
    tr = 512
    spec = pl.BlockSpec((1, tr, c), lambda k, i: (k, i, 0))

    def body(a_ref, b_ref, o_ref):
        o_ref[...] = a_ref[...] + b_ref[...]

    return _call(body, name=name, grid=(n, r // tr), in_specs=[spec, spec], out_specs=spec,
                 out_shape=jax.ShapeDtypeStruct(a.shape, a.dtype), compiler_params=_params("parallel", "parallel"))(a, b)


def _sum_chips(a):
    _, r, c = a.shape
    tr = 512

    def body(a_ref, o_ref):
        o_ref[...] = ((a_ref[0] + a_ref[1]) + a_ref[2]) + a_ref[3]

    return _call(body, name="grad_sum_chips", grid=(r // tr,),
                 in_specs=[pl.BlockSpec((N_CHIPS, tr, c), lambda i: (0, i, 0))],
                 out_specs=pl.BlockSpec((tr, c), lambda i: (i, 0)),
                 out_shape=jax.ShapeDtypeStruct((r, c), a.dtype), compiler_params=_params("parallel"))(a)


def _adamw_math(g_ref, w_ref, m_ref, v_ref, d_ref, nm_ref, nv_ref):
    gv = g_ref[...]
    nm = ADAM_B1 * m_ref[...] + (1.0 - ADAM_B1) * gv
    nv = ADAM_B2 * v_ref[...] + (1.0 - ADAM_B2) * (gv * gv)
    m_hat = nm / (1.0 - ADAM_B1 ** ADAM_STEP)
    v_hat = nv / (1.0 - ADAM_B2 ** ADAM_STEP)
    d_ref[...] = -ADAM_LR * (m_hat / (jnp.sqrt(v_hat) + ADAM_EPS) + ADAM_WD * w_ref[...])
    nm_ref[...] = nm
    nv_ref[...] = nv


def _adamw_big(g, w, m, v, name):
    d, r, c = g.shape
    tr = r if r <= 704 else 256
    spec = pl.BlockSpec((1, tr, c), lambda l, i: (l, i, 0))

    def body(*refs):
        _adamw_math(*refs)

    shp = jax.ShapeDtypeStruct(g.shape, F32)
    return _call(body, name=name, grid=(d, r // tr), in_specs=[spec] * 4, out_specs=[spec] * 3,
                 out_shape=[shp, shp, shp], compiler_params=_params("parallel", "parallel"))(g, w, m, v)


def _adamw_small(gs, ws, ms, vs):
    n = len(gs)

    def body(*refs):
        ins, outs = refs[:4 * n], refs[4 * n:]
        for k in range(n):
            _adamw_math(ins[k], ins[n + k], ins[2 * n + k], ins[3 * n + k], outs[k], outs[n + k], outs[2 * n + k])

    shp = [jax.ShapeDtypeStruct(g.shape, F32) for g in gs]
    res = _call(body, name="adamw_small", out_shape=shp * 3)(*gs, *ws, *ms, *vs)
    return res[:n], res[n:2 * n], res[2 * n:]


def _rows(a, rows):
    flat = a.reshape(-1)
    return jnp.pad(flat, (0, rows * D_MODEL - flat.shape[0])).reshape(rows, D_MODEL)


def _small_rows(p, extra=None):
    parts = [p[n].reshape(-1) for n in SMALL_NAMES]
    if extra is not None:
        parts.append(extra.reshape(-1))
    flat = jnp.concatenate(parts)
    return jnp.pad(flat, (0, ROWS_SMALL * D_MODEL - flat.shape[0])).reshape(ROWS_SMALL, D_MODEL)


def _unpack_shard(pack):
    out = {}
    o = 0
    for name, rows, shape in (("w_in", ROWS_IN, (DEPTH, D_MODEL, IN_COLS // N_CHIPS)),
                              ("w_o", ROWS_O, (DEPTH, D_MODEL // N_CHIPS, D_MODEL)),
                              ("w_up", ROWS_UP, (DEPTH, D_MODEL, 2 * D_FF // N_CHIPS)),
                              ("w_down", ROWS_DOWN, (DEPTH, D_FF // N_CHIPS, D_MODEL)),
                              ("conv_w", ROWS_CONV, (DEPTH, 3, 2 * D_FF // N_CHIPS))):
        n = 1
        for d in shape:
            n *= d
        out[name] = pack[o:o + rows].reshape(-1)[:n].reshape(shape)
        o += rows
    flat = pack[o:].reshape(-1)
    k = 0
    for name in SMALL_NAMES:
        shape = SMALL_SHAPES[name]
        n = 1
        for d in shape:
            n *= d
        out[name] = flat[k:k + n].reshape(shape)
        k += n
    out["extra"] = flat[k]
    return out


def _pack_grads_by_chip(g, loss):
    def cols(a, width):
        d, r, _ = a.shape
        return jnp.transpose(a.reshape(d, r, N_CHIPS, width), (2, 0, 1, 3))

    def rws(a, height):
        d, _, c = a.shape
        return jnp.transpose(a.reshape(d, N_CHIPS, height, c), (1, 0, 2, 3))

    per_chip = dict(w_in=cols(g["w_in"], IN_COLS // N_CHIPS), w_o=rws(g["w_o"], D_MODEL // N_CHIPS),
                    w_up=cols(g["w_up"], 2 * D_FF // N_CHIPS), w_down=rws(g["w_down"], D_FF // N_CHIPS),
                    conv_w=cols(g["conv_w"], 2 * D_FF // N_CHIPS))
    small = _small_rows(g, loss)
    packs = []
    for k in range(N_CHIPS):
        packs.append(jnp.concatenate([
            _rows(per_chip["w_in"][k], ROWS_IN), _rows(per_chip["w_o"][k], ROWS_O), _rows(per_chip["w_up"][k], ROWS_UP),
            _rows(per_chip["w_down"][k], ROWS_DOWN), _rows(per_chip["conv_w"][k], ROWS_CONV), small], axis=0))
    return jnp.stack(packs)


def _gather_weights(p, c):
    conv_bits = lax.bitcast_convert_type(_rows(p["conv_w"], ROWS_CONV), jnp.bfloat16)
    conv_bits = conv_bits.reshape(2 * ROWS_CONV, D_MODEL)
    bf = jnp.bfloat16
    block = jnp.concatenate([_rows(p["w_in"].astype(bf), ROWS_IN), _rows(p["w_o"].astype(bf), ROWS_O),
                             _rows(p["w_up"].astype(bf), ROWS_UP), _rows(p["w_down"].astype(bf), ROWS_DOWN),
                             conv_bits], axis=0)
    half = ROWS_GATHER // 2
    mine = lax.dynamic_slice_in_dim(block, c * half, half, axis=0)
    allw = _all_gather(mine).reshape(N_CHIPS, ROWS_GATHER, D_MODEL)
    o = 0

    def take(rows, shape):
        nonlocal o
        a = allw[:, o:o + rows].reshape((N_CHIPS,) + shape)
        o += rows
        return a

    w_in = take(ROWS_IN, (DEPTH, D_MODEL, IN_COLS // N_CHIPS))
    w_o = take(ROWS_O, (DEPTH, D_MODEL // N_CHIPS, D_MODEL))
    w_up = take(ROWS_UP, (DEPTH, D_MODEL, 2 * D_FF // N_CHIPS))
    w_down = take(ROWS_DOWN, (DEPTH, D_FF // N_CHIPS, D_MODEL))
    conv = lax.bitcast_convert_type(allw[:, o:o + 2 * ROWS_CONV].reshape(N_CHIPS, ROWS_CONV, D_MODEL, 2), F32)
    conv = conv.reshape(N_CHIPS, -1)[:, :DEPTH * 3 * (2 * D_FF // N_CHIPS)].reshape(N_CHIPS, DEPTH, 3, 2 * D_FF // N_CHIPS)

    def by_cols(a):
        k, d, r, wd = a.shape
        return jnp.transpose(a, (1, 2, 0, 3)).reshape(d, r, k * wd)

    def by_rows(a):
        k, d, hgt, cc = a.shape
        return jnp.transpose(a, (1, 0, 2, 3)).reshape(d, k * hgt, cc)

    return dict(w_in=by_cols(w_in), w_o=by_rows(w_o), w_up=by_cols(w_up), w_down=by_rows(w_down), conv_w=by_cols(conv))


def _reduce_grads(g_pack, c):
    half = ROWS_PACK // 2
    keep = lax.dynamic_slice_in_dim(g_pack, c * half, half, axis=1)
    give = lax.dynamic_slice_in_dim(g_pack, (1 - c) * half, half, axis=1)
    got = _swap_cores(give, "grad_swap_cores")
    pair = _add2(keep, got, "grad_add_cores")
    total = _sum_chips(_exchange_chips(pair))
    other = _swap_cores(total, "grad_share_cores")
    lo = jnp.where(c == 0, total, other)
    hi = jnp.where(c == 0, other, total)
    return jnp.concatenate([lo, hi], axis=0)


def kernel(x, norm1_g, w_in, sgu_norm_g, sgu_w, sgu_b, pool_w, pool_scale, mix_norm_g, w_o, norm2_g, w_up, conv_w, conv_b, w_down, final_g, loss_target, m_norm1_g, m_w_in, m_sgu_norm_g, m_sgu_w, m_sgu_b, m_pool_w, m_pool_scale, m_mix_norm_g, m_w_o, m_norm2_g, m_w_up, m_conv_w, m_conv_b, m_w_down, m_final_g, v_norm1_g, v_w_in, v_sgu_norm_g, v_sgu_w, v_sgu_b, v_pool_w, v_pool_scale, v_mix_norm_g, v_w_o, v_norm2_g, v_w_up, v_conv_w, v_conv_b, v_w_down, v_final_g):
    names = ("norm1_g", "w_in", "sgu_norm_g", "sgu_w", "sgu_b", "pool_w", "pool_scale", "mix_norm_g", "w_o",
             "norm2_g", "w_up", "conv_w", "conv_b", "w_down", "final_g")
    p = dict(zip(names, (norm1_g, w_in, sgu_norm_g, sgu_w, sgu_b, pool_w, pool_scale, mix_norm_g, w_o, norm2_g,
                         w_up, conv_w, conv_b, w_down, final_g)))
    pm = dict(zip(names, (m_norm1_g, m_w_in, m_sgu_norm_g, m_sgu_w, m_sgu_b, m_pool_w, m_pool_scale, m_mix_norm_g,
                          m_w_o, m_norm2_g, m_w_up, m_conv_w, m_conv_b, m_w_down, m_final_g)))
    pv = dict(zip(names, (v_norm1_g, v_w_in, v_sgu_norm_g, v_sgu_w, v_sgu_b, v_pool_w, v_pool_scale, v_mix_norm_g,
                          v_w_o, v_norm2_g, v_w_up, v_conv_w, v_conv_b, v_w_down, v_final_g)))
    c = lax.axis_index("c")
    full = dict(p)
    full.update(_gather_weights(p, c))

    loss, dx, grads = _local_step(x[0], loss_target[0], full)

    g = _unpack_shard(_reduce_grads(_pack_grads_by_chip(grads, loss), c))
    d, nm, nv = {}, {}, {}
    for n in BIG_NAMES:
        d[n], nm[n], nv[n] = _adamw_big(g[n], p[n], pm[n], pv[n], "adamw_" + n)

    def two_d(a):
        return a.reshape(1, -1) if a.ndim == 1 else a

    ds, ms, vs = _adamw_small([two_d(g[n]) for n in SMALL_NAMES], [two_d(p[n]) for n in SMALL_NAMES],
                              [two_d(pm[n]) for n in SMALL_NAMES], [two_d(pv[n]) for n in SMALL_NAMES])
    for k, n in enumerate(SMALL_NAMES):
        d[n], nm[n], nv[n] = (a.reshape(p[n].shape) for a in (ds[k], ms[k], vs[k]))
    return (g["extra"], dx[None], *[g[n] for n in names], *[d[n] for n in names], *[nm[n] for n in names],
            *[nv[n] for n in names])
```

```python
import functools

import jax
import jax.numpy as jnp
from jax import lax
from jax.experimental import pallas as pl
from jax.experimental.pallas import tpu as pltpu

F32 = jnp.float32
MXU_DT = jnp.bfloat16

D_MODEL = 1024
DEPTH = 2
HEAD_DIM = 64
W_A = 256
W_B = 512
W_C = 256
IN_COLS = 2 * W_A + 3 * W_B + W_C
CHUNK = 128
POOL_WINDOWS = (2, 4, 8, 16)
D_FF = 2816
EPS = 1e-6
N_CHIPS = 4

ADAM_LR = 0.001
ADAM_B1 = 0.9
ADAM_B2 = 0.999
ADAM_EPS = 1e-08
ADAM_WD = 0.01
ADAM_STEP = 10

LANES = 128
TQ = 512
TK = 256
TM = 256
TM_MM = 512
HALO = 16
VMEM_LIMIT = 56 * 1024 * 1024

ROWS_IN = DEPTH * D_MODEL * (IN_COLS // N_CHIPS) // D_MODEL
ROWS_O = DEPTH * (D_MODEL // N_CHIPS)
ROWS_UP = DEPTH * D_MODEL * (2 * D_FF // N_CHIPS) // D_MODEL
ROWS_DOWN = DEPTH * (D_FF // N_CHIPS)
ROWS_BIG = ROWS_IN + ROWS_O + ROWS_UP + ROWS_DOWN
ROWS_CONV = 16
ROWS_SMALL = 240
ROWS_GATHER = ROWS_BIG + 2 * ROWS_CONV
SP_HALF = (ROWS_CONV + ROWS_SMALL) // 2
ICI_DT = jnp.bfloat16

BIG_NAMES = ("w_in", "w_o", "w_up", "w_down", "conv_w")
SMALL_NAMES = ("norm1_g", "sgu_norm_g", "sgu_w", "sgu_b", "pool_w", "pool_scale",
               "mix_norm_g", "norm2_g", "conv_b", "final_g")
SMALL_SHAPES = {
    "norm1_g": (DEPTH, D_MODEL), "sgu_norm_g": (DEPTH, W_A), "sgu_w": (DEPTH, 4, CHUNK, CHUNK),
    "sgu_b": (DEPTH, 4, CHUNK), "pool_w": (DEPTH, 4, 64, 64), "pool_scale": (DEPTH, W_C),
    "mix_norm_g": (DEPTH, D_MODEL), "norm2_g": (DEPTH, D_MODEL), "conv_b": (DEPTH, 2 * D_FF),
    "final_g": (D_MODEL,),
}


def _call(body, **kw):
    return pl.pallas_call(body, **kw)


def _params(*sem):
    return pltpu.CompilerParams(dimension_semantics=sem, vmem_limit_bytes=VMEM_LIMIT)


def _dot(a, b):
    return jnp.dot(a, b, preferred_element_type=F32)


def _dot_nt(a, b):
    return lax.dot_general(a, b, (((1,), (1,)), ((), ())), preferred_element_type=F32)


def _dot_tn(a, b):
    return lax.dot_general(a, b, (((0,), (0,)), ((), ())), preferred_element_type=F32)


def _split(a):
    hi = a.astype(MXU_DT)
    lo = (a - hi.astype(F32)).astype(MXU_DT)
    return hi, lo


def _dot_split(a, b):
    hi, lo = _split(a)
    return _dot(hi, b) + _dot(lo, b)


def _group_mean(sq, gmat):
    cols = [_dot_split(sq[:, b * LANES:(b + 1) * LANES], gmat) for b in range(sq.shape[1] // LANES)]
    return cols[0] if len(cols) == 1 else jnp.concatenate(cols, axis=-1)


def _group_matrix():
    r = jnp.arange(LANES)
    return jnp.where((r[:, None] // HEAD_DIM) == (r[None, :] // HEAD_DIM), 1.0 / HEAD_DIM, 0.0).astype(MXU_DT)


def _tile(n):
    return max(t for t in range(LANES, 1536 + 1, LANES) if n % t == 0)


def _row_spec(tm, cols, col_block=0):
    return pl.BlockSpec((tm, cols), lambda i, cb=col_block: (i, cb))


def _full_spec(shape):
    nd = len(shape)
    return pl.BlockSpec(shape, lambda *_: (0,) * nd)


def _rms_mm(x, g, w, tn, name, out_dtypes):
    S, D = x.shape
    N = w.shape[1]
    tm = TM_MM

    def body(x_ref, g_ref, w_ref, h_ref, *o_refs):
        @pl.when(pl.program_id(1) == 0)
        def _():
            xv = x_ref[...]
            r = lax.rsqrt(jnp.mean(xv * xv, axis=-1, keepdims=True) + EPS)
            h_ref[...] = (xv * r * g_ref[...]).astype(h_ref.dtype)

        acc = _dot(h_ref[...], w_ref[...])
        for o_ref in o_refs:
            o_ref[...] = acc.astype(o_ref.dtype)

    return _call(
        body, name=name, grid=(S // tm, N // tn),
        in_specs=[pl.BlockSpec((tm, D), lambda i, j: (i, 0)),
                  pl.BlockSpec((1, D), lambda i, j: (0, 0)),
                  pl.BlockSpec((D, tn), lambda i, j: (0, j))],
        out_specs=[pl.BlockSpec((tm, D), lambda i, j: (i, 0))]
        + [pl.BlockSpec((tm, tn), lambda i, j: (i, j)) for _ in out_dtypes],
        out_shape=[jax.ShapeDtypeStruct((S, D), MXU_DT)] + [jax.ShapeDtypeStruct((S, N), dt) for dt in out_dtypes],
        compiler_params=_params("parallel", "arbitrary"),
    )(x, g, w)


def _mm_res(a, w, res, name):
    S, K = a.shape
    N = w.shape[1]
    tm = TM_MM

    def body(a_ref, w_ref, r_ref, o_ref):
        o_ref[...] = r_ref[...] + _dot(a_ref[...], w_ref[...])

    return _call(
        body, name=name, grid=(S // tm,),
        in_specs=[_row_spec(tm, K), _full_spec((K, N)), _row_spec(tm, N)],
        out_specs=_row_spec(tm, N),
        out_shape=jax.ShapeDtypeStruct((S, N), F32),
        compiler_params=_params("parallel"),
    )(a, w, res)


def _mm_nt(a, w, name):
    S, K = a.shape
    N = w.shape[0]
    tm = TM_MM

    def body(a_ref, w_ref, o_ref):
        o_ref[...] = _dot_nt(a_ref[...].astype(MXU_DT), w_ref[...]).astype(o_ref.dtype)

    return _call(
        body, name=name, grid=(S // tm,),
        in_specs=[_row_spec(tm, K), _full_spec((N, K))],
        out_specs=_row_spec(tm, N),
        out_shape=jax.ShapeDtypeStruct((S, N), MXU_DT),
        compiler_params=_params("parallel"),
    )(a, w)


def _mm_tn(a, b, name, col_tiles=False):
    S, K1 = a.shape
    N = b.shape[1]
    ts = TM_MM
    tk = _tile(K1)
    tn = _tile(N)
    if col_tiles:
        out_spec = pl.BlockSpec((None, tk, tn), lambda m, n, s: (n, m, 0))
        out_shape = jax.ShapeDtypeStruct((N // tn, K1, tn), F32)
    else:
        out_spec = pl.BlockSpec((tk, tn), lambda m, n, s: (m, n))
        out_shape = jax.ShapeDtypeStruct((K1, N), F32)

    def body(a_ref, b_ref, o_ref):
        @pl.when(pl.program_id(2) == 0)
        def _():
            o_ref[...] = jnp.zeros_like(o_ref)

        o_ref[...] += _dot_tn(a_ref[...], b_ref[...].astype(MXU_DT))

    return _call(
        body, name=name, grid=(K1 // tk, N // tn, S // ts),
        in_specs=[pl.BlockSpec((ts, tk), lambda m, n, s: (s, m)),
                  pl.BlockSpec((ts, tn), lambda m, n, s: (s, n))],
        out_specs=out_spec, out_shape=out_shape,
        compiler_params=_params("parallel", "parallel", "arbitrary"),
    )(a, b)


def _mm_nt_rmsbwd(pairs, x, g, dres, name):
    S, D = x.shape
    tm = TM
    n = len(pairs)

    def body(*refs):
        a_refs = refs[:n]
        w_refs = refs[n:2 * n]
        x_ref, g_ref, r_ref, dx_ref, dg_ref = refs[2 * n:]
        dh = _dot_nt(a_refs[0][...], w_refs[0][...])
        for k in range(1, n):
            dh += _dot_nt(a_refs[k][...], w_refs[k][...])
        xv = x_ref[...]
        r = lax.rsqrt(jnp.mean(xv * xv, axis=-1, keepdims=True) + EPS)
        xhat = xv * r

        @pl.when(pl.program_id(0) == 0)
        def _():
            dg_ref[...] = jnp.zeros_like(dg_ref)

        dg_ref[...] += jnp.sum(dh * xhat, axis=0, keepdims=True)
        dxh = dh * g_ref[...]
        dx_ref[...] = r_ref[...] + r * (dxh - xhat * jnp.mean(dxh * xhat, axis=-1, keepdims=True))

    in_specs = ([_row_spec(tm, a.shape[1]) for a, _ in pairs] + [_full_spec(w.shape) for _, w in pairs]
                + [_row_spec(tm, D), _full_spec((1, D)), _row_spec(tm, D)])
    return _call(
        body, name=name, grid=(S // tm,), in_specs=in_specs,
        out_specs=[_row_spec(tm, D), _full_spec((1, D))],
        out_shape=[jax.ShapeDtypeStruct((S, D), F32), jax.ShapeDtypeStruct((1, D), F32)],
        compiler_params=_params("arbitrary"),
    )(*[a for a, _ in pairs], *[w for _, w in pairs], x, g, dres)


def _loss_head(x, g, tgt):
    S, D = x.shape
    tm = TM

    def body(x_ref, g_ref, t_ref, dx_ref, dg_ref, l_ref):
        xv = x_ref[...]
        r = lax.rsqrt(jnp.mean(xv * xv, axis=-1, keepdims=True) + EPS)
        xhat = xv * r
        diff = xhat * g_ref[...] - t_ref[...]

        @pl.when(pl.program_id(0) == 0)
        def _():
            dg_ref[...] = jnp.zeros_like(dg_ref)
            l_ref[...] = jnp.zeros_like(l_ref)

        l_ref[...] += jnp.full(l_ref.shape, 0.5 * jnp.sum(jnp.mean(diff * diff, axis=-1, keepdims=True)), F32)
        dout = diff * (1.0 / D)
        dg_ref[...] += jnp.sum(dout * xhat, axis=0, keepdims=True)
        dxh = dout * g_ref[...]
        dx_ref[...] = r * (dxh - xhat * jnp.mean(dxh * xhat, axis=-1, keepdims=True))

    return _call(
        body, name="loss_head", grid=(S // tm,),
        in_specs=[_row_spec(tm, D), _full_spec((1, D)), _row_spec(tm, D)],
        out_specs=[_row_spec(tm, D), _full_spec((1, D)), _full_spec((8, LANES))],
        out_shape=[jax.ShapeDtypeStruct((S, D), F32), jax.ShapeDtypeStruct((1, D), F32),
                   jax.ShapeDtypeStruct((8, LANES), F32)],
        compiler_params=_params("arbitrary"),
    )(x, g, tgt)


def _mix_out(ya, yb, yc, gm, wo, x, gmat):
    S = x.shape[0]
    tm = TM

    def body(ya_ref, yb_ref, yc_ref, gm_ref, wo_ref, x_ref, gmat_ref, x2_ref, yn_ref):
        y = jnp.concatenate([ya_ref[...], yb_ref[...], yc_ref[...]], axis=-1)
        r = lax.rsqrt(_group_mean(y * y, gmat_ref[...]) + EPS)
        yn = (y * r * gm_ref[...]).astype(MXU_DT)
        yn_ref[...] = yn
        x2_ref[...] = x_ref[...] + _dot(yn, wo_ref[...])

    return _call(
        body, name="mix_out", grid=(S // tm,),
        in_specs=[_row_spec(tm, W_A), _row_spec(tm, W_B), _row_spec(tm, W_C), _full_spec((1, D_MODEL)),
                  _full_spec((D_MODEL, D_MODEL)), _row_spec(tm, D_MODEL), _full_spec((LANES, LANES))],
        out_specs=[_row_spec(tm, D_MODEL), _row_spec(tm, D_MODEL)],
        out_shape=[jax.ShapeDtypeStruct((S, D_MODEL), F32), jax.ShapeDtypeStruct((S, D_MODEL), MXU_DT)],
        compiler_params=_params("parallel"),
    )(ya, yb, yc, gm, wo, x, gmat)


def _mix_out_bwd(dx2, wo, ya, yb, yc, gm, gmat):
    S = dx2.shape[0]
    tm = TM

    def body(dx2_ref, wo_ref, ya_ref, yb_ref, yc_ref, gm_ref, gmat_ref, dya_ref, dyb_ref, dyc_ref, dgm_ref):
        dyn = _dot_nt(dx2_ref[...].astype(MXU_DT), wo_ref[...])
        y = jnp.concatenate([ya_ref[...], yb_ref[...], yc_ref[...]], axis=-1)
        r = lax.rsqrt(_group_mean(y * y, gmat_ref[...]) + EPS)
        yhat = y * r

        @pl.when(pl.program_id(0) == 0)
        def _():
            dgm_ref[...] = jnp.zeros_like(dgm_ref)

        dgm_ref[...] += jnp.sum(dyn * yhat, axis=0, keepdims=True)
        dyh = dyn * gm_ref[...]
        dy = r * (dyh - yhat * _group_mean(dyh * yhat, gmat_ref[...]))
        dya_ref[...] = dy[:, :W_A]
        dyb_ref[...] = dy[:, W_A:W_A + W_B]
        dyc_ref[...] = dy[:, W_A + W_B:]

    return _call(
        body, name="mix_out_bwd", grid=(S // tm,),
        in_specs=[_row_spec(tm, D_MODEL), _full_spec((D_MODEL, D_MODEL)), _row_spec(tm, W_A), _row_spec(tm, W_B),
                  _row_spec(tm, W_C), _full_spec((1, D_MODEL)), _full_spec((LANES, LANES))],
        out_specs=[_row_spec(tm, W_A), _row_spec(tm, W_B), _row_spec(tm, W_C), _full_spec((1, D_MODEL))],
        out_shape=[jax.ShapeDtypeStruct((S, W_A), F32), jax.ShapeDtypeStruct((S, W_B), F32),
                   jax.ShapeDtypeStruct((S, W_C), F32), jax.ShapeDtypeStruct((1, D_MODEL), F32)],
        compiler_params=_params("arbitrary"),
    )(dx2, wo, ya, yb, yc, gm, gmat)


_SQRT_HALF = 0.7071067811865476
_INV_SQRT_2PI = 0.3989422804014327


def _sgu_common(a, sng, wm_ref, bias, gmat):
    phi = 0.5 * (1.0 + lax.erf(a * _SQRT_HALF))
    ga = a * phi
    u = ga[:, :W_A]
    v = ga[:, W_A:]
    r = lax.rsqrt(_group_mean(v * v, gmat) + EPS)
    vhat = v * r
    vn = (vhat * sng).astype(MXU_DT)
    head = lax.broadcasted_iota(jnp.int32, (CHUNK, W_A), 1) // HEAD_DIM
    rows = []
    for c in range(a.shape[0] // CHUNK):
        vc = vn[c * CHUNK:(c + 1) * CHUNK]
        s = bias
        for h in range(4):
            s = s + jnp.where(head == h, _dot(wm_ref[h], vc), 0.0)
        rows.append(s)
    s = jnp.concatenate(rows, axis=0)
    return phi, u, r, vhat, vn, s


def _tril_weights(sgu_w_l):
    t = jnp.arange(CHUNK)
    return jnp.where((t[None, :] <= t[:, None])[None], sgu_w_l, 0.0)


def _sgu_fwd(proj, sng, wm, bias, gmat):
    S = proj.shape[0]
    tm = TM

    def body(a_ref, sng_ref, wm_ref, b_ref, gmat_ref, y_ref):
        _, u, _, _, _, s = _sgu_common(a_ref[...], sng_ref[...], wm_ref, b_ref[...], gmat_ref[...])
        y_ref[...] = u * s

    return _call(
        body, name="sgu_fwd", grid=(S // tm,),
        in_specs=[_row_spec(tm, 2 * W_A), _full_spec((1, W_A)), _full_spec((4, CHUNK, CHUNK)),
                  _full_spec((CHUNK, W_A)), _full_spec((LANES, LANES))],
        out_specs=_row_spec(tm, W_A),
        out_shape=jax.ShapeDtypeStruct((S, W_A), F32),
        compiler_params=_params("parallel"),
    )(proj, sng, wm, bias, gmat)


def _sgu_bwd(proj, dy, sng, wm, wmt, bias, gmat):
    S = proj.shape[0]
    tm = TM

    def body(a_ref, dy_ref, sng_ref, wm_ref, wmt_ref, b_ref, gmat_ref, da_ref, dw_ref, db_ref, dsng_ref):
        a = a_ref[...]
        dy = dy_ref[...]
        gmat = gmat_ref[...]
        sng = sng_ref[...]
        phi, u, r, vhat, vn, s = _sgu_common(a, sng, wm_ref, b_ref[...], gmat)
        du = dy * s
        ds = dy * u

        @pl.when(pl.program_id(0) == 0)
        def _():
            dw_ref[...] = jnp.zeros_like(dw_ref)
            db_ref[...] = jnp.zeros_like(db_ref)
            dsng_ref[...] = jnp.zeros_like(dsng_ref)

        head = lax.broadcasted_iota(jnp.int32, (CHUNK, W_A), 1) // HEAD_DIM
        tt = lax.broadcasted_iota(jnp.int32, (CHUNK, CHUNK), 0)
        ss = lax.broadcasted_iota(jnp.int32, (CHUNK, CHUNK), 1)
        rows = []
        for c in range(tm // CHUNK):
            dsc = ds[c * CHUNK:(c + 1) * CHUNK]
            vc = vn[c * CHUNK:(c + 1) * CHUNK]
            db_ref[...] += dsc
            dsb = dsc.astype(MXU_DT)
            dvn = jnp.zeros((CHUNK, W_A), F32)
            for h in range(4):
                dvn = dvn + jnp.where(head == h, _dot(wmt_ref[h], dsb), 0.0)
                dsh = jnp.where(head == h, dsc, 0.0).astype(MXU_DT)
                dw_ref[h] += jnp.where(ss <= tt, _dot_nt(dsh, vc), 0.0)
            rows.append(dvn)
        dvn = jnp.concatenate(rows, axis=0)
        dsng_ref[...] += jnp.sum(dvn * vhat, axis=0, keepdims=True)
        dvh = dvn * sng
        dv = r * (dvh - vhat * _group_mean(dvh * vhat, gmat))
        dga = jnp.concatenate([du, dv], axis=-1)
        dgelu = phi + a * (_INV_SQRT_2PI * jnp.exp(-0.5 * a * a))
        da_ref[...] = (dga * dgelu).astype(da_ref.dtype)

    return _call(
        body, name="sgu_bwd", grid=(S // tm,),
        in_specs=[_row_spec(tm, 2 * W_A), _row_spec(tm, W_A), _full_spec((1, W_A)), _full_spec((4, CHUNK, CHUNK)),
                  _full_spec((4, CHUNK, CHUNK)), _full_spec((CHUNK, W_A)), _full_spec((LANES, LANES))],
        out_specs=[_row_spec(tm, 2 * W_A), _full_spec((4, CHUNK, CHUNK)), _full_spec((CHUNK, W_A)),
                   _full_spec((1, W_A))],
        out_shape=[jax.ShapeDtypeStruct((S, 2 * W_A), MXU_DT), jax.ShapeDtypeStruct((4, CHUNK, CHUNK), F32),
                   jax.ShapeDtypeStruct((CHUNK, W_A), F32), jax.ShapeDtypeStruct((1, W_A), F32)],
        compiler_params=_params("arbitrary"),
    )(proj, dy, sng, wm, wmt, bias, gmat)


Q_BLK0 = (2 * W_A) // LANES
K_BLK0 = Q_BLK0 + W_B // LANES
V_BLK0 = K_BLK0 + W_B // LANES
N_PAIRS = W_B // LANES
EXP_IS_ZERO_BELOW = -120.0


def _tri_matrix():
    r = jnp.arange(TK)
    return (r[:, None] > r[None, :]).astype(MXU_DT)


def _stack_heads(a):
    lane = lax.broadcasted_iota(jnp.int32, a.shape, 1)
    return jnp.concatenate([jnp.where(lane < HEAD_DIM, a, 0.0), jnp.where(lane >= HEAD_DIM, a, 0.0)],
                           axis=0).astype(MXU_DT)


def _unstack_heads(a):
    lane = lax.broadcasted_iota(jnp.int32, (TQ, LANES), 1)
    return jnp.where(lane < HEAD_DIM, a[:TQ], a[TQ:])


def _sb_scores(q2, kj, tri, key_offset):
    z = _dot_nt(q2, kj)
    sp = jnp.log(1.0 + jnp.exp(-jnp.abs(z)))
    lsp = jnp.minimum(z, 0.0) - sp
    lsm = lsp - z
    msk = None
    if key_offset is not None:
        row = lax.broadcasted_iota(jnp.int32, z.shape, 0) & (TQ - 1)
        col = lax.broadcasted_iota(jnp.int32, z.shape, 1) + key_offset
        msk = col < row
        lsm = jnp.where(msk, lsm, 0.0)
    tail = _dot(lsm.astype(MXU_DT), tri)
    return lsp, lsm, tail, msk


def _sb_fwd(proj_b, tri):
    S = proj_b.shape[0]
    nq = S // TQ
    kpq = TQ // TK
    assert S // TK < LANES

    def body(q_ref, k_ref, v_ref, tri_ref, o_ref, rb_ref, acc_ref):
        i = pl.program_id(1)
        lane2 = lax.broadcasted_iota(jnp.int32, (2 * TQ, LANES), 1)
        q2 = _stack_heads(q_ref[...].astype(F32) * (HEAD_DIM ** -0.5))
        tri = tri_ref[...]
        rb_ref[...] = jnp.zeros_like(rb_ref)

        def block(j, run, key_offset=None, first=False):
            start = pl.multiple_of(j * TK, TK)
            kj = k_ref[pl.ds(start, TK), :]
            vj = v_ref[pl.ds(start, TK), :]
            lsp, lsm, tail, msk = _sb_scores(q2, kj, tri, key_offset)
            rb_ref[...] = jnp.where(lane2 == j, run, rb_ref[...])
            att = jnp.exp(lsp + tail + run)
            if msk is not None:
                att = jnp.where(msk, att, 0.0)
            pv = _dot(att.astype(MXU_DT), vj)
            if first:
                acc_ref[...] = pv
            else:
                acc_ref[...] += pv
            return run + tail[:, :1] + lsm[:, :1]

        run = jnp.zeros((2 * TQ, 1), F32)
        for d in reversed(range(kpq)):
            run = block(i * kpq + d, run, key_offset=d * TK, first=(d == kpq - 1))
        past = i * kpq

        def alive(run):
            return (jnp.max(run) > EXP_IS_ZERO_BELOW).astype(jnp.int32)

        def step(carry):
            n, run, _ = carry
            run = block(past - 1 - n, run)
            return n + 1, run, alive(run)

        n, _, _ = lax.while_loop(lambda c: jnp.logical_and(c[0] < past, c[2] > 0), step,
                                 (jnp.int32(0), run, alive(run)))
        rb_ref[...] = jnp.where(lane2 == LANES - 1, n.astype(F32), rb_ref[...])
        o_ref[...] = _unstack_heads(acc_ref[...])

    return _call(
        body, name="sb_fwd", grid=(N_PAIRS, nq),
        in_specs=[pl.BlockSpec((TQ, LANES), lambda p, i: (i, Q_BLK0 + p)),
                  pl.BlockSpec((S, LANES), lambda p, i: (0, K_BLK0 + p)),
                  pl.BlockSpec((S, LANES), lambda p, i: (0, V_BLK0 + p)),
                  pl.BlockSpec((TK, TK), lambda p, i: (0, 0))],
        out_specs=[pl.BlockSpec((TQ, LANES), lambda p, i: (i, p)),
                   pl.BlockSpec((None, None, 2 * TQ, LANES), lambda p, i: (p, i, 0, 0))],
        out_shape=[jax.ShapeDtypeStruct((S, W_B), F32), jax.ShapeDtypeStruct((N_PAIRS, nq, 2 * TQ, LANES), F32)],
        scratch_shapes=[pltpu.VMEM((2 * TQ, LANES), F32)],
        compiler_params=_params("parallel", "arbitrary"),
    )(proj_b, proj_b, proj_b, tri)


def _sb_bwd(proj_b, dyb, rb, tri, trit):
    S = proj_b.shape[0]
    nq = S // TQ

    kpq = TQ // TK

    def body(q_ref, k_ref, v_ref, do_ref, rb_ref, tri_ref, trit_ref, dq_ref, dk_ref, dv_ref,
             dq_acc, dk_acc, dv_acc):
        i = pl.program_id(1)
        lane2 = lax.broadcasted_iota(jnp.int32, (2 * TQ, LANES), 1)
        scale = HEAD_DIM ** -0.5
        q2 = _stack_heads(q_ref[...].astype(F32) * scale)
        do2 = _stack_heads(do_ref[...])
        tri = tri_ref[...]
        trit = trit_ref[...]

        @pl.when(i == 0)
        def _():
            dk_acc[...] = jnp.zeros_like(dk_acc)
            dv_acc[...] = jnp.zeros_like(dv_acc)

        dq_acc[...] = jnp.zeros_like(dq_acc)

        def block(j, pre, key_offset=None):
            start = pl.multiple_of(j * TK, TK)
            kj = k_ref[pl.ds(start, TK), :]
            vj = v_ref[pl.ds(start, TK), :]
            lsp, lsm, tail, msk = _sb_scores(q2, kj, tri, key_offset)
            run = jnp.sum(jnp.where(lane2 == j, rb_ref[...], 0.0), axis=-1, keepdims=True)
            att = jnp.exp(lsp + tail + run)
            if msk is not None:
                att = jnp.where(msk, att, 0.0)
            beta = jnp.exp(lsp)
            dl = _dot_nt(do2, vj) * att
            cin = _dot(dl.astype(MXU_DT), trit)
            dz = dl * (1.0 - beta) - beta * (pre + cin)
            if msk is not None:
                dz = jnp.where(msk, dz, 0.0)
            dzb = dz.astype(MXU_DT)
            dq_acc[...] += _dot(dzb, kj)
            dk_acc[pl.ds(start, TK), :] += _dot_tn(dzb, q2)
            dv_acc[pl.ds(start, TK), :] += _dot_tn(att.astype(MXU_DT), do2)
            return pre + cin[:, TK - 1:] + dl[:, TK - 1:]

        past = i * kpq
        walked = jnp.max(jnp.where(lane2[:8] == LANES - 1, rb_ref[pl.ds(0, 8), :], 0.0)).astype(jnp.int32)
        walked = jnp.clip(walked, 0, past)
        pre = lax.fori_loop(past - walked, past, lambda j, pre: block(j, pre), jnp.zeros((2 * TQ, 1), F32))
        for d in range(kpq):
            pre = block(i * kpq + d, pre, key_offset=d * TK)
        dq_ref[...] = (_unstack_heads(dq_acc[...]) * scale).astype(dq_ref.dtype)

        @pl.when(i == nq - 1)
        def _():
            dk_ref[...] = dk_acc[...].astype(dk_ref.dtype)
            dv_ref[...] = dv_acc[...].astype(dv_ref.dtype)

    return _call(
        body, name="sb_bwd", grid=(N_PAIRS, nq),
        in_specs=[pl.BlockSpec((TQ, LANES), lambda p, i: (i, Q_BLK0 + p)),
                  pl.BlockSpec((S, LANES), lambda p, i: (0, K_BLK0 + p)),
                  pl.BlockSpec((S, LANES), lambda p, i: (0, V_BLK0 + p)),
                  pl.BlockSpec((TQ, LANES), lambda p, i: (i, p)),
                  pl.BlockSpec((None, None, 2 * TQ, LANES), lambda p, i: (p, i, 0, 0)),
                  pl.BlockSpec((TK, TK), lambda p, i: (0, 0)),
                  pl.BlockSpec((TK, TK), lambda p, i: (0, 0))],
        out_specs=[pl.BlockSpec((TQ, LANES), lambda p, i: (i, p)),
                   pl.BlockSpec((S, LANES), lambda p, i: (0, p)),
                   pl.BlockSpec((S, LANES), lambda p, i: (0, p))],
        out_shape=[jax.ShapeDtypeStruct((S, W_B), MXU_DT)] * 3,
        scratch_shapes=[pltpu.VMEM((2 * TQ, LANES), F32), pltpu.VMEM((S, LANES), F32), pltpu.VMEM((S, LANES), F32)],
        compiler_params=_params("parallel", "arbitrary"),
    )(proj_b, proj_b, proj_b, dyb, rb, tri, trit)


P_BLK = (2 * W_A + 3 * W_B) // W_C


def _window_lanes():
    g = lax.broadcasted_iota(jnp.int32, (1, W_C), 1) // (W_C // 4)
    w = jnp.where(g == 0, POOL_WINDOWS[0], jnp.where(g == 1, POOL_WINDOWS[1],
                  jnp.where(g == 2, POOL_WINDOWS[2], POOL_WINDOWS[3])))
    return g, w


def _shift_rows(ext, k, tm, lead):
    n = ext.shape[0]
    return pltpu.roll(ext, shift=k % n, axis=0)[lead:lead + tm]


def _pool_diff(p_cur, p_halo, row0, tm):
    ext = jnp.concatenate([p_halo, p_cur], axis=0)
    g, w = _window_lanes()
    acc = ext
    sums = []
    for sh in (1, 2, 4, 8):
        acc = acc + pltpu.roll(acc, shift=sh, axis=0)
        sums.append(acc[HALO:HALO + tm])
    wsum = jnp.where(g == 0, sums[0], jnp.where(g == 1, sums[1], jnp.where(g == 2, sums[2], sums[3])))
    pos = (row0 + 1 + lax.broadcasted_iota(jnp.int32, (tm, W_C), 0)).astype(F32)
    cnt = jnp.minimum(pos, w.astype(F32))
    return wsum / cnt - p_cur, cnt


def _pool_specs(tm, nrow_blocks_halo):
    cur = pl.BlockSpec((tm, W_C), lambda i: (i, P_BLK))
    prev = pl.BlockSpec((HALO, W_C), lambda i: (jnp.maximum(i * (tm // HALO) - 1, 0), P_BLK))
    return cur, prev


def _pool_fwd(proj, wbd, scale):
    S = proj.shape[0]
    tm = TM

    def body(p_ref, ph_ref, w_ref, sc_ref, y_ref):
        i = pl.program_id(0)
        halo = jnp.where(i > 0, ph_ref[...], 0.0)
        d, _ = _pool_diff(p_ref[...], halo, i * tm, tm)
        y_ref[...] = _dot(d.astype(MXU_DT), w_ref[...]) * sc_ref[...]

    cur, prev = _pool_specs(tm, S // HALO)
    return _call(
        body, name="pool_fwd", grid=(S // tm,),
        in_specs=[cur, prev, _full_spec((W_C, W_C)), _full_spec((1, W_C))],
        out_specs=_row_spec(tm, W_C),
        out_shape=jax.ShapeDtypeStruct((S, W_C), F32),
        compiler_params=_params("parallel"),
    )(proj, proj, wbd, scale)


def _pool_bwd_a(proj, dy, wbd, scale):
    S = proj.shape[0]
    tm = TM

    def body(p_ref, ph_ref, dy_ref, w_ref, sc_ref, dd_ref, e_ref, dw_ref, dsc_ref):
        i = pl.program_id(0)
        halo = jnp.where(i > 0, ph_ref[...], 0.0)
        d, cnt = _pool_diff(p_ref[...], halo, i * tm, tm)
        db = d.astype(MXU_DT)
        dy = dy_ref[...]

        @pl.when(i == 0)
        def _():
            dw_ref[...] = jnp.zeros_like(dw_ref)
            dsc_ref[...] = jnp.zeros_like(dsc_ref)

        dsc_ref[...] += jnp.sum(dy * _dot(db, w_ref[...]), axis=0, keepdims=True)
        dys = (dy * sc_ref[...]).astype(MXU_DT)
        dw_ref[...] += _dot_tn(db, dys)
        dd = _dot_nt(dys, w_ref[...])
        dd_ref[...] = dd
        e_ref[...] = dd / cnt

    cur, prev = _pool_specs(tm, S // HALO)
    return _call(
        body, name="pool_bwd_a", grid=(S // tm,),
        in_specs=[cur, prev, _row_spec(tm, W_C), _full_spec((W_C, W_C)), _full_spec((1, W_C))],
        out_specs=[_row_spec(tm, W_C), _row_spec(tm, W_C), _full_spec((W_C, W_C)), _full_spec((1, W_C))],
        out_shape=[jax.ShapeDtypeStruct((S, W_C), F32), jax.ShapeDtypeStruct((S, W_C), F32),
                   jax.ShapeDtypeStruct((W_C, W_C), F32), jax.ShapeDtypeStruct((1, W_C), F32)],
        compiler_params=_params("arbitrary"),
    )(proj, proj, dy, wbd, scale)


def _pool_bwd_b(dd, e):
    S = dd.shape[0]
    tm = TM
    nb = S // tm

    def body(dd_ref, e_ref, en_ref, dp_ref):
        i = pl.program_id(0)
        halo = jnp.where(i < nb - 1, en_ref[...], 0.0)
        ext = jnp.concatenate([e_ref[...], halo], axis=0)
        n = ext.shape[0]
        g, _ = _window_lanes()
        acc = ext
        sums = []
        for sh in (1, 2, 4, 8):
            acc = acc + pltpu.roll(acc, shift=n - sh, axis=0)
            sums.append(acc[:tm])
        wsum = jnp.where(g == 0, sums[0], jnp.where(g == 1, sums[1], jnp.where(g == 2, sums[2], sums[3])))
        dp_ref[...] = (wsum - dd_ref[...]).astype(dp_ref.dtype)

    nxt = pl.BlockSpec((HALO, W_C), lambda i: (jnp.minimum((i + 1) * (tm // HALO), S // HALO - 1), 0))
    return _call(
        body, name="pool_bwd_b", grid=(nb,),
        in_specs=[_row_spec(tm, W_C), _row_spec(tm, W_C), nxt],
        out_specs=_row_spec(tm, W_C),
        out_shape=jax.ShapeDtypeStruct((S, W_C), MXU_DT),
        compiler_params=_params("parallel"),
    )(dd, e, e)


TN_FF = 1408
NB_FF = D_FF // TN_FF
CONV_ROWS = 8


def _conv(z_cur, z_halo, cwb, tm):
    ext = jnp.concatenate([z_halo, z_cur], axis=0)
    z2 = _shift_rows(ext, 2, tm, HALO)
    z1 = _shift_rows(ext, 1, tm, HALO)
    zc = cwb[3:4] + z2 * cwb[0:1] + z1 * cwb[1:2] + z_cur * cwb[2:3]
    return zc, z2, z1


def _ffn_specs(tm, order):
    def mk(f):
        return (lambda i, j: f(i, j)) if order == "ij" else (lambda j, i: f(i, j))
    hb = tm // HALO
    return [
        pl.BlockSpec((tm, TN_FF), mk(lambda i, j: (i, j))),
        pl.BlockSpec((tm, TN_FF), mk(lambda i, j: (i, j + NB_FF))),
        pl.BlockSpec((HALO, TN_FF), mk(lambda i, j: (jnp.maximum(i * hb - 1, 0), j))),
        pl.BlockSpec((HALO, TN_FF), mk(lambda i, j: (jnp.maximum(i * hb - 1, 0), j + NB_FF))),
        pl.BlockSpec((CONV_ROWS, TN_FF), mk(lambda i, j: (0, j))),
        pl.BlockSpec((CONV_ROWS, TN_FF), mk(lambda i, j: (0, j + NB_FF))),
    ]


def _conv_gate(z, cwb):
    S = z.shape[0]
    tm = TM

    def body(zg_ref, zu_ref, hg_ref, hu_ref, cg_ref, cu_ref, f_ref):
        first = pl.program_id(0) == 0
        g, _, _ = _conv(zg_ref[...].astype(F32), jnp.where(first, 0.0, hg_ref[...].astype(F32)), cg_ref[...], tm)
        u, _, _ = _conv(zu_ref[...].astype(F32), jnp.where(first, 0.0, hu_ref[...].astype(F32)), cu_ref[...], tm)
        f_ref[...] = (g * jax.nn.sigmoid(g) * u).astype(f_ref.dtype)

    return _call(
        body, name="conv_gate", grid=(S // tm, NB_FF), in_specs=_ffn_specs(tm, "ij"),
        out_specs=pl.BlockSpec((tm, TN_FF), lambda i, j: (i, j)),
        out_shape=jax.ShapeDtypeStruct((S, D_FF), MXU_DT),
        compiler_params=_params("parallel", "parallel"),
    )(z, z, z, z, cwb, cwb)


def _conv_gate_bwd(z, df, cwb):
    S = z.shape[0]
    tm = TM

    def body(zg_ref, zu_ref, hg_ref, hu_ref, cg_ref, cu_ref, df_ref, dg_ref, du_ref, dcg_ref, dcu_ref):
        i = pl.program_id(1)
        first = i == 0
        zg = zg_ref[...].astype(F32)
        zu = zu_ref[...].astype(F32)
        g, g2, g1 = _conv(zg, jnp.where(first, 0.0, hg_ref[...].astype(F32)), cg_ref[...], tm)
        u, u2, u1 = _conv(zu, jnp.where(first, 0.0, hu_ref[...].astype(F32)), cu_ref[...], tm)
        sg = jax.nn.sigmoid(g)
        df = df_ref[...].astype(F32)
        dgv = df * u * (sg * (1.0 + g * (1.0 - sg)))
        duv = df * (g * sg)
        dg_ref[...] = dgv.astype(dg_ref.dtype)
        du_ref[...] = duv.astype(du_ref.dtype)

        @pl.when(first)
        def _():
            dcg_ref[...] = jnp.zeros_like(dcg_ref)
            dcu_ref[...] = jnp.zeros_like(dcu_ref)

        rid = lax.broadcasted_iota(jnp.int32, (CONV_ROWS, TN_FF), 0)

        def taps(dv, s2, s1, s0):
            sums = [jnp.sum(dv * s2, axis=0, keepdims=True), jnp.sum(dv * s1, axis=0, keepdims=True),
                    jnp.sum(dv * s0, axis=0, keepdims=True), jnp.sum(dv, axis=0, keepdims=True)]
            out = jnp.zeros((CONV_ROWS, TN_FF), F32)
            for k, v in enumerate(sums):
                out = jnp.where(rid == k, v, out)
            return out

        dcg_ref[...] += taps(dgv, g2, g1, zg)
        dcu_ref[...] += taps(duv, u2, u1, zu)

    acc = pl.BlockSpec((CONV_ROWS, TN_FF), lambda j, i: (0, j))
    tile = pl.BlockSpec((tm, TN_FF), lambda j, i: (i, j))
    return _call(
        body, name="conv_gate_bwd", grid=(NB_FF, S // tm), in_specs=_ffn_specs(tm, "ji") + [tile],
        out_specs=[tile, tile, acc, acc],
        out_shape=[jax.ShapeDtypeStruct((S, D_FF), MXU_DT), jax.ShapeDtypeStruct((S, D_FF), MXU_DT),
                   jax.ShapeDtypeStruct((CONV_ROWS, D_FF), F32), jax.ShapeDtypeStruct((CONV_ROWS, D_FF), F32)],
        compiler_params=_params("parallel", "arbitrary"),
    )(z, z, z, z, cwb, cwb, df)


def _conv_transpose(dzc, cwb_half, name):
    S = dzc.shape[0]
    tm = TM
    nb = S // tm

    def body(d_ref, dn_ref, c_ref, o_ref):
        last = pl.program_id(0) == nb - 1
        cur = d_ref[...].astype(F32)
        ext = jnp.concatenate([cur, jnp.where(last, 0.0, dn_ref[...].astype(F32))], axis=0)
        c = c_ref[...]
        o_ref[...] = (cur * c[2:3] + _shift_rows(ext, -1, tm, 0) * c[1:2]
                      + _shift_rows(ext, -2, tm, 0) * c[0:1]).astype(o_ref.dtype)

    hb = tm // HALO
    return _call(
        body, name=name, grid=(nb, NB_FF),
        in_specs=[pl.BlockSpec((tm, TN_FF), lambda i, j: (i, j)),
                  pl.BlockSpec((HALO, TN_FF), lambda i, j: (jnp.minimum((i + 1) * hb, S // HALO - 1), j)),
                  pl.BlockSpec((CONV_ROWS, TN_FF), lambda i, j: (0, j))],
        out_specs=pl.BlockSpec((tm, TN_FF), lambda i, j: (i, j)),
        out_shape=jax.ShapeDtypeStruct((S, D_FF), MXU_DT),
        compiler_params=_params("parallel", "parallel"),
    )(dzc, dzc, cwb_half)


def _layer_consts(w, l):
    wm = _tril_weights(w["sgu_w"][l])
    eye = jnp.eye(4, dtype=F32)
    wbd = (w["pool_w"][l][:, :, None, :] * eye[:, None, :, None]).reshape(W_C, W_C)
    cwb = jnp.concatenate([w["conv_w"][l], w["conv_b"][l][None], jnp.zeros((CONV_ROWS - 4, 2 * D_FF), F32)], axis=0)
    return dict(
        g1=w["norm1_g"][l][None], g2=w["norm2_g"][l][None], gm=w["mix_norm_g"][l][None],
        sng=w["sgu_norm_g"][l][None], wm=wm.astype(MXU_DT), wmt=jnp.swapaxes(wm, 1, 2).astype(MXU_DT),
        bias=jnp.repeat(jnp.transpose(w["sgu_b"][l]), HEAD_DIM, axis=1),
        wbd=wbd.astype(MXU_DT), scale=w["pool_scale"][l][None], cwb=cwb,
        w_in=w["w_in"][l], w_o=w["w_o"][l], w_up=w["w_up"][l], w_down=w["w_down"][l],
    )


def _local_step(x, tgt, w):
    gmat = _group_matrix()
    tri = _tri_matrix()
    trit = jnp.transpose(tri)
    saved = []
    for l in range(DEPTH):
        c = _layer_consts(w, l)
        h1, proj, proj_b = _rms_mm(x, c["g1"], c["w_in"], IN_COLS // 3, "in_proj", (F32, MXU_DT))
        ya = _sgu_fwd(proj, c["sng"], c["wm"], c["bias"], gmat)
        yb, rb = _sb_fwd(proj_b, tri)
        yc = _pool_fwd(proj, c["wbd"], c["scale"])
        x2, yn = _mix_out(ya, yb, yc, c["gm"], c["w_o"], x, gmat)
        h2, z = _rms_mm(x2, c["g2"], c["w_up"], TN_FF, "up_proj", (MXU_DT,))
        f = _conv_gate(z, c["cwb"])
        x3 = _mm_res(f, c["w_down"], x2, "down_proj")
        saved.append(dict(c=c, x=x, proj=proj, proj_b=proj_b, h1=h1, ya=ya, yb=yb, yc=yc, rb=rb, x2=x2, yn=yn,
                          z=z, h2=h2, f=f))
        x = x3

    dx, d_final_g, loss8 = _loss_head(x, w["final_g"][None], tgt)
    grads = {n: [None] * DEPTH for n in ("norm1_g", "w_in", "sgu_norm_g", "sgu_w", "sgu_b", "pool_w", "pool_scale",
                                         "mix_norm_g", "w_o", "norm2_g", "w_up", "conv_w", "conv_b", "w_down")}
    for l in reversed(range(DEPTH)):
        s = saved[l]
        c = s["c"]
        df = _mm_nt(dx, c["w_down"], "down_proj_bwd")
        grads["w_down"][l] = _mm_tn(s["f"], dx, "down_proj_wgrad").reshape(N_CHIPS, D_FF // N_CHIPS, D_MODEL)
        dzg, dzu, dcg, dcu = _conv_gate_bwd(s["z"], df, c["cwb"])
        dz_g = _conv_transpose(dzg, c["cwb"][:, :D_FF], "conv_t_gate")
        dz_u = _conv_transpose(dzu, c["cwb"][:, D_FF:], "conv_t_value")
        dcwb = jnp.concatenate([dcg, dcu], axis=1)
        grads["conv_w"][l] = dcwb[:3]
        grads["conv_b"][l] = dcwb[3]
        grads["w_up"][l] = jnp.concatenate([_mm_tn(s["h2"], dz_g, "up_proj_wgrad_gate", col_tiles=True),
                                            _mm_tn(s["h2"], dz_u, "up_proj_wgrad_value", col_tiles=True)], axis=0)
        dx2, dg2 = _mm_nt_rmsbwd([(dz_g, c["w_up"][:, :D_FF]), (dz_u, c["w_up"][:, D_FF:])],
                                 s["x2"], c["g2"], dx, "up_proj_bwd")
        grads["norm2_g"][l] = dg2[0]
        grads["w_o"][l] = _mm_tn(s["yn"], dx2, "out_proj_wgrad").reshape(N_CHIPS, D_MODEL // N_CHIPS, D_MODEL)
        dya, dyb, dyc, dgm = _mix_out_bwd(dx2, c["w_o"], s["ya"], s["yb"], s["yc"], c["gm"], gmat)
        grads["mix_norm_g"][l] = dgm[0]
        dd, e, dwbd, dscale = _pool_bwd_a(s["proj"], dyc, c["wbd"], c["scale"])
        dp = _pool_bwd_b(dd, e)
        grads["pool_w"][l] = jnp.stack([dwbd[g * 64:(g + 1) * 64, g * 64:(g + 1) * 64] for g in range(4)])
        grads["pool_scale"][l] = dscale[0]
        dq, dk, dv = _sb_bwd(s["proj_b"], dyb, s["rb"], tri, trit)
        da, dwm, dbias, dsng = _sgu_bwd(s["proj"], dya, c["sng"], c["wm"], c["wmt"], c["bias"], gmat)
        grads["sgu_w"][l] = dwm
        grads["sgu_b"][l] = jnp.transpose(jnp.sum(dbias.reshape(CHUNK, 4, HEAD_DIM), axis=-1))
        grads["sgu_norm_g"][l] = dsng[0]
        dproj = jnp.concatenate([da, dq, dk, dv, dp], axis=1)
        dw_in = _mm_tn(s["h1"], dproj, "in_proj_wgrad")
        grads["w_in"][l] = jnp.transpose(dw_in.reshape(D_MODEL, N_CHIPS, IN_COLS // N_CHIPS), (1, 0, 2))
        dx, dg1 = _mm_nt_rmsbwd([(dproj, c["w_in"])], s["x"], c["g1"], dx2, "in_proj_bwd")
        grads["norm1_g"][l] = dg1[0]

    out = {n: jnp.stack(v) for n, v in grads.items()}
    out["final_g"] = d_final_g[0]
    return loss8[0, 0], dx, out


MESH = pl.DeviceIdType.MESH
ANY = pl.BlockSpec(memory_space=pl.ANY)


def _all_gather(x_half):
    m_per, n = x_half.shape

    def body(x_ref, out_ref, send_sems, recv_sems, local_sem):
        x, y, c = lax.axis_index("x"), lax.axis_index("y"), lax.axis_index("c")
        me, sibling = (x, y, c), (x, y, 1 - c)
        chips = [(1 - x, y), (x, 1 - y), (1 - x, 1 - y)]

        def rows(px, py, pc):
            return out_ref.at[pl.ds((4 * px + 2 * py + pc) * m_per, m_per), :]

        def copy(k, block, to, src=None):
            return pltpu.make_async_remote_copy(
                src_ref=rows(*block) if src is None else src, dst_ref=rows(*block),
                send_sem=send_sems.at[k], recv_sem=recv_sems.at[k], device_id=to, device_id_type=MESH)

        mine = pltpu.make_async_copy(x_ref, rows(*me), local_sem)
        mine.start()
        first = [copy(0, me, sibling, src=x_ref)]
        first += [copy(1 + j, me, (*chip, c), src=x_ref) for j, chip in enumerate(chips)]
        for cp in first:
            cp.start()
        passed = [copy(4 + j, (*chip, c), sibling) for j, chip in enumerate(chips)]
        for j, chip in enumerate(chips):
            copy(1 + j, (*chip, c), me).wait_recv()
            passed[j].start()
        copy(0, sibling, me).wait_recv()
        for j, chip in enumerate(chips):
            copy(4 + j, (*chip, 1 - c), me).wait_recv()
        for cp in first + passed:
            cp.wait_send()
        mine.wait()

    return _call(
        body, name="weight_all_gather",
        out_shape=jax.ShapeDtypeStruct((8 * m_per, n), x_half.dtype),
        in_specs=[ANY], out_specs=ANY,
        scratch_shapes=[pltpu.SemaphoreType.DMA((7,)), pltpu.SemaphoreType.DMA((7,)), pltpu.SemaphoreType.DMA],
    )(x_half)


def _row_tile(r):
    return r if r <= 704 else 256


def _grad_swap(bigs, sp):
    n = len(bigs)

    def body(*refs):
        ins, outs = refs[:n + 1], refs[n + 1:2 * n + 2]
        send_sems, recv_sems = refs[2 * n + 2:]
        x, y, c = lax.axis_index("x"), lax.axis_index("y"), lax.axis_index("c")
        srcs = [ins[a].at[1 - c] for a in range(n)] + [ins[n].at[:, pl.ds((1 - c) * SP_HALF, SP_HALF), :]]
        copies = [pltpu.make_async_remote_copy(src_ref=srcs[a], dst_ref=outs[a], send_sem=send_sems.at[a],
                                               recv_sem=recv_sems.at[a], device_id=(x, y, 1 - c),
                                               device_id_type=MESH) for a in range(n + 1)]
        for cp in copies:
            cp.start()
        for cp in copies:
            cp.wait()

    shapes = [jax.ShapeDtypeStruct(b.shape[1:], b.dtype) for b in bigs]
    shapes.append(jax.ShapeDtypeStruct((N_CHIPS, SP_HALF, D_MODEL), sp.dtype))
    return _call(
        body, name="grad_swap_cores", out_shape=shapes, in_specs=[ANY] * (n + 1), out_specs=[ANY] * (n + 1),
        scratch_shapes=[pltpu.SemaphoreType.DMA((n + 1,)), pltpu.SemaphoreType.DMA((n + 1,))],
    )(*bigs, sp)


def _pair_add(g, r, c_arr, name, out_dtype):
    _, k, rr, cc = g.shape
    tr = _row_tile(rr)

    def body(c_ref, g_ref, r_ref, o_ref):
        o_ref[...] = (g_ref[...] + r_ref[...]).astype(o_ref.dtype)

    spec = pl.BlockSpec((1, tr, cc), lambda kk, i, c_ref: (kk, i, 0))
    grid_spec = pltpu.PrefetchScalarGridSpec(
        num_scalar_prefetch=1, grid=(k, rr // tr),
        in_specs=[pl.BlockSpec((None, 1, tr, cc), lambda kk, i, c_ref: (c_ref[0], kk, i, 0)), spec], out_specs=spec)
    return _call(body, name=name, grid_spec=grid_spec, out_shape=jax.ShapeDtypeStruct((k, rr, cc), out_dtype),
                 compiler_params=_params("parallel", "parallel"))(c_arr, g, r)


def _pair_add_small(sp, r, c_arr):
    def body(c_ref, g_ref, r_ref, o_ref):
        o_ref[...] = g_ref[...] + r_ref[...]

    spec = pl.BlockSpec((1, SP_HALF, D_MODEL), lambda kk, c_ref: (kk, 0, 0))
    grid_spec = pltpu.PrefetchScalarGridSpec(
        num_scalar_prefetch=1, grid=(N_CHIPS,),
        in_specs=[pl.BlockSpec((1, SP_HALF, D_MODEL), lambda kk, c_ref: (kk, c_ref[0], 0)), spec], out_specs=spec)
    return _call(body, name="grad_add_cores_small", grid_spec=grid_spec,
                 out_shape=jax.ShapeDtypeStruct(r.shape, F32), compiler_params=_params("parallel"))(c_arr, sp, r)


def _grad_exchange(hs):
    n = len(hs)

    def body(*refs):
        ins, outs = refs[:n], refs[n:2 * n]
        send_sems, recv_sems, local_sems = refs[2 * n:]
        x, y, c = lax.axis_index("x"), lax.axis_index("y"), lax.axis_index("c")
        my_chip = 2 * x + y
        chips = [(1 - x, y), (x, 1 - y), (1 - x, 1 - y)]
        local = [pltpu.make_async_copy(ins[a].at[my_chip], outs[a].at[my_chip], local_sems.at[a]) for a in range(n)]
        for cp in local:
            cp.start()

        def copy(a, k, src_chip, dst_chip):
            px, py = chips[k]
            return pltpu.make_async_remote_copy(
                src_ref=ins[a].at[src_chip], dst_ref=outs[a].at[dst_chip], send_sem=send_sems.at[a, k],
                recv_sem=recv_sems.at[a, k], device_id=(px, py, c), device_id_type=MESH)

        sends = [copy(a, k, 2 * chips[k][0] + chips[k][1], my_chip) for k in range(3) for a in range(n)]
        for cp in sends:
            cp.start()
        for k in range(3):
            for a in range(n):
                copy(a, k, my_chip, 2 * chips[k][0] + chips[k][1]).wait_recv()
        for cp in sends:
            cp.wait_send()
        for cp in local:
            cp.wait()

    return _call(
        body, name="grad_exchange_chips", out_shape=[jax.ShapeDtypeStruct(h.shape, h.dtype) for h in hs],
        in_specs=[ANY] * n, out_specs=[ANY] * n,
        scratch_shapes=[pltpu.SemaphoreType.DMA((n, 3)), pltpu.SemaphoreType.DMA((n, 3)), pltpu.SemaphoreType.DMA((n,))],
    )(*hs)


def _sum_chips(a, name):
    _, r, c = a.shape
    tr = _row_tile(r)

    def body(a_ref, o_ref):
        o_ref[...] = ((a_ref[0].astype(F32) + a_ref[1].astype(F32)) + a_ref[2].astype(F32)) + a_ref[3].astype(F32)

    return _call(body, name=name, grid=(r // tr,),
                 in_specs=[pl.BlockSpec((N_CHIPS, tr, c), lambda i: (0, i, 0))],
                 out_specs=pl.BlockSpec((tr, c), lambda i: (i, 0)),
                 out_shape=jax.ShapeDtypeStruct((r, c), F32), compiler_params=_params("parallel"))(a)


def _grad_share(fs, f_sp):
    n = len(fs)

    def body(*refs):
        ins, outs = refs[:n + 1], refs[n + 1:2 * n + 2]
        send_sems, recv_sems, local_sems = refs[2 * n + 2:]
        x, y, c = lax.axis_index("x"), lax.axis_index("y"), lax.axis_index("c")
        dsts = [outs[a].at[c] for a in range(n)] + [outs[n].at[pl.ds(c * SP_HALF, SP_HALF), :]]
        local = [pltpu.make_async_copy(ins[a], dsts[a], local_sems.at[a]) for a in range(n + 1)]
        copies = [pltpu.make_async_remote_copy(src_ref=ins[a], dst_ref=dsts[a], send_sem=send_sems.at[a],
                                               recv_sem=recv_sems.at[a], device_id=(x, y, 1 - c),
                                               device_id_type=MESH) for a in range(n + 1)]
        for cp in local + copies:
            cp.start()
        theirs = [outs[a].at[1 - c] for a in range(n)] + [outs[n].at[pl.ds((1 - c) * SP_HALF, SP_HALF), :]]
        for a in range(n + 1):
            pltpu.make_async_remote_copy(src_ref=ins[a], dst_ref=theirs[a], send_sem=send_sems.at[a],
                                         recv_sem=recv_sems.at[a], device_id=(x, y, 1 - c),
                                         device_id_type=MESH).wait_recv()
        for cp in copies:
            cp.wait_send()
        for cp in local:
            cp.wait()

    shapes = [jax.ShapeDtypeStruct((DEPTH,) + f.shape, F32) for f in fs]
    shapes.append(jax.ShapeDtypeStruct((2 * SP_HALF, D_MODEL), F32))
    return _call(
        body, name="grad_share_cores", out_shape=shapes, in_specs=[ANY] * (n + 1), out_specs=[ANY] * (n + 1),
        scratch_shapes=[pltpu.SemaphoreType.DMA((n + 1,)), pltpu.SemaphoreType.DMA((n + 1,)),
                        pltpu.SemaphoreType.DMA((n + 1,))],
    )(*fs, f_sp)


def _adamw_math(g_ref, w_ref, m_ref, v_ref, d_ref, nm_ref, nv_ref):
    gv = g_ref[...]
    nm = ADAM_B1 * m_ref[...] + (1.0 - ADAM_B1) * gv
    nv = ADAM_B2 * v_ref[...] + (1.0 - ADAM_B2) * (gv * gv)
    m_hat = nm / (1.0 - ADAM_B1 ** ADAM_STEP)
    v_hat = nv / (1.0 - ADAM_B2 ** ADAM_STEP)
    d_ref[...] = -ADAM_LR * (m_hat / (jnp.sqrt(v_hat) + ADAM_EPS) + ADAM_WD * w_ref[...])
    nm_ref[...] = nm
    nv_ref[...] = nv


def _adamw_big(g, w, m, v, name):
    d, r, c = g.shape
    tr = r if r <= 704 else 256
    spec = pl.BlockSpec((1, tr, c), lambda l, i: (l, i, 0))

    def body(*refs):
        _adamw_math(*refs)

    shp = jax.ShapeDtypeStruct(g.shape, F32)
    return _call(body, name=name, grid=(d, r // tr), in_specs=[spec] * 4, out_specs=[spec] * 3,
                 out_shape=[shp, shp, shp], compiler_params=_params("parallel", "parallel"))(g, w, m, v)


def _adamw_small(gs, ws, ms, vs):
    n = len(gs)

    def body(*refs):
        ins, outs = refs[:4 * n], refs[4 * n:]
        for k in range(n):
            _adamw_math(ins[k], ins[n + k], ins[2 * n + k], ins[3 * n + k], outs[k], outs[n + k], outs[2 * n + k])

    shp = [jax.ShapeDtypeStruct(g.shape, F32) for g in gs]
    res = _call(body, name="adamw_small", out_shape=shp * 3)(*gs, *ws, *ms, *vs)
    return res[:n], res[n:2 * n], res[2 * n:]


def _rows(a, rows):
    flat = a.reshape(-1)
    return jnp.pad(flat, (0, rows * D_MODEL - flat.shape[0])).reshape(rows, D_MODEL)


def _small_rows(p, extra=None):
    parts = [p[n].reshape(-1) for n in SMALL_NAMES]
    if extra is not None:
        parts.append(extra.reshape(-1))
    flat = jnp.concatenate(parts)
    return jnp.pad(flat, (0, ROWS_SMALL * D_MODEL - flat.shape[0])).reshape(ROWS_SMALL, D_MODEL)


CONV_SHARD = (DEPTH, 3, 2 * D_FF // N_CHIPS)
N_CONV_SHARD = DEPTH * 3 * (2 * D_FF // N_CHIPS)


def _small_pack(g, loss):
    conv = jnp.transpose(g["conv_w"].reshape(DEPTH, 3, N_CHIPS, 2 * D_FF // N_CHIPS), (2, 0, 1, 3))
    conv = jnp.stack([_rows(conv[k], ROWS_CONV) for k in range(N_CHIPS)])
    small = jnp.broadcast_to(_small_rows(g, loss), (N_CHIPS, ROWS_SMALL, D_MODEL))
    return jnp.concatenate([conv, small], axis=1)


def _unpack_small(pack):
    out = {"conv_w": pack[:ROWS_CONV].reshape(-1)[:N_CONV_SHARD].reshape(CONV_SHARD)}
    flat = pack[ROWS_CONV:].reshape(-1)
    k = 0
    for name in SMALL_NAMES:
        shape = SMALL_SHAPES[name]
        n = 1
        for d in shape:
            n *= d
        out[name] = flat[k:k + n].reshape(shape)
        k += n
    out["extra"] = flat[k]
    return out


def _gather_weights(p, c):
    conv_bits = lax.bitcast_convert_type(_rows(p["conv_w"], ROWS_CONV), jnp.bfloat16)
    conv_bits = conv_bits.reshape(2 * ROWS_CONV, D_MODEL)
    bf = jnp.bfloat16
    block = jnp.concatenate([_rows(p["w_in"].astype(bf), ROWS_IN), _rows(p["w_o"].astype(bf), ROWS_O),
                             _rows(p["w_up"].astype(bf), ROWS_UP), _rows(p["w_down"].astype(bf), ROWS_DOWN),
                             conv_bits], axis=0)
    half = ROWS_GATHER // 2
    mine = lax.dynamic_slice_in_dim(block, c * half, half, axis=0)
    allw = _all_gather(mine).reshape(N_CHIPS, ROWS_GATHER, D_MODEL)
    o = 0

    def take(rows, shape):
        nonlocal o
        a = allw[:, o:o + rows].reshape((N_CHIPS,) + shape)
        o += rows
        return a

    w_in = take(ROWS_IN, (DEPTH, D_MODEL, IN_COLS // N_CHIPS))
    w_o = take(ROWS_O, (DEPTH, D_MODEL // N_CHIPS, D_MODEL))
    w_up = take(ROWS_UP, (DEPTH, D_MODEL, 2 * D_FF // N_CHIPS))
    w_down = take(ROWS_DOWN, (DEPTH, D_FF // N_CHIPS, D_MODEL))
    conv = lax.bitcast_convert_type(allw[:, o:o + 2 * ROWS_CONV].reshape(N_CHIPS, ROWS_CONV, D_MODEL, 2), F32)
    conv = conv.reshape(N_CHIPS, -1)[:, :DEPTH * 3 * (2 * D_FF // N_CHIPS)].reshape(N_CHIPS, DEPTH, 3, 2 * D_FF // N_CHIPS)

    def by_cols(a):
        k, d, r, wd = a.shape
        return jnp.transpose(a, (1, 2, 0, 3)).reshape(d, r, k * wd)

    def by_rows(a):
        k, d, hgt, cc = a.shape
        return jnp.transpose(a, (1, 0, 2, 3)).reshape(d, k * hgt, cc)

    return dict(w_in=by_cols(w_in), w_o=by_rows(w_o), w_up=by_cols(w_up), w_down=by_rows(w_down), conv_w=by_cols(conv))


def _reduce_grads(grads, loss, c):
    bigs = [grads[n] for n in BIG_NAMES[:4]]
    sp = _small_pack(grads, loss)
    c_arr = jnp.reshape(c, (1,)).astype(jnp.int32)
    got = _grad_swap(bigs, sp)
    pair = [_pair_add(bigs[a], got[a], c_arr, "grad_add_cores_" + BIG_NAMES[a], ICI_DT) for a in range(4)]
    pair.append(_pair_add_small(sp, got[4], c_arr))
    parts = _grad_exchange(pair)
    total = [_sum_chips(parts[a], "grad_sum_chips_" + (BIG_NAMES[:4] + ("small",))[a]) for a in range(5)]
    shared = _grad_share(total[:4], total[4])
    out = dict(zip(BIG_NAMES[:4], shared[:4]))
    out.update(_unpack_small(shared[4]))
    return out


def kernel(x, norm1_g, w_in, sgu_norm_g, sgu_w, sgu_b, pool_w, pool_scale, mix_norm_g, w_o, norm2_g, w_up, conv_w, conv_b, w_down, final_g, loss_target, m_norm1_g, m_w_in, m_sgu_norm_g, m_sgu_w, m_sgu_b, m_pool_w, m_pool_scale, m_mix_norm_g, m_w_o, m_norm2_g, m_w_up, m_conv_w, m_conv_b, m_w_down, m_final_g, v_norm1_g, v_w_in, v_sgu_norm_g, v_sgu_w, v_sgu_b, v_pool_w, v_pool_scale, v_mix_norm_g, v_w_o, v_norm2_g, v_w_up, v_conv_w, v_conv_b, v_w_down, v_final_g):
    names = ("norm1_g", "w_in", "sgu_norm_g", "sgu_w", "sgu_b", "pool_w", "pool_scale", "mix_norm_g", "w_o",
             "norm2_g", "w_up", "conv_w", "conv_b", "w_down", "final_g")
    p = dict(zip(names, (norm1_g, w_in, sgu_norm_g, sgu_w, sgu_b, pool_w, pool_scale, mix_norm_g, w_o, norm2_g,
                         w_up, conv_w, conv_b, w_down, final_g)))
    pm = dict(zip(names, (m_norm1_g, m_w_in, m_sgu_norm_g, m_sgu_w, m_sgu_b, m_pool_w, m_pool_scale, m_mix_norm_g,
                          m_w_o, m_norm2_g, m_w_up, m_conv_w, m_conv_b, m_w_down, m_final_g)))
    pv = dict(zip(names, (v_norm1_g, v_w_in, v_sgu_norm_g, v_sgu_w, v_sgu_b, v_pool_w, v_pool_scale, v_mix_norm_g,
                          v_w_o, v_norm2_g, v_w_up, v_conv_w, v_conv_b, v_w_down, v_final_g)))
    c = lax.axis_index("c")
    full = dict(p)
    full.update(_gather_weights(p, c))

    loss, dx, grads = _local_step(x[0], loss_target[0], full)

    g = _reduce_grads(grads, loss, c)
    d, nm, nv = {}, {}, {}
    for n in BIG_NAMES:
        d[n], nm[n], nv[n] = _adamw_big(g[n], p[n], pm[n], pv[n], "adamw_" + n)

    def two_d(a):
        return a.reshape(1, -1) if a.ndim == 1 else a

    ds, ms, vs = _adamw_small([two_d(g[n]) for n in SMALL_NAMES], [two_d(p[n]) for n in SMALL_NAMES],
                              [two_d(pm[n]) for n in SMALL_NAMES], [two_d(pv[n]) for n in SMALL_NAMES])
    for k, n in enumerate(SMALL_NAMES):
        d[n], nm[n], nv[n] = (a.reshape(p[n].shape) for a in (ds[k], ms[k], vs[k]))
    return (g["extra"], dx[None], *[g[n] for n in names], *[d[n] for n in names], *[nm[n] for n in names],
            *[nv[n] for n in names])
```

```python
import functools

import jax
import jax.numpy as jnp
from jax import lax
from jax.experimental import pallas as pl
from jax.experimental.pallas import tpu as pltpu

F32 = jnp.float32
MXU_DT = jnp.bfloat16

D_MODEL = 1024
DEPTH = 2
HEAD_DIM = 64
W_A = 256
W_B = 512
W_C = 256
IN_COLS = 2 * W_A + 3 * W_B + W_C
CHUNK = 128
POOL_WINDOWS = (2, 4, 8, 16)
D_FF = 2816
EPS = 1e-6
N_CHIPS = 4

ADAM_LR = 0.001
ADAM_B1 = 0.9
ADAM_B2 = 0.999
ADAM_EPS = 1e-08
ADAM_WD = 0.01
ADAM_STEP = 10

LANES = 128
TQ = 512
TK = 256
TM = 256
TM_MM = 512
HALO = 16
VMEM_LIMIT = 56 * 1024 * 1024

ROWS_IN = DEPTH * D_MODEL * (IN_COLS // N_CHIPS) // D_MODEL
ROWS_O = DEPTH * (D_MODEL // N_CHIPS)
ROWS_UP = DEPTH * D_MODEL * (2 * D_FF // N_CHIPS) // D_MODEL
ROWS_DOWN = DEPTH * (D_FF // N_CHIPS)
ROWS_BIG = ROWS_IN + ROWS_O + ROWS_UP + ROWS_DOWN
ROWS_CONV = 16
ROWS_SMALL = 240
ROWS_GATHER = ROWS_BIG + 2 * ROWS_CONV
SP_HALF = (ROWS_CONV + ROWS_SMALL) // 2
ICI_DT = jnp.bfloat16

BIG_NAMES = ("w_in", "w_o", "w_up", "w_down", "conv_w")
SMALL_NAMES = ("norm1_g", "sgu_norm_g", "sgu_w", "sgu_b", "pool_w", "pool_scale",
               "mix_norm_g", "norm2_g", "conv_b", "final_g")
SMALL_SHAPES = {
    "norm1_g": (DEPTH, D_MODEL), "sgu_norm_g": (DEPTH, W_A), "sgu_w": (DEPTH, 4, CHUNK, CHUNK),
    "sgu_b": (DEPTH, 4, CHUNK), "pool_w": (DEPTH, 4, 64, 64), "pool_scale": (DEPTH, W_C),
    "mix_norm_g": (DEPTH, D_MODEL), "norm2_g": (DEPTH, D_MODEL), "conv_b": (DEPTH, 2 * D_FF),
    "final_g": (D_MODEL,),
}


def _call(body, **kw):
    return pl.pallas_call(body, **kw)


def _params(*sem):
    return pltpu.CompilerParams(dimension_semantics=sem, vmem_limit_bytes=VMEM_LIMIT)


def _dot(a, b):
    return jnp.dot(a, b, preferred_element_type=F32)


def _dot_nt(a, b):
    return lax.dot_general(a, b, (((1,), (1,)), ((), ())), preferred_element_type=F32)


def _dot_tn(a, b):
    return lax.dot_general(a, b, (((0,), (0,)), ((), ())), preferred_element_type=F32)


def _split(a):
    hi = a.astype(MXU_DT)
    lo = (a - hi.astype(F32)).astype(MXU_DT)
    return hi, lo


def _dot_split(a, b):
    hi, lo = _split(a)
    return _dot(hi, b) + _dot(lo, b)


def _group_mean(sq, gmat):
    cols = [_dot_split(sq[:, b * LANES:(b + 1) * LANES], gmat) for b in range(sq.shape[1] // LANES)]
    return cols[0] if len(cols) == 1 else jnp.concatenate(cols, axis=-1)


def _group_matrix():
    r = jnp.arange(LANES)
    return jnp.where((r[:, None] // HEAD_DIM) == (r[None, :] // HEAD_DIM), 1.0 / HEAD_DIM, 0.0).astype(MXU_DT)


def _tile(n):
    return max(t for t in range(LANES, 1536 + 1, LANES) if n % t == 0)


def _row_spec(tm, cols, col_block=0):
    return pl.BlockSpec((tm, cols), lambda i, cb=col_block: (i, cb))


def _full_spec(shape):
    nd = len(shape)
    return pl.BlockSpec(shape, lambda *_: (0,) * nd)


def _rms_mm(x, g, w, tn, name, out_dtypes):
    S, D = x.shape
    N = w.shape[1]
    tm = TM_MM

    def body(x_ref, g_ref, w_ref, h_ref, *o_refs):
        xv = x_ref[...]
        r = lax.rsqrt(jnp.mean(xv * xv, axis=-1, keepdims=True) + EPS)
        h = (xv * r * g_ref[...]).astype(h_ref.dtype)
        h_ref[...] = h
        for n0 in range(0, N, tn):
            acc = _dot(h, w_ref[:, n0:n0 + tn])
            for o_ref in o_refs:
                o_ref[:, n0:n0 + tn] = acc.astype(o_ref.dtype)

    return _call(
        body, name=name, grid=(S // tm,),
        in_specs=[_row_spec(tm, D), _full_spec((1, D)), _full_spec((D, N))],
        out_specs=[_row_spec(tm, D)] + [_row_spec(tm, N) for _ in out_dtypes],
        out_shape=[jax.ShapeDtypeStruct((S, D), MXU_DT)] + [jax.ShapeDtypeStruct((S, N), dt) for dt in out_dtypes],
        compiler_params=_params("parallel"),
    )(x, g, w)


def _mm_res(a, w, res, name):
    S, K = a.shape
    N = w.shape[1]
    tm = TM_MM

    def body(a_ref, w_ref, r_ref, o_ref):
        o_ref[...] = r_ref[...] + _dot(a_ref[...], w_ref[...])

    return _call(
        body, name=name, grid=(S // tm,),
        in_specs=[_row_spec(tm, K), _full_spec((K, N)), _row_spec(tm, N)],
        out_specs=_row_spec(tm, N),
        out_shape=jax.ShapeDtypeStruct((S, N), F32),
        compiler_params=_params("parallel"),
    )(a, w, res)


def _mm_nt(a, w, name):
    S, K = a.shape
    N = w.shape[0]
    tm = TM_MM

    def body(a_ref, w_ref, o_ref):
        o_ref[...] = _dot_nt(a_ref[...].astype(MXU_DT), w_ref[...]).astype(o_ref.dtype)

    return _call(
        body, name=name, grid=(S // tm,),
        in_specs=[_row_spec(tm, K), _full_spec((N, K))],
        out_specs=_row_spec(tm, N),
        out_shape=jax.ShapeDtypeStruct((S, N), MXU_DT),
        compiler_params=_params("parallel"),
    )(a, w)


def _mm_tn(a, b, name, col_tiles=False):
    S, K1 = a.shape
    N = b.shape[1]
    ts = TM_MM
    tk = _tile(K1)
    tn = _tile(N)
    if col_tiles:
        out_spec = pl.BlockSpec((None, tk, tn), lambda m, n, s: (n, m, 0))
        out_shape = jax.ShapeDtypeStruct((N // tn, K1, tn), F32)
    else:
        out_spec = pl.BlockSpec((tk, tn), lambda m, n, s: (m, n))
        out_shape = jax.ShapeDtypeStruct((K1, N), F32)

    def body(a_ref, b_ref, o_ref):
        @pl.when(pl.program_id(2) == 0)
        def _():
            o_ref[...] = jnp.zeros_like(o_ref)

        o_ref[...] += _dot_tn(a_ref[...], b_ref[...].astype(MXU_DT))

    return _call(
        body, name=name, grid=(K1 // tk, N // tn, S // ts),
        in_specs=[pl.BlockSpec((ts, tk), lambda m, n, s: (s, m)),
                  pl.BlockSpec((ts, tn), lambda m, n, s: (s, n))],
        out_specs=out_spec, out_shape=out_shape,
        compiler_params=_params("parallel", "parallel", "arbitrary"),
    )(a, b)


def _mm_nt_rmsbwd(pairs, x, g, dres, name):
    S, D = x.shape
    tm = TM
    n = len(pairs)

    def body(*refs):
        a_refs = refs[:n]
        w_refs = refs[n:2 * n]
        x_ref, g_ref, r_ref, dx_ref, dg_ref = refs[2 * n:]
        dh = _dot_nt(a_refs[0][...], w_refs[0][...])
        for k in range(1, n):
            dh += _dot_nt(a_refs[k][...], w_refs[k][...])
        xv = x_ref[...]
        r = lax.rsqrt(jnp.mean(xv * xv, axis=-1, keepdims=True) + EPS)
        xhat = xv * r

        @pl.when(pl.program_id(0) == 0)
        def _():
            dg_ref[...] = jnp.zeros_like(dg_ref)

        dg_ref[...] += jnp.sum(dh * xhat, axis=0, keepdims=True)
        dxh = dh * g_ref[...]
        dx_ref[...] = r_ref[...] + r * (dxh - xhat * jnp.mean(dxh * xhat, axis=-1, keepdims=True))

    in_specs = ([_row_spec(tm, a.shape[1]) for a, _ in pairs] + [_full_spec(w.shape) for _, w in pairs]
                + [_row_spec(tm, D), _full_spec((1, D)), _row_spec(tm, D)])
    return _call(
        body, name=name, grid=(S // tm,), in_specs=in_specs,
        out_specs=[_row_spec(tm, D), _full_spec((1, D))],
        out_shape=[jax.ShapeDtypeStruct((S, D), F32), jax.ShapeDtypeStruct((1, D), F32)],
        compiler_params=_params("arbitrary"),
    )(*[a for a, _ in pairs], *[w for _, w in pairs], x, g, dres)


def _loss_head(x, g, tgt):
    S, D = x.shape
    tm = TM

    def body(x_ref, g_ref, t_ref, dx_ref, dg_ref, l_ref):
        xv = x_ref[...]
        r = lax.rsqrt(jnp.mean(xv * xv, axis=-1, keepdims=True) + EPS)
        xhat = xv * r
        diff = xhat * g_ref[...] - t_ref[...]

        @pl.when(pl.program_id(0) == 0)
        def _():
            dg_ref[...] = jnp.zeros_like(dg_ref)
            l_ref[...] = jnp.zeros_like(l_ref)

        l_ref[...] += jnp.full(l_ref.shape, 0.5 * jnp.sum(jnp.mean(diff * diff, axis=-1, keepdims=True)), F32)
        dout = diff * (1.0 / D)
        dg_ref[...] += jnp.sum(dout * xhat, axis=0, keepdims=True)
        dxh = dout * g_ref[...]
        dx_ref[...] = r * (dxh - xhat * jnp.mean(dxh * xhat, axis=-1, keepdims=True))

    return _call(
        body, name="loss_head", grid=(S // tm,),
        in_specs=[_row_spec(tm, D), _full_spec((1, D)), _row_spec(tm, D)],
        out_specs=[_row_spec(tm, D), _full_spec((1, D)), _full_spec((8, LANES))],
        out_shape=[jax.ShapeDtypeStruct((S, D), F32), jax.ShapeDtypeStruct((1, D), F32),
                   jax.ShapeDtypeStruct((8, LANES), F32)],
        compiler_params=_params("arbitrary"),
    )(x, g, tgt)


def _mix_out(ya, yb, yc, gm, wo, x, gmat):
    S = x.shape[0]
    tm = TM

    def body(ya_ref, yb_ref, yc_ref, gm_ref, wo_ref, x_ref, gmat_ref, x2_ref, yn_ref):
        y = jnp.concatenate([ya_ref[...], yb_ref[...], yc_ref[...]], axis=-1)
        r = lax.rsqrt(_group_mean(y * y, gmat_ref[...]) + EPS)
        yn = (y * r * gm_ref[...]).astype(MXU_DT)
        yn_ref[...] = yn
        x2_ref[...] = x_ref[...] + _dot(yn, wo_ref[...])

    return _call(
        body, name="mix_out", grid=(S // tm,),
        in_specs=[_row_spec(tm, W_A), _row_spec(tm, W_B), _row_spec(tm, W_C), _full_spec((1, D_MODEL)),
                  _full_spec((D_MODEL, D_MODEL)), _row_spec(tm, D_MODEL), _full_spec((LANES, LANES))],
        out_specs=[_row_spec(tm, D_MODEL), _row_spec(tm, D_MODEL)],
        out_shape=[jax.ShapeDtypeStruct((S, D_MODEL), F32), jax.ShapeDtypeStruct((S, D_MODEL), MXU_DT)],
        compiler_params=_params("parallel"),
    )(ya, yb, yc, gm, wo, x, gmat)


def _mix_out_bwd(dx2, wo, ya, yb, yc, gm, gmat):
    S = dx2.shape[0]
    tm = TM

    def body(dx2_ref, wo_ref, ya_ref, yb_ref, yc_ref, gm_ref, gmat_ref, dya_ref, dyb_ref, dyc_ref, dgm_ref):
        dyn = _dot_nt(dx2_ref[...].astype(MXU_DT), wo_ref[...])
        y = jnp.concatenate([ya_ref[...], yb_ref[...], yc_ref[...]], axis=-1)
        r = lax.rsqrt(_group_mean(y * y, gmat_ref[...]) + EPS)
        yhat = y * r

        @pl.when(pl.program_id(0) == 0)
        def _():
            dgm_ref[...] = jnp.zeros_like(dgm_ref)

        dgm_ref[...] += jnp.sum(dyn * yhat, axis=0, keepdims=True)
        dyh = dyn * gm_ref[...]
        dy = r * (dyh - yhat * _group_mean(dyh * yhat, gmat_ref[...]))
        dya_ref[...] = dy[:, :W_A]
        dyb_ref[...] = dy[:, W_A:W_A + W_B]
        dyc_ref[...] = dy[:, W_A + W_B:]

    return _call(
        body, name="mix_out_bwd", grid=(S // tm,),
        in_specs=[_row_spec(tm, D_MODEL), _full_spec((D_MODEL, D_MODEL)), _row_spec(tm, W_A), _row_spec(tm, W_B),
                  _row_spec(tm, W_C), _full_spec((1, D_MODEL)), _full_spec((LANES, LANES))],
        out_specs=[_row_spec(tm, W_A), _row_spec(tm, W_B), _row_spec(tm, W_C), _full_spec((1, D_MODEL))],
        out_shape=[jax.ShapeDtypeStruct((S, W_A), F32), jax.ShapeDtypeStruct((S, W_B), F32),
                   jax.ShapeDtypeStruct((S, W_C), F32), jax.ShapeDtypeStruct((1, D_MODEL), F32)],
        compiler_params=_params("arbitrary"),
    )(dx2, wo, ya, yb, yc, gm, gmat)


_SQRT_HALF = 0.7071067811865476
_INV_SQRT_2PI = 0.3989422804014327


def _sgu_common(a, sng, wm_ref, bias, gmat):
    phi = 0.5 * (1.0 + lax.erf(a * _SQRT_HALF))
    ga = a * phi
    u = ga[:, :W_A]
    v = ga[:, W_A:]
    r = lax.rsqrt(_group_mean(v * v, gmat) + EPS)
    vhat = v * r
    vn = (vhat * sng).astype(MXU_DT)
    head = lax.broadcasted_iota(jnp.int32, (CHUNK, W_A), 1) // HEAD_DIM
    rows = []
    for c in range(a.shape[0] // CHUNK):
        vc = vn[c * CHUNK:(c + 1) * CHUNK]
        s = bias
        for h in range(4):
            s = s + jnp.where(head == h, _dot(wm_ref[h], vc), 0.0)
        rows.append(s)
    s = jnp.concatenate(rows, axis=0)
    return phi, u, r, vhat, vn, s


def _tril_weights(sgu_w_l):
    t = jnp.arange(CHUNK)
    return jnp.where((t[None, :] <= t[:, None])[None], sgu_w_l, 0.0)


def _sgu_fwd(proj, sng, wm, bias, gmat):
    S = proj.shape[0]
    tm = TM

    def body(a_ref, sng_ref, wm_ref, b_ref, gmat_ref, y_ref):
        _, u, _, _, _, s = _sgu_common(a_ref[...], sng_ref[...], wm_ref, b_ref[...], gmat_ref[...])
        y_ref[...] = u * s

    return _call(
        body, name="sgu_fwd", grid=(S // tm,),
        in_specs=[_row_spec(tm, 2 * W_A), _full_spec((1, W_A)), _full_spec((4, CHUNK, CHUNK)),
                  _full_spec((CHUNK, W_A)), _full_spec((LANES, LANES))],
        out_specs=_row_spec(tm, W_A),
        out_shape=jax.ShapeDtypeStruct((S, W_A), F32),
        compiler_params=_params("parallel"),
    )(proj, sng, wm, bias, gmat)


def _sgu_bwd(proj, dy, sng, wm, wmt, bias, gmat):
    S = proj.shape[0]
    tm = TM

    def body(a_ref, dy_ref, sng_ref, wm_ref, wmt_ref, b_ref, gmat_ref, da_ref, dw_ref, db_ref, dsng_ref):
        a = a_ref[...]
        dy = dy_ref[...]
        gmat = gmat_ref[...]
        sng = sng_ref[...]
        phi, u, r, vhat, vn, s = _sgu_common(a, sng, wm_ref, b_ref[...], gmat)
        du = dy * s
        ds = dy * u

        @pl.when(pl.program_id(0) == 0)
        def _():
            dw_ref[...] = jnp.zeros_like(dw_ref)
            db_ref[...] = jnp.zeros_like(db_ref)
            dsng_ref[...] = jnp.zeros_like(dsng_ref)

        head = lax.broadcasted_iota(jnp.int32, (CHUNK, W_A), 1) // HEAD_DIM
        tt = lax.broadcasted_iota(jnp.int32, (CHUNK, CHUNK), 0)
        ss = lax.broadcasted_iota(jnp.int32, (CHUNK, CHUNK), 1)
        rows = []
        for c in range(tm // CHUNK):
            dsc = ds[c * CHUNK:(c + 1) * CHUNK]
            vc = vn[c * CHUNK:(c + 1) * CHUNK]
            db_ref[...] += dsc
            dsb = dsc.astype(MXU_DT)
            dvn = jnp.zeros((CHUNK, W_A), F32)
            for h in range(4):
                dvn = dvn + jnp.where(head == h, _dot(wmt_ref[h], dsb), 0.0)
                dsh = jnp.where(head == h, dsc, 0.0).astype(MXU_DT)
                dw_ref[h] += jnp.where(ss <= tt, _dot_nt(dsh, vc), 0.0)
            rows.append(dvn)
        dvn = jnp.concatenate(rows, axis=0)
        dsng_ref[...] += jnp.sum(dvn * vhat, axis=0, keepdims=True)
        dvh = dvn * sng
        dv = r * (dvh - vhat * _group_mean(dvh * vhat, gmat))
        dga = jnp.concatenate([du, dv], axis=-1)
        dgelu = phi + a * (_INV_SQRT_2PI * jnp.exp(-0.5 * a * a))
        da_ref[...] = (dga * dgelu).astype(da_ref.dtype)

    return _call(
        body, name="sgu_bwd", grid=(S // tm,),
        in_specs=[_row_spec(tm, 2 * W_A), _row_spec(tm, W_A), _full_spec((1, W_A)), _full_spec((4, CHUNK, CHUNK)),
                  _full_spec((4, CHUNK, CHUNK)), _full_spec((CHUNK, W_A)), _full_spec((LANES, LANES))],
        out_specs=[_row_spec(tm, 2 * W_A), _full_spec((4, CHUNK, CHUNK)), _full_spec((CHUNK, W_A)),
                   _full_spec((1, W_A))],
        out_shape=[jax.ShapeDtypeStruct((S, 2 * W_A), MXU_DT), jax.ShapeDtypeStruct((4, CHUNK, CHUNK), F32),
                   jax.ShapeDtypeStruct((CHUNK, W_A), F32), jax.ShapeDtypeStruct((1, W_A), F32)],
        compiler_params=_params("arbitrary"),
    )(proj, dy, sng, wm, wmt, bias, gmat)


Q_BLK0 = (2 * W_A) // LANES
K_BLK0 = Q_BLK0 + W_B // LANES
V_BLK0 = K_BLK0 + W_B // LANES
N_PAIRS = W_B // LANES
EXP_IS_ZERO_BELOW = -120.0


def _tri_matrix():
    r = jnp.arange(TK)
    return (r[:, None] > r[None, :]).astype(MXU_DT)


def _stack_heads(a):
    lane = lax.broadcasted_iota(jnp.int32, a.shape, 1)
    return jnp.concatenate([jnp.where(lane < HEAD_DIM, a, 0.0), jnp.where(lane >= HEAD_DIM, a, 0.0)],
                           axis=0).astype(MXU_DT)


def _unstack_heads(a):
    lane = lax.broadcasted_iota(jnp.int32, (TQ, LANES), 1)
    return jnp.where(lane < HEAD_DIM, a[:TQ], a[TQ:])


def _sb_scores(q2, kj, tri, key_offset):
    z = _dot_nt(q2, kj)
    sp = jnp.log(1.0 + jnp.exp(-jnp.abs(z)))
    lsp = jnp.minimum(z, 0.0) - sp
    lsm = lsp - z
    msk = None
    if key_offset is not None:
        row = lax.broadcasted_iota(jnp.int32, z.shape, 0) & (TQ - 1)
        col = lax.broadcasted_iota(jnp.int32, z.shape, 1) + key_offset
        msk = col < row
        lsm = jnp.where(msk, lsm, 0.0)
    tail = _dot(lsm.astype(MXU_DT), tri)
    return lsp, lsm, tail, msk


def _sb_fwd(proj_b, tri):
    S = proj_b.shape[0]
    nq = S // TQ
    kpq = TQ // TK
    assert S // TK < LANES

    def body(q_ref, k_ref, v_ref, tri_ref, o_ref, rb_ref, acc_ref):
        i = pl.program_id(1)
        lane2 = lax.broadcasted_iota(jnp.int32, (2 * TQ, LANES), 1)
        q2 = _stack_heads(q_ref[...].astype(F32) * (HEAD_DIM ** -0.5))
        tri = tri_ref[...]
        rb_ref[...] = jnp.zeros_like(rb_ref)

        def block(j, run, key_offset=None, first=False):
            start = pl.multiple_of(j * TK, TK)
            kj = k_ref[pl.ds(start, TK), :]
            vj = v_ref[pl.ds(start, TK), :]
            lsp, lsm, tail, msk = _sb_scores(q2, kj, tri, key_offset)
            rb_ref[...] = jnp.where(lane2 == j, run, rb_ref[...])
            att = jnp.exp(lsp + tail + run)
            if msk is not None:
                att = jnp.where(msk, att, 0.0)
            pv = _dot(att.astype(MXU_DT), vj)
            if first:
                acc_ref[...] = pv
            else:
                acc_ref[...] += pv
            return run + tail[:, :1] + lsm[:, :1]

        run = jnp.zeros((2 * TQ, 1), F32)
        for d in reversed(range(kpq)):
            run = block(i * kpq + d, run, key_offset=d * TK, first=(d == kpq - 1))
        past = i * kpq

        def alive(run):
            return (jnp.max(run) > EXP_IS_ZERO_BELOW).astype(jnp.int32)

        def step(carry):
            n, run, _ = carry
            run = block(past - 1 - n, run)
            return n + 1, run, alive(run)

        n, _, _ = lax.while_loop(lambda c: jnp.logical_and(c[0] < past, c[2] > 0), step,
                                 (jnp.int32(0), run, alive(run)))
        rb_ref[...] = jnp.where(lane2 == LANES - 1, n.astype(F32), rb_ref[...])
        o_ref[...] = _unstack_heads(acc_ref[...])

    return _call(
        body, name="sb_fwd", grid=(N_PAIRS, nq),
        in_specs=[pl.BlockSpec((TQ, LANES), lambda p, i: (i, Q_BLK0 + p)),
                  pl.BlockSpec((S, LANES), lambda p, i: (0, K_BLK0 + p)),
                  pl.BlockSpec((S, LANES), lambda p, i: (0, V_BLK0 + p)),
                  pl.BlockSpec((TK, TK), lambda p, i: (0, 0))],
        out_specs=[pl.BlockSpec((TQ, LANES), lambda p, i: (i, p)),
                   pl.BlockSpec((None, None, 2 * TQ, LANES), lambda p, i: (p, i, 0, 0))],
        out_shape=[jax.ShapeDtypeStruct((S, W_B), F32), jax.ShapeDtypeStruct((N_PAIRS, nq, 2 * TQ, LANES), F32)],
        scratch_shapes=[pltpu.VMEM((2 * TQ, LANES), F32)],
        compiler_params=_params("parallel", "arbitrary"),
    )(proj_b, proj_b, proj_b, tri)


def _sb_bwd(proj_b, dyb, rb, tri, trit):
    S = proj_b.shape[0]
    nq = S // TQ

    kpq = TQ // TK

    def body(q_ref, k_ref, v_ref, do_ref, rb_ref, tri_ref, trit_ref, dq_ref, dk_ref, dv_ref,
             dq_acc, dk_acc, dv_acc):
        i = pl.program_id(1)
        lane2 = lax.broadcasted_iota(jnp.int32, (2 * TQ, LANES), 1)
        scale = HEAD_DIM ** -0.5
        q2 = _stack_heads(q_ref[...].astype(F32) * scale)
        do2 = _stack_heads(do_ref[...])
        tri = tri_ref[...]
        trit = trit_ref[...]

        @pl.when(i == 0)
        def _():
            dk_acc[...] = jnp.zeros_like(dk_acc)
            dv_acc[...] = jnp.zeros_like(dv_acc)

        dq_acc[...] = jnp.zeros_like(dq_acc)

        def block(j, pre, key_offset=None):
            start = pl.multiple_of(j * TK, TK)
            kj = k_ref[pl.ds(start, TK), :]
            vj = v_ref[pl.ds(start, TK), :]
            lsp, lsm, tail, msk = _sb_scores(q2, kj, tri, key_offset)
            run = jnp.sum(jnp.where(lane2 == j, rb_ref[...], 0.0), axis=-1, keepdims=True)
            att = jnp.exp(lsp + tail + run)
            if msk is not None:
                att = jnp.where(msk, att, 0.0)
            beta = jnp.exp(lsp)
            dl = _dot_nt(do2, vj) * att
            cin = _dot(dl.astype(MXU_DT), trit)
            dz = dl * (1.0 - beta) - beta * (pre + cin)
            if msk is not None:
                dz = jnp.where(msk, dz, 0.0)
            dzb = dz.astype(MXU_DT)
            dq_acc[...] += _dot(dzb, kj)
            dk_acc[pl.ds(start, TK), :] += _dot_tn(dzb, q2)
            dv_acc[pl.ds(start, TK), :] += _dot_tn(att.astype(MXU_DT), do2)
            return pre + cin[:, TK - 1:] + dl[:, TK - 1:]

        past = i * kpq
        walked = jnp.max(jnp.where(lane2[:8] == LANES - 1, rb_ref[pl.ds(0, 8), :], 0.0)).astype(jnp.int32)
        walked = jnp.clip(walked, 0, past)
        pre = lax.fori_loop(past - walked, past, lambda j, pre: block(j, pre), jnp.zeros((2 * TQ, 1), F32))
        for d in range(kpq):
            pre = block(i * kpq + d, pre, key_offset=d * TK)
        dq_ref[...] = (_unstack_heads(dq_acc[...]) * scale).astype(dq_ref.dtype)

        @pl.when(i == nq - 1)
        def _():
            dk_ref[...] = dk_acc[...].astype(dk_ref.dtype)
            dv_ref[...] = dv_acc[...].astype(dv_ref.dtype)

    return _call(
        body, name="sb_bwd", grid=(N_PAIRS, nq),
        in_specs=[pl.BlockSpec((TQ, LANES), lambda p, i: (i, Q_BLK0 + p)),
                  pl.BlockSpec((S, LANES), lambda p, i: (0, K_BLK0 + p)),
                  pl.BlockSpec((S, LANES), lambda p, i: (0, V_BLK0 + p)),
                  pl.BlockSpec((TQ, LANES), lambda p, i: (i, p)),
                  pl.BlockSpec((None, None, 2 * TQ, LANES), lambda p, i: (p, i, 0, 0)),
                  pl.BlockSpec((TK, TK), lambda p, i: (0, 0)),
                  pl.BlockSpec((TK, TK), lambda p, i: (0, 0))],
        out_specs=[pl.BlockSpec((TQ, LANES), lambda p, i: (i, p)),
                   pl.BlockSpec((S, LANES), lambda p, i: (0, p)),
                   pl.BlockSpec((S, LANES), lambda p, i: (0, p))],
        out_shape=[jax.ShapeDtypeStruct((S, W_B), MXU_DT)] * 3,
        scratch_shapes=[pltpu.VMEM((2 * TQ, LANES), F32), pltpu.VMEM((S, LANES), F32), pltpu.VMEM((S, LANES), F32)],
        compiler_params=_params("parallel", "arbitrary"),
    )(proj_b, proj_b, proj_b, dyb, rb, tri, trit)


P_BLK = (2 * W_A + 3 * W_B) // W_C


def _window_lanes():
    g = lax.broadcasted_iota(jnp.int32, (1, W_C), 1) // (W_C // 4)
    w = jnp.where(g == 0, POOL_WINDOWS[0], jnp.where(g == 1, POOL_WINDOWS[1],
                  jnp.where(g == 2, POOL_WINDOWS[2], POOL_WINDOWS[3])))
    return g, w


def _shift_rows(ext, k, tm, lead):
    n = ext.shape[0]
    return pltpu.roll(ext, shift=k % n, axis=0)[lead:lead + tm]


def _pool_diff(p_cur, p_halo, row0, tm):
    ext = jnp.concatenate([p_halo, p_cur], axis=0)
    g, w = _window_lanes()
    acc = ext
    sums = []
    for sh in (1, 2, 4, 8):
        acc = acc + pltpu.roll(acc, shift=sh, axis=0)
        sums.append(acc[HALO:HALO + tm])
    wsum = jnp.where(g == 0, sums[0], jnp.where(g == 1, sums[1], jnp.where(g == 2, sums[2], sums[3])))
    pos = (row0 + 1 + lax.broadcasted_iota(jnp.int32, (tm, W_C), 0)).astype(F32)
    cnt = jnp.minimum(pos, w.astype(F32))
    return wsum / cnt - p_cur, cnt


def _pool_specs(tm, nrow_blocks_halo):
    cur = pl.BlockSpec((tm, W_C), lambda i: (i, P_BLK))
    prev = pl.BlockSpec((HALO, W_C), lambda i: (jnp.maximum(i * (tm // HALO) - 1, 0), P_BLK))
    return cur, prev


def _pool_fwd(proj, wbd, scale):
    S = proj.shape[0]
    tm = TM

    def body(p_ref, ph_ref, w_ref, sc_ref, y_ref):
        i = pl.program_id(0)
        halo = jnp.where(i > 0, ph_ref[...], 0.0)
        d, _ = _pool_diff(p_ref[...], halo, i * tm, tm)
        y_ref[...] = _dot(d.astype(MXU_DT), w_ref[...]) * sc_ref[...]

    cur, prev = _pool_specs(tm, S // HALO)
    return _call(
        body, name="pool_fwd", grid=(S // tm,),
        in_specs=[cur, prev, _full_spec((W_C, W_C)), _full_spec((1, W_C))],
        out_specs=_row_spec(tm, W_C),
        out_shape=jax.ShapeDtypeStruct((S, W_C), F32),
        compiler_params=_params("parallel"),
    )(proj, proj, wbd, scale)


def _pool_bwd_a(proj, dy, wbd, scale):
    S = proj.shape[0]
    tm = TM

    def body(p_ref, ph_ref, dy_ref, w_ref, sc_ref, dd_ref, e_ref, dw_ref, dsc_ref):
        i = pl.program_id(0)
        halo = jnp.where(i > 0, ph_ref[...], 0.0)
        d, cnt = _pool_diff(p_ref[...], halo, i * tm, tm)
        db = d.astype(MXU_DT)
        dy = dy_ref[...]

        @pl.when(i == 0)
        def _():
            dw_ref[...] = jnp.zeros_like(dw_ref)
            dsc_ref[...] = jnp.zeros_like(dsc_ref)

        dsc_ref[...] += jnp.sum(dy * _dot(db, w_ref[...]), axis=0, keepdims=True)
        dys = (dy * sc_ref[...]).astype(MXU_DT)
        dw_ref[...] += _dot_tn(db, dys)
        dd = _dot_nt(dys, w_ref[...])
        dd_ref[...] = dd
        e_ref[...] = dd / cnt

    cur, prev = _pool_specs(tm, S // HALO)
    return _call(
        body, name="pool_bwd_a", grid=(S // tm,),
        in_specs=[cur, prev, _row_spec(tm, W_C), _full_spec((W_C, W_C)), _full_spec((1, W_C))],
        out_specs=[_row_spec(tm, W_C), _row_spec(tm, W_C), _full_spec((W_C, W_C)), _full_spec((1, W_C))],
        out_shape=[jax.ShapeDtypeStruct((S, W_C), F32), jax.ShapeDtypeStruct((S, W_C), F32),
                   jax.ShapeDtypeStruct((W_C, W_C), F32), jax.ShapeDtypeStruct((1, W_C), F32)],
        compiler_params=_params("arbitrary"),
    )(proj, proj, dy, wbd, scale)


def _pool_bwd_b(dd, e):
    S = dd.shape[0]
    tm = TM
    nb = S // tm

    def body(dd_ref, e_ref, en_ref, dp_ref):
        i = pl.program_id(0)
        halo = jnp.where(i < nb - 1, en_ref[...], 0.0)
        ext = jnp.concatenate([e_ref[...], halo], axis=0)
        n = ext.shape[0]
        g, _ = _window_lanes()
        acc = ext
        sums = []
        for sh in (1, 2, 4, 8):
            acc = acc + pltpu.roll(acc, shift=n - sh, axis=0)
            sums.append(acc[:tm])
        wsum = jnp.where(g == 0, sums[0], jnp.where(g == 1, sums[1], jnp.where(g == 2, sums[2], sums[3])))
        dp_ref[...] = (wsum - dd_ref[...]).astype(dp_ref.dtype)

    nxt = pl.BlockSpec((HALO, W_C), lambda i: (jnp.minimum((i + 1) * (tm // HALO), S // HALO - 1), 0))
    return _call(
        body, name="pool_bwd_b", grid=(nb,),
        in_specs=[_row_spec(tm, W_C), _row_spec(tm, W_C), nxt],
        out_specs=_row_spec(tm, W_C),
        out_shape=jax.ShapeDtypeStruct((S, W_C), MXU_DT),
        compiler_params=_params("parallel"),
    )(dd, e, e)


TN_FF = 1408
NB_FF = D_FF // TN_FF
CONV_ROWS = 8


def _conv(z_cur, z_halo, cwb, tm):
    ext = jnp.concatenate([z_halo, z_cur], axis=0)
    z2 = _shift_rows(ext, 2, tm, HALO)
    z1 = _shift_rows(ext, 1, tm, HALO)
    zc = cwb[3:4] + z2 * cwb[0:1] + z1 * cwb[1:2] + z_cur * cwb[2:3]
    return zc, z2, z1


def _ffn_specs(tm, order):
    def mk(f):
        return (lambda i, j: f(i, j)) if order == "ij" else (lambda j, i: f(i, j))
    hb = tm // HALO
    return [
        pl.BlockSpec((tm, TN_FF), mk(lambda i, j: (i, j))),
        pl.BlockSpec((tm, TN_FF), mk(lambda i, j: (i, j + NB_FF))),
        pl.BlockSpec((HALO, TN_FF), mk(lambda i, j: (jnp.maximum(i * hb - 1, 0), j))),
        pl.BlockSpec((HALO, TN_FF), mk(lambda i, j: (jnp.maximum(i * hb - 1, 0), j + NB_FF))),
        pl.BlockSpec((CONV_ROWS, TN_FF), mk(lambda i, j: (0, j))),
        pl.BlockSpec((CONV_ROWS, TN_FF), mk(lambda i, j: (0, j + NB_FF))),
    ]


def _conv_gate(z, cwb):
    S = z.shape[0]
    tm = TM

    def body(zg_ref, zu_ref, hg_ref, hu_ref, cg_ref, cu_ref, f_ref):
        first = pl.program_id(0) == 0
        g, _, _ = _conv(zg_ref[...].astype(F32), jnp.where(first, 0.0, hg_ref[...].astype(F32)), cg_ref[...], tm)
        u, _, _ = _conv(zu_ref[...].astype(F32), jnp.where(first, 0.0, hu_ref[...].astype(F32)), cu_ref[...], tm)
        f_ref[...] = (g * jax.nn.sigmoid(g) * u).astype(f_ref.dtype)

    return _call(
        body, name="conv_gate", grid=(S // tm, NB_FF), in_specs=_ffn_specs(tm, "ij"),
        out_specs=pl.BlockSpec((tm, TN_FF), lambda i, j: (i, j)),
        out_shape=jax.ShapeDtypeStruct((S, D_FF), MXU_DT),
        compiler_params=_params("parallel", "parallel"),
    )(z, z, z, z, cwb, cwb)


def _conv_gate_bwd(z, df, cwb):
    S = z.shape[0]
    tm = TM

    def body(zg_ref, zu_ref, hg_ref, hu_ref, cg_ref, cu_ref, df_ref, dg_ref, du_ref, dcg_ref, dcu_ref):
        i = pl.program_id(1)
        first = i == 0
        zg = zg_ref[...].astype(F32)
        zu = zu_ref[...].astype(F32)
        g, g2, g1 = _conv(zg, jnp.where(first, 0.0, hg_ref[...].astype(F32)), cg_ref[...], tm)
        u, u2, u1 = _conv(zu, jnp.where(first, 0.0, hu_ref[...].astype(F32)), cu_ref[...], tm)
        sg = jax.nn.sigmoid(g)
        df = df_ref[...].astype(F32)
        dgv = df * u * (sg * (1.0 + g * (1.0 - sg)))
        duv = df * (g * sg)
        dg_ref[...] = dgv.astype(dg_ref.dtype)
        du_ref[...] = duv.astype(du_ref.dtype)

        @pl.when(first)
        def _():
            dcg_ref[...] = jnp.zeros_like(dcg_ref)
            dcu_ref[...] = jnp.zeros_like(dcu_ref)

        rid = lax.broadcasted_iota(jnp.int32, (CONV_ROWS, TN_FF), 0)

        def taps(dv, s2, s1, s0):
            sums = [jnp.sum(dv * s2, axis=0, keepdims=True), jnp.sum(dv * s1, axis=0, keepdims=True),
                    jnp.sum(dv * s0, axis=0, keepdims=True), jnp.sum(dv, axis=0, keepdims=True)]
            out = jnp.zeros((CONV_ROWS, TN_FF), F32)
            for k, v in enumerate(sums):
                out = jnp.where(rid == k, v, out)
            return out

        dcg_ref[...] += taps(dgv, g2, g1, zg)
        dcu_ref[...] += taps(duv, u2, u1, zu)

    acc = pl.BlockSpec((CONV_ROWS, TN_FF), lambda j, i: (0, j))
    tile = pl.BlockSpec((tm, TN_FF), lambda j, i: (i, j))
    return _call(
        body, name="conv_gate_bwd", grid=(NB_FF, S // tm), in_specs=_ffn_specs(tm, "ji") + [tile],
        out_specs=[tile, tile, acc, acc],
        out_shape=[jax.ShapeDtypeStruct((S, D_FF), MXU_DT), jax.ShapeDtypeStruct((S, D_FF), MXU_DT),
                   jax.ShapeDtypeStruct((CONV_ROWS, D_FF), F32), jax.ShapeDtypeStruct((CONV_ROWS, D_FF), F32)],
        compiler_params=_params("parallel", "arbitrary"),
    )(z, z, z, z, cwb, cwb, df)


def _conv_transpose(dzc, cwb_half, name):
    S = dzc.shape[0]
    tm = TM
    nb = S // tm

    def body(d_ref, dn_ref, c_ref, o_ref):
        last = pl.program_id(0) == nb - 1
        cur = d_ref[...].astype(F32)
        ext = jnp.concatenate([cur, jnp.where(last, 0.0, dn_ref[...].astype(F32))], axis=0)
        c = c_ref[...]
        o_ref[...] = (cur * c[2:3] + _shift_rows(ext, -1, tm, 0) * c[1:2]
                      + _shift_rows(ext, -2, tm, 0) * c[0:1]).astype(o_ref.dtype)

    hb = tm // HALO
    return _call(
        body, name=name, grid=(nb, NB_FF),
        in_specs=[pl.BlockSpec((tm, TN_FF), lambda i, j: (i, j)),
                  pl.BlockSpec((HALO, TN_FF), lambda i, j: (jnp.minimum((i + 1) * hb, S // HALO - 1), j)),
                  pl.BlockSpec((CONV_ROWS, TN_FF), lambda i, j: (0, j))],
        out_specs=pl.BlockSpec((tm, TN_FF), lambda i, j: (i, j)),
        out_shape=jax.ShapeDtypeStruct((S, D_FF), MXU_DT),
        compiler_params=_params("parallel", "parallel"),
    )(dzc, dzc, cwb_half)


def _layer_consts(w, l):
    wm = _tril_weights(w["sgu_w"][l])
    eye = jnp.eye(4, dtype=F32)
    wbd = (w["pool_w"][l][:, :, None, :] * eye[:, None, :, None]).reshape(W_C, W_C)
    cwb = jnp.concatenate([w["conv_w"][l], w["conv_b"][l][None], jnp.zeros((CONV_ROWS - 4, 2 * D_FF), F32)], axis=0)
    return dict(
        g1=w["norm1_g"][l][None], g2=w["norm2_g"][l][None], gm=w["mix_norm_g"][l][None],
        sng=w["sgu_norm_g"][l][None], wm=wm.astype(MXU_DT), wmt=jnp.swapaxes(wm, 1, 2).astype(MXU_DT),
        bias=jnp.repeat(jnp.transpose(w["sgu_b"][l]), HEAD_DIM, axis=1),
        wbd=wbd.astype(MXU_DT), scale=w["pool_scale"][l][None], cwb=cwb,
        w_in=w["w_in"][l], w_o=w["w_o"][l], w_up=w["w_up"][l], w_down=w["w_down"][l],
    )


def _local_step(x, tgt, w):
    gmat = _group_matrix()
    tri = _tri_matrix()
    trit = jnp.transpose(tri)
    saved = []
    for l in range(DEPTH):
        c = _layer_consts(w, l)
        h1, proj, proj_b = _rms_mm(x, c["g1"], c["w_in"], IN_COLS // 3, "in_proj", (F32, MXU_DT))
        ya = _sgu_fwd(proj, c["sng"], c["wm"], c["bias"], gmat)
        yb, rb = _sb_fwd(proj_b, tri)
        yc = _pool_fwd(proj, c["wbd"], c["scale"])
        x2, yn = _mix_out(ya, yb, yc, c["gm"], c["w_o"], x, gmat)
        h2, z = _rms_mm(x2, c["g2"], c["w_up"], TN_FF, "up_proj", (MXU_DT,))
        f = _conv_gate(z, c["cwb"])
        x3 = _mm_res(f, c["w_down"], x2, "down_proj")
        saved.append(dict(c=c, x=x, proj=proj, proj_b=proj_b, h1=h1, ya=ya, yb=yb, yc=yc, rb=rb, x2=x2, yn=yn,
                          z=z, h2=h2, f=f))
        x = x3

    dx, d_final_g, loss8 = _loss_head(x, w["final_g"][None], tgt)
    grads = {n: [None] * DEPTH for n in ("norm1_g", "w_in", "sgu_norm_g", "sgu_w", "sgu_b", "pool_w", "pool_scale",
                                         "mix_norm_g", "w_o", "norm2_g", "w_up", "conv_w", "conv_b", "w_down")}
    for l in reversed(range(DEPTH)):
        s = saved[l]
        c = s["c"]
        df = _mm_nt(dx, c["w_down"], "down_proj_bwd")
        grads["w_down"][l] = _mm_tn(s["f"], dx, "down_proj_wgrad").reshape(N_CHIPS, D_FF // N_CHIPS, D_MODEL)
        dzg, dzu, dcg, dcu = _conv_gate_bwd(s["z"], df, c["cwb"])
        dz_g = _conv_transpose(dzg, c["cwb"][:, :D_FF], "conv_t_gate")
        dz_u = _conv_transpose(dzu, c["cwb"][:, D_FF:], "conv_t_value")
        dcwb = jnp.concatenate([dcg, dcu], axis=1)
        grads["conv_w"][l] = dcwb[:3]
        grads["conv_b"][l] = dcwb[3]
        grads["w_up"][l] = jnp.concatenate([_mm_tn(s["h2"], dz_g, "up_proj_wgrad_gate", col_tiles=True),
                                            _mm_tn(s["h2"], dz_u, "up_proj_wgrad_value", col_tiles=True)], axis=0)
        dx2, dg2 = _mm_nt_rmsbwd([(dz_g, c["w_up"][:, :D_FF]), (dz_u, c["w_up"][:, D_FF:])],
                                 s["x2"], c["g2"], dx, "up_proj_bwd")
        grads["norm2_g"][l] = dg2[0]
        grads["w_o"][l] = _mm_tn(s["yn"], dx2, "out_proj_wgrad").reshape(N_CHIPS, D_MODEL // N_CHIPS, D_MODEL)
        dya, dyb, dyc, dgm = _mix_out_bwd(dx2, c["w_o"], s["ya"], s["yb"], s["yc"], c["gm"], gmat)
        grads["mix_norm_g"][l] = dgm[0]
        dd, e, dwbd, dscale = _pool_bwd_a(s["proj"], dyc, c["wbd"], c["scale"])
        dp = _pool_bwd_b(dd, e)
        grads["pool_w"][l] = jnp.stack([dwbd[g * 64:(g + 1) * 64, g * 64:(g + 1) * 64] for g in range(4)])
        grads["pool_scale"][l] = dscale[0]
        dq, dk, dv = _sb_bwd(s["proj_b"], dyb, s["rb"], tri, trit)
        da, dwm, dbias, dsng = _sgu_bwd(s["proj"], dya, c["sng"], c["wm"], c["wmt"], c["bias"], gmat)
        grads["sgu_w"][l] = dwm
        grads["sgu_b"][l] = jnp.transpose(jnp.sum(dbias.reshape(CHUNK, 4, HEAD_DIM), axis=-1))
        grads["sgu_norm_g"][l] = dsng[0]
        dproj = jnp.concatenate([da, dq, dk, dv, dp], axis=1)
        dw_in = _mm_tn(s["h1"], dproj, "in_proj_wgrad")
        grads["w_in"][l] = jnp.transpose(dw_in.reshape(D_MODEL, N_CHIPS, IN_COLS // N_CHIPS), (1, 0, 2))
        dx, dg1 = _mm_nt_rmsbwd([(dproj, c["w_in"])], s["x"], c["g1"], dx2, "in_proj_bwd")
        grads["norm1_g"][l] = dg1[0]

    out = {n: jnp.stack(v) for n, v in grads.items()}
    out["final_g"] = d_final_g[0]
    return loss8[0, 0], dx, out


MESH = pl.DeviceIdType.MESH
ANY = pl.BlockSpec(memory_space=pl.ANY)


def _all_gather(block):
    m_per, n = block.shape[0] // 2, block.shape[1]

    def body(x_ref, out_ref, send_sems, recv_sems):
        x, y, c = lax.axis_index("x"), lax.axis_index("y"), lax.axis_index("c")
        me, sibling = (x, y, c), (x, y, 1 - c)
        chips = [(1 - x, y), (x, 1 - y), (1 - x, 1 - y)]
        my_half = x_ref.at[pl.ds(c * m_per, m_per), :]

        def rows(px, py, pc):
            return out_ref.at[pl.ds((4 * px + 2 * py + pc) * m_per, m_per), :]

        def copy(k, half, to, src=None):
            return pltpu.make_async_remote_copy(
                src_ref=rows(*half) if src is None else src, dst_ref=rows(*half),
                send_sem=send_sems.at[k], recv_sem=recv_sems.at[k], device_id=to, device_id_type=MESH)

        first = [copy(j, me, (*chip, c), src=my_half) for j, chip in enumerate(chips)]
        for cp in first:
            cp.start()
        passed = [copy(3 + j, (*chip, c), sibling) for j, chip in enumerate(chips)]
        for j, chip in enumerate(chips):
            copy(j, (*chip, c), me).wait_recv()
            passed[j].start()
        for j, chip in enumerate(chips):
            copy(3 + j, (*chip, 1 - c), me).wait_recv()
        for cp in first + passed:
            cp.wait_send()

    return _call(
        body, name="weight_all_gather",
        out_shape=jax.ShapeDtypeStruct((8 * m_per, n), block.dtype),
        in_specs=[ANY], out_specs=ANY,
        scratch_shapes=[pltpu.SemaphoreType.DMA((6,)), pltpu.SemaphoreType.DMA((6,))],
    )(block)


def _row_tile(r):
    return r if r <= 704 else 256


def _grad_swap(bigs, sp):
    n = len(bigs)

    def body(*refs):
        ins, outs = refs[:n + 1], refs[n + 1:2 * n + 2]
        send_sems, recv_sems = refs[2 * n + 2:]
        x, y, c = lax.axis_index("x"), lax.axis_index("y"), lax.axis_index("c")
        srcs = [ins[a].at[1 - c] for a in range(n)] + [ins[n].at[:, pl.ds((1 - c) * SP_HALF, SP_HALF), :]]
        copies = [pltpu.make_async_remote_copy(src_ref=srcs[a], dst_ref=outs[a], send_sem=send_sems.at[a],
                                               recv_sem=recv_sems.at[a], device_id=(x, y, 1 - c),
                                               device_id_type=MESH) for a in range(n + 1)]
        for cp in copies:
            cp.start()
        for cp in copies:
            cp.wait()

    shapes = [jax.ShapeDtypeStruct(b.shape[1:], b.dtype) for b in bigs]
    shapes.append(jax.ShapeDtypeStruct((N_CHIPS, SP_HALF, D_MODEL), sp.dtype))
    return _call(
        body, name="grad_swap_cores", out_shape=shapes, in_specs=[ANY] * (n + 1), out_specs=[ANY] * (n + 1),
        scratch_shapes=[pltpu.SemaphoreType.DMA((n + 1,)), pltpu.SemaphoreType.DMA((n + 1,))],
    )(*bigs, sp)


def _pair_add(g, r, c_arr, name, out_dtype):
    _, k, rr, cc = g.shape
    tr = _row_tile(rr)

    def body(c_ref, g_ref, r_ref, o_ref):
        o_ref[...] = (g_ref[...] + r_ref[...]).astype(o_ref.dtype)

    spec = pl.BlockSpec((1, tr, cc), lambda kk, i, c_ref: (kk, i, 0))
    grid_spec = pltpu.PrefetchScalarGridSpec(
        num_scalar_prefetch=1, grid=(k, rr // tr),
        in_specs=[pl.BlockSpec((None, 1, tr, cc), lambda kk, i, c_ref: (c_ref[0], kk, i, 0)), spec], out_specs=spec)
    return _call(body, name=name, grid_spec=grid_spec, out_shape=jax.ShapeDtypeStruct((k, rr, cc), out_dtype),
                 compiler_params=_params("parallel", "parallel"))(c_arr, g, r)


def _pair_add_small(sp, r, c_arr):
    def body(c_ref, g_ref, r_ref, o_ref):
        o_ref[...] = g_ref[...] + r_ref[...]

    spec = pl.BlockSpec((1, SP_HALF, D_MODEL), lambda kk, c_ref: (kk, 0, 0))
    grid_spec = pltpu.PrefetchScalarGridSpec(
        num_scalar_prefetch=1, grid=(N_CHIPS,),
        in_specs=[pl.BlockSpec((1, SP_HALF, D_MODEL), lambda kk, c_ref: (kk, c_ref[0], 0)), spec], out_specs=spec)
    return _call(body, name="grad_add_cores_small", grid_spec=grid_spec,
                 out_shape=jax.ShapeDtypeStruct(r.shape, F32), compiler_params=_params("parallel"))(c_arr, sp, r)


def _grad_exchange(hs):
    n = len(hs)

    def body(*refs):
        ins, outs = refs[:n], refs[n:2 * n]
        send_sems, recv_sems = refs[2 * n:]
        x, y, c = lax.axis_index("x"), lax.axis_index("y"), lax.axis_index("c")
        my_chip = 2 * x + y
        chips = [(1 - x, y), (x, 1 - y), (1 - x, 1 - y)]

        def copy(a, k, src_chip, dst_chip):
            px, py = chips[k]
            return pltpu.make_async_remote_copy(
                src_ref=ins[a].at[src_chip], dst_ref=outs[a].at[dst_chip], send_sem=send_sems.at[a, k],
                recv_sem=recv_sems.at[a, k], device_id=(px, py, c), device_id_type=MESH)

        sends = [copy(a, k, 2 * chips[k][0] + chips[k][1], my_chip) for k in range(3) for a in range(n)]
        for cp in sends:
            cp.start()
        for k in range(3):
            for a in range(n):
                copy(a, k, my_chip, 2 * chips[k][0] + chips[k][1]).wait_recv()
        for cp in sends:
            cp.wait_send()

    return _call(
        body, name="grad_exchange_chips", out_shape=[jax.ShapeDtypeStruct(h.shape, h.dtype) for h in hs],
        in_specs=[ANY] * n, out_specs=[ANY] * n,
        scratch_shapes=[pltpu.SemaphoreType.DMA((n, 3)), pltpu.SemaphoreType.DMA((n, 3))],
    )(*hs)


def _sum_chips(a, c_arr, name):
    _, r, cc = a.shape
    tr = _row_tile(r)

    def body(c_ref, a_ref, o_ref):
        o_ref[...] = ((a_ref[0].astype(F32) + a_ref[1].astype(F32)) + a_ref[2].astype(F32)) + a_ref[3].astype(F32)

    grid_spec = pltpu.PrefetchScalarGridSpec(
        num_scalar_prefetch=1, grid=(r // tr,),
        in_specs=[pl.BlockSpec((N_CHIPS, tr, cc), lambda i, c_ref: (0, i, 0))],
        out_specs=pl.BlockSpec((None, tr, cc), lambda i, c_ref: (c_ref[0], i, 0)))
    return _call(body, name=name, grid_spec=grid_spec, out_shape=jax.ShapeDtypeStruct((2, r, cc), F32),
                 compiler_params=_params("parallel"))(c_arr, a)


def _grad_share(bufs):
    n = len(bufs)

    def body(*refs):
        outs = refs[n:2 * n]
        send_sems, recv_sems = refs[2 * n:]
        x, y, c = lax.axis_index("x"), lax.axis_index("y"), lax.axis_index("c")
        copies = [pltpu.make_async_remote_copy(src_ref=outs[a].at[c], dst_ref=outs[a].at[c], send_sem=send_sems.at[a],
                                               recv_sem=recv_sems.at[a], device_id=(x, y, 1 - c),
                                               device_id_type=MESH) for a in range(n)]
        for cp in copies:
            cp.start()
        for a in range(n):
            pltpu.make_async_remote_copy(src_ref=outs[a].at[c], dst_ref=outs[a].at[1 - c], send_sem=send_sems.at[a],
                                         recv_sem=recv_sems.at[a], device_id=(x, y, 1 - c),
                                         device_id_type=MESH).wait_recv()
        for cp in copies:
            cp.wait_send()

    return _call(
        body, name="grad_share_cores", out_shape=[jax.ShapeDtypeStruct(b.shape, b.dtype) for b in bufs],
        in_specs=[ANY] * n, out_specs=[ANY] * n, input_output_aliases={a: a for a in range(n)},
        scratch_shapes=[pltpu.SemaphoreType.DMA((n,)), pltpu.SemaphoreType.DMA((n,))],
    )(*bufs)


def _adamw_math(g_ref, w_ref, m_ref, v_ref, d_ref, nm_ref, nv_ref):
    gv = g_ref[...]
    nm = ADAM_B1 * m_ref[...] + (1.0 - ADAM_B1) * gv
    nv = ADAM_B2 * v_ref[...] + (1.0 - ADAM_B2) * (gv * gv)
    m_hat = nm / (1.0 - ADAM_B1 ** ADAM_STEP)
    v_hat = nv / (1.0 - ADAM_B2 ** ADAM_STEP)
    d_ref[...] = -ADAM_LR * (m_hat / (jnp.sqrt(v_hat) + ADAM_EPS) + ADAM_WD * w_ref[...])
    nm_ref[...] = nm
    nv_ref[...] = nv


def _adamw_big(g, w, m, v, name):
    d, r, c = g.shape
    tr = r if r <= 704 else 256
    spec = pl.BlockSpec((1, tr, c), lambda l, i: (l, i, 0))

    def body(*refs):
        _adamw_math(*refs)

    shp = jax.ShapeDtypeStruct(g.shape, F32)
    return _call(body, name=name, grid=(d, r // tr), in_specs=[spec] * 4, out_specs=[spec] * 3,
                 out_shape=[shp, shp, shp], compiler_params=_params("parallel", "parallel"))(g, w, m, v)


def _adamw_small(gs, ws, ms, vs):
    n = len(gs)

    def body(*refs):
        ins, outs = refs[:4 * n], refs[4 * n:]
        for k in range(n):
            _adamw_math(ins[k], ins[n + k], ins[2 * n + k], ins[3 * n + k], outs[k], outs[n + k], outs[2 * n + k])

    shp = [jax.ShapeDtypeStruct(g.shape, F32) for g in gs]
    res = _call(body, name="adamw_small", out_shape=shp * 3)(*gs, *ws, *ms, *vs)
    return res[:n], res[n:2 * n], res[2 * n:]


def _rows(a, rows):
    flat = a.reshape(-1)
    return jnp.pad(flat, (0, rows * D_MODEL - flat.shape[0])).reshape(rows, D_MODEL)


def _small_rows(p, extra=None):
    parts = [p[n].reshape(-1) for n in SMALL_NAMES]
    if extra is not None:
        parts.append(extra.reshape(-1))
    flat = jnp.concatenate(parts)
    return jnp.pad(flat, (0, ROWS_SMALL * D_MODEL - flat.shape[0])).reshape(ROWS_SMALL, D_MODEL)


CONV_SHARD = (DEPTH, 3, 2 * D_FF // N_CHIPS)
N_CONV_SHARD = DEPTH * 3 * (2 * D_FF // N_CHIPS)


def _small_pack(g, loss):
    conv = jnp.transpose(g["conv_w"].reshape(DEPTH, 3, N_CHIPS, 2 * D_FF // N_CHIPS), (2, 0, 1, 3))
    conv = jnp.stack([_rows(conv[k], ROWS_CONV) for k in range(N_CHIPS)])
    small = jnp.broadcast_to(_small_rows(g, loss), (N_CHIPS, ROWS_SMALL, D_MODEL))
    return jnp.concatenate([conv, small], axis=1)


def _unpack_small(pack):
    out = {"conv_w": pack[:ROWS_CONV].reshape(-1)[:N_CONV_SHARD].reshape(CONV_SHARD)}
    flat = pack[ROWS_CONV:].reshape(-1)
    k = 0
    for name in SMALL_NAMES:
        shape = SMALL_SHAPES[name]
        n = 1
        for d in shape:
            n *= d
        out[name] = flat[k:k + n].reshape(shape)
        k += n
    out["extra"] = flat[k]
    return out


def _gather_weights(p, c):
    conv_bits = lax.bitcast_convert_type(_rows(p["conv_w"], ROWS_CONV), jnp.bfloat16)
    conv_bits = conv_bits.reshape(2 * ROWS_CONV, D_MODEL)
    bf = jnp.bfloat16
    block = jnp.concatenate([_rows(p["w_in"].astype(bf), ROWS_IN), _rows(p["w_o"].astype(bf), ROWS_O),
                             _rows(p["w_up"].astype(bf), ROWS_UP), _rows(p["w_down"].astype(bf), ROWS_DOWN),
                             conv_bits], axis=0)
    my_chip = 2 * lax.axis_index("x") + lax.axis_index("y")
    allw = _all_gather(block).reshape(N_CHIPS, ROWS_GATHER, D_MODEL)
    allw = lax.dynamic_update_index_in_dim(allw, block, my_chip, 0)
    o = 0

    def take(rows, shape):
        nonlocal o
        a = allw[:, o:o + rows].reshape((N_CHIPS,) + shape)
        o += rows
        return a

    w_in = take(ROWS_IN, (DEPTH, D_MODEL, IN_COLS // N_CHIPS))
    w_o = take(ROWS_O, (DEPTH, D_MODEL // N_CHIPS, D_MODEL))
    w_up = take(ROWS_UP, (DEPTH, D_MODEL, 2 * D_FF // N_CHIPS))
    w_down = take(ROWS_DOWN, (DEPTH, D_FF // N_CHIPS, D_MODEL))
    conv = lax.bitcast_convert_type(allw[:, o:o + 2 * ROWS_CONV].reshape(N_CHIPS, ROWS_CONV, D_MODEL, 2), F32)
    conv = conv.reshape(N_CHIPS, -1)[:, :DEPTH * 3 * (2 * D_FF // N_CHIPS)].reshape(N_CHIPS, DEPTH, 3, 2 * D_FF // N_CHIPS)

    def by_cols(a):
        k, d, r, wd = a.shape
        return jnp.transpose(a, (1, 2, 0, 3)).reshape(d, r, k * wd)

    def by_rows(a):
        k, d, hgt, cc = a.shape
        return jnp.transpose(a, (1, 0, 2, 3)).reshape(d, k * hgt, cc)

    return dict(w_in=by_cols(w_in), w_o=by_rows(w_o), w_up=by_cols(w_up), w_down=by_rows(w_down), conv_w=by_cols(conv))


def _reduce_grads(grads, loss, c):
    bigs = [grads[n] for n in BIG_NAMES[:4]]
    sp = _small_pack(grads, loss)
    c_arr = jnp.reshape(c, (1,)).astype(jnp.int32)
    got = _grad_swap(bigs, sp)
    pair = [_pair_add(bigs[a], got[a], c_arr, "grad_add_cores_" + BIG_NAMES[a], ICI_DT) for a in range(4)]
    pair.append(_pair_add_small(sp, got[4], c_arr))
    my_chip = 2 * lax.axis_index("x") + lax.axis_index("y")
    parts = [lax.dynamic_update_index_in_dim(got_k, lax.dynamic_index_in_dim(own, my_chip, 0, keepdims=False),
                                             my_chip, 0) for got_k, own in zip(_grad_exchange(pair), pair)]
    total = [_sum_chips(parts[a], c_arr, "grad_sum_chips_" + (BIG_NAMES[:4] + ("small",))[a]) for a in range(5)]
    shared = _grad_share(total)
    out = dict(zip(BIG_NAMES[:4], shared[:4]))
    out.update(_unpack_small(shared[4].reshape(2 * SP_HALF, D_MODEL)))
    return out


def kernel(x, norm1_g, w_in, sgu_norm_g, sgu_w, sgu_b, pool_w, pool_scale, mix_norm_g, w_o, norm2_g, w_up, conv_w, conv_b, w_down, final_g, loss_target, m_norm1_g, m_w_in, m_sgu_norm_g, m_sgu_w, m_sgu_b, m_pool_w, m_pool_scale, m_mix_norm_g, m_w_o, m_norm2_g, m_w_up, m_conv_w, m_conv_b, m_w_down, m_final_g, v_norm1_g, v_w_in, v_sgu_norm_g, v_sgu_w, v_sgu_b, v_pool_w, v_pool_scale, v_mix_norm_g, v_w_o, v_norm2_g, v_w_up, v_conv_w, v_conv_b, v_w_down, v_final_g):
    names = ("norm1_g", "w_in", "sgu_norm_g", "sgu_w", "sgu_b", "pool_w", "pool_scale", "mix_norm_g", "w_o",
             "norm2_g", "w_up", "conv_w", "conv_b", "w_down", "final_g")
    p = dict(zip(names, (norm1_g, w_in, sgu_norm_g, sgu_w, sgu_b, pool_w, pool_scale, mix_norm_g, w_o, norm2_g,
                         w_up, conv_w, conv_b, w_down, final_g)))
    pm = dict(zip(names, (m_norm1_g, m_w_in, m_sgu_norm_g, m_sgu_w, m_sgu_b, m_pool_w, m_pool_scale, m_mix_norm_g,
                          m_w_o, m_norm2_g, m_w_up, m_conv_w, m_conv_b, m_w_down, m_final_g)))
    pv = dict(zip(names, (v_norm1_g, v_w_in, v_sgu_norm_g, v_sgu_w, v_sgu_b, v_pool_w, v_pool_scale, v_mix_norm_g,
                          v_w_o, v_norm2_g, v_w_up, v_conv_w, v_conv_b, v_w_down, v_final_g)))
    c = lax.axis_index("c")
    full = dict(p)
    full.update(_gather_weights(p, c))

    loss, dx, grads = _local_step(x[0], loss_target[0], full)

    g = _reduce_grads(grads, loss, c)
    d, nm, nv = {}, {}, {}
    for n in BIG_NAMES:
        d[n], nm[n], nv[n] = _adamw_big(g[n], p[n], pm[n], pv[n], "adamw_" + n)

    def two_d(a):
        return a.reshape(1, -1) if a.ndim == 1 else a

    ds, ms, vs = _adamw_small([two_d(g[n]) for n in SMALL_NAMES], [two_d(p[n]) for n in SMALL_NAMES],
                              [two_d(pm[n]) for n in SMALL_NAMES], [two_d(pv[n]) for n in SMALL_NAMES])
    for k, n in enumerate(SMALL_NAMES):
        d[n], nm[n], nv[n] = (a.reshape(p[n].shape) for a in (ds[k], ms[k], vs[k]))
    return (g["extra"], dx[None], *[g[n] for n in names], *[d[n] for n in names], *[nm[n] for n in names],
            *[nv[n] for n in names])
```

```python
import functools

import jax
import jax.numpy as jnp
from jax import lax
from jax.experimental import pallas as pl
from jax.experimental.pallas import tpu as pltpu

F32 = jnp.float32
MXU_DT = jnp.bfloat16

D_MODEL = 1024
DEPTH = 2
HEAD_DIM = 64
W_A = 256
W_B = 512
W_C = 256
IN_COLS = 2 * W_A + 3 * W_B + W_C
CHUNK = 128
POOL_WINDOWS = (2, 4, 8, 16)
D_FF = 2816
EPS = 1e-6
N_CHIPS = 4

ADAM_LR = 0.001
ADAM_B1 = 0.9
ADAM_B2 = 0.999
ADAM_EPS = 1e-08
ADAM_WD = 0.01
ADAM_STEP = 10

LANES = 128
TQ = 256
TK = 256
TM = 256
TM_MM = 512
HALO = 16
VMEM_LIMIT = 56 * 1024 * 1024

ROWS_IN = DEPTH * D_MODEL * (IN_COLS // N_CHIPS) // D_MODEL
ROWS_O = DEPTH * (D_MODEL // N_CHIPS)
ROWS_UP = DEPTH * D_MODEL * (2 * D_FF // N_CHIPS) // D_MODEL
ROWS_DOWN = DEPTH * (D_FF // N_CHIPS)
ROWS_BIG = ROWS_IN + ROWS_O + ROWS_UP + ROWS_DOWN
ROWS_CONV = 16
ROWS_SMALL = 240
ROWS_GATHER = ROWS_BIG + 2 * ROWS_CONV
SP_HALF = (ROWS_CONV + ROWS_SMALL) // 2
ICI_DT = jnp.bfloat16

BIG_NAMES = ("w_in", "w_o", "w_up", "w_down", "conv_w")
SMALL_NAMES = ("norm1_g", "sgu_norm_g", "sgu_w", "sgu_b", "pool_w", "pool_scale",
               "mix_norm_g", "norm2_g", "conv_b", "final_g")
SMALL_SHAPES = {
    "norm1_g": (DEPTH, D_MODEL), "sgu_norm_g": (DEPTH, W_A), "sgu_w": (DEPTH, 4, CHUNK, CHUNK),
    "sgu_b": (DEPTH, 4, CHUNK), "pool_w": (DEPTH, 4, 64, 64), "pool_scale": (DEPTH, W_C),
    "mix_norm_g": (DEPTH, D_MODEL), "norm2_g": (DEPTH, D_MODEL), "conv_b": (DEPTH, 2 * D_FF),
    "final_g": (D_MODEL,),
}


def _call(body, **kw):
    return pl.pallas_call(body, **kw)


def _params(*sem):
    return pltpu.CompilerParams(dimension_semantics=sem, vmem_limit_bytes=VMEM_LIMIT)


def _dot(a, b):
    return jnp.dot(a, b, preferred_element_type=F32)


def _dot_nt(a, b):
    return lax.dot_general(a, b, (((1,), (1,)), ((), ())), preferred_element_type=F32)


def _dot_tn(a, b):
    return lax.dot_general(a, b, (((0,), (0,)), ((), ())), preferred_element_type=F32)


def _split(a):
    hi = a.astype(MXU_DT)
    lo = (a - hi.astype(F32)).astype(MXU_DT)
    return hi, lo


def _dot_split(a, b):
    hi, lo = _split(a)
    return _dot(hi, b) + _dot(lo, b)


def _group_mean(sq, gmat):
    cols = [_dot_split(sq[:, b * LANES:(b + 1) * LANES], gmat) for b in range(sq.shape[1] // LANES)]
    return cols[0] if len(cols) == 1 else jnp.concatenate(cols, axis=-1)


def _group_matrix():
    r = jnp.arange(LANES)
    return jnp.where((r[:, None] // HEAD_DIM) == (r[None, :] // HEAD_DIM), 1.0 / HEAD_DIM, 0.0).astype(MXU_DT)


def _tile(n):
    return max(t for t in range(LANES, 1536 + 1, LANES) if n % t == 0)


def _row_spec(tm, cols, col_block=0):
    return pl.BlockSpec((tm, cols), lambda i, cb=col_block: (i, cb))


def _full_spec(shape):
    nd = len(shape)
    return pl.BlockSpec(shape, lambda *_: (0,) * nd)


def _rms_mm(x, g, w, tn, name, out_dtypes):
    S, D = x.shape
    N = w.shape[1]
    tm = TM_MM

    def body(x_ref, g_ref, w_ref, h_ref, *o_refs):
        xv = x_ref[...]
        r = lax.rsqrt(jnp.mean(xv * xv, axis=-1, keepdims=True) + EPS)
        h = (xv * r * g_ref[...]).astype(h_ref.dtype)
        h_ref[...] = h
        for n0 in range(0, N, tn):
            acc = _dot(h, w_ref[:, n0:n0 + tn])
            for o_ref in o_refs:
                o_ref[:, n0:n0 + tn] = acc.astype(o_ref.dtype)

    return _call(
        body, name=name, grid=(S // tm,),
        in_specs=[_row_spec(tm, D), _full_spec((1, D)), _full_spec((D, N))],
        out_specs=[_row_spec(tm, D)] + [_row_spec(tm, N) for _ in out_dtypes],
        out_shape=[jax.ShapeDtypeStruct((S, D), MXU_DT)] + [jax.ShapeDtypeStruct((S, N), dt) for dt in out_dtypes],
        compiler_params=_params("parallel"),
    )(x, g, w)


def _mm_res(a, w, res, name):
    S, K = a.shape
    N = w.shape[1]
    tm = TM_MM

    def body(a_ref, w_ref, r_ref, o_ref):
        o_ref[...] = r_ref[...] + _dot(a_ref[...], w_ref[...])

    return _call(
        body, name=name, grid=(S // tm,),
        in_specs=[_row_spec(tm, K), _full_spec((K, N)), _row_spec(tm, N)],
        out_specs=_row_spec(tm, N),
        out_shape=jax.ShapeDtypeStruct((S, N), F32),
        compiler_params=_params("parallel"),
    )(a, w, res)


def _mm_nt(a, w, name):
    S, K = a.shape
    N = w.shape[0]
    tm = TM_MM

    def body(a_ref, w_ref, o_ref):
        o_ref[...] = _dot_nt(a_ref[...].astype(MXU_DT), w_ref[...]).astype(o_ref.dtype)

    return _call(
        body, name=name, grid=(S // tm,),
        in_specs=[_row_spec(tm, K), _full_spec((N, K))],
        out_specs=_row_spec(tm, N),
        out_shape=jax.ShapeDtypeStruct((S, N), MXU_DT),
        compiler_params=_params("parallel"),
    )(a, w)


def _mm_tn(a, b, name, col_tiles=False):
    S, K1 = a.shape
    N = b.shape[1]
    ts = TM_MM
    tk = _tile(K1)
    tn = _tile(N)
    if col_tiles:
        out_spec = pl.BlockSpec((None, tk, tn), lambda m, n, s: (n, m, 0))
        out_shape = jax.ShapeDtypeStruct((N // tn, K1, tn), F32)
    else:
        out_spec = pl.BlockSpec((tk, tn), lambda m, n, s: (m, n))
        out_shape = jax.ShapeDtypeStruct((K1, N), F32)

    def body(a_ref, b_ref, o_ref):
        @pl.when(pl.program_id(2) == 0)
        def _():
            o_ref[...] = jnp.zeros_like(o_ref)

        o_ref[...] += _dot_tn(a_ref[...], b_ref[...].astype(MXU_DT))

    return _call(
        body, name=name, grid=(K1 // tk, N // tn, S // ts),
        in_specs=[pl.BlockSpec((ts, tk), lambda m, n, s: (s, m)),
                  pl.BlockSpec((ts, tn), lambda m, n, s: (s, n))],
        out_specs=out_spec, out_shape=out_shape,
        compiler_params=_params("parallel", "parallel", "arbitrary"),
    )(a, b)


def _mm_nt_rmsbwd(pairs, x, g, dres, name):
    S, D = x.shape
    tm = TM
    n = len(pairs)

    def body(*refs):
        a_refs = refs[:n]
        w_refs = refs[n:2 * n]
        x_ref, g_ref, r_ref, dx_ref, dg_ref = refs[2 * n:]
        dh = _dot_nt(a_refs[0][...], w_refs[0][...])
        for k in range(1, n):
            dh += _dot_nt(a_refs[k][...], w_refs[k][...])
        xv = x_ref[...]
        r = lax.rsqrt(jnp.mean(xv * xv, axis=-1, keepdims=True) + EPS)
        xhat = xv * r

        @pl.when(pl.program_id(0) == 0)
        def _():
            dg_ref[...] = jnp.zeros_like(dg_ref)

        dg_ref[...] += jnp.sum(dh * xhat, axis=0, keepdims=True)
        dxh = dh * g_ref[...]
        dx_ref[...] = r_ref[...] + r * (dxh - xhat * jnp.mean(dxh * xhat, axis=-1, keepdims=True))

    in_specs = ([_row_spec(tm, a.shape[1]) for a, _ in pairs] + [_full_spec(w.shape) for _, w in pairs]
                + [_row_spec(tm, D), _full_spec((1, D)), _row_spec(tm, D)])
    return _call(
        body, name=name, grid=(S // tm,), in_specs=in_specs,
        out_specs=[_row_spec(tm, D), _full_spec((1, D))],
        out_shape=[jax.ShapeDtypeStruct((S, D), F32), jax.ShapeDtypeStruct((1, D), F32)],
        compiler_params=_params("arbitrary"),
    )(*[a for a, _ in pairs], *[w for _, w in pairs], x, g, dres)


def _loss_head(x, g, tgt):
    S, D = x.shape
    tm = TM

    def body(x_ref, g_ref, t_ref, dx_ref, dg_ref, l_ref):
        xv = x_ref[...]
        r = lax.rsqrt(jnp.mean(xv * xv, axis=-1, keepdims=True) + EPS)
        xhat = xv * r
        diff = xhat * g_ref[...] - t_ref[...]

        @pl.when(pl.program_id(0) == 0)
        def _():
            dg_ref[...] = jnp.zeros_like(dg_ref)
            l_ref[...] = jnp.zeros_like(l_ref)

        l_ref[...] += jnp.full(l_ref.shape, 0.5 * jnp.sum(jnp.mean(diff * diff, axis=-1, keepdims=True)), F32)
        dout = diff * (1.0 / D)
        dg_ref[...] += jnp.sum(dout * xhat, axis=0, keepdims=True)
        dxh = dout * g_ref[...]
        dx_ref[...] = r * (dxh - xhat * jnp.mean(dxh * xhat, axis=-1, keepdims=True))

    return _call(
        body, name="loss_head", grid=(S // tm,),
        in_specs=[_row_spec(tm, D), _full_spec((1, D)), _row_spec(tm, D)],
        out_specs=[_row_spec(tm, D), _full_spec((1, D)), _full_spec((8, LANES))],
        out_shape=[jax.ShapeDtypeStruct((S, D), F32), jax.ShapeDtypeStruct((1, D), F32),
                   jax.ShapeDtypeStruct((8, LANES), F32)],
        compiler_params=_params("arbitrary"),
    )(x, g, tgt)


def _mix_out(ya, yb, yc, gm, wo, x, gmat):
    S = x.shape[0]
    tm = TM

    def body(ya_ref, yb_ref, yc_ref, gm_ref, wo_ref, x_ref, gmat_ref, x2_ref, yn_ref):
        y = jnp.concatenate([ya_ref[...], yb_ref[...], yc_ref[...]], axis=-1)
        r = lax.rsqrt(_group_mean(y * y, gmat_ref[...]) + EPS)
        yn = (y * r * gm_ref[...]).astype(MXU_DT)
        yn_ref[...] = yn
        x2_ref[...] = x_ref[...] + _dot(yn, wo_ref[...])

    return _call(
        body, name="mix_out", grid=(S // tm,),
        in_specs=[_row_spec(tm, W_A), _row_spec(tm, W_B), _row_spec(tm, W_C), _full_spec((1, D_MODEL)),
                  _full_spec((D_MODEL, D_MODEL)), _row_spec(tm, D_MODEL), _full_spec((LANES, LANES))],
        out_specs=[_row_spec(tm, D_MODEL), _row_spec(tm, D_MODEL)],
        out_shape=[jax.ShapeDtypeStruct((S, D_MODEL), F32), jax.ShapeDtypeStruct((S, D_MODEL), MXU_DT)],
        compiler_params=_params("parallel"),
    )(ya, yb, yc, gm, wo, x, gmat)


def _mix_out_bwd(dx2, wo, ya, yb, yc, gm, gmat):
    S = dx2.shape[0]
    tm = TM

    def body(dx2_ref, wo_ref, ya_ref, yb_ref, yc_ref, gm_ref, gmat_ref, dya_ref, dyb_ref, dyc_ref, dgm_ref):
        dyn = _dot_nt(dx2_ref[...].astype(MXU_DT), wo_ref[...])
        y = jnp.concatenate([ya_ref[...], yb_ref[...], yc_ref[...]], axis=-1)
        r = lax.rsqrt(_group_mean(y * y, gmat_ref[...]) + EPS)
        yhat = y * r

        @pl.when(pl.program_id(0) == 0)
        def _():
            dgm_ref[...] = jnp.zeros_like(dgm_ref)

        dgm_ref[...] += jnp.sum(dyn * yhat, axis=0, keepdims=True)
        dyh = dyn * gm_ref[...]
        dy = r * (dyh - yhat * _group_mean(dyh * yhat, gmat_ref[...]))
        dya_ref[...] = dy[:, :W_A]
        dyb_ref[...] = dy[:, W_A:W_A + W_B]
        dyc_ref[...] = dy[:, W_A + W_B:]

    return _call(
        body, name="mix_out_bwd", grid=(S // tm,),
        in_specs=[_row_spec(tm, D_MODEL), _full_spec((D_MODEL, D_MODEL)), _row_spec(tm, W_A), _row_spec(tm, W_B),
                  _row_spec(tm, W_C), _full_spec((1, D_MODEL)), _full_spec((LANES, LANES))],
        out_specs=[_row_spec(tm, W_A), _row_spec(tm, W_B), _row_spec(tm, W_C), _full_spec((1, D_MODEL))],
        out_shape=[jax.ShapeDtypeStruct((S, W_A), F32), jax.ShapeDtypeStruct((S, W_B), F32),
                   jax.ShapeDtypeStruct((S, W_C), F32), jax.ShapeDtypeStruct((1, D_MODEL), F32)],
        compiler_params=_params("arbitrary"),
    )(dx2, wo, ya, yb, yc, gm, gmat)


_SQRT_HALF = 0.7071067811865476
_INV_SQRT_2PI = 0.3989422804014327


def _sgu_common(a, sng, wm_ref, bias, gmat):
    phi = 0.5 * (1.0 + lax.erf(a * _SQRT_HALF))
    ga = a * phi
    u = ga[:, :W_A]
    v = ga[:, W_A:]
    r = lax.rsqrt(_group_mean(v * v, gmat) + EPS)
    vhat = v * r
    vn = (vhat * sng).astype(MXU_DT)
    head = lax.broadcasted_iota(jnp.int32, (CHUNK, W_A), 1) // HEAD_DIM
    rows = []
    for c in range(a.shape[0] // CHUNK):
        vc = vn[c * CHUNK:(c + 1) * CHUNK]
        s = bias
        for h in range(4):
            s = s + jnp.where(head == h, _dot(wm_ref[h], vc), 0.0)
        rows.append(s)
    s = jnp.concatenate(rows, axis=0)
    return phi, u, r, vhat, vn, s


def _tril_weights(sgu_w_l):
    t = jnp.arange(CHUNK)
    return jnp.where((t[None, :] <= t[:, None])[None], sgu_w_l, 0.0)


def _sgu_fwd(proj, sng, wm, bias, gmat):
    S = proj.shape[0]
    tm = TM

    def body(a_ref, sng_ref, wm_ref, b_ref, gmat_ref, y_ref):
        _, u, _, _, _, s = _sgu_common(a_ref[...], sng_ref[...], wm_ref, b_ref[...], gmat_ref[...])
        y_ref[...] = u * s

    return _call(
        body, name="sgu_fwd", grid=(S // tm,),
        in_specs=[_row_spec(tm, 2 * W_A), _full_spec((1, W_A)), _full_spec((4, CHUNK, CHUNK)),
                  _full_spec((CHUNK, W_A)), _full_spec((LANES, LANES))],
        out_specs=_row_spec(tm, W_A),
        out_shape=jax.ShapeDtypeStruct((S, W_A), F32),
        compiler_params=_params("parallel"),
    )(proj, sng, wm, bias, gmat)


def _sgu_bwd(proj, dy, sng, wm, wmt, bias, gmat):
    S = proj.shape[0]
    tm = TM

    def body(a_ref, dy_ref, sng_ref, wm_ref, wmt_ref, b_ref, gmat_ref, da_ref, dw_ref, db_ref, dsng_ref):
        a = a_ref[...]
        dy = dy_ref[...]
        gmat = gmat_ref[...]
        sng = sng_ref[...]
        phi, u, r, vhat, vn, s = _sgu_common(a, sng, wm_ref, b_ref[...], gmat)
        du = dy * s
        ds = dy * u

        @pl.when(pl.program_id(0) == 0)
        def _():
            dw_ref[...] = jnp.zeros_like(dw_ref)
            db_ref[...] = jnp.zeros_like(db_ref)
            dsng_ref[...] = jnp.zeros_like(dsng_ref)

        head = lax.broadcasted_iota(jnp.int32, (CHUNK, W_A), 1) // HEAD_DIM
        tt = lax.broadcasted_iota(jnp.int32, (CHUNK, CHUNK), 0)
        ss = lax.broadcasted_iota(jnp.int32, (CHUNK, CHUNK), 1)
        rows = []
        for c in range(tm // CHUNK):
            dsc = ds[c * CHUNK:(c + 1) * CHUNK]
            vc = vn[c * CHUNK:(c + 1) * CHUNK]
            db_ref[...] += dsc
            dsb = dsc.astype(MXU_DT)
            dvn = jnp.zeros((CHUNK, W_A), F32)
            for h in range(4):
                dvn = dvn + jnp.where(head == h, _dot(wmt_ref[h], dsb), 0.0)
                dsh = jnp.where(head == h, dsc, 0.0).astype(MXU_DT)
                dw_ref[h] += jnp.where(ss <= tt, _dot_nt(dsh, vc), 0.0)
            rows.append(dvn)
        dvn = jnp.concatenate(rows, axis=0)
        dsng_ref[...] += jnp.sum(dvn * vhat, axis=0, keepdims=True)
        dvh = dvn * sng
        dv = r * (dvh - vhat * _group_mean(dvh * vhat, gmat))
        dga = jnp.concatenate([du, dv], axis=-1)
        dgelu = phi + a * (_INV_SQRT_2PI * jnp.exp(-0.5 * a * a))
        da_ref[...] = (dga * dgelu).astype(da_ref.dtype)

    return _call(
        body, name="sgu_bwd", grid=(S // tm,),
        in_specs=[_row_spec(tm, 2 * W_A), _row_spec(tm, W_A), _full_spec((1, W_A)), _full_spec((4, CHUNK, CHUNK)),
                  _full_spec((4, CHUNK, CHUNK)), _full_spec((CHUNK, W_A)), _full_spec((LANES, LANES))],
        out_specs=[_row_spec(tm, 2 * W_A), _full_spec((4, CHUNK, CHUNK)), _full_spec((CHUNK, W_A)),
                   _full_spec((1, W_A))],
        out_shape=[jax.ShapeDtypeStruct((S, 2 * W_A), MXU_DT), jax.ShapeDtypeStruct((4, CHUNK, CHUNK), F32),
                   jax.ShapeDtypeStruct((CHUNK, W_A), F32), jax.ShapeDtypeStruct((1, W_A), F32)],
        compiler_params=_params("arbitrary"),
    )(proj, dy, sng, wm, wmt, bias, gmat)


HG = 4
LW = HG * HEAD_DIM
Q_BLK0 = (2 * W_A) // LW
K_BLK0 = Q_BLK0 + W_B // LW
V_BLK0 = K_BLK0 + W_B // LW
N_GROUPS = W_B // LW
EXP_IS_ZERO_BELOW = -120.0


def _tri_matrix():
    r = jnp.arange(TK)
    return (r[:, None] > r[None, :]).astype(MXU_DT)


def _stack_heads(a):
    head = lax.broadcasted_iota(jnp.int32, a.shape, 1) // HEAD_DIM
    return jnp.concatenate([jnp.where(head == h, a, 0.0) for h in range(HG)], axis=0).astype(MXU_DT)


def _unstack_heads(a):
    head = lax.broadcasted_iota(jnp.int32, (TQ, LW), 1) // HEAD_DIM
    out = a[:TQ]
    for h in range(1, HG):
        out = jnp.where(head == h, a[h * TQ:(h + 1) * TQ], out)
    return out


def _sb_scores(q2, kj, tri, key_offset):
    z = _dot_nt(q2, kj)
    sp = jnp.log(1.0 + jnp.exp(-jnp.abs(z)))
    lsp = jnp.minimum(z, 0.0) - sp
    lsm = lsp - z
    msk = None
    if key_offset is not None:
        row = lax.broadcasted_iota(jnp.int32, z.shape, 0) & (TQ - 1)
        col = lax.broadcasted_iota(jnp.int32, z.shape, 1) + key_offset
        msk = col < row
        lsm = jnp.where(msk, lsm, 0.0)
    tail = _dot(lsm.astype(MXU_DT), tri)
    return lsp, lsm, tail, msk


def _sb_fwd(proj_b, tri):
    S = proj_b.shape[0]
    nq = S // TQ
    kpq = TQ // TK
    assert S // TK < LANES

    def body(q_ref, k_ref, v_ref, tri_ref, o_ref, rb_ref, acc_ref):
        i = pl.program_id(1)
        lane2 = lax.broadcasted_iota(jnp.int32, (HG * TQ, LANES), 1)
        q2 = _stack_heads(q_ref[...].astype(F32) * (HEAD_DIM ** -0.5))
        tri = tri_ref[...]
        rb_ref[...] = jnp.zeros_like(rb_ref)

        def block(j, run, key_offset=None, first=False):
            start = pl.multiple_of(j * TK, TK)
            kj = k_ref[pl.ds(start, TK), :]
            vj = v_ref[pl.ds(start, TK), :]
            lsp, lsm, tail, msk = _sb_scores(q2, kj, tri, key_offset)
            rb_ref[...] = jnp.where(lane2 == j, run, rb_ref[...])
            att = jnp.exp(lsp + tail + run)
            if msk is not None:
                att = jnp.where(msk, att, 0.0)
            pv = _dot(att.astype(MXU_DT), vj)
            if first:
                acc_ref[...] = pv
            else:
                acc_ref[...] += pv
            return run + tail[:, :1] + lsm[:, :1]

        run = jnp.zeros((HG * TQ, 1), F32)
        for d in reversed(range(kpq)):
            run = block(i * kpq + d, run, key_offset=d * TK, first=(d == kpq - 1))
        past = i * kpq

        def alive(run):
            return (jnp.max(run) > EXP_IS_ZERO_BELOW).astype(jnp.int32)

        def step(carry):
            n, run, _ = carry
            run = block(past - 1 - n, run)
            return n + 1, run, alive(run)

        n, _, _ = lax.while_loop(lambda c: jnp.logical_and(c[0] < past, c[2] > 0), step,
                                 (jnp.int32(0), run, alive(run)))
        rb_ref[...] = jnp.where(lane2 == LANES - 1, n.astype(F32), rb_ref[...])
        o_ref[...] = _unstack_heads(acc_ref[...])

    once = pl.Buffered(1)
    return _call(
        body, name="sb_fwd", grid=(N_GROUPS, nq),
        in_specs=[pl.BlockSpec((TQ, LW), lambda p, i: (i, Q_BLK0 + p)),
                  pl.BlockSpec((S, LW), lambda p, i: (0, K_BLK0 + p), pipeline_mode=once),
                  pl.BlockSpec((S, LW), lambda p, i: (0, V_BLK0 + p), pipeline_mode=once),
                  pl.BlockSpec((TK, TK), lambda p, i: (0, 0))],
        out_specs=[pl.BlockSpec((TQ, LW), lambda p, i: (i, p)),
                   pl.BlockSpec((None, None, HG * TQ, LANES), lambda p, i: (p, i, 0, 0))],
        out_shape=[jax.ShapeDtypeStruct((S, W_B), F32), jax.ShapeDtypeStruct((N_GROUPS, nq, HG * TQ, LANES), F32)],
        scratch_shapes=[pltpu.VMEM((HG * TQ, LW), F32)],
        compiler_params=_params("parallel", "arbitrary"),
    )(proj_b, proj_b, proj_b, tri)


def _sb_bwd(proj_b, dyb, rb, tri, trit):
    S = proj_b.shape[0]
    nq = S // TQ
    kpq = TQ // TK

    def body(q_ref, k_ref, v_ref, do_ref, rb_ref, tri_ref, trit_ref, dq_ref, dk_acc, dv_acc, dq_acc):
        i = pl.program_id(1)
        lane2 = lax.broadcasted_iota(jnp.int32, (HG * TQ, LANES), 1)
        scale = HEAD_DIM ** -0.5
        q2 = _stack_heads(q_ref[...].astype(F32) * scale)
        do2 = _stack_heads(do_ref[...])
        tri = tri_ref[...]
        trit = trit_ref[...]

        @pl.when(i == 0)
        def _():
            dk_acc[...] = jnp.zeros_like(dk_acc)
            dv_acc[...] = jnp.zeros_like(dv_acc)

        dq_acc[...] = jnp.zeros_like(dq_acc)

        def block(j, pre, key_offset=None):
            start = pl.multiple_of(j * TK, TK)
            kj = k_ref[pl.ds(start, TK), :]
            vj = v_ref[pl.ds(start, TK), :]
            lsp, lsm, tail, msk = _sb_scores(q2, kj, tri, key_offset)
            run = jnp.sum(jnp.where(lane2 == j, rb_ref[...], 0.0), axis=-1, keepdims=True)
            att = jnp.exp(lsp + tail + run)
            if msk is not None:
                att = jnp.where(msk, att, 0.0)
            beta = jnp.exp(lsp)
            dl = _dot_nt(do2, vj) * att
            cin = _dot(dl.astype(MXU_DT), trit)
            dz = dl * (1.0 - beta) - beta * (pre + cin)
            if msk is not None:
                dz = jnp.where(msk, dz, 0.0)
            dzb = dz.astype(MXU_DT)
            dq_acc[...] += _dot(dzb, kj)
            dk_acc[pl.ds(start, TK), :] += _dot_tn(dzb, q2)
            dv_acc[pl.ds(start, TK), :] += _dot_tn(att.astype(MXU_DT), do2)
            return pre + cin[:, TK - 1:] + dl[:, TK - 1:]

        past = i * kpq
        walked = jnp.max(jnp.where(lane2[:8] == LANES - 1, rb_ref[pl.ds(0, 8), :], 0.0)).astype(jnp.int32)
        walked = jnp.clip(walked, 0, past)
        pre = lax.fori_loop(past - walked, past, lambda j, pre: block(j, pre), jnp.zeros((HG * TQ, 1), F32))
        for d in range(kpq):
            pre = block(i * kpq + d, pre, key_offset=d * TK)
        dq_ref[...] = (_unstack_heads(dq_acc[...]) * scale).astype(dq_ref.dtype)

    once = pl.Buffered(1)
    return _call(
        body, name="sb_bwd", grid=(N_GROUPS, nq),
        in_specs=[pl.BlockSpec((TQ, LW), lambda p, i: (i, Q_BLK0 + p)),
                  pl.BlockSpec((S, LW), lambda p, i: (0, K_BLK0 + p), pipeline_mode=once),
                  pl.BlockSpec((S, LW), lambda p, i: (0, V_BLK0 + p), pipeline_mode=once),
                  pl.BlockSpec((TQ, LW), lambda p, i: (i, p)),
                  pl.BlockSpec((None, None, HG * TQ, LANES), lambda p, i: (p, i, 0, 0)),
                  pl.BlockSpec((TK, TK), lambda p, i: (0, 0)),
                  pl.BlockSpec((TK, TK), lambda p, i: (0, 0))],
        out_specs=[pl.BlockSpec((TQ, LW), lambda p, i: (i, p)),
                   pl.BlockSpec((S, LW), lambda p, i: (0, p), pipeline_mode=once),
                   pl.BlockSpec((S, LW), lambda p, i: (0, p), pipeline_mode=once)],
        out_shape=[jax.ShapeDtypeStruct((S, W_B), MXU_DT), jax.ShapeDtypeStruct((S, W_B), F32),
                   jax.ShapeDtypeStruct((S, W_B), F32)],
        scratch_shapes=[pltpu.VMEM((HG * TQ, LW), F32)],
        compiler_params=_params("parallel", "arbitrary"),
    )(proj_b, proj_b, proj_b, dyb, rb, tri, trit)


P_BLK = (2 * W_A + 3 * W_B) // W_C


def _window_lanes():
    g = lax.broadcasted_iota(jnp.int32, (1, W_C), 1) // (W_C // 4)
    w = jnp.where(g == 0, POOL_WINDOWS[0], jnp.where(g == 1, POOL_WINDOWS[1],
                  jnp.where(g == 2, POOL_WINDOWS[2], POOL_WINDOWS[3])))
    return g, w


def _shift_rows(ext, k, tm, lead):
    n = ext.shape[0]
    return pltpu.roll(ext, shift=k % n, axis=0)[lead:lead + tm]


def _pool_diff(p_cur, p_halo, row0, tm):
    ext = jnp.concatenate([p_halo, p_cur], axis=0)
    g, w = _window_lanes()
    acc = ext
    sums = []
    for sh in (1, 2, 4, 8):
        acc = acc + pltpu.roll(acc, shift=sh, axis=0)
        sums.append(acc[HALO:HALO + tm])
    wsum = jnp.where(g == 0, sums[0], jnp.where(g == 1, sums[1], jnp.where(g == 2, sums[2], sums[3])))
    pos = (row0 + 1 + lax.broadcasted_iota(jnp.int32, (tm, W_C), 0)).astype(F32)
    cnt = jnp.minimum(pos, w.astype(F32))
    return wsum / cnt - p_cur, cnt


def _pool_specs(tm, nrow_blocks_halo):
    cur = pl.BlockSpec((tm, W_C), lambda i: (i, P_BLK))
    prev = pl.BlockSpec((HALO, W_C), lambda i: (jnp.maximum(i * (tm // HALO) - 1, 0), P_BLK))
    return cur, prev


def _pool_fwd(proj, wbd, scale):
    S = proj.shape[0]
    tm = TM

    def body(p_ref, ph_ref, w_ref, sc_ref, y_ref):
        i = pl.program_id(0)
        halo = jnp.where(i > 0, ph_ref[...], 0.0)
        d, _ = _pool_diff(p_ref[...], halo, i * tm, tm)
        y_ref[...] = _dot(d.astype(MXU_DT), w_ref[...]) * sc_ref[...]

    cur, prev = _pool_specs(tm, S // HALO)
    return _call(
        body, name="pool_fwd", grid=(S // tm,),
        in_specs=[cur, prev, _full_spec((W_C, W_C)), _full_spec((1, W_C))],
        out_specs=_row_spec(tm, W_C),
        out_shape=jax.ShapeDtypeStruct((S, W_C), F32),
        compiler_params=_params("parallel"),
    )(proj, proj, wbd, scale)


def _pool_bwd_a(proj, dy, wbd, scale):
    S = proj.shape[0]
    tm = TM

    def body(p_ref, ph_ref, dy_ref, w_ref, sc_ref, dd_ref, e_ref, dw_ref, dsc_ref):
        i = pl.program_id(0)
        halo = jnp.where(i > 0, ph_ref[...], 0.0)
        d, cnt = _pool_diff(p_ref[...], halo, i * tm, tm)
        db = d.astype(MXU_DT)
        dy = dy_ref[...]

        @pl.when(i == 0)
        def _():
            dw_ref[...] = jnp.zeros_like(dw_ref)
            dsc_ref[...] = jnp.zeros_like(dsc_ref)

        dsc_ref[...] += jnp.sum(dy * _dot(db, w_ref[...]), axis=0, keepdims=True)
        dys = (dy * sc_ref[...]).astype(MXU_DT)
        dw_ref[...] += _dot_tn(db, dys)
        dd = _dot_nt(dys, w_ref[...])
        dd_ref[...] = dd
        e_ref[...] = dd / cnt

    cur, prev = _pool_specs(tm, S // HALO)
    return _call(
        body, name="pool_bwd_a", grid=(S // tm,),
        in_specs=[cur, prev, _row_spec(tm, W_C), _full_spec((W_C, W_C)), _full_spec((1, W_C))],
        out_specs=[_row_spec(tm, W_C), _row_spec(tm, W_C), _full_spec((W_C, W_C)), _full_spec((1, W_C))],
        out_shape=[jax.ShapeDtypeStruct((S, W_C), F32), jax.ShapeDtypeStruct((S, W_C), F32),
                   jax.ShapeDtypeStruct((W_C, W_C), F32), jax.ShapeDtypeStruct((1, W_C), F32)],
        compiler_params=_params("arbitrary"),
    )(proj, proj, dy, wbd, scale)


def _pool_bwd_b(dd, e):
    S = dd.shape[0]
    tm = TM
    nb = S // tm

    def body(dd_ref, e_ref, en_ref, dp_ref):
        i = pl.program_id(0)
        halo = jnp.where(i < nb - 1, en_ref[...], 0.0)
        ext = jnp.concatenate([e_ref[...], halo], axis=0)
        n = ext.shape[0]
        g, _ = _window_lanes()
        acc = ext
        sums = []
        for sh in (1, 2, 4, 8):
            acc = acc + pltpu.roll(acc, shift=n - sh, axis=0)
            sums.append(acc[:tm])
        wsum = jnp.where(g == 0, sums[0], jnp.where(g == 1, sums[1], jnp.where(g == 2, sums[2], sums[3])))
        dp_ref[...] = (wsum - dd_ref[...]).astype(dp_ref.dtype)

    nxt = pl.BlockSpec((HALO, W_C), lambda i: (jnp.minimum((i + 1) * (tm // HALO), S // HALO - 1), 0))
    return _call(
        body, name="pool_bwd_b", grid=(nb,),
        in_specs=[_row_spec(tm, W_C), _row_spec(tm, W_C), nxt],
        out_specs=_row_spec(tm, W_C),
        out_shape=jax.ShapeDtypeStruct((S, W_C), MXU_DT),
        compiler_params=_params("parallel"),
    )(dd, e, e)


TN_FF = 1408
NB_FF = D_FF // TN_FF
CONV_ROWS = 8


def _conv(z_cur, z_halo, cwb, tm):
    ext = jnp.concatenate([z_halo, z_cur], axis=0)
    z2 = _shift_rows(ext, 2, tm, HALO)
    z1 = _shift_rows(ext, 1, tm, HALO)
    zc = cwb[3:4] + z2 * cwb[0:1] + z1 * cwb[1:2] + z_cur * cwb[2:3]
    return zc, z2, z1


def _ffn_specs(tm, order):
    def mk(f):
        return (lambda i, j: f(i, j)) if order == "ij" else (lambda j, i: f(i, j))
    hb = tm // HALO
    return [
        pl.BlockSpec((tm, TN_FF), mk(lambda i, j: (i, j))),
        pl.BlockSpec((tm, TN_FF), mk(lambda i, j: (i, j + NB_FF))),
        pl.BlockSpec((HALO, TN_FF), mk(lambda i, j: (jnp.maximum(i * hb - 1, 0), j))),
        pl.BlockSpec((HALO, TN_FF), mk(lambda i, j: (jnp.maximum(i * hb - 1, 0), j + NB_FF))),
        pl.BlockSpec((CONV_ROWS, TN_FF), mk(lambda i, j: (0, j))),
        pl.BlockSpec((CONV_ROWS, TN_FF), mk(lambda i, j: (0, j + NB_FF))),
    ]


def _conv_gate(z, cwb):
    S = z.shape[0]
    tm = TM

    def body(zg_ref, zu_ref, hg_ref, hu_ref, cg_ref, cu_ref, f_ref):
        first = pl.program_id(0) == 0
        g, _, _ = _conv(zg_ref[...].astype(F32), jnp.where(first, 0.0, hg_ref[...].astype(F32)), cg_ref[...], tm)
        u, _, _ = _conv(zu_ref[...].astype(F32), jnp.where(first, 0.0, hu_ref[...].astype(F32)), cu_ref[...], tm)
        f_ref[...] = (g * jax.nn.sigmoid(g) * u).astype(f_ref.dtype)

    return _call(
        body, name="conv_gate", grid=(S // tm, NB_FF), in_specs=_ffn_specs(tm, "ij"),
        out_specs=pl.BlockSpec((tm, TN_FF), lambda i, j: (i, j)),
        out_shape=jax.ShapeDtypeStruct((S, D_FF), MXU_DT),
        compiler_params=_params("parallel", "parallel"),
    )(z, z, z, z, cwb, cwb)


def _conv_gate_bwd(z, df, cwb):
    S = z.shape[0]
    tm = TM
    nb = S // tm
    te = tm + HALO

    def body(zg_ref, zu_ref, hg_ref, hu_ref, cg_ref, cu_ref, ng_ref, nu_ref, df_ref, dfn_ref,
             dzg_ref, dzu_ref, dcg_ref, dcu_ref):
        i = pl.program_id(1)
        first = i == 0
        last = i == nb - 1

        def conv_ext(prev_ref, cur_ref, next_ref, c):
            ze = jnp.concatenate([jnp.where(first, 0.0, prev_ref[...].astype(F32)), cur_ref[...].astype(F32),
                                  jnp.where(last, 0.0, next_ref[...].astype(F32))], axis=0)
            z2 = _shift_rows(ze, 2, te, HALO)
            z1 = _shift_rows(ze, 1, te, HALO)
            z0 = ze[HALO:]
            return c[3:4] + z2 * c[0:1] + z1 * c[1:2] + z0 * c[2:3], z2, z1, z0

        cg = cg_ref[...]
        cu = cu_ref[...]
        g, g2, g1, g0 = conv_ext(hg_ref, zg_ref, ng_ref, cg)
        u, u2, u1, u0 = conv_ext(hu_ref, zu_ref, nu_ref, cu)
        df = jnp.concatenate([df_ref[...].astype(F32), jnp.where(last, 0.0, dfn_ref[...].astype(F32))], axis=0)
        sg = jax.nn.sigmoid(g)
        dgv = df * u * (sg * (1.0 + g * (1.0 - sg)))
        duv = df * (g * sg)

        def conv_t(d, c):
            return d[:tm] * c[2:3] + _shift_rows(d, -1, tm, 0) * c[1:2] + _shift_rows(d, -2, tm, 0) * c[0:1]

        dzg_ref[...] = conv_t(dgv, cg).astype(dzg_ref.dtype)
        dzu_ref[...] = conv_t(duv, cu).astype(dzu_ref.dtype)

        @pl.when(first)
        def _():
            dcg_ref[...] = jnp.zeros_like(dcg_ref)
            dcu_ref[...] = jnp.zeros_like(dcu_ref)

        rid = lax.broadcasted_iota(jnp.int32, (CONV_ROWS, TN_FF), 0)

        def taps(dv, s2, s1, s0):
            dv = dv[:tm]
            sums = [jnp.sum(dv * s2[:tm], axis=0, keepdims=True), jnp.sum(dv * s1[:tm], axis=0, keepdims=True),
                    jnp.sum(dv * s0[:tm], axis=0, keepdims=True), jnp.sum(dv, axis=0, keepdims=True)]
            out = jnp.zeros((CONV_ROWS, TN_FF), F32)
            for k, v in enumerate(sums):
                out = jnp.where(rid == k, v, out)
            return out

        dcg_ref[...] += taps(dgv, g2, g1, g0)
        dcu_ref[...] += taps(duv, u2, u1, u0)

    hb = tm // HALO
    acc = pl.BlockSpec((CONV_ROWS, TN_FF), lambda j, i: (0, j))
    tile = pl.BlockSpec((tm, TN_FF), lambda j, i: (i, j))

    def after(col0):
        return pl.BlockSpec((HALO, TN_FF), lambda j, i: (jnp.minimum((i + 1) * hb, S // HALO - 1), j + col0))

    return _call(
        body, name="conv_gate_bwd", grid=(NB_FF, nb),
        in_specs=_ffn_specs(tm, "ji") + [after(0), after(NB_FF), tile, after(0)],
        out_specs=[tile, tile, acc, acc],
        out_shape=[jax.ShapeDtypeStruct((S, D_FF), MXU_DT), jax.ShapeDtypeStruct((S, D_FF), MXU_DT),
                   jax.ShapeDtypeStruct((CONV_ROWS, D_FF), F32), jax.ShapeDtypeStruct((CONV_ROWS, D_FF), F32)],
        compiler_params=_params("parallel", "arbitrary"),
    )(z, z, z, z, cwb, cwb, z, z, df, df)


def _layer_consts(w, l):
    wm = _tril_weights(w["sgu_w"][l])
    eye = jnp.eye(4, dtype=F32)
    wbd = (w["pool_w"][l][:, :, None, :] * eye[:, None, :, None]).reshape(W_C, W_C)
    cwb = jnp.concatenate([w["conv_w"][l], w["conv_b"][l][None], jnp.zeros((CONV_ROWS - 4, 2 * D_FF), F32)], axis=0)
    return dict(
        g1=w["norm1_g"][l][None], g2=w["norm2_g"][l][None], gm=w["mix_norm_g"][l][None],
        sng=w["sgu_norm_g"][l][None], wm=wm.astype(MXU_DT), wmt=jnp.swapaxes(wm, 1, 2).astype(MXU_DT),
        bias=jnp.repeat(jnp.transpose(w["sgu_b"][l]), HEAD_DIM, axis=1),
        wbd=wbd.astype(MXU_DT), scale=w["pool_scale"][l][None], cwb=cwb,
        w_in=w["w_in"][l], w_o=w["w_o"][l], w_up=w["w_up"][l], w_down=w["w_down"][l],
    )


def _local_step(x, tgt, w):
    gmat = _group_matrix()
    tri = _tri_matrix()
    trit = jnp.transpose(tri)
    saved = []
    for l in range(DEPTH):
        c = _layer_consts(w, l)
        h1, proj, proj_b = _rms_mm(x, c["g1"], c["w_in"], IN_COLS // 3, "in_proj", (F32, MXU_DT))
        ya = _sgu_fwd(proj, c["sng"], c["wm"], c["bias"], gmat)
        yb, rb = _sb_fwd(proj_b, tri)
        yc = _pool_fwd(proj, c["wbd"], c["scale"])
        x2, yn = _mix_out(ya, yb, yc, c["gm"], c["w_o"], x, gmat)
        h2, z = _rms_mm(x2, c["g2"], c["w_up"], TN_FF, "up_proj", (MXU_DT,))
        f = _conv_gate(z, c["cwb"])
        x3 = _mm_res(f, c["w_down"], x2, "down_proj")
        saved.append(dict(c=c, x=x, proj=proj, proj_b=proj_b, h1=h1, ya=ya, yb=yb, yc=yc, rb=rb, x2=x2, yn=yn,
                          z=z, h2=h2, f=f))
        x = x3

    dx, d_final_g, loss8 = _loss_head(x, w["final_g"][None], tgt)
    grads = {n: [None] * DEPTH for n in ("norm1_g", "w_in", "sgu_norm_g", "sgu_w", "sgu_b", "pool_w", "pool_scale",
                                         "mix_norm_g", "w_o", "norm2_g", "w_up", "conv_w", "conv_b", "w_down")}
    for l in reversed(range(DEPTH)):
        s = saved[l]
        c = s["c"]
        df = _mm_nt(dx, c["w_down"], "down_proj_bwd")
        grads["w_down"][l] = _mm_tn(s["f"], dx, "down_proj_wgrad").reshape(N_CHIPS, D_FF // N_CHIPS, D_MODEL)
        dz_g, dz_u, dcg, dcu = _conv_gate_bwd(s["z"], df, c["cwb"])
        dcwb = jnp.concatenate([dcg, dcu], axis=1)
        grads["conv_w"][l] = dcwb[:3]
        grads["conv_b"][l] = dcwb[3]
        grads["w_up"][l] = jnp.concatenate([_mm_tn(s["h2"], dz_g, "up_proj_wgrad_gate", col_tiles=True),
                                            _mm_tn(s["h2"], dz_u, "up_proj_wgrad_value", col_tiles=True)], axis=0)
        dx2, dg2 = _mm_nt_rmsbwd([(dz_g, c["w_up"][:, :D_FF]), (dz_u, c["w_up"][:, D_FF:])],
                                 s["x2"], c["g2"], dx, "up_proj_bwd")
        grads["norm2_g"][l] = dg2[0]
        grads["w_o"][l] = _mm_tn(s["yn"], dx2, "out_proj_wgrad").reshape(N_CHIPS, D_MODEL // N_CHIPS, D_MODEL)
        dya, dyb, dyc, dgm = _mix_out_bwd(dx2, c["w_o"], s["ya"], s["yb"], s["yc"], c["gm"], gmat)
        grads["mix_norm_g"][l] = dgm[0]
        dd, e, dwbd, dscale = _pool_bwd_a(s["proj"], dyc, c["wbd"], c["scale"])
        dp = _pool_bwd_b(dd, e)
        grads["pool_w"][l] = jnp.stack([dwbd[g * 64:(g + 1) * 64, g * 64:(g + 1) * 64] for g in range(4)])
        grads["pool_scale"][l] = dscale[0]
        dq, dk, dv = _sb_bwd(s["proj_b"], dyb, s["rb"], tri, trit)
        da, dwm, dbias, dsng = _sgu_bwd(s["proj"], dya, c["sng"], c["wm"], c["wmt"], c["bias"], gmat)
        grads["sgu_w"][l] = dwm
        grads["sgu_b"][l] = jnp.transpose(jnp.sum(dbias.reshape(CHUNK, 4, HEAD_DIM), axis=-1))
        grads["sgu_norm_g"][l] = dsng[0]
        dproj = jnp.concatenate([da, dq, dk.astype(MXU_DT), dv.astype(MXU_DT), dp], axis=1)
        dw_in = _mm_tn(s["h1"], dproj, "in_proj_wgrad")
        grads["w_in"][l] = jnp.transpose(dw_in.reshape(D_MODEL, N_CHIPS, IN_COLS // N_CHIPS), (1, 0, 2))
        dx, dg1 = _mm_nt_rmsbwd([(dproj, c["w_in"])], s["x"], c["g1"], dx2, "in_proj_bwd")
        grads["norm1_g"][l] = dg1[0]

    out = {n: jnp.stack(v) for n, v in grads.items()}
    out["final_g"] = d_final_g[0]
    return loss8[0, 0], dx, out


MESH = pl.DeviceIdType.MESH
ANY = pl.BlockSpec(memory_space=pl.ANY)


def _all_gather(block):
    m_per, n = block.shape[0] // 2, block.shape[1]

    def body(x_ref, out_ref, send_sems, recv_sems):
        x, y, c = lax.axis_index("x"), lax.axis_index("y"), lax.axis_index("c")
        me, sibling = (x, y, c), (x, y, 1 - c)
        chips = [(1 - x, y), (x, 1 - y), (1 - x, 1 - y)]
        my_half = x_ref.at[pl.ds(c * m_per, m_per), :]

        def rows(px, py, pc):
            return out_ref.at[pl.ds((4 * px + 2 * py + pc) * m_per, m_per), :]

        def copy(k, half, to, src=None):
            return pltpu.make_async_remote_copy(
                src_ref=rows(*half) if src is None else src, dst_ref=rows(*half),
                send_sem=send_sems.at[k], recv_sem=recv_sems.at[k], device_id=to, device_id_type=MESH)

        first = [copy(j, me, (*chip, c), src=my_half) for j, chip in enumerate(chips)]
        for cp in first:
            cp.start()
        passed = [copy(3 + j, (*chip, c), sibling) for j, chip in enumerate(chips)]
        for j, chip in enumerate(chips):
            copy(j, (*chip, c), me).wait_recv()
            passed[j].start()
        for j, chip in enumerate(chips):
            copy(3 + j, (*chip, 1 - c), me).wait_recv()
        for cp in first + passed:
            cp.wait_send()

    return _call(
        body, name="weight_all_gather",
        out_shape=jax.ShapeDtypeStruct((8 * m_per, n), block.dtype),
        in_specs=[ANY], out_specs=ANY,
        scratch_shapes=[pltpu.SemaphoreType.DMA((6,)), pltpu.SemaphoreType.DMA((6,))],
    )(block)


def _row_tile(r):
    return r if r <= 704 else 256


def _grad_swap(bigs, sp):
    n = len(bigs)

    def body(*refs):
        ins, outs = refs[:n + 1], refs[n + 1:2 * n + 2]
        send_sems, recv_sems = refs[2 * n + 2:]
        x, y, c = lax.axis_index("x"), lax.axis_index("y"), lax.axis_index("c")
        srcs = [ins[a].at[1 - c] for a in range(n)] + [ins[n].at[:, pl.ds((1 - c) * SP_HALF, SP_HALF), :]]
        copies = [pltpu.make_async_remote_copy(src_ref=srcs[a], dst_ref=outs[a], send_sem=send_sems.at[a],
                                               recv_sem=recv_sems.at[a], device_id=(x, y, 1 - c),
                                               device_id_type=MESH) for a in range(n + 1)]
        for cp in copies:
            cp.start()
        for cp in copies:
            cp.wait()

    shapes = [jax.ShapeDtypeStruct(b.shape[1:], b.dtype) for b in bigs]
    shapes.append(jax.ShapeDtypeStruct((N_CHIPS, SP_HALF, D_MODEL), sp.dtype))
    return _call(
        body, name="grad_swap_cores", out_shape=shapes, in_specs=[ANY] * (n + 1), out_specs=[ANY] * (n + 1),
        scratch_shapes=[pltpu.SemaphoreType.DMA((n + 1,)), pltpu.SemaphoreType.DMA((n + 1,))],
    )(*bigs, sp)


def _pair_add(g, r, c_arr, name, out_dtype):
    _, k, rr, cc = g.shape
    tr = _row_tile(rr)

    def body(c_ref, g_ref, r_ref, o_ref):
        o_ref[...] = (g_ref[...] + r_ref[...]).astype(o_ref.dtype)

    spec = pl.BlockSpec((1, tr, cc), lambda kk, i, c_ref: (kk, i, 0))
    grid_spec = pltpu.PrefetchScalarGridSpec(
        num_scalar_prefetch=1, grid=(k, rr // tr),
        in_specs=[pl.BlockSpec((None, 1, tr, cc), lambda kk, i, c_ref: (c_ref[0], kk, i, 0)), spec], out_specs=spec)
    return _call(body, name=name, grid_spec=grid_spec, out_shape=jax.ShapeDtypeStruct((k, rr, cc), out_dtype),
                 compiler_params=_params("parallel", "parallel"))(c_arr, g, r)


def _pair_add_small(sp, r, c_arr):
    def body(c_ref, g_ref, r_ref, o_ref):
        o_ref[...] = g_ref[...] + r_ref[...]

    spec = pl.BlockSpec((1, SP_HALF, D_MODEL), lambda kk, c_ref: (kk, 0, 0))
    grid_spec = pltpu.PrefetchScalarGridSpec(
        num_scalar_prefetch=1, grid=(N_CHIPS,),
        in_specs=[pl.BlockSpec((1, SP_HALF, D_MODEL), lambda kk, c_ref: (kk, c_ref[0], 0)), spec], out_specs=spec)
    return _call(body, name="grad_add_cores_small", grid_spec=grid_spec,
                 out_shape=jax.ShapeDtypeStruct(r.shape, F32), compiler_params=_params("parallel"))(c_arr, sp, r)


def _grad_exchange(hs):
    n = len(hs)

    def body(*refs):
        ins, outs = refs[:n], refs[n:2 * n]
        send_sems, recv_sems = refs[2 * n:]
        x, y, c = lax.axis_index("x"), lax.axis_index("y"), lax.axis_index("c")
        my_chip = 2 * x + y
        chips = [(1 - x, y), (x, 1 - y), (1 - x, 1 - y)]

        def copy(a, k, src_chip, dst_chip):
            px, py = chips[k]
            return pltpu.make_async_remote_copy(
                src_ref=ins[a].at[src_chip], dst_ref=outs[a].at[dst_chip], send_sem=send_sems.at[a, k],
                recv_sem=recv_sems.at[a, k], device_id=(px, py, c), device_id_type=MESH)

        sends = [copy(a, k, 2 * chips[k][0] + chips[k][1], my_chip) for k in range(3) for a in range(n)]
        for cp in sends:
            cp.start()
        for k in range(3):
            for a in range(n):
                copy(a, k, my_chip, 2 * chips[k][0] + chips[k][1]).wait_recv()
        for cp in sends:
            cp.wait_send()

    return _call(
        body, name="grad_exchange_chips", out_shape=[jax.ShapeDtypeStruct(h.shape, h.dtype) for h in hs],
        in_specs=[ANY] * n, out_specs=[ANY] * n,
        scratch_shapes=[pltpu.SemaphoreType.DMA((n, 3)), pltpu.SemaphoreType.DMA((n, 3))],
    )(*hs)


def _sum_chips(a, c_arr, name):
    _, r, cc = a.shape
    tr = _row_tile(r)

    def body(c_ref, a_ref, o_ref):
        o_ref[...] = ((a_ref[0].astype(F32) + a_ref[1].astype(F32)) + a_ref[2].astype(F32)) + a_ref[3].astype(F32)

    grid_spec = pltpu.PrefetchScalarGridSpec(
        num_scalar_prefetch=1, grid=(r // tr,),
        in_specs=[pl.BlockSpec((N_CHIPS, tr, cc), lambda i, c_ref: (0, i, 0))],
        out_specs=pl.BlockSpec((None, tr, cc), lambda i, c_ref: (c_ref[0], i, 0)))
    return _call(body, name=name, grid_spec=grid_spec, out_shape=jax.ShapeDtypeStruct((2, r, cc), F32),
                 compiler_params=_params("parallel"))(c_arr, a)


def _grad_share(bufs):
    n = len(bufs)

    def body(*refs):
        outs = refs[n:2 * n]
        send_sems, recv_sems = refs[2 * n:]
        x, y, c = lax.axis_index("x"), lax.axis_index("y"), lax.axis_index("c")
        copies = [pltpu.make_async_remote_copy(src_ref=outs[a].at[c], dst_ref=outs[a].at[c], send_sem=send_sems.at[a],
                                               recv_sem=recv_sems.at[a], device_id=(x, y, 1 - c),
                                               device_id_type=MESH) for a in range(n)]
        for cp in copies:
            cp.start()
        for a in range(n):
            pltpu.make_async_remote_copy(src_ref=outs[a].at[c], dst_ref=outs[a].at[1 - c], send_sem=send_sems.at[a],
                                         recv_sem=recv_sems.at[a], device_id=(x, y, 1 - c),
                                         device_id_type=MESH).wait_recv()
        for cp in copies:
            cp.wait_send()

    return _call(
        body, name="grad_share_cores", out_shape=[jax.ShapeDtypeStruct(b.shape, b.dtype) for b in bufs],
        in_specs=[ANY] * n, out_specs=[ANY] * n, input_output_aliases={a: a for a in range(n)},
        scratch_shapes=[pltpu.SemaphoreType.DMA((n,)), pltpu.SemaphoreType.DMA((n,))],
    )(*bufs)


def _adamw_math(g_ref, w_ref, m_ref, v_ref, d_ref, nm_ref, nv_ref):
    gv = g_ref[...]
    nm = ADAM_B1 * m_ref[...] + (1.0 - ADAM_B1) * gv
    nv = ADAM_B2 * v_ref[...] + (1.0 - ADAM_B2) * (gv * gv)
    m_hat = nm / (1.0 - ADAM_B1 ** ADAM_STEP)
    v_hat = nv / (1.0 - ADAM_B2 ** ADAM_STEP)
    d_ref[...] = -ADAM_LR * (m_hat / (jnp.sqrt(v_hat) + ADAM_EPS) + ADAM_WD * w_ref[...])
    nm_ref[...] = nm
    nv_ref[...] = nv


def _adamw_big(g, w, m, v, name):
    d, r, c = g.shape
    tr = r if r <= 704 else 256
    spec = pl.BlockSpec((1, tr, c), lambda l, i: (l, i, 0))

    def body(*refs):
        _adamw_math(*refs)

    shp = jax.ShapeDtypeStruct(g.shape, F32)
    return _call(body, name=name, grid=(d, r // tr), in_specs=[spec] * 4, out_specs=[spec] * 3,
                 out_shape=[shp, shp, shp], compiler_params=_params("parallel", "parallel"))(g, w, m, v)


def _adamw_small(gs, ws, ms, vs):
    n = len(gs)

    def body(*refs):
        ins, outs = refs[:4 * n], refs[4 * n:]
        for k in range(n):
            _adamw_math(ins[k], ins[n + k], ins[2 * n + k], ins[3 * n + k], outs[k], outs[n + k], outs[2 * n + k])

    shp = [jax.ShapeDtypeStruct(g.shape, F32) for g in gs]
    res = _call(body, name="adamw_small", out_shape=shp * 3)(*gs, *ws, *ms, *vs)
    return res[:n], res[n:2 * n], res[2 * n:]


def _rows(a, rows):
    flat = a.reshape(-1)
    return jnp.pad(flat, (0, rows * D_MODEL - flat.shape[0])).reshape(rows, D_MODEL)


def _small_rows(p, extra=None):
    parts = [p[n].reshape(-1) for n in SMALL_NAMES]
    if extra is not None:
        parts.append(extra.reshape(-1))
    flat = jnp.concatenate(parts)
    return jnp.pad(flat, (0, ROWS_SMALL * D_MODEL - flat.shape[0])).reshape(ROWS_SMALL, D_MODEL)


CONV_SHARD = (DEPTH, 3, 2 * D_FF // N_CHIPS)
N_CONV_SHARD = DEPTH * 3 * (2 * D_FF // N_CHIPS)


def _small_pack(g, loss):
    conv = jnp.transpose(g["conv_w"].reshape(DEPTH, 3, N_CHIPS, 2 * D_FF // N_CHIPS), (2, 0, 1, 3))
    conv = jnp.stack([_rows(conv[k], ROWS_CONV) for k in range(N_CHIPS)])
    small = jnp.broadcast_to(_small_rows(g, loss), (N_CHIPS, ROWS_SMALL, D_MODEL))
    return jnp.concatenate([conv, small], axis=1)


def _unpack_small(pack):
    out = {"conv_w": pack[:ROWS_CONV].reshape(-1)[:N_CONV_SHARD].reshape(CONV_SHARD)}
    flat = pack[ROWS_CONV:].reshape(-1)
    k = 0
    for name in SMALL_NAMES:
        shape = SMALL_SHAPES[name]
        n = 1
        for d in shape:
            n *= d
        out[name] = flat[k:k + n].reshape(shape)
        k += n
    out["extra"] = flat[k]
    return out


def _gather_weights(p, c):
    conv_bits = lax.bitcast_convert_type(_rows(p["conv_w"], ROWS_CONV), jnp.bfloat16)
    conv_bits = conv_bits.reshape(2 * ROWS_CONV, D_MODEL)
    bf = jnp.bfloat16
    block = jnp.concatenate([_rows(p["w_in"].astype(bf), ROWS_IN), _rows(p["w_o"].astype(bf), ROWS_O),
                             _rows(p["w_up"].astype(bf), ROWS_UP), _rows(p["w_down"].astype(bf), ROWS_DOWN),
                             conv_bits], axis=0)
    my_chip = 2 * lax.axis_index("x") + lax.axis_index("y")
    allw = _all_gather(block).reshape(N_CHIPS, ROWS_GATHER, D_MODEL)
    allw = lax.dynamic_update_index_in_dim(allw, block, my_chip, 0)
    o = 0

    def take(rows, shape):
        nonlocal o
        a = allw[:, o:o + rows].reshape((N_CHIPS,) + shape)
        o += rows
        return a

    w_in = take(ROWS_IN, (DEPTH, D_MODEL, IN_COLS // N_CHIPS))
    w_o = take(ROWS_O, (DEPTH, D_MODEL // N_CHIPS, D_MODEL))
    w_up = take(ROWS_UP, (DEPTH, D_MODEL, 2 * D_FF // N_CHIPS))
    w_down = take(ROWS_DOWN, (DEPTH, D_FF // N_CHIPS, D_MODEL))
    conv = lax.bitcast_convert_type(allw[:, o:o + 2 * ROWS_CONV].reshape(N_CHIPS, ROWS_CONV, D_MODEL, 2), F32)
    conv = conv.reshape(N_CHIPS, -1)[:, :DEPTH * 3 * (2 * D_FF // N_CHIPS)].reshape(N_CHIPS, DEPTH, 3, 2 * D_FF // N_CHIPS)

    def by_cols(a):
        k, d, r, wd = a.shape
        return jnp.transpose(a, (1, 2, 0, 3)).reshape(d, r, k * wd)

    def by_rows(a):
        k, d, hgt, cc = a.shape
        return jnp.transpose(a, (1, 0, 2, 3)).reshape(d, k * hgt, cc)

    return dict(w_in=by_cols(w_in), w_o=by_rows(w_o), w_up=by_cols(w_up), w_down=by_rows(w_down), conv_w=by_cols(conv))


def _reduce_grads(grads, loss, c):
    bigs = [grads[n] for n in BIG_NAMES[:4]]
    sp = _small_pack(grads, loss)
    c_arr = jnp.reshape(c, (1,)).astype(jnp.int32)
    got = _grad_swap(bigs, sp)
    pair = [_pair_add(bigs[a], got[a], c_arr, "grad_add_cores_" + BIG_NAMES[a], ICI_DT) for a in range(4)]
    pair.append(_pair_add_small(sp, got[4], c_arr))
    my_chip = 2 * lax.axis_index("x") + lax.axis_index("y")
    parts = [lax.dynamic_update_index_in_dim(got_k, lax.dynamic_index_in_dim(own, my_chip, 0, keepdims=False),
                                             my_chip, 0) for got_k, own in zip(_grad_exchange(pair), pair)]
    total = [_sum_chips(parts[a], c_arr, "grad_sum_chips_" + (BIG_NAMES[:4] + ("small",))[a]) for a in range(5)]
    shared = _grad_share(total)
    out = dict(zip(BIG_NAMES[:4], shared[:4]))
    out.update(_unpack_small(shared[4].reshape(2 * SP_HALF, D_MODEL)))
    return out


def kernel(x, norm1_g, w_in, sgu_norm_g, sgu_w, sgu_b, pool_w, pool_scale, mix_norm_g, w_o, norm2_g, w_up, conv_w, conv_b, w_down, final_g, loss_target, m_norm1_g, m_w_in, m_sgu_norm_g, m_sgu_w, m_sgu_b, m_pool_w, m_pool_scale, m_mix_norm_g, m_w_o, m_norm2_g, m_w_up, m_conv_w, m_conv_b, m_w_down, m_final_g, v_norm1_g, v_w_in, v_sgu_norm_g, v_sgu_w, v_sgu_b, v_pool_w, v_pool_scale, v_mix_norm_g, v_w_o, v_norm2_g, v_w_up, v_conv_w, v_conv_b, v_w_down, v_final_g):
    names = ("norm1_g", "w_in", "sgu_norm_g", "sgu_w", "sgu_b", "pool_w", "pool_scale", "mix_norm_g", "w_o",
             "norm2_g", "w_up", "conv_w", "conv_b", "w_down", "final_g")
    p = dict(zip(names, (norm1_g, w_in, sgu_norm_g, sgu_w, sgu_b, pool_w, pool_scale, mix_norm_g, w_o, norm2_g,
                         w_up, conv_w, conv_b, w_down, final_g)))
    pm = dict(zip(names, (m_norm1_g, m_w_in, m_sgu_norm_g, m_sgu_w, m_sgu_b, m_pool_w, m_pool_scale, m_mix_norm_g,
                          m_w_o, m_norm2_g, m_w_up, m_conv_w, m_conv_b, m_w_down, m_final_g)))
    pv = dict(zip(names, (v_norm1_g, v_w_in, v_sgu_norm_g, v_sgu_w, v_sgu_b, v_pool_w, v_pool_scale, v_mix_norm_g,
                          v_w_o, v_norm2_g, v_w_up, v_conv_w, v_conv_b, v_w_down, v_final_g)))
    c = lax.axis_index("c")
    full = dict(p)
    full.update(_gather_weights(p, c))

    loss, dx, grads = _local_step(x[0], loss_target[0], full)

    g = _reduce_grads(grads, loss, c)
    d, nm, nv = {}, {}, {}
    for n in BIG_NAMES:
        d[n], nm[n], nv[n] = _adamw_big(g[n], p[n], pm[n], pv[n], "adamw_" + n)

    def two_d(a):
        return a.reshape(1, -1) if a.ndim == 1 else a

    ds, ms, vs = _adamw_small([two_d(g[n]) for n in SMALL_NAMES], [two_d(p[n]) for n in SMALL_NAMES],
                              [two_d(pm[n]) for n in SMALL_NAMES], [two_d(pv[n]) for n in SMALL_NAMES])
    for k, n in enumerate(SMALL_NAMES):
        d[n], nm[n], nv[n] = (a.reshape(p[n].shape) for a in (ds[k], ms[k], vs[k]))
    return (g["extra"], dx[None], *[g[n] for n in names], *[d[n] for n in names], *[nm[n] for n in names],
            *[nv[n] for n in names])
```

```python
import functools

import jax
import jax.numpy as jnp
from jax import lax
from jax.experimental import pallas as pl
from jax.experimental.pallas import tpu as pltpu

F32 = jnp.float32
MXU_DT = jnp.bfloat16

D_MODEL = 1024
DEPTH = 2
HEAD_DIM = 64
W_A = 256
W_B = 512
W_C = 256
IN_COLS = 2 * W_A + 3 * W_B + W_C
CHUNK = 128
POOL_WINDOWS = (2, 4, 8, 16)
D_FF = 2816
EPS = 1e-6
N_CHIPS = 4

ADAM_LR = 0.001
ADAM_B1 = 0.9
ADAM_B2 = 0.999
ADAM_EPS = 1e-08
ADAM_WD = 0.01
ADAM_STEP = 10

LANES = 128
TQ = 256
TK = 256
TM = 256
TM_MM = 512
HALO = 16
VMEM_LIMIT = 56 * 1024 * 1024

ROWS_CONV = 16
ROWS_SMALL = 240
SP_HALF = (ROWS_CONV + ROWS_SMALL) // 2
ICI_DT = jnp.bfloat16

BIG_NAMES = ("w_in", "w_o", "w_up", "w_down", "conv_w")
SMALL_NAMES = ("norm1_g", "sgu_norm_g", "sgu_w", "sgu_b", "pool_w", "pool_scale",
               "mix_norm_g", "norm2_g", "conv_b", "final_g")
SMALL_SHAPES = {
    "norm1_g": (DEPTH, D_MODEL), "sgu_norm_g": (DEPTH, W_A), "sgu_w": (DEPTH, 4, CHUNK, CHUNK),
    "sgu_b": (DEPTH, 4, CHUNK), "pool_w": (DEPTH, 4, 64, 64), "pool_scale": (DEPTH, W_C),
    "mix_norm_g": (DEPTH, D_MODEL), "norm2_g": (DEPTH, D_MODEL), "conv_b": (DEPTH, 2 * D_FF),
    "final_g": (D_MODEL,),
}


def _call(body, **kw):
    return pl.pallas_call(body, **kw)


def _params(*sem):
    return pltpu.CompilerParams(dimension_semantics=sem, vmem_limit_bytes=VMEM_LIMIT)


def _dot(a, b):
    return jnp.dot(a, b, preferred_element_type=F32)


def _dot_nt(a, b):
    return lax.dot_general(a, b, (((1,), (1,)), ((), ())), preferred_element_type=F32)


def _dot_tn(a, b):
    return lax.dot_general(a, b, (((0,), (0,)), ((), ())), preferred_element_type=F32)


def _group_mean(sq, gmat):
    sqb = sq.astype(MXU_DT)
    cols = [_dot(sqb[:, b * LANES:(b + 1) * LANES], gmat) for b in range(sq.shape[1] // LANES)]
    return cols[0] if len(cols) == 1 else jnp.concatenate(cols, axis=-1)


def _group_matrix():
    r = jnp.arange(LANES)
    return jnp.where((r[:, None] // HEAD_DIM) == (r[None, :] // HEAD_DIM), 1.0 / HEAD_DIM, 0.0).astype(MXU_DT)


def _tile(n):
    return max(t for t in range(LANES, 1536 + 1, LANES) if n % t == 0)


def _row_spec(tm, cols, col_block=0):
    return pl.BlockSpec((tm, cols), lambda i, cb=col_block: (i, cb))


def _full_spec(shape):
    nd = len(shape)
    return pl.BlockSpec(shape, lambda *_: (0,) * nd)


def _rms_mm(x, g, w, tn, name, out_dtypes):
    S, D = x.shape
    N = w.shape[1]
    tm = TM_MM

    def body(x_ref, g_ref, w_ref, h_ref, *o_refs):
        xv = x_ref[...]
        r = lax.rsqrt(jnp.mean(xv * xv, axis=-1, keepdims=True) + EPS)
        h = (xv * r * g_ref[...]).astype(h_ref.dtype)
        h_ref[...] = h
        for n0 in range(0, N, tn):
            acc = _dot(h, w_ref[:, n0:n0 + tn])
            for o_ref in o_refs:
                o_ref[:, n0:n0 + tn] = acc.astype(o_ref.dtype)

    return _call(
        body, name=name, grid=(S // tm,),
        in_specs=[_row_spec(tm, D), _full_spec((1, D)), _full_spec((D, N))],
        out_specs=[_row_spec(tm, D)] + [_row_spec(tm, N) for _ in out_dtypes],
        out_shape=[jax.ShapeDtypeStruct((S, D), MXU_DT)] + [jax.ShapeDtypeStruct((S, N), dt) for dt in out_dtypes],
        compiler_params=_params("parallel"),
    )(x, g, w)


def _mm_res(a, w, res, name):
    S, K = a.shape
    N = w.shape[1]
    tm = TM_MM

    def body(a_ref, w_ref, r_ref, o_ref):
        o_ref[...] = r_ref[...] + _dot(a_ref[...], w_ref[...])

    return _call(
        body, name=name, grid=(S // tm,),
        in_specs=[_row_spec(tm, K), _full_spec((K, N)), _row_spec(tm, N)],
        out_specs=_row_spec(tm, N),
        out_shape=jax.ShapeDtypeStruct((S, N), F32),
        compiler_params=_params("parallel"),
    )(a, w, res)


def _mm_nt(a, w, name):
    S, K = a.shape
    N = w.shape[0]
    tm = TM_MM

    def body(a_ref, w_ref, o_ref):
        o_ref[...] = _dot_nt(a_ref[...].astype(MXU_DT), w_ref[...]).astype(o_ref.dtype)

    return _call(
        body, name=name, grid=(S // tm,),
        in_specs=[_row_spec(tm, K), _full_spec((N, K))],
        out_specs=_row_spec(tm, N),
        out_shape=jax.ShapeDtypeStruct((S, N), MXU_DT),
        compiler_params=_params("parallel"),
    )(a, w)


def _mm_tn(a, b, name, col_tiles=False):
    S, K1 = a.shape
    N = b.shape[1]
    ts = TM_MM
    tk = _tile(K1)
    tn = _tile(N)
    if col_tiles:
        out_spec = pl.BlockSpec((None, tk, tn), lambda m, n, s: (n, m, 0))
        out_shape = jax.ShapeDtypeStruct((N // tn, K1, tn), F32)
    else:
        out_spec = pl.BlockSpec((tk, tn), lambda m, n, s: (m, n))
        out_shape = jax.ShapeDtypeStruct((K1, N), F32)

    def body(a_ref, b_ref, o_ref):
        @pl.when(pl.program_id(2) == 0)
        def _():
            o_ref[...] = jnp.zeros_like(o_ref)

        o_ref[...] += _dot_tn(a_ref[...], b_ref[...].astype(MXU_DT))

    return _call(
        body, name=name, grid=(K1 // tk, N // tn, S // ts),
        in_specs=[pl.BlockSpec((ts, tk), lambda m, n, s: (s, m)),
                  pl.BlockSpec((ts, tn), lambda m, n, s: (s, n))],
        out_specs=out_spec, out_shape=out_shape,
        compiler_params=_params("parallel", "parallel", "arbitrary"),
    )(a, b)


def _mm_nt_rmsbwd(pairs, x, g, dres, name):
    S, D = x.shape
    tm = TM
    n = len(pairs)

    def body(*refs):
        a_refs = refs[:n]
        w_refs = refs[n:2 * n]
        x_ref, g_ref, r_ref, dx_ref, dg_ref = refs[2 * n:]
        dh = _dot_nt(a_refs[0][...], w_refs[0][...])
        for k in range(1, n):
            dh += _dot_nt(a_refs[k][...], w_refs[k][...])
        xv = x_ref[...]
        r = lax.rsqrt(jnp.mean(xv * xv, axis=-1, keepdims=True) + EPS)
        xhat = xv * r

        @pl.when(pl.program_id(0) == 0)
        def _():
            dg_ref[...] = jnp.zeros_like(dg_ref)

        dg_ref[...] += jnp.sum(dh * xhat, axis=0, keepdims=True)
        dxh = dh * g_ref[...]
        dx_ref[...] = r_ref[...] + r * (dxh - xhat * jnp.mean(dxh * xhat, axis=-1, keepdims=True))

    in_specs = ([_row_spec(tm, a.shape[1]) for a, _ in pairs] + [_full_spec(w.shape) for _, w in pairs]
                + [_row_spec(tm, D), _full_spec((1, D)), _row_spec(tm, D)])
    return _call(
        body, name=name, grid=(S // tm,), in_specs=in_specs,
        out_specs=[_row_spec(tm, D), _full_spec((1, D))],
        out_shape=[jax.ShapeDtypeStruct((S, D), F32), jax.ShapeDtypeStruct((1, D), F32)],
        compiler_params=_params("arbitrary"),
    )(*[a for a, _ in pairs], *[w for _, w in pairs], x, g, dres)


def _loss_head(x, g, tgt):
    S, D = x.shape
    tm = TM

    def body(x_ref, g_ref, t_ref, dx_ref, dg_ref, l_ref):
        xv = x_ref[...]
        r = lax.rsqrt(jnp.mean(xv * xv, axis=-1, keepdims=True) + EPS)
        xhat = xv * r
        diff = xhat * g_ref[...] - t_ref[...]

        @pl.when(pl.program_id(0) == 0)
        def _():
            dg_ref[...] = jnp.zeros_like(dg_ref)
            l_ref[...] = jnp.zeros_like(l_ref)

        l_ref[...] += jnp.full(l_ref.shape, 0.5 * jnp.sum(jnp.mean(diff * diff, axis=-1, keepdims=True)), F32)
        dout = diff * (1.0 / D)
        dg_ref[...] += jnp.sum(dout * xhat, axis=0, keepdims=True)
        dxh = dout * g_ref[...]
        dx_ref[...] = r * (dxh - xhat * jnp.mean(dxh * xhat, axis=-1, keepdims=True))

    return _call(
        body, name="loss_head", grid=(S // tm,),
        in_specs=[_row_spec(tm, D), _full_spec((1, D)), _row_spec(tm, D)],
        out_specs=[_row_spec(tm, D), _full_spec((1, D)), _full_spec((8, LANES))],
        out_shape=[jax.ShapeDtypeStruct((S, D), F32), jax.ShapeDtypeStruct((1, D), F32),
                   jax.ShapeDtypeStruct((8, LANES), F32)],
        compiler_params=_params("arbitrary"),
    )(x, g, tgt)


def _mix_out(ya, yb, yc, gm, wo, x, gmat):
    S = x.shape[0]
    tm = TM

    def body(ya_ref, yb_ref, yc_ref, gm_ref, wo_ref, x_ref, gmat_ref, x2_ref, yn_ref):
        y = jnp.concatenate([ya_ref[...], yb_ref[...], yc_ref[...]], axis=-1)
        r = lax.rsqrt(_group_mean(y * y, gmat_ref[...]) + EPS)
        yn = (y * r * gm_ref[...]).astype(MXU_DT)
        yn_ref[...] = yn
        x2_ref[...] = x_ref[...] + _dot(yn, wo_ref[...])

    return _call(
        body, name="mix_out", grid=(S // tm,),
        in_specs=[_row_spec(tm, W_A), _row_spec(tm, W_B), _row_spec(tm, W_C), _full_spec((1, D_MODEL)),
                  _full_spec((D_MODEL, D_MODEL)), _row_spec(tm, D_MODEL), _full_spec((LANES, LANES))],
        out_specs=[_row_spec(tm, D_MODEL), _row_spec(tm, D_MODEL)],
        out_shape=[jax.ShapeDtypeStruct((S, D_MODEL), F32), jax.ShapeDtypeStruct((S, D_MODEL), MXU_DT)],
        compiler_params=_params("parallel"),
    )(ya, yb, yc, gm, wo, x, gmat)


def _mix_out_bwd(dx2, wo, ya, yb, yc, gm, gmat):
    S = dx2.shape[0]
    tm = TM

    def body(dx2_ref, wo_ref, ya_ref, yb_ref, yc_ref, gm_ref, gmat_ref, dya_ref, dyb_ref, dyc_ref, dgm_ref):
        dyn = _dot_nt(dx2_ref[...].astype(MXU_DT), wo_ref[...])
        y = jnp.concatenate([ya_ref[...], yb_ref[...], yc_ref[...]], axis=-1)
        r = lax.rsqrt(_group_mean(y * y, gmat_ref[...]) + EPS)
        yhat = y * r

        @pl.when(pl.program_id(0) == 0)
        def _():
            dgm_ref[...] = jnp.zeros_like(dgm_ref)

        dgm_ref[...] += jnp.sum(dyn * yhat, axis=0, keepdims=True)
        dyh = dyn * gm_ref[...]
        dy = r * (dyh - yhat * _group_mean(dyh * yhat, gmat_ref[...]))
        dya_ref[...] = dy[:, :W_A]
        dyb_ref[...] = dy[:, W_A:W_A + W_B]
        dyc_ref[...] = dy[:, W_A + W_B:]

    return _call(
        body, name="mix_out_bwd", grid=(S // tm,),
        in_specs=[_row_spec(tm, D_MODEL), _full_spec((D_MODEL, D_MODEL)), _row_spec(tm, W_A), _row_spec(tm, W_B),
                  _row_spec(tm, W_C), _full_spec((1, D_MODEL)), _full_spec((LANES, LANES))],
        out_specs=[_row_spec(tm, W_A), _row_spec(tm, W_B), _row_spec(tm, W_C), _full_spec((1, D_MODEL))],
        out_shape=[jax.ShapeDtypeStruct((S, W_A), F32), jax.ShapeDtypeStruct((S, W_B), F32),
                   jax.ShapeDtypeStruct((S, W_C), F32), jax.ShapeDtypeStruct((1, D_MODEL), F32)],
        compiler_params=_params("arbitrary"),
    )(dx2, wo, ya, yb, yc, gm, gmat)


_SQRT_HALF = 0.7071067811865476
_INV_SQRT_2PI = 0.3989422804014327


def _sgu_common(a, sng, wm_ref, bias, gmat):
    phi = 0.5 * (1.0 + lax.erf(a * _SQRT_HALF))
    ga = a * phi
    u = ga[:, :W_A]
    v = ga[:, W_A:]
    r = lax.rsqrt(_group_mean(v * v, gmat) + EPS)
    vhat = v * r
    vn = (vhat * sng).astype(MXU_DT)
    head = lax.broadcasted_iota(jnp.int32, (CHUNK, W_A), 1) // HEAD_DIM
    rows = []
    for c in range(a.shape[0] // CHUNK):
        vc = vn[c * CHUNK:(c + 1) * CHUNK]
        s = bias
        for h in range(4):
            s = s + jnp.where(head == h, _dot(wm_ref[h], vc), 0.0)
        rows.append(s)
    s = jnp.concatenate(rows, axis=0)
    return phi, u, r, vhat, vn, s


def _tril_weights(sgu_w_l):
    t = jnp.arange(CHUNK)
    return jnp.where((t[None, :] <= t[:, None])[None], sgu_w_l, 0.0)


def _sgu_fwd(proj, sng, wm, bias, gmat):
    S = proj.shape[0]
    tm = TM

    def body(a_ref, sng_ref, wm_ref, b_ref, gmat_ref, y_ref):
        _, u, _, _, _, s = _sgu_common(a_ref[...], sng_ref[...], wm_ref, b_ref[...], gmat_ref[...])
        y_ref[...] = u * s

    return _call(
        body, name="sgu_fwd", grid=(S // tm,),
        in_specs=[_row_spec(tm, 2 * W_A), _full_spec((1, W_A)), _full_spec((4, CHUNK, CHUNK)),
                  _full_spec((CHUNK, W_A)), _full_spec((LANES, LANES))],
        out_specs=_row_spec(tm, W_A),
        out_shape=jax.ShapeDtypeStruct((S, W_A), F32),
        compiler_params=_params("parallel"),
    )(proj, sng, wm, bias, gmat)


def _sgu_bwd(proj, dy, sng, wm, wmt, bias, gmat):
    S = proj.shape[0]
    tm = TM

    def body(a_ref, dy_ref, sng_ref, wm_ref, wmt_ref, b_ref, gmat_ref, da_ref, dw_ref, db_ref, dsng_ref):
        a = a_ref[...]
        dy = dy_ref[...]
        gmat = gmat_ref[...]
        sng = sng_ref[...]
        phi, u, r, vhat, vn, s = _sgu_common(a, sng, wm_ref, b_ref[...], gmat)
        du = dy * s
        ds = dy * u

        @pl.when(pl.program_id(0) == 0)
        def _():
            dw_ref[...] = jnp.zeros_like(dw_ref)
            db_ref[...] = jnp.zeros_like(db_ref)
            dsng_ref[...] = jnp.zeros_like(dsng_ref)

        head = lax.broadcasted_iota(jnp.int32, (CHUNK, W_A), 1) // HEAD_DIM
        tt = lax.broadcasted_iota(jnp.int32, (CHUNK, CHUNK), 0)
        ss = lax.broadcasted_iota(jnp.int32, (CHUNK, CHUNK), 1)
        rows = []
        for c in range(tm // CHUNK):
            dsc = ds[c * CHUNK:(c + 1) * CHUNK]
            vc = vn[c * CHUNK:(c + 1) * CHUNK]
            db_ref[...] += dsc
            dsb = dsc.astype(MXU_DT)
            dvn = jnp.zeros((CHUNK, W_A), F32)
            for h in range(4):
                dvn = dvn + jnp.where(head == h, _dot(wmt_ref[h], dsb), 0.0)
                dsh = jnp.where(head == h, dsc, 0.0).astype(MXU_DT)
                dw_ref[h] += jnp.where(ss <= tt, _dot_nt(dsh, vc), 0.0)
            rows.append(dvn)
        dvn = jnp.concatenate(rows, axis=0)
        dsng_ref[...] += jnp.sum(dvn * vhat, axis=0, keepdims=True)
        dvh = dvn * sng
        dv = r * (dvh - vhat * _group_mean(dvh * vhat, gmat))
        dga = jnp.concatenate([du, dv], axis=-1)
        dgelu = phi + a * (_INV_SQRT_2PI * jnp.exp(-0.5 * a * a))
        da_ref[...] = (dga * dgelu).astype(da_ref.dtype)

    return _call(
        body, name="sgu_bwd", grid=(S // tm,),
        in_specs=[_row_spec(tm, 2 * W_A), _row_spec(tm, W_A), _full_spec((1, W_A)), _full_spec((4, CHUNK, CHUNK)),
                  _full_spec((4, CHUNK, CHUNK)), _full_spec((CHUNK, W_A)), _full_spec((LANES, LANES))],
        out_specs=[_row_spec(tm, 2 * W_A), _full_spec((4, CHUNK, CHUNK)), _full_spec((CHUNK, W_A)),
                   _full_spec((1, W_A))],
        out_shape=[jax.ShapeDtypeStruct((S, 2 * W_A), MXU_DT), jax.ShapeDtypeStruct((4, CHUNK, CHUNK), F32),
                   jax.ShapeDtypeStruct((CHUNK, W_A), F32), jax.ShapeDtypeStruct((1, W_A), F32)],
        compiler_params=_params("arbitrary"),
    )(proj, dy, sng, wm, wmt, bias, gmat)


HG = 4
LW = HG * HEAD_DIM
Q_BLK0 = (2 * W_A) // LW
K_BLK0 = Q_BLK0 + W_B // LW
V_BLK0 = K_BLK0 + W_B // LW
N_GROUPS = W_B // LW
EXP_IS_ZERO_BELOW = -120.0


def _tri_matrix():
    r = jnp.arange(TK)
    return (r[:, None] > r[None, :]).astype(MXU_DT)


def _stack_heads(a):
    head = lax.broadcasted_iota(jnp.int32, a.shape, 1) // HEAD_DIM
    return jnp.concatenate([jnp.where(head == h, a, 0.0) for h in range(HG)], axis=0).astype(MXU_DT)


def _unstack_heads(a):
    head = lax.broadcasted_iota(jnp.int32, (TQ, LW), 1) // HEAD_DIM
    out = a[:TQ]
    for h in range(1, HG):
        out = jnp.where(head == h, a[h * TQ:(h + 1) * TQ], out)
    return out


def _sb_scores(q2, kj, tri, key_offset):
    z = _dot_nt(q2, kj)
    sp = jnp.log(1.0 + jnp.exp(-jnp.abs(z)))
    lsp = jnp.minimum(z, 0.0) - sp
    lsm = lsp - z
    msk = None
    if key_offset is not None:
        row = lax.broadcasted_iota(jnp.int32, z.shape, 0) & (TQ - 1)
        col = lax.broadcasted_iota(jnp.int32, z.shape, 1) + key_offset
        msk = col < row
        lsm = jnp.where(msk, lsm, 0.0)
    tail = _dot(lsm.astype(MXU_DT), tri)
    return lsp, lsm, tail, msk


def _sb_fwd(proj_b, tri):
    S = proj_b.shape[0]
    nq = S // TQ
    kpq = TQ // TK
    assert S // TK < LANES

    def body(q_ref, k_ref, v_ref, tri_ref, o_ref, rb_ref, acc_ref):
        i = pl.program_id(1)
        lane2 = lax.broadcasted_iota(jnp.int32, (HG * TQ, LANES), 1)
        q2 = _stack_heads(q_ref[...].astype(F32) * (HEAD_DIM ** -0.5))
        tri = tri_ref[...]
        rb_ref[...] = jnp.zeros_like(rb_ref)

        def block(j, run, key_offset=None, first=False):
            start = pl.multiple_of(j * TK, TK)
            kj = k_ref[pl.ds(start, TK), :]
            vj = v_ref[pl.ds(start, TK), :]
            lsp, lsm, tail, msk = _sb_scores(q2, kj, tri, key_offset)
            rb_ref[...] = jnp.where(lane2 == j, run, rb_ref[...])
            att = jnp.exp(lsp + tail + run)
            if msk is not None:
                att = jnp.where(msk, att, 0.0)
            pv = _dot(att.astype(MXU_DT), vj)
            if first:
                acc_ref[...] = pv
            else:
                acc_ref[...] += pv
            return run + tail[:, :1] + lsm[:, :1]

        run = jnp.zeros((HG * TQ, 1), F32)
        for d in reversed(range(kpq)):
            run = block(i * kpq + d, run, key_offset=d * TK, first=(d == kpq - 1))
        past = i * kpq

        def alive(run):
            return (jnp.max(run) > EXP_IS_ZERO_BELOW).astype(jnp.int32)

        def step(carry):
            n, run, _ = carry
            run = block(past - 1 - n, run)
            return n + 1, run, alive(run)

        n, _, _ = lax.while_loop(lambda c: jnp.logical_and(c[0] < past, c[2] > 0), step,
                                 (jnp.int32(0), run, alive(run)))
        rb_ref[...] = jnp.where(lane2 == LANES - 1, n.astype(F32), rb_ref[...])
        o_ref[...] = _unstack_heads(acc_ref[...])

    once = pl.Buffered(1)
    return _call(
        body, name="sb_fwd", grid=(N_GROUPS, nq),
        in_specs=[pl.BlockSpec((TQ, LW), lambda p, i: (i, Q_BLK0 + p)),
                  pl.BlockSpec((S, LW), lambda p, i: (0, K_BLK0 + p), pipeline_mode=once),
                  pl.BlockSpec((S, LW), lambda p, i: (0, V_BLK0 + p), pipeline_mode=once),
                  pl.BlockSpec((TK, TK), lambda p, i: (0, 0))],
        out_specs=[pl.BlockSpec((TQ, LW), lambda p, i: (i, p)),
                   pl.BlockSpec((None, None, HG * TQ, LANES), lambda p, i: (p, i, 0, 0))],
        out_shape=[jax.ShapeDtypeStruct((S, W_B), F32), jax.ShapeDtypeStruct((N_GROUPS, nq, HG * TQ, LANES), F32)],
        scratch_shapes=[pltpu.VMEM((HG * TQ, LW), F32)],
        compiler_params=_params("parallel", "arbitrary"),
    )(proj_b, proj_b, proj_b, tri)


def _sb_bwd(proj_b, dyb, rb, tri, trit):
    S = proj_b.shape[0]
    nq = S // TQ
    kpq = TQ // TK

    def body(q_ref, k_ref, v_ref, do_ref, rb_ref, tri_ref, trit_ref, dq_ref, dk_acc, dv_acc, dq_acc):
        i = pl.program_id(1)
        lane2 = lax.broadcasted_iota(jnp.int32, (HG * TQ, LANES), 1)
        scale = HEAD_DIM ** -0.5
        q2 = _stack_heads(q_ref[...].astype(F32) * scale)
        do2 = _stack_heads(do_ref[...])
        tri = tri_ref[...]
        trit = trit_ref[...]

        @pl.when(i == 0)
        def _():
            dk_acc[...] = jnp.zeros_like(dk_acc)
            dv_acc[...] = jnp.zeros_like(dv_acc)

        dq_acc[...] = jnp.zeros_like(dq_acc)

        def block(j, pre, key_offset=None):
            start = pl.multiple_of(j * TK, TK)
            kj = k_ref[pl.ds(start, TK), :]
            vj = v_ref[pl.ds(start, TK), :]
            lsp, lsm, tail, msk = _sb_scores(q2, kj, tri, key_offset)
            run = jnp.sum(jnp.where(lane2 == j, rb_ref[...], 0.0), axis=-1, keepdims=True)
            att = jnp.exp(lsp + tail + run)
            if msk is not None:
                att = jnp.where(msk, att, 0.0)
            beta = jnp.exp(lsp)
            dl = _dot_nt(do2, vj) * att
            cin = _dot(dl.astype(MXU_DT), trit)
            dz = dl * (1.0 - beta) - beta * (pre + cin)
            if msk is not None:
                dz = jnp.where(msk, dz, 0.0)
            dzb = dz.astype(MXU_DT)
            dq_acc[...] += _dot(dzb, kj)
            dk_acc[pl.ds(start, TK), :] += _dot_tn(dzb, q2)
            dv_acc[pl.ds(start, TK), :] += _dot_tn(att.astype(MXU_DT), do2)
            return pre + cin[:, TK - 1:] + dl[:, TK - 1:]

        past = i * kpq
        walked = jnp.max(jnp.where(lane2[:8] == LANES - 1, rb_ref[pl.ds(0, 8), :], 0.0)).astype(jnp.int32)
        walked = jnp.clip(walked, 0, past)
        pre = lax.fori_loop(past - walked, past, lambda j, pre: block(j, pre), jnp.zeros((HG * TQ, 1), F32))
        for d in range(kpq):
            pre = block(i * kpq + d, pre, key_offset=d * TK)
        dq_ref[...] = (_unstack_heads(dq_acc[...]) * scale).astype(dq_ref.dtype)

    once = pl.Buffered(1)
    return _call(
        body, name="sb_bwd", grid=(N_GROUPS, nq),
        in_specs=[pl.BlockSpec((TQ, LW), lambda p, i: (i, Q_BLK0 + p)),
                  pl.BlockSpec((S, LW), lambda p, i: (0, K_BLK0 + p), pipeline_mode=once),
                  pl.BlockSpec((S, LW), lambda p, i: (0, V_BLK0 + p), pipeline_mode=once),
                  pl.BlockSpec((TQ, LW), lambda p, i: (i, p)),
                  pl.BlockSpec((None, None, HG * TQ, LANES), lambda p, i: (p, i, 0, 0)),
                  pl.BlockSpec((TK, TK), lambda p, i: (0, 0)),
                  pl.BlockSpec((TK, TK), lambda p, i: (0, 0))],
        out_specs=[pl.BlockSpec((TQ, LW), lambda p, i: (i, p)),
                   pl.BlockSpec((S, LW), lambda p, i: (0, p), pipeline_mode=once),
                   pl.BlockSpec((S, LW), lambda p, i: (0, p), pipeline_mode=once)],
        out_shape=[jax.ShapeDtypeStruct((S, W_B), MXU_DT), jax.ShapeDtypeStruct((S, W_B), F32),
                   jax.ShapeDtypeStruct((S, W_B), F32)],
        scratch_shapes=[pltpu.VMEM((HG * TQ, LW), F32)],
        compiler_params=_params("parallel", "arbitrary"),
    )(proj_b, proj_b, proj_b, dyb, rb, tri, trit)


P_BLK = (2 * W_A + 3 * W_B) // W_C


def _window_lanes():
    g = lax.broadcasted_iota(jnp.int32, (1, W_C), 1) // (W_C // 4)
    w = jnp.where(g == 0, POOL_WINDOWS[0], jnp.where(g == 1, POOL_WINDOWS[1],
                  jnp.where(g == 2, POOL_WINDOWS[2], POOL_WINDOWS[3])))
    return g, w


def _shift_rows(ext, k, tm, lead):
    n = ext.shape[0]
    return pltpu.roll(ext, shift=k % n, axis=0)[lead:lead + tm]


def _pool_diff(p_cur, p_halo, row0, tm):
    ext = jnp.concatenate([p_halo, p_cur], axis=0)
    g, w = _window_lanes()
    acc = ext
    sums = []
    for sh in (1, 2, 4, 8):
        acc = acc + pltpu.roll(acc, shift=sh, axis=0)
        sums.append(acc[HALO:HALO + tm])
    wsum = jnp.where(g == 0, sums[0], jnp.where(g == 1, sums[1], jnp.where(g == 2, sums[2], sums[3])))
    pos = (row0 + 1 + lax.broadcasted_iota(jnp.int32, (tm, W_C), 0)).astype(F32)
    cnt = jnp.minimum(pos, w.astype(F32))
    return wsum / cnt - p_cur, cnt


def _pool_specs(tm, nrow_blocks_halo):
    cur = pl.BlockSpec((tm, W_C), lambda i: (i, P_BLK))
    prev = pl.BlockSpec((HALO, W_C), lambda i: (jnp.maximum(i * (tm // HALO) - 1, 0), P_BLK))
    return cur, prev


def _pool_fwd(proj, wbd, scale):
    S = proj.shape[0]
    tm = TM

    def body(p_ref, ph_ref, w_ref, sc_ref, y_ref):
        i = pl.program_id(0)
        halo = jnp.where(i > 0, ph_ref[...], 0.0)
        d, _ = _pool_diff(p_ref[...], halo, i * tm, tm)
        y_ref[...] = _dot(d.astype(MXU_DT), w_ref[...]) * sc_ref[...]

    cur, prev = _pool_specs(tm, S // HALO)
    return _call(
        body, name="pool_fwd", grid=(S // tm,),
        in_specs=[cur, prev, _full_spec((W_C, W_C)), _full_spec((1, W_C))],
        out_specs=_row_spec(tm, W_C),
        out_shape=jax.ShapeDtypeStruct((S, W_C), F32),
        compiler_params=_params("parallel"),
    )(proj, proj, wbd, scale)


def _pool_bwd_a(proj, dy, wbd, scale):
    S = proj.shape[0]
    tm = TM

    def body(p_ref, ph_ref, dy_ref, w_ref, sc_ref, dd_ref, e_ref, dw_ref, dsc_ref):
        i = pl.program_id(0)
        halo = jnp.where(i > 0, ph_ref[...], 0.0)
        d, cnt = _pool_diff(p_ref[...], halo, i * tm, tm)
        db = d.astype(MXU_DT)
        dy = dy_ref[...]

        @pl.when(i == 0)
        def _():
            dw_ref[...] = jnp.zeros_like(dw_ref)
            dsc_ref[...] = jnp.zeros_like(dsc_ref)

        dsc_ref[...] += jnp.sum(dy * _dot(db, w_ref[...]), axis=0, keepdims=True)
        dys = (dy * sc_ref[...]).astype(MXU_DT)
        dw_ref[...] += _dot_tn(db, dys)
        dd = _dot_nt(dys, w_ref[...])
        dd_ref[...] = dd
        e_ref[...] = dd / cnt

    cur, prev = _pool_specs(tm, S // HALO)
    return _call(
        body, name="pool_bwd_a", grid=(S // tm,),
        in_specs=[cur, prev, _row_spec(tm, W_C), _full_spec((W_C, W_C)), _full_spec((1, W_C))],
        out_specs=[_row_spec(tm, W_C), _row_spec(tm, W_C), _full_spec((W_C, W_C)), _full_spec((1, W_C))],
        out_shape=[jax.ShapeDtypeStruct((S, W_C), F32), jax.ShapeDtypeStruct((S, W_C), F32),
                   jax.ShapeDtypeStruct((W_C, W_C), F32), jax.ShapeDtypeStruct((1, W_C), F32)],
        compiler_params=_params("arbitrary"),
    )(proj, proj, dy, wbd, scale)


def _pool_bwd_b(dd, e):
    S = dd.shape[0]
    tm = TM
    nb = S // tm

    def body(dd_ref, e_ref, en_ref, dp_ref):
        i = pl.program_id(0)
        halo = jnp.where(i < nb - 1, en_ref[...], 0.0)
        ext = jnp.concatenate([e_ref[...], halo], axis=0)
        n = ext.shape[0]
        g, _ = _window_lanes()
        acc = ext
        sums = []
        for sh in (1, 2, 4, 8):
            acc = acc + pltpu.roll(acc, shift=n - sh, axis=0)
            sums.append(acc[:tm])
        wsum = jnp.where(g == 0, sums[0], jnp.where(g == 1, sums[1], jnp.where(g == 2, sums[2], sums[3])))
        dp_ref[...] = (wsum - dd_ref[...]).astype(dp_ref.dtype)

    nxt = pl.BlockSpec((HALO, W_C), lambda i: (jnp.minimum((i + 1) * (tm // HALO), S // HALO - 1), 0))
    return _call(
        body, name="pool_bwd_b", grid=(nb,),
        in_specs=[_row_spec(tm, W_C), _row_spec(tm, W_C), nxt],
        out_specs=_row_spec(tm, W_C),
        out_shape=jax.ShapeDtypeStruct((S, W_C), MXU_DT),
        compiler_params=_params("parallel"),
    )(dd, e, e)


TN_FF = 1408
NB_FF = D_FF // TN_FF
CONV_ROWS = 8


def _conv(z_cur, z_halo, cwb, tm):
    ext = jnp.concatenate([z_halo, z_cur], axis=0)
    z2 = _shift_rows(ext, 2, tm, HALO)
    z1 = _shift_rows(ext, 1, tm, HALO)
    zc = cwb[3:4] + z2 * cwb[0:1] + z1 * cwb[1:2] + z_cur * cwb[2:3]
    return zc, z2, z1


def _ffn_specs(tm, order):
    def mk(f):
        return (lambda i, j: f(i, j)) if order == "ij" else (lambda j, i: f(i, j))
    hb = tm // HALO
    return [
        pl.BlockSpec((tm, TN_FF), mk(lambda i, j: (i, j))),
        pl.BlockSpec((tm, TN_FF), mk(lambda i, j: (i, j + NB_FF))),
        pl.BlockSpec((HALO, TN_FF), mk(lambda i, j: (jnp.maximum(i * hb - 1, 0), j))),
        pl.BlockSpec((HALO, TN_FF), mk(lambda i, j: (jnp.maximum(i * hb - 1, 0), j + NB_FF))),
        pl.BlockSpec((CONV_ROWS, TN_FF), mk(lambda i, j: (0, j))),
        pl.BlockSpec((CONV_ROWS, TN_FF), mk(lambda i, j: (0, j + NB_FF))),
    ]


def _conv_gate(z, cwb):
    S = z.shape[0]
    tm = TM

    def body(zg_ref, zu_ref, hg_ref, hu_ref, cg_ref, cu_ref, f_ref):
        first = pl.program_id(0) == 0
        g, _, _ = _conv(zg_ref[...].astype(F32), jnp.where(first, 0.0, hg_ref[...].astype(F32)), cg_ref[...], tm)
        u, _, _ = _conv(zu_ref[...].astype(F32), jnp.where(first, 0.0, hu_ref[...].astype(F32)), cu_ref[...], tm)
        f_ref[...] = (g * jax.nn.sigmoid(g) * u).astype(f_ref.dtype)

    return _call(
        body, name="conv_gate", grid=(S // tm, NB_FF), in_specs=_ffn_specs(tm, "ij"),
        out_specs=pl.BlockSpec((tm, TN_FF), lambda i, j: (i, j)),
        out_shape=jax.ShapeDtypeStruct((S, D_FF), MXU_DT),
        compiler_params=_params("parallel", "parallel"),
    )(z, z, z, z, cwb, cwb)


def _conv_gate_bwd(z, df, cwb):
    S = z.shape[0]
    tm = TM
    nb = S // tm
    te = tm + HALO

    def body(zg_ref, zu_ref, hg_ref, hu_ref, cg_ref, cu_ref, ng_ref, nu_ref, df_ref, dfn_ref,
             dzg_ref, dzu_ref, dcg_ref, dcu_ref):
        i = pl.program_id(1)
        first = i == 0
        last = i == nb - 1

        def conv_ext(prev_ref, cur_ref, next_ref, c):
            ze = jnp.concatenate([jnp.where(first, 0.0, prev_ref[...].astype(F32)), cur_ref[...].astype(F32),
                                  jnp.where(last, 0.0, next_ref[...].astype(F32))], axis=0)
            z2 = _shift_rows(ze, 2, te, HALO)
            z1 = _shift_rows(ze, 1, te, HALO)
            z0 = ze[HALO:]
            return c[3:4] + z2 * c[0:1] + z1 * c[1:2] + z0 * c[2:3], z2, z1, z0

        cg = cg_ref[...]
        cu = cu_ref[...]
        g, g2, g1, g0 = conv_ext(hg_ref, zg_ref, ng_ref, cg)
        u, u2, u1, u0 = conv_ext(hu_ref, zu_ref, nu_ref, cu)
        df = jnp.concatenate([df_ref[...].astype(F32), jnp.where(last, 0.0, dfn_ref[...].astype(F32))], axis=0)
        sg = jax.nn.sigmoid(g)
        dgv = df * u * (sg * (1.0 + g * (1.0 - sg)))
        duv = df * (g * sg)

        def conv_t(d, c):
            return d[:tm] * c[2:3] + _shift_rows(d, -1, tm, 0) * c[1:2] + _shift_rows(d, -2, tm, 0) * c[0:1]

        dzg_ref[...] = conv_t(dgv, cg).astype(dzg_ref.dtype)
        dzu_ref[...] = conv_t(duv, cu).astype(dzu_ref.dtype)

        @pl.when(first)
        def _():
            dcg_ref[...] = jnp.zeros_like(dcg_ref)
            dcu_ref[...] = jnp.zeros_like(dcu_ref)

        rid = lax.broadcasted_iota(jnp.int32, (CONV_ROWS, TN_FF), 0)

        def taps(dv, s2, s1, s0):
            dv = dv[:tm]
            sums = [jnp.sum(dv * s2[:tm], axis=0, keepdims=True), jnp.sum(dv * s1[:tm], axis=0, keepdims=True),
                    jnp.sum(dv * s0[:tm], axis=0, keepdims=True), jnp.sum(dv, axis=0, keepdims=True)]
            out = jnp.zeros((CONV_ROWS, TN_FF), F32)
            for k, v in enumerate(sums):
                out = jnp.where(rid == k, v, out)
            return out

        dcg_ref[...] += taps(dgv, g2, g1, g0)
        dcu_ref[...] += taps(duv, u2, u1, u0)

    hb = tm // HALO
    acc = pl.BlockSpec((CONV_ROWS, TN_FF), lambda j, i: (0, j))
    tile = pl.BlockSpec((tm, TN_FF), lambda j, i: (i, j))

    def after(col0):
        return pl.BlockSpec((HALO, TN_FF), lambda j, i: (jnp.minimum((i + 1) * hb, S // HALO - 1), j + col0))

    return _call(
        body, name="conv_gate_bwd", grid=(NB_FF, nb),
        in_specs=_ffn_specs(tm, "ji") + [after(0), after(NB_FF), tile, after(0)],
        out_specs=[tile, tile, acc, acc],
        out_shape=[jax.ShapeDtypeStruct((S, D_FF), MXU_DT), jax.ShapeDtypeStruct((S, D_FF), MXU_DT),
                   jax.ShapeDtypeStruct((CONV_ROWS, D_FF), F32), jax.ShapeDtypeStruct((CONV_ROWS, D_FF), F32)],
        compiler_params=_params("parallel", "arbitrary"),
    )(z, z, z, z, cwb, cwb, z, z, df, df)


def _layer_consts(w, l):
    wm = _tril_weights(w["sgu_w"][l])
    eye = jnp.eye(4, dtype=F32)
    wbd = (w["pool_w"][l][:, :, None, :] * eye[:, None, :, None]).reshape(W_C, W_C)
    cwb = jnp.concatenate([w["conv_w"][l], w["conv_b"][l][None], jnp.zeros((CONV_ROWS - 4, 2 * D_FF), F32)], axis=0)
    return dict(
        g1=w["norm1_g"][l][None], g2=w["norm2_g"][l][None], gm=w["mix_norm_g"][l][None],
        sng=w["sgu_norm_g"][l][None], wm=wm.astype(MXU_DT), wmt=jnp.swapaxes(wm, 1, 2).astype(MXU_DT),
        bias=jnp.repeat(jnp.transpose(w["sgu_b"][l]), HEAD_DIM, axis=1),
        wbd=wbd.astype(MXU_DT), scale=w["pool_scale"][l][None], cwb=cwb,
        w_in=w["w_in"][l], w_o=w["w_o"][l], w_up=w["w_up"][l], w_down=w["w_down"][l],
    )


def _local_step(x, tgt, w):
    gmat = _group_matrix()
    tri = _tri_matrix()
    trit = jnp.transpose(tri)
    saved = []
    for l in range(DEPTH):
        c = _layer_consts(w, l)
        h1, proj, proj_b = _rms_mm(x, c["g1"], c["w_in"], IN_COLS // 3, "in_proj", (F32, MXU_DT))
        ya = _sgu_fwd(proj, c["sng"], c["wm"], c["bias"], gmat)
        yb, rb = _sb_fwd(proj_b, tri)
        yc = _pool_fwd(proj, c["wbd"], c["scale"])
        x2, yn = _mix_out(ya, yb, yc, c["gm"], c["w_o"], x, gmat)
        h2, z = _rms_mm(x2, c["g2"], c["w_up"], TN_FF, "up_proj", (MXU_DT,))
        f = _conv_gate(z, c["cwb"])
        x3 = _mm_res(f, c["w_down"], x2, "down_proj")
        saved.append(dict(c=c, x=x, proj=proj, proj_b=proj_b, h1=h1, ya=ya, yb=yb, yc=yc, rb=rb, x2=x2, yn=yn,
                          z=z, h2=h2, f=f))
        x = x3

    dx, d_final_g, loss8 = _loss_head(x, w["final_g"][None], tgt)
    grads = {n: [None] * DEPTH for n in ("norm1_g", "w_in", "sgu_norm_g", "sgu_w", "sgu_b", "pool_w", "pool_scale",
                                         "mix_norm_g", "w_o", "norm2_g", "w_up", "conv_w", "conv_b", "w_down")}
    for l in reversed(range(DEPTH)):
        s = saved[l]
        c = s["c"]
        df = _mm_nt(dx, c["w_down"], "down_proj_bwd")
        grads["w_down"][l] = _mm_tn(s["f"], dx, "down_proj_wgrad").reshape(N_CHIPS, D_FF // N_CHIPS, D_MODEL)
        dz_g, dz_u, dcg, dcu = _conv_gate_bwd(s["z"], df, c["cwb"])
        dcwb = jnp.concatenate([dcg, dcu], axis=1)
        grads["conv_w"][l] = dcwb[:3]
        grads["conv_b"][l] = dcwb[3]
        grads["w_up"][l] = jnp.concatenate([_mm_tn(s["h2"], dz_g, "up_proj_wgrad_gate", col_tiles=True),
                                            _mm_tn(s["h2"], dz_u, "up_proj_wgrad_value", col_tiles=True)], axis=0)
        dx2, dg2 = _mm_nt_rmsbwd([(dz_g, c["w_up"][:, :D_FF]), (dz_u, c["w_up"][:, D_FF:])],
                                 s["x2"], c["g2"], dx, "up_proj_bwd")
        grads["norm2_g"][l] = dg2[0]
        grads["w_o"][l] = _mm_tn(s["yn"], dx2, "out_proj_wgrad").reshape(N_CHIPS, D_MODEL // N_CHIPS, D_MODEL)
        dya, dyb, dyc, dgm = _mix_out_bwd(dx2, c["w_o"], s["ya"], s["yb"], s["yc"], c["gm"], gmat)
        grads["mix_norm_g"][l] = dgm[0]
        dd, e, dwbd, dscale = _pool_bwd_a(s["proj"], dyc, c["wbd"], c["scale"])
        dp = _pool_bwd_b(dd, e)
        grads["pool_w"][l] = jnp.stack([dwbd[g * 64:(g + 1) * 64, g * 64:(g + 1) * 64] for g in range(4)])
        grads["pool_scale"][l] = dscale[0]
        dq, dk, dv = _sb_bwd(s["proj_b"], dyb, s["rb"], tri, trit)
        da, dwm, dbias, dsng = _sgu_bwd(s["proj"], dya, c["sng"], c["wm"], c["wmt"], c["bias"], gmat)
        grads["sgu_w"][l] = dwm
        grads["sgu_b"][l] = jnp.transpose(jnp.sum(dbias.reshape(CHUNK, 4, HEAD_DIM), axis=-1))
        grads["sgu_norm_g"][l] = dsng[0]
        dproj = jnp.concatenate([da, dq, dk.astype(MXU_DT), dv.astype(MXU_DT), dp], axis=1)
        dw_in = _mm_tn(s["h1"], dproj, "in_proj_wgrad")
        grads["w_in"][l] = jnp.transpose(dw_in.reshape(D_MODEL, N_CHIPS, IN_COLS // N_CHIPS), (1, 0, 2))
        dx, dg1 = _mm_nt_rmsbwd([(dproj, c["w_in"])], s["x"], c["g1"], dx2, "in_proj_bwd")
        grads["norm1_g"][l] = dg1[0]

    out = {n: jnp.stack(v) for n, v in grads.items()}
    out["final_g"] = d_final_g[0]
    return loss8[0, 0], dx, out


MESH = pl.DeviceIdType.MESH
ANY = pl.BlockSpec(memory_space=pl.ANY)


def _all_gather(shards):
    n = len(shards)

    def body(*refs):
        ins, outs = refs[:n], refs[n:2 * n]
        send_sems, recv_sems = refs[2 * n:]
        x, y, c = lax.axis_index("x"), lax.axis_index("y"), lax.axis_index("c")
        sibling = (x, y, 1 - c)
        my_chip = 2 * x + y
        chips = [(1 - x, y), (x, 1 - y), (1 - x, 1 - y)]

        def copy(a, k, chip, layer, to, own=False):
            dst = outs[a].at[chip, layer]
            return pltpu.make_async_remote_copy(
                src_ref=ins[a].at[layer] if own else dst, dst_ref=dst,
                send_sem=send_sems.at[a, k], recv_sem=recv_sems.at[a, k], device_id=to, device_id_type=MESH)

        ids = [2 * px + py for px, py in chips]
        first = [copy(a, j, my_chip, c, (*chips[j], c), own=True) for j in range(3) for a in range(n)]
        for cp in first:
            cp.start()
        passed = []
        for j in range(3):
            for a in range(n):
                copy(a, j, ids[j], c, sibling).wait_recv()
                passed.append(copy(a, 3 + j, ids[j], c, sibling))
                passed[-1].start()
        for j in range(3):
            for a in range(n):
                copy(a, 3 + j, ids[j], 1 - c, sibling).wait_recv()
        for cp in first + passed:
            cp.wait_send()

    return _call(
        body, name="weight_all_gather",
        out_shape=[jax.ShapeDtypeStruct((N_CHIPS,) + s.shape, s.dtype) for s in shards],
        in_specs=[ANY] * n, out_specs=[ANY] * n,
        scratch_shapes=[pltpu.SemaphoreType.DMA((n, 6)), pltpu.SemaphoreType.DMA((n, 6))],
    )(*shards)


def _row_tile(r):
    return r if r <= 704 else 256


def _grad_swap(bigs, sp):
    n = len(bigs)

    def body(*refs):
        ins, outs = refs[:n + 1], refs[n + 1:2 * n + 2]
        send_sems, recv_sems = refs[2 * n + 2:]
        x, y, c = lax.axis_index("x"), lax.axis_index("y"), lax.axis_index("c")
        srcs = [ins[a].at[1 - c] for a in range(n)] + [ins[n].at[:, pl.ds((1 - c) * SP_HALF, SP_HALF), :]]
        copies = [pltpu.make_async_remote_copy(src_ref=srcs[a], dst_ref=outs[a], send_sem=send_sems.at[a],
                                               recv_sem=recv_sems.at[a], device_id=(x, y, 1 - c),
                                               device_id_type=MESH) for a in range(n + 1)]
        for cp in copies:
            cp.start()
        for cp in copies:
            cp.wait()

    shapes = [jax.ShapeDtypeStruct(b.shape[1:], b.dtype) for b in bigs]
    shapes.append(jax.ShapeDtypeStruct((N_CHIPS, SP_HALF, D_MODEL), sp.dtype))
    return _call(
        body, name="grad_swap_cores", out_shape=shapes, in_specs=[ANY] * (n + 1), out_specs=[ANY] * (n + 1),
        scratch_shapes=[pltpu.SemaphoreType.DMA((n + 1,)), pltpu.SemaphoreType.DMA((n + 1,))],
    )(*bigs, sp)


def _pair_add(g, r, c_arr, name, out_dtype):
    _, k, rr, cc = g.shape
    tr = _row_tile(rr)

    def body(c_ref, g_ref, r_ref, o_ref):
        o_ref[...] = (g_ref[...] + r_ref[...]).astype(o_ref.dtype)

    spec = pl.BlockSpec((1, tr, cc), lambda kk, i, c_ref: (kk, i, 0))
    grid_spec = pltpu.PrefetchScalarGridSpec(
        num_scalar_prefetch=1, grid=(k, rr // tr),
        in_specs=[pl.BlockSpec((None, 1, tr, cc), lambda kk, i, c_ref: (c_ref[0], kk, i, 0)), spec], out_specs=spec)
    return _call(body, name=name, grid_spec=grid_spec, out_shape=jax.ShapeDtypeStruct((k, rr, cc), out_dtype),
                 compiler_params=_params("parallel", "parallel"))(c_arr, g, r)


def _pair_add_small(sp, r, c_arr):
    def body(c_ref, g_ref, r_ref, o_ref):
        o_ref[...] = g_ref[...] + r_ref[...]

    spec = pl.BlockSpec((1, SP_HALF, D_MODEL), lambda kk, c_ref: (kk, 0, 0))
    grid_spec = pltpu.PrefetchScalarGridSpec(
        num_scalar_prefetch=1, grid=(N_CHIPS,),
        in_specs=[pl.BlockSpec((1, SP_HALF, D_MODEL), lambda kk, c_ref: (kk, c_ref[0], 0)), spec], out_specs=spec)
    return _call(body, name="grad_add_cores_small", grid_spec=grid_spec,
                 out_shape=jax.ShapeDtypeStruct(r.shape, F32), compiler_params=_params("parallel"))(c_arr, sp, r)


def _grad_exchange(hs):
    n = len(hs)

    def body(*refs):
        ins, outs = refs[:n], refs[n:2 * n]
        send_sems, recv_sems = refs[2 * n:]
        x, y, c = lax.axis_index("x"), lax.axis_index("y"), lax.axis_index("c")
        my_chip = 2 * x + y
        chips = [(1 - x, y), (x, 1 - y), (1 - x, 1 - y)]

        def copy(a, k, src_chip, dst_chip):
            px, py = chips[k]
            return pltpu.make_async_remote_copy(
                src_ref=ins[a].at[src_chip], dst_ref=outs[a].at[dst_chip], send_sem=send_sems.at[a, k],
                recv_sem=recv_sems.at[a, k], device_id=(px, py, c), device_id_type=MESH)

        sends = [copy(a, k, 2 * chips[k][0] + chips[k][1], my_chip) for k in range(3) for a in range(n)]
        for cp in sends:
            cp.start()
        for k in range(3):
            for a in range(n):
                copy(a, k, my_chip, 2 * chips[k][0] + chips[k][1]).wait_recv()
        for cp in sends:
            cp.wait_send()

    return _call(
        body, name="grad_exchange_chips", out_shape=[jax.ShapeDtypeStruct(h.shape, h.dtype) for h in hs],
        in_specs=[ANY] * n, out_specs=[ANY] * n,
        scratch_shapes=[pltpu.SemaphoreType.DMA((n, 3)), pltpu.SemaphoreType.DMA((n, 3))],
    )(*hs)


def _sum_chips(a, c_arr, name):
    _, r, cc = a.shape
    tr = _row_tile(r)

    def body(c_ref, a_ref, o_ref):
        o_ref[...] = ((a_ref[0].astype(F32) + a_ref[1].astype(F32)) + a_ref[2].astype(F32)) + a_ref[3].astype(F32)

    grid_spec = pltpu.PrefetchScalarGridSpec(
        num_scalar_prefetch=1, grid=(r // tr,),
        in_specs=[pl.BlockSpec((N_CHIPS, tr, cc), lambda i, c_ref: (0, i, 0))],
        out_specs=pl.BlockSpec((None, tr, cc), lambda i, c_ref: (c_ref[0], i, 0)))
    return _call(body, name=name, grid_spec=grid_spec, out_shape=jax.ShapeDtypeStruct((2, r, cc), F32),
                 compiler_params=_params("parallel"))(c_arr, a)


def _grad_share(bufs):
    n = len(bufs)

    def body(*refs):
        outs = refs[n:2 * n]
        send_sems, recv_sems = refs[2 * n:]
        x, y, c = lax.axis_index("x"), lax.axis_index("y"), lax.axis_index("c")
        copies = [pltpu.make_async_remote_copy(src_ref=outs[a].at[c], dst_ref=outs[a].at[c], send_sem=send_sems.at[a],
                                               recv_sem=recv_sems.at[a], device_id=(x, y, 1 - c),
                                               device_id_type=MESH) for a in range(n)]
        for cp in copies:
            cp.start()
        for a in range(n):
            pltpu.make_async_remote_copy(src_ref=outs[a].at[c], dst_ref=outs[a].at[1 - c], send_sem=send_sems.at[a],
                                         recv_sem=recv_sems.at[a], device_id=(x, y, 1 - c),
                                         device_id_type=MESH).wait_recv()
        for cp in copies:
            cp.wait_send()

    return _call(
        body, name="grad_share_cores", out_shape=[jax.ShapeDtypeStruct(b.shape, b.dtype) for b in bufs],
        in_specs=[ANY] * n, out_specs=[ANY] * n, input_output_aliases={a: a for a in range(n)},
        scratch_shapes=[pltpu.SemaphoreType.DMA((n,)), pltpu.SemaphoreType.DMA((n,))],
    )(*bufs)


def _adamw_math(g_ref, w_ref, m_ref, v_ref, d_ref, nm_ref, nv_ref):
    gv = g_ref[...]
    nm = ADAM_B1 * m_ref[...] + (1.0 - ADAM_B1) * gv
    nv = ADAM_B2 * v_ref[...] + (1.0 - ADAM_B2) * (gv * gv)
    m_hat = nm / (1.0 - ADAM_B1 ** ADAM_STEP)
    v_hat = nv / (1.0 - ADAM_B2 ** ADAM_STEP)
    d_ref[...] = -ADAM_LR * (m_hat / (jnp.sqrt(v_hat) + ADAM_EPS) + ADAM_WD * w_ref[...])
    nm_ref[...] = nm
    nv_ref[...] = nv


def _adamw_big(g, w, m, v, name):
    d, r, c = g.shape
    tr = r if r <= 704 else 256
    spec = pl.BlockSpec((1, tr, c), lambda l, i: (l, i, 0))

    def body(*refs):
        _adamw_math(*refs)

    shp = jax.ShapeDtypeStruct(g.shape, F32)
    return _call(body, name=name, grid=(d, r // tr), in_specs=[spec] * 4, out_specs=[spec] * 3,
                 out_shape=[shp, shp, shp], compiler_params=_params("parallel", "parallel"))(g, w, m, v)


def _adamw_small(gs, ws, ms, vs):
    n = len(gs)

    def body(*refs):
        ins, outs = refs[:4 * n], refs[4 * n:]
        for k in range(n):
            _adamw_math(ins[k], ins[n + k], ins[2 * n + k], ins[3 * n + k], outs[k], outs[n + k], outs[2 * n + k])

    shp = [jax.ShapeDtypeStruct(g.shape, F32) for g in gs]
    res = _call(body, name="adamw_small", out_shape=shp * 3)(*gs, *ws, *ms, *vs)
    return res[:n], res[n:2 * n], res[2 * n:]


def _rows(a, rows):
    flat = a.reshape(-1)
    return jnp.pad(flat, (0, rows * D_MODEL - flat.shape[0])).reshape(rows, D_MODEL)


def _small_rows(p, extra=None):
    parts = [p[n].reshape(-1) for n in SMALL_NAMES]
    if extra is not None:
        parts.append(extra.reshape(-1))
    flat = jnp.concatenate(parts)
    return jnp.pad(flat, (0, ROWS_SMALL * D_MODEL - flat.shape[0])).reshape(ROWS_SMALL, D_MODEL)


CONV_SHARD = (DEPTH, 3, 2 * D_FF // N_CHIPS)
N_CONV_SHARD = DEPTH * 3 * (2 * D_FF // N_CHIPS)


def _small_pack(g, loss):
    conv = jnp.transpose(g["conv_w"].reshape(DEPTH, 3, N_CHIPS, 2 * D_FF // N_CHIPS), (2, 0, 1, 3))
    conv = jnp.stack([_rows(conv[k], ROWS_CONV) for k in range(N_CHIPS)])
    small = jnp.broadcast_to(_small_rows(g, loss), (N_CHIPS, ROWS_SMALL, D_MODEL))
    return jnp.concatenate([conv, small], axis=1)


def _unpack_small(pack):
    out = {"conv_w": pack[:ROWS_CONV].reshape(-1)[:N_CONV_SHARD].reshape(CONV_SHARD)}
    flat = pack[ROWS_CONV:].reshape(-1)
    k = 0
    for name in SMALL_NAMES:
        shape = SMALL_SHAPES[name]
        n = 1
        for d in shape:
            n *= d
        out[name] = flat[k:k + n].reshape(shape)
        k += n
    out["extra"] = flat[k]
    return out


def _gather_weights(p):
    shards = [p[n].astype(jnp.bfloat16) for n in BIG_NAMES[:4]] + [p["conv_w"]]
    my_chip = 2 * lax.axis_index("x") + lax.axis_index("y")
    w_in, w_o, w_up, w_down, conv = [lax.dynamic_update_index_in_dim(got, own, my_chip, 0)
                                     for got, own in zip(_all_gather(shards), shards)]

    def by_cols(a):
        k, d, r, wd = a.shape
        return jnp.transpose(a, (1, 2, 0, 3)).reshape(d, r, k * wd)

    def by_rows(a):
        k, d, hgt, cc = a.shape
        return jnp.transpose(a, (1, 0, 2, 3)).reshape(d, k * hgt, cc)

    return dict(w_in=by_cols(w_in), w_o=by_rows(w_o), w_up=by_cols(w_up), w_down=by_rows(w_down), conv_w=by_cols(conv))


def _reduce_grads(grads, loss, c):
    bigs = [grads[n] for n in BIG_NAMES[:4]]
    sp = _small_pack(grads, loss)
    c_arr = jnp.reshape(c, (1,)).astype(jnp.int32)
    got = _grad_swap(bigs, sp)
    pair = [_pair_add(bigs[a], got[a], c_arr, "grad_add_cores_" + BIG_NAMES[a], ICI_DT) for a in range(4)]
    pair.append(_pair_add_small(sp, got[4], c_arr))
    my_chip = 2 * lax.axis_index("x") + lax.axis_index("y")
    parts = [lax.dynamic_update_index_in_dim(got_k, lax.dynamic_index_in_dim(own, my_chip, 0, keepdims=False),
                                             my_chip, 0) for got_k, own in zip(_grad_exchange(pair), pair)]
    total = [_sum_chips(parts[a], c_arr, "grad_sum_chips_" + (BIG_NAMES[:4] + ("small",))[a]) for a in range(5)]
    shared = _grad_share(total)
    out = dict(zip(BIG_NAMES[:4], shared[:4]))
    out.update(_unpack_small(shared[4].reshape(2 * SP_HALF, D_MODEL)))
    return out


def kernel(x, norm1_g, w_in, sgu_norm_g, sgu_w, sgu_b, pool_w, pool_scale, mix_norm_g, w_o, norm2_g, w_up, conv_w, conv_b, w_down, final_g, loss_target, m_norm1_g, m_w_in, m_sgu_norm_g, m_sgu_w, m_sgu_b, m_pool_w, m_pool_scale, m_mix_norm_g, m_w_o, m_norm2_g, m_w_up, m_conv_w, m_conv_b, m_w_down, m_final_g, v_norm1_g, v_w_in, v_sgu_norm_g, v_sgu_w, v_sgu_b, v_pool_w, v_pool_scale, v_mix_norm_g, v_w_o, v_norm2_g, v_w_up, v_conv_w, v_conv_b, v_w_down, v_final_g):
    names = ("norm1_g", "w_in", "sgu_norm_g", "sgu_w", "sgu_b", "pool_w", "pool_scale", "mix_norm_g", "w_o",
             "norm2_g", "w_up", "conv_w", "conv_b", "w_down", "final_g")
    p = dict(zip(names, (norm1_g, w_in, sgu_norm_g, sgu_w, sgu_b, pool_w, pool_scale, mix_norm_g, w_o, norm2_g,
                         w_up, conv_w, conv_b, w_down, final_g)))
    pm = dict(zip(names, (m_norm1_g, m_w_in, m_sgu_norm_g, m_sgu_w, m_sgu_b, m_pool_w, m_pool_scale, m_mix_norm_g,
                          m_w_o, m_norm2_g, m_w_up, m_conv_w, m_conv_b, m_w_down, m_final_g)))
    pv = dict(zip(names, (v_norm1_g, v_w_in, v_sgu_norm_g, v_sgu_w, v_sgu_b, v_pool_w, v_pool_scale, v_mix_norm_g,
                          v_w_o, v_norm2_g, v_w_up, v_conv_w, v_conv_b, v_w_down, v_final_g)))
    c = lax.axis_index("c")
    full = dict(p)
    full.update(_gather_weights(p))

    loss, dx, grads = _local_step(x[0], loss_target[0], full)

    g = _reduce_grads(grads, loss, c)
    d, nm, nv = {}, {}, {}
    for n in BIG_NAMES:
        d[n], nm[n], nv[n] = _adamw_big(g[n], p[n], pm[n], pv[n], "adamw_" + n)

    def two_d(a):
        return a.reshape(1, -1) if a.ndim == 1 else a

    ds, ms, vs = _adamw_small([two_d(g[n]) for n in SMALL_NAMES], [two_d(p[n]) for n in SMALL_NAMES],
                              [two_d(pm[n]) for n in SMALL_NAMES], [two_d(pv[n]) for n in SMALL_NAMES])
    for k, n in enumerate(SMALL_NAMES):
        d[n], nm[n], nv[n] = (a.reshape(p[n].shape) for a in (ds[k], ms[k], vs[k]))
    return (g["extra"], dx[None], *[g[n] for n in names], *[d[n] for n in names], *[nm[n] for n in names],
            *[nv[n] for n in names])
```

```python
import functools

import jax
import jax.numpy as jnp
from jax import lax
from jax.experimental import pallas as pl
from jax.experimental.pallas import tpu as pltpu

F32 = jnp.float32
MXU_DT = jnp.bfloat16

D_MODEL = 1024
DEPTH = 2
HEAD_DIM = 64
W_A = 256
W_B = 512
W_C = 256
IN_COLS = 2 * W_A + 3 * W_B + W_C
CHUNK = 128
POOL_WINDOWS = (2, 4, 8, 16)
D_FF = 2816
EPS = 1e-6
N_CHIPS = 4

ADAM_LR = 0.001
ADAM_B1 = 0.9
ADAM_B2 = 0.999
ADAM_EPS = 1e-08
ADAM_WD = 0.01
ADAM_STEP = 10

LANES = 128
TQ = 256
TK = 256
TM = 256
TM_MM = 512
HALO = 16
VMEM_LIMIT = 56 * 1024 * 1024

ROWS_CONV = 16
ROWS_SMALL = 240
SP_HALF = (ROWS_CONV + ROWS_SMALL) // 2
ICI_DT = jnp.bfloat16

BIG_NAMES = ("w_in", "w_o", "w_up", "w_down", "conv_w")
SMALL_NAMES = ("norm1_g", "sgu_norm_g", "sgu_w", "sgu_b", "pool_w", "pool_scale",
               "mix_norm_g", "norm2_g", "conv_b", "final_g")
SMALL_SHAPES = {
    "norm1_g": (DEPTH, D_MODEL), "sgu_norm_g": (DEPTH, W_A), "sgu_w": (DEPTH, 4, CHUNK, CHUNK),
    "sgu_b": (DEPTH, 4, CHUNK), "pool_w": (DEPTH, 4, 64, 64), "pool_scale": (DEPTH, W_C),
    "mix_norm_g": (DEPTH, D_MODEL), "norm2_g": (DEPTH, D_MODEL), "conv_b": (DEPTH, 2 * D_FF),
    "final_g": (D_MODEL,),
}


def _call(body, **kw):
    return pl.pallas_call(body, **kw)


def _params(*sem):
    return pltpu.CompilerParams(dimension_semantics=sem, vmem_limit_bytes=VMEM_LIMIT)


def _dot(a, b):
    return jnp.dot(a, b, preferred_element_type=F32)


def _dot_nt(a, b):
    return lax.dot_general(a, b, (((1,), (1,)), ((), ())), preferred_element_type=F32)


def _dot_tn(a, b):
    return lax.dot_general(a, b, (((0,), (0,)), ((), ())), preferred_element_type=F32)


def _group_mean(sq, gmat):
    sqb = sq.astype(MXU_DT)
    cols = [_dot(sqb[:, b * LANES:(b + 1) * LANES], gmat) for b in range(sq.shape[1] // LANES)]
    return cols[0] if len(cols) == 1 else jnp.concatenate(cols, axis=-1)


def _group_matrix():
    r = jnp.arange(LANES)
    return jnp.where((r[:, None] // HEAD_DIM) == (r[None, :] // HEAD_DIM), 1.0 / HEAD_DIM, 0.0).astype(MXU_DT)


def _tile(n):
    return max(t for t in range(LANES, 1536 + 1, LANES) if n % t == 0)


def _row_spec(tm, cols, col_block=0):
    return pl.BlockSpec((tm, cols), lambda i, cb=col_block: (i, cb))


def _full_spec(shape):
    nd = len(shape)
    return pl.BlockSpec(shape, lambda *_: (0,) * nd)


def _rms_mm(x, g, w, tn, name, out_dtypes):
    S, D = x.shape
    N = w.shape[1]
    tm = TM_MM

    def body(x_ref, g_ref, w_ref, h_ref, *o_refs):
        xv = x_ref[...]
        r = lax.rsqrt(jnp.mean(xv * xv, axis=-1, keepdims=True) + EPS)
        h = (xv * r * g_ref[...]).astype(h_ref.dtype)
        h_ref[...] = h
        for n0 in range(0, N, tn):
            acc = _dot(h, w_ref[:, n0:n0 + tn])
            for o_ref in o_refs:
                o_ref[:, n0:n0 + tn] = acc.astype(o_ref.dtype)

    return _call(
        body, name=name, grid=(S // tm,),
        in_specs=[_row_spec(tm, D), _full_spec((1, D)), _full_spec((D, N))],
        out_specs=[_row_spec(tm, D)] + [_row_spec(tm, N) for _ in out_dtypes],
        out_shape=[jax.ShapeDtypeStruct((S, D), MXU_DT)] + [jax.ShapeDtypeStruct((S, N), dt) for dt in out_dtypes],
        compiler_params=_params("parallel"),
    )(x, g, w)


def _mm_res(a, w, res, name):
    S, K = a.shape
    N = w.shape[1]
    tm = TM_MM

    def body(a_ref, w_ref, r_ref, o_ref):
        o_ref[...] = r_ref[...] + _dot(a_ref[...], w_ref[...])

    return _call(
        body, name=name, grid=(S // tm,),
        in_specs=[_row_spec(tm, K), _full_spec((K, N)), _row_spec(tm, N)],
        out_specs=_row_spec(tm, N),
        out_shape=jax.ShapeDtypeStruct((S, N), F32),
        compiler_params=_params("parallel"),
    )(a, w, res)


def _mm_nt(a, w, name):
    S, K = a.shape
    N = w.shape[0]
    tm = TM_MM

    def body(a_ref, w_ref, o_ref):
        o_ref[...] = _dot_nt(a_ref[...].astype(MXU_DT), w_ref[...]).astype(o_ref.dtype)

    return _call(
        body, name=name, grid=(S // tm,),
        in_specs=[_row_spec(tm, K), _full_spec((N, K))],
        out_specs=_row_spec(tm, N),
        out_shape=jax.ShapeDtypeStruct((S, N), MXU_DT),
        compiler_params=_params("parallel"),
    )(a, w)


def _mm_tn(a, b, name, col_tiles=False):
    S, K1 = a.shape
    N = b.shape[1]
    ts = TM_MM
    tk = _tile(K1)
    tn = _tile(N)
    if col_tiles:
        out_spec = pl.BlockSpec((None, tk, tn), lambda m, n, s: (n, m, 0))
        out_shape = jax.ShapeDtypeStruct((N // tn, K1, tn), F32)
    else:
        out_spec = pl.BlockSpec((tk, tn), lambda m, n, s: (m, n))
        out_shape = jax.ShapeDtypeStruct((K1, N), F32)

    def body(a_ref, b_ref, o_ref):
        @pl.when(pl.program_id(2) == 0)
        def _():
            o_ref[...] = jnp.zeros_like(o_ref)

        o_ref[...] += _dot_tn(a_ref[...], b_ref[...].astype(MXU_DT))

    return _call(
        body, name=name, grid=(K1 // tk, N // tn, S // ts),
        in_specs=[pl.BlockSpec((ts, tk), lambda m, n, s: (s, m)),
                  pl.BlockSpec((ts, tn), lambda m, n, s: (s, n))],
        out_specs=out_spec, out_shape=out_shape,
        compiler_params=_params("parallel", "parallel", "arbitrary"),
    )(a, b)


def _mm_nt_rmsbwd(pairs, x, g, dres, name):
    S, D = x.shape
    tm = TM
    n = len(pairs)

    def body(*refs):
        a_refs = refs[:n]
        w_refs = refs[n:2 * n]
        x_ref, g_ref, r_ref, dx_ref, dg_ref = refs[2 * n:]
        dh = _dot_nt(a_refs[0][...], w_refs[0][...])
        for k in range(1, n):
            dh += _dot_nt(a_refs[k][...], w_refs[k][...])
        xv = x_ref[...]
        r = lax.rsqrt(jnp.mean(xv * xv, axis=-1, keepdims=True) + EPS)
        xhat = xv * r

        @pl.when(pl.program_id(0) == 0)
        def _():
            dg_ref[...] = jnp.zeros_like(dg_ref)

        dg_ref[...] += jnp.sum(dh * xhat, axis=0, keepdims=True)
        dxh = dh * g_ref[...]
        dx_ref[...] = r_ref[...] + r * (dxh - xhat * jnp.mean(dxh * xhat, axis=-1, keepdims=True))

    in_specs = ([_row_spec(tm, a.shape[1]) for a, _ in pairs] + [_full_spec(w.shape) for _, w in pairs]
                + [_row_spec(tm, D), _full_spec((1, D)), _row_spec(tm, D)])
    return _call(
        body, name=name, grid=(S // tm,), in_specs=in_specs,
        out_specs=[_row_spec(tm, D), _full_spec((1, D))],
        out_shape=[jax.ShapeDtypeStruct((S, D), F32), jax.ShapeDtypeStruct((1, D), F32)],
        compiler_params=_params("arbitrary"),
    )(*[a for a, _ in pairs], *[w for _, w in pairs], x, g, dres)


def _loss_head(x, g, tgt):
    S, D = x.shape
    tm = TM

    def body(x_ref, g_ref, t_ref, dx_ref, dg_ref, l_ref):
        xv = x_ref[...]
        r = lax.rsqrt(jnp.mean(xv * xv, axis=-1, keepdims=True) + EPS)
        xhat = xv * r
        diff = xhat * g_ref[...] - t_ref[...]

        @pl.when(pl.program_id(0) == 0)
        def _():
            dg_ref[...] = jnp.zeros_like(dg_ref)
            l_ref[...] = jnp.zeros_like(l_ref)

        l_ref[...] += jnp.full(l_ref.shape, 0.5 * jnp.sum(jnp.mean(diff * diff, axis=-1, keepdims=True)), F32)
        dout = diff * (1.0 / D)
        dg_ref[...] += jnp.sum(dout * xhat, axis=0, keepdims=True)
        dxh = dout * g_ref[...]
        dx_ref[...] = r * (dxh - xhat * jnp.mean(dxh * xhat, axis=-1, keepdims=True))

    return _call(
        body, name="loss_head", grid=(S // tm,),
        in_specs=[_row_spec(tm, D), _full_spec((1, D)), _row_spec(tm, D)],
        out_specs=[_row_spec(tm, D), _full_spec((1, D)), _full_spec((8, LANES))],
        out_shape=[jax.ShapeDtypeStruct((S, D), F32), jax.ShapeDtypeStruct((1, D), F32),
                   jax.ShapeDtypeStruct((8, LANES), F32)],
        compiler_params=_params("arbitrary"),
    )(x, g, tgt)


def _mix_out(ya, yb, yc, gm, wo, x, gmat):
    S = x.shape[0]
    tm = TM

    def body(ya_ref, yb_ref, yc_ref, gm_ref, wo_ref, x_ref, gmat_ref, x2_ref, yn_ref):
        y = jnp.concatenate([ya_ref[...], yb_ref[...], yc_ref[...]], axis=-1)
        r = lax.rsqrt(_group_mean(y * y, gmat_ref[...]) + EPS)
        yn = (y * r * gm_ref[...]).astype(MXU_DT)
        yn_ref[...] = yn
        x2_ref[...] = x_ref[...] + _dot(yn, wo_ref[...])

    return _call(
        body, name="mix_out", grid=(S // tm,),
        in_specs=[_row_spec(tm, W_A), _row_spec(tm, W_B), _row_spec(tm, W_C), _full_spec((1, D_MODEL)),
                  _full_spec((D_MODEL, D_MODEL)), _row_spec(tm, D_MODEL), _full_spec((LANES, LANES))],
        out_specs=[_row_spec(tm, D_MODEL), _row_spec(tm, D_MODEL)],
        out_shape=[jax.ShapeDtypeStruct((S, D_MODEL), F32), jax.ShapeDtypeStruct((S, D_MODEL), MXU_DT)],
        compiler_params=_params("parallel"),
    )(ya, yb, yc, gm, wo, x, gmat)


def _mix_out_bwd(dx2, wo, ya, yb, yc, gm, gmat):
    S = dx2.shape[0]
    tm = TM

    def body(dx2_ref, wo_ref, ya_ref, yb_ref, yc_ref, gm_ref, gmat_ref, dya_ref, dyb_ref, dyc_ref, dgm_ref):
        dyn = _dot_nt(dx2_ref[...].astype(MXU_DT), wo_ref[...])
        y = jnp.concatenate([ya_ref[...], yb_ref[...], yc_ref[...]], axis=-1)
        r = lax.rsqrt(_group_mean(y * y, gmat_ref[...]) + EPS)
        yhat = y * r

        @pl.when(pl.program_id(0) == 0)
        def _():
            dgm_ref[...] = jnp.zeros_like(dgm_ref)

        dgm_ref[...] += jnp.sum(dyn * yhat, axis=0, keepdims=True)
        dyh = dyn * gm_ref[...]
        dy = r * (dyh - yhat * _group_mean(dyh * yhat, gmat_ref[...]))
        dya_ref[...] = dy[:, :W_A]
        dyb_ref[...] = dy[:, W_A:W_A + W_B]
        dyc_ref[...] = dy[:, W_A + W_B:]

    return _call(
        body, name="mix_out_bwd", grid=(S // tm,),
        in_specs=[_row_spec(tm, D_MODEL), _full_spec((D_MODEL, D_MODEL)), _row_spec(tm, W_A), _row_spec(tm, W_B),
                  _row_spec(tm, W_C), _full_spec((1, D_MODEL)), _full_spec((LANES, LANES))],
        out_specs=[_row_spec(tm, W_A), _row_spec(tm, W_B), _row_spec(tm, W_C), _full_spec((1, D_MODEL))],
        out_shape=[jax.ShapeDtypeStruct((S, W_A), F32), jax.ShapeDtypeStruct((S, W_B), F32),
                   jax.ShapeDtypeStruct((S, W_C), F32), jax.ShapeDtypeStruct((1, D_MODEL), F32)],
        compiler_params=_params("arbitrary"),
    )(dx2, wo, ya, yb, yc, gm, gmat)


_SQRT_HALF = 0.7071067811865476
_INV_SQRT_2PI = 0.3989422804014327


def _sgu_common(a, sng, wm_ref, bias, gmat):
    phi = 0.5 * (1.0 + lax.erf(a * _SQRT_HALF))
    ga = a * phi
    u = ga[:, :W_A]
    v = ga[:, W_A:]
    r = lax.rsqrt(_group_mean(v * v, gmat) + EPS)
    vhat = v * r
    vn = (vhat * sng).astype(MXU_DT)
    head = lax.broadcasted_iota(jnp.int32, (CHUNK, W_A), 1) // HEAD_DIM
    rows = []
    for c in range(a.shape[0] // CHUNK):
        vc = vn[c * CHUNK:(c + 1) * CHUNK]
        s = bias
        for h in range(4):
            s = s + jnp.where(head == h, _dot(wm_ref[h], vc), 0.0)
        rows.append(s)
    s = jnp.concatenate(rows, axis=0)
    return phi, u, r, vhat, vn, s


def _tril_weights(sgu_w_l):
    t = jnp.arange(CHUNK)
    return jnp.where((t[None, :] <= t[:, None])[None], sgu_w_l, 0.0)


def _sgu_fwd(proj, sng, wm, bias, gmat):
    S = proj.shape[0]
    tm = TM

    def body(a_ref, sng_ref, wm_ref, b_ref, gmat_ref, y_ref):
        _, u, _, _, _, s = _sgu_common(a_ref[...], sng_ref[...], wm_ref, b_ref[...], gmat_ref[...])
        y_ref[...] = u * s

    return _call(
        body, name="sgu_fwd", grid=(S // tm,),
        in_specs=[_row_spec(tm, 2 * W_A), _full_spec((1, W_A)), _full_spec((4, CHUNK, CHUNK)),
                  _full_spec((CHUNK, W_A)), _full_spec((LANES, LANES))],
        out_specs=_row_spec(tm, W_A),
        out_shape=jax.ShapeDtypeStruct((S, W_A), F32),
        compiler_params=_params("parallel"),
    )(proj, sng, wm, bias, gmat)


def _sgu_bwd(proj, dy, sng, wm, wmt, bias, gmat):
    S = proj.shape[0]
    tm = TM

    def body(a_ref, dy_ref, sng_ref, wm_ref, wmt_ref, b_ref, gmat_ref, da_ref, dw_ref, db_ref, dsng_ref):
        a = a_ref[...]
        dy = dy_ref[...]
        gmat = gmat_ref[...]
        sng = sng_ref[...]
        phi, u, r, vhat, vn, s = _sgu_common(a, sng, wm_ref, b_ref[...], gmat)
        du = dy * s
        ds = dy * u

        @pl.when(pl.program_id(0) == 0)
        def _():
            dw_ref[...] = jnp.zeros_like(dw_ref)
            db_ref[...] = jnp.zeros_like(db_ref)
            dsng_ref[...] = jnp.zeros_like(dsng_ref)

        head = lax.broadcasted_iota(jnp.int32, (CHUNK, W_A), 1) // HEAD_DIM
        tt = lax.broadcasted_iota(jnp.int32, (CHUNK, CHUNK), 0)
        ss = lax.broadcasted_iota(jnp.int32, (CHUNK, CHUNK), 1)
        rows = []
        for c in range(tm // CHUNK):
            dsc = ds[c * CHUNK:(c + 1) * CHUNK]
            vc = vn[c * CHUNK:(c + 1) * CHUNK]
            db_ref[...] += dsc
            dsb = dsc.astype(MXU_DT)
            dvn = jnp.zeros((CHUNK, W_A), F32)
            for h in range(4):
                dvn = dvn + jnp.where(head == h, _dot(wmt_ref[h], dsb), 0.0)
                dsh = jnp.where(head == h, dsc, 0.0).astype(MXU_DT)
                dw_ref[h] += jnp.where(ss <= tt, _dot_nt(dsh, vc), 0.0)
            rows.append(dvn)
        dvn = jnp.concatenate(rows, axis=0)
        dsng_ref[...] += jnp.sum(dvn * vhat, axis=0, keepdims=True)
        dvh = dvn * sng
        dv = r * (dvh - vhat * _group_mean(dvh * vhat, gmat))
        dga = jnp.concatenate([du, dv], axis=-1)
        dgelu = phi + a * (_INV_SQRT_2PI * jnp.exp(-0.5 * a * a))
        da_ref[...] = (dga * dgelu).astype(da_ref.dtype)

    return _call(
        body, name="sgu_bwd", grid=(S // tm,),
        in_specs=[_row_spec(tm, 2 * W_A), _row_spec(tm, W_A), _full_spec((1, W_A)), _full_spec((4, CHUNK, CHUNK)),
                  _full_spec((4, CHUNK, CHUNK)), _full_spec((CHUNK, W_A)), _full_spec((LANES, LANES))],
        out_specs=[_row_spec(tm, 2 * W_A), _full_spec((4, CHUNK, CHUNK)), _full_spec((CHUNK, W_A)),
                   _full_spec((1, W_A))],
        out_shape=[jax.ShapeDtypeStruct((S, 2 * W_A), MXU_DT), jax.ShapeDtypeStruct((4, CHUNK, CHUNK), F32),
                   jax.ShapeDtypeStruct((CHUNK, W_A), F32), jax.ShapeDtypeStruct((1, W_A), F32)],
        compiler_params=_params("arbitrary"),
    )(proj, dy, sng, wm, wmt, bias, gmat)


HG = 4
LW = HG * HEAD_DIM
Q_BLK0 = (2 * W_A) // LW
K_BLK0 = Q_BLK0 + W_B // LW
V_BLK0 = K_BLK0 + W_B // LW
N_GROUPS = W_B // LW
EXP_IS_ZERO_BELOW = -120.0


def _tri_matrix():
    r = jnp.arange(TK)
    return (r[:, None] > r[None, :]).astype(MXU_DT)


def _stack_heads(a):
    head = lax.broadcasted_iota(jnp.int32, a.shape, 1) // HEAD_DIM
    return jnp.concatenate([jnp.where(head == h, a, 0.0) for h in range(HG)], axis=0).astype(MXU_DT)


def _unstack_heads(a):
    head = lax.broadcasted_iota(jnp.int32, (TQ, LW), 1) // HEAD_DIM
    out = a[:TQ]
    for h in range(1, HG):
        out = jnp.where(head == h, a[h * TQ:(h + 1) * TQ], out)
    return out


def _sb_scores(q2, kj, tri, key_offset):
    z = _dot_nt(q2, kj)
    sp = jnp.log(1.0 + jnp.exp(-jnp.abs(z)))
    lsp = jnp.minimum(z, 0.0) - sp
    lsm = lsp - z
    msk = None
    if key_offset is not None:
        row = lax.broadcasted_iota(jnp.int32, z.shape, 0) & (TQ - 1)
        col = lax.broadcasted_iota(jnp.int32, z.shape, 1) + key_offset
        msk = col < row
        lsm = jnp.where(msk, lsm, 0.0)
    tail = _dot(lsm.astype(MXU_DT), tri)
    return lsp, lsm, tail, msk


def _sb_fwd(proj_b, tri):
    S = proj_b.shape[0]
    nq = S // TQ
    kpq = TQ // TK
    assert S // TK < LANES

    def body(q_ref, k_ref, v_ref, tri_ref, o_ref, rb_ref, acc_ref):
        i = pl.program_id(1)
        lane2 = lax.broadcasted_iota(jnp.int32, (HG * TQ, LANES), 1)
        q2 = _stack_heads(q_ref[...].astype(F32) * (HEAD_DIM ** -0.5))
        tri = tri_ref[...]
        rb_ref[...] = jnp.zeros_like(rb_ref)

        def block(j, run, key_offset=None, first=False):
            start = pl.multiple_of(j * TK, TK)
            kj = k_ref[pl.ds(start, TK), :]
            vj = v_ref[pl.ds(start, TK), :]
            lsp, lsm, tail, msk = _sb_scores(q2, kj, tri, key_offset)
            rb_ref[...] = jnp.where(lane2 == j, run, rb_ref[...])
            att = jnp.exp(lsp + tail + run)
            if msk is not None:
                att = jnp.where(msk, att, 0.0)
            pv = _dot(att.astype(MXU_DT), vj)
            if first:
                acc_ref[...] = pv
            else:
                acc_ref[...] += pv
            return run + tail[:, :1] + lsm[:, :1]

        run = jnp.zeros((HG * TQ, 1), F32)
        for d in reversed(range(kpq)):
            run = block(i * kpq + d, run, key_offset=d * TK, first=(d == kpq - 1))
        past = i * kpq

        def alive(run):
            return (jnp.max(run) > EXP_IS_ZERO_BELOW).astype(jnp.int32)

        def step(carry):
            n, run, _ = carry
            run = block(past - 1 - n, run)
            return n + 1, run, alive(run)

        n, _, _ = lax.while_loop(lambda c: jnp.logical_and(c[0] < past, c[2] > 0), step,
                                 (jnp.int32(0), run, alive(run)))
        rb_ref[...] = jnp.where(lane2 == LANES - 1, n.astype(F32), rb_ref[...])
        o_ref[...] = _unstack_heads(acc_ref[...])

    once = pl.Buffered(1)
    return _call(
        body, name="sb_fwd", grid=(N_GROUPS, nq),
        in_specs=[pl.BlockSpec((TQ, LW), lambda p, i: (i, Q_BLK0 + p)),
                  pl.BlockSpec((S, LW), lambda p, i: (0, K_BLK0 + p), pipeline_mode=once),
                  pl.BlockSpec((S, LW), lambda p, i: (0, V_BLK0 + p), pipeline_mode=once),
                  pl.BlockSpec((TK, TK), lambda p, i: (0, 0))],
        out_specs=[pl.BlockSpec((TQ, LW), lambda p, i: (i, p)),
                   pl.BlockSpec((None, None, HG * TQ, LANES), lambda p, i: (p, i, 0, 0))],
        out_shape=[jax.ShapeDtypeStruct((S, W_B), F32), jax.ShapeDtypeStruct((N_GROUPS, nq, HG * TQ, LANES), F32)],
        scratch_shapes=[pltpu.VMEM((HG * TQ, LW), F32)],
        compiler_params=_params("parallel", "arbitrary"),
    )(proj_b, proj_b, proj_b, tri)


def _sb_bwd(proj_b, dyb, rb, tri, trit):
    S = proj_b.shape[0]
    nq = S // TQ
    kpq = TQ // TK

    def body(q_ref, k_ref, v_ref, do_ref, rb_ref, tri_ref, trit_ref, dq_ref, dk_acc, dv_acc, dq_acc):
        i = pl.program_id(1)
        lane2 = lax.broadcasted_iota(jnp.int32, (HG * TQ, LANES), 1)
        scale = HEAD_DIM ** -0.5
        q2 = _stack_heads(q_ref[...].astype(F32) * scale)
        do2 = _stack_heads(do_ref[...])
        tri = tri_ref[...]
        trit = trit_ref[...]

        @pl.when(i == 0)
        def _():
            dk_acc[...] = jnp.zeros_like(dk_acc)
            dv_acc[...] = jnp.zeros_like(dv_acc)

        dq_acc[...] = jnp.zeros_like(dq_acc)

        def block(j, pre, key_offset=None):
            start = pl.multiple_of(j * TK, TK)
            kj = k_ref[pl.ds(start, TK), :]
            vj = v_ref[pl.ds(start, TK), :]
            lsp, lsm, tail, msk = _sb_scores(q2, kj, tri, key_offset)
            run = jnp.sum(jnp.where(lane2 == j, rb_ref[...], 0.0), axis=-1, keepdims=True)
            att = jnp.exp(lsp + tail + run)
            if msk is not None:
                att = jnp.where(msk, att, 0.0)
            beta = jnp.exp(lsp)
            dl = _dot_nt(do2, vj) * att
            cin = _dot(dl.astype(MXU_DT), trit)
            dz = dl * (1.0 - beta) - beta * (pre + cin)
            if msk is not None:
                dz = jnp.where(msk, dz, 0.0)
            dzb = dz.astype(MXU_DT)
            dq_acc[...] += _dot(dzb, kj)
            dk_acc[pl.ds(start, TK), :] += _dot_tn(dzb, q2)
            dv_acc[pl.ds(start, TK), :] += _dot_tn(att.astype(MXU_DT), do2)
            return pre + cin[:, TK - 1:] + dl[:, TK - 1:]

        past = i * kpq
        walked = jnp.max(jnp.where(lane2[:8] == LANES - 1, rb_ref[pl.ds(0, 8), :], 0.0)).astype(jnp.int32)
        walked = jnp.clip(walked, 0, past)
        pre = lax.fori_loop(past - walked, past, lambda j, pre: block(j, pre), jnp.zeros((HG * TQ, 1), F32))
        for d in range(kpq):
            pre = block(i * kpq + d, pre, key_offset=d * TK)
        dq_ref[...] = (_unstack_heads(dq_acc[...]) * scale).astype(dq_ref.dtype)

    once = pl.Buffered(1)
    return _call(
        body, name="sb_bwd", grid=(N_GROUPS, nq),
        in_specs=[pl.BlockSpec((TQ, LW), lambda p, i: (i, Q_BLK0 + p)),
                  pl.BlockSpec((S, LW), lambda p, i: (0, K_BLK0 + p), pipeline_mode=once),
                  pl.BlockSpec((S, LW), lambda p, i: (0, V_BLK0 + p), pipeline_mode=once),
                  pl.BlockSpec((TQ, LW), lambda p, i: (i, p)),
                  pl.BlockSpec((None, None, HG * TQ, LANES), lambda p, i: (p, i, 0, 0)),
                  pl.BlockSpec((TK, TK), lambda p, i: (0, 0)),
                  pl.BlockSpec((TK, TK), lambda p, i: (0, 0))],
        out_specs=[pl.BlockSpec((TQ, LW), lambda p, i: (i, p)),
                   pl.BlockSpec((S, LW), lambda p, i: (0, p), pipeline_mode=once),
                   pl.BlockSpec((S, LW), lambda p, i: (0, p), pipeline_mode=once)],
        out_shape=[jax.ShapeDtypeStruct((S, W_B), MXU_DT), jax.ShapeDtypeStruct((S, W_B), F32),
                   jax.ShapeDtypeStruct((S, W_B), F32)],
        scratch_shapes=[pltpu.VMEM((HG * TQ, LW), F32)],
        compiler_params=_params("parallel", "arbitrary"),
    )(proj_b, proj_b, proj_b, dyb, rb, tri, trit)


P_BLK = (2 * W_A + 3 * W_B) // W_C


def _window_lanes():
    g = lax.broadcasted_iota(jnp.int32, (1, W_C), 1) // (W_C // 4)
    w = jnp.where(g == 0, POOL_WINDOWS[0], jnp.where(g == 1, POOL_WINDOWS[1],
                  jnp.where(g == 2, POOL_WINDOWS[2], POOL_WINDOWS[3])))
    return g, w


def _shift_rows(ext, k, tm, lead):
    n = ext.shape[0]
    return pltpu.roll(ext, shift=k % n, axis=0)[lead:lead + tm]


def _pool_diff(p_cur, p_halo, row0, tm):
    ext = jnp.concatenate([p_halo, p_cur], axis=0)
    g, w = _window_lanes()
    acc = ext
    sums = []
    for sh in (1, 2, 4, 8):
        acc = acc + pltpu.roll(acc, shift=sh, axis=0)
        sums.append(acc[HALO:HALO + tm])
    wsum = jnp.where(g == 0, sums[0], jnp.where(g == 1, sums[1], jnp.where(g == 2, sums[2], sums[3])))
    pos = (row0 + 1 + lax.broadcasted_iota(jnp.int32, (tm, W_C), 0)).astype(F32)
    cnt = jnp.minimum(pos, w.astype(F32))
    return wsum / cnt - p_cur, cnt


def _pool_specs(tm, nrow_blocks_halo):
    cur = pl.BlockSpec((tm, W_C), lambda i: (i, P_BLK))
    prev = pl.BlockSpec((HALO, W_C), lambda i: (jnp.maximum(i * (tm // HALO) - 1, 0), P_BLK))
    return cur, prev


def _pool_fwd(proj, wbd, scale):
    S = proj.shape[0]
    tm = TM

    def body(p_ref, ph_ref, w_ref, sc_ref, y_ref):
        i = pl.program_id(0)
        halo = jnp.where(i > 0, ph_ref[...], 0.0)
        d, _ = _pool_diff(p_ref[...], halo, i * tm, tm)
        y_ref[...] = _dot(d.astype(MXU_DT), w_ref[...]) * sc_ref[...]

    cur, prev = _pool_specs(tm, S // HALO)
    return _call(
        body, name="pool_fwd", grid=(S // tm,),
        in_specs=[cur, prev, _full_spec((W_C, W_C)), _full_spec((1, W_C))],
        out_specs=_row_spec(tm, W_C),
        out_shape=jax.ShapeDtypeStruct((S, W_C), F32),
        compiler_params=_params("parallel"),
    )(proj, proj, wbd, scale)


def _pool_bwd_a(proj, dy, wbd, scale):
    S = proj.shape[0]
    tm = TM

    def body(p_ref, ph_ref, dy_ref, w_ref, sc_ref, dd_ref, e_ref, dw_ref, dsc_ref):
        i = pl.program_id(0)
        halo = jnp.where(i > 0, ph_ref[...], 0.0)
        d, cnt = _pool_diff(p_ref[...], halo, i * tm, tm)
        db = d.astype(MXU_DT)
        dy = dy_ref[...]

        @pl.when(i == 0)
        def _():
            dw_ref[...] = jnp.zeros_like(dw_ref)
            dsc_ref[...] = jnp.zeros_like(dsc_ref)

        dsc_ref[...] += jnp.sum(dy * _dot(db, w_ref[...]), axis=0, keepdims=True)
        dys = (dy * sc_ref[...]).astype(MXU_DT)
        dw_ref[...] += _dot_tn(db, dys)
        dd = _dot_nt(dys, w_ref[...])
        dd_ref[...] = dd
        e_ref[...] = dd / cnt

    cur, prev = _pool_specs(tm, S // HALO)
    return _call(
        body, name="pool_bwd_a", grid=(S // tm,),
        in_specs=[cur, prev, _row_spec(tm, W_C), _full_spec((W_C, W_C)), _full_spec((1, W_C))],
        out_specs=[_row_spec(tm, W_C), _row_spec(tm, W_C), _full_spec((W_C, W_C)), _full_spec((1, W_C))],
        out_shape=[jax.ShapeDtypeStruct((S, W_C), F32), jax.ShapeDtypeStruct((S, W_C), F32),
                   jax.ShapeDtypeStruct((W_C, W_C), F32), jax.ShapeDtypeStruct((1, W_C), F32)],
        compiler_params=_params("arbitrary"),
    )(proj, proj, dy, wbd, scale)


def _pool_bwd_b(dd, e):
    S = dd.shape[0]
    tm = TM
    nb = S // tm

    def body(dd_ref, e_ref, en_ref, dp_ref):
        i = pl.program_id(0)
        halo = jnp.where(i < nb - 1, en_ref[...], 0.0)
        ext = jnp.concatenate([e_ref[...], halo], axis=0)
        n = ext.shape[0]
        g, _ = _window_lanes()
        acc = ext
        sums = []
        for sh in (1, 2, 4, 8):
            acc = acc + pltpu.roll(acc, shift=n - sh, axis=0)
            sums.append(acc[:tm])
        wsum = jnp.where(g == 0, sums[0], jnp.where(g == 1, sums[1], jnp.where(g == 2, sums[2], sums[3])))
        dp_ref[...] = (wsum - dd_ref[...]).astype(dp_ref.dtype)

    nxt = pl.BlockSpec((HALO, W_C), lambda i: (jnp.minimum((i + 1) * (tm // HALO), S // HALO - 1), 0))
    return _call(
        body, name="pool_bwd_b", grid=(nb,),
        in_specs=[_row_spec(tm, W_C), _row_spec(tm, W_C), nxt],
        out_specs=_row_spec(tm, W_C),
        out_shape=jax.ShapeDtypeStruct((S, W_C), MXU_DT),
        compiler_params=_params("parallel"),
    )(dd, e, e)


TN_FF = 1408
NB_FF = D_FF // TN_FF
CONV_ROWS = 8


def _conv(z_cur, z_halo, cwb, tm):
    ext = jnp.concatenate([z_halo, z_cur], axis=0)
    z2 = _shift_rows(ext, 2, tm, HALO)
    z1 = _shift_rows(ext, 1, tm, HALO)
    zc = cwb[3:4] + z2 * cwb[0:1] + z1 * cwb[1:2] + z_cur * cwb[2:3]
    return zc, z2, z1


def _up_proj_gate(x, g, w, cwb):
    S, D = x.shape
    tm = TM

    def body(x_ref, g_ref, w_ref, c_ref, h_ref, z_ref, f_ref, tail_ref):
        first = pl.program_id(0) == 0
        xv = x_ref[...]
        r = lax.rsqrt(jnp.mean(xv * xv, axis=-1, keepdims=True) + EPS)
        h = (xv * r * g_ref[...]).astype(h_ref.dtype)
        h_ref[...] = h
        for j in range(NB_FF):
            halves = []
            for col0 in (j * TN_FF, D_FF + j * TN_FF):
                zb = _dot(h, w_ref[:, col0:col0 + TN_FF]).astype(z_ref.dtype)
                z_ref[:, col0:col0 + TN_FF] = zb
                zf = zb.astype(F32)
                prev = jnp.where(first, 0.0, tail_ref[:, col0:col0 + TN_FF])
                tail_ref[:, col0:col0 + TN_FF] = zf[tm - HALO:]
                halves.append(_conv(zf, prev, c_ref[:, col0:col0 + TN_FF], tm)[0])
            gate, value = halves
            f_ref[:, j * TN_FF:(j + 1) * TN_FF] = (gate * jax.nn.sigmoid(gate) * value).astype(f_ref.dtype)

    return _call(
        body, name="up_proj_gate", grid=(S // tm,),
        in_specs=[_row_spec(tm, D), _full_spec((1, D)),
                  pl.BlockSpec((D, 2 * D_FF), lambda i: (0, 0), pipeline_mode=pl.Buffered(1)),
                  _full_spec((CONV_ROWS, 2 * D_FF))],
        out_specs=[_row_spec(tm, D), _row_spec(tm, 2 * D_FF), _row_spec(tm, D_FF)],
        out_shape=[jax.ShapeDtypeStruct((S, D), MXU_DT), jax.ShapeDtypeStruct((S, 2 * D_FF), MXU_DT),
                   jax.ShapeDtypeStruct((S, D_FF), MXU_DT)],
        scratch_shapes=[pltpu.VMEM((HALO, 2 * D_FF), F32)],
        compiler_params=_params("arbitrary"),
    )(x, g, w, cwb)


def _gate_up_bwd(z, df, cwb, w, x, g, dres):
    S, D = x.shape
    tm = TM
    nb = S // tm
    te = tm + HALO

    def body(z_ref, zp_ref, zn_ref, df_ref, dfn_ref, c_ref, w_ref, x_ref, g_ref, r_ref,
             dz_ref, dc_ref, dx_ref, dg_ref):
        i = pl.program_id(0)
        first = i == 0
        last = i == nb - 1

        @pl.when(first)
        def _():
            dc_ref[...] = jnp.zeros_like(dc_ref)
            dg_ref[...] = jnp.zeros_like(dg_ref)

        rid = lax.broadcasted_iota(jnp.int32, (CONV_ROWS, TN_FF), 0)

        def conv_ext(cols, c):
            ze = jnp.concatenate([jnp.where(first, 0.0, zp_ref[:, cols].astype(F32)), z_ref[:, cols].astype(F32),
                                  jnp.where(last, 0.0, zn_ref[:, cols].astype(F32))], axis=0)
            z2 = _shift_rows(ze, 2, te, HALO)
            z1 = _shift_rows(ze, 1, te, HALO)
            z0 = ze[HALO:]
            return c[3:4] + z2 * c[0:1] + z1 * c[1:2] + z0 * c[2:3], z2, z1, z0

        def conv_t(d, c):
            return d[:tm] * c[2:3] + _shift_rows(d, -1, tm, 0) * c[1:2] + _shift_rows(d, -2, tm, 0) * c[0:1]

        def taps(dv, s2, s1, s0):
            dv = dv[:tm]
            sums = [jnp.sum(dv * s2[:tm], axis=0, keepdims=True), jnp.sum(dv * s1[:tm], axis=0, keepdims=True),
                    jnp.sum(dv * s0[:tm], axis=0, keepdims=True), jnp.sum(dv, axis=0, keepdims=True)]
            out = jnp.zeros((CONV_ROWS, TN_FF), F32)
            for k, v in enumerate(sums):
                out = jnp.where(rid == k, v, out)
            return out

        dh = jnp.zeros((tm, D), F32)
        for j in range(NB_FF):
            gc = slice(j * TN_FF, (j + 1) * TN_FF)
            uc = slice(D_FF + j * TN_FF, D_FF + (j + 1) * TN_FF)
            cg = c_ref[:, gc]
            cu = c_ref[:, uc]
            gt, g2, g1, g0 = conv_ext(gc, cg)
            ut, u2, u1, u0 = conv_ext(uc, cu)
            df = jnp.concatenate([df_ref[:, gc].astype(F32), jnp.where(last, 0.0, dfn_ref[:, gc].astype(F32))], axis=0)
            sg = jax.nn.sigmoid(gt)
            dgv = df * ut * (sg * (1.0 + gt * (1.0 - sg)))
            duv = df * (gt * sg)
            dzg = conv_t(dgv, cg).astype(dz_ref.dtype)
            dzu = conv_t(duv, cu).astype(dz_ref.dtype)
            dz_ref[:, gc] = dzg
            dz_ref[:, uc] = dzu
            dc_ref[:, gc] += taps(dgv, g2, g1, g0)
            dc_ref[:, uc] += taps(duv, u2, u1, u0)
            dh += _dot_nt(dzg, w_ref[:, gc]) + _dot_nt(dzu, w_ref[:, uc])

        xv = x_ref[...]
        r = lax.rsqrt(jnp.mean(xv * xv, axis=-1, keepdims=True) + EPS)
        xhat = xv * r
        dg_ref[...] += jnp.sum(dh * xhat, axis=0, keepdims=True)
        dxh = dh * g_ref[...]
        dx_ref[...] = r_ref[...] + r * (dxh - xhat * jnp.mean(dxh * xhat, axis=-1, keepdims=True))

    hb = tm // HALO
    last_halo = S // HALO - 1
    return _call(
        body, name="gate_up_bwd", grid=(nb,),
        in_specs=[_row_spec(tm, 2 * D_FF),
                  pl.BlockSpec((HALO, 2 * D_FF), lambda i: (jnp.maximum(i * hb - 1, 0), 0)),
                  pl.BlockSpec((HALO, 2 * D_FF), lambda i: (jnp.minimum((i + 1) * hb, last_halo), 0)),
                  _row_spec(tm, D_FF),
                  pl.BlockSpec((HALO, D_FF), lambda i: (jnp.minimum((i + 1) * hb, last_halo), 0)),
                  _full_spec((CONV_ROWS, 2 * D_FF)),
                  pl.BlockSpec((D, 2 * D_FF), lambda i: (0, 0), pipeline_mode=pl.Buffered(1)),
                  _row_spec(tm, D), _full_spec((1, D)), _row_spec(tm, D)],
        out_specs=[_row_spec(tm, 2 * D_FF), _full_spec((CONV_ROWS, 2 * D_FF)), _row_spec(tm, D), _full_spec((1, D))],
        out_shape=[jax.ShapeDtypeStruct((S, 2 * D_FF), MXU_DT), jax.ShapeDtypeStruct((CONV_ROWS, 2 * D_FF), F32),
                   jax.ShapeDtypeStruct((S, D), F32), jax.ShapeDtypeStruct((1, D), F32)],
        compiler_params=_params("arbitrary"),
    )(z, z, z, df, df, cwb, w, x, g, dres)


def _layer_consts(w, l):
    wm = _tril_weights(w["sgu_w"][l])
    eye = jnp.eye(4, dtype=F32)
    wbd = (w["pool_w"][l][:, :, None, :] * eye[:, None, :, None]).reshape(W_C, W_C)
    cwb = jnp.concatenate([w["conv_w"][l], w["conv_b"][l][None], jnp.zeros((CONV_ROWS - 4, 2 * D_FF), F32)], axis=0)
    return dict(
        g1=w["norm1_g"][l][None], g2=w["norm2_g"][l][None], gm=w["mix_norm_g"][l][None],
        sng=w["sgu_norm_g"][l][None], wm=wm.astype(MXU_DT), wmt=jnp.swapaxes(wm, 1, 2).astype(MXU_DT),
        bias=jnp.repeat(jnp.transpose(w["sgu_b"][l]), HEAD_DIM, axis=1),
        wbd=wbd.astype(MXU_DT), scale=w["pool_scale"][l][None], cwb=cwb,
        w_in=w["w_in"][l], w_o=w["w_o"][l], w_up=w["w_up"][l], w_down=w["w_down"][l],
    )


def _local_step(x, tgt, w):
    gmat = _group_matrix()
    tri = _tri_matrix()
    trit = jnp.transpose(tri)
    saved = []
    for l in range(DEPTH):
        c = _layer_consts(w, l)
        h1, proj, proj_b = _rms_mm(x, c["g1"], c["w_in"], IN_COLS // 3, "in_proj", (F32, MXU_DT))
        ya = _sgu_fwd(proj, c["sng"], c["wm"], c["bias"], gmat)
        yb, rb = _sb_fwd(proj_b, tri)
        yc = _pool_fwd(proj, c["wbd"], c["scale"])
        x2, yn = _mix_out(ya, yb, yc, c["gm"], c["w_o"], x, gmat)
        h2, z, f = _up_proj_gate(x2, c["g2"], c["w_up"], c["cwb"])
        x3 = _mm_res(f, c["w_down"], x2, "down_proj")
        saved.append(dict(c=c, x=x, proj=proj, proj_b=proj_b, h1=h1, ya=ya, yb=yb, yc=yc, rb=rb, x2=x2, yn=yn,
                          z=z, h2=h2, f=f))
        x = x3

    dx, d_final_g, loss8 = _loss_head(x, w["final_g"][None], tgt)
    grads = {n: [None] * DEPTH for n in ("norm1_g", "w_in", "sgu_norm_g", "sgu_w", "sgu_b", "pool_w", "pool_scale",
                                         "mix_norm_g", "w_o", "norm2_g", "w_up", "conv_w", "conv_b", "w_down")}
    for l in reversed(range(DEPTH)):
        s = saved[l]
        c = s["c"]
        df = _mm_nt(dx, c["w_down"], "down_proj_bwd")
        grads["w_down"][l] = _mm_tn(s["f"], dx, "down_proj_wgrad").reshape(N_CHIPS, D_FF // N_CHIPS, D_MODEL)
        dz, dcwb, dx2, dg2 = _gate_up_bwd(s["z"], df, c["cwb"], c["w_up"], s["x2"], c["g2"], dx)
        grads["conv_w"][l] = dcwb[:3]
        grads["conv_b"][l] = dcwb[3]
        grads["w_up"][l] = _mm_tn(s["h2"], dz, "up_proj_wgrad", col_tiles=True)
        grads["norm2_g"][l] = dg2[0]
        grads["w_o"][l] = _mm_tn(s["yn"], dx2, "out_proj_wgrad").reshape(N_CHIPS, D_MODEL // N_CHIPS, D_MODEL)
        dya, dyb, dyc, dgm = _mix_out_bwd(dx2, c["w_o"], s["ya"], s["yb"], s["yc"], c["gm"], gmat)
        grads["mix_norm_g"][l] = dgm[0]
        dd, e, dwbd, dscale = _pool_bwd_a(s["proj"], dyc, c["wbd"], c["scale"])
        dp = _pool_bwd_b(dd, e)
        grads["pool_w"][l] = jnp.stack([dwbd[g * 64:(g + 1) * 64, g * 64:(g + 1) * 64] for g in range(4)])
        grads["pool_scale"][l] = dscale[0]
        dq, dk, dv = _sb_bwd(s["proj_b"], dyb, s["rb"], tri, trit)
        da, dwm, dbias, dsng = _sgu_bwd(s["proj"], dya, c["sng"], c["wm"], c["wmt"], c["bias"], gmat)
        grads["sgu_w"][l] = dwm
        grads["sgu_b"][l] = jnp.transpose(jnp.sum(dbias.reshape(CHUNK, 4, HEAD_DIM), axis=-1))
        grads["sgu_norm_g"][l] = dsng[0]
        dproj = jnp.concatenate([da, dq, dk.astype(MXU_DT), dv.astype(MXU_DT), dp], axis=1)
        dw_in = _mm_tn(s["h1"], dproj, "in_proj_wgrad")
        grads["w_in"][l] = jnp.transpose(dw_in.reshape(D_MODEL, N_CHIPS, IN_COLS // N_CHIPS), (1, 0, 2))
        dx, dg1 = _mm_nt_rmsbwd([(dproj, c["w_in"])], s["x"], c["g1"], dx2, "in_proj_bwd")
        grads["norm1_g"][l] = dg1[0]

    out = {n: jnp.stack(v) for n, v in grads.items()}
    out["final_g"] = d_final_g[0]
    return loss8[0, 0], dx, out


MESH = pl.DeviceIdType.MESH
ANY = pl.BlockSpec(memory_space=pl.ANY)


def _all_gather(shards):
    n = len(shards)

    def body(*refs):
        ins, outs = refs[:n], refs[n:2 * n]
        send_sems, recv_sems = refs[2 * n:]
        x, y, c = lax.axis_index("x"), lax.axis_index("y"), lax.axis_index("c")
        sibling = (x, y, 1 - c)
        my_chip = 2 * x + y
        chips = [(1 - x, y), (x, 1 - y), (1 - x, 1 - y)]

        def copy(a, k, chip, layer, to, own=False):
            dst = outs[a].at[chip, layer]
            return pltpu.make_async_remote_copy(
                src_ref=ins[a].at[layer] if own else dst, dst_ref=dst,
                send_sem=send_sems.at[a, k], recv_sem=recv_sems.at[a, k], device_id=to, device_id_type=MESH)

        ids = [2 * px + py for px, py in chips]
        first = [copy(a, j, my_chip, c, (*chips[j], c), own=True) for j in range(3) for a in range(n)]
        for cp in first:
            cp.start()
        passed = []
        for j in range(3):
            for a in range(n):
                copy(a, j, ids[j], c, sibling).wait_recv()
                passed.append(copy(a, 3 + j, ids[j], c, sibling))
                passed[-1].start()
        for j in range(3):
            for a in range(n):
                copy(a, 3 + j, ids[j], 1 - c, sibling).wait_recv()
        for cp in first + passed:
            cp.wait_send()

    return _call(
        body, name="weight_all_gather",
        out_shape=[jax.ShapeDtypeStruct((N_CHIPS,) + s.shape, s.dtype) for s in shards],
        in_specs=[ANY] * n, out_specs=[ANY] * n,
        scratch_shapes=[pltpu.SemaphoreType.DMA((n, 6)), pltpu.SemaphoreType.DMA((n, 6))],
    )(*shards)


def _row_tile(r):
    return r if r <= 704 else 256


def _grad_swap(bigs, sp):
    n = len(bigs)

    def body(*refs):
        ins, outs = refs[:n + 1], refs[n + 1:2 * n + 2]
        send_sems, recv_sems = refs[2 * n + 2:]
        x, y, c = lax.axis_index("x"), lax.axis_index("y"), lax.axis_index("c")
        srcs = [ins[a].at[1 - c] for a in range(n)] + [ins[n].at[:, pl.ds((1 - c) * SP_HALF, SP_HALF), :]]
        copies = [pltpu.make_async_remote_copy(src_ref=srcs[a], dst_ref=outs[a], send_sem=send_sems.at[a],
                                               recv_sem=recv_sems.at[a], device_id=(x, y, 1 - c),
                                               device_id_type=MESH) for a in range(n + 1)]
        for cp in copies:
            cp.start()
        for cp in copies:
            cp.wait()

    shapes = [jax.ShapeDtypeStruct(b.shape[1:], b.dtype) for b in bigs]
    shapes.append(jax.ShapeDtypeStruct((N_CHIPS, SP_HALF, D_MODEL), sp.dtype))
    return _call(
        body, name="grad_swap_cores", out_shape=shapes, in_specs=[ANY] * (n + 1), out_specs=[ANY] * (n + 1),
        scratch_shapes=[pltpu.SemaphoreType.DMA((n + 1,)), pltpu.SemaphoreType.DMA((n + 1,))],
    )(*bigs, sp)


def _pair_add(g, r, c_arr, name, out_dtype):
    _, k, rr, cc = g.shape
    tr = _row_tile(rr)

    def body(c_ref, g_ref, r_ref, o_ref):
        o_ref[...] = (g_ref[...] + r_ref[...]).astype(o_ref.dtype)

    spec = pl.BlockSpec((1, tr, cc), lambda kk, i, c_ref: (kk, i, 0))
    grid_spec = pltpu.PrefetchScalarGridSpec(
        num_scalar_prefetch=1, grid=(k, rr // tr),
        in_specs=[pl.BlockSpec((None, 1, tr, cc), lambda kk, i, c_ref: (c_ref[0], kk, i, 0)), spec], out_specs=spec)
    return _call(body, name=name, grid_spec=grid_spec, out_shape=jax.ShapeDtypeStruct((k, rr, cc), out_dtype),
                 compiler_params=_params("parallel", "parallel"))(c_arr, g, r)


def _pair_add_small(sp, r, c_arr):
    def body(c_ref, g_ref, r_ref, o_ref):
        o_ref[...] = g_ref[...] + r_ref[...]

    spec = pl.BlockSpec((1, SP_HALF, D_MODEL), lambda kk, c_ref: (kk, 0, 0))
    grid_spec = pltpu.PrefetchScalarGridSpec(
        num_scalar_prefetch=1, grid=(N_CHIPS,),
        in_specs=[pl.BlockSpec((1, SP_HALF, D_MODEL), lambda kk, c_ref: (kk, c_ref[0], 0)), spec], out_specs=spec)
    return _call(body, name="grad_add_cores_small", grid_spec=grid_spec,
                 out_shape=jax.ShapeDtypeStruct(r.shape, F32), compiler_params=_params("parallel"))(c_arr, sp, r)


def _grad_exchange(hs):
    n = len(hs)

    def body(*refs):
        ins, outs = refs[:n], refs[n:2 * n]
        send_sems, recv_sems = refs[2 * n:]
        x, y, c = lax.axis_index("x"), lax.axis_index("y"), lax.axis_index("c")
        my_chip = 2 * x + y
        chips = [(1 - x, y), (x, 1 - y), (1 - x, 1 - y)]

        def copy(a, k, src_chip, dst_chip):
            px, py = chips[k]
            return pltpu.make_async_remote_copy(
                src_ref=ins[a].at[src_chip], dst_ref=outs[a].at[dst_chip], send_sem=send_sems.at[a, k],
                recv_sem=recv_sems.at[a, k], device_id=(px, py, c), device_id_type=MESH)

        sends = [copy(a, k, 2 * chips[k][0] + chips[k][1], my_chip) for k in range(3) for a in range(n)]
        for cp in sends:
            cp.start()
        for k in range(3):
            for a in range(n):
                copy(a, k, my_chip, 2 * chips[k][0] + chips[k][1]).wait_recv()
        for cp in sends:
            cp.wait_send()

    return _call(
        body, name="grad_exchange_chips", out_shape=[jax.ShapeDtypeStruct(h.shape, h.dtype) for h in hs],
        in_specs=[ANY] * n, out_specs=[ANY] * n,
        scratch_shapes=[pltpu.SemaphoreType.DMA((n, 3)), pltpu.SemaphoreType.DMA((n, 3))],
    )(*hs)


def _sum_chips(a, c_arr, name):
    _, r, cc = a.shape
    tr = _row_tile(r)

    def body(c_ref, a_ref, o_ref):
        o_ref[...] = ((a_ref[0].astype(F32) + a_ref[1].astype(F32)) + a_ref[2].astype(F32)) + a_ref[3].astype(F32)

    grid_spec = pltpu.PrefetchScalarGridSpec(
        num_scalar_prefetch=1, grid=(r // tr,),
        in_specs=[pl.BlockSpec((N_CHIPS, tr, cc), lambda i, c_ref: (0, i, 0))],
        out_specs=pl.BlockSpec((None, tr, cc), lambda i, c_ref: (c_ref[0], i, 0)))
    return _call(body, name=name, grid_spec=grid_spec, out_shape=jax.ShapeDtypeStruct((2, r, cc), F32),
                 compiler_params=_params("parallel"))(c_arr, a)


def _grad_share(bufs):
    n = len(bufs)

    def body(*refs):
        outs = refs[n:2 * n]
        send_sems, recv_sems = refs[2 * n:]
        x, y, c = lax.axis_index("x"), lax.axis_index("y"), lax.axis_index("c")
        copies = [pltpu.make_async_remote_copy(src_ref=outs[a].at[c], dst_ref=outs[a].at[c], send_sem=send_sems.at[a],
                                               recv_sem=recv_sems.at[a], device_id=(x, y, 1 - c),
                                               device_id_type=MESH) for a in range(n)]
        for cp in copies:
            cp.start()
        for a in range(n):
            pltpu.make_async_remote_copy(src_ref=outs[a].at[c], dst_ref=outs[a].at[1 - c], send_sem=send_sems.at[a],
                                         recv_sem=recv_sems.at[a], device_id=(x, y, 1 - c),
                                         device_id_type=MESH).wait_recv()
        for cp in copies:
            cp.wait_send()

    return _call(
        body, name="grad_share_cores", out_shape=[jax.ShapeDtypeStruct(b.shape, b.dtype) for b in bufs],
        in_specs=[ANY] * n, out_specs=[ANY] * n, input_output_aliases={a: a for a in range(n)},
        scratch_shapes=[pltpu.SemaphoreType.DMA((n,)), pltpu.SemaphoreType.DMA((n,))],
    )(*bufs)


def _adamw_math(g_ref, w_ref, m_ref, v_ref, d_ref, nm_ref, nv_ref):
    gv = g_ref[...]
    nm = ADAM_B1 * m_ref[...] + (1.0 - ADAM_B1) * gv
    nv = ADAM_B2 * v_ref[...] + (1.0 - ADAM_B2) * (gv * gv)
    m_hat = nm / (1.0 - ADAM_B1 ** ADAM_STEP)
    v_hat = nv / (1.0 - ADAM_B2 ** ADAM_STEP)
    d_ref[...] = -ADAM_LR * (m_hat / (jnp.sqrt(v_hat) + ADAM_EPS) + ADAM_WD * w_ref[...])
    nm_ref[...] = nm
    nv_ref[...] = nv


def _adamw_big(g, w, m, v, name):
    d, r, c = g.shape
    tr = r if r <= 704 else 256
    spec = pl.BlockSpec((1, tr, c), lambda l, i: (l, i, 0))

    def body(*refs):
        _adamw_math(*refs)

    shp = jax.ShapeDtypeStruct(g.shape, F32)
    return _call(body, name=name, grid=(d, r // tr), in_specs=[spec] * 4, out_specs=[spec] * 3,
                 out_shape=[shp, shp, shp], compiler_params=_params("parallel", "parallel"))(g, w, m, v)


def _adamw_small(gs, ws, ms, vs):
    n = len(gs)

    def body(*refs):
        ins, outs = refs[:4 * n], refs[4 * n:]
        for k in range(n):
            _adamw_math(ins[k], ins[n + k], ins[2 * n + k], ins[3 * n + k], outs[k], outs[n + k], outs[2 * n + k])

    shp = [jax.ShapeDtypeStruct(g.shape, F32) for g in gs]
    res = _call(body, name="adamw_small", out_shape=shp * 3)(*gs, *ws, *ms, *vs)
    return res[:n], res[n:2 * n], res[2 * n:]


def _rows(a, rows):
    flat = a.reshape(-1)
    return jnp.pad(flat, (0, rows * D_MODEL - flat.shape[0])).reshape(rows, D_MODEL)


def _small_rows(p, extra=None):
    parts = [p[n].reshape(-1) for n in SMALL_NAMES]
    if extra is not None:
        parts.append(extra.reshape(-1))
    flat = jnp.concatenate(parts)
    return jnp.pad(flat, (0, ROWS_SMALL * D_MODEL - flat.shape[0])).reshape(ROWS_SMALL, D_MODEL)


CONV_SHARD = (DEPTH, 3, 2 * D_FF // N_CHIPS)
N_CONV_SHARD = DEPTH * 3 * (2 * D_FF // N_CHIPS)


def _small_pack(g, loss):
    conv = jnp.transpose(g["conv_w"].reshape(DEPTH, 3, N_CHIPS, 2 * D_FF // N_CHIPS), (2, 0, 1, 3))
    conv = jnp.stack([_rows(conv[k], ROWS_CONV) for k in range(N_CHIPS)])
    small = jnp.broadcast_to(_small_rows(g, loss), (N_CHIPS, ROWS_SMALL, D_MODEL))
    return jnp.concatenate([conv, small], axis=1)


def _unpack_small(pack):
    out = {"conv_w": pack[:ROWS_CONV].reshape(-1)[:N_CONV_SHARD].reshape(CONV_SHARD)}
    flat = pack[ROWS_CONV:].reshape(-1)
    k = 0
    for name in SMALL_NAMES:
        shape = SMALL_SHAPES[name]
        n = 1
        for d in shape:
            n *= d
        out[name] = flat[k:k + n].reshape(shape)
        k += n
    out["extra"] = flat[k]
    return out


def _gather_weights(p):
    shards = [p[n].astype(jnp.bfloat16) for n in BIG_NAMES[:4]] + [p["conv_w"]]
    my_chip = 2 * lax.axis_index("x") + lax.axis_index("y")
    w_in, w_o, w_up, w_down, conv = [lax.dynamic_update_index_in_dim(got, own, my_chip, 0)
                                     for got, own in zip(_all_gather(shards), shards)]

    def by_cols(a):
        k, d, r, wd = a.shape
        return jnp.transpose(a, (1, 2, 0, 3)).reshape(d, r, k * wd)

    def by_rows(a):
        k, d, hgt, cc = a.shape
        return jnp.transpose(a, (1, 0, 2, 3)).reshape(d, k * hgt, cc)

    return dict(w_in=by_cols(w_in), w_o=by_rows(w_o), w_up=by_cols(w_up), w_down=by_rows(w_down), conv_w=by_cols(conv))


def _reduce_grads(grads, loss, c):
    bigs = [grads[n] for n in BIG_NAMES[:4]]
    sp = _small_pack(grads, loss)
    c_arr = jnp.reshape(c, (1,)).astype(jnp.int32)
    got = _grad_swap(bigs, sp)
    pair = [_pair_add(bigs[a], got[a], c_arr, "grad_add_cores_" + BIG_NAMES[a], ICI_DT) for a in range(4)]
    pair.append(_pair_add_small(sp, got[4], c_arr))
    my_chip = 2 * lax.axis_index("x") + lax.axis_index("y")
    parts = [lax.dynamic_update_index_in_dim(got_k, lax.dynamic_index_in_dim(own, my_chip, 0, keepdims=False),
                                             my_chip, 0) for got_k, own in zip(_grad_exchange(pair), pair)]
    total = [_sum_chips(parts[a], c_arr, "grad_sum_chips_" + (BIG_NAMES[:4] + ("small",))[a]) for a in range(5)]
    shared = _grad_share(total)
    out = dict(zip(BIG_NAMES[:4], shared[:4]))
    out.update(_unpack_small(shared[4].reshape(2 * SP_HALF, D_MODEL)))
    return out


def kernel(x, norm1_g, w_in, sgu_norm_g, sgu_w, sgu_b, pool_w, pool_scale, mix_norm_g, w_o, norm2_g, w_up, conv_w, conv_b, w_down, final_g, loss_target, m_norm1_g, m_w_in, m_sgu_norm_g, m_sgu_w, m_sgu_b, m_pool_w, m_pool_scale, m_mix_norm_g, m_w_o, m_norm2_g, m_w_up, m_conv_w, m_conv_b, m_w_down, m_final_g, v_norm1_g, v_w_in, v_sgu_norm_g, v_sgu_w, v_sgu_b, v_pool_w, v_pool_scale, v_mix_norm_g, v_w_o, v_norm2_g, v_w_up, v_conv_w, v_conv_b, v_w_down, v_final_g):
    names = ("norm1_g", "w_in", "sgu_norm_g", "sgu_w", "sgu_b", "pool_w", "pool_scale", "mix_norm_g", "w_o",
             "norm2_g", "w_up", "conv_w", "conv_b", "w_down", "final_g")
    p = dict(zip(names, (norm1_g, w_in, sgu_norm_g, sgu_w, sgu_b, pool_w, pool_scale, mix_norm_g, w_o, norm2_g,
                         w_up, conv_w, conv_b, w_down, final_g)))
    pm = dict(zip(names, (m_norm1_g, m_w_in, m_sgu_norm_g, m_sgu_w, m_sgu_b, m_pool_w, m_pool_scale, m_mix_norm_g,
                          m_w_o, m_norm2_g, m_w_up, m_conv_w, m_conv_b, m_w_down, m_final_g)))
    pv = dict(zip(names, (v_norm1_g, v_w_in, v_sgu_norm_g, v_sgu_w, v_sgu_b, v_pool_w, v_pool_scale, v_mix_norm_g,
                          v_w_o, v_norm2_g, v_w_up, v_conv_w, v_conv_b, v_w_down, v_final_g)))
    c = lax.axis_index("c")
    full = dict(p)
    full.update(_gather_weights(p))

    loss, dx, grads = _local_step(x[0], loss_target[0], full)

    g = _reduce_grads(grads, loss, c)
    d, nm, nv = {}, {}, {}
    for n in BIG_NAMES:
        d[n], nm[n], nv[n] = _adamw_big(g[n], p[n], pm[n], pv[n], "adamw_" + n)

    def two_d(a):
        return a.reshape(1, -1) if a.ndim == 1 else a

    ds, ms, vs = _adamw_small([two_d(g[n]) for n in SMALL_NAMES], [two_d(p[n]) for n in SMALL_NAMES],
                              [two_d(pm[n]) for n in SMALL_NAMES], [two_d(pv[n]) for n in SMALL_NAMES])
    for k, n in enumerate(SMALL_NAMES):
        d[n], nm[n], nv[n] = (a.reshape(p[n].shape) for a in (ds[k], ms[k], vs[k]))
    return (g["extra"], dx[None], *[g[n] for n in names], *[d[n] for n in names], *[nm[n] for n in names],
            *[nv[n] for n in names])
```

```python
import functools

import jax
import jax.numpy as jnp
from jax import lax
from jax.experimental import pallas as pl
from jax.experimental.pallas import tpu as pltpu

F32 = jnp.float32
MXU_DT = jnp.bfloat16

D_MODEL = 1024
DEPTH = 2
HEAD_DIM = 64
W_A = 256
W_B = 512
W_C = 256
IN_COLS = 2 * W_A + 3 * W_B + W_C
CHUNK = 128
POOL_WINDOWS = (2, 4, 8, 16)
D_FF = 2816
EPS = 1e-6
N_CHIPS = 4

ADAM_LR = 0.001
ADAM_B1 = 0.9
ADAM_B2 = 0.999
ADAM_EPS = 1e-08
ADAM_WD = 0.01
ADAM_STEP = 10

LANES = 128
TQ = 256
TK = 256
TM = 256
TM_MM = 512
HALO = 16
VMEM_LIMIT = 56 * 1024 * 1024

ROWS_CONV = 16
ROWS_SMALL = 240
SP_HALF = (ROWS_CONV + ROWS_SMALL) // 2
ICI_DT = jnp.bfloat16

BIG_NAMES = ("w_in", "w_o", "w_up", "w_down", "conv_w")
SMALL_NAMES = ("norm1_g", "sgu_norm_g", "sgu_w", "sgu_b", "pool_w", "pool_scale",
               "mix_norm_g", "norm2_g", "conv_b", "final_g")
SMALL_SHAPES = {
    "norm1_g": (DEPTH, D_MODEL), "sgu_norm_g": (DEPTH, W_A), "sgu_w": (DEPTH, 4, CHUNK, CHUNK),
    "sgu_b": (DEPTH, 4, CHUNK), "pool_w": (DEPTH, 4, 64, 64), "pool_scale": (DEPTH, W_C),
    "mix_norm_g": (DEPTH, D_MODEL), "norm2_g": (DEPTH, D_MODEL), "conv_b": (DEPTH, 2 * D_FF),
    "final_g": (D_MODEL,),
}


def _call(body, **kw):
    return pl.pallas_call(body, **kw)


def _params(*sem):
    return pltpu.CompilerParams(dimension_semantics=sem, vmem_limit_bytes=VMEM_LIMIT)


def _dot(a, b):
    return jnp.dot(a, b, preferred_element_type=F32)


def _dot_nt(a, b):
    return lax.dot_general(a, b, (((1,), (1,)), ((), ())), preferred_element_type=F32)


def _dot_tn(a, b):
    return lax.dot_general(a, b, (((0,), (0,)), ((), ())), preferred_element_type=F32)


def _group_mean(sq, gmat):
    sqb = sq.astype(MXU_DT)
    cols = [_dot(sqb[:, b * LANES:(b + 1) * LANES], gmat) for b in range(sq.shape[1] // LANES)]
    return cols[0] if len(cols) == 1 else jnp.concatenate(cols, axis=-1)


def _group_matrix():
    r = jnp.arange(LANES)
    return jnp.where((r[:, None] // HEAD_DIM) == (r[None, :] // HEAD_DIM), 1.0 / HEAD_DIM, 0.0).astype(MXU_DT)


def _tile(n):
    return max(t for t in range(LANES, 1536 + 1, LANES) if n % t == 0)


def _row_spec(tm, cols, col_block=0):
    return pl.BlockSpec((tm, cols), lambda i, cb=col_block: (i, cb))


def _full_spec(shape):
    nd = len(shape)
    return pl.BlockSpec(shape, lambda *_: (0,) * nd)


def _mm_res(a, w, res, name):
    S, K = a.shape
    N = w.shape[1]
    tm = TM_MM

    def body(a_ref, w_ref, r_ref, o_ref):
        o_ref[...] = r_ref[...] + _dot(a_ref[...], w_ref[...])

    return _call(
        body, name=name, grid=(S // tm,),
        in_specs=[_row_spec(tm, K), _full_spec((K, N)), _row_spec(tm, N)],
        out_specs=_row_spec(tm, N),
        out_shape=jax.ShapeDtypeStruct((S, N), F32),
        compiler_params=_params("parallel"),
    )(a, w, res)


def _mm_tn(a, b, name, col_tiles=False):
    S, K1 = a.shape
    N = b.shape[1]
    ts = TM_MM
    tk = _tile(K1)
    tn = _tile(N)
    if col_tiles:
        out_spec = pl.BlockSpec((None, tk, tn), lambda m, n, s: (n, m, 0))
        out_shape = jax.ShapeDtypeStruct((N // tn, K1, tn), F32)
    else:
        out_spec = pl.BlockSpec((tk, tn), lambda m, n, s: (m, n))
        out_shape = jax.ShapeDtypeStruct((K1, N), F32)

    def body(a_ref, b_ref, o_ref):
        @pl.when(pl.program_id(2) == 0)
        def _():
            o_ref[...] = jnp.zeros_like(o_ref)

        o_ref[...] += _dot_tn(a_ref[...], b_ref[...].astype(MXU_DT))

    return _call(
        body, name=name, grid=(K1 // tk, N // tn, S // ts),
        in_specs=[pl.BlockSpec((ts, tk), lambda m, n, s: (s, m)),
                  pl.BlockSpec((ts, tn), lambda m, n, s: (s, n))],
        out_specs=out_spec, out_shape=out_shape,
        compiler_params=_params("parallel", "parallel", "arbitrary"),
    )(a, b)


def _mm_nt_rmsbwd(pairs, x, g, dres, name):
    S, D = x.shape
    tm = TM
    n = len(pairs)

    def body(*refs):
        a_refs = refs[:n]
        w_refs = refs[n:2 * n]
        x_ref, g_ref, r_ref, dx_ref, dg_ref = refs[2 * n:]
        dh = _dot_nt(a_refs[0][...], w_refs[0][...])
        for k in range(1, n):
            dh += _dot_nt(a_refs[k][...], w_refs[k][...])
        xv = x_ref[...]
        r = lax.rsqrt(jnp.mean(xv * xv, axis=-1, keepdims=True) + EPS)
        xhat = xv * r

        @pl.when(pl.program_id(0) == 0)
        def _():
            dg_ref[...] = jnp.zeros_like(dg_ref)

        dg_ref[...] += jnp.sum(dh * xhat, axis=0, keepdims=True)
        dxh = dh * g_ref[...]
        dx_ref[...] = r_ref[...] + r * (dxh - xhat * jnp.mean(dxh * xhat, axis=-1, keepdims=True))

    in_specs = ([_row_spec(tm, a.shape[1]) for a, _ in pairs] + [_full_spec(w.shape) for _, w in pairs]
                + [_row_spec(tm, D), _full_spec((1, D)), _row_spec(tm, D)])
    return _call(
        body, name=name, grid=(S // tm,), in_specs=in_specs,
        out_specs=[_row_spec(tm, D), _full_spec((1, D))],
        out_shape=[jax.ShapeDtypeStruct((S, D), F32), jax.ShapeDtypeStruct((1, D), F32)],
        compiler_params=_params("arbitrary"),
    )(*[a for a, _ in pairs], *[w for _, w in pairs], x, g, dres)


def _down_proj_loss(a, w, res, g, tgt):
    S, D = res.shape
    K = a.shape[1]
    tm = TM

    def body(a_ref, w_ref, res_ref, g_ref, t_ref, dx_ref, dg_ref, l_ref):
        xv = res_ref[...] + _dot(a_ref[...], w_ref[...])
        r = lax.rsqrt(jnp.mean(xv * xv, axis=-1, keepdims=True) + EPS)
        xhat = xv * r
        diff = xhat * g_ref[...] - t_ref[...]

        @pl.when(pl.program_id(0) == 0)
        def _():
            dg_ref[...] = jnp.zeros_like(dg_ref)
            l_ref[...] = jnp.zeros_like(l_ref)

        l_ref[...] += jnp.full(l_ref.shape, 0.5 * jnp.sum(jnp.mean(diff * diff, axis=-1, keepdims=True)), F32)
        dout = diff * (1.0 / D)
        dg_ref[...] += jnp.sum(dout * xhat, axis=0, keepdims=True)
        dxh = dout * g_ref[...]
        dx_ref[...] = r * (dxh - xhat * jnp.mean(dxh * xhat, axis=-1, keepdims=True))

    return _call(
        body, name="down_proj_loss", grid=(S // tm,),
        in_specs=[_row_spec(tm, K), _full_spec((K, D)), _row_spec(tm, D), _full_spec((1, D)), _row_spec(tm, D)],
        out_specs=[_row_spec(tm, D), _full_spec((1, D)), _full_spec((8, LANES))],
        out_shape=[jax.ShapeDtypeStruct((S, D), F32), jax.ShapeDtypeStruct((1, D), F32),
                   jax.ShapeDtypeStruct((8, LANES), F32)],
        compiler_params=_params("arbitrary"),
    )(a, w, res, g, tgt)


def _mix_out(ya, yb, yc, gm, wo, x, gmat):
    S = x.shape[0]
    tm = TM

    def body(ya_ref, yb_ref, yc_ref, gm_ref, wo_ref, x_ref, gmat_ref, x2_ref, yn_ref):
        y = jnp.concatenate([ya_ref[...], yb_ref[...], yc_ref[...]], axis=-1)
        r = lax.rsqrt(_group_mean(y * y, gmat_ref[...]) + EPS)
        yn = (y * r * gm_ref[...]).astype(MXU_DT)
        yn_ref[...] = yn
        x2_ref[...] = x_ref[...] + _dot(yn, wo_ref[...])

    return _call(
        body, name="mix_out", grid=(S // tm,),
        in_specs=[_row_spec(tm, W_A), _row_spec(tm, W_B), _row_spec(tm, W_C), _full_spec((1, D_MODEL)),
                  _full_spec((D_MODEL, D_MODEL)), _row_spec(tm, D_MODEL), _full_spec((LANES, LANES))],
        out_specs=[_row_spec(tm, D_MODEL), _row_spec(tm, D_MODEL)],
        out_shape=[jax.ShapeDtypeStruct((S, D_MODEL), F32), jax.ShapeDtypeStruct((S, D_MODEL), MXU_DT)],
        compiler_params=_params("parallel"),
    )(ya, yb, yc, gm, wo, x, gmat)


def _mix_out_bwd(dx2, wo, ya, yb, yc, gm, gmat):
    S = dx2.shape[0]
    tm = TM

    def body(dx2_ref, wo_ref, ya_ref, yb_ref, yc_ref, gm_ref, gmat_ref, dya_ref, dyb_ref, dyc_ref, dgm_ref):
        dyn = _dot_nt(dx2_ref[...].astype(MXU_DT), wo_ref[...])
        y = jnp.concatenate([ya_ref[...], yb_ref[...], yc_ref[...]], axis=-1)
        r = lax.rsqrt(_group_mean(y * y, gmat_ref[...]) + EPS)
        yhat = y * r

        @pl.when(pl.program_id(0) == 0)
        def _():
            dgm_ref[...] = jnp.zeros_like(dgm_ref)

        dgm_ref[...] += jnp.sum(dyn * yhat, axis=0, keepdims=True)
        dyh = dyn * gm_ref[...]
        dy = r * (dyh - yhat * _group_mean(dyh * yhat, gmat_ref[...]))
        dya_ref[...] = dy[:, :W_A]
        dyb_ref[...] = dy[:, W_A:W_A + W_B]
        dyc_ref[...] = dy[:, W_A + W_B:]

    return _call(
        body, name="mix_out_bwd", grid=(S // tm,),
        in_specs=[_row_spec(tm, D_MODEL), _full_spec((D_MODEL, D_MODEL)), _row_spec(tm, W_A), _row_spec(tm, W_B),
                  _row_spec(tm, W_C), _full_spec((1, D_MODEL)), _full_spec((LANES, LANES))],
        out_specs=[_row_spec(tm, W_A), _row_spec(tm, W_B), _row_spec(tm, W_C), _full_spec((1, D_MODEL))],
        out_shape=[jax.ShapeDtypeStruct((S, W_A), F32), jax.ShapeDtypeStruct((S, W_B), F32),
                   jax.ShapeDtypeStruct((S, W_C), F32), jax.ShapeDtypeStruct((1, D_MODEL), F32)],
        compiler_params=_params("arbitrary"),
    )(dx2, wo, ya, yb, yc, gm, gmat)


_SQRT_HALF = 0.7071067811865476
_INV_SQRT_2PI = 0.3989422804014327


def _sgu_common(a, sng, wm_ref, bias, gmat):
    phi = 0.5 * (1.0 + lax.erf(a * _SQRT_HALF))
    ga = a * phi
    u = ga[:, :W_A]
    v = ga[:, W_A:]
    r = lax.rsqrt(_group_mean(v * v, gmat) + EPS)
    vhat = v * r
    vn = (vhat * sng).astype(MXU_DT)
    head = lax.broadcasted_iota(jnp.int32, (CHUNK, W_A), 1) // HEAD_DIM
    rows = []
    for c in range(a.shape[0] // CHUNK):
        vc = vn[c * CHUNK:(c + 1) * CHUNK]
        s = bias
        for h in range(4):
            s = s + jnp.where(head == h, _dot(wm_ref[h], vc), 0.0)
        rows.append(s)
    s = jnp.concatenate(rows, axis=0)
    return phi, u, r, vhat, vn, s


def _tril_weights(sgu_w_l):
    t = jnp.arange(CHUNK)
    return jnp.where((t[None, :] <= t[:, None])[None], sgu_w_l, 0.0)


def _sgu_bwd(proj, dy, sng, wm, wmt, bias, gmat):
    S = proj.shape[0]
    tm = TM

    def body(a_ref, dy_ref, sng_ref, wm_ref, wmt_ref, b_ref, gmat_ref, da_ref, dw_ref, db_ref, dsng_ref):
        a = a_ref[...]
        dy = dy_ref[...]
        gmat = gmat_ref[...]
        sng = sng_ref[...]
        phi, u, r, vhat, vn, s = _sgu_common(a, sng, wm_ref, b_ref[...], gmat)
        du = dy * s
        ds = dy * u

        @pl.when(pl.program_id(0) == 0)
        def _():
            dw_ref[...] = jnp.zeros_like(dw_ref)
            db_ref[...] = jnp.zeros_like(db_ref)
            dsng_ref[...] = jnp.zeros_like(dsng_ref)

        head = lax.broadcasted_iota(jnp.int32, (CHUNK, W_A), 1) // HEAD_DIM
        tt = lax.broadcasted_iota(jnp.int32, (CHUNK, CHUNK), 0)
        ss = lax.broadcasted_iota(jnp.int32, (CHUNK, CHUNK), 1)
        rows = []
        for c in range(tm // CHUNK):
            dsc = ds[c * CHUNK:(c + 1) * CHUNK]
            vc = vn[c * CHUNK:(c + 1) * CHUNK]
            db_ref[...] += dsc
            dsb = dsc.astype(MXU_DT)
            dvn = jnp.zeros((CHUNK, W_A), F32)
            for h in range(4):
                dvn = dvn + jnp.where(head == h, _dot(wmt_ref[h], dsb), 0.0)
                dsh = jnp.where(head == h, dsc, 0.0).astype(MXU_DT)
                dw_ref[h] += jnp.where(ss <= tt, _dot_nt(dsh, vc), 0.0)
            rows.append(dvn)
        dvn = jnp.concatenate(rows, axis=0)
        dsng_ref[...] += jnp.sum(dvn * vhat, axis=0, keepdims=True)
        dvh = dvn * sng
        dv = r * (dvh - vhat * _group_mean(dvh * vhat, gmat))
        dga = jnp.concatenate([du, dv], axis=-1)
        dgelu = phi + a * (_INV_SQRT_2PI * jnp.exp(-0.5 * a * a))
        da_ref[...] = (dga * dgelu).astype(da_ref.dtype)

    return _call(
        body, name="sgu_bwd", grid=(S // tm,),
        in_specs=[_row_spec(tm, 2 * W_A), _row_spec(tm, W_A), _full_spec((1, W_A)), _full_spec((4, CHUNK, CHUNK)),
                  _full_spec((4, CHUNK, CHUNK)), _full_spec((CHUNK, W_A)), _full_spec((LANES, LANES))],
        out_specs=[_row_spec(tm, 2 * W_A), _full_spec((4, CHUNK, CHUNK)), _full_spec((CHUNK, W_A)),
                   _full_spec((1, W_A))],
        out_shape=[jax.ShapeDtypeStruct((S, 2 * W_A), MXU_DT), jax.ShapeDtypeStruct((4, CHUNK, CHUNK), F32),
                   jax.ShapeDtypeStruct((CHUNK, W_A), F32), jax.ShapeDtypeStruct((1, W_A), F32)],
        compiler_params=_params("arbitrary"),
    )(proj, dy, sng, wm, wmt, bias, gmat)


HG = 4
LW = HG * HEAD_DIM
Q_BLK0 = (2 * W_A) // LW
K_BLK0 = Q_BLK0 + W_B // LW
V_BLK0 = K_BLK0 + W_B // LW
N_GROUPS = W_B // LW
EXP_IS_ZERO_BELOW = -120.0


def _tri_matrix():
    r = jnp.arange(TK)
    return (r[:, None] > r[None, :]).astype(MXU_DT)


def _stack_heads(a):
    head = lax.broadcasted_iota(jnp.int32, a.shape, 1) // HEAD_DIM
    return jnp.concatenate([jnp.where(head == h, a, 0.0) for h in range(HG)], axis=0).astype(MXU_DT)


def _unstack_heads(a):
    head = lax.broadcasted_iota(jnp.int32, (TQ, LW), 1) // HEAD_DIM
    out = a[:TQ]
    for h in range(1, HG):
        out = jnp.where(head == h, a[h * TQ:(h + 1) * TQ], out)
    return out


def _sb_scores(q2, kj, tri, key_offset):
    z = _dot_nt(q2, kj)
    sp = jnp.log(1.0 + jnp.exp(-jnp.abs(z)))
    lsp = jnp.minimum(z, 0.0) - sp
    lsm = lsp - z
    msk = None
    if key_offset is not None:
        row = lax.broadcasted_iota(jnp.int32, z.shape, 0) & (TQ - 1)
        col = lax.broadcasted_iota(jnp.int32, z.shape, 1) + key_offset
        msk = col < row
        lsm = jnp.where(msk, lsm, 0.0)
    tail = _dot(lsm.astype(MXU_DT), tri)
    return lsp, lsm, tail, msk


def _sb_fwd(proj_b, tri):
    S = proj_b.shape[0]
    nq = S // TQ
    kpq = TQ // TK
    assert S // TK < LANES

    def body(q_ref, k_ref, v_ref, tri_ref, o_ref, rb_ref, acc_ref):
        i = pl.program_id(1)
        lane2 = lax.broadcasted_iota(jnp.int32, (HG * TQ, LANES), 1)
        q2 = _stack_heads(q_ref[...].astype(F32) * (HEAD_DIM ** -0.5))
        tri = tri_ref[...]
        rb_ref[...] = jnp.zeros_like(rb_ref)

        def block(j, run, key_offset=None, first=False):
            start = pl.multiple_of(j * TK, TK)
            kj = k_ref[pl.ds(start, TK), :]
            vj = v_ref[pl.ds(start, TK), :]
            lsp, lsm, tail, msk = _sb_scores(q2, kj, tri, key_offset)
            rb_ref[...] = jnp.where(lane2 == j, run, rb_ref[...])
            att = jnp.exp(lsp + tail + run)
            if msk is not None:
                att = jnp.where(msk, att, 0.0)
            pv = _dot(att.astype(MXU_DT), vj)
            if first:
                acc_ref[...] = pv
            else:
                acc_ref[...] += pv
            return run + tail[:, :1] + lsm[:, :1]

        run = jnp.zeros((HG * TQ, 1), F32)
        for d in reversed(range(kpq)):
            run = block(i * kpq + d, run, key_offset=d * TK, first=(d == kpq - 1))
        past = i * kpq

        def alive(run):
            return (jnp.max(run) > EXP_IS_ZERO_BELOW).astype(jnp.int32)

        def step(carry):
            n, run, _ = carry
            run = block(past - 1 - n, run)
            return n + 1, run, alive(run)

        n, _, _ = lax.while_loop(lambda c: jnp.logical_and(c[0] < past, c[2] > 0), step,
                                 (jnp.int32(0), run, alive(run)))
        rb_ref[...] = jnp.where(lane2 == LANES - 1, n.astype(F32), rb_ref[...])
        o_ref[...] = _unstack_heads(acc_ref[...])

    once = pl.Buffered(1)
    return _call(
        body, name="sb_fwd", grid=(N_GROUPS, nq),
        in_specs=[pl.BlockSpec((TQ, LW), lambda p, i: (i, Q_BLK0 + p)),
                  pl.BlockSpec((S, LW), lambda p, i: (0, K_BLK0 + p), pipeline_mode=once),
                  pl.BlockSpec((S, LW), lambda p, i: (0, V_BLK0 + p), pipeline_mode=once),
                  pl.BlockSpec((TK, TK), lambda p, i: (0, 0))],
        out_specs=[pl.BlockSpec((TQ, LW), lambda p, i: (i, p)),
                   pl.BlockSpec((None, None, HG * TQ, LANES), lambda p, i: (p, i, 0, 0))],
        out_shape=[jax.ShapeDtypeStruct((S, W_B), F32), jax.ShapeDtypeStruct((N_GROUPS, nq, HG * TQ, LANES), F32)],
        scratch_shapes=[pltpu.VMEM((HG * TQ, LW), F32)],
        compiler_params=_params("parallel", "arbitrary"),
    )(proj_b, proj_b, proj_b, tri)


def _sb_bwd(proj_b, dyb, rb, tri, trit):
    S = proj_b.shape[0]
    nq = S // TQ
    kpq = TQ // TK

    def body(q_ref, k_ref, v_ref, do_ref, rb_ref, tri_ref, trit_ref, dq_ref, dk_acc, dv_acc, dq_acc):
        i = pl.program_id(1)
        lane2 = lax.broadcasted_iota(jnp.int32, (HG * TQ, LANES), 1)
        scale = HEAD_DIM ** -0.5
        q2 = _stack_heads(q_ref[...].astype(F32) * scale)
        do2 = _stack_heads(do_ref[...])
        tri = tri_ref[...]
        trit = trit_ref[...]

        @pl.when(i == 0)
        def _():
            dk_acc[...] = jnp.zeros_like(dk_acc)
            dv_acc[...] = jnp.zeros_like(dv_acc)

        dq_acc[...] = jnp.zeros_like(dq_acc)

        def block(j, pre, key_offset=None):
            start = pl.multiple_of(j * TK, TK)
            kj = k_ref[pl.ds(start, TK), :]
            vj = v_ref[pl.ds(start, TK), :]
            lsp, lsm, tail, msk = _sb_scores(q2, kj, tri, key_offset)
            run = jnp.sum(jnp.where(lane2 == j, rb_ref[...], 0.0), axis=-1, keepdims=True)
            att = jnp.exp(lsp + tail + run)
            if msk is not None:
                att = jnp.where(msk, att, 0.0)
            beta = jnp.exp(lsp)
            dl = _dot_nt(do2, vj) * att
            cin = _dot(dl.astype(MXU_DT), trit)
            dz = dl * (1.0 - beta) - beta * (pre + cin)
            if msk is not None:
                dz = jnp.where(msk, dz, 0.0)
            dzb = dz.astype(MXU_DT)
            dq_acc[...] += _dot(dzb, kj)
            dk_acc[pl.ds(start, TK), :] += _dot_tn(dzb, q2)
            dv_acc[pl.ds(start, TK), :] += _dot_tn(att.astype(MXU_DT), do2)
            return pre + cin[:, TK - 1:] + dl[:, TK - 1:]

        past = i * kpq
        walked = jnp.max(jnp.where(lane2[:8] == LANES - 1, rb_ref[pl.ds(0, 8), :], 0.0)).astype(jnp.int32)
        walked = jnp.clip(walked, 0, past)
        pre = lax.fori_loop(past - walked, past, lambda j, pre: block(j, pre), jnp.zeros((HG * TQ, 1), F32))
        for d in range(kpq):
            pre = block(i * kpq + d, pre, key_offset=d * TK)
        dq_ref[...] = (_unstack_heads(dq_acc[...]) * scale).astype(dq_ref.dtype)

    once = pl.Buffered(1)
    return _call(
        body, name="sb_bwd", grid=(N_GROUPS, nq),
        in_specs=[pl.BlockSpec((TQ, LW), lambda p, i: (i, Q_BLK0 + p)),
                  pl.BlockSpec((S, LW), lambda p, i: (0, K_BLK0 + p), pipeline_mode=once),
                  pl.BlockSpec((S, LW), lambda p, i: (0, V_BLK0 + p), pipeline_mode=once),
                  pl.BlockSpec((TQ, LW), lambda p, i: (i, p)),
                  pl.BlockSpec((None, None, HG * TQ, LANES), lambda p, i: (p, i, 0, 0)),
                  pl.BlockSpec((TK, TK), lambda p, i: (0, 0)),
                  pl.BlockSpec((TK, TK), lambda p, i: (0, 0))],
        out_specs=[pl.BlockSpec((TQ, LW), lambda p, i: (i, p)),
                   pl.BlockSpec((S, LW), lambda p, i: (0, p), pipeline_mode=once),
                   pl.BlockSpec((S, LW), lambda p, i: (0, p), pipeline_mode=once)],
        out_shape=[jax.ShapeDtypeStruct((S, W_B), MXU_DT), jax.ShapeDtypeStruct((S, W_B), F32),
                   jax.ShapeDtypeStruct((S, W_B), F32)],
        scratch_shapes=[pltpu.VMEM((HG * TQ, LW), F32)],
        compiler_params=_params("parallel", "arbitrary"),
    )(proj_b, proj_b, proj_b, dyb, rb, tri, trit)


P_BLK = (2 * W_A + 3 * W_B) // W_C


def _window_lanes():
    g = lax.broadcasted_iota(jnp.int32, (1, W_C), 1) // (W_C // 4)
    w = jnp.where(g == 0, POOL_WINDOWS[0], jnp.where(g == 1, POOL_WINDOWS[1],
                  jnp.where(g == 2, POOL_WINDOWS[2], POOL_WINDOWS[3])))
    return g, w


def _shift_rows(ext, k, tm, lead):
    n = ext.shape[0]
    return pltpu.roll(ext, shift=k % n, axis=0)[lead:lead + tm]


def _pool_diff(p_cur, p_halo, row0, tm):
    ext = jnp.concatenate([p_halo, p_cur], axis=0)
    g, w = _window_lanes()
    acc = ext
    sums = []
    for sh in (1, 2, 4, 8):
        acc = acc + pltpu.roll(acc, shift=sh, axis=0)
        sums.append(acc[HALO:HALO + tm])
    wsum = jnp.where(g == 0, sums[0], jnp.where(g == 1, sums[1], jnp.where(g == 2, sums[2], sums[3])))
    pos = (row0 + 1 + lax.broadcasted_iota(jnp.int32, (tm, W_C), 0)).astype(F32)
    cnt = jnp.minimum(pos, w.astype(F32))
    return wsum / cnt - p_cur, cnt


def _pool_specs(tm, nrow_blocks_halo):
    cur = pl.BlockSpec((tm, W_C), lambda i: (i, P_BLK))
    prev = pl.BlockSpec((HALO, W_C), lambda i: (jnp.maximum(i * (tm // HALO) - 1, 0), P_BLK))
    return cur, prev


def _in_proj_groups(x, g, w, sng, wm, bias, gmat, wbd, scale):
    S, D = x.shape
    tm = TM_MM
    p0 = 2 * W_A + 3 * W_B
    qkv_chunk = 3 * W_B // 2

    def body(x_ref, g_ref, w_ref, sng_ref, wm_ref, b_ref, gmat_ref, wbd_ref, sc_ref,
             h_ref, o_ref, ob_ref, ya_ref, yc_ref, tail_ref):
        i = pl.program_id(0)
        xv = x_ref[...]
        r = lax.rsqrt(jnp.mean(xv * xv, axis=-1, keepdims=True) + EPS)
        h = (xv * r * g_ref[...]).astype(h_ref.dtype)
        h_ref[...] = h

        def project(c0, c1):
            acc = _dot(h, w_ref[:, c0:c1])
            o_ref[:, c0:c1] = acc
            ob_ref[:, c0:c1] = acc.astype(ob_ref.dtype)
            return acc

        a = project(0, 2 * W_A)
        _, u, _, _, _, s = _sgu_common(a, sng_ref[...], wm_ref, b_ref[...], gmat_ref[...])
        ya_ref[...] = u * s
        for c0 in range(2 * W_A, p0, qkv_chunk):
            project(c0, c0 + qkv_chunk)
        p = project(p0, p0 + W_C)
        halo = jnp.where(i > 0, tail_ref[...], 0.0)
        tail_ref[...] = p[tm - HALO:]
        d, _ = _pool_diff(p, halo, i * tm, tm)
        yc_ref[...] = _dot(d.astype(MXU_DT), wbd_ref[...]) * sc_ref[...]

    return _call(
        body, name="in_proj_groups", grid=(S // tm,),
        in_specs=[_row_spec(tm, D), _full_spec((1, D)),
                  pl.BlockSpec((D, IN_COLS), lambda i: (0, 0), pipeline_mode=pl.Buffered(1)),
                  _full_spec((1, W_A)), _full_spec((4, CHUNK, CHUNK)), _full_spec((CHUNK, W_A)),
                  _full_spec((LANES, LANES)), _full_spec((W_C, W_C)), _full_spec((1, W_C))],
        out_specs=[_row_spec(tm, D), _row_spec(tm, IN_COLS), _row_spec(tm, IN_COLS), _row_spec(tm, W_A),
                   _row_spec(tm, W_C)],
        out_shape=[jax.ShapeDtypeStruct((S, D), MXU_DT), jax.ShapeDtypeStruct((S, IN_COLS), F32),
                   jax.ShapeDtypeStruct((S, IN_COLS), MXU_DT), jax.ShapeDtypeStruct((S, W_A), F32),
                   jax.ShapeDtypeStruct((S, W_C), F32)],
        scratch_shapes=[pltpu.VMEM((HALO, W_C), F32)],
        compiler_params=_params("arbitrary"),
    )(x, g, w, sng, wm, bias, gmat, wbd, scale)


def _pool_bwd_a(proj, dy, wbd, scale):
    S = proj.shape[0]
    tm = TM

    def body(p_ref, ph_ref, dy_ref, w_ref, sc_ref, dd_ref, e_ref, dw_ref, dsc_ref):
        i = pl.program_id(0)
        halo = jnp.where(i > 0, ph_ref[...], 0.0)
        d, cnt = _pool_diff(p_ref[...], halo, i * tm, tm)
        db = d.astype(MXU_DT)
        dy = dy_ref[...]

        @pl.when(i == 0)
        def _():
            dw_ref[...] = jnp.zeros_like(dw_ref)
            dsc_ref[...] = jnp.zeros_like(dsc_ref)

        dsc_ref[...] += jnp.sum(dy * _dot(db, w_ref[...]), axis=0, keepdims=True)
        dys = (dy * sc_ref[...]).astype(MXU_DT)
        dw_ref[...] += _dot_tn(db, dys)
        dd = _dot_nt(dys, w_ref[...])
        dd_ref[...] = dd
        e_ref[...] = dd / cnt

    cur, prev = _pool_specs(tm, S // HALO)
    return _call(
        body, name="pool_bwd_a", grid=(S // tm,),
        in_specs=[cur, prev, _row_spec(tm, W_C), _full_spec((W_C, W_C)), _full_spec((1, W_C))],
        out_specs=[_row_spec(tm, W_C), _row_spec(tm, W_C), _full_spec((W_C, W_C)), _full_spec((1, W_C))],
        out_shape=[jax.ShapeDtypeStruct((S, W_C), F32), jax.ShapeDtypeStruct((S, W_C), F32),
                   jax.ShapeDtypeStruct((W_C, W_C), F32), jax.ShapeDtypeStruct((1, W_C), F32)],
        compiler_params=_params("arbitrary"),
    )(proj, proj, dy, wbd, scale)


def _pool_bwd_b(dd, e):
    S = dd.shape[0]
    tm = TM
    nb = S // tm

    def body(dd_ref, e_ref, en_ref, dp_ref):
        i = pl.program_id(0)
        halo = jnp.where(i < nb - 1, en_ref[...], 0.0)
        ext = jnp.concatenate([e_ref[...], halo], axis=0)
        n = ext.shape[0]
        g, _ = _window_lanes()
        acc = ext
        sums = []
        for sh in (1, 2, 4, 8):
            acc = acc + pltpu.roll(acc, shift=n - sh, axis=0)
            sums.append(acc[:tm])
        wsum = jnp.where(g == 0, sums[0], jnp.where(g == 1, sums[1], jnp.where(g == 2, sums[2], sums[3])))
        dp_ref[...] = (wsum - dd_ref[...]).astype(dp_ref.dtype)

    nxt = pl.BlockSpec((HALO, W_C), lambda i: (jnp.minimum((i + 1) * (tm // HALO), S // HALO - 1), 0))
    return _call(
        body, name="pool_bwd_b", grid=(nb,),
        in_specs=[_row_spec(tm, W_C), _row_spec(tm, W_C), nxt],
        out_specs=_row_spec(tm, W_C),
        out_shape=jax.ShapeDtypeStruct((S, W_C), MXU_DT),
        compiler_params=_params("parallel"),
    )(dd, e, e)


TN_FF = 1408
NB_FF = D_FF // TN_FF
CONV_ROWS = 8


def _conv(z_cur, z_halo, cwb, tm):
    ext = jnp.concatenate([z_halo, z_cur], axis=0)
    z2 = _shift_rows(ext, 2, tm, HALO)
    z1 = _shift_rows(ext, 1, tm, HALO)
    zc = cwb[3:4] + z2 * cwb[0:1] + z1 * cwb[1:2] + z_cur * cwb[2:3]
    return zc, z2, z1


def _up_proj_gate(x, g, w, cwb):
    S, D = x.shape
    tm = TM

    def body(x_ref, g_ref, w_ref, c_ref, h_ref, z_ref, f_ref, tail_ref):
        first = pl.program_id(0) == 0
        xv = x_ref[...]
        r = lax.rsqrt(jnp.mean(xv * xv, axis=-1, keepdims=True) + EPS)
        h = (xv * r * g_ref[...]).astype(h_ref.dtype)
        h_ref[...] = h
        for j in range(NB_FF):
            halves = []
            for col0 in (j * TN_FF, D_FF + j * TN_FF):
                zb = _dot(h, w_ref[:, col0:col0 + TN_FF]).astype(z_ref.dtype)
                z_ref[:, col0:col0 + TN_FF] = zb
                zf = zb.astype(F32)
                prev = jnp.where(first, 0.0, tail_ref[:, col0:col0 + TN_FF])
                tail_ref[:, col0:col0 + TN_FF] = zf[tm - HALO:]
                halves.append(_conv(zf, prev, c_ref[:, col0:col0 + TN_FF], tm)[0])
            gate, value = halves
            f_ref[:, j * TN_FF:(j + 1) * TN_FF] = (gate * jax.nn.sigmoid(gate) * value).astype(f_ref.dtype)

    return _call(
        body, name="up_proj_gate", grid=(S // tm,),
        in_specs=[_row_spec(tm, D), _full_spec((1, D)),
                  pl.BlockSpec((D, 2 * D_FF), lambda i: (0, 0), pipeline_mode=pl.Buffered(1)),
                  _full_spec((CONV_ROWS, 2 * D_FF))],
        out_specs=[_row_spec(tm, D), _row_spec(tm, 2 * D_FF), _row_spec(tm, D_FF)],
        out_shape=[jax.ShapeDtypeStruct((S, D), MXU_DT), jax.ShapeDtypeStruct((S, 2 * D_FF), MXU_DT),
                   jax.ShapeDtypeStruct((S, D_FF), MXU_DT)],
        scratch_shapes=[pltpu.VMEM((HALO, 2 * D_FF), F32)],
        compiler_params=_params("arbitrary"),
    )(x, g, w, cwb)


def _gate_up_bwd(z, cwb, w, wd, x, g, dres):
    S, D = x.shape
    tm = TM
    nb = S // tm
    te = tm + HALO

    def body(z_ref, zp_ref, zn_ref, c_ref, w_ref, wd_ref, x_ref, g_ref, r_ref, rn_ref,
             dz_ref, dc_ref, dx_ref, dg_ref):
        i = pl.program_id(0)
        first = i == 0
        last = i == nb - 1
        dxe = jnp.concatenate([r_ref[...], jnp.where(last, 0.0, rn_ref[...])], axis=0).astype(MXU_DT)

        @pl.when(first)
        def _():
            dc_ref[...] = jnp.zeros_like(dc_ref)
            dg_ref[...] = jnp.zeros_like(dg_ref)

        rid = lax.broadcasted_iota(jnp.int32, (CONV_ROWS, TN_FF), 0)

        def conv_ext(cols, c):
            ze = jnp.concatenate([jnp.where(first, 0.0, zp_ref[:, cols].astype(F32)), z_ref[:, cols].astype(F32),
                                  jnp.where(last, 0.0, zn_ref[:, cols].astype(F32))], axis=0)
            z2 = _shift_rows(ze, 2, te, HALO)
            z1 = _shift_rows(ze, 1, te, HALO)
            z0 = ze[HALO:]
            return c[3:4] + z2 * c[0:1] + z1 * c[1:2] + z0 * c[2:3], z2, z1, z0

        def conv_t(d, c):
            return d[:tm] * c[2:3] + _shift_rows(d, -1, tm, 0) * c[1:2] + _shift_rows(d, -2, tm, 0) * c[0:1]

        def taps(dv, s2, s1, s0):
            dv = dv[:tm]
            sums = [jnp.sum(dv * s2[:tm], axis=0, keepdims=True), jnp.sum(dv * s1[:tm], axis=0, keepdims=True),
                    jnp.sum(dv * s0[:tm], axis=0, keepdims=True), jnp.sum(dv, axis=0, keepdims=True)]
            out = jnp.zeros((CONV_ROWS, TN_FF), F32)
            for k, v in enumerate(sums):
                out = jnp.where(rid == k, v, out)
            return out

        dh = jnp.zeros((tm, D), F32)
        for j in range(NB_FF):
            gc = slice(j * TN_FF, (j + 1) * TN_FF)
            uc = slice(D_FF + j * TN_FF, D_FF + (j + 1) * TN_FF)
            cg = c_ref[:, gc]
            cu = c_ref[:, uc]
            gt, g2, g1, g0 = conv_ext(gc, cg)
            ut, u2, u1, u0 = conv_ext(uc, cu)
            df = _dot_nt(dxe, wd_ref[gc, :])
            sg = jax.nn.sigmoid(gt)
            dgv = df * ut * (sg * (1.0 + gt * (1.0 - sg)))
            duv = df * (gt * sg)
            dzg = conv_t(dgv, cg).astype(dz_ref.dtype)
            dzu = conv_t(duv, cu).astype(dz_ref.dtype)
            dz_ref[:, gc] = dzg
            dz_ref[:, uc] = dzu
            dc_ref[:, gc] += taps(dgv, g2, g1, g0)
            dc_ref[:, uc] += taps(duv, u2, u1, u0)
            dh += _dot_nt(dzg, w_ref[:, gc]) + _dot_nt(dzu, w_ref[:, uc])

        xv = x_ref[...]
        r = lax.rsqrt(jnp.mean(xv * xv, axis=-1, keepdims=True) + EPS)
        xhat = xv * r
        dg_ref[...] += jnp.sum(dh * xhat, axis=0, keepdims=True)
        dxh = dh * g_ref[...]
        dx_ref[...] = r_ref[...] + r * (dxh - xhat * jnp.mean(dxh * xhat, axis=-1, keepdims=True))

    hb = tm // HALO
    last_halo = S // HALO - 1
    return _call(
        body, name="gate_up_bwd", grid=(nb,),
        in_specs=[_row_spec(tm, 2 * D_FF),
                  pl.BlockSpec((HALO, 2 * D_FF), lambda i: (jnp.maximum(i * hb - 1, 0), 0)),
                  pl.BlockSpec((HALO, 2 * D_FF), lambda i: (jnp.minimum((i + 1) * hb, last_halo), 0)),
                  _full_spec((CONV_ROWS, 2 * D_FF)),
                  pl.BlockSpec((D, 2 * D_FF), lambda i: (0, 0), pipeline_mode=pl.Buffered(1)),
                  pl.BlockSpec((D_FF, D), lambda i: (0, 0), pipeline_mode=pl.Buffered(1)),
                  _row_spec(tm, D), _full_spec((1, D)), _row_spec(tm, D),
                  pl.BlockSpec((HALO, D), lambda i: (jnp.minimum((i + 1) * hb, last_halo), 0))],
        out_specs=[_row_spec(tm, 2 * D_FF), _full_spec((CONV_ROWS, 2 * D_FF)), _row_spec(tm, D), _full_spec((1, D))],
        out_shape=[jax.ShapeDtypeStruct((S, 2 * D_FF), MXU_DT), jax.ShapeDtypeStruct((CONV_ROWS, 2 * D_FF), F32),
                   jax.ShapeDtypeStruct((S, D), F32), jax.ShapeDtypeStruct((1, D), F32)],
        compiler_params=_params("arbitrary"),
    )(z, z, z, cwb, w, wd, x, g, dres, dres)


def _layer_consts(w, l):
    wm = _tril_weights(w["sgu_w"][l])
    eye = jnp.eye(4, dtype=F32)
    wbd = (w["pool_w"][l][:, :, None, :] * eye[:, None, :, None]).reshape(W_C, W_C)
    cwb = jnp.concatenate([w["conv_w"][l], w["conv_b"][l][None], jnp.zeros((CONV_ROWS - 4, 2 * D_FF), F32)], axis=0)
    return dict(
        g1=w["norm1_g"][l][None], g2=w["norm2_g"][l][None], gm=w["mix_norm_g"][l][None],
        sng=w["sgu_norm_g"][l][None], wm=wm.astype(MXU_DT), wmt=jnp.swapaxes(wm, 1, 2).astype(MXU_DT),
        bias=jnp.repeat(jnp.transpose(w["sgu_b"][l]), HEAD_DIM, axis=1),
        wbd=wbd.astype(MXU_DT), scale=w["pool_scale"][l][None], cwb=cwb,
        w_in=w["w_in"][l], w_o=w["w_o"][l], w_up=w["w_up"][l], w_down=w["w_down"][l],
    )


def _local_step(x, tgt, w):
    gmat = _group_matrix()
    tri = _tri_matrix()
    trit = jnp.transpose(tri)
    saved = []
    for l in range(DEPTH):
        c = _layer_consts(w, l)
        h1, proj, proj_b, ya, yc = _in_proj_groups(x, c["g1"], c["w_in"], c["sng"], c["wm"], c["bias"], gmat,
                                                   c["wbd"], c["scale"])
        yb, rb = _sb_fwd(proj_b, tri)
        x2, yn = _mix_out(ya, yb, yc, c["gm"], c["w_o"], x, gmat)
        h2, z, f = _up_proj_gate(x2, c["g2"], c["w_up"], c["cwb"])
        saved.append(dict(c=c, x=x, proj=proj, proj_b=proj_b, h1=h1, ya=ya, yb=yb, yc=yc, rb=rb, x2=x2, yn=yn,
                          z=z, h2=h2, f=f))
        if l < DEPTH - 1:
            x = _mm_res(f, c["w_down"], x2, "down_proj")

    last = saved[-1]
    dx, d_final_g, loss8 = _down_proj_loss(last["f"], last["c"]["w_down"], last["x2"], w["final_g"][None], tgt)
    grads = {n: [None] * DEPTH for n in ("norm1_g", "w_in", "sgu_norm_g", "sgu_w", "sgu_b", "pool_w", "pool_scale",
                                         "mix_norm_g", "w_o", "norm2_g", "w_up", "conv_w", "conv_b", "w_down")}
    for l in reversed(range(DEPTH)):
        s = saved[l]
        c = s["c"]
        grads["w_down"][l] = _mm_tn(s["f"], dx, "down_proj_wgrad").reshape(N_CHIPS, D_FF // N_CHIPS, D_MODEL)
        dz, dcwb, dx2, dg2 = _gate_up_bwd(s["z"], c["cwb"], c["w_up"], c["w_down"], s["x2"], c["g2"], dx)
        grads["conv_w"][l] = dcwb[:3]
        grads["conv_b"][l] = dcwb[3]
        grads["w_up"][l] = _mm_tn(s["h2"], dz, "up_proj_wgrad", col_tiles=True)
        grads["norm2_g"][l] = dg2[0]
        grads["w_o"][l] = _mm_tn(s["yn"], dx2, "out_proj_wgrad").reshape(N_CHIPS, D_MODEL // N_CHIPS, D_MODEL)
        dya, dyb, dyc, dgm = _mix_out_bwd(dx2, c["w_o"], s["ya"], s["yb"], s["yc"], c["gm"], gmat)
        grads["mix_norm_g"][l] = dgm[0]
        dd, e, dwbd, dscale = _pool_bwd_a(s["proj"], dyc, c["wbd"], c["scale"])
        dp = _pool_bwd_b(dd, e)
        grads["pool_w"][l] = jnp.stack([dwbd[g * 64:(g + 1) * 64, g * 64:(g + 1) * 64] for g in range(4)])
        grads["pool_scale"][l] = dscale[0]
        dq, dk, dv = _sb_bwd(s["proj_b"], dyb, s["rb"], tri, trit)
        da, dwm, dbias, dsng = _sgu_bwd(s["proj"], dya, c["sng"], c["wm"], c["wmt"], c["bias"], gmat)
        grads["sgu_w"][l] = dwm
        grads["sgu_b"][l] = jnp.transpose(jnp.sum(dbias.reshape(CHUNK, 4, HEAD_DIM), axis=-1))
        grads["sgu_norm_g"][l] = dsng[0]
        dproj = jnp.concatenate([da, dq, dk.astype(MXU_DT), dv.astype(MXU_DT), dp], axis=1)
        dw_in = _mm_tn(s["h1"], dproj, "in_proj_wgrad")
        grads["w_in"][l] = jnp.transpose(dw_in.reshape(D_MODEL, N_CHIPS, IN_COLS // N_CHIPS), (1, 0, 2))
        dx, dg1 = _mm_nt_rmsbwd([(dproj, c["w_in"])], s["x"], c["g1"], dx2, "in_proj_bwd")
        grads["norm1_g"][l] = dg1[0]

    out = {n: jnp.stack(v) for n, v in grads.items()}
    out["final_g"] = d_final_g[0]
    return loss8[0, 0], dx, out


MESH = pl.DeviceIdType.MESH
ANY = pl.BlockSpec(memory_space=pl.ANY)


def _all_gather(shards):
    n = len(shards)

    def body(*refs):
        ins, outs = refs[:n], refs[n:2 * n]
        send_sems, recv_sems = refs[2 * n:]
        x, y, c = lax.axis_index("x"), lax.axis_index("y"), lax.axis_index("c")
        sibling = (x, y, 1 - c)
        my_chip = 2 * x + y
        chips = [(1 - x, y), (x, 1 - y), (1 - x, 1 - y)]

        def copy(a, k, chip, layer, to, own=False):
            dst = outs[a].at[chip, layer]
            return pltpu.make_async_remote_copy(
                src_ref=ins[a].at[layer] if own else dst, dst_ref=dst,
                send_sem=send_sems.at[a, k], recv_sem=recv_sems.at[a, k], device_id=to, device_id_type=MESH)

        ids = [2 * px + py for px, py in chips]
        first = [copy(a, j, my_chip, c, (*chips[j], c), own=True) for j in range(3) for a in range(n)]
        for cp in first:
            cp.start()
        passed = []
        for j in range(3):
            for a in range(n):
                copy(a, j, ids[j], c, sibling).wait_recv()
                passed.append(copy(a, 3 + j, ids[j], c, sibling))
                passed[-1].start()
        for j in range(3):
            for a in range(n):
                copy(a, 3 + j, ids[j], 1 - c, sibling).wait_recv()
        for cp in first + passed:
            cp.wait_send()

    return _call(
        body, name="weight_all_gather",
        out_shape=[jax.ShapeDtypeStruct((N_CHIPS,) + s.shape, s.dtype) for s in shards],
        in_specs=[ANY] * n, out_specs=[ANY] * n,
        scratch_shapes=[pltpu.SemaphoreType.DMA((n, 6)), pltpu.SemaphoreType.DMA((n, 6))],
    )(*shards)


def _row_tile(r):
    return r if r <= 704 else 256


def _grad_swap(bigs, sp):
    n = len(bigs)

    def body(*refs):
        ins, outs = refs[:n + 1], refs[n + 1:2 * n + 2]
        send_sems, recv_sems = refs[2 * n + 2:]
        x, y, c = lax.axis_index("x"), lax.axis_index("y"), lax.axis_index("c")
        srcs = [ins[a].at[1 - c] for a in range(n)] + [ins[n].at[:, pl.ds((1 - c) * SP_HALF, SP_HALF), :]]
        copies = [pltpu.make_async_remote_copy(src_ref=srcs[a], dst_ref=outs[a], send_sem=send_sems.at[a],
                                               recv_sem=recv_sems.at[a], device_id=(x, y, 1 - c),
                                               device_id_type=MESH) for a in range(n + 1)]
        for cp in copies:
            cp.start()
        for cp in copies:
            cp.wait()

    shapes = [jax.ShapeDtypeStruct(b.shape[1:], b.dtype) for b in bigs]
    shapes.append(jax.ShapeDtypeStruct((N_CHIPS, SP_HALF, D_MODEL), sp.dtype))
    return _call(
        body, name="grad_swap_cores", out_shape=shapes, in_specs=[ANY] * (n + 1), out_specs=[ANY] * (n + 1),
        scratch_shapes=[pltpu.SemaphoreType.DMA((n + 1,)), pltpu.SemaphoreType.DMA((n + 1,))],
    )(*bigs, sp)


def _pair_add(g, r, c_arr, name, out_dtype):
    _, k, rr, cc = g.shape
    tr = _row_tile(rr)

    def body(c_ref, g_ref, r_ref, o_ref):
        o_ref[...] = (g_ref[...] + r_ref[...]).astype(o_ref.dtype)

    spec = pl.BlockSpec((1, tr, cc), lambda kk, i, c_ref: (kk, i, 0))
    grid_spec = pltpu.PrefetchScalarGridSpec(
        num_scalar_prefetch=1, grid=(k, rr // tr),
        in_specs=[pl.BlockSpec((None, 1, tr, cc), lambda kk, i, c_ref: (c_ref[0], kk, i, 0)), spec], out_specs=spec)
    return _call(body, name=name, grid_spec=grid_spec, out_shape=jax.ShapeDtypeStruct((k, rr, cc), out_dtype),
                 compiler_params=_params("parallel", "parallel"))(c_arr, g, r)


def _pair_add_small(sp, r, c_arr):
    def body(c_ref, g_ref, r_ref, o_ref):
        o_ref[...] = g_ref[...] + r_ref[...]

    spec = pl.BlockSpec((1, SP_HALF, D_MODEL), lambda kk, c_ref: (kk, 0, 0))
    grid_spec = pltpu.PrefetchScalarGridSpec(
        num_scalar_prefetch=1, grid=(N_CHIPS,),
        in_specs=[pl.BlockSpec((1, SP_HALF, D_MODEL), lambda kk, c_ref: (kk, c_ref[0], 0)), spec], out_specs=spec)
    return _call(body, name="grad_add_cores_small", grid_spec=grid_spec,
                 out_shape=jax.ShapeDtypeStruct(r.shape, F32), compiler_params=_params("parallel"))(c_arr, sp, r)


def _grad_exchange(hs):
    n = len(hs)

    def body(*refs):
        ins, outs = refs[:n], refs[n:2 * n]
        send_sems, recv_sems = refs[2 * n:]
        x, y, c = lax.axis_index("x"), lax.axis_index("y"), lax.axis_index("c")
        my_chip = 2 * x + y
        chips = [(1 - x, y), (x, 1 - y), (1 - x, 1 - y)]

        def copy(a, k, src_chip, dst_chip):
            px, py = chips[k]
            return pltpu.make_async_remote_copy(
                src_ref=ins[a].at[src_chip], dst_ref=outs[a].at[dst_chip], send_sem=send_sems.at[a, k],
                recv_sem=recv_sems.at[a, k], device_id=(px, py, c), device_id_type=MESH)

        sends = [copy(a, k, 2 * chips[k][0] + chips[k][1], my_chip) for k in range(3) for a in range(n)]
        for cp in sends:
            cp.start()
        for k in range(3):
            for a in range(n):
                copy(a, k, my_chip, 2 * chips[k][0] + chips[k][1]).wait_recv()
        for cp in sends:
            cp.wait_send()

    return _call(
        body, name="grad_exchange_chips", out_shape=[jax.ShapeDtypeStruct(h.shape, h.dtype) for h in hs],
        in_specs=[ANY] * n, out_specs=[ANY] * n,
        scratch_shapes=[pltpu.SemaphoreType.DMA((n, 3)), pltpu.SemaphoreType.DMA((n, 3))],
    )(*hs)


def _sum_chips(a, c_arr, name):
    _, r, cc = a.shape
    tr = _row_tile(r)

    def body(c_ref, a_ref, o_ref):
        o_ref[...] = ((a_ref[0].astype(F32) + a_ref[1].astype(F32)) + a_ref[2].astype(F32)) + a_ref[3].astype(F32)

    grid_spec = pltpu.PrefetchScalarGridSpec(
        num_scalar_prefetch=1, grid=(r // tr,),
        in_specs=[pl.BlockSpec((N_CHIPS, tr, cc), lambda i, c_ref: (0, i, 0))],
        out_specs=pl.BlockSpec((None, tr, cc), lambda i, c_ref: (c_ref[0], i, 0)))
    return _call(body, name=name, grid_spec=grid_spec, out_shape=jax.ShapeDtypeStruct((2, r, cc), F32),
                 compiler_params=_params("parallel"))(c_arr, a)


def _grad_share(bufs):
    n = len(bufs)

    def body(*refs):
        outs = refs[n:2 * n]
        send_sems, recv_sems = refs[2 * n:]
        x, y, c = lax.axis_index("x"), lax.axis_index("y"), lax.axis_index("c")
        copies = [pltpu.make_async_remote_copy(src_ref=outs[a].at[c], dst_ref=outs[a].at[c], send_sem=send_sems.at[a],
                                               recv_sem=recv_sems.at[a], device_id=(x, y, 1 - c),
                                               device_id_type=MESH) for a in range(n)]
        for cp in copies:
            cp.start()
        for a in range(n):
            pltpu.make_async_remote_copy(src_ref=outs[a].at[c], dst_ref=outs[a].at[1 - c], send_sem=send_sems.at[a],
                                         recv_sem=recv_sems.at[a], device_id=(x, y, 1 - c),
                                         device_id_type=MESH).wait_recv()
        for cp in copies:
            cp.wait_send()

    return _call(
        body, name="grad_share_cores", out_shape=[jax.ShapeDtypeStruct(b.shape, b.dtype) for b in bufs],
        in_specs=[ANY] * n, out_specs=[ANY] * n, input_output_aliases={a: a for a in range(n)},
        scratch_shapes=[pltpu.SemaphoreType.DMA((n,)), pltpu.SemaphoreType.DMA((n,))],
    )(*bufs)


def _adamw_math(g_ref, w_ref, m_ref, v_ref, d_ref, nm_ref, nv_ref):
    gv = g_ref[...]
    nm = ADAM_B1 * m_ref[...] + (1.0 - ADAM_B1) * gv
    nv = ADAM_B2 * v_ref[...] + (1.0 - ADAM_B2) * (gv * gv)
    m_hat = nm / (1.0 - ADAM_B1 ** ADAM_STEP)
    v_hat = nv / (1.0 - ADAM_B2 ** ADAM_STEP)
    d_ref[...] = -ADAM_LR * (m_hat / (jnp.sqrt(v_hat) + ADAM_EPS) + ADAM_WD * w_ref[...])
    nm_ref[...] = nm
    nv_ref[...] = nv


def _adamw_big(g, w, m, v, name):
    d, r, c = g.shape
    tr = r if r <= 704 else 256
    spec = pl.BlockSpec((1, tr, c), lambda l, i: (l, i, 0))

    def body(*refs):
        _adamw_math(*refs)

    shp = jax.ShapeDtypeStruct(g.shape, F32)
    return _call(body, name=name, grid=(d, r // tr), in_specs=[spec] * 4, out_specs=[spec] * 3,
                 out_shape=[shp, shp, shp], compiler_params=_params("parallel", "parallel"))(g, w, m, v)


def _adamw_small(gs, ws, ms, vs):
    n = len(gs)

    def body(*refs):
        ins, outs = refs[:4 * n], refs[4 * n:]
        for k in range(n):
            _adamw_math(ins[k], ins[n + k], ins[2 * n + k], ins[3 * n + k], outs[k], outs[n + k], outs[2 * n + k])

    shp = [jax.ShapeDtypeStruct(g.shape, F32) for g in gs]
    res = _call(body, name="adamw_small", out_shape=shp * 3)(*gs, *ws, *ms, *vs)
    return res[:n], res[n:2 * n], res[2 * n:]


def _rows(a, rows):
    flat = a.reshape(-1)
    return jnp.pad(flat, (0, rows * D_MODEL - flat.shape[0])).reshape(rows, D_MODEL)


def _small_rows(p, extra=None):
    parts = [p[n].reshape(-1) for n in SMALL_NAMES]
    if extra is not None:
        parts.append(extra.reshape(-1))
    flat = jnp.concatenate(parts)
    return jnp.pad(flat, (0, ROWS_SMALL * D_MODEL - flat.shape[0])).reshape(ROWS_SMALL, D_MODEL)


CONV_SHARD = (DEPTH, 3, 2 * D_FF // N_CHIPS)
N_CONV_SHARD = DEPTH * 3 * (2 * D_FF // N_CHIPS)


def _small_pack(g, loss):
    conv = jnp.transpose(g["conv_w"].reshape(DEPTH, 3, N_CHIPS, 2 * D_FF // N_CHIPS), (2, 0, 1, 3))
    conv = jnp.stack([_rows(conv[k], ROWS_CONV) for k in range(N_CHIPS)])
    small = jnp.broadcast_to(_small_rows(g, loss), (N_CHIPS, ROWS_SMALL, D_MODEL))
    return jnp.concatenate([conv, small], axis=1)


def _unpack_small(pack):
    out = {"conv_w": pack[:ROWS_CONV].reshape(-1)[:N_CONV_SHARD].reshape(CONV_SHARD)}
    flat = pack[ROWS_CONV:].reshape(-1)
    k = 0
    for name in SMALL_NAMES:
        shape = SMALL_SHAPES[name]
        n = 1
        for d in shape:
            n *= d
        out[name] = flat[k:k + n].reshape(shape)
        k += n
    out["extra"] = flat[k]
    return out


def _gather_weights(p):
    shards = [p[n].astype(jnp.bfloat16) for n in BIG_NAMES[:4]] + [p["conv_w"]]
    my_chip = 2 * lax.axis_index("x") + lax.axis_index("y")
    w_in, w_o, w_up, w_down, conv = [lax.dynamic_update_index_in_dim(got, own, my_chip, 0)
                                     for got, own in zip(_all_gather(shards), shards)]

    def by_cols(a):
        k, d, r, wd = a.shape
        return jnp.transpose(a, (1, 2, 0, 3)).reshape(d, r, k * wd)

    def by_rows(a):
        k, d, hgt, cc = a.shape
        return jnp.transpose(a, (1, 0, 2, 3)).reshape(d, k * hgt, cc)

    return dict(w_in=by_cols(w_in), w_o=by_rows(w_o), w_up=by_cols(w_up), w_down=by_rows(w_down), conv_w=by_cols(conv))


def _reduce_grads(grads, loss, c):
    bigs = [grads[n] for n in BIG_NAMES[:4]]
    sp = _small_pack(grads, loss)
    c_arr = jnp.reshape(c, (1,)).astype(jnp.int32)
    got = _grad_swap(bigs, sp)
    pair = [_pair_add(bigs[a], got[a], c_arr, "grad_add_cores_" + BIG_NAMES[a], ICI_DT) for a in range(4)]
    pair.append(_pair_add_small(sp, got[4], c_arr))
    my_chip = 2 * lax.axis_index("x") + lax.axis_index("y")
    parts = [lax.dynamic_update_index_in_dim(got_k, lax.dynamic_index_in_dim(own, my_chip, 0, keepdims=False),
                                             my_chip, 0) for got_k, own in zip(_grad_exchange(pair), pair)]
    total = [_sum_chips(parts[a], c_arr, "grad_sum_chips_" + (BIG_NAMES[:4] + ("small",))[a]) for a in range(5)]
    shared = _grad_share(total)
    out = dict(zip(BIG_NAMES[:4], shared[:4]))
    out.update(_unpack_small(shared[4].reshape(2 * SP_HALF, D_MODEL)))
    return out


def kernel(x, norm1_g, w_in, sgu_norm_g, sgu_w, sgu_b, pool_w, pool_scale, mix_norm_g, w_o, norm2_g, w_up, conv_w, conv_b, w_down, final_g, loss_target, m_norm1_g, m_w_in, m_sgu_norm_g, m_sgu_w, m_sgu_b, m_pool_w, m_pool_scale, m_mix_norm_g, m_w_o, m_norm2_g, m_w_up, m_conv_w, m_conv_b, m_w_down, m_final_g, v_norm1_g, v_w_in, v_sgu_norm_g, v_sgu_w, v_sgu_b, v_pool_w, v_pool_scale, v_mix_norm_g, v_w_o, v_norm2_g, v_w_up, v_conv_w, v_conv_b, v_w_down, v_final_g):
    names = ("norm1_g", "w_in", "sgu_norm_g", "sgu_w", "sgu_b", "pool_w", "pool_scale", "mix_norm_g", "w_o",
             "norm2_g", "w_up", "conv_w", "conv_b", "w_down", "final_g")
    p = dict(zip(names, (norm1_g, w_in, sgu_norm_g, sgu_w, sgu_b, pool_w, pool_scale, mix_norm_g, w_o, norm2_g,
                         w_up, conv_w, conv_b, w_down, final_g)))
    pm = dict(zip(names, (m_norm1_g, m_w_in, m_sgu_norm_g, m_sgu_w, m_sgu_b, m_pool_w, m_pool_scale, m_mix_norm_g,
                          m_w_o, m_norm2_g, m_w_up, m_conv_w, m_conv_b, m_w_down, m_final_g)))
    pv = dict(zip(names, (v_norm1_g, v_w_in, v_sgu_norm_g, v_sgu_w, v_sgu_b, v_pool_w, v_pool_scale, v_mix_norm_g,
                          v_w_o, v_norm2_g, v_w_up, v_conv_w, v_conv_b, v_w_down, v_final_g)))
    c = lax.axis_index("c")
    full = dict(p)
    full.update(_gather_weights(p))

    loss, dx, grads = _local_step(x[0], loss_target[0], full)

    g = _reduce_grads(grads, loss, c)
    d, nm, nv = {}, {}, {}
    for n in BIG_NAMES:
        d[n], nm[n], nv[n] = _adamw_big(g[n], p[n], pm[n], pv[n], "adamw_" + n)

    def two_d(a):
        return a.reshape(1, -1) if a.ndim == 1 else a

    ds, ms, vs = _adamw_small([two_d(g[n]) for n in SMALL_NAMES], [two_d(p[n]) for n in SMALL_NAMES],
                              [two_d(pm[n]) for n in SMALL_NAMES], [two_d(pv[n]) for n in SMALL_NAMES])
    for k, n in enumerate(SMALL_NAMES):
        d[n], nm[n], nv[n] = (a.reshape(p[n].shape) for a in (ds[k], ms[k], vs[k]))
    return (g["extra"], dx[None], *[g[n] for n in names], *[d[n] for n in names], *[nm[n] for n in names],
            *[nv[n] for n in names])
```

```python
import functools

import jax
import jax.numpy as jnp
from jax import lax
from jax.experimental import pallas as pl
from jax.experimental.pallas import tpu as pltpu

F32 = jnp.float32
MXU_DT = jnp.bfloat16

D_MODEL = 1024
DEPTH = 2
HEAD_DIM = 64
W_A = 256
W_B = 512
W_C = 256
IN_COLS = 2 * W_A + 3 * W_B + W_C
CHUNK = 128
POOL_WINDOWS = (2, 4, 8, 16)
D_FF = 2816
EPS = 1e-6
N_CHIPS = 4

ADAM_LR = 0.001
ADAM_B1 = 0.9
ADAM_B2 = 0.999
ADAM_EPS = 1e-08
ADAM_WD = 0.01
ADAM_STEP = 10

LANES = 128
TQ = 256
TK = 256
TM = 256
TM_MM = 512
HALO = 16
VMEM_LIMIT = 56 * 1024 * 1024

ROWS_CONV = 16
ROWS_SMALL = 240
SP_HALF = (ROWS_CONV + ROWS_SMALL) // 2
ICI_DT = jnp.bfloat16

BIG_NAMES = ("w_in", "w_o", "w_up", "w_down", "conv_w")
SMALL_NAMES = ("norm1_g", "sgu_norm_g", "sgu_w", "sgu_b", "pool_w", "pool_scale",
               "mix_norm_g", "norm2_g", "conv_b", "final_g")
SMALL_SHAPES = {
    "norm1_g": (DEPTH, D_MODEL), "sgu_norm_g": (DEPTH, W_A), "sgu_w": (DEPTH, 4, CHUNK, CHUNK),
    "sgu_b": (DEPTH, 4, CHUNK), "pool_w": (DEPTH, 4, 64, 64), "pool_scale": (DEPTH, W_C),
    "mix_norm_g": (DEPTH, D_MODEL), "norm2_g": (DEPTH, D_MODEL), "conv_b": (DEPTH, 2 * D_FF),
    "final_g": (D_MODEL,),
}


def _call(body, **kw):
    return pl.pallas_call(body, **kw)


def _params(*sem):
    return pltpu.CompilerParams(dimension_semantics=sem, vmem_limit_bytes=VMEM_LIMIT)


def _dot(a, b):
    return jnp.dot(a, b, preferred_element_type=F32)


def _dot_nt(a, b):
    return lax.dot_general(a, b, (((1,), (1,)), ((), ())), preferred_element_type=F32)


def _dot_tn(a, b):
    return lax.dot_general(a, b, (((0,), (0,)), ((), ())), preferred_element_type=F32)


def _group_mean(sq, gmat):
    sqb = sq.astype(MXU_DT)
    cols = [_dot(sqb[:, b * LANES:(b + 1) * LANES], gmat) for b in range(sq.shape[1] // LANES)]
    return cols[0] if len(cols) == 1 else jnp.concatenate(cols, axis=-1)


def _group_matrix():
    r = jnp.arange(LANES)
    return jnp.where((r[:, None] // HEAD_DIM) == (r[None, :] // HEAD_DIM), 1.0 / HEAD_DIM, 0.0).astype(MXU_DT)


def _tile(n):
    return max(t for t in range(LANES, 1536 + 1, LANES) if n % t == 0)


def _row_spec(tm, cols, col_block=0):
    return pl.BlockSpec((tm, cols), lambda i, cb=col_block: (i, cb))


def _full_spec(shape):
    nd = len(shape)
    return pl.BlockSpec(shape, lambda *_: (0,) * nd)


def _mm_res(a, w, res, name):
    S, K = a.shape
    N = w.shape[1]
    tm = TM_MM

    def body(a_ref, w_ref, r_ref, o_ref):
        o_ref[...] = r_ref[...] + _dot(a_ref[...], w_ref[...])

    return _call(
        body, name=name, grid=(S // tm,),
        in_specs=[_row_spec(tm, K), _full_spec((K, N)), _row_spec(tm, N)],
        out_specs=_row_spec(tm, N),
        out_shape=jax.ShapeDtypeStruct((S, N), F32),
        compiler_params=_params("parallel"),
    )(a, w, res)


def _mm_tn(a, b, name, col_tiles=False):
    S, K1 = a.shape
    N = b.shape[1]
    ts = TM_MM
    tk = _tile(K1)
    tn = _tile(N)
    if col_tiles:
        out_spec = pl.BlockSpec((None, tk, tn), lambda m, n, s: (n, m, 0))
        out_shape = jax.ShapeDtypeStruct((N // tn, K1, tn), F32)
    else:
        out_spec = pl.BlockSpec((tk, tn), lambda m, n, s: (m, n))
        out_shape = jax.ShapeDtypeStruct((K1, N), F32)

    def body(a_ref, b_ref, o_ref):
        @pl.when(pl.program_id(2) == 0)
        def _():
            o_ref[...] = jnp.zeros_like(o_ref)

        o_ref[...] += _dot_tn(a_ref[...], b_ref[...].astype(MXU_DT))

    return _call(
        body, name=name, grid=(K1 // tk, N // tn, S // ts),
        in_specs=[pl.BlockSpec((ts, tk), lambda m, n, s: (s, m)),
                  pl.BlockSpec((ts, tn), lambda m, n, s: (s, n))],
        out_specs=out_spec, out_shape=out_shape,
        compiler_params=_params("parallel", "parallel", "arbitrary"),
    )(a, b)


def _mm_nt_rmsbwd(pairs, x, g, dres, name):
    S, D = x.shape
    tm = TM
    n = len(pairs)

    def body(*refs):
        a_refs = refs[:n]
        w_refs = refs[n:2 * n]
        x_ref, g_ref, r_ref, dx_ref, dg_ref = refs[2 * n:]
        dh = _dot_nt(a_refs[0][...], w_refs[0][...])
        for k in range(1, n):
            dh += _dot_nt(a_refs[k][...], w_refs[k][...])
        xv = x_ref[...]
        r = lax.rsqrt(jnp.mean(xv * xv, axis=-1, keepdims=True) + EPS)
        xhat = xv * r

        @pl.when(pl.program_id(0) == 0)
        def _():
            dg_ref[...] = jnp.zeros_like(dg_ref)

        dg_ref[...] += jnp.sum(dh * xhat, axis=0, keepdims=True)
        dxh = dh * g_ref[...]
        dx_ref[...] = r_ref[...] + r * (dxh - xhat * jnp.mean(dxh * xhat, axis=-1, keepdims=True))

    in_specs = ([_row_spec(tm, a.shape[1]) for a, _ in pairs] + [_full_spec(w.shape) for _, w in pairs]
                + [_row_spec(tm, D), _full_spec((1, D)), _row_spec(tm, D)])
    return _call(
        body, name=name, grid=(S // tm,), in_specs=in_specs,
        out_specs=[_row_spec(tm, D), _full_spec((1, D))],
        out_shape=[jax.ShapeDtypeStruct((S, D), F32), jax.ShapeDtypeStruct((1, D), F32)],
        compiler_params=_params("arbitrary"),
    )(*[a for a, _ in pairs], *[w for _, w in pairs], x, g, dres)


def _down_proj_loss(a, w, res, g, tgt):
    S, D = res.shape
    K = a.shape[1]
    tm = TM

    def body(a_ref, w_ref, res_ref, g_ref, t_ref, dx_ref, dg_ref, l_ref):
        xv = res_ref[...] + _dot(a_ref[...], w_ref[...])
        r = lax.rsqrt(jnp.mean(xv * xv, axis=-1, keepdims=True) + EPS)
        xhat = xv * r
        diff = xhat * g_ref[...] - t_ref[...]

        @pl.when(pl.program_id(0) == 0)
        def _():
            dg_ref[...] = jnp.zeros_like(dg_ref)
            l_ref[...] = jnp.zeros_like(l_ref)

        l_ref[...] += jnp.full(l_ref.shape, 0.5 * jnp.sum(jnp.mean(diff * diff, axis=-1, keepdims=True)), F32)
        dout = diff * (1.0 / D)
        dg_ref[...] += jnp.sum(dout * xhat, axis=0, keepdims=True)
        dxh = dout * g_ref[...]
        dx_ref[...] = r * (dxh - xhat * jnp.mean(dxh * xhat, axis=-1, keepdims=True))

    return _call(
        body, name="down_proj_loss", grid=(S // tm,),
        in_specs=[_row_spec(tm, K), _full_spec((K, D)), _row_spec(tm, D), _full_spec((1, D)), _row_spec(tm, D)],
        out_specs=[_row_spec(tm, D), _full_spec((1, D)), _full_spec((8, LANES))],
        out_shape=[jax.ShapeDtypeStruct((S, D), F32), jax.ShapeDtypeStruct((1, D), F32),
                   jax.ShapeDtypeStruct((8, LANES), F32)],
        compiler_params=_params("arbitrary"),
    )(a, w, res, g, tgt)


def _mix_out(ya, yb, yc, gm, wo, x, gmat):
    S = x.shape[0]
    tm = TM

    def body(ya_ref, yb_ref, yc_ref, gm_ref, wo_ref, x_ref, gmat_ref, x2_ref, yn_ref):
        y = jnp.concatenate([ya_ref[...], yb_ref[...], yc_ref[...]], axis=-1)
        r = lax.rsqrt(_group_mean(y * y, gmat_ref[...]) + EPS)
        yn = (y * r * gm_ref[...]).astype(MXU_DT)
        yn_ref[...] = yn
        x2_ref[...] = x_ref[...] + _dot(yn, wo_ref[...])

    return _call(
        body, name="mix_out", grid=(S // tm,),
        in_specs=[_row_spec(tm, W_A), _row_spec(tm, W_B), _row_spec(tm, W_C), _full_spec((1, D_MODEL)),
                  _full_spec((D_MODEL, D_MODEL)), _row_spec(tm, D_MODEL), _full_spec((LANES, LANES))],
        out_specs=[_row_spec(tm, D_MODEL), _row_spec(tm, D_MODEL)],
        out_shape=[jax.ShapeDtypeStruct((S, D_MODEL), F32), jax.ShapeDtypeStruct((S, D_MODEL), MXU_DT)],
        compiler_params=_params("parallel"),
    )(ya, yb, yc, gm, wo, x, gmat)


def _mix_out_bwd(dx2, wo, ya, yb, yc, gm, gmat):
    S = dx2.shape[0]
    tm = TM

    def body(dx2_ref, wo_ref, ya_ref, yb_ref, yc_ref, gm_ref, gmat_ref, dya_ref, dyb_ref, dyc_ref, dgm_ref):
        dyn = _dot_nt(dx2_ref[...].astype(MXU_DT), wo_ref[...])
        y = jnp.concatenate([ya_ref[...], yb_ref[...], yc_ref[...]], axis=-1)
        r = lax.rsqrt(_group_mean(y * y, gmat_ref[...]) + EPS)
        yhat = y * r

        @pl.when(pl.program_id(0) == 0)
        def _():
            dgm_ref[...] = jnp.zeros_like(dgm_ref)

        dgm_ref[...] += jnp.sum(dyn * yhat, axis=0, keepdims=True)
        dyh = dyn * gm_ref[...]
        dy = r * (dyh - yhat * _group_mean(dyh * yhat, gmat_ref[...]))
        dya_ref[...] = dy[:, :W_A]
        dyb_ref[...] = dy[:, W_A:W_A + W_B]
        dyc_ref[...] = dy[:, W_A + W_B:]

    return _call(
        body, name="mix_out_bwd", grid=(S // tm,),
        in_specs=[_row_spec(tm, D_MODEL), _full_spec((D_MODEL, D_MODEL)), _row_spec(tm, W_A), _row_spec(tm, W_B),
                  _row_spec(tm, W_C), _full_spec((1, D_MODEL)), _full_spec((LANES, LANES))],
        out_specs=[_row_spec(tm, W_A), _row_spec(tm, W_B), _row_spec(tm, W_C), _full_spec((1, D_MODEL))],
        out_shape=[jax.ShapeDtypeStruct((S, W_A), F32), jax.ShapeDtypeStruct((S, W_B), F32),
                   jax.ShapeDtypeStruct((S, W_C), F32), jax.ShapeDtypeStruct((1, D_MODEL), F32)],
        compiler_params=_params("arbitrary"),
    )(dx2, wo, ya, yb, yc, gm, gmat)


_SQRT_HALF = 0.7071067811865476
_INV_SQRT_2PI = 0.3989422804014327


def _sgu_common(a, sng, wm_ref, bias, gmat):
    phi = 0.5 * (1.0 + lax.erf(a * _SQRT_HALF))
    ga = a * phi
    u = ga[:, :W_A]
    v = ga[:, W_A:]
    r = lax.rsqrt(_group_mean(v * v, gmat) + EPS)
    vhat = v * r
    vn = (vhat * sng).astype(MXU_DT)
    head = lax.broadcasted_iota(jnp.int32, (CHUNK, W_A), 1) // HEAD_DIM
    rows = []
    for c in range(a.shape[0] // CHUNK):
        vc = vn[c * CHUNK:(c + 1) * CHUNK]
        s = bias
        for h in range(4):
            s = s + jnp.where(head == h, _dot(wm_ref[h], vc), 0.0)
        rows.append(s)
    s = jnp.concatenate(rows, axis=0)
    return phi, u, r, vhat, vn, s


def _tril_weights(sgu_w_l):
    t = jnp.arange(CHUNK)
    return jnp.where((t[None, :] <= t[:, None])[None], sgu_w_l, 0.0)


def _sgu_bwd(proj, dy, sng, wm, wmt, bias, gmat):
    S = proj.shape[0]
    tm = TM

    def body(a_ref, dy_ref, sng_ref, wm_ref, wmt_ref, b_ref, gmat_ref, da_ref, dw_ref, db_ref, dsng_ref):
        a = a_ref[...]
        dy = dy_ref[...]
        gmat = gmat_ref[...]
        sng = sng_ref[...]
        phi, u, r, vhat, vn, s = _sgu_common(a, sng, wm_ref, b_ref[...], gmat)
        du = dy * s
        ds = dy * u

        @pl.when(pl.program_id(0) == 0)
        def _():
            dw_ref[...] = jnp.zeros_like(dw_ref)
            db_ref[...] = jnp.zeros_like(db_ref)
            dsng_ref[...] = jnp.zeros_like(dsng_ref)

        head = lax.broadcasted_iota(jnp.int32, (CHUNK, W_A), 1) // HEAD_DIM
        tt = lax.broadcasted_iota(jnp.int32, (CHUNK, CHUNK), 0)
        ss = lax.broadcasted_iota(jnp.int32, (CHUNK, CHUNK), 1)
        rows = []
        for c in range(tm // CHUNK):
            dsc = ds[c * CHUNK:(c + 1) * CHUNK]
            vc = vn[c * CHUNK:(c + 1) * CHUNK]
            db_ref[...] += dsc
            dsb = dsc.astype(MXU_DT)
            dvn = jnp.zeros((CHUNK, W_A), F32)
            for h in range(4):
                dvn = dvn + jnp.where(head == h, _dot(wmt_ref[h], dsb), 0.0)
                dsh = jnp.where(head == h, dsc, 0.0).astype(MXU_DT)
                dw_ref[h] += jnp.where(ss <= tt, _dot_nt(dsh, vc), 0.0)
            rows.append(dvn)
        dvn = jnp.concatenate(rows, axis=0)
        dsng_ref[...] += jnp.sum(dvn * vhat, axis=0, keepdims=True)
        dvh = dvn * sng
        dv = r * (dvh - vhat * _group_mean(dvh * vhat, gmat))
        dga = jnp.concatenate([du, dv], axis=-1)
        dgelu = phi + a * (_INV_SQRT_2PI * jnp.exp(-0.5 * a * a))
        da_ref[...] = (dga * dgelu).astype(da_ref.dtype)

    return _call(
        body, name="sgu_bwd", grid=(S // tm,),
        in_specs=[_row_spec(tm, 2 * W_A), _row_spec(tm, W_A), _full_spec((1, W_A)), _full_spec((4, CHUNK, CHUNK)),
                  _full_spec((4, CHUNK, CHUNK)), _full_spec((CHUNK, W_A)), _full_spec((LANES, LANES))],
        out_specs=[_row_spec(tm, 2 * W_A), _full_spec((4, CHUNK, CHUNK)), _full_spec((CHUNK, W_A)),
                   _full_spec((1, W_A))],
        out_shape=[jax.ShapeDtypeStruct((S, 2 * W_A), MXU_DT), jax.ShapeDtypeStruct((4, CHUNK, CHUNK), F32),
                   jax.ShapeDtypeStruct((CHUNK, W_A), F32), jax.ShapeDtypeStruct((1, W_A), F32)],
        compiler_params=_params("arbitrary"),
    )(proj, dy, sng, wm, wmt, bias, gmat)


HG = 4
LW = HG * HEAD_DIM
Q_BLK0 = (2 * W_A) // LW
K_BLK0 = Q_BLK0 + W_B // LW
V_BLK0 = K_BLK0 + W_B // LW
N_GROUPS = W_B // LW
EXP_IS_ZERO_BELOW = -120.0


def _tri_matrix():
    r = jnp.arange(TK)
    return (r[:, None] > r[None, :]).astype(MXU_DT)


def _stack_heads(a):
    head = lax.broadcasted_iota(jnp.int32, a.shape, 1) // HEAD_DIM
    return jnp.concatenate([jnp.where(head == h, a, 0.0) for h in range(HG)], axis=0).astype(MXU_DT)


def _unstack_heads(a):
    head = lax.broadcasted_iota(jnp.int32, (TQ, LW), 1) // HEAD_DIM
    out = a[:TQ]
    for h in range(1, HG):
        out = jnp.where(head == h, a[h * TQ:(h + 1) * TQ], out)
    return out


def _sb_scores(q2, kj, tri, key_offset):
    z = _dot_nt(q2, kj)
    sp = jnp.log(1.0 + jnp.exp(-jnp.abs(z)))
    lsp = jnp.minimum(z, 0.0) - sp
    lsm = lsp - z
    msk = None
    if key_offset is not None:
        row = lax.broadcasted_iota(jnp.int32, z.shape, 0) & (TQ - 1)
        col = lax.broadcasted_iota(jnp.int32, z.shape, 1) + key_offset
        msk = col < row
        lsm = jnp.where(msk, lsm, 0.0)
    tail = _dot(lsm.astype(MXU_DT), tri)
    return lsp, lsm, tail, msk


def _sb_fwd(proj_b, tri):
    S = proj_b.shape[0]
    nq = S // TQ
    kpq = TQ // TK
    assert S // TK < LANES

    def body(q_ref, k_ref, v_ref, tri_ref, o_ref, rb_ref, acc_ref):
        i = pl.program_id(1)
        lane2 = lax.broadcasted_iota(jnp.int32, (HG * TQ, LANES), 1)
        q2 = _stack_heads(q_ref[...].astype(F32) * (HEAD_DIM ** -0.5))
        tri = tri_ref[...]
        rb_ref[...] = jnp.zeros_like(rb_ref)

        def block(j, run, key_offset=None, first=False):
            start = pl.multiple_of(j * TK, TK)
            kj = k_ref[pl.ds(start, TK), :]
            vj = v_ref[pl.ds(start, TK), :]
            lsp, lsm, tail, msk = _sb_scores(q2, kj, tri, key_offset)
            rb_ref[...] = jnp.where(lane2 == j, run, rb_ref[...])
            att = jnp.exp(lsp + tail + run)
            if msk is not None:
                att = jnp.where(msk, att, 0.0)
            pv = _dot(att.astype(MXU_DT), vj)
            if first:
                acc_ref[...] = pv
            else:
                acc_ref[...] += pv
            return run + tail[:, :1] + lsm[:, :1]

        run = jnp.zeros((HG * TQ, 1), F32)
        for d in reversed(range(kpq)):
            run = block(i * kpq + d, run, key_offset=d * TK, first=(d == kpq - 1))
        past = i * kpq

        def alive(run):
            return (jnp.max(run) > EXP_IS_ZERO_BELOW).astype(jnp.int32)

        def step(carry):
            n, run, _ = carry
            run = block(past - 1 - n, run)
            return n + 1, run, alive(run)

        n, _, _ = lax.while_loop(lambda c: jnp.logical_and(c[0] < past, c[2] > 0), step,
                                 (jnp.int32(0), run, alive(run)))
        rb_ref[...] = jnp.where(lane2 == LANES - 1, n.astype(F32), rb_ref[...])
        o_ref[...] = _unstack_heads(acc_ref[...])

    once = pl.Buffered(1)
    return _call(
        body, name="sb_fwd", grid=(N_GROUPS, nq),
        in_specs=[pl.BlockSpec((TQ, LW), lambda p, i: (i, Q_BLK0 + p)),
                  pl.BlockSpec((S, LW), lambda p, i: (0, K_BLK0 + p), pipeline_mode=once),
                  pl.BlockSpec((S, LW), lambda p, i: (0, V_BLK0 + p), pipeline_mode=once),
                  pl.BlockSpec((TK, TK), lambda p, i: (0, 0))],
        out_specs=[pl.BlockSpec((TQ, LW), lambda p, i: (i, p)),
                   pl.BlockSpec((None, None, HG * TQ, LANES), lambda p, i: (p, i, 0, 0))],
        out_shape=[jax.ShapeDtypeStruct((S, W_B), F32), jax.ShapeDtypeStruct((N_GROUPS, nq, HG * TQ, LANES), F32)],
        scratch_shapes=[pltpu.VMEM((HG * TQ, LW), F32)],
        compiler_params=_params("parallel", "arbitrary"),
    )(proj_b, proj_b, proj_b, tri)


def _sb_bwd(proj_b, dyb, rb, tri, trit):
    S = proj_b.shape[0]
    nq = S // TQ
    kpq = TQ // TK

    def body(q_ref, k_ref, v_ref, do_ref, rb_ref, tri_ref, trit_ref, dq_ref, dk_acc, dv_acc, dq_acc):
        i = pl.program_id(1)
        lane2 = lax.broadcasted_iota(jnp.int32, (HG * TQ, LANES), 1)
        scale = HEAD_DIM ** -0.5
        q2 = _stack_heads(q_ref[...].astype(F32) * scale)
        do2 = _stack_heads(do_ref[...])
        tri = tri_ref[...]
        trit = trit_ref[...]

        @pl.when(i == 0)
        def _():
            dk_acc[...] = jnp.zeros_like(dk_acc)
            dv_acc[...] = jnp.zeros_like(dv_acc)

        dq_acc[...] = jnp.zeros_like(dq_acc)

        def block(j, pre, key_offset=None):
            start = pl.multiple_of(j * TK, TK)
            kj = k_ref[pl.ds(start, TK), :]
            vj = v_ref[pl.ds(start, TK), :]
            lsp, lsm, tail, msk = _sb_scores(q2, kj, tri, key_offset)
            run = jnp.sum(jnp.where(lane2 == j, rb_ref[...], 0.0), axis=-1, keepdims=True)
            att = jnp.exp(lsp + tail + run)
            if msk is not None:
                att = jnp.where(msk, att, 0.0)
            beta = jnp.exp(lsp)
            dl = _dot_nt(do2, vj) * att
            cin = _dot(dl.astype(MXU_DT), trit)
            dz = dl * (1.0 - beta) - beta * (pre + cin)
            if msk is not None:
                dz = jnp.where(msk, dz, 0.0)
            dzb = dz.astype(MXU_DT)
            dq_acc[...] += _dot(dzb, kj)
            dk_acc[pl.ds(start, TK), :] += _dot_tn(dzb, q2)
            dv_acc[pl.ds(start, TK), :] += _dot_tn(att.astype(MXU_DT), do2)
            return pre + cin[:, TK - 1:] + dl[:, TK - 1:]

        past = i * kpq
        walked = jnp.max(jnp.where(lane2[:8] == LANES - 1, rb_ref[pl.ds(0, 8), :], 0.0)).astype(jnp.int32)
        walked = jnp.clip(walked, 0, past)
        pre = lax.fori_loop(past - walked, past, lambda j, pre: block(j, pre), jnp.zeros((HG * TQ, 1), F32))
        for d in range(kpq):
            pre = block(i * kpq + d, pre, key_offset=d * TK)
        dq_ref[...] = (_unstack_heads(dq_acc[...]) * scale).astype(dq_ref.dtype)

    once = pl.Buffered(1)
    return _call(
        body, name="sb_bwd", grid=(N_GROUPS, nq),
        in_specs=[pl.BlockSpec((TQ, LW), lambda p, i: (i, Q_BLK0 + p)),
                  pl.BlockSpec((S, LW), lambda p, i: (0, K_BLK0 + p), pipeline_mode=once),
                  pl.BlockSpec((S, LW), lambda p, i: (0, V_BLK0 + p), pipeline_mode=once),
                  pl.BlockSpec((TQ, LW), lambda p, i: (i, p)),
                  pl.BlockSpec((None, None, HG * TQ, LANES), lambda p, i: (p, i, 0, 0)),
                  pl.BlockSpec((TK, TK), lambda p, i: (0, 0)),
                  pl.BlockSpec((TK, TK), lambda p, i: (0, 0))],
        out_specs=[pl.BlockSpec((TQ, LW), lambda p, i: (i, p)),
                   pl.BlockSpec((S, LW), lambda p, i: (0, p), pipeline_mode=once),
                   pl.BlockSpec((S, LW), lambda p, i: (0, p), pipeline_mode=once)],
        out_shape=[jax.ShapeDtypeStruct((S, W_B), MXU_DT), jax.ShapeDtypeStruct((S, W_B), F32),
                   jax.ShapeDtypeStruct((S, W_B), F32)],
        scratch_shapes=[pltpu.VMEM((HG * TQ, LW), F32)],
        compiler_params=_params("parallel", "arbitrary"),
    )(proj_b, proj_b, proj_b, dyb, rb, tri, trit)


P_BLK = (2 * W_A + 3 * W_B) // W_C


def _window_lanes():
    g = lax.broadcasted_iota(jnp.int32, (1, W_C), 1) // (W_C // 4)
    w = jnp.where(g == 0, POOL_WINDOWS[0], jnp.where(g == 1, POOL_WINDOWS[1],
                  jnp.where(g == 2, POOL_WINDOWS[2], POOL_WINDOWS[3])))
    return g, w


def _shift_rows(ext, k, tm, lead):
    n = ext.shape[0]
    return pltpu.roll(ext, shift=k % n, axis=0)[lead:lead + tm]


def _pool_diff(p_cur, p_halo, row0, tm):
    ext = jnp.concatenate([p_halo, p_cur], axis=0)
    g, w = _window_lanes()
    acc = ext
    sums = []
    for sh in (1, 2, 4, 8):
        acc = acc + pltpu.roll(acc, shift=sh, axis=0)
        sums.append(acc[HALO:HALO + tm])
    wsum = jnp.where(g == 0, sums[0], jnp.where(g == 1, sums[1], jnp.where(g == 2, sums[2], sums[3])))
    pos = (row0 + 1 + lax.broadcasted_iota(jnp.int32, (tm, W_C), 0)).astype(F32)
    cnt = jnp.minimum(pos, w.astype(F32))
    return wsum / cnt - p_cur, cnt


def _pool_specs(tm, nrow_blocks_halo):
    cur = pl.BlockSpec((tm, W_C), lambda i: (i, P_BLK))
    prev = pl.BlockSpec((HALO, W_C), lambda i: (jnp.maximum(i * (tm // HALO) - 1, 0), P_BLK))
    return cur, prev


def _in_proj_groups(x, g, w, sng, wm, bias, gmat, wbd, scale):
    S, D = x.shape
    tm = TM_MM
    p0 = 2 * W_A + 3 * W_B
    qkv_chunk = 3 * W_B // 2

    def body(x_ref, g_ref, w_ref, sng_ref, wm_ref, b_ref, gmat_ref, wbd_ref, sc_ref,
             h_ref, o_ref, ob_ref, ya_ref, yc_ref, tail_ref):
        i = pl.program_id(0)
        xv = x_ref[...]
        r = lax.rsqrt(jnp.mean(xv * xv, axis=-1, keepdims=True) + EPS)
        h = (xv * r * g_ref[...]).astype(h_ref.dtype)
        h_ref[...] = h

        def project(c0, c1):
            acc = _dot(h, w_ref[:, c0:c1])
            o_ref[:, c0:c1] = acc
            ob_ref[:, c0:c1] = acc.astype(ob_ref.dtype)
            return acc

        a = project(0, 2 * W_A)
        _, u, _, _, _, s = _sgu_common(a, sng_ref[...], wm_ref, b_ref[...], gmat_ref[...])
        ya_ref[...] = u * s
        for c0 in range(2 * W_A, p0, qkv_chunk):
            project(c0, c0 + qkv_chunk)
        p = project(p0, p0 + W_C)
        halo = jnp.where(i > 0, tail_ref[...], 0.0)
        tail_ref[...] = p[tm - HALO:]
        d, _ = _pool_diff(p, halo, i * tm, tm)
        yc_ref[...] = _dot(d.astype(MXU_DT), wbd_ref[...]) * sc_ref[...]

    return _call(
        body, name="in_proj_groups", grid=(S // tm,),
        in_specs=[_row_spec(tm, D), _full_spec((1, D)),
                  pl.BlockSpec((D, IN_COLS), lambda i: (0, 0), pipeline_mode=pl.Buffered(1)),
                  _full_spec((1, W_A)), _full_spec((4, CHUNK, CHUNK)), _full_spec((CHUNK, W_A)),
                  _full_spec((LANES, LANES)), _full_spec((W_C, W_C)), _full_spec((1, W_C))],
        out_specs=[_row_spec(tm, D), _row_spec(tm, IN_COLS), _row_spec(tm, IN_COLS), _row_spec(tm, W_A),
                   _row_spec(tm, W_C)],
        out_shape=[jax.ShapeDtypeStruct((S, D), MXU_DT), jax.ShapeDtypeStruct((S, IN_COLS), F32),
                   jax.ShapeDtypeStruct((S, IN_COLS), MXU_DT), jax.ShapeDtypeStruct((S, W_A), F32),
                   jax.ShapeDtypeStruct((S, W_C), F32)],
        scratch_shapes=[pltpu.VMEM((HALO, W_C), F32)],
        compiler_params=_params("arbitrary"),
    )(x, g, w, sng, wm, bias, gmat, wbd, scale)


def _pool_bwd_a(proj, dy, wbd, scale):
    S = proj.shape[0]
    tm = TM

    def body(p_ref, ph_ref, dy_ref, w_ref, sc_ref, dd_ref, e_ref, dw_ref, dsc_ref):
        i = pl.program_id(0)
        halo = jnp.where(i > 0, ph_ref[...], 0.0)
        d, cnt = _pool_diff(p_ref[...], halo, i * tm, tm)
        db = d.astype(MXU_DT)
        dy = dy_ref[...]

        @pl.when(i == 0)
        def _():
            dw_ref[...] = jnp.zeros_like(dw_ref)
            dsc_ref[...] = jnp.zeros_like(dsc_ref)

        dsc_ref[...] += jnp.sum(dy * _dot(db, w_ref[...]), axis=0, keepdims=True)
        dys = (dy * sc_ref[...]).astype(MXU_DT)
        dw_ref[...] += _dot_tn(db, dys)
        dd = _dot_nt(dys, w_ref[...])
        dd_ref[...] = dd
        e_ref[...] = dd / cnt

    cur, prev = _pool_specs(tm, S // HALO)
    return _call(
        body, name="pool_bwd_a", grid=(S // tm,),
        in_specs=[cur, prev, _row_spec(tm, W_C), _full_spec((W_C, W_C)), _full_spec((1, W_C))],
        out_specs=[_row_spec(tm, W_C), _row_spec(tm, W_C), _full_spec((W_C, W_C)), _full_spec((1, W_C))],
        out_shape=[jax.ShapeDtypeStruct((S, W_C), F32), jax.ShapeDtypeStruct((S, W_C), F32),
                   jax.ShapeDtypeStruct((W_C, W_C), F32), jax.ShapeDtypeStruct((1, W_C), F32)],
        compiler_params=_params("arbitrary"),
    )(proj, proj, dy, wbd, scale)


def _pool_bwd_b(dd, e):
    S = dd.shape[0]
    tm = TM
    nb = S // tm

    def body(dd_ref, e_ref, en_ref, dp_ref):
        i = pl.program_id(0)
        halo = jnp.where(i < nb - 1, en_ref[...], 0.0)
        ext = jnp.concatenate([e_ref[...], halo], axis=0)
        n = ext.shape[0]
        g, _ = _window_lanes()
        acc = ext
        sums = []
        for sh in (1, 2, 4, 8):
            acc = acc + pltpu.roll(acc, shift=n - sh, axis=0)
            sums.append(acc[:tm])
        wsum = jnp.where(g == 0, sums[0], jnp.where(g == 1, sums[1], jnp.where(g == 2, sums[2], sums[3])))
        dp_ref[...] = (wsum - dd_ref[...]).astype(dp_ref.dtype)

    nxt = pl.BlockSpec((HALO, W_C), lambda i: (jnp.minimum((i + 1) * (tm // HALO), S // HALO - 1), 0))
    return _call(
        body, name="pool_bwd_b", grid=(nb,),
        in_specs=[_row_spec(tm, W_C), _row_spec(tm, W_C), nxt],
        out_specs=_row_spec(tm, W_C),
        out_shape=jax.ShapeDtypeStruct((S, W_C), MXU_DT),
        compiler_params=_params("parallel"),
    )(dd, e, e)


TN_FF = 1408
NB_FF = D_FF // TN_FF
CONV_ROWS = 8


def _conv(z_cur, z_halo, cwb, tm):
    ext = jnp.concatenate([z_halo, z_cur], axis=0)
    z2 = _shift_rows(ext, 2, tm, HALO)
    z1 = _shift_rows(ext, 1, tm, HALO)
    zc = cwb[3:4] + z2 * cwb[0:1] + z1 * cwb[1:2] + z_cur * cwb[2:3]
    return zc, z2, z1


def _up_proj_gate(x, g, w, cwb):
    S, D = x.shape
    tm = TM

    def body(x_ref, g_ref, w_ref, c_ref, h_ref, z_ref, zc_ref, f_ref, tail_ref):
        first = pl.program_id(0) == 0
        xv = x_ref[...]
        r = lax.rsqrt(jnp.mean(xv * xv, axis=-1, keepdims=True) + EPS)
        h = (xv * r * g_ref[...]).astype(h_ref.dtype)
        h_ref[...] = h
        for j in range(NB_FF):
            halves = []
            for col0 in (j * TN_FF, D_FF + j * TN_FF):
                zb = _dot(h, w_ref[:, col0:col0 + TN_FF]).astype(z_ref.dtype)
                z_ref[:, col0:col0 + TN_FF] = zb
                zf = zb.astype(F32)
                prev = jnp.where(first, 0.0, tail_ref[:, col0:col0 + TN_FF])
                tail_ref[:, col0:col0 + TN_FF] = zf[tm - HALO:]
                zc = _conv(zf, prev, c_ref[:, col0:col0 + TN_FF], tm)[0]
                zc_ref[:, col0:col0 + TN_FF] = zc.astype(zc_ref.dtype)
                halves.append(zc)
            gate, value = halves
            f_ref[:, j * TN_FF:(j + 1) * TN_FF] = (gate * jax.nn.sigmoid(gate) * value).astype(f_ref.dtype)

    return _call(
        body, name="up_proj_gate", grid=(S // tm,),
        in_specs=[_row_spec(tm, D), _full_spec((1, D)),
                  pl.BlockSpec((D, 2 * D_FF), lambda i: (0, 0), pipeline_mode=pl.Buffered(1)),
                  _full_spec((CONV_ROWS, 2 * D_FF))],
        out_specs=[_row_spec(tm, D), _row_spec(tm, 2 * D_FF), _row_spec(tm, 2 * D_FF), _row_spec(tm, D_FF)],
        out_shape=[jax.ShapeDtypeStruct((S, D), MXU_DT), jax.ShapeDtypeStruct((S, 2 * D_FF), MXU_DT),
                   jax.ShapeDtypeStruct((S, 2 * D_FF), MXU_DT), jax.ShapeDtypeStruct((S, D_FF), MXU_DT)],
        scratch_shapes=[pltpu.VMEM((HALO, 2 * D_FF), F32)],
        compiler_params=_params("arbitrary"),
    )(x, g, w, cwb)


def _gate_up_bwd(z, zc, cwb, w, wd, x, g, dres):
    S, D = x.shape
    tm = TM
    nb = S // tm

    def body(z_ref, zc_ref, zcn_ref, c_ref, w_ref, wd_ref, x_ref, g_ref, r_ref, rn_ref,
             dz_ref, dc_ref, dx_ref, dg_ref):
        i = pl.program_id(0)
        first = i == 0
        last = i == nb - 1
        dxe = jnp.concatenate([r_ref[...], jnp.where(last, 0.0, rn_ref[...])], axis=0).astype(MXU_DT)

        @pl.when(first)
        def _():
            dc_ref[...] = jnp.zeros_like(dc_ref)
            dg_ref[...] = jnp.zeros_like(dg_ref)

        rid = lax.broadcasted_iota(jnp.int32, (CONV_ROWS, TN_FF), 0)

        def conv_out(cols):
            return jnp.concatenate([zc_ref[:, cols].astype(F32), zcn_ref[:, cols].astype(F32)], axis=0)

        def conv_bwd(d, z0, c):
            d0 = d[:tm]
            d1 = _shift_rows(d, -1, tm, 0)
            d2 = _shift_rows(d, -2, tm, 0)
            sums = [jnp.sum(d2 * z0, axis=0, keepdims=True), jnp.sum(d1 * z0, axis=0, keepdims=True),
                    jnp.sum(d0 * z0, axis=0, keepdims=True), jnp.sum(d0, axis=0, keepdims=True)]
            dtaps = jnp.zeros((CONV_ROWS, TN_FF), F32)
            for k, v in enumerate(sums):
                dtaps = jnp.where(rid == k, v, dtaps)
            return d0 * c[2:3] + d1 * c[1:2] + d2 * c[0:1], dtaps

        dh = jnp.zeros((tm, D), F32)
        for j in range(NB_FF):
            gc = slice(j * TN_FF, (j + 1) * TN_FF)
            uc = slice(D_FF + j * TN_FF, D_FF + (j + 1) * TN_FF)
            gt = conv_out(gc)
            ut = conv_out(uc)
            df = _dot_nt(dxe, wd_ref[gc, :])
            sg = jax.nn.sigmoid(gt)
            dzg, dtg = conv_bwd(df * ut * (sg * (1.0 + gt * (1.0 - sg))), z_ref[:, gc].astype(F32), c_ref[:, gc])
            dzu, dtu = conv_bwd(df * (gt * sg), z_ref[:, uc].astype(F32), c_ref[:, uc])
            dzg = dzg.astype(dz_ref.dtype)
            dzu = dzu.astype(dz_ref.dtype)
            dz_ref[:, gc] = dzg
            dz_ref[:, uc] = dzu
            dc_ref[:, gc] += dtg
            dc_ref[:, uc] += dtu
            dh += _dot_nt(dzg, w_ref[:, gc]) + _dot_nt(dzu, w_ref[:, uc])

        xv = x_ref[...]
        r = lax.rsqrt(jnp.mean(xv * xv, axis=-1, keepdims=True) + EPS)
        xhat = xv * r
        dg_ref[...] += jnp.sum(dh * xhat, axis=0, keepdims=True)
        dxh = dh * g_ref[...]
        dx_ref[...] = r_ref[...] + r * (dxh - xhat * jnp.mean(dxh * xhat, axis=-1, keepdims=True))

    hb = tm // HALO
    last_halo = S // HALO - 1
    return _call(
        body, name="gate_up_bwd", grid=(nb,),
        in_specs=[_row_spec(tm, 2 * D_FF), _row_spec(tm, 2 * D_FF),
                  pl.BlockSpec((HALO, 2 * D_FF), lambda i: (jnp.minimum((i + 1) * hb, last_halo), 0)),
                  _full_spec((CONV_ROWS, 2 * D_FF)),
                  pl.BlockSpec((D, 2 * D_FF), lambda i: (0, 0), pipeline_mode=pl.Buffered(1)),
                  pl.BlockSpec((D_FF, D), lambda i: (0, 0), pipeline_mode=pl.Buffered(1)),
                  _row_spec(tm, D), _full_spec((1, D)), _row_spec(tm, D),
                  pl.BlockSpec((HALO, D), lambda i: (jnp.minimum((i + 1) * hb, last_halo), 0))],
        out_specs=[_row_spec(tm, 2 * D_FF), _full_spec((CONV_ROWS, 2 * D_FF)), _row_spec(tm, D), _full_spec((1, D))],
        out_shape=[jax.ShapeDtypeStruct((S, 2 * D_FF), MXU_DT), jax.ShapeDtypeStruct((CONV_ROWS, 2 * D_FF), F32),
                   jax.ShapeDtypeStruct((S, D), F32), jax.ShapeDtypeStruct((1, D), F32)],
        compiler_params=_params("arbitrary"),
    )(z, zc, zc, cwb, w, wd, x, g, dres, dres)


def _layer_consts(w, l):
    wm = _tril_weights(w["sgu_w"][l])
    eye = jnp.eye(4, dtype=F32)
    wbd = (w["pool_w"][l][:, :, None, :] * eye[:, None, :, None]).reshape(W_C, W_C)
    cwb = jnp.concatenate([w["conv_w"][l], w["conv_b"][l][None], jnp.zeros((CONV_ROWS - 4, 2 * D_FF), F32)], axis=0)
    return dict(
        g1=w["norm1_g"][l][None], g2=w["norm2_g"][l][None], gm=w["mix_norm_g"][l][None],
        sng=w["sgu_norm_g"][l][None], wm=wm.astype(MXU_DT), wmt=jnp.swapaxes(wm, 1, 2).astype(MXU_DT),
        bias=jnp.repeat(jnp.transpose(w["sgu_b"][l]), HEAD_DIM, axis=1),
        wbd=wbd.astype(MXU_DT), scale=w["pool_scale"][l][None], cwb=cwb,
        w_in=w["w_in"][l], w_o=w["w_o"][l], w_up=w["w_up"][l], w_down=w["w_down"][l],
    )


def _local_step(x, tgt, w):
    gmat = _group_matrix()
    tri = _tri_matrix()
    trit = jnp.transpose(tri)
    saved = []
    for l in range(DEPTH):
        c = _layer_consts(w, l)
        h1, proj, proj_b, ya, yc = _in_proj_groups(x, c["g1"], c["w_in"], c["sng"], c["wm"], c["bias"], gmat,
                                                   c["wbd"], c["scale"])
        yb, rb = _sb_fwd(proj_b, tri)
        x2, yn = _mix_out(ya, yb, yc, c["gm"], c["w_o"], x, gmat)
        h2, z, zc, f = _up_proj_gate(x2, c["g2"], c["w_up"], c["cwb"])
        saved.append(dict(c=c, x=x, proj=proj, proj_b=proj_b, h1=h1, ya=ya, yb=yb, yc=yc, rb=rb, x2=x2, yn=yn,
                          z=z, zc=zc, h2=h2, f=f))
        if l < DEPTH - 1:
            x = _mm_res(f, c["w_down"], x2, "down_proj")

    last = saved[-1]
    dx, d_final_g, loss8 = _down_proj_loss(last["f"], last["c"]["w_down"], last["x2"], w["final_g"][None], tgt)
    grads = {n: [None] * DEPTH for n in ("norm1_g", "w_in", "sgu_norm_g", "sgu_w", "sgu_b", "pool_w", "pool_scale",
                                         "mix_norm_g", "w_o", "norm2_g", "w_up", "conv_w", "conv_b", "w_down")}
    for l in reversed(range(DEPTH)):
        s = saved[l]
        c = s["c"]
        grads["w_down"][l] = _mm_tn(s["f"], dx, "down_proj_wgrad").reshape(N_CHIPS, D_FF // N_CHIPS, D_MODEL)
        dz, dcwb, dx2, dg2 = _gate_up_bwd(s["z"], s["zc"], c["cwb"], c["w_up"], c["w_down"], s["x2"], c["g2"], dx)
        grads["conv_w"][l] = dcwb[:3]
        grads["conv_b"][l] = dcwb[3]
        grads["w_up"][l] = _mm_tn(s["h2"], dz, "up_proj_wgrad", col_tiles=True)
        grads["norm2_g"][l] = dg2[0]
        grads["w_o"][l] = _mm_tn(s["yn"], dx2, "out_proj_wgrad").reshape(N_CHIPS, D_MODEL // N_CHIPS, D_MODEL)
        dya, dyb, dyc, dgm = _mix_out_bwd(dx2, c["w_o"], s["ya"], s["yb"], s["yc"], c["gm"], gmat)
        grads["mix_norm_g"][l] = dgm[0]
        dd, e, dwbd, dscale = _pool_bwd_a(s["proj"], dyc, c["wbd"], c["scale"])
        dp = _pool_bwd_b(dd, e)
        grads["pool_w"][l] = jnp.stack([dwbd[g * 64:(g + 1) * 64, g * 64:(g + 1) * 64] for g in range(4)])
        grads["pool_scale"][l] = dscale[0]
        dq, dk, dv = _sb_bwd(s["proj_b"], dyb, s["rb"], tri, trit)
        da, dwm, dbias, dsng = _sgu_bwd(s["proj"], dya, c["sng"], c["wm"], c["wmt"], c["bias"], gmat)
        grads["sgu_w"][l] = dwm
        grads["sgu_b"][l] = jnp.transpose(jnp.sum(dbias.reshape(CHUNK, 4, HEAD_DIM), axis=-1))
        grads["sgu_norm_g"][l] = dsng[0]
        dproj = jnp.concatenate([da, dq, dk.astype(MXU_DT), dv.astype(MXU_DT), dp], axis=1)
        dw_in = _mm_tn(s["h1"], dproj, "in_proj_wgrad")
        grads["w_in"][l] = jnp.transpose(dw_in.reshape(D_MODEL, N_CHIPS, IN_COLS // N_CHIPS), (1, 0, 2))
        dx, dg1 = _mm_nt_rmsbwd([(dproj, c["w_in"])], s["x"], c["g1"], dx2, "in_proj_bwd")
        grads["norm1_g"][l] = dg1[0]

    out = {n: jnp.stack(v) for n, v in grads.items()}
    out["final_g"] = d_final_g[0]
    return loss8[0, 0], dx, out


MESH = pl.DeviceIdType.MESH
ANY = pl.BlockSpec(memory_space=pl.ANY)


def _all_gather(shards):
    n = len(shards)

    def body(*refs):
        ins, outs = refs[:n], refs[n:2 * n]
        send_sems, recv_sems = refs[2 * n:]
        x, y, c = lax.axis_index("x"), lax.axis_index("y"), lax.axis_index("c")
        sibling = (x, y, 1 - c)
        my_chip = 2 * x + y
        chips = [(1 - x, y), (x, 1 - y), (1 - x, 1 - y)]

        def copy(a, k, chip, layer, to, own=False):
            dst = outs[a].at[chip, layer]
            return pltpu.make_async_remote_copy(
                src_ref=ins[a].at[layer] if own else dst, dst_ref=dst,
                send_sem=send_sems.at[a, k], recv_sem=recv_sems.at[a, k], device_id=to, device_id_type=MESH)

        ids = [2 * px + py for px, py in chips]
        first = [copy(a, j, my_chip, c, (*chips[j], c), own=True) for j in range(3) for a in range(n)]
        for cp in first:
            cp.start()
        passed = []
        for j in range(3):
            for a in range(n):
                copy(a, j, ids[j], c, sibling).wait_recv()
                passed.append(copy(a, 3 + j, ids[j], c, sibling))
                passed[-1].start()
        for j in range(3):
            for a in range(n):
                copy(a, 3 + j, ids[j], 1 - c, sibling).wait_recv()
        for cp in first + passed:
            cp.wait_send()

    return _call(
        body, name="weight_all_gather",
        out_shape=[jax.ShapeDtypeStruct((N_CHIPS,) + s.shape, s.dtype) for s in shards],
        in_specs=[ANY] * n, out_specs=[ANY] * n,
        scratch_shapes=[pltpu.SemaphoreType.DMA((n, 6)), pltpu.SemaphoreType.DMA((n, 6))],
    )(*shards)


def _row_tile(r):
    return r if r <= 704 else 256


def _grad_swap(bigs, sp):
    n = len(bigs)

    def body(*refs):
        ins, outs = refs[:n + 1], refs[n + 1:2 * n + 2]
        send_sems, recv_sems = refs[2 * n + 2:]
        x, y, c = lax.axis_index("x"), lax.axis_index("y"), lax.axis_index("c")
        srcs = [ins[a].at[1 - c] for a in range(n)] + [ins[n].at[:, pl.ds((1 - c) * SP_HALF, SP_HALF), :]]
        copies = [pltpu.make_async_remote_copy(src_ref=srcs[a], dst_ref=outs[a], send_sem=send_sems.at[a],
                                               recv_sem=recv_sems.at[a], device_id=(x, y, 1 - c),
                                               device_id_type=MESH) for a in range(n + 1)]
        for cp in copies:
            cp.start()
        for cp in copies:
            cp.wait()

    shapes = [jax.ShapeDtypeStruct(b.shape[1:], b.dtype) for b in bigs]
    shapes.append(jax.ShapeDtypeStruct((N_CHIPS, SP_HALF, D_MODEL), sp.dtype))
    return _call(
        body, name="grad_swap_cores", out_shape=shapes, in_specs=[ANY] * (n + 1), out_specs=[ANY] * (n + 1),
        scratch_shapes=[pltpu.SemaphoreType.DMA((n + 1,)), pltpu.SemaphoreType.DMA((n + 1,))],
    )(*bigs, sp)


def _pair_add(g, r, c_arr, name, out_dtype):
    _, k, rr, cc = g.shape
    tr = _row_tile(rr)

    def body(c_ref, g_ref, r_ref, o_ref):
        o_ref[...] = (g_ref[...] + r_ref[...]).astype(o_ref.dtype)

    spec = pl.BlockSpec((1, tr, cc), lambda kk, i, c_ref: (kk, i, 0))
    grid_spec = pltpu.PrefetchScalarGridSpec(
        num_scalar_prefetch=1, grid=(k, rr // tr),
        in_specs=[pl.BlockSpec((None, 1, tr, cc), lambda kk, i, c_ref: (c_ref[0], kk, i, 0)), spec], out_specs=spec)
    return _call(body, name=name, grid_spec=grid_spec, out_shape=jax.ShapeDtypeStruct((k, rr, cc), out_dtype),
                 compiler_params=_params("parallel", "parallel"))(c_arr, g, r)


def _pair_add_small(sp, r, c_arr):
    def body(c_ref, g_ref, r_ref, o_ref):
        o_ref[...] = g_ref[...] + r_ref[...]

    spec = pl.BlockSpec((1, SP_HALF, D_MODEL), lambda kk, c_ref: (kk, 0, 0))
    grid_spec = pltpu.PrefetchScalarGridSpec(
        num_scalar_prefetch=1, grid=(N_CHIPS,),
        in_specs=[pl.BlockSpec((1, SP_HALF, D_MODEL), lambda kk, c_ref: (kk, c_ref[0], 0)), spec], out_specs=spec)
    return _call(body, name="grad_add_cores_small", grid_spec=grid_spec,
                 out_shape=jax.ShapeDtypeStruct(r.shape, F32), compiler_params=_params("parallel"))(c_arr, sp, r)


def _grad_exchange(hs):
    n = len(hs)

    def body(*refs):
        ins, outs = refs[:n], refs[n:2 * n]
        send_sems, recv_sems = refs[2 * n:]
        x, y, c = lax.axis_index("x"), lax.axis_index("y"), lax.axis_index("c")
        my_chip = 2 * x + y
        chips = [(1 - x, y), (x, 1 - y), (1 - x, 1 - y)]

        def copy(a, k, src_chip, dst_chip):
            px, py = chips[k]
            return pltpu.make_async_remote_copy(
                src_ref=ins[a].at[src_chip], dst_ref=outs[a].at[dst_chip], send_sem=send_sems.at[a, k],
                recv_sem=recv_sems.at[a, k], device_id=(px, py, c), device_id_type=MESH)

        sends = [copy(a, k, 2 * chips[k][0] + chips[k][1], my_chip) for k in range(3) for a in range(n)]
        for cp in sends:
            cp.start()
        for k in range(3):
            for a in range(n):
                copy(a, k, my_chip, 2 * chips[k][0] + chips[k][1]).wait_recv()
        for cp in sends:
            cp.wait_send()

    return _call(
        body, name="grad_exchange_chips", out_shape=[jax.ShapeDtypeStruct(h.shape, h.dtype) for h in hs],
        in_specs=[ANY] * n, out_specs=[ANY] * n,
        scratch_shapes=[pltpu.SemaphoreType.DMA((n, 3)), pltpu.SemaphoreType.DMA((n, 3))],
    )(*hs)


def _sum_chips(a, c_arr, name):
    _, r, cc = a.shape
    tr = _row_tile(r)

    def body(c_ref, a_ref, o_ref):
        o_ref[...] = ((a_ref[0].astype(F32) + a_ref[1].astype(F32)) + a_ref[2].astype(F32)) + a_ref[3].astype(F32)

    grid_spec = pltpu.PrefetchScalarGridSpec(
        num_scalar_prefetch=1, grid=(r // tr,),
        in_specs=[pl.BlockSpec((N_CHIPS, tr, cc), lambda i, c_ref: (0, i, 0))],
        out_specs=pl.BlockSpec((None, tr, cc), lambda i, c_ref: (c_ref[0], i, 0)))
    return _call(body, name=name, grid_spec=grid_spec, out_shape=jax.ShapeDtypeStruct((2, r, cc), F32),
                 compiler_params=_params("parallel"))(c_arr, a)


def _grad_share(bufs):
    n = len(bufs)

    def body(*refs):
        outs = refs[n:2 * n]
        send_sems, recv_sems = refs[2 * n:]
        x, y, c = lax.axis_index("x"), lax.axis_index("y"), lax.axis_index("c")
        copies = [pltpu.make_async_remote_copy(src_ref=outs[a].at[c], dst_ref=outs[a].at[c], send_sem=send_sems.at[a],
                                               recv_sem=recv_sems.at[a], device_id=(x, y, 1 - c),
                                               device_id_type=MESH) for a in range(n)]
        for cp in copies:
            cp.start()
        for a in range(n):
            pltpu.make_async_remote_copy(src_ref=outs[a].at[c], dst_ref=outs[a].at[1 - c], send_sem=send_sems.at[a],
                                         recv_sem=recv_sems.at[a], device_id=(x, y, 1 - c),
                                         device_id_type=MESH).wait_recv()
        for cp in copies:
            cp.wait_send()

    return _call(
        body, name="grad_share_cores", out_shape=[jax.ShapeDtypeStruct(b.shape, b.dtype) for b in bufs],
        in_specs=[ANY] * n, out_specs=[ANY] * n, input_output_aliases={a: a for a in range(n)},
        scratch_shapes=[pltpu.SemaphoreType.DMA((n,)), pltpu.SemaphoreType.DMA((n,))],
    )(*bufs)


def _adamw_math(g_ref, w_ref, m_ref, v_ref, d_ref, nm_ref, nv_ref):
    gv = g_ref[...]
    nm = ADAM_B1 * m_ref[...] + (1.0 - ADAM_B1) * gv
    nv = ADAM_B2 * v_ref[...] + (1.0 - ADAM_B2) * (gv * gv)
    m_hat = nm / (1.0 - ADAM_B1 ** ADAM_STEP)
    v_hat = nv / (1.0 - ADAM_B2 ** ADAM_STEP)
    d_ref[...] = -ADAM_LR * (m_hat / (jnp.sqrt(v_hat) + ADAM_EPS) + ADAM_WD * w_ref[...])
    nm_ref[...] = nm
    nv_ref[...] = nv


def _adamw_big(g, w, m, v, name):
    d, r, c = g.shape
    tr = r if r <= 704 else 256
    spec = pl.BlockSpec((1, tr, c), lambda l, i: (l, i, 0))

    def body(*refs):
        _adamw_math(*refs)

    shp = jax.ShapeDtypeStruct(g.shape, F32)
    return _call(body, name=name, grid=(d, r // tr), in_specs=[spec] * 4, out_specs=[spec] * 3,
                 out_shape=[shp, shp, shp], compiler_params=_params("parallel", "parallel"))(g, w, m, v)


def _adamw_small(gs, ws, ms, vs):
    n = len(gs)

    def body(*refs):
        ins, outs = refs[:4 * n], refs[4 * n:]
        for k in range(n):
            _adamw_math(ins[k], ins[n + k], ins[2 * n + k], ins[3 * n + k], outs[k], outs[n + k], outs[2 * n + k])

    shp = [jax.ShapeDtypeStruct(g.shape, F32) for g in gs]
    res = _call(body, name="adamw_small", out_shape=shp * 3)(*gs, *ws, *ms, *vs)
    return res[:n], res[n:2 * n], res[2 * n:]


def _rows(a, rows):
    flat = a.reshape(-1)
    return jnp.pad(flat, (0, rows * D_MODEL - flat.shape[0])).reshape(rows, D_MODEL)


def _small_rows(p, extra=None):
    parts = [p[n].reshape(-1) for n in SMALL_NAMES]
    if extra is not None:
        parts.append(extra.reshape(-1))
    flat = jnp.concatenate(parts)
    return jnp.pad(flat, (0, ROWS_SMALL * D_MODEL - flat.shape[0])).reshape(ROWS_SMALL, D_MODEL)


CONV_SHARD = (DEPTH, 3, 2 * D_FF // N_CHIPS)
N_CONV_SHARD = DEPTH * 3 * (2 * D_FF // N_CHIPS)


def _small_pack(g, loss):
    conv = jnp.transpose(g["conv_w"].reshape(DEPTH, 3, N_CHIPS, 2 * D_FF // N_CHIPS), (2, 0, 1, 3))
    conv = jnp.stack([_rows(conv[k], ROWS_CONV) for k in range(N_CHIPS)])
    small = jnp.broadcast_to(_small_rows(g, loss), (N_CHIPS, ROWS_SMALL, D_MODEL))
    return jnp.concatenate([conv, small], axis=1)


def _unpack_small(pack):
    out = {"conv_w": pack[:ROWS_CONV].reshape(-1)[:N_CONV_SHARD].reshape(CONV_SHARD)}
    flat = pack[ROWS_CONV:].reshape(-1)
    k = 0
    for name in SMALL_NAMES:
        shape = SMALL_SHAPES[name]
        n = 1
        for d in shape:
            n *= d
        out[name] = flat[k:k + n].reshape(shape)
        k += n
    out["extra"] = flat[k]
    return out


def _gather_weights(p):
    shards = [p[n].astype(jnp.bfloat16) for n in BIG_NAMES[:4]] + [p["conv_w"]]
    my_chip = 2 * lax.axis_index("x") + lax.axis_index("y")
    w_in, w_o, w_up, w_down, conv = [lax.dynamic_update_index_in_dim(got, own, my_chip, 0)
                                     for got, own in zip(_all_gather(shards), shards)]

    def by_cols(a):
        k, d, r, wd = a.shape
        return jnp.transpose(a, (1, 2, 0, 3)).reshape(d, r, k * wd)

    def by_rows(a):
        k, d, hgt, cc = a.shape
        return jnp.transpose(a, (1, 0, 2, 3)).reshape(d, k * hgt, cc)

    return dict(w_in=by_cols(w_in), w_o=by_rows(w_o), w_up=by_cols(w_up), w_down=by_rows(w_down), conv_w=by_cols(conv))


def _reduce_grads(grads, loss, c):
    bigs = [grads[n] for n in BIG_NAMES[:4]]
    sp = _small_pack(grads, loss)
    c_arr = jnp.reshape(c, (1,)).astype(jnp.int32)
    got = _grad_swap(bigs, sp)
    pair = [_pair_add(bigs[a], got[a], c_arr, "grad_add_cores_" + BIG_NAMES[a], ICI_DT) for a in range(4)]
    pair.append(_pair_add_small(sp, got[4], c_arr))
    my_chip = 2 * lax.axis_index("x") + lax.axis_index("y")
    parts = [lax.dynamic_update_index_in_dim(got_k, lax.dynamic_index_in_dim(own, my_chip, 0, keepdims=False),
                                             my_chip, 0) for got_k, own in zip(_grad_exchange(pair), pair)]
    total = [_sum_chips(parts[a], c_arr, "grad_sum_chips_" + (BIG_NAMES[:4] + ("small",))[a]) for a in range(5)]
    shared = _grad_share(total)
    out = dict(zip(BIG_NAMES[:4], shared[:4]))
    out.update(_unpack_small(shared[4].reshape(2 * SP_HALF, D_MODEL)))
    return out


def kernel(x, norm1_g, w_in, sgu_norm_g, sgu_w, sgu_b, pool_w, pool_scale, mix_norm_g, w_o, norm2_g, w_up, conv_w, conv_b, w_down, final_g, loss_target, m_norm1_g, m_w_in, m_sgu_norm_g, m_sgu_w, m_sgu_b, m_pool_w, m_pool_scale, m_mix_norm_g, m_w_o, m_norm2_g, m_w_up, m_conv_w, m_conv_b, m_w_down, m_final_g, v_norm1_g, v_w_in, v_sgu_norm_g, v_sgu_w, v_sgu_b, v_pool_w, v_pool_scale, v_mix_norm_g, v_w_o, v_norm2_g, v_w_up, v_conv_w, v_conv_b, v_w_down, v_final_g):
    names = ("norm1_g", "w_in", "sgu_norm_g", "sgu_w", "sgu_b", "pool_w", "pool_scale", "mix_norm_g", "w_o",
             "norm2_g", "w_up", "conv_w", "conv_b", "w_down", "final_g")
    p = dict(zip(names, (norm1_g, w_in, sgu_norm_g, sgu_w, sgu_b, pool_w, pool_scale, mix_norm_g, w_o, norm2_g,
                         w_up, conv_w, conv_b, w_down, final_g)))
    pm = dict(zip(names, (m_norm1_g, m_w_in, m_sgu_norm_g, m_sgu_w, m_sgu_b, m_pool_w, m_pool_scale, m_mix_norm_g,
                          m_w_o, m_norm2_g, m_w_up, m_conv_w, m_conv_b, m_w_down, m_final_g)))
    pv = dict(zip(names, (v_norm1_g, v_w_in, v_sgu_norm_g, v_sgu_w, v_sgu_b, v_pool_w, v_pool_scale, v_mix_norm_g,
                          v_w_o, v_norm2_g, v_w_up, v_conv_w, v_conv_b, v_w_down, v_final_g)))
    c = lax.axis_index("c")
    full = dict(p)
    full.update(_gather_weights(p))

    loss, dx, grads = _local_step(x[0], loss_target[0], full)

    g = _reduce_grads(grads, loss, c)
    d, nm, nv = {}, {}, {}
    for n in BIG_NAMES:
        d[n], nm[n], nv[n] = _adamw_big(g[n], p[n], pm[n], pv[n], "adamw_" + n)

    def two_d(a):
        return a.reshape(1, -1) if a.ndim == 1 else a

    ds, ms, vs = _adamw_small([two_d(g[n]) for n in SMALL_NAMES], [two_d(p[n]) for n in SMALL_NAMES],
                              [two_d(pm[n]) for n in SMALL_NAMES], [two_d(pv[n]) for n in SMALL_NAMES])
    for k, n in enumerate(SMALL_NAMES):
        d[n], nm[n], nv[n] = (a.reshape(p[n].shape) for a in (ds[k], ms[k], vs[k]))
    return (g["extra"], dx[None], *[g[n] for n in names], *[d[n] for n in names], *[nm[n] for n in names],
            *[nv[n] for n in names])
```

```python
import functools

import jax
import jax.numpy as jnp
from jax import lax
from jax.experimental import pallas as pl
from jax.experimental.pallas import tpu as pltpu

F32 = jnp.float32
MXU_DT = jnp.bfloat16

D_MODEL = 1024
DEPTH = 2
HEAD_DIM = 64
W_A = 256
W_B = 512
W_C = 256
IN_COLS = 2 * W_A + 3 * W_B + W_C
CHUNK = 128
POOL_WINDOWS = (2, 4, 8, 16)
D_FF = 2816
EPS = 1e-6
N_CHIPS = 4

ADAM_LR = 0.001
ADAM_B1 = 0.9
ADAM_B2 = 0.999
ADAM_EPS = 1e-08
ADAM_WD = 0.01
ADAM_STEP = 10

LANES = 128
TQ = 256
TK = 256
TM = 256
TM_MM = 512
HALO = 16
VMEM_LIMIT = 56 * 1024 * 1024

ROWS_CONV = 16
ROWS_SMALL = 240
SP_HALF = (ROWS_CONV + ROWS_SMALL) // 2
ICI_DT = jnp.bfloat16

BIG_NAMES = ("w_in", "w_o", "w_up", "w_down", "conv_w")
SMALL_NAMES = ("norm1_g", "sgu_norm_g", "sgu_w", "sgu_b", "pool_w", "pool_scale",
               "mix_norm_g", "norm2_g", "conv_b", "final_g")
SMALL_SHAPES = {
    "norm1_g": (DEPTH, D_MODEL), "sgu_norm_g": (DEPTH, W_A), "sgu_w": (DEPTH, 4, CHUNK, CHUNK),
    "sgu_b": (DEPTH, 4, CHUNK), "pool_w": (DEPTH, 4, 64, 64), "pool_scale": (DEPTH, W_C),
    "mix_norm_g": (DEPTH, D_MODEL), "norm2_g": (DEPTH, D_MODEL), "conv_b": (DEPTH, 2 * D_FF),
    "final_g": (D_MODEL,),
}


def _call(body, **kw):
    return pl.pallas_call(body, **kw)


def _params(*sem):
    return pltpu.CompilerParams(dimension_semantics=sem, vmem_limit_bytes=VMEM_LIMIT)


def _dot(a, b):
    return jnp.dot(a, b, preferred_element_type=F32)


def _dot_nt(a, b):
    return lax.dot_general(a, b, (((1,), (1,)), ((), ())), preferred_element_type=F32)


def _dot_tn(a, b):
    return lax.dot_general(a, b, (((0,), (0,)), ((), ())), preferred_element_type=F32)


def _group_mean(sq, gmat):
    sqb = sq.astype(MXU_DT)
    cols = [_dot(sqb[:, b * LANES:(b + 1) * LANES], gmat) for b in range(sq.shape[1] // LANES)]
    return cols[0] if len(cols) == 1 else jnp.concatenate(cols, axis=-1)


def _group_matrix():
    r = jnp.arange(LANES)
    return jnp.where((r[:, None] // HEAD_DIM) == (r[None, :] // HEAD_DIM), 1.0 / HEAD_DIM, 0.0).astype(MXU_DT)


def _tile(n):
    return max(t for t in range(LANES, 1536 + 1, LANES) if n % t == 0)


def _row_spec(tm, cols, col_block=0):
    return pl.BlockSpec((tm, cols), lambda i, cb=col_block: (i, cb))


def _full_spec(shape):
    nd = len(shape)
    return pl.BlockSpec(shape, lambda *_: (0,) * nd)


def _mm_res(a, w, res, name):
    S, K = a.shape
    N = w.shape[1]
    tm = TM_MM

    def body(a_ref, w_ref, r_ref, o_ref):
        o_ref[...] = r_ref[...] + _dot(a_ref[...], w_ref[...])

    return _call(
        body, name=name, grid=(S // tm,),
        in_specs=[_row_spec(tm, K), _full_spec((K, N)), _row_spec(tm, N)],
        out_specs=_row_spec(tm, N),
        out_shape=jax.ShapeDtypeStruct((S, N), F32),
        compiler_params=_params("parallel"),
    )(a, w, res)


def _mm_tn(a, b, name, col_tiles=False):
    S, K1 = a.shape
    N = b.shape[1]
    ts = TM_MM
    tk = _tile(K1)
    tn = _tile(N)
    if col_tiles:
        out_spec = pl.BlockSpec((None, tk, tn), lambda m, n, s: (n, m, 0))
        out_shape = jax.ShapeDtypeStruct((N // tn, K1, tn), F32)
    else:
        out_spec = pl.BlockSpec((tk, tn), lambda m, n, s: (m, n))
        out_shape = jax.ShapeDtypeStruct((K1, N), F32)

    def body(a_ref, b_ref, o_ref):
        @pl.when(pl.program_id(2) == 0)
        def _():
            o_ref[...] = jnp.zeros_like(o_ref)

        o_ref[...] += _dot_tn(a_ref[...], b_ref[...].astype(MXU_DT))

    return _call(
        body, name=name, grid=(K1 // tk, N // tn, S // ts),
        in_specs=[pl.BlockSpec((ts, tk), lambda m, n, s: (s, m)),
                  pl.BlockSpec((ts, tn), lambda m, n, s: (s, n))],
        out_specs=out_spec, out_shape=out_shape,
        compiler_params=_params("parallel", "parallel", "arbitrary"),
    )(a, b)


def _mm_nt_rmsbwd(pairs, x, g, dres, name):
    S, D = x.shape
    tm = TM
    n = len(pairs)

    def body(*refs):
        a_refs = refs[:n]
        w_refs = refs[n:2 * n]
        x_ref, g_ref, r_ref, dx_ref, dg_ref = refs[2 * n:]
        dh = _dot_nt(a_refs[0][...], w_refs[0][...])
        for k in range(1, n):
            dh += _dot_nt(a_refs[k][...], w_refs[k][...])
        xv = x_ref[...]
        r = lax.rsqrt(jnp.mean(xv * xv, axis=-1, keepdims=True) + EPS)
        xhat = xv * r

        @pl.when(pl.program_id(0) == 0)
        def _():
            dg_ref[...] = jnp.zeros_like(dg_ref)

        dg_ref[...] += jnp.sum(dh * xhat, axis=0, keepdims=True)
        dxh = dh * g_ref[...]
        dx_ref[...] = r_ref[...] + r * (dxh - xhat * jnp.mean(dxh * xhat, axis=-1, keepdims=True))

    in_specs = ([_row_spec(tm, a.shape[1]) for a, _ in pairs] + [_full_spec(w.shape) for _, w in pairs]
                + [_row_spec(tm, D), _full_spec((1, D)), _row_spec(tm, D)])
    return _call(
        body, name=name, grid=(S // tm,), in_specs=in_specs,
        out_specs=[_row_spec(tm, D), _full_spec((1, D))],
        out_shape=[jax.ShapeDtypeStruct((S, D), F32), jax.ShapeDtypeStruct((1, D), F32)],
        compiler_params=_params("arbitrary"),
    )(*[a for a, _ in pairs], *[w for _, w in pairs], x, g, dres)


def _down_proj_loss(a, w, res, g, tgt):
    S, D = res.shape
    K = a.shape[1]
    tm = TM

    def body(a_ref, w_ref, res_ref, g_ref, t_ref, dx_ref, dg_ref, l_ref):
        xv = res_ref[...] + _dot(a_ref[...], w_ref[...])
        r = lax.rsqrt(jnp.mean(xv * xv, axis=-1, keepdims=True) + EPS)
        xhat = xv * r
        diff = xhat * g_ref[...] - t_ref[...]

        @pl.when(pl.program_id(0) == 0)
        def _():
            dg_ref[...] = jnp.zeros_like(dg_ref)
            l_ref[...] = jnp.zeros_like(l_ref)

        l_ref[...] += jnp.full(l_ref.shape, 0.5 * jnp.sum(jnp.mean(diff * diff, axis=-1, keepdims=True)), F32)
        dout = diff * (1.0 / D)
        dg_ref[...] += jnp.sum(dout * xhat, axis=0, keepdims=True)
        dxh = dout * g_ref[...]
        dx_ref[...] = r * (dxh - xhat * jnp.mean(dxh * xhat, axis=-1, keepdims=True))

    return _call(
        body, name="down_proj_loss", grid=(S // tm,),
        in_specs=[_row_spec(tm, K), _full_spec((K, D)), _row_spec(tm, D), _full_spec((1, D)), _row_spec(tm, D)],
        out_specs=[_row_spec(tm, D), _full_spec((1, D)), _full_spec((8, LANES))],
        out_shape=[jax.ShapeDtypeStruct((S, D), F32), jax.ShapeDtypeStruct((1, D), F32),
                   jax.ShapeDtypeStruct((8, LANES), F32)],
        compiler_params=_params("arbitrary"),
    )(a, w, res, g, tgt)


def _mix_out(ya, yb, yc, gm, wo, x, gmat):
    S = x.shape[0]
    tm = TM

    def body(ya_ref, yb_ref, yc_ref, gm_ref, wo_ref, x_ref, gmat_ref, x2_ref, yn_ref):
        y = jnp.concatenate([ya_ref[...], yb_ref[...], yc_ref[...]], axis=-1)
        r = lax.rsqrt(_group_mean(y * y, gmat_ref[...]) + EPS)
        yn = (y * r * gm_ref[...]).astype(MXU_DT)
        yn_ref[...] = yn
        x2_ref[...] = x_ref[...] + _dot(yn, wo_ref[...])

    return _call(
        body, name="mix_out", grid=(S // tm,),
        in_specs=[_row_spec(tm, W_A), _row_spec(tm, W_B), _row_spec(tm, W_C), _full_spec((1, D_MODEL)),
                  _full_spec((D_MODEL, D_MODEL)), _row_spec(tm, D_MODEL), _full_spec((LANES, LANES))],
        out_specs=[_row_spec(tm, D_MODEL), _row_spec(tm, D_MODEL)],
        out_shape=[jax.ShapeDtypeStruct((S, D_MODEL), F32), jax.ShapeDtypeStruct((S, D_MODEL), MXU_DT)],
        compiler_params=_params("parallel"),
    )(ya, yb, yc, gm, wo, x, gmat)


def _mix_out_bwd(dx2, wo, ya, yb, yc, gm, gmat):
    S = dx2.shape[0]
    tm = TM

    def body(dx2_ref, wo_ref, ya_ref, yb_ref, yc_ref, gm_ref, gmat_ref, dya_ref, dyb_ref, dyc_ref, dgm_ref):
        dyn = _dot_nt(dx2_ref[...].astype(MXU_DT), wo_ref[...])
        y = jnp.concatenate([ya_ref[...], yb_ref[...], yc_ref[...]], axis=-1)
        r = lax.rsqrt(_group_mean(y * y, gmat_ref[...]) + EPS)
        yhat = y * r

        @pl.when(pl.program_id(0) == 0)
        def _():
            dgm_ref[...] = jnp.zeros_like(dgm_ref)

        dgm_ref[...] += jnp.sum(dyn * yhat, axis=0, keepdims=True)
        dyh = dyn * gm_ref[...]
        dy = r * (dyh - yhat * _group_mean(dyh * yhat, gmat_ref[...]))
        dya_ref[...] = dy[:, :W_A]
        dyb_ref[...] = dy[:, W_A:W_A + W_B]
        dyc_ref[...] = dy[:, W_A + W_B:]

    return _call(
        body, name="mix_out_bwd", grid=(S // tm,),
        in_specs=[_row_spec(tm, D_MODEL), _full_spec((D_MODEL, D_MODEL)), _row_spec(tm, W_A), _row_spec(tm, W_B),
                  _row_spec(tm, W_C), _full_spec((1, D_MODEL)), _full_spec((LANES, LANES))],
        out_specs=[_row_spec(tm, W_A), _row_spec(tm, W_B), _row_spec(tm, W_C), _full_spec((1, D_MODEL))],
        out_shape=[jax.ShapeDtypeStruct((S, W_A), F32), jax.ShapeDtypeStruct((S, W_B), F32),
                   jax.ShapeDtypeStruct((S, W_C), F32), jax.ShapeDtypeStruct((1, D_MODEL), F32)],
        compiler_params=_params("arbitrary"),
    )(dx2, wo, ya, yb, yc, gm, gmat)


_SQRT_HALF = 0.7071067811865476
_INV_SQRT_2PI = 0.3989422804014327


def _sgu_common(a, sng, wm_ref, bias, gmat):
    phi = 0.5 * (1.0 + lax.erf(a * _SQRT_HALF))
    ga = a * phi
    u = ga[:, :W_A]
    v = ga[:, W_A:]
    r = lax.rsqrt(_group_mean(v * v, gmat) + EPS)
    vhat = v * r
    vn = (vhat * sng).astype(MXU_DT)
    head = lax.broadcasted_iota(jnp.int32, (CHUNK, W_A), 1) // HEAD_DIM
    rows = []
    for c in range(a.shape[0] // CHUNK):
        vc = vn[c * CHUNK:(c + 1) * CHUNK]
        s = bias
        for h in range(4):
            s = s + jnp.where(head == h, _dot(wm_ref[h], vc), 0.0)
        rows.append(s)
    s = jnp.concatenate(rows, axis=0)
    return phi, u, r, vhat, vn, s


def _tril_weights(sgu_w_l):
    t = jnp.arange(CHUNK)
    return jnp.where((t[None, :] <= t[:, None])[None], sgu_w_l, 0.0)


def _sgu_bwd(proj, dy, sng, wm, wmt, bias, gmat):
    S = proj.shape[0]
    tm = TM

    def body(a_ref, dy_ref, sng_ref, wm_ref, wmt_ref, b_ref, gmat_ref, da_ref, dw_ref, db_ref, dsng_ref):
        a = a_ref[...]
        dy = dy_ref[...]
        gmat = gmat_ref[...]
        sng = sng_ref[...]
        phi, u, r, vhat, vn, s = _sgu_common(a, sng, wm_ref, b_ref[...], gmat)
        du = dy * s
        ds = dy * u

        @pl.when(pl.program_id(0) == 0)
        def _():
            dw_ref[...] = jnp.zeros_like(dw_ref)
            db_ref[...] = jnp.zeros_like(db_ref)
            dsng_ref[...] = jnp.zeros_like(dsng_ref)

        head = lax.broadcasted_iota(jnp.int32, (CHUNK, W_A), 1) // HEAD_DIM
        tt = lax.broadcasted_iota(jnp.int32, (CHUNK, CHUNK), 0)
        ss = lax.broadcasted_iota(jnp.int32, (CHUNK, CHUNK), 1)
        rows = []
        for c in range(tm // CHUNK):
            dsc = ds[c * CHUNK:(c + 1) * CHUNK]
            vc = vn[c * CHUNK:(c + 1) * CHUNK]
            db_ref[...] += dsc
            dsb = dsc.astype(MXU_DT)
            dvn = jnp.zeros((CHUNK, W_A), F32)
            for h in range(4):
                dvn = dvn + jnp.where(head == h, _dot(wmt_ref[h], dsb), 0.0)
                dsh = jnp.where(head == h, dsc, 0.0).astype(MXU_DT)
                dw_ref[h] += jnp.where(ss <= tt, _dot_nt(dsh, vc), 0.0)
            rows.append(dvn)
        dvn = jnp.concatenate(rows, axis=0)
        dsng_ref[...] += jnp.sum(dvn * vhat, axis=0, keepdims=True)
        dvh = dvn * sng
        dv = r * (dvh - vhat * _group_mean(dvh * vhat, gmat))
        dga = jnp.concatenate([du, dv], axis=-1)
        dgelu = phi + a * (_INV_SQRT_2PI * jnp.exp(-0.5 * a * a))
        da_ref[...] = (dga * dgelu).astype(da_ref.dtype)

    return _call(
        body, name="sgu_bwd", grid=(S // tm,),
        in_specs=[_row_spec(tm, 2 * W_A), _row_spec(tm, W_A), _full_spec((1, W_A)), _full_spec((4, CHUNK, CHUNK)),
                  _full_spec((4, CHUNK, CHUNK)), _full_spec((CHUNK, W_A)), _full_spec((LANES, LANES))],
        out_specs=[_row_spec(tm, 2 * W_A), _full_spec((4, CHUNK, CHUNK)), _full_spec((CHUNK, W_A)),
                   _full_spec((1, W_A))],
        out_shape=[jax.ShapeDtypeStruct((S, 2 * W_A), MXU_DT), jax.ShapeDtypeStruct((4, CHUNK, CHUNK), F32),
                   jax.ShapeDtypeStruct((CHUNK, W_A), F32), jax.ShapeDtypeStruct((1, W_A), F32)],
        compiler_params=_params("arbitrary"),
    )(proj, dy, sng, wm, wmt, bias, gmat)


HG = 4
LW = HG * HEAD_DIM
Q_BLK0 = (2 * W_A) // LW
K_BLK0 = Q_BLK0 + W_B // LW
V_BLK0 = K_BLK0 + W_B // LW
N_GROUPS = W_B // LW
EXP_IS_ZERO_BELOW = -120.0


def _tri_matrix():
    r = jnp.arange(TK)
    return (r[:, None] > r[None, :]).astype(MXU_DT)


def _stack_heads(a):
    head = lax.broadcasted_iota(jnp.int32, a.shape, 1) // HEAD_DIM
    return jnp.concatenate([jnp.where(head == h, a, 0.0) for h in range(HG)], axis=0).astype(MXU_DT)


def _unstack_heads(a):
    head = lax.broadcasted_iota(jnp.int32, (TQ, LW), 1) // HEAD_DIM
    out = a[:TQ]
    for h in range(1, HG):
        out = jnp.where(head == h, a[h * TQ:(h + 1) * TQ], out)
    return out


def _sb_scores(q2, kj, tri, key_offset):
    z = _dot_nt(q2, kj)
    sp = jnp.log(1.0 + jnp.exp(-jnp.abs(z)))
    lsp = jnp.minimum(z, 0.0) - sp
    lsm = lsp - z
    msk = None
    if key_offset is not None:
        row = lax.broadcasted_iota(jnp.int32, z.shape, 0) & (TQ - 1)
        col = lax.broadcasted_iota(jnp.int32, z.shape, 1) + key_offset
        msk = col < row
        lsm = jnp.where(msk, lsm, 0.0)
    tail = _dot(lsm.astype(MXU_DT), tri)
    return lsp, lsm, tail, msk


def _sb_fwd(proj_b, tri, carry_gather=None):
    S = proj_b.shape[0]
    nq = S // TQ
    kpq = TQ // TK
    assert S // TK < LANES
    shards = list(carry_gather or [])
    ng = len(shards)

    def body(*refs):
        q_ref, k_ref, v_ref, tri_ref = refs[:4]
        o_ref, rb_ref = refs[4 + ng:6 + ng]
        acc_ref = refs[6 + 2 * ng]
        i = pl.program_id(1)
        if ng:
            step = pl.program_id(0) * nq + i
            sender = 1
            c, sends, arrivals, forwards = _gather_plan(refs[4:4 + ng], refs[6 + ng:6 + 2 * ng], refs[7 + 2 * ng],
                                                        refs[8 + 2 * ng], 1, sender)

            @pl.when(jnp.logical_and(step == 0, c == sender))
            def _():
                for cp in sends:
                    cp.start()

            @pl.when(jnp.logical_and(step == N_GROUPS * nq - max(nq // 8, 1), c == sender))
            def _():
                for arrived, onward in zip(arrivals, forwards):
                    arrived.wait_recv()
                    onward.start()


        lane2 = lax.broadcasted_iota(jnp.int32, (HG * TQ, LANES), 1)
        q2 = _stack_heads(q_ref[...].astype(F32) * (HEAD_DIM ** -0.5))
        tri = tri_ref[...]
        rb_ref[...] = jnp.zeros_like(rb_ref)

        def block(j, run, key_offset=None, first=False):
            start = pl.multiple_of(j * TK, TK)
            kj = k_ref[pl.ds(start, TK), :]
            vj = v_ref[pl.ds(start, TK), :]
            lsp, lsm, tail, msk = _sb_scores(q2, kj, tri, key_offset)
            rb_ref[...] = jnp.where(lane2 == j, run, rb_ref[...])
            att = jnp.exp(lsp + tail + run)
            if msk is not None:
                att = jnp.where(msk, att, 0.0)
            pv = _dot(att.astype(MXU_DT), vj)
            if first:
                acc_ref[...] = pv
            else:
                acc_ref[...] += pv
            return run + tail[:, :1] + lsm[:, :1]

        run = jnp.zeros((HG * TQ, 1), F32)
        for d in reversed(range(kpq)):
            run = block(i * kpq + d, run, key_offset=d * TK, first=(d == kpq - 1))
        past = i * kpq

        def alive(run):
            return (jnp.max(run) > EXP_IS_ZERO_BELOW).astype(jnp.int32)

        def walk(carry):
            n, run, _ = carry
            run = block(past - 1 - n, run)
            return n + 1, run, alive(run)

        n, _, _ = lax.while_loop(lambda s: jnp.logical_and(s[0] < past, s[2] > 0), walk,
                                 (jnp.int32(0), run, alive(run)))
        rb_ref[...] = jnp.where(lane2 == LANES - 1, n.astype(F32), rb_ref[...])
        o_ref[...] = _unstack_heads(acc_ref[...])

        if ng:
            @pl.when(jnp.logical_and(step == N_GROUPS * nq - 1, c == sender))
            def _():
                for cp in sends + forwards:
                    cp.wait_send()

            @pl.when(jnp.logical_and(step == N_GROUPS * nq - 1, c != sender))
            def _():
                for cp in forwards:
                    cp.wait_recv()

    once = pl.Buffered(1)
    gathered, sems = _gather_shapes(shards) if ng else ([], [])
    return _call(
        body, name="sb_fwd_gather" if ng else "sb_fwd", grid=(N_GROUPS, nq),
        in_specs=[pl.BlockSpec((TQ, LW), lambda p, i: (i, Q_BLK0 + p)),
                  pl.BlockSpec((S, LW), lambda p, i: (0, K_BLK0 + p), pipeline_mode=once),
                  pl.BlockSpec((S, LW), lambda p, i: (0, V_BLK0 + p), pipeline_mode=once),
                  pl.BlockSpec((TK, TK), lambda p, i: (0, 0))] + [ANY] * ng,
        out_specs=[pl.BlockSpec((TQ, LW), lambda p, i: (i, p)),
                   pl.BlockSpec((None, None, HG * TQ, LANES), lambda p, i: (p, i, 0, 0))] + [ANY] * ng,
        out_shape=[jax.ShapeDtypeStruct((S, W_B), F32),
                   jax.ShapeDtypeStruct((N_GROUPS, nq, HG * TQ, LANES), F32)] + gathered,
        scratch_shapes=[pltpu.VMEM((HG * TQ, LW), F32)] + sems,
        compiler_params=_params("arbitrary", "arbitrary"),
    )(proj_b, proj_b, proj_b, tri, *shards)


def _sb_bwd(proj_b, dyb, rb, tri, trit):
    S = proj_b.shape[0]
    nq = S // TQ
    kpq = TQ // TK

    def body(q_ref, k_ref, v_ref, do_ref, rb_ref, tri_ref, trit_ref, dq_ref, dk_acc, dv_acc, dq_acc):
        i = pl.program_id(1)
        lane2 = lax.broadcasted_iota(jnp.int32, (HG * TQ, LANES), 1)
        scale = HEAD_DIM ** -0.5
        q2 = _stack_heads(q_ref[...].astype(F32) * scale)
        do2 = _stack_heads(do_ref[...])
        tri = tri_ref[...]
        trit = trit_ref[...]

        @pl.when(i == 0)
        def _():
            dk_acc[...] = jnp.zeros_like(dk_acc)
            dv_acc[...] = jnp.zeros_like(dv_acc)

        dq_acc[...] = jnp.zeros_like(dq_acc)

        def block(j, pre, key_offset=None):
            start = pl.multiple_of(j * TK, TK)
            kj = k_ref[pl.ds(start, TK), :]
            vj = v_ref[pl.ds(start, TK), :]
            lsp, lsm, tail, msk = _sb_scores(q2, kj, tri, key_offset)
            run = jnp.sum(jnp.where(lane2 == j, rb_ref[...], 0.0), axis=-1, keepdims=True)
            att = jnp.exp(lsp + tail + run)
            if msk is not None:
                att = jnp.where(msk, att, 0.0)
            beta = jnp.exp(lsp)
            dl = _dot_nt(do2, vj) * att
            cin = _dot(dl.astype(MXU_DT), trit)
            dz = dl * (1.0 - beta) - beta * (pre + cin)
            if msk is not None:
                dz = jnp.where(msk, dz, 0.0)
            dzb = dz.astype(MXU_DT)
            dq_acc[...] += _dot(dzb, kj)
            dk_acc[pl.ds(start, TK), :] += _dot_tn(dzb, q2)
            dv_acc[pl.ds(start, TK), :] += _dot_tn(att.astype(MXU_DT), do2)
            return pre + cin[:, TK - 1:] + dl[:, TK - 1:]

        past = i * kpq
        walked = jnp.max(jnp.where(lane2[:8] == LANES - 1, rb_ref[pl.ds(0, 8), :], 0.0)).astype(jnp.int32)
        walked = jnp.clip(walked, 0, past)
        pre = lax.fori_loop(past - walked, past, lambda j, pre: block(j, pre), jnp.zeros((HG * TQ, 1), F32))
        for d in range(kpq):
            pre = block(i * kpq + d, pre, key_offset=d * TK)
        dq_ref[...] = (_unstack_heads(dq_acc[...]) * scale).astype(dq_ref.dtype)

    once = pl.Buffered(1)
    return _call(
        body, name="sb_bwd", grid=(N_GROUPS, nq),
        in_specs=[pl.BlockSpec((TQ, LW), lambda p, i: (i, Q_BLK0 + p)),
                  pl.BlockSpec((S, LW), lambda p, i: (0, K_BLK0 + p), pipeline_mode=once),
                  pl.BlockSpec((S, LW), lambda p, i: (0, V_BLK0 + p), pipeline_mode=once),
                  pl.BlockSpec((TQ, LW), lambda p, i: (i, p)),
                  pl.BlockSpec((None, None, HG * TQ, LANES), lambda p, i: (p, i, 0, 0)),
                  pl.BlockSpec((TK, TK), lambda p, i: (0, 0)),
                  pl.BlockSpec((TK, TK), lambda p, i: (0, 0))],
        out_specs=[pl.BlockSpec((TQ, LW), lambda p, i: (i, p)),
                   pl.BlockSpec((S, LW), lambda p, i: (0, p), pipeline_mode=once),
                   pl.BlockSpec((S, LW), lambda p, i: (0, p), pipeline_mode=once)],
        out_shape=[jax.ShapeDtypeStruct((S, W_B), MXU_DT), jax.ShapeDtypeStruct((S, W_B), F32),
                   jax.ShapeDtypeStruct((S, W_B), F32)],
        scratch_shapes=[pltpu.VMEM((HG * TQ, LW), F32)],
        compiler_params=_params("parallel", "arbitrary"),
    )(proj_b, proj_b, proj_b, dyb, rb, tri, trit)


P_BLK = (2 * W_A + 3 * W_B) // W_C


def _window_lanes():
    g = lax.broadcasted_iota(jnp.int32, (1, W_C), 1) // (W_C // 4)
    w = jnp.where(g == 0, POOL_WINDOWS[0], jnp.where(g == 1, POOL_WINDOWS[1],
                  jnp.where(g == 2, POOL_WINDOWS[2], POOL_WINDOWS[3])))
    return g, w


def _shift_rows(ext, k, tm, lead):
    n = ext.shape[0]
    return pltpu.roll(ext, shift=k % n, axis=0)[lead:lead + tm]


def _pool_diff(p_cur, p_halo, row0, tm):
    ext = jnp.concatenate([p_halo, p_cur], axis=0)
    g, w = _window_lanes()
    acc = ext
    sums = []
    for sh in (1, 2, 4, 8):
        acc = acc + pltpu.roll(acc, shift=sh, axis=0)
        sums.append(acc[HALO:HALO + tm])
    wsum = jnp.where(g == 0, sums[0], jnp.where(g == 1, sums[1], jnp.where(g == 2, sums[2], sums[3])))
    pos = (row0 + 1 + lax.broadcasted_iota(jnp.int32, (tm, W_C), 0)).astype(F32)
    cnt = jnp.minimum(pos, w.astype(F32))
    return wsum / cnt - p_cur, cnt


def _pool_specs(tm, nrow_blocks_halo):
    cur = pl.BlockSpec((tm, W_C), lambda i: (i, P_BLK))
    prev = pl.BlockSpec((HALO, W_C), lambda i: (jnp.maximum(i * (tm // HALO) - 1, 0), P_BLK))
    return cur, prev


def _in_proj_groups(x, g, w, sng, wm, bias, gmat, wbd, scale):
    S, D = x.shape
    tm = TM_MM
    p0 = 2 * W_A + 3 * W_B
    qkv_chunk = 3 * W_B // 2

    def body(x_ref, g_ref, w_ref, sng_ref, wm_ref, b_ref, gmat_ref, wbd_ref, sc_ref,
             h_ref, o_ref, ob_ref, ya_ref, yc_ref, tail_ref):
        i = pl.program_id(0)
        xv = x_ref[...]
        r = lax.rsqrt(jnp.mean(xv * xv, axis=-1, keepdims=True) + EPS)
        h = (xv * r * g_ref[...]).astype(h_ref.dtype)
        h_ref[...] = h

        def project(c0, c1):
            acc = _dot(h, w_ref[:, c0:c1])
            o_ref[:, c0:c1] = acc
            ob_ref[:, c0:c1] = acc.astype(ob_ref.dtype)
            return acc

        a = project(0, 2 * W_A)
        _, u, _, _, _, s = _sgu_common(a, sng_ref[...], wm_ref, b_ref[...], gmat_ref[...])
        ya_ref[...] = u * s
        for c0 in range(2 * W_A, p0, qkv_chunk):
            project(c0, c0 + qkv_chunk)
        p = project(p0, p0 + W_C)
        halo = jnp.where(i > 0, tail_ref[...], 0.0)
        tail_ref[...] = p[tm - HALO:]
        d, _ = _pool_diff(p, halo, i * tm, tm)
        yc_ref[...] = _dot(d.astype(MXU_DT), wbd_ref[...]) * sc_ref[...]

    return _call(
        body, name="in_proj_groups", grid=(S // tm,),
        in_specs=[_row_spec(tm, D), _full_spec((1, D)),
                  pl.BlockSpec((D, IN_COLS), lambda i: (0, 0), pipeline_mode=pl.Buffered(1)),
                  _full_spec((1, W_A)), _full_spec((4, CHUNK, CHUNK)), _full_spec((CHUNK, W_A)),
                  _full_spec((LANES, LANES)), _full_spec((W_C, W_C)), _full_spec((1, W_C))],
        out_specs=[_row_spec(tm, D), _row_spec(tm, IN_COLS), _row_spec(tm, IN_COLS), _row_spec(tm, W_A),
                   _row_spec(tm, W_C)],
        out_shape=[jax.ShapeDtypeStruct((S, D), MXU_DT), jax.ShapeDtypeStruct((S, IN_COLS), F32),
                   jax.ShapeDtypeStruct((S, IN_COLS), MXU_DT), jax.ShapeDtypeStruct((S, W_A), F32),
                   jax.ShapeDtypeStruct((S, W_C), F32)],
        scratch_shapes=[pltpu.VMEM((HALO, W_C), F32)],
        compiler_params=_params("arbitrary"),
    )(x, g, w, sng, wm, bias, gmat, wbd, scale)


def _pool_bwd_a(proj, dy, wbd, scale):
    S = proj.shape[0]
    tm = TM

    def body(p_ref, ph_ref, dy_ref, w_ref, sc_ref, dd_ref, e_ref, dw_ref, dsc_ref):
        i = pl.program_id(0)
        halo = jnp.where(i > 0, ph_ref[...], 0.0)
        d, cnt = _pool_diff(p_ref[...], halo, i * tm, tm)
        db = d.astype(MXU_DT)
        dy = dy_ref[...]

        @pl.when(i == 0)
        def _():
            dw_ref[...] = jnp.zeros_like(dw_ref)
            dsc_ref[...] = jnp.zeros_like(dsc_ref)

        dsc_ref[...] += jnp.sum(dy * _dot(db, w_ref[...]), axis=0, keepdims=True)
        dys = (dy * sc_ref[...]).astype(MXU_DT)
        dw_ref[...] += _dot_tn(db, dys)
        dd = _dot_nt(dys, w_ref[...])
        dd_ref[...] = dd
        e_ref[...] = dd / cnt

    cur, prev = _pool_specs(tm, S // HALO)
    return _call(
        body, name="pool_bwd_a", grid=(S // tm,),
        in_specs=[cur, prev, _row_spec(tm, W_C), _full_spec((W_C, W_C)), _full_spec((1, W_C))],
        out_specs=[_row_spec(tm, W_C), _row_spec(tm, W_C), _full_spec((W_C, W_C)), _full_spec((1, W_C))],
        out_shape=[jax.ShapeDtypeStruct((S, W_C), F32), jax.ShapeDtypeStruct((S, W_C), F32),
                   jax.ShapeDtypeStruct((W_C, W_C), F32), jax.ShapeDtypeStruct((1, W_C), F32)],
        compiler_params=_params("arbitrary"),
    )(proj, proj, dy, wbd, scale)


def _pool_bwd_b(dd, e):
    S = dd.shape[0]
    tm = TM
    nb = S // tm

    def body(dd_ref, e_ref, en_ref, dp_ref):
        i = pl.program_id(0)
        halo = jnp.where(i < nb - 1, en_ref[...], 0.0)
        ext = jnp.concatenate([e_ref[...], halo], axis=0)
        n = ext.shape[0]
        g, _ = _window_lanes()
        acc = ext
        sums = []
        for sh in (1, 2, 4, 8):
            acc = acc + pltpu.roll(acc, shift=n - sh, axis=0)
            sums.append(acc[:tm])
        wsum = jnp.where(g == 0, sums[0], jnp.where(g == 1, sums[1], jnp.where(g == 2, sums[2], sums[3])))
        dp_ref[...] = (wsum - dd_ref[...]).astype(dp_ref.dtype)

    nxt = pl.BlockSpec((HALO, W_C), lambda i: (jnp.minimum((i + 1) * (tm // HALO), S // HALO - 1), 0))
    return _call(
        body, name="pool_bwd_b", grid=(nb,),
        in_specs=[_row_spec(tm, W_C), _row_spec(tm, W_C), nxt],
        out_specs=_row_spec(tm, W_C),
        out_shape=jax.ShapeDtypeStruct((S, W_C), MXU_DT),
        compiler_params=_params("parallel"),
    )(dd, e, e)


TN_FF = 1408
NB_FF = D_FF // TN_FF
CONV_ROWS = 8


def _conv(z_cur, z_halo, cwb, tm):
    ext = jnp.concatenate([z_halo, z_cur], axis=0)
    z2 = _shift_rows(ext, 2, tm, HALO)
    z1 = _shift_rows(ext, 1, tm, HALO)
    zc = cwb[3:4] + z2 * cwb[0:1] + z1 * cwb[1:2] + z_cur * cwb[2:3]
    return zc, z2, z1


def _up_proj_gate(x, g, w, cwb):
    S, D = x.shape
    tm = TM

    def body(x_ref, g_ref, w_ref, c_ref, h_ref, z_ref, zc_ref, f_ref, tail_ref):
        first = pl.program_id(0) == 0
        xv = x_ref[...]
        r = lax.rsqrt(jnp.mean(xv * xv, axis=-1, keepdims=True) + EPS)
        h = (xv * r * g_ref[...]).astype(h_ref.dtype)
        h_ref[...] = h
        for j in range(NB_FF):
            halves = []
            for col0 in (j * TN_FF, D_FF + j * TN_FF):
                zb = _dot(h, w_ref[:, col0:col0 + TN_FF]).astype(z_ref.dtype)
                z_ref[:, col0:col0 + TN_FF] = zb
                zf = zb.astype(F32)
                prev = jnp.where(first, 0.0, tail_ref[:, col0:col0 + TN_FF])
                tail_ref[:, col0:col0 + TN_FF] = zf[tm - HALO:]
                zc = _conv(zf, prev, c_ref[:, col0:col0 + TN_FF], tm)[0]
                zc_ref[:, col0:col0 + TN_FF] = zc.astype(zc_ref.dtype)
                halves.append(zc)
            gate, value = halves
            f_ref[:, j * TN_FF:(j + 1) * TN_FF] = (gate * jax.nn.sigmoid(gate) * value).astype(f_ref.dtype)

    return _call(
        body, name="up_proj_gate", grid=(S // tm,),
        in_specs=[_row_spec(tm, D), _full_spec((1, D)),
                  pl.BlockSpec((D, 2 * D_FF), lambda i: (0, 0), pipeline_mode=pl.Buffered(1)),
                  _full_spec((CONV_ROWS, 2 * D_FF))],
        out_specs=[_row_spec(tm, D), _row_spec(tm, 2 * D_FF), _row_spec(tm, 2 * D_FF), _row_spec(tm, D_FF)],
        out_shape=[jax.ShapeDtypeStruct((S, D), MXU_DT), jax.ShapeDtypeStruct((S, 2 * D_FF), MXU_DT),
                   jax.ShapeDtypeStruct((S, 2 * D_FF), MXU_DT), jax.ShapeDtypeStruct((S, D_FF), MXU_DT)],
        scratch_shapes=[pltpu.VMEM((HALO, 2 * D_FF), F32)],
        compiler_params=_params("arbitrary"),
    )(x, g, w, cwb)


def _gate_up_bwd(z, zc, cwb, w, wd, x, g, dres):
    S, D = x.shape
    tm = TM
    nb = S // tm

    def body(z_ref, zc_ref, zcn_ref, c_ref, w_ref, wd_ref, x_ref, g_ref, r_ref, rn_ref,
             dz_ref, dc_ref, dx_ref, dg_ref):
        i = pl.program_id(0)
        first = i == 0
        last = i == nb - 1
        dxe = jnp.concatenate([r_ref[...], jnp.where(last, 0.0, rn_ref[...])], axis=0).astype(MXU_DT)

        @pl.when(first)
        def _():
            dc_ref[...] = jnp.zeros_like(dc_ref)
            dg_ref[...] = jnp.zeros_like(dg_ref)

        rid = lax.broadcasted_iota(jnp.int32, (CONV_ROWS, TN_FF), 0)

        def conv_out(cols):
            return jnp.concatenate([zc_ref[:, cols].astype(F32), zcn_ref[:, cols].astype(F32)], axis=0)

        def conv_bwd(d, z0, c):
            d0 = d[:tm]
            d1 = _shift_rows(d, -1, tm, 0)
            d2 = _shift_rows(d, -2, tm, 0)
            sums = [jnp.sum(d2 * z0, axis=0, keepdims=True), jnp.sum(d1 * z0, axis=0, keepdims=True),
                    jnp.sum(d0 * z0, axis=0, keepdims=True), jnp.sum(d0, axis=0, keepdims=True)]
            dtaps = jnp.zeros((CONV_ROWS, TN_FF), F32)
            for k, v in enumerate(sums):
                dtaps = jnp.where(rid == k, v, dtaps)
            return d0 * c[2:3] + d1 * c[1:2] + d2 * c[0:1], dtaps

        dh = jnp.zeros((tm, D), F32)
        for j in range(NB_FF):
            gc = slice(j * TN_FF, (j + 1) * TN_FF)
            uc = slice(D_FF + j * TN_FF, D_FF + (j + 1) * TN_FF)
            gt = conv_out(gc)
            ut = conv_out(uc)
            df = _dot_nt(dxe, wd_ref[gc, :])
            sg = jax.nn.sigmoid(gt)
            dzg, dtg = conv_bwd(df * ut * (sg * (1.0 + gt * (1.0 - sg))), z_ref[:, gc].astype(F32), c_ref[:, gc])
            dzu, dtu = conv_bwd(df * (gt * sg), z_ref[:, uc].astype(F32), c_ref[:, uc])
            dzg = dzg.astype(dz_ref.dtype)
            dzu = dzu.astype(dz_ref.dtype)
            dz_ref[:, gc] = dzg
            dz_ref[:, uc] = dzu
            dc_ref[:, gc] += dtg
            dc_ref[:, uc] += dtu
            dh += _dot_nt(dzg, w_ref[:, gc]) + _dot_nt(dzu, w_ref[:, uc])

        xv = x_ref[...]
        r = lax.rsqrt(jnp.mean(xv * xv, axis=-1, keepdims=True) + EPS)
        xhat = xv * r
        dg_ref[...] += jnp.sum(dh * xhat, axis=0, keepdims=True)
        dxh = dh * g_ref[...]
        dx_ref[...] = r_ref[...] + r * (dxh - xhat * jnp.mean(dxh * xhat, axis=-1, keepdims=True))

    hb = tm // HALO
    last_halo = S // HALO - 1
    return _call(
        body, name="gate_up_bwd", grid=(nb,),
        in_specs=[_row_spec(tm, 2 * D_FF), _row_spec(tm, 2 * D_FF),
                  pl.BlockSpec((HALO, 2 * D_FF), lambda i: (jnp.minimum((i + 1) * hb, last_halo), 0)),
                  _full_spec((CONV_ROWS, 2 * D_FF)),
                  pl.BlockSpec((D, 2 * D_FF), lambda i: (0, 0), pipeline_mode=pl.Buffered(1)),
                  pl.BlockSpec((D_FF, D), lambda i: (0, 0), pipeline_mode=pl.Buffered(1)),
                  _row_spec(tm, D), _full_spec((1, D)), _row_spec(tm, D),
                  pl.BlockSpec((HALO, D), lambda i: (jnp.minimum((i + 1) * hb, last_halo), 0))],
        out_specs=[_row_spec(tm, 2 * D_FF), _full_spec((CONV_ROWS, 2 * D_FF)), _row_spec(tm, D), _full_spec((1, D))],
        out_shape=[jax.ShapeDtypeStruct((S, 2 * D_FF), MXU_DT), jax.ShapeDtypeStruct((CONV_ROWS, 2 * D_FF), F32),
                   jax.ShapeDtypeStruct((S, D), F32), jax.ShapeDtypeStruct((1, D), F32)],
        compiler_params=_params("arbitrary"),
    )(z, zc, zc, cwb, w, wd, x, g, dres, dres)


def _layer_consts(w, l):
    wm = _tril_weights(w["sgu_w"][l])
    eye = jnp.eye(4, dtype=F32)
    wbd = (w["pool_w"][l][:, :, None, :] * eye[:, None, :, None]).reshape(W_C, W_C)
    cwb = jnp.concatenate([w["conv_w"][l], w["conv_b"][l][None], jnp.zeros((CONV_ROWS - 4, 2 * D_FF), F32)], axis=0)
    return dict(
        g1=w["norm1_g"][l][None], g2=w["norm2_g"][l][None], gm=w["mix_norm_g"][l][None],
        sng=w["sgu_norm_g"][l][None], wm=wm.astype(MXU_DT), wmt=jnp.swapaxes(wm, 1, 2).astype(MXU_DT),
        bias=jnp.repeat(jnp.transpose(w["sgu_b"][l]), HEAD_DIM, axis=1),
        wbd=wbd.astype(MXU_DT), scale=w["pool_scale"][l][None], cwb=cwb,
        w_in=w["w_in"][l], w_o=w["w_o"][l], w_up=w["w_up"][l], w_down=w["w_down"][l],
    )


def _local_step(x, tgt, w, late=None):
    gmat = _group_matrix()
    tri = _tri_matrix()
    trit = jnp.transpose(tri)
    saved = []
    for l in range(DEPTH):
        c = _layer_consts(w, l)
        h1, proj, proj_b, ya, yc = _in_proj_groups(x, c["g1"], c["w_in"], c["sng"], c["wm"], c["bias"], gmat,
                                                   c["wbd"], c["scale"])
        if l == 0 and late is not None:
            assert DEPTH == 2
            yb, rb, *gathered = _sb_fwd(proj_b, tri, carry_gather=late[0])
            w = dict(w)
            for name, arr in late[1](gathered).items():
                w[name] = [w[name][0], arr]
        else:
            yb, rb = _sb_fwd(proj_b, tri)
        x2, yn = _mix_out(ya, yb, yc, c["gm"], c["w_o"], x, gmat)
        h2, z, zc, f = _up_proj_gate(x2, c["g2"], c["w_up"], c["cwb"])
        saved.append(dict(c=c, x=x, proj=proj, proj_b=proj_b, h1=h1, ya=ya, yb=yb, yc=yc, rb=rb, x2=x2, yn=yn,
                          z=z, zc=zc, h2=h2, f=f))
        if l < DEPTH - 1:
            x = _mm_res(f, c["w_down"], x2, "down_proj")

    last = saved[-1]
    dx, d_final_g, loss8 = _down_proj_loss(last["f"], last["c"]["w_down"], last["x2"], w["final_g"][None], tgt)
    grads = {n: [None] * DEPTH for n in ("norm1_g", "w_in", "sgu_norm_g", "sgu_w", "sgu_b", "pool_w", "pool_scale",
                                         "mix_norm_g", "w_o", "norm2_g", "w_up", "conv_w", "conv_b", "w_down")}
    for l in reversed(range(DEPTH)):
        s = saved[l]
        c = s["c"]
        grads["w_down"][l] = _mm_tn(s["f"], dx, "down_proj_wgrad").reshape(N_CHIPS, D_FF // N_CHIPS, D_MODEL)
        dz, dcwb, dx2, dg2 = _gate_up_bwd(s["z"], s["zc"], c["cwb"], c["w_up"], c["w_down"], s["x2"], c["g2"], dx)
        grads["conv_w"][l] = dcwb[:3]
        grads["conv_b"][l] = dcwb[3]
        grads["w_up"][l] = _mm_tn(s["h2"], dz, "up_proj_wgrad", col_tiles=True)
        grads["norm2_g"][l] = dg2[0]
        grads["w_o"][l] = _mm_tn(s["yn"], dx2, "out_proj_wgrad").reshape(N_CHIPS, D_MODEL // N_CHIPS, D_MODEL)
        dya, dyb, dyc, dgm = _mix_out_bwd(dx2, c["w_o"], s["ya"], s["yb"], s["yc"], c["gm"], gmat)
        grads["mix_norm_g"][l] = dgm[0]
        dd, e, dwbd, dscale = _pool_bwd_a(s["proj"], dyc, c["wbd"], c["scale"])
        dp = _pool_bwd_b(dd, e)
        grads["pool_w"][l] = jnp.stack([dwbd[g * 64:(g + 1) * 64, g * 64:(g + 1) * 64] for g in range(4)])
        grads["pool_scale"][l] = dscale[0]
        dq, dk, dv = _sb_bwd(s["proj_b"], dyb, s["rb"], tri, trit)
        da, dwm, dbias, dsng = _sgu_bwd(s["proj"], dya, c["sng"], c["wm"], c["wmt"], c["bias"], gmat)
        grads["sgu_w"][l] = dwm
        grads["sgu_b"][l] = jnp.transpose(jnp.sum(dbias.reshape(CHUNK, 4, HEAD_DIM), axis=-1))
        grads["sgu_norm_g"][l] = dsng[0]
        dproj = jnp.concatenate([da, dq, dk.astype(MXU_DT), dv.astype(MXU_DT), dp], axis=1)
        dw_in = _mm_tn(s["h1"], dproj, "in_proj_wgrad")
        grads["w_in"][l] = jnp.transpose(dw_in.reshape(D_MODEL, N_CHIPS, IN_COLS // N_CHIPS), (1, 0, 2))
        dx, dg1 = _mm_nt_rmsbwd([(dproj, c["w_in"])], s["x"], c["g1"], dx2, "in_proj_bwd")
        grads["norm1_g"][l] = dg1[0]

    out = {n: jnp.stack(v) for n, v in grads.items()}
    out["final_g"] = d_final_g[0]
    return loss8[0, 0], dx, out


MESH = pl.DeviceIdType.MESH
ANY = pl.BlockSpec(memory_space=pl.ANY)


def _gather_plan(ins, outs, send_sems, recv_sems, layer, sender):
    n = len(ins)
    x, y, c = lax.axis_index("x"), lax.axis_index("y"), lax.axis_index("c")
    sibling = (x, y, 1 - c)
    my_chip = 2 * x + y
    chips = [(1 - x, y), (x, 1 - y), (1 - x, 1 - y)]
    ids = [2 * px + py for px, py in chips]

    def copy(a, k, chip, to, own=False):
        dst = outs[a].at[chip]
        return pltpu.make_async_remote_copy(
            src_ref=ins[a].at[layer] if own else dst, dst_ref=dst,
            send_sem=send_sems.at[a, k], recv_sem=recv_sems.at[a, k], device_id=to, device_id_type=MESH)

    sends = [copy(a, j, my_chip, (*chips[j], sender), own=True) for j in range(3) for a in range(n)]
    arrivals = [copy(a, j, ids[j], sibling) for j in range(3) for a in range(n)]
    forwards = [copy(a, 3 + j, ids[j], sibling) for j in range(3) for a in range(n)]
    return c, sends, arrivals, forwards


def _gather_shapes(shards):
    n = len(shards)
    return ([jax.ShapeDtypeStruct((N_CHIPS,) + s.shape[1:], s.dtype) for s in shards],
            [pltpu.SemaphoreType.DMA((n, 6)), pltpu.SemaphoreType.DMA((n, 6))])


def _all_gather(shards, layer, sender):
    n = len(shards)

    def body(*refs):
        c, sends, arrivals, forwards = _gather_plan(refs[:n], refs[n:2 * n], refs[2 * n], refs[2 * n + 1],
                                                    layer, sender)

        @pl.when(c == sender)
        def _():
            for cp in sends:
                cp.start()
            for arrived, onward in zip(arrivals, forwards):
                arrived.wait_recv()
                onward.start()
            for cp in sends + forwards:
                cp.wait_send()

        @pl.when(c != sender)
        def _():
            for cp in forwards:
                cp.wait_recv()

    out_shape, sems = _gather_shapes(shards)
    return _call(body, name="weight_all_gather", out_shape=out_shape, in_specs=[ANY] * n, out_specs=[ANY] * n,
                 scratch_shapes=sems)(*shards)


def _row_tile(r):
    return r if r <= 704 else 256


def _grad_swap(bigs, sp):
    n = len(bigs)

    def body(*refs):
        ins, outs = refs[:n + 1], refs[n + 1:2 * n + 2]
        send_sems, recv_sems = refs[2 * n + 2:]
        x, y, c = lax.axis_index("x"), lax.axis_index("y"), lax.axis_index("c")
        srcs = [ins[a].at[1 - c] for a in range(n)] + [ins[n].at[:, pl.ds((1 - c) * SP_HALF, SP_HALF), :]]
        copies = [pltpu.make_async_remote_copy(src_ref=srcs[a], dst_ref=outs[a], send_sem=send_sems.at[a],
                                               recv_sem=recv_sems.at[a], device_id=(x, y, 1 - c),
                                               device_id_type=MESH) for a in range(n + 1)]
        for cp in copies:
            cp.start()
        for cp in copies:
            cp.wait()

    shapes = [jax.ShapeDtypeStruct(b.shape[1:], b.dtype) for b in bigs]
    shapes.append(jax.ShapeDtypeStruct((N_CHIPS, SP_HALF, D_MODEL), sp.dtype))
    return _call(
        body, name="grad_swap_cores", out_shape=shapes, in_specs=[ANY] * (n + 1), out_specs=[ANY] * (n + 1),
        scratch_shapes=[pltpu.SemaphoreType.DMA((n + 1,)), pltpu.SemaphoreType.DMA((n + 1,))],
    )(*bigs, sp)


def _pair_add(g, r, c_arr, name, out_dtype):
    _, k, rr, cc = g.shape
    tr = _row_tile(rr)

    def body(c_ref, g_ref, r_ref, o_ref):
        o_ref[...] = (g_ref[...] + r_ref[...]).astype(o_ref.dtype)

    spec = pl.BlockSpec((1, tr, cc), lambda kk, i, c_ref: (kk, i, 0))
    grid_spec = pltpu.PrefetchScalarGridSpec(
        num_scalar_prefetch=1, grid=(k, rr // tr),
        in_specs=[pl.BlockSpec((None, 1, tr, cc), lambda kk, i, c_ref: (c_ref[0], kk, i, 0)), spec], out_specs=spec)
    return _call(body, name=name, grid_spec=grid_spec, out_shape=jax.ShapeDtypeStruct((k, rr, cc), out_dtype),
                 compiler_params=_params("parallel", "parallel"))(c_arr, g, r)


def _pair_add_small(sp, r, c_arr):
    def body(c_ref, g_ref, r_ref, o_ref):
        o_ref[...] = g_ref[...] + r_ref[...]

    spec = pl.BlockSpec((1, SP_HALF, D_MODEL), lambda kk, c_ref: (kk, 0, 0))
    grid_spec = pltpu.PrefetchScalarGridSpec(
        num_scalar_prefetch=1, grid=(N_CHIPS,),
        in_specs=[pl.BlockSpec((1, SP_HALF, D_MODEL), lambda kk, c_ref: (kk, c_ref[0], 0)), spec], out_specs=spec)
    return _call(body, name="grad_add_cores_small", grid_spec=grid_spec,
                 out_shape=jax.ShapeDtypeStruct(r.shape, F32), compiler_params=_params("parallel"))(c_arr, sp, r)


def _grad_exchange(hs):
    n = len(hs)

    def body(*refs):
        ins, outs = refs[:n], refs[n:2 * n]
        send_sems, recv_sems = refs[2 * n:]
        x, y, c = lax.axis_index("x"), lax.axis_index("y"), lax.axis_index("c")
        my_chip = 2 * x + y
        chips = [(1 - x, y), (x, 1 - y), (1 - x, 1 - y)]

        def copy(a, k, src_chip, dst_chip):
            px, py = chips[k]
            return pltpu.make_async_remote_copy(
                src_ref=ins[a].at[src_chip], dst_ref=outs[a].at[dst_chip], send_sem=send_sems.at[a, k],
                recv_sem=recv_sems.at[a, k], device_id=(px, py, c), device_id_type=MESH)

        sends = [copy(a, k, 2 * chips[k][0] + chips[k][1], my_chip) for k in range(3) for a in range(n)]
        for cp in sends:
            cp.start()
        for k in range(3):
            for a in range(n):
                copy(a, k, my_chip, 2 * chips[k][0] + chips[k][1]).wait_recv()
        for cp in sends:
            cp.wait_send()

    return _call(
        body, name="grad_exchange_chips", out_shape=[jax.ShapeDtypeStruct(h.shape, h.dtype) for h in hs],
        in_specs=[ANY] * n, out_specs=[ANY] * n,
        scratch_shapes=[pltpu.SemaphoreType.DMA((n, 3)), pltpu.SemaphoreType.DMA((n, 3))],
    )(*hs)


def _sum_chips(a, c_arr, name):
    _, r, cc = a.shape
    tr = _row_tile(r)

    def body(c_ref, a_ref, o_ref):
        o_ref[...] = ((a_ref[0].astype(F32) + a_ref[1].astype(F32)) + a_ref[2].astype(F32)) + a_ref[3].astype(F32)

    grid_spec = pltpu.PrefetchScalarGridSpec(
        num_scalar_prefetch=1, grid=(r // tr,),
        in_specs=[pl.BlockSpec((N_CHIPS, tr, cc), lambda i, c_ref: (0, i, 0))],
        out_specs=pl.BlockSpec((None, tr, cc), lambda i, c_ref: (c_ref[0], i, 0)))
    return _call(body, name=name, grid_spec=grid_spec, out_shape=jax.ShapeDtypeStruct((2, r, cc), F32),
                 compiler_params=_params("parallel"))(c_arr, a)


def _grad_share(bufs):
    n = len(bufs)

    def body(*refs):
        outs = refs[n:2 * n]
        send_sems, recv_sems = refs[2 * n:]
        x, y, c = lax.axis_index("x"), lax.axis_index("y"), lax.axis_index("c")
        copies = [pltpu.make_async_remote_copy(src_ref=outs[a].at[c], dst_ref=outs[a].at[c], send_sem=send_sems.at[a],
                                               recv_sem=recv_sems.at[a], device_id=(x, y, 1 - c),
                                               device_id_type=MESH) for a in range(n)]
        for cp in copies:
            cp.start()
        for a in range(n):
            pltpu.make_async_remote_copy(src_ref=outs[a].at[c], dst_ref=outs[a].at[1 - c], send_sem=send_sems.at[a],
                                         recv_sem=recv_sems.at[a], device_id=(x, y, 1 - c),
                                         device_id_type=MESH).wait_recv()
        for cp in copies:
            cp.wait_send()

    return _call(
        body, name="grad_share_cores", out_shape=[jax.ShapeDtypeStruct(b.shape, b.dtype) for b in bufs],
        in_specs=[ANY] * n, out_specs=[ANY] * n, input_output_aliases={a: a for a in range(n)},
        scratch_shapes=[pltpu.SemaphoreType.DMA((n,)), pltpu.SemaphoreType.DMA((n,))],
    )(*bufs)


def _adamw_math(g_ref, w_ref, m_ref, v_ref, d_ref, nm_ref, nv_ref):
    gv = g_ref[...]
    nm = ADAM_B1 * m_ref[...] + (1.0 - ADAM_B1) * gv
    nv = ADAM_B2 * v_ref[...] + (1.0 - ADAM_B2) * (gv * gv)
    m_hat = nm / (1.0 - ADAM_B1 ** ADAM_STEP)
    v_hat = nv / (1.0 - ADAM_B2 ** ADAM_STEP)
    d_ref[...] = -ADAM_LR * (m_hat / (jnp.sqrt(v_hat) + ADAM_EPS) + ADAM_WD * w_ref[...])
    nm_ref[...] = nm
    nv_ref[...] = nv


def _adamw_big(g, w, m, v, name):
    d, r, c = g.shape
    tr = r if r <= 704 else 256
    spec = pl.BlockSpec((1, tr, c), lambda l, i: (l, i, 0))

    def body(*refs):
        _adamw_math(*refs)

    shp = jax.ShapeDtypeStruct(g.shape, F32)
    return _call(body, name=name, grid=(d, r // tr), in_specs=[spec] * 4, out_specs=[spec] * 3,
                 out_shape=[shp, shp, shp], compiler_params=_params("parallel", "parallel"))(g, w, m, v)


def _adamw_small(gs, ws, ms, vs):
    n = len(gs)

    def body(*refs):
        ins, outs = refs[:4 * n], refs[4 * n:]
        for k in range(n):
            _adamw_math(ins[k], ins[n + k], ins[2 * n + k], ins[3 * n + k], outs[k], outs[n + k], outs[2 * n + k])

    shp = [jax.ShapeDtypeStruct(g.shape, F32) for g in gs]
    res = _call(body, name="adamw_small", out_shape=shp * 3)(*gs, *ws, *ms, *vs)
    return res[:n], res[n:2 * n], res[2 * n:]


def _rows(a, rows):
    flat = a.reshape(-1)
    return jnp.pad(flat, (0, rows * D_MODEL - flat.shape[0])).reshape(rows, D_MODEL)


def _small_rows(p, extra=None):
    parts = [p[n].reshape(-1) for n in SMALL_NAMES]
    if extra is not None:
        parts.append(extra.reshape(-1))
    flat = jnp.concatenate(parts)
    return jnp.pad(flat, (0, ROWS_SMALL * D_MODEL - flat.shape[0])).reshape(ROWS_SMALL, D_MODEL)


CONV_SHARD = (DEPTH, 3, 2 * D_FF // N_CHIPS)
N_CONV_SHARD = DEPTH * 3 * (2 * D_FF // N_CHIPS)


def _small_pack(g, loss):
    conv = jnp.transpose(g["conv_w"].reshape(DEPTH, 3, N_CHIPS, 2 * D_FF // N_CHIPS), (2, 0, 1, 3))
    conv = jnp.stack([_rows(conv[k], ROWS_CONV) for k in range(N_CHIPS)])
    small = jnp.broadcast_to(_small_rows(g, loss), (N_CHIPS, ROWS_SMALL, D_MODEL))
    return jnp.concatenate([conv, small], axis=1)


def _unpack_small(pack):
    out = {"conv_w": pack[:ROWS_CONV].reshape(-1)[:N_CONV_SHARD].reshape(CONV_SHARD)}
    flat = pack[ROWS_CONV:].reshape(-1)
    k = 0
    for name in SMALL_NAMES:
        shape = SMALL_SHAPES[name]
        n = 1
        for d in shape:
            n *= d
        out[name] = flat[k:k + n].reshape(shape)
        k += n
    out["extra"] = flat[k]
    return out


def _assemble_layer(gathered, shards, layer):
    my_chip = 2 * lax.axis_index("x") + lax.axis_index("y")
    w_in, w_o, w_up, w_down = [lax.dynamic_update_index_in_dim(got, own[layer], my_chip, 0)
                               for got, own in zip(gathered, shards)]

    def by_cols(a):
        k, r, wd = a.shape
        return jnp.transpose(a, (1, 0, 2)).reshape(r, k * wd)

    return dict(w_in=by_cols(w_in), w_o=w_o.reshape(-1, D_MODEL), w_up=by_cols(w_up),
                w_down=w_down.reshape(-1, D_MODEL))


def _gather_weights(p):
    shards = [p[n].astype(jnp.bfloat16) for n in BIG_NAMES[:4]]
    conv_all = p["conv_w"].reshape(1, -1, p["conv_w"].shape[-1])
    got = _all_gather(shards + [conv_all], 0, 0)
    my_chip = 2 * lax.axis_index("x") + lax.axis_index("y")
    conv = lax.dynamic_update_index_in_dim(got[4], conv_all[0], my_chip, 0)
    conv = jnp.transpose(conv.reshape((N_CHIPS,) + CONV_SHARD), (1, 2, 0, 3)).reshape(DEPTH, 3, 2 * D_FF)
    full = {n: [a, None] for n, a in _assemble_layer(got[:4], shards, 0).items()}
    full["conv_w"] = conv
    return full, (shards, lambda gathered: _assemble_layer(gathered, shards, 1))


def _reduce_grads(grads, loss, c):
    bigs = [grads[n] for n in BIG_NAMES[:4]]
    sp = _small_pack(grads, loss)
    c_arr = jnp.reshape(c, (1,)).astype(jnp.int32)
    got = _grad_swap(bigs, sp)
    pair = [_pair_add(bigs[a], got[a], c_arr, "grad_add_cores_" + BIG_NAMES[a], ICI_DT) for a in range(4)]
    pair.append(_pair_add_small(sp, got[4], c_arr))
    my_chip = 2 * lax.axis_index("x") + lax.axis_index("y")
    parts = [lax.dynamic_update_index_in_dim(got_k, lax.dynamic_index_in_dim(own, my_chip, 0, keepdims=False),
                                             my_chip, 0) for got_k, own in zip(_grad_exchange(pair), pair)]
    total = [_sum_chips(parts[a], c_arr, "grad_sum_chips_" + (BIG_NAMES[:4] + ("small",))[a]) for a in range(5)]
    shared = _grad_share(total)
    out = dict(zip(BIG_NAMES[:4], shared[:4]))
    out.update(_unpack_small(shared[4].reshape(2 * SP_HALF, D_MODEL)))
    return out


def kernel(x, norm1_g, w_in, sgu_norm_g, sgu_w, sgu_b, pool_w, pool_scale, mix_norm_g, w_o, norm2_g, w_up, conv_w, conv_b, w_down, final_g, loss_target, m_norm1_g, m_w_in, m_sgu_norm_g, m_sgu_w, m_sgu_b, m_pool_w, m_pool_scale, m_mix_norm_g, m_w_o, m_norm2_g, m_w_up, m_conv_w, m_conv_b, m_w_down, m_final_g, v_norm1_g, v_w_in, v_sgu_norm_g, v_sgu_w, v_sgu_b, v_pool_w, v_pool_scale, v_mix_norm_g, v_w_o, v_norm2_g, v_w_up, v_conv_w, v_conv_b, v_w_down, v_final_g):
    names = ("norm1_g", "w_in", "sgu_norm_g", "sgu_w", "sgu_b", "pool_w", "pool_scale", "mix_norm_g", "w_o",
             "norm2_g", "w_up", "conv_w", "conv_b", "w_down", "final_g")
    p = dict(zip(names, (norm1_g, w_in, sgu_norm_g, sgu_w, sgu_b, pool_w, pool_scale, mix_norm_g, w_o, norm2_g,
                         w_up, conv_w, conv_b, w_down, final_g)))
    pm = dict(zip(names, (m_norm1_g, m_w_in, m_sgu_norm_g, m_sgu_w, m_sgu_b, m_pool_w, m_pool_scale, m_mix_norm_g,
                          m_w_o, m_norm2_g, m_w_up, m_conv_w, m_conv_b, m_w_down, m_final_g)))
    pv = dict(zip(names, (v_norm1_g, v_w_in, v_sgu_norm_g, v_sgu_w, v_sgu_b, v_pool_w, v_pool_scale, v_mix_norm_g,
                          v_w_o, v_norm2_g, v_w_up, v_conv_w, v_conv_b, v_w_down, v_final_g)))
    c = lax.axis_index("c")
    gathered, late = _gather_weights(p)
    full = dict(p)
    full.update(gathered)

    loss, dx, grads = _local_step(x[0], loss_target[0], full, late)

    g = _reduce_grads(grads, loss, c)
    d, nm, nv = {}, {}, {}
    for n in BIG_NAMES:
        d[n], nm[n], nv[n] = _adamw_big(g[n], p[n], pm[n], pv[n], "adamw_" + n)

    def two_d(a):
        return a.reshape(1, -1) if a.ndim == 1 else a

    ds, ms, vs = _adamw_small([two_d(g[n]) for n in SMALL_NAMES], [two_d(p[n]) for n in SMALL_NAMES],
                              [two_d(pm[n]) for n in SMALL_NAMES], [two_d(pv[n]) for n in SMALL_NAMES])
    for k, n in enumerate(SMALL_NAMES):
        d[n], nm[n], nv[n] = (a.reshape(p[n].shape) for a in (ds[k], ms[k], vs[k]))
    return (g["extra"], dx[None], *[g[n] for n in names], *[d[n] for n in names], *[nm[n] for n in names],
            *[nv[n] for n in names])
```

```python
import functools

import jax
import jax.numpy as jnp
from jax import lax
from jax.experimental import pallas as pl
from jax.experimental.pallas import tpu as pltpu

F32 = jnp.float32
MXU_DT = jnp.bfloat16

D_MODEL = 1024
DEPTH = 2
HEAD_DIM = 64
W_A = 256
W_B = 512
W_C = 256
IN_COLS = 2 * W_A + 3 * W_B + W_C
CHUNK = 128
POOL_WINDOWS = (2, 4, 8, 16)
D_FF = 2816
EPS = 1e-6
N_CHIPS = 4

ADAM_LR = 0.001
ADAM_B1 = 0.9
ADAM_B2 = 0.999
ADAM_EPS = 1e-08
ADAM_WD = 0.01
ADAM_STEP = 10

LANES = 128
TQ = 256
TK = 256
TM = 256
TM_MM = 512
HALO = 16
VMEM_LIMIT = 56 * 1024 * 1024

ROWS_CONV = 16
ROWS_SMALL = 240
SP_HALF = (ROWS_CONV + ROWS_SMALL) // 2
ICI_DT = jnp.bfloat16

BIG_NAMES = ("w_in", "w_o", "w_up", "w_down", "conv_w")
SMALL_NAMES = ("norm1_g", "sgu_norm_g", "sgu_w", "sgu_b", "pool_w", "pool_scale",
               "mix_norm_g", "norm2_g", "conv_b", "final_g")
SMALL_SHAPES = {
    "norm1_g": (DEPTH, D_MODEL), "sgu_norm_g": (DEPTH, W_A), "sgu_w": (DEPTH, 4, CHUNK, CHUNK),
    "sgu_b": (DEPTH, 4, CHUNK), "pool_w": (DEPTH, 4, 64, 64), "pool_scale": (DEPTH, W_C),
    "mix_norm_g": (DEPTH, D_MODEL), "norm2_g": (DEPTH, D_MODEL), "conv_b": (DEPTH, 2 * D_FF),
    "final_g": (D_MODEL,),
}


def _call(body, **kw):
    return pl.pallas_call(body, **kw)


def _params(*sem):
    return pltpu.CompilerParams(dimension_semantics=sem, vmem_limit_bytes=VMEM_LIMIT)


def _dot(a, b):
    return jnp.dot(a, b, preferred_element_type=F32)


def _dot_nt(a, b):
    return lax.dot_general(a, b, (((1,), (1,)), ((), ())), preferred_element_type=F32)


def _dot_tn(a, b):
    return lax.dot_general(a, b, (((0,), (0,)), ((), ())), preferred_element_type=F32)


def _group_mean(sq, gmat):
    sqb = sq.astype(MXU_DT)
    cols = [_dot(sqb[:, b * LANES:(b + 1) * LANES], gmat) for b in range(sq.shape[1] // LANES)]
    return cols[0] if len(cols) == 1 else jnp.concatenate(cols, axis=-1)


def _group_matrix():
    r = jnp.arange(LANES)
    return jnp.where((r[:, None] // HEAD_DIM) == (r[None, :] // HEAD_DIM), 1.0 / HEAD_DIM, 0.0).astype(MXU_DT)


def _tile(n):
    return max(t for t in range(LANES, 1536 + 1, LANES) if n % t == 0)


def _row_spec(tm, cols, col_block=0):
    return pl.BlockSpec((tm, cols), lambda i, cb=col_block: (i, cb))


def _full_spec(shape):
    nd = len(shape)
    return pl.BlockSpec(shape, lambda *_: (0,) * nd)


def _mm_res(a, w, res, name):
    S, K = a.shape
    N = w.shape[1]
    tm = TM_MM

    def body(a_ref, w_ref, r_ref, o_ref):
        o_ref[...] = r_ref[...] + _dot(a_ref[...], w_ref[...])

    return _call(
        body, name=name, grid=(S // tm,),
        in_specs=[_row_spec(tm, K), _full_spec((K, N)), _row_spec(tm, N)],
        out_specs=_row_spec(tm, N),
        out_shape=jax.ShapeDtypeStruct((S, N), F32),
        compiler_params=_params("parallel"),
    )(a, w, res)


def _mm_tn(a, b, name, col_tiles=False):
    S, K1 = a.shape
    N = b.shape[1]
    ts = TM_MM
    tk = _tile(K1)
    tn = _tile(N)
    if col_tiles:
        out_spec = pl.BlockSpec((None, tk, tn), lambda m, n, s: (n, m, 0))
        out_shape = jax.ShapeDtypeStruct((N // tn, K1, tn), F32)
    else:
        out_spec = pl.BlockSpec((tk, tn), lambda m, n, s: (m, n))
        out_shape = jax.ShapeDtypeStruct((K1, N), F32)

    def body(a_ref, b_ref, o_ref):
        @pl.when(pl.program_id(2) == 0)
        def _():
            o_ref[...] = jnp.zeros_like(o_ref)

        o_ref[...] += _dot_tn(a_ref[...], b_ref[...].astype(MXU_DT))

    return _call(
        body, name=name, grid=(K1 // tk, N // tn, S // ts),
        in_specs=[pl.BlockSpec((ts, tk), lambda m, n, s: (s, m)),
                  pl.BlockSpec((ts, tn), lambda m, n, s: (s, n))],
        out_specs=out_spec, out_shape=out_shape,
        compiler_params=_params("parallel", "parallel", "arbitrary"),
    )(a, b)


def _mm_nt_rmsbwd(pairs, x, g, dres, name):
    S, D = x.shape
    tm = TM
    n = len(pairs)

    def body(*refs):
        a_refs = refs[:n]
        w_refs = refs[n:2 * n]
        x_ref, g_ref, r_ref, dx_ref, dg_ref = refs[2 * n:]
        dh = _dot_nt(a_refs[0][...], w_refs[0][...])
        for k in range(1, n):
            dh += _dot_nt(a_refs[k][...], w_refs[k][...])
        xv = x_ref[...]
        r = lax.rsqrt(jnp.mean(xv * xv, axis=-1, keepdims=True) + EPS)
        xhat = xv * r

        @pl.when(pl.program_id(0) == 0)
        def _():
            dg_ref[...] = jnp.zeros_like(dg_ref)

        dg_ref[...] += jnp.sum(dh * xhat, axis=0, keepdims=True)
        dxh = dh * g_ref[...]
        dx_ref[...] = r_ref[...] + r * (dxh - xhat * jnp.mean(dxh * xhat, axis=-1, keepdims=True))

    in_specs = ([_row_spec(tm, a.shape[1]) for a, _ in pairs] + [_full_spec(w.shape) for _, w in pairs]
                + [_row_spec(tm, D), _full_spec((1, D)), _row_spec(tm, D)])
    return _call(
        body, name=name, grid=(S // tm,), in_specs=in_specs,
        out_specs=[_row_spec(tm, D), _full_spec((1, D))],
        out_shape=[jax.ShapeDtypeStruct((S, D), F32), jax.ShapeDtypeStruct((1, D), F32)],
        compiler_params=_params("arbitrary"),
    )(*[a for a, _ in pairs], *[w for _, w in pairs], x, g, dres)


def _down_proj_loss(a, w, res, g, tgt):
    S, D = res.shape
    K = a.shape[1]
    tm = TM

    def body(a_ref, w_ref, res_ref, g_ref, t_ref, dx_ref, dg_ref, l_ref):
        xv = res_ref[...] + _dot(a_ref[...], w_ref[...])
        r = lax.rsqrt(jnp.mean(xv * xv, axis=-1, keepdims=True) + EPS)
        xhat = xv * r
        diff = xhat * g_ref[...] - t_ref[...]

        @pl.when(pl.program_id(0) == 0)
        def _():
            dg_ref[...] = jnp.zeros_like(dg_ref)
            l_ref[...] = jnp.zeros_like(l_ref)

        l_ref[...] += jnp.full(l_ref.shape, 0.5 * jnp.sum(jnp.mean(diff * diff, axis=-1, keepdims=True)), F32)
        dout = diff * (1.0 / D)
        dg_ref[...] += jnp.sum(dout * xhat, axis=0, keepdims=True)
        dxh = dout * g_ref[...]
        dx_ref[...] = r * (dxh - xhat * jnp.mean(dxh * xhat, axis=-1, keepdims=True))

    return _call(
        body, name="down_proj_loss", grid=(S // tm,),
        in_specs=[_row_spec(tm, K), _full_spec((K, D)), _row_spec(tm, D), _full_spec((1, D)), _row_spec(tm, D)],
        out_specs=[_row_spec(tm, D), _full_spec((1, D)), _full_spec((8, LANES))],
        out_shape=[jax.ShapeDtypeStruct((S, D), F32), jax.ShapeDtypeStruct((1, D), F32),
                   jax.ShapeDtypeStruct((8, LANES), F32)],
        compiler_params=_params("arbitrary"),
    )(a, w, res, g, tgt)


def _mix_out(ya, yb, yc, gm, wo, x, gmat):
    S = x.shape[0]
    tm = TM

    def body(ya_ref, yb_ref, yc_ref, gm_ref, wo_ref, x_ref, gmat_ref, x2_ref, yn_ref):
        y = jnp.concatenate([ya_ref[...], yb_ref[...], yc_ref[...]], axis=-1)
        r = lax.rsqrt(_group_mean(y * y, gmat_ref[...]) + EPS)
        yn = (y * r * gm_ref[...]).astype(MXU_DT)
        yn_ref[...] = yn
        x2_ref[...] = x_ref[...] + _dot(yn, wo_ref[...])

    return _call(
        body, name="mix_out", grid=(S // tm,),
        in_specs=[_row_spec(tm, W_A), _row_spec(tm, W_B), _row_spec(tm, W_C), _full_spec((1, D_MODEL)),
                  _full_spec((D_MODEL, D_MODEL)), _row_spec(tm, D_MODEL), _full_spec((LANES, LANES))],
        out_specs=[_row_spec(tm, D_MODEL), _row_spec(tm, D_MODEL)],
        out_shape=[jax.ShapeDtypeStruct((S, D_MODEL), F32), jax.ShapeDtypeStruct((S, D_MODEL), MXU_DT)],
        compiler_params=_params("parallel"),
    )(ya, yb, yc, gm, wo, x, gmat)


def _mix_out_bwd(dx2, wo, ya, yb, yc, gm, gmat):
    S = dx2.shape[0]
    tm = TM

    def body(dx2_ref, wo_ref, ya_ref, yb_ref, yc_ref, gm_ref, gmat_ref, dya_ref, dyb_ref, dyc_ref, dgm_ref):
        dyn = _dot_nt(dx2_ref[...].astype(MXU_DT), wo_ref[...])
        y = jnp.concatenate([ya_ref[...], yb_ref[...], yc_ref[...]], axis=-1)
        r = lax.rsqrt(_group_mean(y * y, gmat_ref[...]) + EPS)
        yhat = y * r

        @pl.when(pl.program_id(0) == 0)
        def _():
            dgm_ref[...] = jnp.zeros_like(dgm_ref)

        dgm_ref[...] += jnp.sum(dyn * yhat, axis=0, keepdims=True)
        dyh = dyn * gm_ref[...]
        dy = r * (dyh - yhat * _group_mean(dyh * yhat, gmat_ref[...]))
        dya_ref[...] = dy[:, :W_A]
        dyb_ref[...] = dy[:, W_A:W_A + W_B]
        dyc_ref[...] = dy[:, W_A + W_B:]

    return _call(
        body, name="mix_out_bwd", grid=(S // tm,),
        in_specs=[_row_spec(tm, D_MODEL), _full_spec((D_MODEL, D_MODEL)), _row_spec(tm, W_A), _row_spec(tm, W_B),
                  _row_spec(tm, W_C), _full_spec((1, D_MODEL)), _full_spec((LANES, LANES))],
        out_specs=[_row_spec(tm, W_A), _row_spec(tm, W_B), _row_spec(tm, W_C), _full_spec((1, D_MODEL))],
        out_shape=[jax.ShapeDtypeStruct((S, W_A), F32), jax.ShapeDtypeStruct((S, W_B), F32),
                   jax.ShapeDtypeStruct((S, W_C), F32), jax.ShapeDtypeStruct((1, D_MODEL), F32)],
        compiler_params=_params("arbitrary"),
    )(dx2, wo, ya, yb, yc, gm, gmat)


_SQRT_HALF = 0.7071067811865476
_INV_SQRT_2PI = 0.3989422804014327


def _sgu_common(a, sng, wm_ref, bias, gmat):
    phi = 0.5 * (1.0 + lax.erf(a * _SQRT_HALF))
    ga = a * phi
    u = ga[:, :W_A]
    v = ga[:, W_A:]
    r = lax.rsqrt(_group_mean(v * v, gmat) + EPS)
    vhat = v * r
    vn = (vhat * sng).astype(MXU_DT)
    head = lax.broadcasted_iota(jnp.int32, (CHUNK, W_A), 1) // HEAD_DIM
    rows = []
    for c in range(a.shape[0] // CHUNK):
        vc = vn[c * CHUNK:(c + 1) * CHUNK]
        s = bias
        for h in range(4):
            s = s + jnp.where(head == h, _dot(wm_ref[h], vc), 0.0)
        rows.append(s)
    s = jnp.concatenate(rows, axis=0)
    return phi, u, r, vhat, vn, s


def _tril_weights(sgu_w_l):
    t = jnp.arange(CHUNK)
    return jnp.where((t[None, :] <= t[:, None])[None], sgu_w_l, 0.0)


def _sgu_bwd(proj, dy, sng, wm, wmt, bias, gmat):
    S = proj.shape[0]
    tm = TM

    def body(a_ref, dy_ref, sng_ref, wm_ref, wmt_ref, b_ref, gmat_ref, da_ref, dw_ref, db_ref, dsng_ref):
        a = a_ref[...]
        dy = dy_ref[...]
        gmat = gmat_ref[...]
        sng = sng_ref[...]
        phi, u, r, vhat, vn, s = _sgu_common(a, sng, wm_ref, b_ref[...], gmat)
        du = dy * s
        ds = dy * u

        @pl.when(pl.program_id(0) == 0)
        def _():
            dw_ref[...] = jnp.zeros_like(dw_ref)
            db_ref[...] = jnp.zeros_like(db_ref)
            dsng_ref[...] = jnp.zeros_like(dsng_ref)

        head = lax.broadcasted_iota(jnp.int32, (CHUNK, W_A), 1) // HEAD_DIM
        tt = lax.broadcasted_iota(jnp.int32, (CHUNK, CHUNK), 0)
        ss = lax.broadcasted_iota(jnp.int32, (CHUNK, CHUNK), 1)
        rows = []
        for c in range(tm // CHUNK):
            dsc = ds[c * CHUNK:(c + 1) * CHUNK]
            vc = vn[c * CHUNK:(c + 1) * CHUNK]
            db_ref[...] += dsc
            dsb = dsc.astype(MXU_DT)
            dvn = jnp.zeros((CHUNK, W_A), F32)
            for h in range(4):
                dvn = dvn + jnp.where(head == h, _dot(wmt_ref[h], dsb), 0.0)
                dsh = jnp.where(head == h, dsc, 0.0).astype(MXU_DT)
                dw_ref[h] += jnp.where(ss <= tt, _dot_nt(dsh, vc), 0.0)
            rows.append(dvn)
        dvn = jnp.concatenate(rows, axis=0)
        dsng_ref[...] += jnp.sum(dvn * vhat, axis=0, keepdims=True)
        dvh = dvn * sng
        dv = r * (dvh - vhat * _group_mean(dvh * vhat, gmat))
        dga = jnp.concatenate([du, dv], axis=-1)
        dgelu = phi + a * (_INV_SQRT_2PI * jnp.exp(-0.5 * a * a))
        da_ref[...] = (dga * dgelu).astype(da_ref.dtype)

    return _call(
        body, name="sgu_bwd", grid=(S // tm,),
        in_specs=[_row_spec(tm, 2 * W_A), _row_spec(tm, W_A), _full_spec((1, W_A)), _full_spec((4, CHUNK, CHUNK)),
                  _full_spec((4, CHUNK, CHUNK)), _full_spec((CHUNK, W_A)), _full_spec((LANES, LANES))],
        out_specs=[_row_spec(tm, 2 * W_A), _full_spec((4, CHUNK, CHUNK)), _full_spec((CHUNK, W_A)),
                   _full_spec((1, W_A))],
        out_shape=[jax.ShapeDtypeStruct((S, 2 * W_A), MXU_DT), jax.ShapeDtypeStruct((4, CHUNK, CHUNK), F32),
                   jax.ShapeDtypeStruct((CHUNK, W_A), F32), jax.ShapeDtypeStruct((1, W_A), F32)],
        compiler_params=_params("arbitrary"),
    )(proj, dy, sng, wm, wmt, bias, gmat)


HG = 4
LW = HG * HEAD_DIM
Q_BLK0 = (2 * W_A) // LW
K_BLK0 = Q_BLK0 + W_B // LW
V_BLK0 = K_BLK0 + W_B // LW
N_GROUPS = W_B // LW
EXP_IS_ZERO_BELOW = -120.0


def _tri_matrix():
    r = jnp.arange(TK)
    return (r[:, None] > r[None, :]).astype(MXU_DT)


def _stack_heads(a):
    head = lax.broadcasted_iota(jnp.int32, a.shape, 1) // HEAD_DIM
    return jnp.concatenate([jnp.where(head == h, a, 0.0) for h in range(HG)], axis=0).astype(MXU_DT)


def _unstack_heads(a):
    head = lax.broadcasted_iota(jnp.int32, (TQ, LW), 1) // HEAD_DIM
    out = a[:TQ]
    for h in range(1, HG):
        out = jnp.where(head == h, a[h * TQ:(h + 1) * TQ], out)
    return out


def _sb_scores(q2, kj, tri, key_offset):
    z = _dot_nt(q2, kj)
    sp = jnp.log(1.0 + jnp.exp(-jnp.abs(z)))
    lsp = jnp.minimum(z, 0.0) - sp
    lsm = lsp - z
    msk = None
    if key_offset is not None:
        row = lax.broadcasted_iota(jnp.int32, z.shape, 0) & (TQ - 1)
        col = lax.broadcasted_iota(jnp.int32, z.shape, 1) + key_offset
        msk = col < row
        lsm = jnp.where(msk, lsm, 0.0)
    tail = _dot(lsm.astype(MXU_DT), tri)
    return lsp, lsm, tail, msk


def _sb_fwd(proj_b, tri, carry_gather=None):
    S = proj_b.shape[0]
    nq = S // TQ
    kpq = TQ // TK
    assert S // TK < LANES
    shards = list(carry_gather or [])
    ng = len(shards)

    def body(*refs):
        q_ref, k_ref, v_ref, tri_ref = refs[:4]
        o_ref, rb_ref = refs[4 + ng:6 + ng]
        acc_ref = refs[6 + 2 * ng]
        i = pl.program_id(1)
        if ng:
            step = pl.program_id(0) * nq + i
            sender = 1
            c, sends, arrivals, forwards = _gather_plan(refs[4:4 + ng], refs[6 + ng:6 + 2 * ng], refs[7 + 2 * ng],
                                                        refs[8 + 2 * ng], 1, sender)

            @pl.when(jnp.logical_and(step == 0, c == sender))
            def _():
                for cp in sends:
                    cp.start()

            @pl.when(jnp.logical_and(step == N_GROUPS * nq - max(nq // 8, 1), c == sender))
            def _():
                for arrived, onward in zip(arrivals, forwards):
                    arrived.wait_recv()
                    onward.start()


        lane2 = lax.broadcasted_iota(jnp.int32, (HG * TQ, LANES), 1)
        q2 = _stack_heads(q_ref[...].astype(F32) * (HEAD_DIM ** -0.5))
        tri = tri_ref[...]
        rb_ref[...] = jnp.zeros_like(rb_ref)

        def block(j, run, key_offset=None, first=False):
            start = pl.multiple_of(j * TK, TK)
            kj = k_ref[pl.ds(start, TK), :]
            vj = v_ref[pl.ds(start, TK), :]
            lsp, lsm, tail, msk = _sb_scores(q2, kj, tri, key_offset)
            rb_ref[...] = jnp.where(lane2 == j, run, rb_ref[...])
            att = jnp.exp(lsp + tail + run)
            if msk is not None:
                att = jnp.where(msk, att, 0.0)
            pv = _dot(att.astype(MXU_DT), vj)
            if first:
                acc_ref[...] = pv
            else:
                acc_ref[...] += pv
            return run + tail[:, :1] + lsm[:, :1]

        run = jnp.zeros((HG * TQ, 1), F32)
        for d in reversed(range(kpq)):
            run = block(i * kpq + d, run, key_offset=d * TK, first=(d == kpq - 1))
        past = i * kpq

        def alive(run):
            return (jnp.max(run) > EXP_IS_ZERO_BELOW).astype(jnp.int32)

        def walk(carry):
            n, run, _ = carry
            run = block(past - 1 - n, run)
            return n + 1, run, alive(run)

        n, _, _ = lax.while_loop(lambda s: jnp.logical_and(s[0] < past, s[2] > 0), walk,
                                 (jnp.int32(0), run, alive(run)))
        rb_ref[...] = jnp.where(lane2 == LANES - 1, n.astype(F32), rb_ref[...])
        o_ref[...] = _unstack_heads(acc_ref[...])

        if ng:
            @pl.when(jnp.logical_and(step == N_GROUPS * nq - 1, c == sender))
            def _():
                for cp in sends + forwards:
                    cp.wait_send()

            @pl.when(jnp.logical_and(step == N_GROUPS * nq - 1, c != sender))
            def _():
                for cp in forwards:
                    cp.wait_recv()

    once = pl.Buffered(1)
    gathered, sems = _gather_shapes(shards) if ng else ([], [])
    return _call(
        body, name="sb_fwd_gather" if ng else "sb_fwd", grid=(N_GROUPS, nq),
        in_specs=[pl.BlockSpec((TQ, LW), lambda p, i: (i, Q_BLK0 + p)),
                  pl.BlockSpec((S, LW), lambda p, i: (0, K_BLK0 + p), pipeline_mode=once),
                  pl.BlockSpec((S, LW), lambda p, i: (0, V_BLK0 + p), pipeline_mode=once),
                  pl.BlockSpec((TK, TK), lambda p, i: (0, 0))] + [ANY] * ng,
        out_specs=[pl.BlockSpec((TQ, LW), lambda p, i: (i, p)),
                   pl.BlockSpec((None, None, HG * TQ, LANES), lambda p, i: (p, i, 0, 0))] + [ANY] * ng,
        out_shape=[jax.ShapeDtypeStruct((S, W_B), F32),
                   jax.ShapeDtypeStruct((N_GROUPS, nq, HG * TQ, LANES), F32)] + gathered,
        scratch_shapes=[pltpu.VMEM((HG * TQ, LW), F32)] + sems,
        compiler_params=_params("arbitrary", "arbitrary"),
    )(proj_b, proj_b, proj_b, tri, *shards)


def _sb_bwd(proj_b, dyb, rb, tri, trit, carry_exchange=None):
    S = proj_b.shape[0]
    nq = S // TQ
    kpq = TQ // TK
    hs = list(carry_exchange or [])
    ne = len(hs)

    def body(*refs):
        q_ref, k_ref, v_ref, do_ref, rb_ref, tri_ref, trit_ref = refs[:7]
        dq_ref, dk_acc, dv_acc = refs[7 + ne:10 + ne]
        dq_acc = refs[10 + 2 * ne]
        i = pl.program_id(1)
        if ne:
            tick = pl.program_id(0) * nq + i
            sends, arrivals = _exchange_plan(refs[7:7 + ne], refs[10 + ne:10 + 2 * ne], refs[11 + 2 * ne],
                                             refs[12 + 2 * ne])

            @pl.when(tick == 0)
            def _():
                for cp in sends:
                    cp.start()

        lane2 = lax.broadcasted_iota(jnp.int32, (HG * TQ, LANES), 1)
        scale = HEAD_DIM ** -0.5
        q2 = _stack_heads(q_ref[...].astype(F32) * scale)
        do2 = _stack_heads(do_ref[...])
        tri = tri_ref[...]
        trit = trit_ref[...]

        @pl.when(i == 0)
        def _():
            dk_acc[...] = jnp.zeros_like(dk_acc)
            dv_acc[...] = jnp.zeros_like(dv_acc)

        dq_acc[...] = jnp.zeros_like(dq_acc)

        def block(j, pre, key_offset=None):
            start = pl.multiple_of(j * TK, TK)
            kj = k_ref[pl.ds(start, TK), :]
            vj = v_ref[pl.ds(start, TK), :]
            lsp, lsm, tail, msk = _sb_scores(q2, kj, tri, key_offset)
            run = jnp.sum(jnp.where(lane2 == j, rb_ref[...], 0.0), axis=-1, keepdims=True)
            att = jnp.exp(lsp + tail + run)
            if msk is not None:
                att = jnp.where(msk, att, 0.0)
            beta = jnp.exp(lsp)
            dl = _dot_nt(do2, vj) * att
            cin = _dot(dl.astype(MXU_DT), trit)
            dz = dl * (1.0 - beta) - beta * (pre + cin)
            if msk is not None:
                dz = jnp.where(msk, dz, 0.0)
            dzb = dz.astype(MXU_DT)
            dq_acc[...] += _dot(dzb, kj)
            dk_acc[pl.ds(start, TK), :] += _dot_tn(dzb, q2)
            dv_acc[pl.ds(start, TK), :] += _dot_tn(att.astype(MXU_DT), do2)
            return pre + cin[:, TK - 1:] + dl[:, TK - 1:]

        past = i * kpq
        walked = jnp.max(jnp.where(lane2[:8] == LANES - 1, rb_ref[pl.ds(0, 8), :], 0.0)).astype(jnp.int32)
        walked = jnp.clip(walked, 0, past)
        pre = lax.fori_loop(past - walked, past, lambda j, pre: block(j, pre), jnp.zeros((HG * TQ, 1), F32))
        for d in range(kpq):
            pre = block(i * kpq + d, pre, key_offset=d * TK)
        dq_ref[...] = (_unstack_heads(dq_acc[...]) * scale).astype(dq_ref.dtype)

        if ne:
            @pl.when(tick == N_GROUPS * nq - 1)
            def _():
                for cp in arrivals:
                    cp.wait_recv()
                for cp in sends:
                    cp.wait_send()

    once = pl.Buffered(1)
    exchanged, sems = _exchange_shapes(hs) if ne else ([], [])
    return _call(
        body, name="sb_bwd_exchange" if ne else "sb_bwd", grid=(N_GROUPS, nq),
        in_specs=[pl.BlockSpec((TQ, LW), lambda p, i: (i, Q_BLK0 + p)),
                  pl.BlockSpec((S, LW), lambda p, i: (0, K_BLK0 + p), pipeline_mode=once),
                  pl.BlockSpec((S, LW), lambda p, i: (0, V_BLK0 + p), pipeline_mode=once),
                  pl.BlockSpec((TQ, LW), lambda p, i: (i, p)),
                  pl.BlockSpec((None, None, HG * TQ, LANES), lambda p, i: (p, i, 0, 0)),
                  pl.BlockSpec((TK, TK), lambda p, i: (0, 0)),
                  pl.BlockSpec((TK, TK), lambda p, i: (0, 0))] + [ANY] * ne,
        out_specs=[pl.BlockSpec((TQ, LW), lambda p, i: (i, p)),
                   pl.BlockSpec((S, LW), lambda p, i: (0, p), pipeline_mode=once),
                   pl.BlockSpec((S, LW), lambda p, i: (0, p), pipeline_mode=once)] + [ANY] * ne,
        out_shape=[jax.ShapeDtypeStruct((S, W_B), MXU_DT), jax.ShapeDtypeStruct((S, W_B), F32),
                   jax.ShapeDtypeStruct((S, W_B), F32)] + exchanged,
        scratch_shapes=[pltpu.VMEM((HG * TQ, LW), F32)] + sems,
        compiler_params=_params("arbitrary", "arbitrary"),
    )(proj_b, proj_b, proj_b, dyb, rb, tri, trit, *hs)


P_BLK = (2 * W_A + 3 * W_B) // W_C


def _window_lanes():
    g = lax.broadcasted_iota(jnp.int32, (1, W_C), 1) // (W_C // 4)
    w = jnp.where(g == 0, POOL_WINDOWS[0], jnp.where(g == 1, POOL_WINDOWS[1],
                  jnp.where(g == 2, POOL_WINDOWS[2], POOL_WINDOWS[3])))
    return g, w


def _shift_rows(ext, k, tm, lead):
    n = ext.shape[0]
    return pltpu.roll(ext, shift=k % n, axis=0)[lead:lead + tm]


def _pool_diff(p_cur, p_halo, row0, tm):
    ext = jnp.concatenate([p_halo, p_cur], axis=0)
    g, w = _window_lanes()
    acc = ext
    sums = []
    for sh in (1, 2, 4, 8):
        acc = acc + pltpu.roll(acc, shift=sh, axis=0)
        sums.append(acc[HALO:HALO + tm])
    wsum = jnp.where(g == 0, sums[0], jnp.where(g == 1, sums[1], jnp.where(g == 2, sums[2], sums[3])))
    pos = (row0 + 1 + lax.broadcasted_iota(jnp.int32, (tm, W_C), 0)).astype(F32)
    cnt = jnp.minimum(pos, w.astype(F32))
    return wsum / cnt - p_cur, cnt


def _pool_specs(tm, nrow_blocks_halo):
    cur = pl.BlockSpec((tm, W_C), lambda i: (i, P_BLK))
    prev = pl.BlockSpec((HALO, W_C), lambda i: (jnp.maximum(i * (tm // HALO) - 1, 0), P_BLK))
    return cur, prev


def _in_proj_groups(x, g, w, sng, wm, bias, gmat, wbd, scale):
    S, D = x.shape
    tm = TM_MM
    p0 = 2 * W_A + 3 * W_B
    qkv_chunk = 3 * W_B // 2

    def body(x_ref, g_ref, w_ref, sng_ref, wm_ref, b_ref, gmat_ref, wbd_ref, sc_ref,
             h_ref, o_ref, ob_ref, ya_ref, yc_ref, tail_ref):
        i = pl.program_id(0)
        xv = x_ref[...]
        r = lax.rsqrt(jnp.mean(xv * xv, axis=-1, keepdims=True) + EPS)
        h = (xv * r * g_ref[...]).astype(h_ref.dtype)
        h_ref[...] = h

        def project(c0, c1):
            acc = _dot(h, w_ref[:, c0:c1])
            o_ref[:, c0:c1] = acc
            ob_ref[:, c0:c1] = acc.astype(ob_ref.dtype)
            return acc

        a = project(0, 2 * W_A)
        _, u, _, _, _, s = _sgu_common(a, sng_ref[...], wm_ref, b_ref[...], gmat_ref[...])
        ya_ref[...] = u * s
        for c0 in range(2 * W_A, p0, qkv_chunk):
            project(c0, c0 + qkv_chunk)
        p = project(p0, p0 + W_C)
        halo = jnp.where(i > 0, tail_ref[...], 0.0)
        tail_ref[...] = p[tm - HALO:]
        d, _ = _pool_diff(p, halo, i * tm, tm)
        yc_ref[...] = _dot(d.astype(MXU_DT), wbd_ref[...]) * sc_ref[...]

    return _call(
        body, name="in_proj_groups", grid=(S // tm,),
        in_specs=[_row_spec(tm, D), _full_spec((1, D)),
                  pl.BlockSpec((D, IN_COLS), lambda i: (0, 0), pipeline_mode=pl.Buffered(1)),
                  _full_spec((1, W_A)), _full_spec((4, CHUNK, CHUNK)), _full_spec((CHUNK, W_A)),
                  _full_spec((LANES, LANES)), _full_spec((W_C, W_C)), _full_spec((1, W_C))],
        out_specs=[_row_spec(tm, D), _row_spec(tm, IN_COLS), _row_spec(tm, IN_COLS), _row_spec(tm, W_A),
                   _row_spec(tm, W_C)],
        out_shape=[jax.ShapeDtypeStruct((S, D), MXU_DT), jax.ShapeDtypeStruct((S, IN_COLS), F32),
                   jax.ShapeDtypeStruct((S, IN_COLS), MXU_DT), jax.ShapeDtypeStruct((S, W_A), F32),
                   jax.ShapeDtypeStruct((S, W_C), F32)],
        scratch_shapes=[pltpu.VMEM((HALO, W_C), F32)],
        compiler_params=_params("arbitrary"),
    )(x, g, w, sng, wm, bias, gmat, wbd, scale)


def _pool_bwd_a(proj, dy, wbd, scale):
    S = proj.shape[0]
    tm = TM

    def body(p_ref, ph_ref, dy_ref, w_ref, sc_ref, dd_ref, e_ref, dw_ref, dsc_ref):
        i = pl.program_id(0)
        halo = jnp.where(i > 0, ph_ref[...], 0.0)
        d, cnt = _pool_diff(p_ref[...], halo, i * tm, tm)
        db = d.astype(MXU_DT)
        dy = dy_ref[...]

        @pl.when(i == 0)
        def _():
            dw_ref[...] = jnp.zeros_like(dw_ref)
            dsc_ref[...] = jnp.zeros_like(dsc_ref)

        dsc_ref[...] += jnp.sum(dy * _dot(db, w_ref[...]), axis=0, keepdims=True)
        dys = (dy * sc_ref[...]).astype(MXU_DT)
        dw_ref[...] += _dot_tn(db, dys)
        dd = _dot_nt(dys, w_ref[...])
        dd_ref[...] = dd
        e_ref[...] = dd / cnt

    cur, prev = _pool_specs(tm, S // HALO)
    return _call(
        body, name="pool_bwd_a", grid=(S // tm,),
        in_specs=[cur, prev, _row_spec(tm, W_C), _full_spec((W_C, W_C)), _full_spec((1, W_C))],
        out_specs=[_row_spec(tm, W_C), _row_spec(tm, W_C), _full_spec((W_C, W_C)), _full_spec((1, W_C))],
        out_shape=[jax.ShapeDtypeStruct((S, W_C), F32), jax.ShapeDtypeStruct((S, W_C), F32),
                   jax.ShapeDtypeStruct((W_C, W_C), F32), jax.ShapeDtypeStruct((1, W_C), F32)],
        compiler_params=_params("arbitrary"),
    )(proj, proj, dy, wbd, scale)


def _pool_bwd_b(dd, e):
    S = dd.shape[0]
    tm = TM
    nb = S // tm

    def body(dd_ref, e_ref, en_ref, dp_ref):
        i = pl.program_id(0)
        halo = jnp.where(i < nb - 1, en_ref[...], 0.0)
        ext = jnp.concatenate([e_ref[...], halo], axis=0)
        n = ext.shape[0]
        g, _ = _window_lanes()
        acc = ext
        sums = []
        for sh in (1, 2, 4, 8):
            acc = acc + pltpu.roll(acc, shift=n - sh, axis=0)
            sums.append(acc[:tm])
        wsum = jnp.where(g == 0, sums[0], jnp.where(g == 1, sums[1], jnp.where(g == 2, sums[2], sums[3])))
        dp_ref[...] = (wsum - dd_ref[...]).astype(dp_ref.dtype)

    nxt = pl.BlockSpec((HALO, W_C), lambda i: (jnp.minimum((i + 1) * (tm // HALO), S // HALO - 1), 0))
    return _call(
        body, name="pool_bwd_b", grid=(nb,),
        in_specs=[_row_spec(tm, W_C), _row_spec(tm, W_C), nxt],
        out_specs=_row_spec(tm, W_C),
        out_shape=jax.ShapeDtypeStruct((S, W_C), MXU_DT),
        compiler_params=_params("parallel"),
    )(dd, e, e)


TN_FF = 1408
NB_FF = D_FF // TN_FF
CONV_ROWS = 8


def _conv(z_cur, z_halo, cwb, tm):
    ext = jnp.concatenate([z_halo, z_cur], axis=0)
    z2 = _shift_rows(ext, 2, tm, HALO)
    z1 = _shift_rows(ext, 1, tm, HALO)
    zc = cwb[3:4] + z2 * cwb[0:1] + z1 * cwb[1:2] + z_cur * cwb[2:3]
    return zc, z2, z1


def _up_proj_gate(x, g, w, cwb):
    S, D = x.shape
    tm = TM

    def body(x_ref, g_ref, w_ref, c_ref, h_ref, z_ref, zc_ref, f_ref, tail_ref):
        first = pl.program_id(0) == 0
        xv = x_ref[...]
        r = lax.rsqrt(jnp.mean(xv * xv, axis=-1, keepdims=True) + EPS)
        h = (xv * r * g_ref[...]).astype(h_ref.dtype)
        h_ref[...] = h
        for j in range(NB_FF):
            halves = []
            for col0 in (j * TN_FF, D_FF + j * TN_FF):
                zb = _dot(h, w_ref[:, col0:col0 + TN_FF]).astype(z_ref.dtype)
                z_ref[:, col0:col0 + TN_FF] = zb
                zf = zb.astype(F32)
                prev = jnp.where(first, 0.0, tail_ref[:, col0:col0 + TN_FF])
                tail_ref[:, col0:col0 + TN_FF] = zf[tm - HALO:]
                zc = _conv(zf, prev, c_ref[:, col0:col0 + TN_FF], tm)[0]
                zc_ref[:, col0:col0 + TN_FF] = zc.astype(zc_ref.dtype)
                halves.append(zc)
            gate, value = halves
            f_ref[:, j * TN_FF:(j + 1) * TN_FF] = (gate * jax.nn.sigmoid(gate) * value).astype(f_ref.dtype)

    return _call(
        body, name="up_proj_gate", grid=(S // tm,),
        in_specs=[_row_spec(tm, D), _full_spec((1, D)),
                  pl.BlockSpec((D, 2 * D_FF), lambda i: (0, 0), pipeline_mode=pl.Buffered(1)),
                  _full_spec((CONV_ROWS, 2 * D_FF))],
        out_specs=[_row_spec(tm, D), _row_spec(tm, 2 * D_FF), _row_spec(tm, 2 * D_FF), _row_spec(tm, D_FF)],
        out_shape=[jax.ShapeDtypeStruct((S, D), MXU_DT), jax.ShapeDtypeStruct((S, 2 * D_FF), MXU_DT),
                   jax.ShapeDtypeStruct((S, 2 * D_FF), MXU_DT), jax.ShapeDtypeStruct((S, D_FF), MXU_DT)],
        scratch_shapes=[pltpu.VMEM((HALO, 2 * D_FF), F32)],
        compiler_params=_params("arbitrary"),
    )(x, g, w, cwb)


def _gate_up_bwd(z, zc, cwb, w, wd, x, g, dres):
    S, D = x.shape
    tm = TM
    nb = S // tm

    def body(z_ref, zc_ref, zcn_ref, c_ref, w_ref, wd_ref, x_ref, g_ref, r_ref, rn_ref,
             dz_ref, dc_ref, dx_ref, dg_ref):
        i = pl.program_id(0)
        first = i == 0
        last = i == nb - 1
        dxe = jnp.concatenate([r_ref[...], jnp.where(last, 0.0, rn_ref[...])], axis=0).astype(MXU_DT)

        @pl.when(first)
        def _():
            dc_ref[...] = jnp.zeros_like(dc_ref)
            dg_ref[...] = jnp.zeros_like(dg_ref)

        rid = lax.broadcasted_iota(jnp.int32, (CONV_ROWS, TN_FF), 0)

        def conv_out(cols):
            return jnp.concatenate([zc_ref[:, cols].astype(F32), zcn_ref[:, cols].astype(F32)], axis=0)

        def conv_bwd(d, z0, c):
            d0 = d[:tm]
            d1 = _shift_rows(d, -1, tm, 0)
            d2 = _shift_rows(d, -2, tm, 0)
            sums = [jnp.sum(d2 * z0, axis=0, keepdims=True), jnp.sum(d1 * z0, axis=0, keepdims=True),
                    jnp.sum(d0 * z0, axis=0, keepdims=True), jnp.sum(d0, axis=0, keepdims=True)]
            dtaps = jnp.zeros((CONV_ROWS, TN_FF), F32)
            for k, v in enumerate(sums):
                dtaps = jnp.where(rid == k, v, dtaps)
            return d0 * c[2:3] + d1 * c[1:2] + d2 * c[0:1], dtaps

        dh = jnp.zeros((tm, D), F32)
        for j in range(NB_FF):
            gc = slice(j * TN_FF, (j + 1) * TN_FF)
            uc = slice(D_FF + j * TN_FF, D_FF + (j + 1) * TN_FF)
            gt = conv_out(gc)
            ut = conv_out(uc)
            df = _dot_nt(dxe, wd_ref[gc, :])
            sg = jax.nn.sigmoid(gt)
            dzg, dtg = conv_bwd(df * ut * (sg * (1.0 + gt * (1.0 - sg))), z_ref[:, gc].astype(F32), c_ref[:, gc])
            dzu, dtu = conv_bwd(df * (gt * sg), z_ref[:, uc].astype(F32), c_ref[:, uc])
            dzg = dzg.astype(dz_ref.dtype)
            dzu = dzu.astype(dz_ref.dtype)
            dz_ref[:, gc] = dzg
            dz_ref[:, uc] = dzu
            dc_ref[:, gc] += dtg
            dc_ref[:, uc] += dtu
            dh += _dot_nt(dzg, w_ref[:, gc]) + _dot_nt(dzu, w_ref[:, uc])

        xv = x_ref[...]
        r = lax.rsqrt(jnp.mean(xv * xv, axis=-1, keepdims=True) + EPS)
        xhat = xv * r
        dg_ref[...] += jnp.sum(dh * xhat, axis=0, keepdims=True)
        dxh = dh * g_ref[...]
        dx_ref[...] = r_ref[...] + r * (dxh - xhat * jnp.mean(dxh * xhat, axis=-1, keepdims=True))

    hb = tm // HALO
    last_halo = S // HALO - 1
    return _call(
        body, name="gate_up_bwd", grid=(nb,),
        in_specs=[_row_spec(tm, 2 * D_FF), _row_spec(tm, 2 * D_FF),
                  pl.BlockSpec((HALO, 2 * D_FF), lambda i: (jnp.minimum((i + 1) * hb, last_halo), 0)),
                  _full_spec((CONV_ROWS, 2 * D_FF)),
                  pl.BlockSpec((D, 2 * D_FF), lambda i: (0, 0), pipeline_mode=pl.Buffered(1)),
                  pl.BlockSpec((D_FF, D), lambda i: (0, 0), pipeline_mode=pl.Buffered(1)),
                  _row_spec(tm, D), _full_spec((1, D)), _row_spec(tm, D),
                  pl.BlockSpec((HALO, D), lambda i: (jnp.minimum((i + 1) * hb, last_halo), 0))],
        out_specs=[_row_spec(tm, 2 * D_FF), _full_spec((CONV_ROWS, 2 * D_FF)), _row_spec(tm, D), _full_spec((1, D))],
        out_shape=[jax.ShapeDtypeStruct((S, 2 * D_FF), MXU_DT), jax.ShapeDtypeStruct((CONV_ROWS, 2 * D_FF), F32),
                   jax.ShapeDtypeStruct((S, D), F32), jax.ShapeDtypeStruct((1, D), F32)],
        compiler_params=_params("arbitrary"),
    )(z, zc, zc, cwb, w, wd, x, g, dres, dres)


def _layer_consts(w, l):
    wm = _tril_weights(w["sgu_w"][l])
    eye = jnp.eye(4, dtype=F32)
    wbd = (w["pool_w"][l][:, :, None, :] * eye[:, None, :, None]).reshape(W_C, W_C)
    cwb = jnp.concatenate([w["conv_w"][l], w["conv_b"][l][None], jnp.zeros((CONV_ROWS - 4, 2 * D_FF), F32)], axis=0)
    return dict(
        g1=w["norm1_g"][l][None], g2=w["norm2_g"][l][None], gm=w["mix_norm_g"][l][None],
        sng=w["sgu_norm_g"][l][None], wm=wm.astype(MXU_DT), wmt=jnp.swapaxes(wm, 1, 2).astype(MXU_DT),
        bias=jnp.repeat(jnp.transpose(w["sgu_b"][l]), HEAD_DIM, axis=1),
        wbd=wbd.astype(MXU_DT), scale=w["pool_scale"][l][None], cwb=cwb,
        w_in=w["w_in"][l], w_o=w["w_o"][l], w_up=w["w_up"][l], w_down=w["w_down"][l],
    )


def _local_step(x, tgt, w, late=None, early_exchange=None):
    gmat = _group_matrix()
    tri = _tri_matrix()
    trit = jnp.transpose(tri)
    saved = []
    early = None
    for l in range(DEPTH):
        c = _layer_consts(w, l)
        h1, proj, proj_b, ya, yc = _in_proj_groups(x, c["g1"], c["w_in"], c["sng"], c["wm"], c["bias"], gmat,
                                                   c["wbd"], c["scale"])
        if l == 0 and late is not None:
            assert DEPTH == 2
            yb, rb, *gathered = _sb_fwd(proj_b, tri, carry_gather=late[0])
            w = dict(w)
            for name, arr in late[1](gathered).items():
                w[name] = [w[name][0], arr]
        else:
            yb, rb = _sb_fwd(proj_b, tri)
        x2, yn = _mix_out(ya, yb, yc, c["gm"], c["w_o"], x, gmat)
        h2, z, zc, f = _up_proj_gate(x2, c["g2"], c["w_up"], c["cwb"])
        saved.append(dict(c=c, x=x, proj=proj, proj_b=proj_b, h1=h1, ya=ya, yb=yb, yc=yc, rb=rb, x2=x2, yn=yn,
                          z=z, zc=zc, h2=h2, f=f))
        if l < DEPTH - 1:
            x = _mm_res(f, c["w_down"], x2, "down_proj")

    last = saved[-1]
    dx, d_final_g, loss8 = _down_proj_loss(last["f"], last["c"]["w_down"], last["x2"], w["final_g"][None], tgt)
    grads = {n: [None] * DEPTH for n in ("norm1_g", "w_in", "sgu_norm_g", "sgu_w", "sgu_b", "pool_w", "pool_scale",
                                         "mix_norm_g", "w_o", "norm2_g", "w_up", "conv_w", "conv_b", "w_down")}
    for l in reversed(range(DEPTH)):
        s = saved[l]
        c = s["c"]
        grads["w_down"][l] = _mm_tn(s["f"], dx, "down_proj_wgrad").reshape(N_CHIPS, D_FF // N_CHIPS, D_MODEL)
        dz, dcwb, dx2, dg2 = _gate_up_bwd(s["z"], s["zc"], c["cwb"], c["w_up"], c["w_down"], s["x2"], c["g2"], dx)
        grads["conv_w"][l] = dcwb[:3]
        grads["conv_b"][l] = dcwb[3]
        grads["w_up"][l] = _mm_tn(s["h2"], dz, "up_proj_wgrad", col_tiles=True)
        grads["norm2_g"][l] = dg2[0]
        grads["w_o"][l] = _mm_tn(s["yn"], dx2, "out_proj_wgrad").reshape(N_CHIPS, D_MODEL // N_CHIPS, D_MODEL)
        dya, dyb, dyc, dgm = _mix_out_bwd(dx2, c["w_o"], s["ya"], s["yb"], s["yc"], c["gm"], gmat)
        grads["mix_norm_g"][l] = dgm[0]
        dd, e, dwbd, dscale = _pool_bwd_a(s["proj"], dyc, c["wbd"], c["scale"])
        dp = _pool_bwd_b(dd, e)
        grads["pool_w"][l] = jnp.stack([dwbd[g * 64:(g + 1) * 64, g * 64:(g + 1) * 64] for g in range(4)])
        grads["pool_scale"][l] = dscale[0]
        if l == 0 and early_exchange is not None:
            sent = early_exchange(grads)
            dq, dk, dv, *parts = _sb_bwd(s["proj_b"], dyb, s["rb"], tri, trit, carry_exchange=sent)
            early = (sent, parts)
        else:
            dq, dk, dv = _sb_bwd(s["proj_b"], dyb, s["rb"], tri, trit)
        da, dwm, dbias, dsng = _sgu_bwd(s["proj"], dya, c["sng"], c["wm"], c["wmt"], c["bias"], gmat)
        grads["sgu_w"][l] = dwm
        grads["sgu_b"][l] = jnp.transpose(jnp.sum(dbias.reshape(CHUNK, 4, HEAD_DIM), axis=-1))
        grads["sgu_norm_g"][l] = dsng[0]
        dproj = jnp.concatenate([da, dq, dk.astype(MXU_DT), dv.astype(MXU_DT), dp], axis=1)
        dw_in = _mm_tn(s["h1"], dproj, "in_proj_wgrad")
        grads["w_in"][l] = jnp.transpose(dw_in.reshape(D_MODEL, N_CHIPS, IN_COLS // N_CHIPS), (1, 0, 2))
        dx, dg1 = _mm_nt_rmsbwd([(dproj, c["w_in"])], s["x"], c["g1"], dx2, "in_proj_bwd")
        grads["norm1_g"][l] = dg1[0]

    out = {n: (v if n in BIG_NAMES[:4] else jnp.stack(v)) for n, v in grads.items()}
    out["final_g"] = d_final_g[0]
    return loss8[0, 0], dx, out, early


MESH = pl.DeviceIdType.MESH
ANY = pl.BlockSpec(memory_space=pl.ANY)


def _gather_plan(ins, outs, send_sems, recv_sems, layer, sender):
    n = len(ins)
    x, y, c = lax.axis_index("x"), lax.axis_index("y"), lax.axis_index("c")
    sibling = (x, y, 1 - c)
    my_chip = 2 * x + y
    chips = [(1 - x, y), (x, 1 - y), (1 - x, 1 - y)]
    ids = [2 * px + py for px, py in chips]

    def copy(a, k, chip, to, own=False):
        dst = outs[a].at[chip]
        return pltpu.make_async_remote_copy(
            src_ref=ins[a].at[layer] if own else dst, dst_ref=dst,
            send_sem=send_sems.at[a, k], recv_sem=recv_sems.at[a, k], device_id=to, device_id_type=MESH)

    sends = [copy(a, j, my_chip, (*chips[j], sender), own=True) for j in range(3) for a in range(n)]
    arrivals = [copy(a, j, ids[j], sibling) for j in range(3) for a in range(n)]
    forwards = [copy(a, 3 + j, ids[j], sibling) for j in range(3) for a in range(n)]
    return c, sends, arrivals, forwards


def _gather_shapes(shards):
    n = len(shards)
    return ([jax.ShapeDtypeStruct((N_CHIPS,) + s.shape[1:], s.dtype) for s in shards],
            [pltpu.SemaphoreType.DMA((n, 6)), pltpu.SemaphoreType.DMA((n, 6))])


def _all_gather(shards, layer, sender):
    n = len(shards)

    def body(*refs):
        c, sends, arrivals, forwards = _gather_plan(refs[:n], refs[n:2 * n], refs[2 * n], refs[2 * n + 1],
                                                    layer, sender)

        @pl.when(c == sender)
        def _():
            for cp in sends:
                cp.start()
            for arrived, onward in zip(arrivals, forwards):
                arrived.wait_recv()
                onward.start()
            for cp in sends + forwards:
                cp.wait_send()

        @pl.when(c != sender)
        def _():
            for cp in forwards:
                cp.wait_recv()

    out_shape, sems = _gather_shapes(shards)
    return _call(body, name="weight_all_gather", out_shape=out_shape, in_specs=[ANY] * n, out_specs=[ANY] * n,
                 scratch_shapes=sems)(*shards)


def _row_tile(r):
    return r if r <= 704 else 256


def _grad_swap(items, name):
    n = len(items)

    def body(*refs):
        firsts, seconds, outs = refs[:n], refs[n:2 * n], refs[2 * n:3 * n]
        send_sems, recv_sems = refs[3 * n:]
        x, y, c = lax.axis_index("x"), lax.axis_index("y"), lax.axis_index("c")

        def copies(srcs):
            return [pltpu.make_async_remote_copy(src_ref=srcs[a], dst_ref=outs[a], send_sem=send_sems.at[a],
                                                 recv_sem=recv_sems.at[a], device_id=(x, y, 1 - c),
                                                 device_id_type=MESH) for a in range(n)]

        @pl.when(c == 0)
        def _():
            for cp in copies(seconds):
                cp.start()

        @pl.when(c == 1)
        def _():
            for cp in copies(firsts):
                cp.start()

        for cp in copies(firsts):
            cp.wait()

    return _call(
        body, name=name, out_shape=[jax.ShapeDtypeStruct(a0.shape, a0.dtype) for a0, _ in items],
        in_specs=[ANY] * (2 * n), out_specs=[ANY] * n,
        scratch_shapes=[pltpu.SemaphoreType.DMA((n,)), pltpu.SemaphoreType.DMA((n,))],
    )(*[a0 for a0, _ in items], *[a1 for _, a1 in items])


def _pair_add(a0, a1, r, c_arr, name, out_dtype):
    k, rr, cc = r.shape
    tr = _row_tile(rr)

    def body(c_ref, a0_ref, a1_ref, r_ref, o_ref):
        mine = jnp.where(c_ref[0] == 0, a0_ref[...], a1_ref[...])
        o_ref[...] = (mine + r_ref[...]).astype(o_ref.dtype)

    def member(which):
        def index(kk, i, c_ref):
            used = (c_ref[0] == which).astype(jnp.int32)
            return (kk * used, i * used, 0)
        return pl.BlockSpec((1, tr, cc), index)

    spec = pl.BlockSpec((1, tr, cc), lambda kk, i, c_ref: (kk, i, 0))
    grid_spec = pltpu.PrefetchScalarGridSpec(num_scalar_prefetch=1, grid=(k, rr // tr),
                                             in_specs=[member(0), member(1), spec], out_specs=spec)
    return _call(body, name=name, grid_spec=grid_spec, out_shape=jax.ShapeDtypeStruct((k, rr, cc), out_dtype),
                 compiler_params=_params("parallel", "parallel"))(c_arr, a0, a1, r)


def _exchange_plan(ins, outs, send_sems, recv_sems):
    n = len(ins)
    x, y, c = lax.axis_index("x"), lax.axis_index("y"), lax.axis_index("c")
    my_chip = 2 * x + y
    chips = [(1 - x, y), (x, 1 - y), (1 - x, 1 - y)]

    def copy(a, k, src_chip, dst_chip):
        px, py = chips[k]
        return pltpu.make_async_remote_copy(
            src_ref=ins[a].at[src_chip], dst_ref=outs[a].at[dst_chip], send_sem=send_sems.at[a, k],
            recv_sem=recv_sems.at[a, k], device_id=(px, py, c), device_id_type=MESH)

    sends = [copy(a, k, 2 * chips[k][0] + chips[k][1], my_chip) for k in range(3) for a in range(n)]
    arrivals = [copy(a, k, my_chip, 2 * chips[k][0] + chips[k][1]) for k in range(3) for a in range(n)]
    return sends, arrivals


def _exchange_shapes(hs):
    n = len(hs)
    return ([jax.ShapeDtypeStruct(h.shape, h.dtype) for h in hs],
            [pltpu.SemaphoreType.DMA((n, 3)), pltpu.SemaphoreType.DMA((n, 3))])


def _grad_exchange(hs):
    n = len(hs)

    def body(*refs):
        sends, arrivals = _exchange_plan(refs[:n], refs[n:2 * n], refs[2 * n], refs[2 * n + 1])
        for cp in sends:
            cp.start()
        for cp in arrivals:
            cp.wait_recv()
        for cp in sends:
            cp.wait_send()

    out_shape, sems = _exchange_shapes(hs)
    return _call(body, name="grad_exchange_chips", out_shape=out_shape, in_specs=[ANY] * n, out_specs=[ANY] * n,
                 scratch_shapes=sems)(*hs)


def _sum_chips(a, c_arr, name):
    _, r, cc = a.shape
    tr = _row_tile(r)

    def body(c_ref, a_ref, o_ref):
        o_ref[...] = ((a_ref[0].astype(F32) + a_ref[1].astype(F32)) + a_ref[2].astype(F32)) + a_ref[3].astype(F32)

    grid_spec = pltpu.PrefetchScalarGridSpec(
        num_scalar_prefetch=1, grid=(r // tr,),
        in_specs=[pl.BlockSpec((N_CHIPS, tr, cc), lambda i, c_ref: (0, i, 0))],
        out_specs=pl.BlockSpec((None, tr, cc), lambda i, c_ref: (c_ref[0], i, 0)))
    return _call(body, name=name, grid_spec=grid_spec, out_shape=jax.ShapeDtypeStruct((2, r, cc), F32),
                 compiler_params=_params("parallel"))(c_arr, a)


def _grad_share(bufs):
    n = len(bufs)

    def body(*refs):
        outs = refs[n:2 * n]
        send_sems, recv_sems = refs[2 * n:]
        x, y, c = lax.axis_index("x"), lax.axis_index("y"), lax.axis_index("c")
        copies = [pltpu.make_async_remote_copy(src_ref=outs[a].at[c], dst_ref=outs[a].at[c], send_sem=send_sems.at[a],
                                               recv_sem=recv_sems.at[a], device_id=(x, y, 1 - c),
                                               device_id_type=MESH) for a in range(n)]
        for cp in copies:
            cp.start()
        for a in range(n):
            pltpu.make_async_remote_copy(src_ref=outs[a].at[c], dst_ref=outs[a].at[1 - c], send_sem=send_sems.at[a],
                                         recv_sem=recv_sems.at[a], device_id=(x, y, 1 - c),
                                         device_id_type=MESH).wait_recv()
        for cp in copies:
            cp.wait_send()

    return _call(
        body, name="grad_share_cores", out_shape=[jax.ShapeDtypeStruct(b.shape, b.dtype) for b in bufs],
        in_specs=[ANY] * n, out_specs=[ANY] * n, input_output_aliases={a: a for a in range(n)},
        scratch_shapes=[pltpu.SemaphoreType.DMA((n,)), pltpu.SemaphoreType.DMA((n,))],
    )(*bufs)


def _adamw_math(g_ref, w_ref, m_ref, v_ref, d_ref, nm_ref, nv_ref):
    gv = g_ref[...]
    nm = ADAM_B1 * m_ref[...] + (1.0 - ADAM_B1) * gv
    nv = ADAM_B2 * v_ref[...] + (1.0 - ADAM_B2) * (gv * gv)
    m_hat = nm / (1.0 - ADAM_B1 ** ADAM_STEP)
    v_hat = nv / (1.0 - ADAM_B2 ** ADAM_STEP)
    d_ref[...] = -ADAM_LR * (m_hat / (jnp.sqrt(v_hat) + ADAM_EPS) + ADAM_WD * w_ref[...])
    nm_ref[...] = nm
    nv_ref[...] = nv


def _adamw_big(g, w, m, v, name):
    d, r, c = g.shape
    tr = r if r <= 704 else 256
    spec = pl.BlockSpec((1, tr, c), lambda l, i: (l, i, 0))

    def body(*refs):
        _adamw_math(*refs)

    shp = jax.ShapeDtypeStruct(g.shape, F32)
    return _call(body, name=name, grid=(d, r // tr), in_specs=[spec] * 4, out_specs=[spec] * 3,
                 out_shape=[shp, shp, shp], compiler_params=_params("parallel", "parallel"))(g, w, m, v)


def _adamw_small(gs, ws, ms, vs):
    n = len(gs)

    def body(*refs):
        ins, outs = refs[:4 * n], refs[4 * n:]
        for k in range(n):
            _adamw_math(ins[k], ins[n + k], ins[2 * n + k], ins[3 * n + k], outs[k], outs[n + k], outs[2 * n + k])

    shp = [jax.ShapeDtypeStruct(g.shape, F32) for g in gs]
    res = _call(body, name="adamw_small", out_shape=shp * 3)(*gs, *ws, *ms, *vs)
    return res[:n], res[n:2 * n], res[2 * n:]


def _rows(a, rows):
    flat = a.reshape(-1)
    return jnp.pad(flat, (0, rows * D_MODEL - flat.shape[0])).reshape(rows, D_MODEL)


def _small_rows(p, extra=None):
    parts = [p[n].reshape(-1) for n in SMALL_NAMES]
    if extra is not None:
        parts.append(extra.reshape(-1))
    flat = jnp.concatenate(parts)
    return jnp.pad(flat, (0, ROWS_SMALL * D_MODEL - flat.shape[0])).reshape(ROWS_SMALL, D_MODEL)


CONV_SHARD = (DEPTH, 3, 2 * D_FF // N_CHIPS)
N_CONV_SHARD = DEPTH * 3 * (2 * D_FF // N_CHIPS)


def _small_pack(g, loss):
    conv = jnp.transpose(g["conv_w"].reshape(DEPTH, 3, N_CHIPS, 2 * D_FF // N_CHIPS), (2, 0, 1, 3))
    conv = jnp.stack([_rows(conv[k], ROWS_CONV) for k in range(N_CHIPS)])
    small = jnp.broadcast_to(_small_rows(g, loss), (N_CHIPS, ROWS_SMALL, D_MODEL))
    return jnp.concatenate([conv, small], axis=1)


def _unpack_small(pack):
    out = {"conv_w": pack[:ROWS_CONV].reshape(-1)[:N_CONV_SHARD].reshape(CONV_SHARD)}
    flat = pack[ROWS_CONV:].reshape(-1)
    k = 0
    for name in SMALL_NAMES:
        shape = SMALL_SHAPES[name]
        n = 1
        for d in shape:
            n *= d
        out[name] = flat[k:k + n].reshape(shape)
        k += n
    out["extra"] = flat[k]
    return out


def _assemble_layer(gathered, shards, layer):
    my_chip = 2 * lax.axis_index("x") + lax.axis_index("y")
    w_in, w_o, w_up, w_down = [lax.dynamic_update_index_in_dim(got, own[layer], my_chip, 0)
                               for got, own in zip(gathered, shards)]

    def by_cols(a):
        k, r, wd = a.shape
        return jnp.transpose(a, (1, 0, 2)).reshape(r, k * wd)

    return dict(w_in=by_cols(w_in), w_o=w_o.reshape(-1, D_MODEL), w_up=by_cols(w_up),
                w_down=w_down.reshape(-1, D_MODEL))


def _gather_weights(p):
    shards = [p[n].astype(jnp.bfloat16) for n in BIG_NAMES[:4]]
    conv_all = p["conv_w"].reshape(1, -1, p["conv_w"].shape[-1])
    got = _all_gather(shards + [conv_all], 0, 0)
    my_chip = 2 * lax.axis_index("x") + lax.axis_index("y")
    conv = lax.dynamic_update_index_in_dim(got[4], conv_all[0], my_chip, 0)
    conv = jnp.transpose(conv.reshape((N_CHIPS,) + CONV_SHARD), (1, 2, 0, 3)).reshape(DEPTH, 3, 2 * D_FF)
    full = {n: [a, None] for n, a in _assemble_layer(got[:4], shards, 0).items()}
    full["conv_w"] = conv
    return full, (shards, lambda gathered: _assemble_layer(gathered, shards, 1))


def _halves(a):
    r = a.shape[1] // 2
    return a[:, :r], a[:, r:]


def _reduce_begin(items, names, dtypes, c_arr, tag):
    got = _grad_swap(items, "grad_swap_cores_" + tag)
    return [_pair_add(a0, a1, r, c_arr, "grad_add_cores_" + nm, dt)
            for (a0, a1), r, nm, dt in zip(items, got, names, dtypes)]


def _reduce_end(parts, sent, names, c_arr):
    my_chip = 2 * lax.axis_index("x") + lax.axis_index("y")
    full = [lax.dynamic_update_index_in_dim(p, lax.dynamic_index_in_dim(own, my_chip, 0, keepdims=False), my_chip, 0)
            for p, own in zip(parts, sent)]
    return [_sum_chips(f, c_arr, "grad_sum_chips_" + nm) for f, nm in zip(full, names)]


EARLY_NAMES = ("w_o", "w_up", "w_down", "w_in_1")


def _early_items(grads):
    return [tuple(grads[n]) for n in ("w_o", "w_up", "w_down")] + [_halves(grads["w_in"][1])]


def kernel(x, norm1_g, w_in, sgu_norm_g, sgu_w, sgu_b, pool_w, pool_scale, mix_norm_g, w_o, norm2_g, w_up, conv_w, conv_b, w_down, final_g, loss_target, m_norm1_g, m_w_in, m_sgu_norm_g, m_sgu_w, m_sgu_b, m_pool_w, m_pool_scale, m_mix_norm_g, m_w_o, m_norm2_g, m_w_up, m_conv_w, m_conv_b, m_w_down, m_final_g, v_norm1_g, v_w_in, v_sgu_norm_g, v_sgu_w, v_sgu_b, v_pool_w, v_pool_scale, v_mix_norm_g, v_w_o, v_norm2_g, v_w_up, v_conv_w, v_conv_b, v_w_down, v_final_g):
    names = ("norm1_g", "w_in", "sgu_norm_g", "sgu_w", "sgu_b", "pool_w", "pool_scale", "mix_norm_g", "w_o",
             "norm2_g", "w_up", "conv_w", "conv_b", "w_down", "final_g")
    p = dict(zip(names, (norm1_g, w_in, sgu_norm_g, sgu_w, sgu_b, pool_w, pool_scale, mix_norm_g, w_o, norm2_g,
                         w_up, conv_w, conv_b, w_down, final_g)))
    pm = dict(zip(names, (m_norm1_g, m_w_in, m_sgu_norm_g, m_sgu_w, m_sgu_b, m_pool_w, m_pool_scale, m_mix_norm_g,
                          m_w_o, m_norm2_g, m_w_up, m_conv_w, m_conv_b, m_w_down, m_final_g)))
    pv = dict(zip(names, (v_norm1_g, v_w_in, v_sgu_norm_g, v_sgu_w, v_sgu_b, v_pool_w, v_pool_scale, v_mix_norm_g,
                          v_w_o, v_norm2_g, v_w_up, v_conv_w, v_conv_b, v_w_down, v_final_g)))
    c = lax.axis_index("c")
    gathered, late = _gather_weights(p)
    full = dict(p)
    full.update(gathered)

    c_arr = jnp.reshape(c, (1,)).astype(jnp.int32)
    early_types = [ICI_DT] * len(EARLY_NAMES)
    loss, dx, grads, (sent, received) = _local_step(
        x[0], loss_target[0], full, late,
        lambda known: _reduce_begin(_early_items(known), EARLY_NAMES, early_types, c_arr, "early"))
    early_sums = _reduce_end(received, sent, EARLY_NAMES, c_arr)
    small_pack = _small_pack(grads, loss)
    late_names = ("w_in_0", "small")
    late_sent = _reduce_begin([_halves(grads["w_in"][0]), _halves(small_pack)], late_names, [ICI_DT, F32], c_arr, "late")
    late_sums = _reduce_end(_grad_exchange(late_sent), late_sent, late_names, c_arr)
    r_o, r_up, r_down, r_in1, r_in0, r_small = _grad_share(early_sums + late_sums)
    g = dict(w_o=r_o, w_up=r_up, w_down=r_down,
             w_in=jnp.stack([r_in0.reshape(D_MODEL, -1), r_in1.reshape(D_MODEL, -1)]))
    g.update(_unpack_small(r_small.reshape(2 * SP_HALF, D_MODEL)))
    d, nm, nv = {}, {}, {}
    for n in BIG_NAMES:
        d[n], nm[n], nv[n] = _adamw_big(g[n], p[n], pm[n], pv[n], "adamw_" + n)

    def two_d(a):
        return a.reshape(1, -1) if a.ndim == 1 else a

    ds, ms, vs = _adamw_small([two_d(g[n]) for n in SMALL_NAMES], [two_d(p[n]) for n in SMALL_NAMES],
                              [two_d(pm[n]) for n in SMALL_NAMES], [two_d(pv[n]) for n in SMALL_NAMES])
    for k, n in enumerate(SMALL_NAMES):
        d[n], nm[n], nv[n] = (a.reshape(p[n].shape) for a in (ds[k], ms[k], vs[k]))
    return (g["extra"], dx[None], *[g[n] for n in names], *[d[n] for n in names], *[nm[n] for n in names],
            *[nv[n] for n in names])
```

```python
import functools

import jax
import jax.numpy as jnp
from jax import lax
from jax.experimental import pallas as pl
from jax.experimental.pallas import tpu as pltpu

F32 = jnp.float32
MXU_DT = jnp.bfloat16

D_MODEL = 1024
DEPTH = 2
HEAD_DIM = 64
W_A = 256
W_B = 512
W_C = 256
IN_COLS = 2 * W_A + 3 * W_B + W_C
CHUNK = 128
POOL_WINDOWS = (2, 4, 8, 16)
D_FF = 2816
EPS = 1e-6
N_CHIPS = 4

ADAM_LR = 0.001
ADAM_B1 = 0.9
ADAM_B2 = 0.999
ADAM_EPS = 1e-08
ADAM_WD = 0.01
ADAM_STEP = 10

LANES = 128
TQ = 256
TK = 256
TM = 256
TM_MM = 512
HALO = 16
VMEM_LIMIT = 56 * 1024 * 1024

ROWS_CONV = 16
ROWS_SMALL = 240
SP_HALF = (ROWS_CONV + ROWS_SMALL) // 2
ICI_DT = jnp.bfloat16

BIG_NAMES = ("w_in", "w_o", "w_up", "w_down", "conv_w")
SMALL_NAMES = ("norm1_g", "sgu_norm_g", "sgu_w", "sgu_b", "pool_w", "pool_scale",
               "mix_norm_g", "norm2_g", "conv_b", "final_g")
SMALL_SHAPES = {
    "norm1_g": (DEPTH, D_MODEL), "sgu_norm_g": (DEPTH, W_A), "sgu_w": (DEPTH, 4, CHUNK, CHUNK),
    "sgu_b": (DEPTH, 4, CHUNK), "pool_w": (DEPTH, 4, 64, 64), "pool_scale": (DEPTH, W_C),
    "mix_norm_g": (DEPTH, D_MODEL), "norm2_g": (DEPTH, D_MODEL), "conv_b": (DEPTH, 2 * D_FF),
    "final_g": (D_MODEL,),
}


def _call(body, **kw):
    return pl.pallas_call(body, **kw)


def _params(*sem):
    return pltpu.CompilerParams(dimension_semantics=sem, vmem_limit_bytes=VMEM_LIMIT)


def _dot(a, b):
    return jnp.dot(a, b, preferred_element_type=F32)


def _dot_nt(a, b):
    return lax.dot_general(a, b, (((1,), (1,)), ((), ())), preferred_element_type=F32)


def _dot_tn(a, b):
    return lax.dot_general(a, b, (((0,), (0,)), ((), ())), preferred_element_type=F32)


def _group_mean(sq, gmat):
    sqb = sq.astype(MXU_DT)
    cols = [_dot(sqb[:, b * LANES:(b + 1) * LANES], gmat) for b in range(sq.shape[1] // LANES)]
    return cols[0] if len(cols) == 1 else jnp.concatenate(cols, axis=-1)


def _group_matrix():
    r = jnp.arange(LANES)
    return jnp.where((r[:, None] // HEAD_DIM) == (r[None, :] // HEAD_DIM), 1.0 / HEAD_DIM, 0.0).astype(MXU_DT)


def _tile(n):
    return max(t for t in range(LANES, 1536 + 1, LANES) if n % t == 0)


def _row_spec(tm, cols, col_block=0):
    return pl.BlockSpec((tm, cols), lambda i, cb=col_block: (i, cb))


def _full_spec(shape):
    nd = len(shape)
    return pl.BlockSpec(shape, lambda *_: (0,) * nd)


def _mm_res(a, w, res, name):
    S, K = a.shape
    N = w.shape[1]
    tm = TM_MM

    def body(a_ref, w_ref, r_ref, o_ref):
        o_ref[...] = r_ref[...] + _dot(a_ref[...], w_ref[...])

    return _call(
        body, name=name, grid=(S // tm,),
        in_specs=[_row_spec(tm, K), _full_spec((K, N)), _row_spec(tm, N)],
        out_specs=_row_spec(tm, N),
        out_shape=jax.ShapeDtypeStruct((S, N), F32),
        compiler_params=_params("parallel"),
    )(a, w, res)


def _mm_tn(a, b, name, col_tiles=False):
    S, K1 = a.shape
    N = b.shape[1]
    ts = 2 * TM_MM
    tk = _tile(K1)
    tn = _tile(N)
    if col_tiles:
        out_spec = pl.BlockSpec((None, tk, tn), lambda m, n, s: (n, m, 0))
        out_shape = jax.ShapeDtypeStruct((N // tn, K1, tn), F32)
    else:
        out_spec = pl.BlockSpec((tk, tn), lambda m, n, s: (m, n))
        out_shape = jax.ShapeDtypeStruct((K1, N), F32)

    def body(a_ref, b_ref, o_ref):
        @pl.when(pl.program_id(2) == 0)
        def _():
            o_ref[...] = jnp.zeros_like(o_ref)

        o_ref[...] += _dot_tn(a_ref[...], b_ref[...].astype(MXU_DT))

    return _call(
        body, name=name, grid=(K1 // tk, N // tn, S // ts),
        in_specs=[pl.BlockSpec((ts, tk), lambda m, n, s: (s, m)),
                  pl.BlockSpec((ts, tn), lambda m, n, s: (s, n))],
        out_specs=out_spec, out_shape=out_shape,
        compiler_params=_params("parallel", "parallel", "arbitrary"),
    )(a, b)


def _mm_nt_rmsbwd(pairs, x, g, dres, name):
    S, D = x.shape
    tm = TM
    n = len(pairs)

    def body(*refs):
        a_refs = refs[:n]
        w_refs = refs[n:2 * n]
        x_ref, g_ref, r_ref, dx_ref, dg_ref = refs[2 * n:]
        dh = _dot_nt(a_refs[0][...], w_refs[0][...])
        for k in range(1, n):
            dh += _dot_nt(a_refs[k][...], w_refs[k][...])
        xv = x_ref[...]
        r = lax.rsqrt(jnp.mean(xv * xv, axis=-1, keepdims=True) + EPS)
        xhat = xv * r

        @pl.when(pl.program_id(0) == 0)
        def _():
            dg_ref[...] = jnp.zeros_like(dg_ref)

        dg_ref[...] += jnp.sum(dh * xhat, axis=0, keepdims=True)
        dxh = dh * g_ref[...]
        dx_ref[...] = r_ref[...] + r * (dxh - xhat * jnp.mean(dxh * xhat, axis=-1, keepdims=True))

    in_specs = ([_row_spec(tm, a.shape[1]) for a, _ in pairs] + [_full_spec(w.shape) for _, w in pairs]
                + [_row_spec(tm, D), _full_spec((1, D)), _row_spec(tm, D)])
    return _call(
        body, name=name, grid=(S // tm,), in_specs=in_specs,
        out_specs=[_row_spec(tm, D), _full_spec((1, D))],
        out_shape=[jax.ShapeDtypeStruct((S, D), F32), jax.ShapeDtypeStruct((1, D), F32)],
        compiler_params=_params("arbitrary"),
    )(*[a for a, _ in pairs], *[w for _, w in pairs], x, g, dres)


def _down_proj_loss(a, w, res, g, tgt):
    S, D = res.shape
    K = a.shape[1]
    tm = TM

    def body(a_ref, w_ref, res_ref, g_ref, t_ref, dx_ref, dg_ref, l_ref):
        xv = res_ref[...] + _dot(a_ref[...], w_ref[...])
        r = lax.rsqrt(jnp.mean(xv * xv, axis=-1, keepdims=True) + EPS)
        xhat = xv * r
        diff = xhat * g_ref[...] - t_ref[...]

        @pl.when(pl.program_id(0) == 0)
        def _():
            dg_ref[...] = jnp.zeros_like(dg_ref)
            l_ref[...] = jnp.zeros_like(l_ref)

        l_ref[...] += jnp.full(l_ref.shape, 0.5 * jnp.sum(jnp.mean(diff * diff, axis=-1, keepdims=True)), F32)
        dout = diff * (1.0 / D)
        dg_ref[...] += jnp.sum(dout * xhat, axis=0, keepdims=True)
        dxh = dout * g_ref[...]
        dx_ref[...] = r * (dxh - xhat * jnp.mean(dxh * xhat, axis=-1, keepdims=True))

    return _call(
        body, name="down_proj_loss", grid=(S // tm,),
        in_specs=[_row_spec(tm, K), _full_spec((K, D)), _row_spec(tm, D), _full_spec((1, D)), _row_spec(tm, D)],
        out_specs=[_row_spec(tm, D), _full_spec((1, D)), _full_spec((8, LANES))],
        out_shape=[jax.ShapeDtypeStruct((S, D), F32), jax.ShapeDtypeStruct((1, D), F32),
                   jax.ShapeDtypeStruct((8, LANES), F32)],
        compiler_params=_params("arbitrary"),
    )(a, w, res, g, tgt)


def _mix_out(ya, yb, yc, gm, wo, x, gmat):
    S = x.shape[0]
    tm = TM

    def body(ya_ref, yb_ref, yc_ref, gm_ref, wo_ref, x_ref, gmat_ref, x2_ref, yn_ref):
        y = jnp.concatenate([ya_ref[...], yb_ref[...], yc_ref[...]], axis=-1)
        r = lax.rsqrt(_group_mean(y * y, gmat_ref[...]) + EPS)
        yn = (y * r * gm_ref[...]).astype(MXU_DT)
        yn_ref[...] = yn
        x2_ref[...] = x_ref[...] + _dot(yn, wo_ref[...])

    return _call(
        body, name="mix_out", grid=(S // tm,),
        in_specs=[_row_spec(tm, W_A), _row_spec(tm, W_B), _row_spec(tm, W_C), _full_spec((1, D_MODEL)),
                  _full_spec((D_MODEL, D_MODEL)), _row_spec(tm, D_MODEL), _full_spec((LANES, LANES))],
        out_specs=[_row_spec(tm, D_MODEL), _row_spec(tm, D_MODEL)],
        out_shape=[jax.ShapeDtypeStruct((S, D_MODEL), F32), jax.ShapeDtypeStruct((S, D_MODEL), MXU_DT)],
        compiler_params=_params("parallel"),
    )(ya, yb, yc, gm, wo, x, gmat)


def _mix_out_bwd(dx2, wo, ya, yb, yc, gm, gmat, carry_swap=None):
    S = dx2.shape[0]
    tm = TM
    nb = S // tm
    items = list(carry_swap or [])
    ns = len(items)

    def body(*refs):
        dx2_ref, wo_ref, ya_ref, yb_ref, yc_ref, gm_ref, gmat_ref = refs[:7]
        dya_ref, dyb_ref, dyc_ref, dgm_ref = refs[7 + 2 * ns:11 + 2 * ns]
        if ns:
            start, finish = _swap_plan(refs[7:7 + ns], refs[7 + ns:7 + 2 * ns], refs[11 + 2 * ns:11 + 3 * ns],
                                       refs[11 + 3 * ns], refs[12 + 3 * ns])

            @pl.when(pl.program_id(0) == 0)
            def _():
                start()

        dyn = _dot_nt(dx2_ref[...].astype(MXU_DT), wo_ref[...])
        y = jnp.concatenate([ya_ref[...], yb_ref[...], yc_ref[...]], axis=-1)
        r = lax.rsqrt(_group_mean(y * y, gmat_ref[...]) + EPS)
        yhat = y * r

        @pl.when(pl.program_id(0) == 0)
        def _():
            dgm_ref[...] = jnp.zeros_like(dgm_ref)

        dgm_ref[...] += jnp.sum(dyn * yhat, axis=0, keepdims=True)
        dyh = dyn * gm_ref[...]
        dy = r * (dyh - yhat * _group_mean(dyh * yhat, gmat_ref[...]))
        dya_ref[...] = dy[:, :W_A]
        dyb_ref[...] = dy[:, W_A:W_A + W_B]
        dyc_ref[...] = dy[:, W_A + W_B:]

        if ns:
            @pl.when(pl.program_id(0) == nb - 1)
            def _():
                finish()

    swapped, sems = _swap_shapes(items) if ns else ([], [])
    return _call(
        body, name="mix_out_bwd_swap" if ns else "mix_out_bwd", grid=(nb,),
        in_specs=[_row_spec(tm, D_MODEL), _full_spec((D_MODEL, D_MODEL)), _row_spec(tm, W_A), _row_spec(tm, W_B),
                  _row_spec(tm, W_C), _full_spec((1, D_MODEL)), _full_spec((LANES, LANES))] + [ANY] * (2 * ns),
        out_specs=[_row_spec(tm, W_A), _row_spec(tm, W_B), _row_spec(tm, W_C), _full_spec((1, D_MODEL))]
        + [ANY] * ns,
        out_shape=[jax.ShapeDtypeStruct((S, W_A), F32), jax.ShapeDtypeStruct((S, W_B), F32),
                   jax.ShapeDtypeStruct((S, W_C), F32), jax.ShapeDtypeStruct((1, D_MODEL), F32)] + swapped,
        scratch_shapes=sems,
        compiler_params=_params("arbitrary"),
    )(dx2, wo, ya, yb, yc, gm, gmat, *[a0 for a0, _ in items], *[a1 for _, a1 in items])


_SQRT_HALF = 0.7071067811865476
_INV_SQRT_2PI = 0.3989422804014327


def _sgu_common(a, sng, wm_ref, bias, gmat):
    phi = 0.5 * (1.0 + lax.erf(a * _SQRT_HALF))
    ga = a * phi
    u = ga[:, :W_A]
    v = ga[:, W_A:]
    r = lax.rsqrt(_group_mean(v * v, gmat) + EPS)
    vhat = v * r
    vn = (vhat * sng).astype(MXU_DT)
    head = lax.broadcasted_iota(jnp.int32, (CHUNK, W_A), 1) // HEAD_DIM
    rows = []
    for c in range(a.shape[0] // CHUNK):
        vc = vn[c * CHUNK:(c + 1) * CHUNK]
        s = bias
        for h in range(4):
            s = s + jnp.where(head == h, _dot(wm_ref[h], vc), 0.0)
        rows.append(s)
    s = jnp.concatenate(rows, axis=0)
    return phi, u, r, vhat, vn, s


def _tril_weights(sgu_w_l):
    t = jnp.arange(CHUNK)
    return jnp.where((t[None, :] <= t[:, None])[None], sgu_w_l, 0.0)


def _sgu_bwd(proj, dy, sng, wm, wmt, bias, gmat):
    S = proj.shape[0]
    tm = TM

    def body(a_ref, dy_ref, sng_ref, wm_ref, wmt_ref, b_ref, gmat_ref, da_ref, dw_ref, db_ref, dsng_ref):
        a = a_ref[...]
        dy = dy_ref[...]
        gmat = gmat_ref[...]
        sng = sng_ref[...]
        phi, u, r, vhat, vn, s = _sgu_common(a, sng, wm_ref, b_ref[...], gmat)
        du = dy * s
        ds = dy * u

        @pl.when(pl.program_id(0) == 0)
        def _():
            dw_ref[...] = jnp.zeros_like(dw_ref)
            db_ref[...] = jnp.zeros_like(db_ref)
            dsng_ref[...] = jnp.zeros_like(dsng_ref)

        head = lax.broadcasted_iota(jnp.int32, (CHUNK, W_A), 1) // HEAD_DIM
        tt = lax.broadcasted_iota(jnp.int32, (CHUNK, CHUNK), 0)
        ss = lax.broadcasted_iota(jnp.int32, (CHUNK, CHUNK), 1)
        rows = []
        for c in range(tm // CHUNK):
            dsc = ds[c * CHUNK:(c + 1) * CHUNK]
            vc = vn[c * CHUNK:(c + 1) * CHUNK]
            db_ref[...] += dsc
            dsb = dsc.astype(MXU_DT)
            dvn = jnp.zeros((CHUNK, W_A), F32)
            for h in range(4):
                dvn = dvn + jnp.where(head == h, _dot(wmt_ref[h], dsb), 0.0)
                dsh = jnp.where(head == h, dsc, 0.0).astype(MXU_DT)
                dw_ref[h] += jnp.where(ss <= tt, _dot_nt(dsh, vc), 0.0)
            rows.append(dvn)
        dvn = jnp.concatenate(rows, axis=0)
        dsng_ref[...] += jnp.sum(dvn * vhat, axis=0, keepdims=True)
        dvh = dvn * sng
        dv = r * (dvh - vhat * _group_mean(dvh * vhat, gmat))
        dga = jnp.concatenate([du, dv], axis=-1)
        dgelu = phi + a * (_INV_SQRT_2PI * jnp.exp(-0.5 * a * a))
        da_ref[...] = (dga * dgelu).astype(da_ref.dtype)

    return _call(
        body, name="sgu_bwd", grid=(S // tm,),
        in_specs=[_row_spec(tm, 2 * W_A), _row_spec(tm, W_A), _full_spec((1, W_A)), _full_spec((4, CHUNK, CHUNK)),
                  _full_spec((4, CHUNK, CHUNK)), _full_spec((CHUNK, W_A)), _full_spec((LANES, LANES))],
        out_specs=[_row_spec(tm, 2 * W_A), _full_spec((4, CHUNK, CHUNK)), _full_spec((CHUNK, W_A)),
                   _full_spec((1, W_A))],
        out_shape=[jax.ShapeDtypeStruct((S, 2 * W_A), MXU_DT), jax.ShapeDtypeStruct((4, CHUNK, CHUNK), F32),
                   jax.ShapeDtypeStruct((CHUNK, W_A), F32), jax.ShapeDtypeStruct((1, W_A), F32)],
        compiler_params=_params("arbitrary"),
    )(proj, dy, sng, wm, wmt, bias, gmat)


HG = 4
LW = HG * HEAD_DIM
Q_BLK0 = (2 * W_A) // LW
K_BLK0 = Q_BLK0 + W_B // LW
V_BLK0 = K_BLK0 + W_B // LW
N_GROUPS = W_B // LW
EXP_IS_ZERO_BELOW = -120.0


def _tri_matrix():
    r = jnp.arange(TK)
    return (r[:, None] > r[None, :]).astype(MXU_DT)


def _stack_heads(a):
    head = lax.broadcasted_iota(jnp.int32, a.shape, 1) // HEAD_DIM
    return jnp.concatenate([jnp.where(head == h, a, 0.0) for h in range(HG)], axis=0).astype(MXU_DT)


def _unstack_heads(a):
    head = lax.broadcasted_iota(jnp.int32, (TQ, LW), 1) // HEAD_DIM
    out = a[:TQ]
    for h in range(1, HG):
        out = jnp.where(head == h, a[h * TQ:(h + 1) * TQ], out)
    return out


def _sb_scores(q2, kj, tri, key_offset):
    z = _dot_nt(q2, kj)
    sp = jnp.log(1.0 + jnp.exp(-jnp.abs(z)))
    lsp = jnp.minimum(z, 0.0) - sp
    lsm = lsp - z
    msk = None
    if key_offset is not None:
        row = lax.broadcasted_iota(jnp.int32, z.shape, 0) & (TQ - 1)
        col = lax.broadcasted_iota(jnp.int32, z.shape, 1) + key_offset
        msk = col < row
        lsm = jnp.where(msk, lsm, 0.0)
    tail = _dot(lsm.astype(MXU_DT), tri)
    return lsp, lsm, tail, msk


def _sb_fwd(proj_b, tri, carry_gather=None):
    S = proj_b.shape[0]
    nq = S // TQ
    kpq = TQ // TK
    assert S // TK < LANES
    shards = list(carry_gather or [])
    ng = len(shards)

    def body(*refs):
        q_ref, k_ref, v_ref, tri_ref = refs[:4]
        o_ref, rb_ref = refs[4 + ng:6 + ng]
        acc_ref = refs[6 + 2 * ng]
        i = pl.program_id(1)
        if ng:
            step = pl.program_id(0) * nq + i
            sender = 1
            c, sends, arrivals, forwards = _gather_plan(refs[4:4 + ng], refs[6 + ng:6 + 2 * ng], refs[7 + 2 * ng],
                                                        refs[8 + 2 * ng], 1, sender)

            @pl.when(jnp.logical_and(step == 0, c == sender))
            def _():
                for cp in sends:
                    cp.start()

            @pl.when(jnp.logical_and(step == N_GROUPS * nq - max(nq // 8, 1), c == sender))
            def _():
                for arrived, onward in zip(arrivals, forwards):
                    arrived.wait_recv()
                    onward.start()


        lane2 = lax.broadcasted_iota(jnp.int32, (HG * TQ, LANES), 1)
        q2 = _stack_heads(q_ref[...].astype(F32) * (HEAD_DIM ** -0.5))
        tri = tri_ref[...]
        rb_ref[...] = jnp.zeros_like(rb_ref)

        def block(j, run, key_offset=None, first=False):
            start = pl.multiple_of(j * TK, TK)
            kj = k_ref[pl.ds(start, TK), :]
            vj = v_ref[pl.ds(start, TK), :]
            lsp, lsm, tail, msk = _sb_scores(q2, kj, tri, key_offset)
            rb_ref[...] = jnp.where(lane2 == j, run, rb_ref[...])
            att = jnp.exp(lsp + tail + run)
            if msk is not None:
                att = jnp.where(msk, att, 0.0)
            pv = _dot(att.astype(MXU_DT), vj)
            if first:
                acc_ref[...] = pv
            else:
                acc_ref[...] += pv
            return run + tail[:, :1] + lsm[:, :1]

        run = jnp.zeros((HG * TQ, 1), F32)
        for d in reversed(range(kpq)):
            run = block(i * kpq + d, run, key_offset=d * TK, first=(d == kpq - 1))
        past = i * kpq

        def alive(run):
            return (jnp.max(run) > EXP_IS_ZERO_BELOW).astype(jnp.int32)

        def walk(carry):
            n, run, _ = carry
            run = block(past - 1 - n, run)
            return n + 1, run, alive(run)

        n, _, _ = lax.while_loop(lambda s: jnp.logical_and(s[0] < past, s[2] > 0), walk,
                                 (jnp.int32(0), run, alive(run)))
        rb_ref[...] = jnp.where(lane2 == LANES - 1, n.astype(F32), rb_ref[...])
        o_ref[...] = _unstack_heads(acc_ref[...])

        if ng:
            @pl.when(jnp.logical_and(step == N_GROUPS * nq - 1, c == sender))
            def _():
                for cp in sends + forwards:
                    cp.wait_send()

            @pl.when(jnp.logical_and(step == N_GROUPS * nq - 1, c != sender))
            def _():
                for cp in forwards:
                    cp.wait_recv()

    once = pl.Buffered(1)
    gathered, sems = _gather_shapes(shards) if ng else ([], [])
    return _call(
        body, name="sb_fwd_gather" if ng else "sb_fwd", grid=(N_GROUPS, nq),
        in_specs=[pl.BlockSpec((TQ, LW), lambda p, i: (i, Q_BLK0 + p)),
                  pl.BlockSpec((S, LW), lambda p, i: (0, K_BLK0 + p), pipeline_mode=once),
                  pl.BlockSpec((S, LW), lambda p, i: (0, V_BLK0 + p), pipeline_mode=once),
                  pl.BlockSpec((TK, TK), lambda p, i: (0, 0))] + [ANY] * ng,
        out_specs=[pl.BlockSpec((TQ, LW), lambda p, i: (i, p)),
                   pl.BlockSpec((None, None, HG * TQ, LANES), lambda p, i: (p, i, 0, 0))] + [ANY] * ng,
        out_shape=[jax.ShapeDtypeStruct((S, W_B), F32),
                   jax.ShapeDtypeStruct((N_GROUPS, nq, HG * TQ, LANES), F32)] + gathered,
        scratch_shapes=[pltpu.VMEM((HG * TQ, LW), F32)] + sems,
        compiler_params=_params("arbitrary", "arbitrary"),
    )(proj_b, proj_b, proj_b, tri, *shards)


def _sb_bwd(proj_b, dyb, rb, tri, trit, carry_exchange=None):
    S = proj_b.shape[0]
    nq = S // TQ
    kpq = TQ // TK
    hs = list(carry_exchange or [])
    ne = len(hs)

    def body(*refs):
        q_ref, k_ref, v_ref, do_ref, rb_ref, tri_ref, trit_ref = refs[:7]
        dq_ref, dk_acc, dv_acc = refs[7 + ne:10 + ne]
        dq_acc = refs[10 + 2 * ne]
        i = pl.program_id(1)
        if ne:
            tick = pl.program_id(0) * nq + i
            sends, arrivals = _exchange_plan(refs[7:7 + ne], refs[10 + ne:10 + 2 * ne], refs[11 + 2 * ne],
                                             refs[12 + 2 * ne])

            @pl.when(tick == 0)
            def _():
                for cp in sends:
                    cp.start()

        lane2 = lax.broadcasted_iota(jnp.int32, (HG * TQ, LANES), 1)
        scale = HEAD_DIM ** -0.5
        q2 = _stack_heads(q_ref[...].astype(F32) * scale)
        do2 = _stack_heads(do_ref[...])
        tri = tri_ref[...]
        trit = trit_ref[...]

        @pl.when(i == 0)
        def _():
            dk_acc[...] = jnp.zeros_like(dk_acc)
            dv_acc[...] = jnp.zeros_like(dv_acc)

        dq_acc[...] = jnp.zeros_like(dq_acc)

        def block(j, pre, key_offset=None):
            start = pl.multiple_of(j * TK, TK)
            kj = k_ref[pl.ds(start, TK), :]
            vj = v_ref[pl.ds(start, TK), :]
            lsp, lsm, tail, msk = _sb_scores(q2, kj, tri, key_offset)
            run = jnp.sum(jnp.where(lane2 == j, rb_ref[...], 0.0), axis=-1, keepdims=True)
            att = jnp.exp(lsp + tail + run)
            if msk is not None:
                att = jnp.where(msk, att, 0.0)
            beta = jnp.exp(lsp)
            dl = _dot_nt(do2, vj) * att
            cin = _dot(dl.astype(MXU_DT), trit)
            dz = dl * (1.0 - beta) - beta * (pre + cin)
            if msk is not None:
                dz = jnp.where(msk, dz, 0.0)
            dzb = dz.astype(MXU_DT)
            dq_acc[...] += _dot(dzb, kj)
            dk_acc[pl.ds(start, TK), :] += _dot_tn(dzb, q2)
            dv_acc[pl.ds(start, TK), :] += _dot_tn(att.astype(MXU_DT), do2)
            return pre + cin[:, TK - 1:] + dl[:, TK - 1:]

        past = i * kpq
        walked = jnp.max(jnp.where(lane2[:8] == LANES - 1, rb_ref[pl.ds(0, 8), :], 0.0)).astype(jnp.int32)
        walked = jnp.clip(walked, 0, past)
        pre = lax.fori_loop(past - walked, past, lambda j, pre: block(j, pre), jnp.zeros((HG * TQ, 1), F32))
        for d in range(kpq):
            pre = block(i * kpq + d, pre, key_offset=d * TK)
        dq_ref[...] = (_unstack_heads(dq_acc[...]) * scale).astype(dq_ref.dtype)

        if ne:
            @pl.when(tick == N_GROUPS * nq - 1)
            def _():
                for cp in arrivals:
                    cp.wait_recv()
                for cp in sends:
                    cp.wait_send()

    once = pl.Buffered(1)
    exchanged, sems = _exchange_shapes(hs) if ne else ([], [])
    return _call(
        body, name="sb_bwd_exchange" if ne else "sb_bwd", grid=(N_GROUPS, nq),
        in_specs=[pl.BlockSpec((TQ, LW), lambda p, i: (i, Q_BLK0 + p)),
                  pl.BlockSpec((S, LW), lambda p, i: (0, K_BLK0 + p), pipeline_mode=once),
                  pl.BlockSpec((S, LW), lambda p, i: (0, V_BLK0 + p), pipeline_mode=once),
                  pl.BlockSpec((TQ, LW), lambda p, i: (i, p)),
                  pl.BlockSpec((None, None, HG * TQ, LANES), lambda p, i: (p, i, 0, 0)),
                  pl.BlockSpec((TK, TK), lambda p, i: (0, 0)),
                  pl.BlockSpec((TK, TK), lambda p, i: (0, 0))] + [ANY] * ne,
        out_specs=[pl.BlockSpec((TQ, LW), lambda p, i: (i, p)),
                   pl.BlockSpec((S, LW), lambda p, i: (0, p), pipeline_mode=once),
                   pl.BlockSpec((S, LW), lambda p, i: (0, p), pipeline_mode=once)] + [ANY] * ne,
        out_shape=[jax.ShapeDtypeStruct((S, W_B), MXU_DT), jax.ShapeDtypeStruct((S, W_B), F32),
                   jax.ShapeDtypeStruct((S, W_B), F32)] + exchanged,
        scratch_shapes=[pltpu.VMEM((HG * TQ, LW), F32)] + sems,
        compiler_params=_params("arbitrary", "arbitrary"),
    )(proj_b, proj_b, proj_b, dyb, rb, tri, trit, *hs)


P_BLK = (2 * W_A + 3 * W_B) // W_C


def _window_lanes():
    g = lax.broadcasted_iota(jnp.int32, (1, W_C), 1) // (W_C // 4)
    w = jnp.where(g == 0, POOL_WINDOWS[0], jnp.where(g == 1, POOL_WINDOWS[1],
                  jnp.where(g == 2, POOL_WINDOWS[2], POOL_WINDOWS[3])))
    return g, w


def _shift_rows(ext, k, tm, lead):
    n = ext.shape[0]
    return pltpu.roll(ext, shift=k % n, axis=0)[lead:lead + tm]


def _pool_diff(p_cur, p_halo, row0, tm):
    ext = jnp.concatenate([p_halo, p_cur], axis=0)
    g, w = _window_lanes()
    acc = ext
    sums = []
    for sh in (1, 2, 4, 8):
        acc = acc + pltpu.roll(acc, shift=sh, axis=0)
        sums.append(acc[HALO:HALO + tm])
    wsum = jnp.where(g == 0, sums[0], jnp.where(g == 1, sums[1], jnp.where(g == 2, sums[2], sums[3])))
    pos = (row0 + 1 + lax.broadcasted_iota(jnp.int32, (tm, W_C), 0)).astype(F32)
    cnt = jnp.minimum(pos, w.astype(F32))
    return wsum / cnt - p_cur, cnt


def _pool_specs(tm, nrow_blocks_halo):
    cur = pl.BlockSpec((tm, W_C), lambda i: (i, P_BLK))
    prev = pl.BlockSpec((HALO, W_C), lambda i: (jnp.maximum(i * (tm // HALO) - 1, 0), P_BLK))
    return cur, prev


def _in_proj_groups(x, g, w, sng, wm, bias, gmat, wbd, scale):
    S, D = x.shape
    tm = TM_MM
    p0 = 2 * W_A + 3 * W_B
    qkv_chunk = 3 * W_B // 2

    def body(x_ref, g_ref, w_ref, sng_ref, wm_ref, b_ref, gmat_ref, wbd_ref, sc_ref,
             h_ref, o_ref, ob_ref, ya_ref, yc_ref, tail_ref):
        i = pl.program_id(0)
        xv = x_ref[...]
        r = lax.rsqrt(jnp.mean(xv * xv, axis=-1, keepdims=True) + EPS)
        h = (xv * r * g_ref[...]).astype(h_ref.dtype)
        h_ref[...] = h

        def project(c0, c1):
            acc = _dot(h, w_ref[:, c0:c1])
            o_ref[:, c0:c1] = acc
            ob_ref[:, c0:c1] = acc.astype(ob_ref.dtype)
            return acc

        a = project(0, 2 * W_A)
        _, u, _, _, _, s = _sgu_common(a, sng_ref[...], wm_ref, b_ref[...], gmat_ref[...])
        ya_ref[...] = u * s
        for c0 in range(2 * W_A, p0, qkv_chunk):
            project(c0, c0 + qkv_chunk)
        p = project(p0, p0 + W_C)
        halo = jnp.where(i > 0, tail_ref[...], 0.0)
        tail_ref[...] = p[tm - HALO:]
        d, _ = _pool_diff(p, halo, i * tm, tm)
        yc_ref[...] = _dot(d.astype(MXU_DT), wbd_ref[...]) * sc_ref[...]

    return _call(
        body, name="in_proj_groups", grid=(S // tm,),
        in_specs=[_row_spec(tm, D), _full_spec((1, D)),
                  pl.BlockSpec((D, IN_COLS), lambda i: (0, 0), pipeline_mode=pl.Buffered(1)),
                  _full_spec((1, W_A)), _full_spec((4, CHUNK, CHUNK)), _full_spec((CHUNK, W_A)),
                  _full_spec((LANES, LANES)), _full_spec((W_C, W_C)), _full_spec((1, W_C))],
        out_specs=[_row_spec(tm, D), _row_spec(tm, IN_COLS), _row_spec(tm, IN_COLS), _row_spec(tm, W_A),
                   _row_spec(tm, W_C)],
        out_shape=[jax.ShapeDtypeStruct((S, D), MXU_DT), jax.ShapeDtypeStruct((S, IN_COLS), F32),
                   jax.ShapeDtypeStruct((S, IN_COLS), MXU_DT), jax.ShapeDtypeStruct((S, W_A), F32),
                   jax.ShapeDtypeStruct((S, W_C), F32)],
        scratch_shapes=[pltpu.VMEM((HALO, W_C), F32)],
        compiler_params=_params("arbitrary"),
    )(x, g, w, sng, wm, bias, gmat, wbd, scale)


def _pool_bwd_a(proj, dy, wbd, scale):
    S = proj.shape[0]
    tm = TM

    def body(p_ref, ph_ref, dy_ref, w_ref, sc_ref, dd_ref, e_ref, dw_ref, dsc_ref):
        i = pl.program_id(0)
        halo = jnp.where(i > 0, ph_ref[...], 0.0)
        d, cnt = _pool_diff(p_ref[...], halo, i * tm, tm)
        db = d.astype(MXU_DT)
        dy = dy_ref[...]

        @pl.when(i == 0)
        def _():
            dw_ref[...] = jnp.zeros_like(dw_ref)
            dsc_ref[...] = jnp.zeros_like(dsc_ref)

        dsc_ref[...] += jnp.sum(dy * _dot(db, w_ref[...]), axis=0, keepdims=True)
        dys = (dy * sc_ref[...]).astype(MXU_DT)
        dw_ref[...] += _dot_tn(db, dys)
        dd = _dot_nt(dys, w_ref[...])
        dd_ref[...] = dd
        e_ref[...] = dd / cnt

    cur, prev = _pool_specs(tm, S // HALO)
    return _call(
        body, name="pool_bwd_a", grid=(S // tm,),
        in_specs=[cur, prev, _row_spec(tm, W_C), _full_spec((W_C, W_C)), _full_spec((1, W_C))],
        out_specs=[_row_spec(tm, W_C), _row_spec(tm, W_C), _full_spec((W_C, W_C)), _full_spec((1, W_C))],
        out_shape=[jax.ShapeDtypeStruct((S, W_C), F32), jax.ShapeDtypeStruct((S, W_C), F32),
                   jax.ShapeDtypeStruct((W_C, W_C), F32), jax.ShapeDtypeStruct((1, W_C), F32)],
        compiler_params=_params("arbitrary"),
    )(proj, proj, dy, wbd, scale)


def _pool_bwd_b(dd, e):
    S = dd.shape[0]
    tm = TM
    nb = S // tm

    def body(dd_ref, e_ref, en_ref, dp_ref):
        i = pl.program_id(0)
        halo = jnp.where(i < nb - 1, en_ref[...], 0.0)
        ext = jnp.concatenate([e_ref[...], halo], axis=0)
        n = ext.shape[0]
        g, _ = _window_lanes()
        acc = ext
        sums = []
        for sh in (1, 2, 4, 8):
            acc = acc + pltpu.roll(acc, shift=n - sh, axis=0)
            sums.append(acc[:tm])
        wsum = jnp.where(g == 0, sums[0], jnp.where(g == 1, sums[1], jnp.where(g == 2, sums[2], sums[3])))
        dp_ref[...] = (wsum - dd_ref[...]).astype(dp_ref.dtype)

    nxt = pl.BlockSpec((HALO, W_C), lambda i: (jnp.minimum((i + 1) * (tm // HALO), S // HALO - 1), 0))
    return _call(
        body, name="pool_bwd_b", grid=(nb,),
        in_specs=[_row_spec(tm, W_C), _row_spec(tm, W_C), nxt],
        out_specs=_row_spec(tm, W_C),
        out_shape=jax.ShapeDtypeStruct((S, W_C), MXU_DT),
        compiler_params=_params("parallel"),
    )(dd, e, e)


TN_FF = 1408
NB_FF = D_FF // TN_FF
CONV_ROWS = 8


def _conv(z_cur, z_halo, cwb, tm):
    ext = jnp.concatenate([z_halo, z_cur], axis=0)
    z2 = _shift_rows(ext, 2, tm, HALO)
    z1 = _shift_rows(ext, 1, tm, HALO)
    zc = cwb[3:4] + z2 * cwb[0:1] + z1 * cwb[1:2] + z_cur * cwb[2:3]
    return zc, z2, z1


def _up_proj_gate(x, g, w, cwb):
    S, D = x.shape
    tm = TM

    def body(x_ref, g_ref, w_ref, c_ref, h_ref, z_ref, zc_ref, f_ref, tail_ref):
        first = pl.program_id(0) == 0
        xv = x_ref[...]
        r = lax.rsqrt(jnp.mean(xv * xv, axis=-1, keepdims=True) + EPS)
        h = (xv * r * g_ref[...]).astype(h_ref.dtype)
        h_ref[...] = h
        for j in range(NB_FF):
            halves = []
            for col0 in (j * TN_FF, D_FF + j * TN_FF):
                zb = _dot(h, w_ref[:, col0:col0 + TN_FF]).astype(z_ref.dtype)
                z_ref[:, col0:col0 + TN_FF] = zb
                zf = zb.astype(F32)
                prev = jnp.where(first, 0.0, tail_ref[:, col0:col0 + TN_FF])
                tail_ref[:, col0:col0 + TN_FF] = zf[tm - HALO:]
                zc = _conv(zf, prev, c_ref[:, col0:col0 + TN_FF], tm)[0]
                zc_ref[:, col0:col0 + TN_FF] = zc.astype(zc_ref.dtype)
                halves.append(zc)
            gate, value = halves
            f_ref[:, j * TN_FF:(j + 1) * TN_FF] = (gate * jax.nn.sigmoid(gate) * value).astype(f_ref.dtype)

    return _call(
        body, name="up_proj_gate", grid=(S // tm,),
        in_specs=[_row_spec(tm, D), _full_spec((1, D)),
                  pl.BlockSpec((D, 2 * D_FF), lambda i: (0, 0), pipeline_mode=pl.Buffered(1)),
                  _full_spec((CONV_ROWS, 2 * D_FF))],
        out_specs=[_row_spec(tm, D), _row_spec(tm, 2 * D_FF), _row_spec(tm, 2 * D_FF), _row_spec(tm, D_FF)],
        out_shape=[jax.ShapeDtypeStruct((S, D), MXU_DT), jax.ShapeDtypeStruct((S, 2 * D_FF), MXU_DT),
                   jax.ShapeDtypeStruct((S, 2 * D_FF), MXU_DT), jax.ShapeDtypeStruct((S, D_FF), MXU_DT)],
        scratch_shapes=[pltpu.VMEM((HALO, 2 * D_FF), F32)],
        compiler_params=_params("arbitrary"),
    )(x, g, w, cwb)


def _gate_up_bwd(z, zc, cwb, w, wd, x, g, dres):
    S, D = x.shape
    tm = TM
    nb = S // tm

    def body(z_ref, zc_ref, zcn_ref, c_ref, w_ref, wd_ref, x_ref, g_ref, r_ref, rn_ref,
             dz_ref, dc_ref, dx_ref, dg_ref):
        i = pl.program_id(0)
        first = i == 0
        last = i == nb - 1
        dxe = jnp.concatenate([r_ref[...], jnp.where(last, 0.0, rn_ref[...])], axis=0).astype(MXU_DT)

        @pl.when(first)
        def _():
            dc_ref[...] = jnp.zeros_like(dc_ref)
            dg_ref[...] = jnp.zeros_like(dg_ref)

        rid = lax.broadcasted_iota(jnp.int32, (CONV_ROWS, TN_FF), 0)

        def conv_out(cols):
            return jnp.concatenate([zc_ref[:, cols].astype(F32), zcn_ref[:, cols].astype(F32)], axis=0)

        def conv_bwd(d, z0, c):
            d0 = d[:tm]
            d1 = _shift_rows(d, -1, tm, 0)
            d2 = _shift_rows(d, -2, tm, 0)
            sums = [jnp.sum(d2 * z0, axis=0, keepdims=True), jnp.sum(d1 * z0, axis=0, keepdims=True),
                    jnp.sum(d0 * z0, axis=0, keepdims=True), jnp.sum(d0, axis=0, keepdims=True)]
            dtaps = jnp.zeros((CONV_ROWS, TN_FF), F32)
            for k, v in enumerate(sums):
                dtaps = jnp.where(rid == k, v, dtaps)
            return d0 * c[2:3] + d1 * c[1:2] + d2 * c[0:1], dtaps

        dh = jnp.zeros((tm, D), F32)
        for j in range(NB_FF):
            gc = slice(j * TN_FF, (j + 1) * TN_FF)
            uc = slice(D_FF + j * TN_FF, D_FF + (j + 1) * TN_FF)
            gt = conv_out(gc)
            ut = conv_out(uc)
            df = _dot_nt(dxe, wd_ref[gc, :])
            sg = jax.nn.sigmoid(gt)
            dzg, dtg = conv_bwd(df * ut * (sg * (1.0 + gt * (1.0 - sg))), z_ref[:, gc].astype(F32), c_ref[:, gc])
            dzu, dtu = conv_bwd(df * (gt * sg), z_ref[:, uc].astype(F32), c_ref[:, uc])
            dzg = dzg.astype(dz_ref.dtype)
            dzu = dzu.astype(dz_ref.dtype)
            dz_ref[:, gc] = dzg
            dz_ref[:, uc] = dzu
            dc_ref[:, gc] += dtg
            dc_ref[:, uc] += dtu
            dh += _dot_nt(dzg, w_ref[:, gc]) + _dot_nt(dzu, w_ref[:, uc])

        xv = x_ref[...]
        r = lax.rsqrt(jnp.mean(xv * xv, axis=-1, keepdims=True) + EPS)
        xhat = xv * r
        dg_ref[...] += jnp.sum(dh * xhat, axis=0, keepdims=True)
        dxh = dh * g_ref[...]
        dx_ref[...] = r_ref[...] + r * (dxh - xhat * jnp.mean(dxh * xhat, axis=-1, keepdims=True))

    hb = tm // HALO
    last_halo = S // HALO - 1
    return _call(
        body, name="gate_up_bwd", grid=(nb,),
        in_specs=[_row_spec(tm, 2 * D_FF), _row_spec(tm, 2 * D_FF),
                  pl.BlockSpec((HALO, 2 * D_FF), lambda i: (jnp.minimum((i + 1) * hb, last_halo), 0)),
                  _full_spec((CONV_ROWS, 2 * D_FF)),
                  pl.BlockSpec((D, 2 * D_FF), lambda i: (0, 0), pipeline_mode=pl.Buffered(1)),
                  pl.BlockSpec((D_FF, D), lambda i: (0, 0), pipeline_mode=pl.Buffered(1)),
                  _row_spec(tm, D), _full_spec((1, D)), _row_spec(tm, D),
                  pl.BlockSpec((HALO, D), lambda i: (jnp.minimum((i + 1) * hb, last_halo), 0))],
        out_specs=[_row_spec(tm, 2 * D_FF), _full_spec((CONV_ROWS, 2 * D_FF)), _row_spec(tm, D), _full_spec((1, D))],
        out_shape=[jax.ShapeDtypeStruct((S, 2 * D_FF), MXU_DT), jax.ShapeDtypeStruct((CONV_ROWS, 2 * D_FF), F32),
                   jax.ShapeDtypeStruct((S, D), F32), jax.ShapeDtypeStruct((1, D), F32)],
        compiler_params=_params("arbitrary"),
    )(z, zc, zc, cwb, w, wd, x, g, dres, dres)


def _layer_consts(w, l):
    wm = _tril_weights(w["sgu_w"][l])
    eye = jnp.eye(4, dtype=F32)
    wbd = (w["pool_w"][l][:, :, None, :] * eye[:, None, :, None]).reshape(W_C, W_C)
    cwb = jnp.concatenate([w["conv_w"][l], w["conv_b"][l][None], jnp.zeros((CONV_ROWS - 4, 2 * D_FF), F32)], axis=0)
    return dict(
        g1=w["norm1_g"][l][None], g2=w["norm2_g"][l][None], gm=w["mix_norm_g"][l][None],
        sng=w["sgu_norm_g"][l][None], wm=wm.astype(MXU_DT), wmt=jnp.swapaxes(wm, 1, 2).astype(MXU_DT),
        bias=jnp.repeat(jnp.transpose(w["sgu_b"][l]), HEAD_DIM, axis=1),
        wbd=wbd.astype(MXU_DT), scale=w["pool_scale"][l][None], cwb=cwb,
        w_in=w["w_in"][l], w_o=w["w_o"][l], w_up=w["w_up"][l], w_down=w["w_down"][l],
    )


def _local_step(x, tgt, w, late=None, early_exchange=None):
    gmat = _group_matrix()
    tri = _tri_matrix()
    trit = jnp.transpose(tri)
    saved = []
    early = None
    for l in range(DEPTH):
        c = _layer_consts(w, l)
        h1, proj, proj_b, ya, yc = _in_proj_groups(x, c["g1"], c["w_in"], c["sng"], c["wm"], c["bias"], gmat,
                                                   c["wbd"], c["scale"])
        if l == 0 and late is not None:
            assert DEPTH == 2
            yb, rb, *gathered = _sb_fwd(proj_b, tri, carry_gather=late[0])
            w = dict(w)
            for name, arr in late[1](gathered).items():
                w[name] = [w[name][0], arr]
        else:
            yb, rb = _sb_fwd(proj_b, tri)
        x2, yn = _mix_out(ya, yb, yc, c["gm"], c["w_o"], x, gmat)
        h2, z, zc, f = _up_proj_gate(x2, c["g2"], c["w_up"], c["cwb"])
        saved.append(dict(c=c, x=x, proj=proj, proj_b=proj_b, h1=h1, ya=ya, yb=yb, yc=yc, rb=rb, x2=x2, yn=yn,
                          z=z, zc=zc, h2=h2, f=f))
        if l < DEPTH - 1:
            x = _mm_res(f, c["w_down"], x2, "down_proj")

    last = saved[-1]
    dx, d_final_g, loss8 = _down_proj_loss(last["f"], last["c"]["w_down"], last["x2"], w["final_g"][None], tgt)
    grads = {n: [None] * DEPTH for n in ("norm1_g", "w_in", "sgu_norm_g", "sgu_w", "sgu_b", "pool_w", "pool_scale",
                                         "mix_norm_g", "w_o", "norm2_g", "w_up", "conv_w", "conv_b", "w_down")}
    for l in reversed(range(DEPTH)):
        s = saved[l]
        c = s["c"]
        grads["w_down"][l] = _mm_tn(s["f"], dx, "down_proj_wgrad").reshape(N_CHIPS, D_FF // N_CHIPS, D_MODEL)
        dz, dcwb, dx2, dg2 = _gate_up_bwd(s["z"], s["zc"], c["cwb"], c["w_up"], c["w_down"], s["x2"], c["g2"], dx)
        grads["conv_w"][l] = dcwb[:3]
        grads["conv_b"][l] = dcwb[3]
        grads["w_up"][l] = _mm_tn(s["h2"], dz, "up_proj_wgrad", col_tiles=True)
        grads["norm2_g"][l] = dg2[0]
        grads["w_o"][l] = _mm_tn(s["yn"], dx2, "out_proj_wgrad").reshape(N_CHIPS, D_MODEL // N_CHIPS, D_MODEL)
        if l == 0 and early_exchange is not None:
            early_items = early_exchange[0](grads)
            dya, dyb, dyc, dgm, *swapped = _mix_out_bwd(dx2, c["w_o"], s["ya"], s["yb"], s["yc"], c["gm"], gmat,
                                                        carry_swap=early_items)
        else:
            dya, dyb, dyc, dgm = _mix_out_bwd(dx2, c["w_o"], s["ya"], s["yb"], s["yc"], c["gm"], gmat)
        grads["mix_norm_g"][l] = dgm[0]
        dd, e, dwbd, dscale = _pool_bwd_a(s["proj"], dyc, c["wbd"], c["scale"])
        dp = _pool_bwd_b(dd, e)
        grads["pool_w"][l] = jnp.stack([dwbd[g * 64:(g + 1) * 64, g * 64:(g + 1) * 64] for g in range(4)])
        grads["pool_scale"][l] = dscale[0]
        if l == 0 and early_exchange is not None:
            sent = early_exchange[1](early_items, swapped)
            dq, dk, dv, *parts = _sb_bwd(s["proj_b"], dyb, s["rb"], tri, trit, carry_exchange=sent)
            early = (sent, parts)
        else:
            dq, dk, dv = _sb_bwd(s["proj_b"], dyb, s["rb"], tri, trit)
        da, dwm, dbias, dsng = _sgu_bwd(s["proj"], dya, c["sng"], c["wm"], c["wmt"], c["bias"], gmat)
        grads["sgu_w"][l] = dwm
        grads["sgu_b"][l] = jnp.transpose(jnp.sum(dbias.reshape(CHUNK, 4, HEAD_DIM), axis=-1))
        grads["sgu_norm_g"][l] = dsng[0]
        dproj = jnp.concatenate([da, dq, dk.astype(MXU_DT), dv.astype(MXU_DT), dp], axis=1)
        dw_in = _mm_tn(s["h1"], dproj, "in_proj_wgrad")
        grads["w_in"][l] = jnp.transpose(dw_in.reshape(D_MODEL, N_CHIPS, IN_COLS // N_CHIPS), (1, 0, 2))
        dx, dg1 = _mm_nt_rmsbwd([(dproj, c["w_in"])], s["x"], c["g1"], dx2, "in_proj_bwd")
        grads["norm1_g"][l] = dg1[0]

    out = {n: (v if n in BIG_NAMES[:4] else jnp.stack(v)) for n, v in grads.items()}
    out["final_g"] = d_final_g[0]
    return loss8[0, 0], dx, out, early


MESH = pl.DeviceIdType.MESH
ANY = pl.BlockSpec(memory_space=pl.ANY)


def _gather_plan(ins, outs, send_sems, recv_sems, layer, sender):
    n = len(ins)
    x, y, c = lax.axis_index("x"), lax.axis_index("y"), lax.axis_index("c")
    sibling = (x, y, 1 - c)
    my_chip = 2 * x + y
    chips = [(1 - x, y), (x, 1 - y), (1 - x, 1 - y)]
    ids = [2 * px + py for px, py in chips]

    def copy(a, k, chip, to, own=False):
        dst = outs[a].at[chip]
        return pltpu.make_async_remote_copy(
            src_ref=ins[a].at[layer] if own else dst, dst_ref=dst,
            send_sem=send_sems.at[a, k], recv_sem=recv_sems.at[a, k], device_id=to, device_id_type=MESH)

    sends = [copy(a, j, my_chip, (*chips[j], sender), own=True) for j in range(3) for a in range(n)]
    arrivals = [copy(a, j, ids[j], sibling) for j in range(3) for a in range(n)]
    forwards = [copy(a, 3 + j, ids[j], sibling) for j in range(3) for a in range(n)]
    return c, sends, arrivals, forwards


def _gather_shapes(shards):
    n = len(shards)
    return ([jax.ShapeDtypeStruct((N_CHIPS,) + s.shape[1:], s.dtype) for s in shards],
            [pltpu.SemaphoreType.DMA((n, 6)), pltpu.SemaphoreType.DMA((n, 6))])


def _all_gather(shards, layer, sender):
    n = len(shards)

    def body(*refs):
        c, sends, arrivals, forwards = _gather_plan(refs[:n], refs[n:2 * n], refs[2 * n], refs[2 * n + 1],
                                                    layer, sender)

        @pl.when(c == sender)
        def _():
            for cp in sends:
                cp.start()
            for arrived, onward in zip(arrivals, forwards):
                arrived.wait_recv()
                onward.start()
            for cp in sends + forwards:
                cp.wait_send()

        @pl.when(c != sender)
        def _():
            for cp in forwards:
                cp.wait_recv()

    out_shape, sems = _gather_shapes(shards)
    return _call(body, name="weight_all_gather", out_shape=out_shape, in_specs=[ANY] * n, out_specs=[ANY] * n,
                 scratch_shapes=sems)(*shards)


def _row_tile(r):
    return r if r <= 704 else 256


def _grad_swap(items, name):
    n = len(items)

    def body(*refs):
        start, finish = _swap_plan(refs[:n], refs[n:2 * n], refs[2 * n:3 * n], refs[3 * n], refs[3 * n + 1])
        start()
        finish()

    out_shape, sems = _swap_shapes(items)
    return _call(body, name=name, out_shape=out_shape, in_specs=[ANY] * (2 * n), out_specs=[ANY] * n,
                 scratch_shapes=sems)(*[a0 for a0, _ in items], *[a1 for _, a1 in items])


def _swap_plan(firsts, seconds, outs, send_sems, recv_sems):
    n = len(firsts)
    x, y, c = lax.axis_index("x"), lax.axis_index("y"), lax.axis_index("c")

    def copies(srcs):
        return [pltpu.make_async_remote_copy(src_ref=srcs[a], dst_ref=outs[a], send_sem=send_sems.at[a],
                                             recv_sem=recv_sems.at[a], device_id=(x, y, 1 - c),
                                             device_id_type=MESH) for a in range(n)]

    def start():
        @pl.when(c == 0)
        def _():
            for cp in copies(seconds):
                cp.start()

        @pl.when(c == 1)
        def _():
            for cp in copies(firsts):
                cp.start()

    def finish():
        for cp in copies(firsts):
            cp.wait()

    return start, finish


def _swap_shapes(items):
    n = len(items)
    return ([jax.ShapeDtypeStruct(a0.shape, a0.dtype) for a0, _ in items],
            [pltpu.SemaphoreType.DMA((n,)), pltpu.SemaphoreType.DMA((n,))])


def _pair_add(a0, a1, r, c_arr, name, out_dtype):
    k, rr, cc = r.shape
    tr = _row_tile(rr)

    def body(c_ref, a0_ref, a1_ref, r_ref, o_ref):
        mine = jnp.where(c_ref[0] == 0, a0_ref[...], a1_ref[...])
        o_ref[...] = (mine + r_ref[...]).astype(o_ref.dtype)

    def member(which):
        def index(kk, i, c_ref):
            used = (c_ref[0] == which).astype(jnp.int32)
            return (kk * used, i * used, 0)
        return pl.BlockSpec((1, tr, cc), index)

    spec = pl.BlockSpec((1, tr, cc), lambda kk, i, c_ref: (kk, i, 0))
    grid_spec = pltpu.PrefetchScalarGridSpec(num_scalar_prefetch=1, grid=(k, rr // tr),
                                             in_specs=[member(0), member(1), spec], out_specs=spec)
    return _call(body, name=name, grid_spec=grid_spec, out_shape=jax.ShapeDtypeStruct((k, rr, cc), out_dtype),
                 compiler_params=_params("parallel", "parallel"))(c_arr, a0, a1, r)


def _exchange_plan(ins, outs, send_sems, recv_sems):
    n = len(ins)
    x, y, c = lax.axis_index("x"), lax.axis_index("y"), lax.axis_index("c")
    my_chip = 2 * x + y
    chips = [(1 - x, y), (x, 1 - y), (1 - x, 1 - y)]

    def copy(a, k, src_chip, dst_chip):
        px, py = chips[k]
        return pltpu.make_async_remote_copy(
            src_ref=ins[a].at[src_chip], dst_ref=outs[a].at[dst_chip], send_sem=send_sems.at[a, k],
            recv_sem=recv_sems.at[a, k], device_id=(px, py, c), device_id_type=MESH)

    sends = [copy(a, k, 2 * chips[k][0] + chips[k][1], my_chip) for k in range(3) for a in range(n)]
    arrivals = [copy(a, k, my_chip, 2 * chips[k][0] + chips[k][1]) for k in range(3) for a in range(n)]
    return sends, arrivals


def _exchange_shapes(hs):
    n = len(hs)
    return ([jax.ShapeDtypeStruct(h.shape, h.dtype) for h in hs],
            [pltpu.SemaphoreType.DMA((n, 3)), pltpu.SemaphoreType.DMA((n, 3))])


def _grad_exchange(hs):
    n = len(hs)

    def body(*refs):
        sends, arrivals = _exchange_plan(refs[:n], refs[n:2 * n], refs[2 * n], refs[2 * n + 1])
        for cp in sends:
            cp.start()
        for cp in arrivals:
            cp.wait_recv()
        for cp in sends:
            cp.wait_send()

    out_shape, sems = _exchange_shapes(hs)
    return _call(body, name="grad_exchange_chips", out_shape=out_shape, in_specs=[ANY] * n, out_specs=[ANY] * n,
                 scratch_shapes=sems)(*hs)


def _sum_chips(a, c_arr, name):
    _, r, cc = a.shape
    tr = _row_tile(r)

    def body(c_ref, a_ref, o_ref):
        o_ref[...] = ((a_ref[0].astype(F32) + a_ref[1].astype(F32)) + a_ref[2].astype(F32)) + a_ref[3].astype(F32)

    grid_spec = pltpu.PrefetchScalarGridSpec(
        num_scalar_prefetch=1, grid=(r // tr,),
        in_specs=[pl.BlockSpec((N_CHIPS, tr, cc), lambda i, c_ref: (0, i, 0))],
        out_specs=pl.BlockSpec((None, tr, cc), lambda i, c_ref: (c_ref[0], i, 0)))
    return _call(body, name=name, grid_spec=grid_spec, out_shape=jax.ShapeDtypeStruct((2, r, cc), F32),
                 compiler_params=_params("parallel"))(c_arr, a)


def _grad_share(bufs):
    n = len(bufs)

    def body(*refs):
        outs = refs[n:2 * n]
        send_sems, recv_sems = refs[2 * n:]
        x, y, c = lax.axis_index("x"), lax.axis_index("y"), lax.axis_index("c")
        copies = [pltpu.make_async_remote_copy(src_ref=outs[a].at[c], dst_ref=outs[a].at[c], send_sem=send_sems.at[a],
                                               recv_sem=recv_sems.at[a], device_id=(x, y, 1 - c),
                                               device_id_type=MESH) for a in range(n)]
        for cp in copies:
            cp.start()
        for a in range(n):
            pltpu.make_async_remote_copy(src_ref=outs[a].at[c], dst_ref=outs[a].at[1 - c], send_sem=send_sems.at[a],
                                         recv_sem=recv_sems.at[a], device_id=(x, y, 1 - c),
                                         device_id_type=MESH).wait_recv()
        for cp in copies:
            cp.wait_send()

    return _call(
        body, name="grad_share_cores", out_shape=[jax.ShapeDtypeStruct(b.shape, b.dtype) for b in bufs],
        in_specs=[ANY] * n, out_specs=[ANY] * n, input_output_aliases={a: a for a in range(n)},
        scratch_shapes=[pltpu.SemaphoreType.DMA((n,)), pltpu.SemaphoreType.DMA((n,))],
    )(*bufs)


def _adamw_math(g_ref, w_ref, m_ref, v_ref, d_ref, nm_ref, nv_ref):
    gv = g_ref[...]
    nm = ADAM_B1 * m_ref[...] + (1.0 - ADAM_B1) * gv
    nv = ADAM_B2 * v_ref[...] + (1.0 - ADAM_B2) * (gv * gv)
    m_hat = nm / (1.0 - ADAM_B1 ** ADAM_STEP)
    v_hat = nv / (1.0 - ADAM_B2 ** ADAM_STEP)
    d_ref[...] = -ADAM_LR * (m_hat / (jnp.sqrt(v_hat) + ADAM_EPS) + ADAM_WD * w_ref[...])
    nm_ref[...] = nm
    nv_ref[...] = nv


def _adamw_big(g, w, m, v, name):
    d, r, c = g.shape
    tr = r if r <= 704 else 256
    spec = pl.BlockSpec((1, tr, c), lambda l, i: (l, i, 0))

    def body(*refs):
        _adamw_math(*refs)

    shp = jax.ShapeDtypeStruct(g.shape, F32)
    return _call(body, name=name, grid=(d, r // tr), in_specs=[spec] * 4, out_specs=[spec] * 3,
                 out_shape=[shp, shp, shp], compiler_params=_params("parallel", "parallel"))(g, w, m, v)


def _adamw_small(gs, ws, ms, vs):
    n = len(gs)

    def body(*refs):
        ins, outs = refs[:4 * n], refs[4 * n:]
        for k in range(n):
            _adamw_math(ins[k], ins[n + k], ins[2 * n + k], ins[3 * n + k], outs[k], outs[n + k], outs[2 * n + k])

    shp = [jax.ShapeDtypeStruct(g.shape, F32) for g in gs]
    res = _call(body, name="adamw_small", out_shape=shp * 3)(*gs, *ws, *ms, *vs)
    return res[:n], res[n:2 * n], res[2 * n:]


def _rows(a, rows):
    flat = a.reshape(-1)
    return jnp.pad(flat, (0, rows * D_MODEL - flat.shape[0])).reshape(rows, D_MODEL)


def _small_rows(p, extra=None):
    parts = [p[n].reshape(-1) for n in SMALL_NAMES]
    if extra is not None:
        parts.append(extra.reshape(-1))
    flat = jnp.concatenate(parts)
    return jnp.pad(flat, (0, ROWS_SMALL * D_MODEL - flat.shape[0])).reshape(ROWS_SMALL, D_MODEL)


CONV_SHARD = (DEPTH, 3, 2 * D_FF // N_CHIPS)
N_CONV_SHARD = DEPTH * 3 * (2 * D_FF // N_CHIPS)


def _small_pack(g, loss):
    conv = jnp.transpose(g["conv_w"].reshape(DEPTH, 3, N_CHIPS, 2 * D_FF // N_CHIPS), (2, 0, 1, 3))
    conv = jnp.stack([_rows(conv[k], ROWS_CONV) for k in range(N_CHIPS)])
    small = jnp.broadcast_to(_small_rows(g, loss), (N_CHIPS, ROWS_SMALL, D_MODEL))
    return jnp.concatenate([conv, small], axis=1)


def _unpack_small(pack):
    out = {"conv_w": pack[:ROWS_CONV].reshape(-1)[:N_CONV_SHARD].reshape(CONV_SHARD)}
    flat = pack[ROWS_CONV:].reshape(-1)
    k = 0
    for name in SMALL_NAMES:
        shape = SMALL_SHAPES[name]
        n = 1
        for d in shape:
            n *= d
        out[name] = flat[k:k + n].reshape(shape)
        k += n
    out["extra"] = flat[k]
    return out


def _assemble_layer(gathered, shards, layer):
    my_chip = 2 * lax.axis_index("x") + lax.axis_index("y")
    w_in, w_o, w_up, w_down = [lax.dynamic_update_index_in_dim(got, own[layer], my_chip, 0)
                               for got, own in zip(gathered, shards)]

    def by_cols(a):
        k, r, wd = a.shape
        return jnp.transpose(a, (1, 0, 2)).reshape(r, k * wd)

    return dict(w_in=by_cols(w_in), w_o=w_o.reshape(-1, D_MODEL), w_up=by_cols(w_up),
                w_down=w_down.reshape(-1, D_MODEL))


def _gather_weights(p):
    shards = [p[n].astype(jnp.bfloat16) for n in BIG_NAMES[:4]]
    conv_all = p["conv_w"].reshape(1, -1, p["conv_w"].shape[-1])
    got = _all_gather(shards + [conv_all], 0, 0)
    my_chip = 2 * lax.axis_index("x") + lax.axis_index("y")
    conv = lax.dynamic_update_index_in_dim(got[4], conv_all[0], my_chip, 0)
    conv = jnp.transpose(conv.reshape((N_CHIPS,) + CONV_SHARD), (1, 2, 0, 3)).reshape(DEPTH, 3, 2 * D_FF)
    full = {n: [a, None] for n, a in _assemble_layer(got[:4], shards, 0).items()}
    full["conv_w"] = conv
    return full, (shards, lambda gathered: _assemble_layer(gathered, shards, 1))


def _halves(a):
    r = a.shape[1] // 2
    return a[:, :r], a[:, r:]


def _reduce_begin(items, names, dtypes, c_arr, tag):
    return _pair_adds(items, _grad_swap(items, "grad_swap_cores_" + tag), names, dtypes, c_arr)


def _pair_adds(items, got, names, dtypes, c_arr):
    return [_pair_add(a0, a1, r, c_arr, "grad_add_cores_" + nm, dt)
            for (a0, a1), r, nm, dt in zip(items, got, names, dtypes)]


def _reduce_end(parts, sent, names, c_arr):
    my_chip = 2 * lax.axis_index("x") + lax.axis_index("y")
    full = [lax.dynamic_update_index_in_dim(p, lax.dynamic_index_in_dim(own, my_chip, 0, keepdims=False), my_chip, 0)
            for p, own in zip(parts, sent)]
    return [_sum_chips(f, c_arr, "grad_sum_chips_" + nm) for f, nm in zip(full, names)]


EARLY_NAMES = ("w_o", "w_up", "w_down", "w_in_1")


def _early_items(grads):
    return [tuple(grads[n]) for n in ("w_o", "w_up", "w_down")] + [_halves(grads["w_in"][1])]


def kernel(x, norm1_g, w_in, sgu_norm_g, sgu_w, sgu_b, pool_w, pool_scale, mix_norm_g, w_o, norm2_g, w_up, conv_w, conv_b, w_down, final_g, loss_target, m_norm1_g, m_w_in, m_sgu_norm_g, m_sgu_w, m_sgu_b, m_pool_w, m_pool_scale, m_mix_norm_g, m_w_o, m_norm2_g, m_w_up, m_conv_w, m_conv_b, m_w_down, m_final_g, v_norm1_g, v_w_in, v_sgu_norm_g, v_sgu_w, v_sgu_b, v_pool_w, v_pool_scale, v_mix_norm_g, v_w_o, v_norm2_g, v_w_up, v_conv_w, v_conv_b, v_w_down, v_final_g):
    names = ("norm1_g", "w_in", "sgu_norm_g", "sgu_w", "sgu_b", "pool_w", "pool_scale", "mix_norm_g", "w_o",
             "norm2_g", "w_up", "conv_w", "conv_b", "w_down", "final_g")
    p = dict(zip(names, (norm1_g, w_in, sgu_norm_g, sgu_w, sgu_b, pool_w, pool_scale, mix_norm_g, w_o, norm2_g,
                         w_up, conv_w, conv_b, w_down, final_g)))
    pm = dict(zip(names, (m_norm1_g, m_w_in, m_sgu_norm_g, m_sgu_w, m_sgu_b, m_pool_w, m_pool_scale, m_mix_norm_g,
                          m_w_o, m_norm2_g, m_w_up, m_conv_w, m_conv_b, m_w_down, m_final_g)))
    pv = dict(zip(names, (v_norm1_g, v_w_in, v_sgu_norm_g, v_sgu_w, v_sgu_b, v_pool_w, v_pool_scale, v_mix_norm_g,
                          v_w_o, v_norm2_g, v_w_up, v_conv_w, v_conv_b, v_w_down, v_final_g)))
    c = lax.axis_index("c")
    gathered, late = _gather_weights(p)
    full = dict(p)
    full.update(gathered)

    c_arr = jnp.reshape(c, (1,)).astype(jnp.int32)
    early_types = [ICI_DT] * len(EARLY_NAMES)
    loss, dx, grads, (sent, received) = _local_step(
        x[0], loss_target[0], full, late,
        (_early_items, lambda items, swapped: _pair_adds(items, swapped, EARLY_NAMES, early_types, c_arr)))
    early_sums = _reduce_end(received, sent, EARLY_NAMES, c_arr)
    small_pack = _small_pack(grads, loss)
    late_names = ("w_in_0", "small")
    late_sent = _reduce_begin([_halves(grads["w_in"][0]), _halves(small_pack)], late_names, [ICI_DT, F32], c_arr, "late")
    late_sums = _reduce_end(_grad_exchange(late_sent), late_sent, late_names, c_arr)
    r_o, r_up, r_down, r_in1, r_in0, r_small = _grad_share(early_sums + late_sums)
    g = dict(w_o=r_o, w_up=r_up, w_down=r_down,
             w_in=jnp.stack([r_in0.reshape(D_MODEL, -1), r_in1.reshape(D_MODEL, -1)]))
    g.update(_unpack_small(r_small.reshape(2 * SP_HALF, D_MODEL)))
    d, nm, nv = {}, {}, {}
    for n in BIG_NAMES:
        d[n], nm[n], nv[n] = _adamw_big(g[n], p[n], pm[n], pv[n], "adamw_" + n)

    def two_d(a):
        return a.reshape(1, -1) if a.ndim == 1 else a

    ds, ms, vs = _adamw_small([two_d(g[n]) for n in SMALL_NAMES], [two_d(p[n]) for n in SMALL_NAMES],
                              [two_d(pm[n]) for n in SMALL_NAMES], [two_d(pv[n]) for n in SMALL_NAMES])
    for k, n in enumerate(SMALL_NAMES):
        d[n], nm[n], nv[n] = (a.reshape(p[n].shape) for a in (ds[k], ms[k], vs[k]))
    return (g["extra"], dx[None], *[g[n] for n in names], *[d[n] for n in names], *[nm[n] for n in names],
            *[nv[n] for n in names])
```

```python
import functools

import jax
import jax.numpy as jnp
from jax import lax
from jax.experimental import pallas as pl
from jax.experimental.pallas import tpu as pltpu

F32 = jnp.float32
MXU_DT = jnp.bfloat16

D_MODEL = 1024
DEPTH = 2
HEAD_DIM = 64
W_A = 256
W_B = 512
W_C = 256
IN_COLS = 2 * W_A + 3 * W_B + W_C
CHUNK = 128
POOL_WINDOWS = (2, 4, 8, 16)
D_FF = 2816
EPS = 1e-6
N_CHIPS = 4

ADAM_LR = 0.001
ADAM_B1 = 0.9
ADAM_B2 = 0.999
ADAM_EPS = 1e-08
ADAM_WD = 0.01
ADAM_STEP = 10

LANES = 128
TQ = 256
TK = 256
TM = 256
TM_MM = 512
HALO = 16
VMEM_LIMIT = 56 * 1024 * 1024

ROWS_CONV = 16
ROWS_SMALL = 240
SP_HALF = (ROWS_CONV + ROWS_SMALL) // 2
ICI_DT = jnp.bfloat16

BIG_NAMES = ("w_in", "w_o", "w_up", "w_down", "conv_w")
SMALL_NAMES = ("norm1_g", "sgu_norm_g", "sgu_w", "sgu_b", "pool_w", "pool_scale",
               "mix_norm_g", "norm2_g", "conv_b", "final_g")
SMALL_SHAPES = {
    "norm1_g": (DEPTH, D_MODEL), "sgu_norm_g": (DEPTH, W_A), "sgu_w": (DEPTH, 4, CHUNK, CHUNK),
    "sgu_b": (DEPTH, 4, CHUNK), "pool_w": (DEPTH, 4, 64, 64), "pool_scale": (DEPTH, W_C),
    "mix_norm_g": (DEPTH, D_MODEL), "norm2_g": (DEPTH, D_MODEL), "conv_b": (DEPTH, 2 * D_FF),
    "final_g": (D_MODEL,),
}


def _call(body, **kw):
    return pl.pallas_call(body, **kw)


def _params(*sem):
    return pltpu.CompilerParams(dimension_semantics=sem, vmem_limit_bytes=VMEM_LIMIT)


def _dot(a, b):
    return jnp.dot(a, b, preferred_element_type=F32)


def _dot_nt(a, b):
    return lax.dot_general(a, b, (((1,), (1,)), ((), ())), preferred_element_type=F32)


def _dot_tn(a, b):
    return lax.dot_general(a, b, (((0,), (0,)), ((), ())), preferred_element_type=F32)


def _group_mean(sq, gmat):
    sqb = sq.astype(MXU_DT)
    cols = [_dot(sqb[:, b * LANES:(b + 1) * LANES], gmat) for b in range(sq.shape[1] // LANES)]
    return cols[0] if len(cols) == 1 else jnp.concatenate(cols, axis=-1)


def _group_matrix():
    r = jnp.arange(LANES)
    return jnp.where((r[:, None] // HEAD_DIM) == (r[None, :] // HEAD_DIM), 1.0 / HEAD_DIM, 0.0).astype(MXU_DT)


def _tile(n):
    return max(t for t in range(LANES, 1536 + 1, LANES) if n % t == 0)


def _row_spec(tm, cols, col_block=0):
    return pl.BlockSpec((tm, cols), lambda i, cb=col_block: (i, cb))


def _full_spec(shape):
    nd = len(shape)
    return pl.BlockSpec(shape, lambda *_: (0,) * nd)


def _mm_res(a, w, res, name):
    S, K = a.shape
    N = w.shape[1]
    tm = TM_MM

    def body(a_ref, w_ref, r_ref, o_ref):
        o_ref[...] = r_ref[...] + _dot(a_ref[...], w_ref[...])

    return _call(
        body, name=name, grid=(S // tm,),
        in_specs=[_row_spec(tm, K), _full_spec((K, N)), _row_spec(tm, N)],
        out_specs=_row_spec(tm, N),
        out_shape=jax.ShapeDtypeStruct((S, N), F32),
        compiler_params=_params("parallel"),
    )(a, w, res)


def _mm_tn(a, b, name, col_tiles=False):
    S, K1 = a.shape
    N = b.shape[1]
    ts = 2 * TM_MM
    tk = _tile(K1)
    tn = _tile(N)
    if col_tiles:
        out_spec = pl.BlockSpec((None, tk, tn), lambda m, n, s: (n, m, 0))
        out_shape = jax.ShapeDtypeStruct((N // tn, K1, tn), F32)
    else:
        out_spec = pl.BlockSpec((tk, tn), lambda m, n, s: (m, n))
        out_shape = jax.ShapeDtypeStruct((K1, N), F32)

    def body(a_ref, b_ref, o_ref):
        @pl.when(pl.program_id(2) == 0)
        def _():
            o_ref[...] = jnp.zeros_like(o_ref)

        o_ref[...] += _dot_tn(a_ref[...], b_ref[...].astype(MXU_DT))

    return _call(
        body, name=name, grid=(K1 // tk, N // tn, S // ts),
        in_specs=[pl.BlockSpec((ts, tk), lambda m, n, s: (s, m)),
                  pl.BlockSpec((ts, tn), lambda m, n, s: (s, n))],
        out_specs=out_spec, out_shape=out_shape,
        compiler_params=_params("parallel", "parallel", "arbitrary"),
    )(a, b)


def _mm_nt_rmsbwd(pairs, x, g, dres, name):
    S, D = x.shape
    tm = TM
    n = len(pairs)

    def body(*refs):
        a_refs = refs[:n]
        w_refs = refs[n:2 * n]
        x_ref, g_ref, r_ref, dx_ref, dg_ref = refs[2 * n:]
        dh = _dot_nt(a_refs[0][...], w_refs[0][...])
        for k in range(1, n):
            dh += _dot_nt(a_refs[k][...], w_refs[k][...])
        xv = x_ref[...]
        r = lax.rsqrt(jnp.mean(xv * xv, axis=-1, keepdims=True) + EPS)
        xhat = xv * r

        @pl.when(pl.program_id(0) == 0)
        def _():
            dg_ref[...] = jnp.zeros_like(dg_ref)

        dg_ref[...] += jnp.sum(dh * xhat, axis=0, keepdims=True)
        dxh = dh * g_ref[...]
        dx_ref[...] = r_ref[...] + r * (dxh - xhat * jnp.mean(dxh * xhat, axis=-1, keepdims=True))

    in_specs = ([_row_spec(tm, a.shape[1]) for a, _ in pairs] + [_full_spec(w.shape) for _, w in pairs]
                + [_row_spec(tm, D), _full_spec((1, D)), _row_spec(tm, D)])
    return _call(
        body, name=name, grid=(S // tm,), in_specs=in_specs,
        out_specs=[_row_spec(tm, D), _full_spec((1, D))],
        out_shape=[jax.ShapeDtypeStruct((S, D), F32), jax.ShapeDtypeStruct((1, D), F32)],
        compiler_params=_params("arbitrary"),
    )(*[a for a, _ in pairs], *[w for _, w in pairs], x, g, dres)


def _down_proj_loss(a, w, res, g, tgt):
    S, D = res.shape
    K = a.shape[1]
    tm = TM

    def body(a_ref, w_ref, res_ref, g_ref, t_ref, dx_ref, dg_ref, l_ref):
        xv = res_ref[...] + _dot(a_ref[...], w_ref[...])
        r = lax.rsqrt(jnp.mean(xv * xv, axis=-1, keepdims=True) + EPS)
        xhat = xv * r
        diff = xhat * g_ref[...] - t_ref[...]

        @pl.when(pl.program_id(0) == 0)
        def _():
            dg_ref[...] = jnp.zeros_like(dg_ref)
            l_ref[...] = jnp.zeros_like(l_ref)

        l_ref[...] += jnp.full(l_ref.shape, 0.5 * jnp.sum(jnp.mean(diff * diff, axis=-1, keepdims=True)), F32)
        dout = diff * (1.0 / D)
        dg_ref[...] += jnp.sum(dout * xhat, axis=0, keepdims=True)
        dxh = dout * g_ref[...]
        dx_ref[...] = r * (dxh - xhat * jnp.mean(dxh * xhat, axis=-1, keepdims=True))

    return _call(
        body, name="down_proj_loss", grid=(S // tm,),
        in_specs=[_row_spec(tm, K), _full_spec((K, D)), _row_spec(tm, D), _full_spec((1, D)), _row_spec(tm, D)],
        out_specs=[_row_spec(tm, D), _full_spec((1, D)), _full_spec((8, LANES))],
        out_shape=[jax.ShapeDtypeStruct((S, D), F32), jax.ShapeDtypeStruct((1, D), F32),
                   jax.ShapeDtypeStruct((8, LANES), F32)],
        compiler_params=_params("arbitrary"),
    )(a, w, res, g, tgt)


def _mix_out(ya, yb, yc, gm, wo, x, gmat):
    S = x.shape[0]
    tm = TM

    def body(ya_ref, yb_ref, yc_ref, gm_ref, wo_ref, x_ref, gmat_ref, x2_ref, yn_ref):
        y = jnp.concatenate([ya_ref[...], yb_ref[...], yc_ref[...]], axis=-1)
        r = lax.rsqrt(_group_mean(y * y, gmat_ref[...]) + EPS)
        yn = (y * r * gm_ref[...]).astype(MXU_DT)
        yn_ref[...] = yn
        x2_ref[...] = x_ref[...] + _dot(yn, wo_ref[...])

    return _call(
        body, name="mix_out", grid=(S // tm,),
        in_specs=[_row_spec(tm, W_A), _row_spec(tm, W_B), _row_spec(tm, W_C), _full_spec((1, D_MODEL)),
                  _full_spec((D_MODEL, D_MODEL)), _row_spec(tm, D_MODEL), _full_spec((LANES, LANES))],
        out_specs=[_row_spec(tm, D_MODEL), _row_spec(tm, D_MODEL)],
        out_shape=[jax.ShapeDtypeStruct((S, D_MODEL), F32), jax.ShapeDtypeStruct((S, D_MODEL), MXU_DT)],
        compiler_params=_params("parallel"),
    )(ya, yb, yc, gm, wo, x, gmat)


def _mix_out_bwd(dx2, wo, ya, yb, yc, gm, gmat, carry_swap=None):
    S = dx2.shape[0]
    tm = TM
    nb = S // tm
    items = list(carry_swap or [])
    ns = len(items)

    def body(*refs):
        dx2_ref, wo_ref, ya_ref, yb_ref, yc_ref, gm_ref, gmat_ref = refs[:7]
        dya_ref, dyb_ref, dyc_ref, dgm_ref = refs[7 + 2 * ns:11 + 2 * ns]
        if ns:
            start, finish = _swap_plan(refs[7:7 + ns], refs[7 + ns:7 + 2 * ns], refs[11 + 2 * ns:11 + 3 * ns],
                                       refs[11 + 3 * ns], refs[12 + 3 * ns])

            @pl.when(pl.program_id(0) == 0)
            def _():
                start()

        dyn = _dot_nt(dx2_ref[...].astype(MXU_DT), wo_ref[...])
        y = jnp.concatenate([ya_ref[...], yb_ref[...], yc_ref[...]], axis=-1)
        r = lax.rsqrt(_group_mean(y * y, gmat_ref[...]) + EPS)
        yhat = y * r

        @pl.when(pl.program_id(0) == 0)
        def _():
            dgm_ref[...] = jnp.zeros_like(dgm_ref)

        dgm_ref[...] += jnp.sum(dyn * yhat, axis=0, keepdims=True)
        dyh = dyn * gm_ref[...]
        dy = r * (dyh - yhat * _group_mean(dyh * yhat, gmat_ref[...]))
        dya_ref[...] = dy[:, :W_A]
        dyb_ref[...] = dy[:, W_A:W_A + W_B]
        dyc_ref[...] = dy[:, W_A + W_B:]

        if ns:
            @pl.when(pl.program_id(0) == nb - 1)
            def _():
                finish()

    swapped, sems = _swap_shapes(items) if ns else ([], [])
    return _call(
        body, name="mix_out_bwd_swap" if ns else "mix_out_bwd", grid=(nb,),
        in_specs=[_row_spec(tm, D_MODEL), _full_spec((D_MODEL, D_MODEL)), _row_spec(tm, W_A), _row_spec(tm, W_B),
                  _row_spec(tm, W_C), _full_spec((1, D_MODEL)), _full_spec((LANES, LANES))] + [ANY] * (2 * ns),
        out_specs=[_row_spec(tm, W_A), _row_spec(tm, W_B), _row_spec(tm, W_C), _full_spec((1, D_MODEL))]
        + [ANY] * ns,
        out_shape=[jax.ShapeDtypeStruct((S, W_A), F32), jax.ShapeDtypeStruct((S, W_B), F32),
                   jax.ShapeDtypeStruct((S, W_C), F32), jax.ShapeDtypeStruct((1, D_MODEL), F32)] + swapped,
        scratch_shapes=sems,
        compiler_params=_params("arbitrary"),
    )(dx2, wo, ya, yb, yc, gm, gmat, *[a0 for a0, _ in items], *[a1 for _, a1 in items])


_SQRT_HALF = 0.7071067811865476
_INV_SQRT_2PI = 0.3989422804014327


def _sgu_common(a, sng, wm_ref, bias, gmat):
    phi = 0.5 * (1.0 + lax.erf(a * _SQRT_HALF))
    ga = a * phi
    u = ga[:, :W_A]
    v = ga[:, W_A:]
    r = lax.rsqrt(_group_mean(v * v, gmat) + EPS)
    vhat = v * r
    vn = (vhat * sng).astype(MXU_DT)
    head = lax.broadcasted_iota(jnp.int32, (CHUNK, W_A), 1) // HEAD_DIM
    rows = []
    for c in range(a.shape[0] // CHUNK):
        vc = vn[c * CHUNK:(c + 1) * CHUNK]
        s = bias
        for h in range(4):
            s = s + jnp.where(head == h, _dot(wm_ref[h], vc), 0.0)
        rows.append(s)
    s = jnp.concatenate(rows, axis=0)
    return phi, u, r, vhat, vn, s


def _tril_weights(sgu_w_l):
    t = jnp.arange(CHUNK)
    return jnp.where((t[None, :] <= t[:, None])[None], sgu_w_l, 0.0)


def _sgu_bwd(proj, dy, sng, wm, wmt, bias, gmat):
    S = proj.shape[0]
    tm = TM

    def body(a_ref, dy_ref, sng_ref, wm_ref, wmt_ref, b_ref, gmat_ref, da_ref, dw_ref, db_ref, dsng_ref):
        a = a_ref[...]
        dy = dy_ref[...]
        gmat = gmat_ref[...]
        sng = sng_ref[...]
        phi, u, r, vhat, vn, s = _sgu_common(a, sng, wm_ref, b_ref[...], gmat)
        du = dy * s
        ds = dy * u

        @pl.when(pl.program_id(0) == 0)
        def _():
            dw_ref[...] = jnp.zeros_like(dw_ref)
            db_ref[...] = jnp.zeros_like(db_ref)
            dsng_ref[...] = jnp.zeros_like(dsng_ref)

        head = lax.broadcasted_iota(jnp.int32, (CHUNK, W_A), 1) // HEAD_DIM
        tt = lax.broadcasted_iota(jnp.int32, (CHUNK, CHUNK), 0)
        ss = lax.broadcasted_iota(jnp.int32, (CHUNK, CHUNK), 1)
        rows = []
        for c in range(tm // CHUNK):
            dsc = ds[c * CHUNK:(c + 1) * CHUNK]
            vc = vn[c * CHUNK:(c + 1) * CHUNK]
            db_ref[...] += dsc
            dsb = dsc.astype(MXU_DT)
            dvn = jnp.zeros((CHUNK, W_A), F32)
            for h in range(4):
                dvn = dvn + jnp.where(head == h, _dot(wmt_ref[h], dsb), 0.0)
                dsh = jnp.where(head == h, dsc, 0.0).astype(MXU_DT)
                dw_ref[h] += jnp.where(ss <= tt, _dot_nt(dsh, vc), 0.0)
            rows.append(dvn)
        dvn = jnp.concatenate(rows, axis=0)
        dsng_ref[...] += jnp.sum(dvn * vhat, axis=0, keepdims=True)
        dvh = dvn * sng
        dv = r * (dvh - vhat * _group_mean(dvh * vhat, gmat))
        dga = jnp.concatenate([du, dv], axis=-1)
        dgelu = phi + a * (_INV_SQRT_2PI * jnp.exp(-0.5 * a * a))
        da_ref[...] = (dga * dgelu).astype(da_ref.dtype)

    return _call(
        body, name="sgu_bwd", grid=(S // tm,),
        in_specs=[_row_spec(tm, 2 * W_A), _row_spec(tm, W_A), _full_spec((1, W_A)), _full_spec((4, CHUNK, CHUNK)),
                  _full_spec((4, CHUNK, CHUNK)), _full_spec((CHUNK, W_A)), _full_spec((LANES, LANES))],
        out_specs=[_row_spec(tm, 2 * W_A), _full_spec((4, CHUNK, CHUNK)), _full_spec((CHUNK, W_A)),
                   _full_spec((1, W_A))],
        out_shape=[jax.ShapeDtypeStruct((S, 2 * W_A), MXU_DT), jax.ShapeDtypeStruct((4, CHUNK, CHUNK), F32),
                   jax.ShapeDtypeStruct((CHUNK, W_A), F32), jax.ShapeDtypeStruct((1, W_A), F32)],
        compiler_params=_params("arbitrary"),
    )(proj, dy, sng, wm, wmt, bias, gmat)


HG = 4
LW = HG * HEAD_DIM
Q_BLK0 = (2 * W_A) // LW
K_BLK0 = Q_BLK0 + W_B // LW
V_BLK0 = K_BLK0 + W_B // LW
N_GROUPS = W_B // LW
EXP_IS_ZERO_BELOW = -120.0


def _tri_matrix():
    r = jnp.arange(TK)
    return (r[:, None] > r[None, :]).astype(MXU_DT)


def _stack_heads(a):
    head = lax.broadcasted_iota(jnp.int32, a.shape, 1) // HEAD_DIM
    return jnp.concatenate([jnp.where(head == h, a, 0.0) for h in range(HG)], axis=0).astype(MXU_DT)


def _unstack_heads(a):
    head = lax.broadcasted_iota(jnp.int32, (TQ, LW), 1) // HEAD_DIM
    out = a[:TQ]
    for h in range(1, HG):
        out = jnp.where(head == h, a[h * TQ:(h + 1) * TQ], out)
    return out


def _sb_scores(q2, kj, tri, key_offset):
    z = _dot_nt(q2, kj)
    sp = jnp.log(1.0 + jnp.exp(-jnp.abs(z)))
    lsp = jnp.minimum(z, 0.0) - sp
    lsm = lsp - z
    msk = None
    if key_offset is not None:
        row = lax.broadcasted_iota(jnp.int32, z.shape, 0) & (TQ - 1)
        col = lax.broadcasted_iota(jnp.int32, z.shape, 1) + key_offset
        msk = col < row
        lsm = jnp.where(msk, lsm, 0.0)
    tail = _dot(lsm.astype(MXU_DT), tri)
    return lsp, lsm, tail, msk


def _sb_fwd(proj_b, tri, carry_gather=None):
    S = proj_b.shape[0]
    nq = S // TQ
    kpq = TQ // TK
    assert S // TK < LANES
    shards = list(carry_gather or [])
    ng = len(shards)

    def body(*refs):
        q_ref, k_ref, v_ref, tri_ref = refs[:4]
        o_ref, rb_ref = refs[4 + ng:6 + ng]
        acc_ref = refs[6 + 2 * ng]
        i = pl.program_id(1)
        if ng:
            step = pl.program_id(0) * nq + i
            sender = 1
            c, sends, arrivals, forwards = _gather_plan(refs[4:4 + ng], refs[6 + ng:6 + 2 * ng], refs[7 + 2 * ng],
                                                        refs[8 + 2 * ng], 1, sender)

            @pl.when(jnp.logical_and(step == 0, c == sender))
            def _():
                for cp in sends:
                    cp.start()

            @pl.when(jnp.logical_and(step == N_GROUPS * nq - max(nq // 8, 1), c == sender))
            def _():
                for arrived, onward in zip(arrivals, forwards):
                    arrived.wait_recv()
                    onward.start()


        lane2 = lax.broadcasted_iota(jnp.int32, (HG * TQ, LANES), 1)
        q2 = _stack_heads(q_ref[...].astype(F32) * (HEAD_DIM ** -0.5))
        tri = tri_ref[...]
        rb_ref[...] = jnp.zeros_like(rb_ref)

        def block(j, run, key_offset=None, first=False):
            start = pl.multiple_of(j * TK, TK)
            kj = k_ref[pl.ds(start, TK), :]
            vj = v_ref[pl.ds(start, TK), :]
            lsp, lsm, tail, msk = _sb_scores(q2, kj, tri, key_offset)
            rb_ref[...] = jnp.where(lane2 == j, run, rb_ref[...])
            att = jnp.exp(lsp + tail + run)
            if msk is not None:
                att = jnp.where(msk, att, 0.0)
            pv = _dot(att.astype(MXU_DT), vj)
            if first:
                acc_ref[...] = pv
            else:
                acc_ref[...] += pv
            return run + tail[:, :1] + lsm[:, :1]

        past = i * kpq

        def overlapping():
            run = jnp.zeros((HG * TQ, 1), F32)
            for d in reversed(range(kpq)):
                run = block(i * kpq + d, run, key_offset=d * TK, first=(d == kpq - 1))
            return run

        def alive(run):
            return (jnp.max(run) > EXP_IS_ZERO_BELOW).astype(jnp.int32)

        def walk(carry):
            n, run, _ = carry
            run = block(past - 1 - n, run)
            return n + 1, run, alive(run)

        n, run = lax.cond(past > 0, lambda: (jnp.int32(1), block(past - 1, overlapping())),
                          lambda: (jnp.int32(0), overlapping()))
        n, _, _ = lax.while_loop(lambda s: jnp.logical_and(s[0] < past, s[2] > 0), walk, (n, run, alive(run)))
        rb_ref[...] = jnp.where(lane2 == LANES - 1, n.astype(F32), rb_ref[...])
        o_ref[...] = _unstack_heads(acc_ref[...])

        if ng:
            @pl.when(jnp.logical_and(step == N_GROUPS * nq - 1, c == sender))
            def _():
                for cp in sends + forwards:
                    cp.wait_send()

            @pl.when(jnp.logical_and(step == N_GROUPS * nq - 1, c != sender))
            def _():
                for cp in forwards:
                    cp.wait_recv()

    once = pl.Buffered(1)
    gathered, sems = _gather_shapes(shards) if ng else ([], [])
    return _call(
        body, name="sb_fwd_gather" if ng else "sb_fwd", grid=(N_GROUPS, nq),
        in_specs=[pl.BlockSpec((TQ, LW), lambda p, i: (i, Q_BLK0 + p)),
                  pl.BlockSpec((S, LW), lambda p, i: (0, K_BLK0 + p), pipeline_mode=once),
                  pl.BlockSpec((S, LW), lambda p, i: (0, V_BLK0 + p), pipeline_mode=once),
                  pl.BlockSpec((TK, TK), lambda p, i: (0, 0))] + [ANY] * ng,
        out_specs=[pl.BlockSpec((TQ, LW), lambda p, i: (i, p)),
                   pl.BlockSpec((None, None, HG * TQ, LANES), lambda p, i: (p, i, 0, 0))] + [ANY] * ng,
        out_shape=[jax.ShapeDtypeStruct((S, W_B), F32),
                   jax.ShapeDtypeStruct((N_GROUPS, nq, HG * TQ, LANES), F32)] + gathered,
        scratch_shapes=[pltpu.VMEM((HG * TQ, LW), F32)] + sems,
        compiler_params=_params("arbitrary", "arbitrary"),
    )(proj_b, proj_b, proj_b, tri, *shards)


def _sb_bwd(proj_b, dyb, rb, tri, trit, carry_exchange=None):
    S = proj_b.shape[0]
    nq = S // TQ
    kpq = TQ // TK
    hs = list(carry_exchange or [])
    ne = len(hs)

    def body(*refs):
        q_ref, k_ref, v_ref, do_ref, rb_ref, tri_ref, trit_ref = refs[:7]
        dq_ref, dk_acc, dv_acc = refs[7 + ne:10 + ne]
        dq_acc = refs[10 + 2 * ne]
        i = pl.program_id(1)
        if ne:
            tick = pl.program_id(0) * nq + i
            sends, arrivals = _exchange_plan(refs[7:7 + ne], refs[10 + ne:10 + 2 * ne], refs[11 + 2 * ne],
                                             refs[12 + 2 * ne])

            @pl.when(tick == 0)
            def _():
                for cp in sends:
                    cp.start()

        lane2 = lax.broadcasted_iota(jnp.int32, (HG * TQ, LANES), 1)
        scale = HEAD_DIM ** -0.5
        q2 = _stack_heads(q_ref[...].astype(F32) * scale)
        do2 = _stack_heads(do_ref[...])
        tri = tri_ref[...]
        trit = trit_ref[...]

        @pl.when(i == 0)
        def _():
            dk_acc[...] = jnp.zeros_like(dk_acc)
            dv_acc[...] = jnp.zeros_like(dv_acc)

        dq_acc[...] = jnp.zeros_like(dq_acc)

        def block(j, pre, key_offset=None):
            start = pl.multiple_of(j * TK, TK)
            kj = k_ref[pl.ds(start, TK), :]
            vj = v_ref[pl.ds(start, TK), :]
            lsp, lsm, tail, msk = _sb_scores(q2, kj, tri, key_offset)
            run = jnp.sum(jnp.where(lane2 == j, rb_ref[...], 0.0), axis=-1, keepdims=True)
            att = jnp.exp(lsp + tail + run)
            if msk is not None:
                att = jnp.where(msk, att, 0.0)
            beta = jnp.exp(lsp)
            dl = _dot_nt(do2, vj) * att
            cin = _dot(dl.astype(MXU_DT), trit)
            dz = dl * (1.0 - beta) - beta * (pre + cin)
            if msk is not None:
                dz = jnp.where(msk, dz, 0.0)
            dzb = dz.astype(MXU_DT)
            dq_acc[...] += _dot(dzb, kj)
            dk_acc[pl.ds(start, TK), :] += _dot_tn(dzb, q2)
            dv_acc[pl.ds(start, TK), :] += _dot_tn(att.astype(MXU_DT), do2)
            return pre + cin[:, TK - 1:] + dl[:, TK - 1:]

        past = i * kpq
        walked = jnp.max(jnp.where(lane2[:8] == LANES - 1, rb_ref[pl.ds(0, 8), :], 0.0)).astype(jnp.int32)
        walked = jnp.clip(walked, 0, past)
        def overlapping(pre):
            for d in range(kpq):
                pre = block(i * kpq + d, pre, key_offset=d * TK)
            return jnp.int32(0)

        def with_past():
            pre = lax.fori_loop(past - walked, past - 1, lambda j, pre: block(j, pre),
                                jnp.zeros((HG * TQ, 1), F32))
            return overlapping(block(past - 1, pre))

        lax.cond(walked > 0, with_past, lambda: overlapping(jnp.zeros((HG * TQ, 1), F32)))
        dq_ref[...] = (_unstack_heads(dq_acc[...]) * scale).astype(dq_ref.dtype)

        if ne:
            @pl.when(tick == N_GROUPS * nq - 1)
            def _():
                for cp in arrivals:
                    cp.wait_recv()
                for cp in sends:
                    cp.wait_send()

    once = pl.Buffered(1)
    exchanged, sems = _exchange_shapes(hs) if ne else ([], [])
    return _call(
        body, name="sb_bwd_exchange" if ne else "sb_bwd", grid=(N_GROUPS, nq),
        in_specs=[pl.BlockSpec((TQ, LW), lambda p, i: (i, Q_BLK0 + p)),
                  pl.BlockSpec((S, LW), lambda p, i: (0, K_BLK0 + p), pipeline_mode=once),
                  pl.BlockSpec((S, LW), lambda p, i: (0, V_BLK0 + p), pipeline_mode=once),
                  pl.BlockSpec((TQ, LW), lambda p, i: (i, p)),
                  pl.BlockSpec((None, None, HG * TQ, LANES), lambda p, i: (p, i, 0, 0)),
                  pl.BlockSpec((TK, TK), lambda p, i: (0, 0)),
                  pl.BlockSpec((TK, TK), lambda p, i: (0, 0))] + [ANY] * ne,
        out_specs=[pl.BlockSpec((TQ, LW), lambda p, i: (i, p)),
                   pl.BlockSpec((S, LW), lambda p, i: (0, p), pipeline_mode=once),
                   pl.BlockSpec((S, LW), lambda p, i: (0, p), pipeline_mode=once)] + [ANY] * ne,
        out_shape=[jax.ShapeDtypeStruct((S, W_B), MXU_DT), jax.ShapeDtypeStruct((S, W_B), F32),
                   jax.ShapeDtypeStruct((S, W_B), F32)] + exchanged,
        scratch_shapes=[pltpu.VMEM((HG * TQ, LW), F32)] + sems,
        compiler_params=_params("arbitrary", "arbitrary"),
    )(proj_b, proj_b, proj_b, dyb, rb, tri, trit, *hs)


P_BLK = (2 * W_A + 3 * W_B) // W_C


def _window_lanes():
    g = lax.broadcasted_iota(jnp.int32, (1, W_C), 1) // (W_C // 4)
    w = jnp.where(g == 0, POOL_WINDOWS[0], jnp.where(g == 1, POOL_WINDOWS[1],
                  jnp.where(g == 2, POOL_WINDOWS[2], POOL_WINDOWS[3])))
    return g, w


def _shift_rows(ext, k, tm, lead):
    n = ext.shape[0]
    return pltpu.roll(ext, shift=k % n, axis=0)[lead:lead + tm]


def _pool_diff(p_cur, p_halo, row0, tm):
    ext = jnp.concatenate([p_halo, p_cur], axis=0)
    g, w = _window_lanes()
    acc = ext
    sums = []
    for sh in (1, 2, 4, 8):
        acc = acc + pltpu.roll(acc, shift=sh, axis=0)
        sums.append(acc[HALO:HALO + tm])
    wsum = jnp.where(g == 0, sums[0], jnp.where(g == 1, sums[1], jnp.where(g == 2, sums[2], sums[3])))
    pos = (row0 + 1 + lax.broadcasted_iota(jnp.int32, (tm, W_C), 0)).astype(F32)
    cnt = jnp.minimum(pos, w.astype(F32))
    return wsum / cnt - p_cur, cnt


def _pool_specs(tm, nrow_blocks_halo):
    cur = pl.BlockSpec((tm, W_C), lambda i: (i, P_BLK))
    prev = pl.BlockSpec((HALO, W_C), lambda i: (jnp.maximum(i * (tm // HALO) - 1, 0), P_BLK))
    return cur, prev


def _in_proj_groups(x, g, w, sng, wm, bias, gmat, wbd, scale):
    S, D = x.shape
    tm = TM_MM
    p0 = 2 * W_A + 3 * W_B
    qkv_chunk = 3 * W_B // 2

    def body(x_ref, g_ref, w_ref, sng_ref, wm_ref, b_ref, gmat_ref, wbd_ref, sc_ref,
             h_ref, o_ref, ob_ref, ya_ref, yc_ref, tail_ref):
        i = pl.program_id(0)
        xv = x_ref[...]
        r = lax.rsqrt(jnp.mean(xv * xv, axis=-1, keepdims=True) + EPS)
        h = (xv * r * g_ref[...]).astype(h_ref.dtype)
        h_ref[...] = h

        def project(c0, c1):
            acc = _dot(h, w_ref[:, c0:c1])
            o_ref[:, c0:c1] = acc
            ob_ref[:, c0:c1] = acc.astype(ob_ref.dtype)
            return acc

        a = project(0, 2 * W_A)
        _, u, _, _, _, s = _sgu_common(a, sng_ref[...], wm_ref, b_ref[...], gmat_ref[...])
        ya_ref[...] = u * s
        for c0 in range(2 * W_A, p0, qkv_chunk):
            project(c0, c0 + qkv_chunk)
        p = project(p0, p0 + W_C)
        halo = jnp.where(i > 0, tail_ref[...], 0.0)
        tail_ref[...] = p[tm - HALO:]
        d, _ = _pool_diff(p, halo, i * tm, tm)
        yc_ref[...] = _dot(d.astype(MXU_DT), wbd_ref[...]) * sc_ref[...]

    return _call(
        body, name="in_proj_groups", grid=(S // tm,),
        in_specs=[_row_spec(tm, D), _full_spec((1, D)),
                  pl.BlockSpec((D, IN_COLS), lambda i: (0, 0), pipeline_mode=pl.Buffered(1)),
                  _full_spec((1, W_A)), _full_spec((4, CHUNK, CHUNK)), _full_spec((CHUNK, W_A)),
                  _full_spec((LANES, LANES)), _full_spec((W_C, W_C)), _full_spec((1, W_C))],
        out_specs=[_row_spec(tm, D), _row_spec(tm, IN_COLS), _row_spec(tm, IN_COLS), _row_spec(tm, W_A),
                   _row_spec(tm, W_C)],
        out_shape=[jax.ShapeDtypeStruct((S, D), MXU_DT), jax.ShapeDtypeStruct((S, IN_COLS), F32),
                   jax.ShapeDtypeStruct((S, IN_COLS), MXU_DT), jax.ShapeDtypeStruct((S, W_A), F32),
                   jax.ShapeDtypeStruct((S, W_C), F32)],
        scratch_shapes=[pltpu.VMEM((HALO, W_C), F32)],
        compiler_params=_params("arbitrary"),
    )(x, g, w, sng, wm, bias, gmat, wbd, scale)


def _pool_bwd_a(proj, dy, wbd, scale):
    S = proj.shape[0]
    tm = TM

    def body(p_ref, ph_ref, dy_ref, w_ref, sc_ref, dd_ref, e_ref, dw_ref, dsc_ref):
        i = pl.program_id(0)
        halo = jnp.where(i > 0, ph_ref[...], 0.0)
        d, cnt = _pool_diff(p_ref[...], halo, i * tm, tm)
        db = d.astype(MXU_DT)
        dy = dy_ref[...]

        @pl.when(i == 0)
        def _():
            dw_ref[...] = jnp.zeros_like(dw_ref)
            dsc_ref[...] = jnp.zeros_like(dsc_ref)

        dsc_ref[...] += jnp.sum(dy * _dot(db, w_ref[...]), axis=0, keepdims=True)
        dys = (dy * sc_ref[...]).astype(MXU_DT)
        dw_ref[...] += _dot_tn(db, dys)
        dd = _dot_nt(dys, w_ref[...])
        dd_ref[...] = dd
        e_ref[...] = dd / cnt

    cur, prev = _pool_specs(tm, S // HALO)
    return _call(
        body, name="pool_bwd_a", grid=(S // tm,),
        in_specs=[cur, prev, _row_spec(tm, W_C), _full_spec((W_C, W_C)), _full_spec((1, W_C))],
        out_specs=[_row_spec(tm, W_C), _row_spec(tm, W_C), _full_spec((W_C, W_C)), _full_spec((1, W_C))],
        out_shape=[jax.ShapeDtypeStruct((S, W_C), F32), jax.ShapeDtypeStruct((S, W_C), F32),
                   jax.ShapeDtypeStruct((W_C, W_C), F32), jax.ShapeDtypeStruct((1, W_C), F32)],
        compiler_params=_params("arbitrary"),
    )(proj, proj, dy, wbd, scale)


def _pool_bwd_b(dd, e):
    S = dd.shape[0]
    tm = TM
    nb = S // tm

    def body(dd_ref, e_ref, en_ref, dp_ref):
        i = pl.program_id(0)
        halo = jnp.where(i < nb - 1, en_ref[...], 0.0)
        ext = jnp.concatenate([e_ref[...], halo], axis=0)
        n = ext.shape[0]
        g, _ = _window_lanes()
        acc = ext
        sums = []
        for sh in (1, 2, 4, 8):
            acc = acc + pltpu.roll(acc, shift=n - sh, axis=0)
            sums.append(acc[:tm])
        wsum = jnp.where(g == 0, sums[0], jnp.where(g == 1, sums[1], jnp.where(g == 2, sums[2], sums[3])))
        dp_ref[...] = (wsum - dd_ref[...]).astype(dp_ref.dtype)

    nxt = pl.BlockSpec((HALO, W_C), lambda i: (jnp.minimum((i + 1) * (tm // HALO), S // HALO - 1), 0))
    return _call(
        body, name="pool_bwd_b", grid=(nb,),
        in_specs=[_row_spec(tm, W_C), _row_spec(tm, W_C), nxt],
        out_specs=_row_spec(tm, W_C),
        out_shape=jax.ShapeDtypeStruct((S, W_C), MXU_DT),
        compiler_params=_params("parallel"),
    )(dd, e, e)


TN_FF = 1408
NB_FF = D_FF // TN_FF
CONV_ROWS = 8


def _conv(z_cur, z_halo, cwb, tm):
    ext = jnp.concatenate([z_halo, z_cur], axis=0)
    z2 = _shift_rows(ext, 2, tm, HALO)
    z1 = _shift_rows(ext, 1, tm, HALO)
    zc = cwb[3:4] + z2 * cwb[0:1] + z1 * cwb[1:2] + z_cur * cwb[2:3]
    return zc, z2, z1


def _up_proj_gate(x, g, w, cwb):
    S, D = x.shape
    tm = TM

    def body(x_ref, g_ref, w_ref, c_ref, h_ref, z_ref, zc_ref, f_ref, tail_ref):
        first = pl.program_id(0) == 0
        xv = x_ref[...]
        r = lax.rsqrt(jnp.mean(xv * xv, axis=-1, keepdims=True) + EPS)
        h = (xv * r * g_ref[...]).astype(h_ref.dtype)
        h_ref[...] = h
        for j in range(NB_FF):
            halves = []
            for col0 in (j * TN_FF, D_FF + j * TN_FF):
                zb = _dot(h, w_ref[:, col0:col0 + TN_FF]).astype(z_ref.dtype)
                z_ref[:, col0:col0 + TN_FF] = zb
                zf = zb.astype(F32)
                prev = jnp.where(first, 0.0, tail_ref[:, col0:col0 + TN_FF])
                tail_ref[:, col0:col0 + TN_FF] = zf[tm - HALO:]
                zc = _conv(zf, prev, c_ref[:, col0:col0 + TN_FF], tm)[0]
                zc_ref[:, col0:col0 + TN_FF] = zc.astype(zc_ref.dtype)
                halves.append(zc)
            gate, value = halves
            f_ref[:, j * TN_FF:(j + 1) * TN_FF] = (gate * jax.nn.sigmoid(gate) * value).astype(f_ref.dtype)

    return _call(
        body, name="up_proj_gate", grid=(S // tm,),
        in_specs=[_row_spec(tm, D), _full_spec((1, D)),
                  pl.BlockSpec((D, 2 * D_FF), lambda i: (0, 0), pipeline_mode=pl.Buffered(1)),
                  _full_spec((CONV_ROWS, 2 * D_FF))],
        out_specs=[_row_spec(tm, D), _row_spec(tm, 2 * D_FF), _row_spec(tm, 2 * D_FF), _row_spec(tm, D_FF)],
        out_shape=[jax.ShapeDtypeStruct((S, D), MXU_DT), jax.ShapeDtypeStruct((S, 2 * D_FF), MXU_DT),
                   jax.ShapeDtypeStruct((S, 2 * D_FF), MXU_DT), jax.ShapeDtypeStruct((S, D_FF), MXU_DT)],
        scratch_shapes=[pltpu.VMEM((HALO, 2 * D_FF), F32)],
        compiler_params=_params("arbitrary"),
    )(x, g, w, cwb)


def _gate_up_bwd(z, zc, cwb, w, wd, x, g, dres):
    S, D = x.shape
    tm = TM
    nb = S // tm

    def body(z_ref, zc_ref, zcn_ref, c_ref, w_ref, wd_ref, x_ref, g_ref, r_ref, rn_ref,
             dz_ref, dc_ref, dx_ref, dg_ref):
        i = pl.program_id(0)
        first = i == 0
        last = i == nb - 1
        dxe = jnp.concatenate([r_ref[...], jnp.where(last, 0.0, rn_ref[...])], axis=0).astype(MXU_DT)

        @pl.when(first)
        def _():
            dc_ref[...] = jnp.zeros_like(dc_ref)
            dg_ref[...] = jnp.zeros_like(dg_ref)

        rid = lax.broadcasted_iota(jnp.int32, (CONV_ROWS, TN_FF), 0)

        def conv_out(cols):
            return jnp.concatenate([zc_ref[:, cols].astype(F32), zcn_ref[:, cols].astype(F32)], axis=0)

        def conv_bwd(d, z0, c):
            d0 = d[:tm]
            d1 = _shift_rows(d, -1, tm, 0)
            d2 = _shift_rows(d, -2, tm, 0)
            sums = [jnp.sum(d2 * z0, axis=0, keepdims=True), jnp.sum(d1 * z0, axis=0, keepdims=True),
                    jnp.sum(d0 * z0, axis=0, keepdims=True), jnp.sum(d0, axis=0, keepdims=True)]
            dtaps = jnp.zeros((CONV_ROWS, TN_FF), F32)
            for k, v in enumerate(sums):
                dtaps = jnp.where(rid == k, v, dtaps)
            return d0 * c[2:3] + d1 * c[1:2] + d2 * c[0:1], dtaps

        dh = jnp.zeros((tm, D), F32)
        for j in range(NB_FF):
            gc = slice(j * TN_FF, (j + 1) * TN_FF)
            uc = slice(D_FF + j * TN_FF, D_FF + (j + 1) * TN_FF)
            gt = conv_out(gc)
            ut = conv_out(uc)
            df = _dot_nt(dxe, wd_ref[gc, :])
            sg = jax.nn.sigmoid(gt)
            dzg, dtg = conv_bwd(df * ut * (sg * (1.0 + gt * (1.0 - sg))), z_ref[:, gc].astype(F32), c_ref[:, gc])
            dzu, dtu = conv_bwd(df * (gt * sg), z_ref[:, uc].astype(F32), c_ref[:, uc])
            dzg = dzg.astype(dz_ref.dtype)
            dzu = dzu.astype(dz_ref.dtype)
            dz_ref[:, gc] = dzg
            dz_ref[:, uc] = dzu
            dc_ref[:, gc] += dtg
            dc_ref[:, uc] += dtu
            dh += _dot_nt(dzg, w_ref[:, gc]) + _dot_nt(dzu, w_ref[:, uc])

        xv = x_ref[...]
        r = lax.rsqrt(jnp.mean(xv * xv, axis=-1, keepdims=True) + EPS)
        xhat = xv * r
        dg_ref[...] += jnp.sum(dh * xhat, axis=0, keepdims=True)
        dxh = dh * g_ref[...]
        dx_ref[...] = r_ref[...] + r * (dxh - xhat * jnp.mean(dxh * xhat, axis=-1, keepdims=True))

    hb = tm // HALO
    last_halo = S // HALO - 1
    return _call(
        body, name="gate_up_bwd", grid=(nb,),
        in_specs=[_row_spec(tm, 2 * D_FF), _row_spec(tm, 2 * D_FF),
                  pl.BlockSpec((HALO, 2 * D_FF), lambda i: (jnp.minimum((i + 1) * hb, last_halo), 0)),
                  _full_spec((CONV_ROWS, 2 * D_FF)),
                  pl.BlockSpec((D, 2 * D_FF), lambda i: (0, 0), pipeline_mode=pl.Buffered(1)),
                  pl.BlockSpec((D_FF, D), lambda i: (0, 0), pipeline_mode=pl.Buffered(1)),
                  _row_spec(tm, D), _full_spec((1, D)), _row_spec(tm, D),
                  pl.BlockSpec((HALO, D), lambda i: (jnp.minimum((i + 1) * hb, last_halo), 0))],
        out_specs=[_row_spec(tm, 2 * D_FF), _full_spec((CONV_ROWS, 2 * D_FF)), _row_spec(tm, D), _full_spec((1, D))],
        out_shape=[jax.ShapeDtypeStruct((S, 2 * D_FF), MXU_DT), jax.ShapeDtypeStruct((CONV_ROWS, 2 * D_FF), F32),
                   jax.ShapeDtypeStruct((S, D), F32), jax.ShapeDtypeStruct((1, D), F32)],
        compiler_params=_params("arbitrary"),
    )(z, zc, zc, cwb, w, wd, x, g, dres, dres)


def _layer_consts(w, l):
    wm = _tril_weights(w["sgu_w"][l])
    eye = jnp.eye(4, dtype=F32)
    wbd = (w["pool_w"][l][:, :, None, :] * eye[:, None, :, None]).reshape(W_C, W_C)
    cwb = jnp.concatenate([w["conv_w"][l], w["conv_b"][l][None], jnp.zeros((CONV_ROWS - 4, 2 * D_FF), F32)], axis=0)
    return dict(
        g1=w["norm1_g"][l][None], g2=w["norm2_g"][l][None], gm=w["mix_norm_g"][l][None],
        sng=w["sgu_norm_g"][l][None], wm=wm.astype(MXU_DT), wmt=jnp.swapaxes(wm, 1, 2).astype(MXU_DT),
        bias=jnp.repeat(jnp.transpose(w["sgu_b"][l]), HEAD_DIM, axis=1),
        wbd=wbd.astype(MXU_DT), scale=w["pool_scale"][l][None], cwb=cwb,
        w_in=w["w_in"][l], w_o=w["w_o"][l], w_up=w["w_up"][l], w_down=w["w_down"][l],
    )


def _local_step(x, tgt, w, late=None, early_exchange=None):
    gmat = _group_matrix()
    tri = _tri_matrix()
    trit = jnp.transpose(tri)
    saved = []
    early = None
    for l in range(DEPTH):
        c = _layer_consts(w, l)
        h1, proj, proj_b, ya, yc = _in_proj_groups(x, c["g1"], c["w_in"], c["sng"], c["wm"], c["bias"], gmat,
                                                   c["wbd"], c["scale"])
        if l == 0 and late is not None:
            assert DEPTH == 2
            yb, rb, *gathered = _sb_fwd(proj_b, tri, carry_gather=late[0])
            w = dict(w)
            for name, arr in late[1](gathered).items():
                w[name] = [w[name][0], arr]
        else:
            yb, rb = _sb_fwd(proj_b, tri)
        x2, yn = _mix_out(ya, yb, yc, c["gm"], c["w_o"], x, gmat)
        h2, z, zc, f = _up_proj_gate(x2, c["g2"], c["w_up"], c["cwb"])
        saved.append(dict(c=c, x=x, proj=proj, proj_b=proj_b, h1=h1, ya=ya, yb=yb, yc=yc, rb=rb, x2=x2, yn=yn,
                          z=z, zc=zc, h2=h2, f=f))
        if l < DEPTH - 1:
            x = _mm_res(f, c["w_down"], x2, "down_proj")

    last = saved[-1]
    dx, d_final_g, loss8 = _down_proj_loss(last["f"], last["c"]["w_down"], last["x2"], w["final_g"][None], tgt)
    grads = {n: [None] * DEPTH for n in ("norm1_g", "w_in", "sgu_norm_g", "sgu_w", "sgu_b", "pool_w", "pool_scale",
                                         "mix_norm_g", "w_o", "norm2_g", "w_up", "conv_w", "conv_b", "w_down")}
    for l in reversed(range(DEPTH)):
        s = saved[l]
        c = s["c"]
        grads["w_down"][l] = _mm_tn(s["f"], dx, "down_proj_wgrad").reshape(N_CHIPS, D_FF // N_CHIPS, D_MODEL)
        dz, dcwb, dx2, dg2 = _gate_up_bwd(s["z"], s["zc"], c["cwb"], c["w_up"], c["w_down"], s["x2"], c["g2"], dx)
        grads["conv_w"][l] = dcwb[:3]
        grads["conv_b"][l] = dcwb[3]
        grads["w_up"][l] = _mm_tn(s["h2"], dz, "up_proj_wgrad", col_tiles=True)
        grads["norm2_g"][l] = dg2[0]
        grads["w_o"][l] = _mm_tn(s["yn"], dx2, "out_proj_wgrad").reshape(N_CHIPS, D_MODEL // N_CHIPS, D_MODEL)
        if l == 0 and early_exchange is not None:
            early_items = early_exchange[0](grads)
            dya, dyb, dyc, dgm, *swapped = _mix_out_bwd(dx2, c["w_o"], s["ya"], s["yb"], s["yc"], c["gm"], gmat,
                                                        carry_swap=early_items)
        else:
            dya, dyb, dyc, dgm = _mix_out_bwd(dx2, c["w_o"], s["ya"], s["yb"], s["yc"], c["gm"], gmat)
        grads["mix_norm_g"][l] = dgm[0]
        dd, e, dwbd, dscale = _pool_bwd_a(s["proj"], dyc, c["wbd"], c["scale"])
        dp = _pool_bwd_b(dd, e)
        grads["pool_w"][l] = jnp.stack([dwbd[g * 64:(g + 1) * 64, g * 64:(g + 1) * 64] for g in range(4)])
        grads["pool_scale"][l] = dscale[0]
        if l == 0 and early_exchange is not None:
            sent = early_exchange[1](early_items, swapped)
            dq, dk, dv, *parts = _sb_bwd(s["proj_b"], dyb, s["rb"], tri, trit, carry_exchange=sent)
            early = (sent, parts)
        else:
            dq, dk, dv = _sb_bwd(s["proj_b"], dyb, s["rb"], tri, trit)
        da, dwm, dbias, dsng = _sgu_bwd(s["proj"], dya, c["sng"], c["wm"], c["wmt"], c["bias"], gmat)
        grads["sgu_w"][l] = dwm
        grads["sgu_b"][l] = jnp.transpose(jnp.sum(dbias.reshape(CHUNK, 4, HEAD_DIM), axis=-1))
        grads["sgu_norm_g"][l] = dsng[0]
        dproj = jnp.concatenate([da, dq, dk.astype(MXU_DT), dv.astype(MXU_DT), dp], axis=1)
        dw_in = _mm_tn(s["h1"], dproj, "in_proj_wgrad")
        grads["w_in"][l] = jnp.transpose(dw_in.reshape(D_MODEL, N_CHIPS, IN_COLS // N_CHIPS), (1, 0, 2))
        dx, dg1 = _mm_nt_rmsbwd([(dproj, c["w_in"])], s["x"], c["g1"], dx2, "in_proj_bwd")
        grads["norm1_g"][l] = dg1[0]

    out = {n: (v if n in BIG_NAMES[:4] else jnp.stack(v)) for n, v in grads.items()}
    out["final_g"] = d_final_g[0]
    return loss8[0, 0], dx, out, early


MESH = pl.DeviceIdType.MESH
ANY = pl.BlockSpec(memory_space=pl.ANY)


def _gather_plan(ins, outs, send_sems, recv_sems, layer, sender):
    n = len(ins)
    x, y, c = lax.axis_index("x"), lax.axis_index("y"), lax.axis_index("c")
    sibling = (x, y, 1 - c)
    my_chip = 2 * x + y
    chips = [(1 - x, y), (x, 1 - y), (1 - x, 1 - y)]
    ids = [2 * px + py for px, py in chips]

    def copy(a, k, chip, to, own=False):
        dst = outs[a].at[chip]
        return pltpu.make_async_remote_copy(
            src_ref=ins[a].at[layer] if own else dst, dst_ref=dst,
            send_sem=send_sems.at[a, k], recv_sem=recv_sems.at[a, k], device_id=to, device_id_type=MESH)

    sends = [copy(a, j, my_chip, (*chips[j], sender), own=True) for j in range(3) for a in range(n)]
    arrivals = [copy(a, j, ids[j], sibling) for j in range(3) for a in range(n)]
    forwards = [copy(a, 3 + j, ids[j], sibling) for j in range(3) for a in range(n)]
    return c, sends, arrivals, forwards


def _gather_shapes(shards):
    n = len(shards)
    return ([jax.ShapeDtypeStruct((N_CHIPS,) + s.shape[1:], s.dtype) for s in shards],
            [pltpu.SemaphoreType.DMA((n, 6)), pltpu.SemaphoreType.DMA((n, 6))])


def _all_gather(shards, layer, sender):
    n = len(shards)

    def body(*refs):
        c, sends, arrivals, forwards = _gather_plan(refs[:n], refs[n:2 * n], refs[2 * n], refs[2 * n + 1],
                                                    layer, sender)

        @pl.when(c == sender)
        def _():
            for cp in sends:
                cp.start()
            for arrived, onward in zip(arrivals, forwards):
                arrived.wait_recv()
                onward.start()
            for cp in sends + forwards:
                cp.wait_send()

        @pl.when(c != sender)
        def _():
            for cp in forwards:
                cp.wait_recv()

    out_shape, sems = _gather_shapes(shards)
    return _call(body, name="weight_all_gather", out_shape=out_shape, in_specs=[ANY] * n, out_specs=[ANY] * n,
                 scratch_shapes=sems)(*shards)


def _row_tile(r):
    return r if r <= 704 else 256


def _grad_swap(items, name):
    n = len(items)

    def body(*refs):
        start, finish = _swap_plan(refs[:n], refs[n:2 * n], refs[2 * n:3 * n], refs[3 * n], refs[3 * n + 1])
        start()
        finish()

    out_shape, sems = _swap_shapes(items)
    return _call(body, name=name, out_shape=out_shape, in_specs=[ANY] * (2 * n), out_specs=[ANY] * n,
                 scratch_shapes=sems)(*[a0 for a0, _ in items], *[a1 for _, a1 in items])


def _swap_plan(firsts, seconds, outs, send_sems, recv_sems):
    n = len(firsts)
    x, y, c = lax.axis_index("x"), lax.axis_index("y"), lax.axis_index("c")

    def copies(srcs):
        return [pltpu.make_async_remote_copy(src_ref=srcs[a], dst_ref=outs[a], send_sem=send_sems.at[a],
                                             recv_sem=recv_sems.at[a], device_id=(x, y, 1 - c),
                                             device_id_type=MESH) for a in range(n)]

    def start():
        @pl.when(c == 0)
        def _():
            for cp in copies(seconds):
                cp.start()

        @pl.when(c == 1)
        def _():
            for cp in copies(firsts):
                cp.start()

    def finish():
        for cp in copies(firsts):
            cp.wait()

    return start, finish


def _swap_shapes(items):
    n = len(items)
    return ([jax.ShapeDtypeStruct(a0.shape, a0.dtype) for a0, _ in items],
            [pltpu.SemaphoreType.DMA((n,)), pltpu.SemaphoreType.DMA((n,))])


def _pair_add(a0, a1, r, c_arr, name, out_dtype):
    k, rr, cc = r.shape
    tr = _row_tile(rr)

    def body(c_ref, a0_ref, a1_ref, r_ref, o_ref):
        mine = jnp.where(c_ref[0] == 0, a0_ref[...], a1_ref[...])
        o_ref[...] = (mine + r_ref[...]).astype(o_ref.dtype)

    def member(which):
        def index(kk, i, c_ref):
            used = (c_ref[0] == which).astype(jnp.int32)
            return (kk * used, i * used, 0)
        return pl.BlockSpec((1, tr, cc), index)

    spec = pl.BlockSpec((1, tr, cc), lambda kk, i, c_ref: (kk, i, 0))
    grid_spec = pltpu.PrefetchScalarGridSpec(num_scalar_prefetch=1, grid=(k, rr // tr),
                                             in_specs=[member(0), member(1), spec], out_specs=spec)
    return _call(body, name=name, grid_spec=grid_spec, out_shape=jax.ShapeDtypeStruct((k, rr, cc), out_dtype),
                 compiler_params=_params("parallel", "parallel"))(c_arr, a0, a1, r)


def _exchange_plan(ins, outs, send_sems, recv_sems):
    n = len(ins)
    x, y, c = lax.axis_index("x"), lax.axis_index("y"), lax.axis_index("c")
    my_chip = 2 * x + y
    chips = [(1 - x, y), (x, 1 - y), (1 - x, 1 - y)]

    def copy(a, k, src_chip, dst_chip):
        px, py = chips[k]
        return pltpu.make_async_remote_copy(
            src_ref=ins[a].at[src_chip], dst_ref=outs[a].at[dst_chip], send_sem=send_sems.at[a, k],
            recv_sem=recv_sems.at[a, k], device_id=(px, py, c), device_id_type=MESH)

    sends = [copy(a, k, 2 * chips[k][0] + chips[k][1], my_chip) for k in range(3) for a in range(n)]
    arrivals = [copy(a, k, my_chip, 2 * chips[k][0] + chips[k][1]) for k in range(3) for a in range(n)]
    return sends, arrivals


def _exchange_shapes(hs):
    n = len(hs)
    return ([jax.ShapeDtypeStruct(h.shape, h.dtype) for h in hs],
            [pltpu.SemaphoreType.DMA((n, 3)), pltpu.SemaphoreType.DMA((n, 3))])


def _grad_exchange(hs):
    n = len(hs)

    def body(*refs):
        sends, arrivals = _exchange_plan(refs[:n], refs[n:2 * n], refs[2 * n], refs[2 * n + 1])
        for cp in sends:
            cp.start()
        for cp in arrivals:
            cp.wait_recv()
        for cp in sends:
            cp.wait_send()

    out_shape, sems = _exchange_shapes(hs)
    return _call(body, name="grad_exchange_chips", out_shape=out_shape, in_specs=[ANY] * n, out_specs=[ANY] * n,
                 scratch_shapes=sems)(*hs)


def _sum_chips(a, c_arr, name):
    _, r, cc = a.shape
    tr = _row_tile(r)

    def body(c_ref, a_ref, o_ref):
        o_ref[...] = ((a_ref[0].astype(F32) + a_ref[1].astype(F32)) + a_ref[2].astype(F32)) + a_ref[3].astype(F32)

    grid_spec = pltpu.PrefetchScalarGridSpec(
        num_scalar_prefetch=1, grid=(r // tr,),
        in_specs=[pl.BlockSpec((N_CHIPS, tr, cc), lambda i, c_ref: (0, i, 0))],
        out_specs=pl.BlockSpec((None, tr, cc), lambda i, c_ref: (c_ref[0], i, 0)))
    return _call(body, name=name, grid_spec=grid_spec, out_shape=jax.ShapeDtypeStruct((2, r, cc), F32),
                 compiler_params=_params("parallel"))(c_arr, a)


def _grad_share(bufs):
    n = len(bufs)

    def body(*refs):
        outs = refs[n:2 * n]
        send_sems, recv_sems = refs[2 * n:]
        x, y, c = lax.axis_index("x"), lax.axis_index("y"), lax.axis_index("c")
        copies = [pltpu.make_async_remote_copy(src_ref=outs[a].at[c], dst_ref=outs[a].at[c], send_sem=send_sems.at[a],
                                               recv_sem=recv_sems.at[a], device_id=(x, y, 1 - c),
                                               device_id_type=MESH) for a in range(n)]
        for cp in copies:
            cp.start()
        for a in range(n):
            pltpu.make_async_remote_copy(src_ref=outs[a].at[c], dst_ref=outs[a].at[1 - c], send_sem=send_sems.at[a],
                                         recv_sem=recv_sems.at[a], device_id=(x, y, 1 - c),
                                         device_id_type=MESH).wait_recv()
        for cp in copies:
            cp.wait_send()

    return _call(
        body, name="grad_share_cores", out_shape=[jax.ShapeDtypeStruct(b.shape, b.dtype) for b in bufs],
        in_specs=[ANY] * n, out_specs=[ANY] * n, input_output_aliases={a: a for a in range(n)},
        scratch_shapes=[pltpu.SemaphoreType.DMA((n,)), pltpu.SemaphoreType.DMA((n,))],
    )(*bufs)


def _adamw_math(g_ref, w_ref, m_ref, v_ref, d_ref, nm_ref, nv_ref):
    gv = g_ref[...]
    nm = ADAM_B1 * m_ref[...] + (1.0 - ADAM_B1) * gv
    nv = ADAM_B2 * v_ref[...] + (1.0 - ADAM_B2) * (gv * gv)
    m_hat = nm / (1.0 - ADAM_B1 ** ADAM_STEP)
    v_hat = nv / (1.0 - ADAM_B2 ** ADAM_STEP)
    d_ref[...] = -ADAM_LR * (m_hat / (jnp.sqrt(v_hat) + ADAM_EPS) + ADAM_WD * w_ref[...])
    nm_ref[...] = nm
    nv_ref[...] = nv


def _adamw_big(g, w, m, v, name):
    d, r, c = g.shape
    tr = r if r <= 704 else 256
    spec = pl.BlockSpec((1, tr, c), lambda l, i: (l, i, 0))

    def body(*refs):
        _adamw_math(*refs)

    shp = jax.ShapeDtypeStruct(g.shape, F32)
    return _call(body, name=name, grid=(d, r // tr), in_specs=[spec] * 4, out_specs=[spec] * 3,
                 out_shape=[shp, shp, shp], compiler_params=_params("parallel", "parallel"))(g, w, m, v)


def _adamw_small(gs, ws, ms, vs):
    n = len(gs)

    def body(*refs):
        ins, outs = refs[:4 * n], refs[4 * n:]
        for k in range(n):
            _adamw_math(ins[k], ins[n + k], ins[2 * n + k], ins[3 * n + k], outs[k], outs[n + k], outs[2 * n + k])

    shp = [jax.ShapeDtypeStruct(g.shape, F32) for g in gs]
    res = _call(body, name="adamw_small", out_shape=shp * 3)(*gs, *ws, *ms, *vs)
    return res[:n], res[n:2 * n], res[2 * n:]


def _rows(a, rows):
    flat = a.reshape(-1)
    return jnp.pad(flat, (0, rows * D_MODEL - flat.shape[0])).reshape(rows, D_MODEL)


def _small_rows(p, extra=None):
    parts = [p[n].reshape(-1) for n in SMALL_NAMES]
    if extra is not None:
        parts.append(extra.reshape(-1))
    flat = jnp.concatenate(parts)
    return jnp.pad(flat, (0, ROWS_SMALL * D_MODEL - flat.shape[0])).reshape(ROWS_SMALL, D_MODEL)


CONV_SHARD = (DEPTH, 3, 2 * D_FF // N_CHIPS)
N_CONV_SHARD = DEPTH * 3 * (2 * D_FF // N_CHIPS)


def _small_pack(g, loss):
    conv = jnp.transpose(g["conv_w"].reshape(DEPTH, 3, N_CHIPS, 2 * D_FF // N_CHIPS), (2, 0, 1, 3))
    conv = jnp.stack([_rows(conv[k], ROWS_CONV) for k in range(N_CHIPS)])
    small = jnp.broadcast_to(_small_rows(g, loss), (N_CHIPS, ROWS_SMALL, D_MODEL))
    return jnp.concatenate([conv, small], axis=1)


def _unpack_small(pack):
    out = {"conv_w": pack[:ROWS_CONV].reshape(-1)[:N_CONV_SHARD].reshape(CONV_SHARD)}
    flat = pack[ROWS_CONV:].reshape(-1)
    k = 0
    for name in SMALL_NAMES:
        shape = SMALL_SHAPES[name]
        n = 1
        for d in shape:
            n *= d
        out[name] = flat[k:k + n].reshape(shape)
        k += n
    out["extra"] = flat[k]
    return out


def _assemble_layer(gathered, shards, layer):
    my_chip = 2 * lax.axis_index("x") + lax.axis_index("y")
    w_in, w_o, w_up, w_down = [lax.dynamic_update_index_in_dim(got, own[layer], my_chip, 0)
                               for got, own in zip(gathered, shards)]

    def by_cols(a):
        k, r, wd = a.shape
        return jnp.transpose(a, (1, 0, 2)).reshape(r, k * wd)

    return dict(w_in=by_cols(w_in), w_o=w_o.reshape(-1, D_MODEL), w_up=by_cols(w_up),
                w_down=w_down.reshape(-1, D_MODEL))


def _gather_weights(p):
    shards = [p[n].astype(jnp.bfloat16) for n in BIG_NAMES[:4]]
    conv_all = p["conv_w"].reshape(1, -1, p["conv_w"].shape[-1])
    got = _all_gather(shards + [conv_all], 0, 0)
    my_chip = 2 * lax.axis_index("x") + lax.axis_index("y")
    conv = lax.dynamic_update_index_in_dim(got[4], conv_all[0], my_chip, 0)
    conv = jnp.transpose(conv.reshape((N_CHIPS,) + CONV_SHARD), (1, 2, 0, 3)).reshape(DEPTH, 3, 2 * D_FF)
    full = {n: [a, None] for n, a in _assemble_layer(got[:4], shards, 0).items()}
    full["conv_w"] = conv
    return full, (shards, lambda gathered: _assemble_layer(gathered, shards, 1))


def _halves(a):
    r = a.shape[1] // 2
    return a[:, :r], a[:, r:]


def _reduce_begin(items, names, dtypes, c_arr, tag):
    return _pair_adds(items, _grad_swap(items, "grad_swap_cores_" + tag), names, dtypes, c_arr)


def _pair_adds(items, got, names, dtypes, c_arr):
    return [_pair_add(a0, a1, r, c_arr, "grad_add_cores_" + nm, dt)
            for (a0, a1), r, nm, dt in zip(items, got, names, dtypes)]


def _reduce_end(parts, sent, names, c_arr):
    my_chip = 2 * lax.axis_index("x") + lax.axis_index("y")
    full = [lax.dynamic_update_index_in_dim(p, lax.dynamic_index_in_dim(own, my_chip, 0, keepdims=False), my_chip, 0)
            for p, own in zip(parts, sent)]
    return [_sum_chips(f, c_arr, "grad_sum_chips_" + nm) for f, nm in zip(full, names)]


EARLY_NAMES = ("w_o", "w_up", "w_down", "w_in_1")


def _early_items(grads):
    return [tuple(grads[n]) for n in ("w_o", "w_up", "w_down")] + [_halves(grads["w_in"][1])]


def kernel(x, norm1_g, w_in, sgu_norm_g, sgu_w, sgu_b, pool_w, pool_scale, mix_norm_g, w_o, norm2_g, w_up, conv_w, conv_b, w_down, final_g, loss_target, m_norm1_g, m_w_in, m_sgu_norm_g, m_sgu_w, m_sgu_b, m_pool_w, m_pool_scale, m_mix_norm_g, m_w_o, m_norm2_g, m_w_up, m_conv_w, m_conv_b, m_w_down, m_final_g, v_norm1_g, v_w_in, v_sgu_norm_g, v_sgu_w, v_sgu_b, v_pool_w, v_pool_scale, v_mix_norm_g, v_w_o, v_norm2_g, v_w_up, v_conv_w, v_conv_b, v_w_down, v_final_g):
    names = ("norm1_g", "w_in", "sgu_norm_g", "sgu_w", "sgu_b", "pool_w", "pool_scale", "mix_norm_g", "w_o",
             "norm2_g", "w_up", "conv_w", "conv_b", "w_down", "final_g")
    p = dict(zip(names, (norm1_g, w_in, sgu_norm_g, sgu_w, sgu_b, pool_w, pool_scale, mix_norm_g, w_o, norm2_g,
                         w_up, conv_w, conv_b, w_down, final_g)))
    pm = dict(zip(names, (m_norm1_g, m_w_in, m_sgu_norm_g, m_sgu_w, m_sgu_b, m_pool_w, m_pool_scale, m_mix_norm_g,
                          m_w_o, m_norm2_g, m_w_up, m_conv_w, m_conv_b, m_w_down, m_final_g)))
    pv = dict(zip(names, (v_norm1_g, v_w_in, v_sgu_norm_g, v_sgu_w, v_sgu_b, v_pool_w, v_pool_scale, v_mix_norm_g,
                          v_w_o, v_norm2_g, v_w_up, v_conv_w, v_conv_b, v_w_down, v_final_g)))
    c = lax.axis_index("c")
    gathered, late = _gather_weights(p)
    full = dict(p)
    full.update(gathered)

    c_arr = jnp.reshape(c, (1,)).astype(jnp.int32)
    early_types = [ICI_DT] * len(EARLY_NAMES)
    loss, dx, grads, (sent, received) = _local_step(
        x[0], loss_target[0], full, late,
        (_early_items, lambda items, swapped: _pair_adds(items, swapped, EARLY_NAMES, early_types, c_arr)))
    early_sums = _reduce_end(received, sent, EARLY_NAMES, c_arr)
    small_pack = _small_pack(grads, loss)
    late_names = ("w_in_0", "small")
    late_sent = _reduce_begin([_halves(grads["w_in"][0]), _halves(small_pack)], late_names, [ICI_DT, F32], c_arr, "late")
    late_sums = _reduce_end(_grad_exchange(late_sent), late_sent, late_names, c_arr)
    r_o, r_up, r_down, r_in1, r_in0, r_small = _grad_share(early_sums + late_sums)
    g = dict(w_o=r_o, w_up=r_up, w_down=r_down,
             w_in=jnp.stack([r_in0.reshape(D_MODEL, -1), r_in1.reshape(D_MODEL, -1)]))
    g.update(_unpack_small(r_small.reshape(2 * SP_HALF, D_MODEL)))
    d, nm, nv = {}, {}, {}
    for n in BIG_NAMES:
        d[n], nm[n], nv[n] = _adamw_big(g[n], p[n], pm[n], pv[n], "adamw_" + n)

    def two_d(a):
        return a.reshape(1, -1) if a.ndim == 1 else a

    ds, ms, vs = _adamw_small([two_d(g[n]) for n in SMALL_NAMES], [two_d(p[n]) for n in SMALL_NAMES],
                              [two_d(pm[n]) for n in SMALL_NAMES], [two_d(pv[n]) for n in SMALL_NAMES])
    for k, n in enumerate(SMALL_NAMES):
        d[n], nm[n], nv[n] = (a.reshape(p[n].shape) for a in (ds[k], ms[k], vs[k]))
    return (g["extra"], dx[None], *[g[n] for n in names], *[d[n] for n in names], *[nm[n] for n in names],
            *[nv[n] for n in names])
```

```python
import functools

import jax
import jax.numpy as jnp
from jax import lax
from jax.experimental import pallas as pl
from jax.experimental.pallas import tpu as pltpu

F32 = jnp.float32
MXU_DT = jnp.bfloat16

D_MODEL = 1024
DEPTH = 2
HEAD_DIM = 64
W_A = 256
W_B = 512
W_C = 256
IN_COLS = 2 * W_A + 3 * W_B + W_C
CHUNK = 128
POOL_WINDOWS = (2, 4, 8, 16)
D_FF = 2816
EPS = 1e-6
N_CHIPS = 4

ADAM_LR = 0.001
ADAM_B1 = 0.9
ADAM_B2 = 0.999
ADAM_EPS = 1e-08
ADAM_WD = 0.01
ADAM_STEP = 10

LANES = 128
TQ = 256
TK = 256
TM = 256
TM_MM = 512
HALO = 16
VMEM_LIMIT = 56 * 1024 * 1024

ROWS_CONV = 16
ROWS_SMALL = 240
SP_HALF = (ROWS_CONV + ROWS_SMALL) // 2
ICI_DT = jnp.bfloat16

BIG_NAMES = ("w_in", "w_o", "w_up", "w_down", "conv_w")
SMALL_NAMES = ("norm1_g", "sgu_norm_g", "sgu_w", "sgu_b", "pool_w", "pool_scale",
               "mix_norm_g", "norm2_g", "conv_b", "final_g")
SMALL_SHAPES = {
    "norm1_g": (DEPTH, D_MODEL), "sgu_norm_g": (DEPTH, W_A), "sgu_w": (DEPTH, 4, CHUNK, CHUNK),
    "sgu_b": (DEPTH, 4, CHUNK), "pool_w": (DEPTH, 4, 64, 64), "pool_scale": (DEPTH, W_C),
    "mix_norm_g": (DEPTH, D_MODEL), "norm2_g": (DEPTH, D_MODEL), "conv_b": (DEPTH, 2 * D_FF),
    "final_g": (D_MODEL,),
}


def _call(body, **kw):
    return pl.pallas_call(body, **kw)


def _params(*sem):
    return pltpu.CompilerParams(dimension_semantics=sem, vmem_limit_bytes=VMEM_LIMIT)


def _dot(a, b):
    return jnp.dot(a, b, preferred_element_type=F32)


def _dot_nt(a, b):
    return lax.dot_general(a, b, (((1,), (1,)), ((), ())), preferred_element_type=F32)


def _dot_tn(a, b):
    return lax.dot_general(a, b, (((0,), (0,)), ((), ())), preferred_element_type=F32)


def _group_mean(sq, gmat):
    sqb = sq.astype(MXU_DT)
    cols = [_dot(sqb[:, b * LANES:(b + 1) * LANES], gmat) for b in range(sq.shape[1] // LANES)]
    return cols[0] if len(cols) == 1 else jnp.concatenate(cols, axis=-1)


def _group_matrix():
    r = jnp.arange(LANES)
    return jnp.where((r[:, None] // HEAD_DIM) == (r[None, :] // HEAD_DIM), 1.0 / HEAD_DIM, 0.0).astype(MXU_DT)


def _tile(n):
    return max(t for t in range(LANES, 1536 + 1, LANES) if n % t == 0)


def _row_spec(tm, cols, col_block=0):
    return pl.BlockSpec((tm, cols), lambda i, cb=col_block: (i, cb))


def _full_spec(shape):
    nd = len(shape)
    return pl.BlockSpec(shape, lambda *_: (0,) * nd)


def _mm_res(a, w, res, name):
    S, K = a.shape
    N = w.shape[1]
    tm = TM_MM

    def body(a_ref, w_ref, r_ref, o_ref):
        o_ref[...] = r_ref[...] + _dot(a_ref[...], w_ref[...])

    return _call(
        body, name=name, grid=(S // tm,),
        in_specs=[_row_spec(tm, K), _full_spec((K, N)), _row_spec(tm, N)],
        out_specs=_row_spec(tm, N),
        out_shape=jax.ShapeDtypeStruct((S, N), F32),
        compiler_params=_params("parallel"),
    )(a, w, res)


def _mm_tn(a, b, name, col_tiles=False):
    S, K1 = a.shape
    N = b.shape[1]
    ts = 2 * TM_MM
    tk = _tile(K1)
    tn = _tile(N)
    if col_tiles:
        out_spec = pl.BlockSpec((None, tk, tn), lambda m, n, s: (n, m, 0))
        out_shape = jax.ShapeDtypeStruct((N // tn, K1, tn), F32)
    else:
        out_spec = pl.BlockSpec((tk, tn), lambda m, n, s: (m, n))
        out_shape = jax.ShapeDtypeStruct((K1, N), F32)

    def body(a_ref, b_ref, o_ref):
        @pl.when(pl.program_id(2) == 0)
        def _():
            o_ref[...] = jnp.zeros_like(o_ref)

        o_ref[...] += _dot_tn(a_ref[...], b_ref[...].astype(MXU_DT))

    return _call(
        body, name=name, grid=(K1 // tk, N // tn, S // ts),
        in_specs=[pl.BlockSpec((ts, tk), lambda m, n, s: (s, m)),
                  pl.BlockSpec((ts, tn), lambda m, n, s: (s, n))],
        out_specs=out_spec, out_shape=out_shape,
        compiler_params=_params("parallel", "parallel", "arbitrary"),
    )(a, b)


def _mm_nt_rmsbwd(pairs, x, g, dres, name):
    S, D = x.shape
    tm = TM
    n = len(pairs)

    def body(*refs):
        a_refs = refs[:n]
        w_refs = refs[n:2 * n]
        x_ref, g_ref, r_ref, dx_ref, dg_ref = refs[2 * n:]
        dh = _dot_nt(a_refs[0][...], w_refs[0][...])
        for k in range(1, n):
            dh += _dot_nt(a_refs[k][...], w_refs[k][...])
        xv = x_ref[...]
        r = lax.rsqrt(jnp.mean(xv * xv, axis=-1, keepdims=True) + EPS)
        xhat = xv * r

        @pl.when(pl.program_id(0) == 0)
        def _():
            dg_ref[...] = jnp.zeros_like(dg_ref)

        dg_ref[...] += jnp.sum(dh * xhat, axis=0, keepdims=True)
        dxh = dh * g_ref[...]
        dx_ref[...] = r_ref[...] + r * (dxh - xhat * jnp.mean(dxh * xhat, axis=-1, keepdims=True))

    in_specs = ([_row_spec(tm, a.shape[1]) for a, _ in pairs] + [_full_spec(w.shape) for _, w in pairs]
                + [_row_spec(tm, D), _full_spec((1, D)), _row_spec(tm, D)])
    return _call(
        body, name=name, grid=(S // tm,), in_specs=in_specs,
        out_specs=[_row_spec(tm, D), _full_spec((1, D))],
        out_shape=[jax.ShapeDtypeStruct((S, D), F32), jax.ShapeDtypeStruct((1, D), F32)],
        compiler_params=_params("arbitrary"),
    )(*[a for a, _ in pairs], *[w for _, w in pairs], x, g, dres)


def _down_proj_loss(a, w, res, g, tgt):
    S, D = res.shape
    K = a.shape[1]
    tm = TM

    def body(a_ref, w_ref, res_ref, g_ref, t_ref, dx_ref, dg_ref, l_ref):
        xv = res_ref[...] + _dot(a_ref[...], w_ref[...])
        r = lax.rsqrt(jnp.mean(xv * xv, axis=-1, keepdims=True) + EPS)
        xhat = xv * r
        diff = xhat * g_ref[...] - t_ref[...]

        @pl.when(pl.program_id(0) == 0)
        def _():
            dg_ref[...] = jnp.zeros_like(dg_ref)
            l_ref[...] = jnp.zeros_like(l_ref)

        l_ref[...] += jnp.full(l_ref.shape, 0.5 * jnp.sum(jnp.mean(diff * diff, axis=-1, keepdims=True)), F32)
        dout = diff * (1.0 / D)
        dg_ref[...] += jnp.sum(dout * xhat, axis=0, keepdims=True)
        dxh = dout * g_ref[...]
        dx_ref[...] = r * (dxh - xhat * jnp.mean(dxh * xhat, axis=-1, keepdims=True))

    return _call(
        body, name="down_proj_loss", grid=(S // tm,),
        in_specs=[_row_spec(tm, K), _full_spec((K, D)), _row_spec(tm, D), _full_spec((1, D)), _row_spec(tm, D)],
        out_specs=[_row_spec(tm, D), _full_spec((1, D)), _full_spec((8, LANES))],
        out_shape=[jax.ShapeDtypeStruct((S, D), F32), jax.ShapeDtypeStruct((1, D), F32),
                   jax.ShapeDtypeStruct((8, LANES), F32)],
        compiler_params=_params("arbitrary"),
    )(a, w, res, g, tgt)


def _mix_out(ya, yb, yc, gm, wo, x, gmat):
    S = x.shape[0]
    tm = TM

    def body(ya_ref, yb_ref, yc_ref, gm_ref, wo_ref, x_ref, gmat_ref, x2_ref, yn_ref):
        y = jnp.concatenate([ya_ref[...], yb_ref[...], yc_ref[...]], axis=-1)
        r = lax.rsqrt(_group_mean(y * y, gmat_ref[...]) + EPS)
        yn = (y * r * gm_ref[...]).astype(MXU_DT)
        yn_ref[...] = yn
        x2_ref[...] = x_ref[...] + _dot(yn, wo_ref[...])

    return _call(
        body, name="mix_out", grid=(S // tm,),
        in_specs=[_row_spec(tm, W_A), _row_spec(tm, W_B), _row_spec(tm, W_C), _full_spec((1, D_MODEL)),
                  _full_spec((D_MODEL, D_MODEL)), _row_spec(tm, D_MODEL), _full_spec((LANES, LANES))],
        out_specs=[_row_spec(tm, D_MODEL), _row_spec(tm, D_MODEL)],
        out_shape=[jax.ShapeDtypeStruct((S, D_MODEL), F32), jax.ShapeDtypeStruct((S, D_MODEL), MXU_DT)],
        compiler_params=_params("parallel"),
    )(ya, yb, yc, gm, wo, x, gmat)


def _mix_out_bwd(dx2, wo, ya, yb, yc, gm, gmat, carry_swap=None):
    S = dx2.shape[0]
    tm = TM
    nb = S // tm
    items = list(carry_swap or [])
    ns = len(items)

    def body(*refs):
        dx2_ref, wo_ref, ya_ref, yb_ref, yc_ref, gm_ref, gmat_ref = refs[:7]
        dya_ref, dyb_ref, dyc_ref, dgm_ref = refs[7 + 2 * ns:11 + 2 * ns]
        if ns:
            start, finish = _swap_plan(refs[7:7 + ns], refs[7 + ns:7 + 2 * ns], refs[11 + 2 * ns:11 + 3 * ns],
                                       refs[11 + 3 * ns], refs[12 + 3 * ns])

            @pl.when(pl.program_id(0) == 0)
            def _():
                start()

        dyn = _dot_nt(dx2_ref[...].astype(MXU_DT), wo_ref[...])
        y = jnp.concatenate([ya_ref[...], yb_ref[...], yc_ref[...]], axis=-1)
        r = lax.rsqrt(_group_mean(y * y, gmat_ref[...]) + EPS)
        yhat = y * r

        @pl.when(pl.program_id(0) == 0)
        def _():
            dgm_ref[...] = jnp.zeros_like(dgm_ref)

        dgm_ref[...] += jnp.sum(dyn * yhat, axis=0, keepdims=True)
        dyh = dyn * gm_ref[...]
        dy = r * (dyh - yhat * _group_mean(dyh * yhat, gmat_ref[...]))
        dya_ref[...] = dy[:, :W_A]
        dyb_ref[...] = dy[:, W_A:W_A + W_B]
        dyc_ref[...] = dy[:, W_A + W_B:]

        if ns:
            @pl.when(pl.program_id(0) == nb - 1)
            def _():
                finish()

    swapped, sems = _swap_shapes(items) if ns else ([], [])
    return _call(
        body, name="mix_out_bwd_swap" if ns else "mix_out_bwd", grid=(nb,),
        in_specs=[_row_spec(tm, D_MODEL), _full_spec((D_MODEL, D_MODEL)), _row_spec(tm, W_A), _row_spec(tm, W_B),
                  _row_spec(tm, W_C), _full_spec((1, D_MODEL)), _full_spec((LANES, LANES))] + [ANY] * (2 * ns),
        out_specs=[_row_spec(tm, W_A), _row_spec(tm, W_B), _row_spec(tm, W_C), _full_spec((1, D_MODEL))]
        + [ANY] * ns,
        out_shape=[jax.ShapeDtypeStruct((S, W_A), F32), jax.ShapeDtypeStruct((S, W_B), F32),
                   jax.ShapeDtypeStruct((S, W_C), F32), jax.ShapeDtypeStruct((1, D_MODEL), F32)] + swapped,
        scratch_shapes=sems,
        compiler_params=_params("arbitrary"),
    )(dx2, wo, ya, yb, yc, gm, gmat, *[a0 for a0, _ in items], *[a1 for _, a1 in items])


_SQRT_HALF = 0.7071067811865476
_INV_SQRT_2PI = 0.3989422804014327


def _sgu_common(a, sng, wm_ref, bias, gmat):
    phi = 0.5 * (1.0 + lax.erf(a * _SQRT_HALF))
    ga = a * phi
    u = ga[:, :W_A]
    v = ga[:, W_A:]
    r = lax.rsqrt(_group_mean(v * v, gmat) + EPS)
    vhat = v * r
    vn = (vhat * sng).astype(MXU_DT)
    head = lax.broadcasted_iota(jnp.int32, (CHUNK, W_A), 1) // HEAD_DIM
    rows = []
    for c in range(a.shape[0] // CHUNK):
        vc = vn[c * CHUNK:(c + 1) * CHUNK]
        s = bias
        for h in range(4):
            s = s + jnp.where(head == h, _dot(wm_ref[h], vc), 0.0)
        rows.append(s)
    s = jnp.concatenate(rows, axis=0)
    return phi, u, r, vhat, vn, s


def _tril_weights(sgu_w_l):
    t = jnp.arange(CHUNK)
    return jnp.where((t[None, :] <= t[:, None])[None], sgu_w_l, 0.0)


def _sgu_bwd(proj, dy, sng, wm, wmt, bias, gmat):
    S = proj.shape[0]
    tm = TM

    def body(a_ref, dy_ref, sng_ref, wm_ref, wmt_ref, b_ref, gmat_ref, da_ref, dw_ref, db_ref, dsng_ref):
        a = a_ref[...]
        dy = dy_ref[...]
        gmat = gmat_ref[...]
        sng = sng_ref[...]
        phi, u, r, vhat, vn, s = _sgu_common(a, sng, wm_ref, b_ref[...], gmat)
        du = dy * s
        ds = dy * u

        @pl.when(pl.program_id(0) == 0)
        def _():
            dw_ref[...] = jnp.zeros_like(dw_ref)
            db_ref[...] = jnp.zeros_like(db_ref)
            dsng_ref[...] = jnp.zeros_like(dsng_ref)

        head = lax.broadcasted_iota(jnp.int32, (CHUNK, W_A), 1) // HEAD_DIM
        tt = lax.broadcasted_iota(jnp.int32, (CHUNK, CHUNK), 0)
        ss = lax.broadcasted_iota(jnp.int32, (CHUNK, CHUNK), 1)
        rows = []
        for c in range(tm // CHUNK):
            dsc = ds[c * CHUNK:(c + 1) * CHUNK]
            vc = vn[c * CHUNK:(c + 1) * CHUNK]
            db_ref[...] += dsc
            dsb = dsc.astype(MXU_DT)
            dvn = jnp.zeros((CHUNK, W_A), F32)
            for h in range(4):
                dvn = dvn + jnp.where(head == h, _dot(wmt_ref[h], dsb), 0.0)
                dsh = jnp.where(head == h, dsc, 0.0).astype(MXU_DT)
                dw_ref[h] += jnp.where(ss <= tt, _dot_nt(dsh, vc), 0.0)
            rows.append(dvn)
        dvn = jnp.concatenate(rows, axis=0)
        dsng_ref[...] += jnp.sum(dvn * vhat, axis=0, keepdims=True)
        dvh = dvn * sng
        dv = r * (dvh - vhat * _group_mean(dvh * vhat, gmat))
        dga = jnp.concatenate([du, dv], axis=-1)
        dgelu = phi + a * (_INV_SQRT_2PI * jnp.exp(-0.5 * a * a))
        da_ref[...] = (dga * dgelu).astype(da_ref.dtype)

    return _call(
        body, name="sgu_bwd", grid=(S // tm,),
        in_specs=[_row_spec(tm, 2 * W_A), _row_spec(tm, W_A), _full_spec((1, W_A)), _full_spec((4, CHUNK, CHUNK)),
                  _full_spec((4, CHUNK, CHUNK)), _full_spec((CHUNK, W_A)), _full_spec((LANES, LANES))],
        out_specs=[_row_spec(tm, 2 * W_A), _full_spec((4, CHUNK, CHUNK)), _full_spec((CHUNK, W_A)),
                   _full_spec((1, W_A))],
        out_shape=[jax.ShapeDtypeStruct((S, 2 * W_A), MXU_DT), jax.ShapeDtypeStruct((4, CHUNK, CHUNK), F32),
                   jax.ShapeDtypeStruct((CHUNK, W_A), F32), jax.ShapeDtypeStruct((1, W_A), F32)],
        compiler_params=_params("arbitrary"),
    )(proj, dy, sng, wm, wmt, bias, gmat)


HG = 4
LW = HG * HEAD_DIM
Q_BLK0 = (2 * W_A) // LW
K_BLK0 = Q_BLK0 + W_B // LW
V_BLK0 = K_BLK0 + W_B // LW
N_GROUPS = W_B // LW
EXP_IS_ZERO_BELOW = -120.0


def _tri_matrix():
    r = jnp.arange(TK)
    return (r[:, None] > r[None, :]).astype(MXU_DT)


def _stack_heads(a):
    head = lax.broadcasted_iota(jnp.int32, a.shape, 1) // HEAD_DIM
    return jnp.concatenate([jnp.where(head == h, a, 0.0) for h in range(HG)], axis=0).astype(MXU_DT)


def _unstack_heads(a):
    head = lax.broadcasted_iota(jnp.int32, (TQ, LW), 1) // HEAD_DIM
    out = a[:TQ]
    for h in range(1, HG):
        out = jnp.where(head == h, a[h * TQ:(h + 1) * TQ], out)
    return out


def _sb_scores(q2, kj, tri, key_offset):
    z = _dot_nt(q2, kj)
    sp = jnp.log(1.0 + jnp.exp(-jnp.abs(z)))
    lsp = jnp.minimum(z, 0.0) - sp
    lsm = lsp - z
    msk = None
    if key_offset is not None:
        row = lax.broadcasted_iota(jnp.int32, z.shape, 0) & (TQ - 1)
        col = lax.broadcasted_iota(jnp.int32, z.shape, 1) + key_offset
        msk = col < row
        lsm = jnp.where(msk, lsm, 0.0)
    tail = _dot(lsm.astype(MXU_DT), tri)
    return lsp, lsm, tail, msk


def _sb_fwd(proj_b, tri, carry_gather=None):
    S = proj_b.shape[0]
    nq = S // TQ
    kpq = TQ // TK
    assert S // TK < LANES
    shards = list(carry_gather or [])
    ng = len(shards)

    def body(*refs):
        q_ref, k_ref, v_ref, tri_ref = refs[:4]
        o_ref, rb_ref = refs[4 + ng:6 + ng]
        acc_ref = refs[6 + 2 * ng]
        i = pl.program_id(1)
        if ng:
            step = pl.program_id(0) * nq + i
            sender = 1
            c, sends, arrivals, forwards = _gather_plan(refs[4:4 + ng], refs[6 + ng:6 + 2 * ng], refs[7 + 2 * ng],
                                                        refs[8 + 2 * ng], 1, sender)

            @pl.when(jnp.logical_and(step == 0, c == sender))
            def _():
                for cp in sends:
                    cp.start()

            @pl.when(jnp.logical_and(step == N_GROUPS * nq - max(nq // 8, 1), c == sender))
            def _():
                for arrived, onward in zip(arrivals, forwards):
                    arrived.wait_recv()
                    onward.start()


        lane2 = lax.broadcasted_iota(jnp.int32, (HG * TQ, LANES), 1)
        q2 = _stack_heads(q_ref[...].astype(F32) * (HEAD_DIM ** -0.5))
        tri = tri_ref[...]
        rb_ref[...] = jnp.zeros_like(rb_ref)

        def block(j, run, key_offset=None, first=False):
            start = pl.multiple_of(j * TK, TK)
            kj = k_ref[pl.ds(start, TK), :]
            vj = v_ref[pl.ds(start, TK), :]
            lsp, lsm, tail, msk = _sb_scores(q2, kj, tri, key_offset)
            rb_ref[...] = jnp.where(lane2 == j, run, rb_ref[...])
            att = jnp.exp(lsp + tail + run)
            if msk is not None:
                att = jnp.where(msk, att, 0.0)
            pv = _dot(att.astype(MXU_DT), vj)
            if first:
                acc_ref[...] = pv
            else:
                acc_ref[...] += pv
            return run + tail[:, :1] + lsm[:, :1]

        past = i * kpq

        def overlapping():
            run = jnp.zeros((HG * TQ, 1), F32)
            for d in reversed(range(kpq)):
                run = block(i * kpq + d, run, key_offset=d * TK, first=(d == kpq - 1))
            return run

        def alive(run):
            return (jnp.max(run) > EXP_IS_ZERO_BELOW).astype(jnp.int32)

        def walk(carry):
            n, run, _ = carry
            run = block(past - 1 - n, run)
            return n + 1, run, alive(run)

        n, run = lax.cond(past > 0, lambda: (jnp.int32(1), block(past - 1, overlapping())),
                          lambda: (jnp.int32(0), overlapping()))
        n, _, _ = lax.while_loop(lambda s: jnp.logical_and(s[0] < past, s[2] > 0), walk, (n, run, alive(run)))
        rb_ref[...] = jnp.where(lane2 == LANES - 1, n.astype(F32), rb_ref[...])
        o_ref[...] = _unstack_heads(acc_ref[...])

        if ng:
            @pl.when(jnp.logical_and(step == N_GROUPS * nq - 1, c == sender))
            def _():
                for cp in sends + forwards:
                    cp.wait_send()

            @pl.when(jnp.logical_and(step == N_GROUPS * nq - 1, c != sender))
            def _():
                for cp in forwards:
                    cp.wait_recv()

    once = pl.Buffered(1)
    gathered, sems = _gather_shapes(shards) if ng else ([], [])
    return _call(
        body, name="sb_fwd_gather" if ng else "sb_fwd", grid=(N_GROUPS, nq),
        in_specs=[pl.BlockSpec((TQ, LW), lambda p, i: (i, Q_BLK0 + p)),
                  pl.BlockSpec((S, LW), lambda p, i: (0, K_BLK0 + p), pipeline_mode=once),
                  pl.BlockSpec((S, LW), lambda p, i: (0, V_BLK0 + p), pipeline_mode=once),
                  pl.BlockSpec((TK, TK), lambda p, i: (0, 0))] + [ANY] * ng,
        out_specs=[pl.BlockSpec((TQ, LW), lambda p, i: (i, p)),
                   pl.BlockSpec((None, None, HG * TQ, LANES), lambda p, i: (p, i, 0, 0))] + [ANY] * ng,
        out_shape=[jax.ShapeDtypeStruct((S, W_B), F32),
                   jax.ShapeDtypeStruct((N_GROUPS, nq, HG * TQ, LANES), F32)] + gathered,
        scratch_shapes=[pltpu.VMEM((HG * TQ, LW), F32)] + sems,
        compiler_params=_params("arbitrary", "arbitrary"),
    )(proj_b, proj_b, proj_b, tri, *shards)


def _sb_bwd(proj_b, dyb, rb, tri, trit, carry_exchange=None):
    S = proj_b.shape[0]
    nq = S // TQ
    kpq = TQ // TK
    hs = list(carry_exchange or [])
    ne = len(hs)

    def body(*refs):
        q_ref, k_ref, v_ref, do_ref, rb_ref, tri_ref, trit_ref = refs[:7]
        dq_ref, dk_acc, dv_acc = refs[7 + ne:10 + ne]
        dq_acc = refs[10 + 2 * ne]
        i = pl.program_id(1)
        if ne:
            tick = pl.program_id(0) * nq + i
            sends, arrivals = _exchange_plan(refs[7:7 + ne], refs[10 + ne:10 + 2 * ne], refs[11 + 2 * ne],
                                             refs[12 + 2 * ne])

            @pl.when(tick == 0)
            def _():
                for cp in sends:
                    cp.start()

        lane2 = lax.broadcasted_iota(jnp.int32, (HG * TQ, LANES), 1)
        scale = HEAD_DIM ** -0.5
        q2 = _stack_heads(q_ref[...].astype(F32) * scale)
        do2 = _stack_heads(do_ref[...])
        tri = tri_ref[...]
        trit = trit_ref[...]

        @pl.when(i == 0)
        def _():
            dk_acc[...] = jnp.zeros_like(dk_acc)
            dv_acc[...] = jnp.zeros_like(dv_acc)

        dq_acc[...] = jnp.zeros_like(dq_acc)

        def block(j, pre, key_offset=None):
            start = pl.multiple_of(j * TK, TK)
            kj = k_ref[pl.ds(start, TK), :]
            vj = v_ref[pl.ds(start, TK), :]
            lsp, lsm, tail, msk = _sb_scores(q2, kj, tri, key_offset)
            run = jnp.sum(jnp.where(lane2 == j, rb_ref[...], 0.0), axis=-1, keepdims=True)
            att = jnp.exp(lsp + tail + run)
            if msk is not None:
                att = jnp.where(msk, att, 0.0)
            beta = jnp.exp(lsp)
            dl = _dot_nt(do2, vj) * att
            cin = _dot(dl.astype(MXU_DT), trit)
            dz = dl * (1.0 - beta) - beta * (pre + cin)
            if msk is not None:
                dz = jnp.where(msk, dz, 0.0)
            dzb = dz.astype(MXU_DT)
            dq_acc[...] += _dot(dzb, kj)
            dk_acc[pl.ds(start, TK), :] += _dot_tn(dzb, q2)
            dv_acc[pl.ds(start, TK), :] += _dot_tn(att.astype(MXU_DT), do2)
            return pre + cin[:, TK - 1:] + dl[:, TK - 1:]

        past = i * kpq
        walked = jnp.max(jnp.where(lane2[:8] == LANES - 1, rb_ref[pl.ds(0, 8), :], 0.0)).astype(jnp.int32)
        walked = jnp.clip(walked, 0, past)
        def overlapping(pre):
            for d in range(kpq):
                pre = block(i * kpq + d, pre, key_offset=d * TK)
            return jnp.int32(0)

        def with_past():
            pre = lax.fori_loop(past - walked, past - 1, lambda j, pre: block(j, pre),
                                jnp.zeros((HG * TQ, 1), F32))
            return overlapping(block(past - 1, pre))

        lax.cond(walked > 0, with_past, lambda: overlapping(jnp.zeros((HG * TQ, 1), F32)))
        dq_ref[...] = (_unstack_heads(dq_acc[...]) * scale).astype(dq_ref.dtype)

        if ne:
            @pl.when(tick == N_GROUPS * nq - 1)
            def _():
                for cp in arrivals:
                    cp.wait_recv()
                for cp in sends:
                    cp.wait_send()

    once = pl.Buffered(1)
    exchanged, sems = _exchange_shapes(hs) if ne else ([], [])
    return _call(
        body, name="sb_bwd_exchange" if ne else "sb_bwd", grid=(N_GROUPS, nq),
        in_specs=[pl.BlockSpec((TQ, LW), lambda p, i: (i, Q_BLK0 + p)),
                  pl.BlockSpec((S, LW), lambda p, i: (0, K_BLK0 + p), pipeline_mode=once),
                  pl.BlockSpec((S, LW), lambda p, i: (0, V_BLK0 + p), pipeline_mode=once),
                  pl.BlockSpec((TQ, LW), lambda p, i: (i, p)),
                  pl.BlockSpec((None, None, HG * TQ, LANES), lambda p, i: (p, i, 0, 0)),
                  pl.BlockSpec((TK, TK), lambda p, i: (0, 0)),
                  pl.BlockSpec((TK, TK), lambda p, i: (0, 0))] + [ANY] * ne,
        out_specs=[pl.BlockSpec((TQ, LW), lambda p, i: (i, p)),
                   pl.BlockSpec((S, LW), lambda p, i: (0, p), pipeline_mode=once),
                   pl.BlockSpec((S, LW), lambda p, i: (0, p), pipeline_mode=once)] + [ANY] * ne,
        out_shape=[jax.ShapeDtypeStruct((S, W_B), MXU_DT), jax.ShapeDtypeStruct((S, W_B), F32),
                   jax.ShapeDtypeStruct((S, W_B), F32)] + exchanged,
        scratch_shapes=[pltpu.VMEM((HG * TQ, LW), F32)] + sems,
        compiler_params=_params("arbitrary", "arbitrary"),
    )(proj_b, proj_b, proj_b, dyb, rb, tri, trit, *hs)


P_BLK = (2 * W_A + 3 * W_B) // W_C


def _window_lanes():
    g = lax.broadcasted_iota(jnp.int32, (1, W_C), 1) // (W_C // 4)
    w = jnp.where(g == 0, POOL_WINDOWS[0], jnp.where(g == 1, POOL_WINDOWS[1],
                  jnp.where(g == 2, POOL_WINDOWS[2], POOL_WINDOWS[3])))
    return g, w


def _shift_rows(ext, k, tm, lead):
    n = ext.shape[0]
    return pltpu.roll(ext, shift=k % n, axis=0)[lead:lead + tm]


def _pool_diff(p_cur, p_halo, row0, tm):
    ext = jnp.concatenate([p_halo, p_cur], axis=0)
    g, w = _window_lanes()
    acc = ext
    sums = []
    for sh in (1, 2, 4, 8):
        acc = acc + pltpu.roll(acc, shift=sh, axis=0)
        sums.append(acc[HALO:HALO + tm])
    wsum = jnp.where(g == 0, sums[0], jnp.where(g == 1, sums[1], jnp.where(g == 2, sums[2], sums[3])))
    pos = (row0 + 1 + lax.broadcasted_iota(jnp.int32, (tm, W_C), 0)).astype(F32)
    cnt = jnp.minimum(pos, w.astype(F32))
    return wsum / cnt - p_cur, cnt


def _pool_specs(tm, nrow_blocks_halo):
    cur = pl.BlockSpec((tm, W_C), lambda i: (i, P_BLK))
    prev = pl.BlockSpec((HALO, W_C), lambda i: (jnp.maximum(i * (tm // HALO) - 1, 0), P_BLK))
    return cur, prev


def _in_proj_groups(x, g, w, sng, wm, bias, gmat, wbd, scale, carry_gather=None):
    S, D = x.shape
    tm = TM_MM
    nb = S // tm
    p0 = 2 * W_A + 3 * W_B
    qkv_chunk = 3 * W_B // 2
    shards = list(carry_gather or [])
    ng = len(shards)

    def body(*refs):
        x_ref, g_ref, w_ref, sng_ref, wm_ref, b_ref, gmat_ref, wbd_ref, sc_ref = refs[:9]
        h_ref, o_ref, ob_ref, ya_ref, yc_ref = refs[9 + ng:14 + ng]
        tail_ref = refs[14 + 2 * ng]
        i = pl.program_id(0)
        if ng:
            sender = 0
            c, sends, arrivals, forwards = _gather_plan(refs[9:9 + ng], refs[14 + ng:14 + 2 * ng], refs[15 + 2 * ng],
                                                        refs[16 + 2 * ng], 0, sender)

            @pl.when(jnp.logical_and(i == 0, c == sender))
            def _():
                for cp in sends:
                    cp.start()
        xv = x_ref[...]
        r = lax.rsqrt(jnp.mean(xv * xv, axis=-1, keepdims=True) + EPS)
        h = (xv * r * g_ref[...]).astype(h_ref.dtype)
        h_ref[...] = h

        def project(c0, c1):
            acc = _dot(h, w_ref[:, c0:c1])
            o_ref[:, c0:c1] = acc
            ob_ref[:, c0:c1] = acc.astype(ob_ref.dtype)
            return acc

        a = project(0, 2 * W_A)
        _, u, _, _, _, s = _sgu_common(a, sng_ref[...], wm_ref, b_ref[...], gmat_ref[...])
        ya_ref[...] = u * s
        for c0 in range(2 * W_A, p0, qkv_chunk):
            project(c0, c0 + qkv_chunk)
        p = project(p0, p0 + W_C)
        halo = jnp.where(i > 0, tail_ref[...], 0.0)
        tail_ref[...] = p[tm - HALO:]
        d, _ = _pool_diff(p, halo, i * tm, tm)
        yc_ref[...] = _dot(d.astype(MXU_DT), wbd_ref[...]) * sc_ref[...]

        if ng:
            @pl.when(jnp.logical_and(i == nb - 1, c == sender))
            def _():
                for arrived, onward in zip(arrivals, forwards):
                    arrived.wait_recv()
                    onward.start()
                for cp in sends + forwards:
                    cp.wait_send()

            @pl.when(jnp.logical_and(i == nb - 1, c != sender))
            def _():
                for cp in forwards:
                    cp.wait_recv()

    gathered, sems = _gather_shapes(shards) if ng else ([], [])
    return _call(
        body, name="in_proj_groups_gather" if ng else "in_proj_groups", grid=(nb,),
        in_specs=[_row_spec(tm, D), _full_spec((1, D)),
                  pl.BlockSpec((D, IN_COLS), lambda i: (0, 0), pipeline_mode=pl.Buffered(1)),
                  _full_spec((1, W_A)), _full_spec((4, CHUNK, CHUNK)), _full_spec((CHUNK, W_A)),
                  _full_spec((LANES, LANES)), _full_spec((W_C, W_C)), _full_spec((1, W_C))] + [ANY] * ng,
        out_specs=[_row_spec(tm, D), _row_spec(tm, IN_COLS), _row_spec(tm, IN_COLS), _row_spec(tm, W_A),
                   _row_spec(tm, W_C)] + [ANY] * ng,
        out_shape=[jax.ShapeDtypeStruct((S, D), MXU_DT), jax.ShapeDtypeStruct((S, IN_COLS), F32),
                   jax.ShapeDtypeStruct((S, IN_COLS), MXU_DT), jax.ShapeDtypeStruct((S, W_A), F32),
                   jax.ShapeDtypeStruct((S, W_C), F32)] + gathered,
        scratch_shapes=[pltpu.VMEM((HALO, W_C), F32)] + sems,
        compiler_params=_params("arbitrary"),
    )(x, g, w, sng, wm, bias, gmat, wbd, scale, *shards)


def _pool_bwd_a(proj, dy, wbd, scale):
    S = proj.shape[0]
    tm = TM

    def body(p_ref, ph_ref, dy_ref, w_ref, sc_ref, dd_ref, e_ref, dw_ref, dsc_ref):
        i = pl.program_id(0)
        halo = jnp.where(i > 0, ph_ref[...], 0.0)
        d, cnt = _pool_diff(p_ref[...], halo, i * tm, tm)
        db = d.astype(MXU_DT)
        dy = dy_ref[...]

        @pl.when(i == 0)
        def _():
            dw_ref[...] = jnp.zeros_like(dw_ref)
            dsc_ref[...] = jnp.zeros_like(dsc_ref)

        dsc_ref[...] += jnp.sum(dy * _dot(db, w_ref[...]), axis=0, keepdims=True)
        dys = (dy * sc_ref[...]).astype(MXU_DT)
        dw_ref[...] += _dot_tn(db, dys)
        dd = _dot_nt(dys, w_ref[...])
        dd_ref[...] = dd
        e_ref[...] = dd / cnt

    cur, prev = _pool_specs(tm, S // HALO)
    return _call(
        body, name="pool_bwd_a", grid=(S // tm,),
        in_specs=[cur, prev, _row_spec(tm, W_C), _full_spec((W_C, W_C)), _full_spec((1, W_C))],
        out_specs=[_row_spec(tm, W_C), _row_spec(tm, W_C), _full_spec((W_C, W_C)), _full_spec((1, W_C))],
        out_shape=[jax.ShapeDtypeStruct((S, W_C), F32), jax.ShapeDtypeStruct((S, W_C), F32),
                   jax.ShapeDtypeStruct((W_C, W_C), F32), jax.ShapeDtypeStruct((1, W_C), F32)],
        compiler_params=_params("arbitrary"),
    )(proj, proj, dy, wbd, scale)


def _pool_bwd_b(dd, e):
    S = dd.shape[0]
    tm = TM
    nb = S // tm

    def body(dd_ref, e_ref, en_ref, dp_ref):
        i = pl.program_id(0)
        halo = jnp.where(i < nb - 1, en_ref[...], 0.0)
        ext = jnp.concatenate([e_ref[...], halo], axis=0)
        n = ext.shape[0]
        g, _ = _window_lanes()
        acc = ext
        sums = []
        for sh in (1, 2, 4, 8):
            acc = acc + pltpu.roll(acc, shift=n - sh, axis=0)
            sums.append(acc[:tm])
        wsum = jnp.where(g == 0, sums[0], jnp.where(g == 1, sums[1], jnp.where(g == 2, sums[2], sums[3])))
        dp_ref[...] = (wsum - dd_ref[...]).astype(dp_ref.dtype)

    nxt = pl.BlockSpec((HALO, W_C), lambda i: (jnp.minimum((i + 1) * (tm // HALO), S // HALO - 1), 0))
    return _call(
        body, name="pool_bwd_b", grid=(nb,),
        in_specs=[_row_spec(tm, W_C), _row_spec(tm, W_C), nxt],
        out_specs=_row_spec(tm, W_C),
        out_shape=jax.ShapeDtypeStruct((S, W_C), MXU_DT),
        compiler_params=_params("parallel"),
    )(dd, e, e)


TN_FF = 1408
NB_FF = D_FF // TN_FF
CONV_ROWS = 8


def _conv(z_cur, z_halo, cwb, tm):
    ext = jnp.concatenate([z_halo, z_cur], axis=0)
    z2 = _shift_rows(ext, 2, tm, HALO)
    z1 = _shift_rows(ext, 1, tm, HALO)
    zc = cwb[3:4] + z2 * cwb[0:1] + z1 * cwb[1:2] + z_cur * cwb[2:3]
    return zc, z2, z1


def _up_proj_gate(x, g, w, cwb):
    S, D = x.shape
    tm = TM

    def body(x_ref, g_ref, w_ref, c_ref, h_ref, z_ref, zc_ref, f_ref, tail_ref):
        first = pl.program_id(0) == 0
        xv = x_ref[...]
        r = lax.rsqrt(jnp.mean(xv * xv, axis=-1, keepdims=True) + EPS)
        h = (xv * r * g_ref[...]).astype(h_ref.dtype)
        h_ref[...] = h
        for j in range(NB_FF):
            halves = []
            for col0 in (j * TN_FF, D_FF + j * TN_FF):
                zb = _dot(h, w_ref[:, col0:col0 + TN_FF]).astype(z_ref.dtype)
                z_ref[:, col0:col0 + TN_FF] = zb
                zf = zb.astype(F32)
                prev = jnp.where(first, 0.0, tail_ref[:, col0:col0 + TN_FF])
                tail_ref[:, col0:col0 + TN_FF] = zf[tm - HALO:]
                zc = _conv(zf, prev, c_ref[:, col0:col0 + TN_FF], tm)[0]
                zc_ref[:, col0:col0 + TN_FF] = zc.astype(zc_ref.dtype)
                halves.append(zc)
            gate, value = halves
            f_ref[:, j * TN_FF:(j + 1) * TN_FF] = (gate * jax.nn.sigmoid(gate) * value).astype(f_ref.dtype)

    return _call(
        body, name="up_proj_gate", grid=(S // tm,),
        in_specs=[_row_spec(tm, D), _full_spec((1, D)),
                  pl.BlockSpec((D, 2 * D_FF), lambda i: (0, 0), pipeline_mode=pl.Buffered(1)),
                  _full_spec((CONV_ROWS, 2 * D_FF))],
        out_specs=[_row_spec(tm, D), _row_spec(tm, 2 * D_FF), _row_spec(tm, 2 * D_FF), _row_spec(tm, D_FF)],
        out_shape=[jax.ShapeDtypeStruct((S, D), MXU_DT), jax.ShapeDtypeStruct((S, 2 * D_FF), MXU_DT),
                   jax.ShapeDtypeStruct((S, 2 * D_FF), MXU_DT), jax.ShapeDtypeStruct((S, D_FF), MXU_DT)],
        scratch_shapes=[pltpu.VMEM((HALO, 2 * D_FF), F32)],
        compiler_params=_params("arbitrary"),
    )(x, g, w, cwb)


def _gate_up_bwd(z, zc, cwb, w, wd, x, g, dres):
    S, D = x.shape
    tm = TM
    nb = S // tm

    def body(z_ref, zc_ref, zcn_ref, c_ref, w_ref, wd_ref, x_ref, g_ref, r_ref, rn_ref,
             dz_ref, dc_ref, dx_ref, dg_ref):
        i = pl.program_id(0)
        first = i == 0
        last = i == nb - 1
        dxe = jnp.concatenate([r_ref[...], jnp.where(last, 0.0, rn_ref[...])], axis=0).astype(MXU_DT)

        @pl.when(first)
        def _():
            dc_ref[...] = jnp.zeros_like(dc_ref)
            dg_ref[...] = jnp.zeros_like(dg_ref)

        rid = lax.broadcasted_iota(jnp.int32, (CONV_ROWS, TN_FF), 0)

        def conv_out(cols):
            return jnp.concatenate([zc_ref[:, cols].astype(F32), zcn_ref[:, cols].astype(F32)], axis=0)

        def conv_bwd(d, z0, c):
            d0 = d[:tm]
            d1 = _shift_rows(d, -1, tm, 0)
            d2 = _shift_rows(d, -2, tm, 0)
            sums = [jnp.sum(d2 * z0, axis=0, keepdims=True), jnp.sum(d1 * z0, axis=0, keepdims=True),
                    jnp.sum(d0 * z0, axis=0, keepdims=True), jnp.sum(d0, axis=0, keepdims=True)]
            dtaps = jnp.zeros((CONV_ROWS, TN_FF), F32)
            for k, v in enumerate(sums):
                dtaps = jnp.where(rid == k, v, dtaps)
            return d0 * c[2:3] + d1 * c[1:2] + d2 * c[0:1], dtaps

        dh = jnp.zeros((tm, D), F32)
        for j in range(NB_FF):
            gc = slice(j * TN_FF, (j + 1) * TN_FF)
            uc = slice(D_FF + j * TN_FF, D_FF + (j + 1) * TN_FF)
            gt = conv_out(gc)
            ut = conv_out(uc)
            df = _dot_nt(dxe, wd_ref[gc, :])
            sg = jax.nn.sigmoid(gt)
            dzg, dtg = conv_bwd(df * ut * (sg * (1.0 + gt * (1.0 - sg))), z_ref[:, gc].astype(F32), c_ref[:, gc])
            dzu, dtu = conv_bwd(df * (gt * sg), z_ref[:, uc].astype(F32), c_ref[:, uc])
            dzg = dzg.astype(dz_ref.dtype)
            dzu = dzu.astype(dz_ref.dtype)
            dz_ref[:, gc] = dzg
            dz_ref[:, uc] = dzu
            dc_ref[:, gc] += dtg
            dc_ref[:, uc] += dtu
            dh += _dot_nt(dzg, w_ref[:, gc]) + _dot_nt(dzu, w_ref[:, uc])

        xv = x_ref[...]
        r = lax.rsqrt(jnp.mean(xv * xv, axis=-1, keepdims=True) + EPS)
        xhat = xv * r
        dg_ref[...] += jnp.sum(dh * xhat, axis=0, keepdims=True)
        dxh = dh * g_ref[...]
        dx_ref[...] = r_ref[...] + r * (dxh - xhat * jnp.mean(dxh * xhat, axis=-1, keepdims=True))

    hb = tm // HALO
    last_halo = S // HALO - 1
    return _call(
        body, name="gate_up_bwd", grid=(nb,),
        in_specs=[_row_spec(tm, 2 * D_FF), _row_spec(tm, 2 * D_FF),
                  pl.BlockSpec((HALO, 2 * D_FF), lambda i: (jnp.minimum((i + 1) * hb, last_halo), 0)),
                  _full_spec((CONV_ROWS, 2 * D_FF)),
                  pl.BlockSpec((D, 2 * D_FF), lambda i: (0, 0), pipeline_mode=pl.Buffered(1)),
                  pl.BlockSpec((D_FF, D), lambda i: (0, 0), pipeline_mode=pl.Buffered(1)),
                  _row_spec(tm, D), _full_spec((1, D)), _row_spec(tm, D),
                  pl.BlockSpec((HALO, D), lambda i: (jnp.minimum((i + 1) * hb, last_halo), 0))],
        out_specs=[_row_spec(tm, 2 * D_FF), _full_spec((CONV_ROWS, 2 * D_FF)), _row_spec(tm, D), _full_spec((1, D))],
        out_shape=[jax.ShapeDtypeStruct((S, 2 * D_FF), MXU_DT), jax.ShapeDtypeStruct((CONV_ROWS, 2 * D_FF), F32),
                   jax.ShapeDtypeStruct((S, D), F32), jax.ShapeDtypeStruct((1, D), F32)],
        compiler_params=_params("arbitrary"),
    )(z, zc, zc, cwb, w, wd, x, g, dres, dres)


def _layer_consts(w, l):
    wm = _tril_weights(w["sgu_w"][l])
    eye = jnp.eye(4, dtype=F32)
    wbd = (w["pool_w"][l][:, :, None, :] * eye[:, None, :, None]).reshape(W_C, W_C)
    cwb = jnp.concatenate([w["conv_w"][l], w["conv_b"][l][None], jnp.zeros((CONV_ROWS - 4, 2 * D_FF), F32)], axis=0)
    return dict(
        g1=w["norm1_g"][l][None], g2=w["norm2_g"][l][None], gm=w["mix_norm_g"][l][None],
        sng=w["sgu_norm_g"][l][None], wm=wm.astype(MXU_DT), wmt=jnp.swapaxes(wm, 1, 2).astype(MXU_DT),
        bias=jnp.repeat(jnp.transpose(w["sgu_b"][l]), HEAD_DIM, axis=1),
        wbd=wbd.astype(MXU_DT), scale=w["pool_scale"][l][None], cwb=cwb,
    )


def _local_step(x, tgt, w, late=None, early_exchange=None):
    gmat = _group_matrix()
    tri = _tri_matrix()
    trit = jnp.transpose(tri)
    saved = []
    early = None
    big = {n: [w[n][l] for l in range(DEPTH)] for n in BIG_NAMES[:4]}
    for l in range(DEPTH):
        c = _layer_consts(w, l)
        if l == 0 and late is not None:
            assert DEPTH == 2
            shards = late["shards"]
            h1, proj, proj_b, ya, yc, *gathered = _in_proj_groups(
                x, c["g1"], big["w_in"][l], c["sng"], c["wm"], c["bias"], gmat, c["wbd"], c["scale"],
                carry_gather=shards[1:])
            for name, arr in late["assemble"](gathered, shards[1:], 0, BIG_NAMES[1:4]).items():
                big[name][0] = arr
            yb, rb, *gathered = _sb_fwd(proj_b, tri, carry_gather=shards)
            for name, arr in late["assemble"](gathered, shards, 1, BIG_NAMES[:4]).items():
                big[name][1] = arr
        else:
            h1, proj, proj_b, ya, yc = _in_proj_groups(x, c["g1"], big["w_in"][l], c["sng"], c["wm"], c["bias"],
                                                       gmat, c["wbd"], c["scale"])
            yb, rb = _sb_fwd(proj_b, tri)
        x2, yn = _mix_out(ya, yb, yc, c["gm"], big["w_o"][l], x, gmat)
        h2, z, zc, f = _up_proj_gate(x2, c["g2"], big["w_up"][l], c["cwb"])
        saved.append(dict(c=c, x=x, proj=proj, proj_b=proj_b, h1=h1, ya=ya, yb=yb, yc=yc, rb=rb, x2=x2, yn=yn,
                          z=z, zc=zc, h2=h2, f=f))
        if l < DEPTH - 1:
            x = _mm_res(f, big["w_down"][l], x2, "down_proj")
    for l in range(DEPTH):
        saved[l]["c"] = dict(saved[l]["c"], **{n: big[n][l] for n in BIG_NAMES[:4]})

    last = saved[-1]
    dx, d_final_g, loss8 = _down_proj_loss(last["f"], last["c"]["w_down"], last["x2"], w["final_g"][None], tgt)
    grads = {n: [None] * DEPTH for n in ("norm1_g", "w_in", "sgu_norm_g", "sgu_w", "sgu_b", "pool_w", "pool_scale",
                                         "mix_norm_g", "w_o", "norm2_g", "w_up", "conv_w", "conv_b", "w_down")}
    for l in reversed(range(DEPTH)):
        s = saved[l]
        c = s["c"]
        grads["w_down"][l] = _mm_tn(s["f"], dx, "down_proj_wgrad").reshape(N_CHIPS, D_FF // N_CHIPS, D_MODEL)
        dz, dcwb, dx2, dg2 = _gate_up_bwd(s["z"], s["zc"], c["cwb"], c["w_up"], c["w_down"], s["x2"], c["g2"], dx)
        grads["conv_w"][l] = dcwb[:3]
        grads["conv_b"][l] = dcwb[3]
        grads["w_up"][l] = _mm_tn(s["h2"], dz, "up_proj_wgrad", col_tiles=True)
        grads["norm2_g"][l] = dg2[0]
        grads["w_o"][l] = _mm_tn(s["yn"], dx2, "out_proj_wgrad").reshape(N_CHIPS, D_MODEL // N_CHIPS, D_MODEL)
        if l == 0 and early_exchange is not None:
            early_items = early_exchange[0](grads)
            dya, dyb, dyc, dgm, *swapped = _mix_out_bwd(dx2, c["w_o"], s["ya"], s["yb"], s["yc"], c["gm"], gmat,
                                                        carry_swap=early_items)
        else:
            dya, dyb, dyc, dgm = _mix_out_bwd(dx2, c["w_o"], s["ya"], s["yb"], s["yc"], c["gm"], gmat)
        grads["mix_norm_g"][l] = dgm[0]
        dd, e, dwbd, dscale = _pool_bwd_a(s["proj"], dyc, c["wbd"], c["scale"])
        dp = _pool_bwd_b(dd, e)
        grads["pool_w"][l] = jnp.stack([dwbd[g * 64:(g + 1) * 64, g * 64:(g + 1) * 64] for g in range(4)])
        grads["pool_scale"][l] = dscale[0]
        if l == 0 and early_exchange is not None:
            sent = early_exchange[1](early_items, swapped)
            dq, dk, dv, *parts = _sb_bwd(s["proj_b"], dyb, s["rb"], tri, trit, carry_exchange=sent)
            early = (sent, parts)
        else:
            dq, dk, dv = _sb_bwd(s["proj_b"], dyb, s["rb"], tri, trit)
        da, dwm, dbias, dsng = _sgu_bwd(s["proj"], dya, c["sng"], c["wm"], c["wmt"], c["bias"], gmat)
        grads["sgu_w"][l] = dwm
        grads["sgu_b"][l] = jnp.transpose(jnp.sum(dbias.reshape(CHUNK, 4, HEAD_DIM), axis=-1))
        grads["sgu_norm_g"][l] = dsng[0]
        dproj = jnp.concatenate([da, dq, dk.astype(MXU_DT), dv.astype(MXU_DT), dp], axis=1)
        dw_in = _mm_tn(s["h1"], dproj, "in_proj_wgrad")
        grads["w_in"][l] = jnp.transpose(dw_in.reshape(D_MODEL, N_CHIPS, IN_COLS // N_CHIPS), (1, 0, 2))
        dx, dg1 = _mm_nt_rmsbwd([(dproj, c["w_in"])], s["x"], c["g1"], dx2, "in_proj_bwd")
        grads["norm1_g"][l] = dg1[0]

    out = {n: (v if n in BIG_NAMES[:4] else jnp.stack(v)) for n, v in grads.items()}
    out["final_g"] = d_final_g[0]
    return loss8[0, 0], dx, out, early


MESH = pl.DeviceIdType.MESH
ANY = pl.BlockSpec(memory_space=pl.ANY)


def _gather_plan(ins, outs, send_sems, recv_sems, layer, sender):
    n = len(ins)
    x, y, c = lax.axis_index("x"), lax.axis_index("y"), lax.axis_index("c")
    sibling = (x, y, 1 - c)
    my_chip = 2 * x + y
    chips = [(1 - x, y), (x, 1 - y), (1 - x, 1 - y)]
    ids = [2 * px + py for px, py in chips]

    def copy(a, k, chip, to, own=False):
        dst = outs[a].at[chip]
        return pltpu.make_async_remote_copy(
            src_ref=ins[a].at[layer] if own else dst, dst_ref=dst,
            send_sem=send_sems.at[a, k], recv_sem=recv_sems.at[a, k], device_id=to, device_id_type=MESH)

    sends = [copy(a, j, my_chip, (*chips[j], sender), own=True) for j in range(3) for a in range(n)]
    arrivals = [copy(a, j, ids[j], sibling) for j in range(3) for a in range(n)]
    forwards = [copy(a, 3 + j, ids[j], sibling) for j in range(3) for a in range(n)]
    return c, sends, arrivals, forwards


def _gather_shapes(shards):
    n = len(shards)
    return ([jax.ShapeDtypeStruct((N_CHIPS,) + s.shape[1:], s.dtype) for s in shards],
            [pltpu.SemaphoreType.DMA((n, 6)), pltpu.SemaphoreType.DMA((n, 6))])


def _all_gather(shards, layer, sender):
    n = len(shards)

    def body(*refs):
        c, sends, arrivals, forwards = _gather_plan(refs[:n], refs[n:2 * n], refs[2 * n], refs[2 * n + 1],
                                                    layer, sender)

        @pl.when(c == sender)
        def _():
            for cp in sends:
                cp.start()
            for arrived, onward in zip(arrivals, forwards):
                arrived.wait_recv()
                onward.start()
            for cp in sends + forwards:
                cp.wait_send()

        @pl.when(c != sender)
        def _():
            for cp in forwards:
                cp.wait_recv()

    out_shape, sems = _gather_shapes(shards)
    return _call(body, name="weight_all_gather", out_shape=out_shape, in_specs=[ANY] * n, out_specs=[ANY] * n,
                 scratch_shapes=sems)(*shards)


def _row_tile(r):
    return r if r <= 704 else 256


def _grad_swap(items, name):
    n = len(items)

    def body(*refs):
        start, finish = _swap_plan(refs[:n], refs[n:2 * n], refs[2 * n:3 * n], refs[3 * n], refs[3 * n + 1])
        start()
        finish()

    out_shape, sems = _swap_shapes(items)
    return _call(body, name=name, out_shape=out_shape, in_specs=[ANY] * (2 * n), out_specs=[ANY] * n,
                 scratch_shapes=sems)(*[a0 for a0, _ in items], *[a1 for _, a1 in items])


def _swap_plan(firsts, seconds, outs, send_sems, recv_sems):
    n = len(firsts)
    x, y, c = lax.axis_index("x"), lax.axis_index("y"), lax.axis_index("c")

    def copies(srcs):
        return [pltpu.make_async_remote_copy(src_ref=srcs[a], dst_ref=outs[a], send_sem=send_sems.at[a],
                                             recv_sem=recv_sems.at[a], device_id=(x, y, 1 - c),
                                             device_id_type=MESH) for a in range(n)]

    def start():
        @pl.when(c == 0)
        def _():
            for cp in copies(seconds):
                cp.start()

        @pl.when(c == 1)
        def _():
            for cp in copies(firsts):
                cp.start()

    def finish():
        for cp in copies(firsts):
            cp.wait()

    return start, finish


def _swap_shapes(items):
    n = len(items)
    return ([jax.ShapeDtypeStruct(a0.shape, a0.dtype) for a0, _ in items],
            [pltpu.SemaphoreType.DMA((n,)), pltpu.SemaphoreType.DMA((n,))])


def _pair_add(a0, a1, r, c_arr, name, out_dtype):
    k, rr, cc = r.shape
    tr = _row_tile(rr)

    def body(c_ref, a0_ref, a1_ref, r_ref, o_ref):
        mine = jnp.where(c_ref[0] == 0, a0_ref[...], a1_ref[...])
        o_ref[...] = (mine + r_ref[...]).astype(o_ref.dtype)

    def member(which):
        def index(kk, i, c_ref):
            used = (c_ref[0] == which).astype(jnp.int32)
            return (kk * used, i * used, 0)
        return pl.BlockSpec((1, tr, cc), index)

    spec = pl.BlockSpec((1, tr, cc), lambda kk, i, c_ref: (kk, i, 0))
    grid_spec = pltpu.PrefetchScalarGridSpec(num_scalar_prefetch=1, grid=(k, rr // tr),
                                             in_specs=[member(0), member(1), spec], out_specs=spec)
    return _call(body, name=name, grid_spec=grid_spec, out_shape=jax.ShapeDtypeStruct((k, rr, cc), out_dtype),
                 compiler_params=_params("parallel", "parallel"))(c_arr, a0, a1, r)


def _exchange_plan(ins, outs, send_sems, recv_sems):
    n = len(ins)
    x, y, c = lax.axis_index("x"), lax.axis_index("y"), lax.axis_index("c")
    my_chip = 2 * x + y
    chips = [(1 - x, y), (x, 1 - y), (1 - x, 1 - y)]

    def copy(a, k, src_chip, dst_chip):
        px, py = chips[k]
        return pltpu.make_async_remote_copy(
            src_ref=ins[a].at[src_chip], dst_ref=outs[a].at[dst_chip], send_sem=send_sems.at[a, k],
            recv_sem=recv_sems.at[a, k], device_id=(px, py, c), device_id_type=MESH)

    sends = [copy(a, k, 2 * chips[k][0] + chips[k][1], my_chip) for k in range(3) for a in range(n)]
    arrivals = [copy(a, k, my_chip, 2 * chips[k][0] + chips[k][1]) for k in range(3) for a in range(n)]
    return sends, arrivals


def _exchange_shapes(hs):
    n = len(hs)
    return ([jax.ShapeDtypeStruct(h.shape, h.dtype) for h in hs],
            [pltpu.SemaphoreType.DMA((n, 3)), pltpu.SemaphoreType.DMA((n, 3))])


def _grad_exchange(hs):
    n = len(hs)

    def body(*refs):
        sends, arrivals = _exchange_plan(refs[:n], refs[n:2 * n], refs[2 * n], refs[2 * n + 1])
        for cp in sends:
            cp.start()
        for cp in arrivals:
            cp.wait_recv()
        for cp in sends:
            cp.wait_send()

    out_shape, sems = _exchange_shapes(hs)
    return _call(body, name="grad_exchange_chips", out_shape=out_shape, in_specs=[ANY] * n, out_specs=[ANY] * n,
                 scratch_shapes=sems)(*hs)


def _sum_chips(a, c_arr, name):
    _, r, cc = a.shape
    tr = _row_tile(r)

    def body(c_ref, a_ref, o_ref):
        o_ref[...] = ((a_ref[0].astype(F32) + a_ref[1].astype(F32)) + a_ref[2].astype(F32)) + a_ref[3].astype(F32)

    grid_spec = pltpu.PrefetchScalarGridSpec(
        num_scalar_prefetch=1, grid=(r // tr,),
        in_specs=[pl.BlockSpec((N_CHIPS, tr, cc), lambda i, c_ref: (0, i, 0))],
        out_specs=pl.BlockSpec((None, tr, cc), lambda i, c_ref: (c_ref[0], i, 0)))
    return _call(body, name=name, grid_spec=grid_spec, out_shape=jax.ShapeDtypeStruct((2, r, cc), F32),
                 compiler_params=_params("parallel"))(c_arr, a)


def _grad_share(bufs):
    n = len(bufs)

    def body(*refs):
        outs = refs[n:2 * n]
        send_sems, recv_sems = refs[2 * n:]
        x, y, c = lax.axis_index("x"), lax.axis_index("y"), lax.axis_index("c")
        copies = [pltpu.make_async_remote_copy(src_ref=outs[a].at[c], dst_ref=outs[a].at[c], send_sem=send_sems.at[a],
                                               recv_sem=recv_sems.at[a], device_id=(x, y, 1 - c),
                                               device_id_type=MESH) for a in range(n)]
        for cp in copies:
            cp.start()
        for a in range(n):
            pltpu.make_async_remote_copy(src_ref=outs[a].at[c], dst_ref=outs[a].at[1 - c], send_sem=send_sems.at[a],
                                         recv_sem=recv_sems.at[a], device_id=(x, y, 1 - c),
                                         device_id_type=MESH).wait_recv()
        for cp in copies:
            cp.wait_send()

    return _call(
        body, name="grad_share_cores", out_shape=[jax.ShapeDtypeStruct(b.shape, b.dtype) for b in bufs],
        in_specs=[ANY] * n, out_specs=[ANY] * n, input_output_aliases={a: a for a in range(n)},
        scratch_shapes=[pltpu.SemaphoreType.DMA((n,)), pltpu.SemaphoreType.DMA((n,))],
    )(*bufs)


def _adamw_math(g_ref, w_ref, m_ref, v_ref, d_ref, nm_ref, nv_ref):
    gv = g_ref[...]
    nm = ADAM_B1 * m_ref[...] + (1.0 - ADAM_B1) * gv
    nv = ADAM_B2 * v_ref[...] + (1.0 - ADAM_B2) * (gv * gv)
    m_hat = nm / (1.0 - ADAM_B1 ** ADAM_STEP)
    v_hat = nv / (1.0 - ADAM_B2 ** ADAM_STEP)
    d_ref[...] = -ADAM_LR * (m_hat / (jnp.sqrt(v_hat) + ADAM_EPS) + ADAM_WD * w_ref[...])
    nm_ref[...] = nm
    nv_ref[...] = nv


def _adamw_big(g, w, m, v, name):
    d, r, c = g.shape
    tr = r if r <= 704 else 256
    spec = pl.BlockSpec((1, tr, c), lambda l, i: (l, i, 0))

    def body(*refs):
        _adamw_math(*refs)

    shp = jax.ShapeDtypeStruct(g.shape, F32)
    return _call(body, name=name, grid=(d, r // tr), in_specs=[spec] * 4, out_specs=[spec] * 3,
                 out_shape=[shp, shp, shp], compiler_params=_params("parallel", "parallel"))(g, w, m, v)


def _adamw_small(gs, ws, ms, vs):
    n = len(gs)

    def body(*refs):
        ins, outs = refs[:4 * n], refs[4 * n:]
        for k in range(n):
            _adamw_math(ins[k], ins[n + k], ins[2 * n + k], ins[3 * n + k], outs[k], outs[n + k], outs[2 * n + k])

    shp = [jax.ShapeDtypeStruct(g.shape, F32) for g in gs]
    res = _call(body, name="adamw_small", out_shape=shp * 3)(*gs, *ws, *ms, *vs)
    return res[:n], res[n:2 * n], res[2 * n:]


def _rows(a, rows):
    flat = a.reshape(-1)
    return jnp.pad(flat, (0, rows * D_MODEL - flat.shape[0])).reshape(rows, D_MODEL)


def _small_rows(p, extra=None):
    parts = [p[n].reshape(-1) for n in SMALL_NAMES]
    if extra is not None:
        parts.append(extra.reshape(-1))
    flat = jnp.concatenate(parts)
    return jnp.pad(flat, (0, ROWS_SMALL * D_MODEL - flat.shape[0])).reshape(ROWS_SMALL, D_MODEL)


CONV_SHARD = (DEPTH, 3, 2 * D_FF // N_CHIPS)
N_CONV_SHARD = DEPTH * 3 * (2 * D_FF // N_CHIPS)


def _small_pack(g, loss):
    conv = jnp.transpose(g["conv_w"].reshape(DEPTH, 3, N_CHIPS, 2 * D_FF // N_CHIPS), (2, 0, 1, 3))
    conv = jnp.stack([_rows(conv[k], ROWS_CONV) for k in range(N_CHIPS)])
    small = jnp.broadcast_to(_small_rows(g, loss), (N_CHIPS, ROWS_SMALL, D_MODEL))
    return jnp.concatenate([conv, small], axis=1)


def _unpack_small(pack):
    out = {"conv_w": pack[:ROWS_CONV].reshape(-1)[:N_CONV_SHARD].reshape(CONV_SHARD)}
    flat = pack[ROWS_CONV:].reshape(-1)
    k = 0
    for name in SMALL_NAMES:
        shape = SMALL_SHAPES[name]
        n = 1
        for d in shape:
            n *= d
        out[name] = flat[k:k + n].reshape(shape)
        k += n
    out["extra"] = flat[k]
    return out


def _assemble_layer(gathered, shards, layer, names):
    my_chip = 2 * lax.axis_index("x") + lax.axis_index("y")
    out = {}
    for name, got, own in zip(names, gathered, shards):
        full = lax.dynamic_update_index_in_dim(got, own[layer], my_chip, 0)
        if name in ("w_in", "w_up"):
            k, r, wd = full.shape
            out[name] = jnp.transpose(full, (1, 0, 2)).reshape(r, k * wd)
        else:
            out[name] = full.reshape(-1, D_MODEL)
    return out


def _gather_weights(p):
    shards = [p[n].astype(jnp.bfloat16) for n in BIG_NAMES[:4]]
    conv_all = p["conv_w"].reshape(1, -1, p["conv_w"].shape[-1])
    got = _all_gather([shards[0], conv_all], 0, 0)
    my_chip = 2 * lax.axis_index("x") + lax.axis_index("y")
    conv = lax.dynamic_update_index_in_dim(got[1], conv_all[0], my_chip, 0)
    conv = jnp.transpose(conv.reshape((N_CHIPS,) + CONV_SHARD), (1, 2, 0, 3)).reshape(DEPTH, 3, 2 * D_FF)
    full = {n: [None, None] for n in BIG_NAMES[:4]}
    full["w_in"][0] = _assemble_layer(got[:1], shards[:1], 0, ("w_in",))["w_in"]
    full["conv_w"] = conv
    return full, dict(shards=shards, assemble=_assemble_layer)


def _halves(a):
    r = a.shape[1] // 2
    return a[:, :r], a[:, r:]


def _reduce_begin(items, names, dtypes, c_arr, tag):
    return _pair_adds(items, _grad_swap(items, "grad_swap_cores_" + tag), names, dtypes, c_arr)


def _pair_adds(items, got, names, dtypes, c_arr):
    return [_pair_add(a0, a1, r, c_arr, "grad_add_cores_" + nm, dt)
            for (a0, a1), r, nm, dt in zip(items, got, names, dtypes)]


def _reduce_end(parts, sent, names, c_arr):
    my_chip = 2 * lax.axis_index("x") + lax.axis_index("y")
    full = [lax.dynamic_update_index_in_dim(p, lax.dynamic_index_in_dim(own, my_chip, 0, keepdims=False), my_chip, 0)
            for p, own in zip(parts, sent)]
    return [_sum_chips(f, c_arr, "grad_sum_chips_" + nm) for f, nm in zip(full, names)]


EARLY_NAMES = ("w_o", "w_up", "w_down", "w_in_1")


def _early_items(grads):
    return [tuple(grads[n]) for n in ("w_o", "w_up", "w_down")] + [_halves(grads["w_in"][1])]


def kernel(x, norm1_g, w_in, sgu_norm_g, sgu_w, sgu_b, pool_w, pool_scale, mix_norm_g, w_o, norm2_g, w_up, conv_w, conv_b, w_down, final_g, loss_target, m_norm1_g, m_w_in, m_sgu_norm_g, m_sgu_w, m_sgu_b, m_pool_w, m_pool_scale, m_mix_norm_g, m_w_o, m_norm2_g, m_w_up, m_conv_w, m_conv_b, m_w_down, m_final_g, v_norm1_g, v_w_in, v_sgu_norm_g, v_sgu_w, v_sgu_b, v_pool_w, v_pool_scale, v_mix_norm_g, v_w_o, v_norm2_g, v_w_up, v_conv_w, v_conv_b, v_w_down, v_final_g):
    names = ("norm1_g", "w_in", "sgu_norm_g", "sgu_w", "sgu_b", "pool_w", "pool_scale", "mix_norm_g", "w_o",
             "norm2_g", "w_up", "conv_w", "conv_b", "w_down", "final_g")
    p = dict(zip(names, (norm1_g, w_in, sgu_norm_g, sgu_w, sgu_b, pool_w, pool_scale, mix_norm_g, w_o, norm2_g,
                         w_up, conv_w, conv_b, w_down, final_g)))
    pm = dict(zip(names, (m_norm1_g, m_w_in, m_sgu_norm_g, m_sgu_w, m_sgu_b, m_pool_w, m_pool_scale, m_mix_norm_g,
                          m_w_o, m_norm2_g, m_w_up, m_conv_w, m_conv_b, m_w_down, m_final_g)))
    pv = dict(zip(names, (v_norm1_g, v_w_in, v_sgu_norm_g, v_sgu_w, v_sgu_b, v_pool_w, v_pool_scale, v_mix_norm_g,
                          v_w_o, v_norm2_g, v_w_up, v_conv_w, v_conv_b, v_w_down, v_final_g)))
    c = lax.axis_index("c")
    gathered, late = _gather_weights(p)
    full = dict(p)
    full.update(gathered)

    c_arr = jnp.reshape(c, (1,)).astype(jnp.int32)
    early_types = [ICI_DT] * len(EARLY_NAMES)
    loss, dx, grads, (sent, received) = _local_step(
        x[0], loss_target[0], full, late,
        (_early_items, lambda items, swapped: _pair_adds(items, swapped, EARLY_NAMES, early_types, c_arr)))
    early_sums = _reduce_end(received, sent, EARLY_NAMES, c_arr)
    small_pack = _small_pack(grads, loss)
    late_names = ("w_in_0", "small")
    late_sent = _reduce_begin([_halves(grads["w_in"][0]), _halves(small_pack)], late_names, [ICI_DT, F32], c_arr, "late")
    late_sums = _reduce_end(_grad_exchange(late_sent), late_sent, late_names, c_arr)
    r_o, r_up, r_down, r_in1, r_in0, r_small = _grad_share(early_sums + late_sums)
    g = dict(w_o=r_o, w_up=r_up, w_down=r_down,
             w_in=jnp.stack([r_in0.reshape(D_MODEL, -1), r_in1.reshape(D_MODEL, -1)]))
    g.update(_unpack_small(r_small.reshape(2 * SP_HALF, D_MODEL)))
    d, nm, nv = {}, {}, {}
    for n in BIG_NAMES:
        d[n], nm[n], nv[n] = _adamw_big(g[n], p[n], pm[n], pv[n], "adamw_" + n)

    def two_d(a):
        return a.reshape(1, -1) if a.ndim == 1 else a

    ds, ms, vs = _adamw_small([two_d(g[n]) for n in SMALL_NAMES], [two_d(p[n]) for n in SMALL_NAMES],
                              [two_d(pm[n]) for n in SMALL_NAMES], [two_d(pv[n]) for n in SMALL_NAMES])
    for k, n in enumerate(SMALL_NAMES):
        d[n], nm[n], nv[n] = (a.reshape(p[n].shape) for a in (ds[k], ms[k], vs[k]))
    return (g["extra"], dx[None], *[g[n] for n in names], *[d[n] for n in names], *[nm[n] for n in names],
            *[nv[n] for n in names])
```

```python
import functools

import jax
import jax.numpy as jnp
from jax import lax
from jax.experimental import pallas as pl
from jax.experimental.pallas import tpu as pltpu

F32 = jnp.float32
MXU_DT = jnp.bfloat16

D_MODEL = 1024
DEPTH = 2
HEAD_DIM = 64
W_A = 256
W_B = 512
W_C = 256
IN_COLS = 2 * W_A + 3 * W_B + W_C
CHUNK = 128
POOL_WINDOWS = (2, 4, 8, 16)
D_FF = 2816
EPS = 1e-6
N_CHIPS = 4

ADAM_LR = 0.001
ADAM_B1 = 0.9
ADAM_B2 = 0.999
ADAM_EPS = 1e-08
ADAM_WD = 0.01
ADAM_STEP = 10

LANES = 128
TQ = 256
TK = 256
TM = 256
TM_MM = 512
HALO = 16
VMEM_LIMIT = 56 * 1024 * 1024

ROWS_CONV = 16
ROWS_SMALL = 240
SP_HALF = (ROWS_CONV + ROWS_SMALL) // 2
ICI_DT = jnp.bfloat16

BIG_NAMES = ("w_in", "w_o", "w_up", "w_down", "conv_w")
SMALL_NAMES = ("norm1_g", "sgu_norm_g", "sgu_w", "sgu_b", "pool_w", "pool_scale",
               "mix_norm_g", "norm2_g", "conv_b", "final_g")
SMALL_SHAPES = {
    "norm1_g": (DEPTH, D_MODEL), "sgu_norm_g": (DEPTH, W_A), "sgu_w": (DEPTH, 4, CHUNK, CHUNK),
    "sgu_b": (DEPTH, 4, CHUNK), "pool_w": (DEPTH, 4, 64, 64), "pool_scale": (DEPTH, W_C),
    "mix_norm_g": (DEPTH, D_MODEL), "norm2_g": (DEPTH, D_MODEL), "conv_b": (DEPTH, 2 * D_FF),
    "final_g": (D_MODEL,),
}


def _call(body, **kw):
    return pl.pallas_call(body, **kw)


def _params(*sem):
    return pltpu.CompilerParams(dimension_semantics=sem, vmem_limit_bytes=VMEM_LIMIT)


def _dot(a, b):
    return jnp.dot(a, b, preferred_element_type=F32)


def _dot_nt(a, b):
    return lax.dot_general(a, b, (((1,), (1,)), ((), ())), preferred_element_type=F32)


def _dot_tn(a, b):
    return lax.dot_general(a, b, (((0,), (0,)), ((), ())), preferred_element_type=F32)


def _group_mean(sq, gmat):
    sqb = sq.astype(MXU_DT)
    cols = [_dot(sqb[:, b * LANES:(b + 1) * LANES], gmat) for b in range(sq.shape[1] // LANES)]
    return cols[0] if len(cols) == 1 else jnp.concatenate(cols, axis=-1)


def _group_matrix():
    r = jnp.arange(LANES)
    return jnp.where((r[:, None] // HEAD_DIM) == (r[None, :] // HEAD_DIM), 1.0 / HEAD_DIM, 0.0).astype(MXU_DT)


def _tile(n):
    return max(t for t in range(LANES, 1536 + 1, LANES) if n % t == 0)


def _row_spec(tm, cols, col_block=0):
    return pl.BlockSpec((tm, cols), lambda i, cb=col_block: (i, cb))


def _full_spec(shape):
    nd = len(shape)
    return pl.BlockSpec(shape, lambda *_: (0,) * nd)


def _mm_res(a, w, res, name):
    S, K = a.shape
    N = w.shape[1]
    tm = TM_MM

    def body(a_ref, w_ref, r_ref, o_ref):
        o_ref[...] = r_ref[...] + _dot(a_ref[...], w_ref[...])

    return _call(
        body, name=name, grid=(S // tm,),
        in_specs=[_row_spec(tm, K), _full_spec((K, N)), _row_spec(tm, N)],
        out_specs=_row_spec(tm, N),
        out_shape=jax.ShapeDtypeStruct((S, N), F32),
        compiler_params=_params("parallel"),
    )(a, w, res)


def _mm_tn(a, b, name, col_tiles=False):
    S, K1 = a.shape
    N = b.shape[1]
    ts = 4 * TM_MM
    tk = _tile(K1)
    tn = _tile(N)
    if col_tiles:
        out_spec = pl.BlockSpec((None, tk, tn), lambda m, n, s: (n, m, 0))
        out_shape = jax.ShapeDtypeStruct((N // tn, K1, tn), F32)
    else:
        out_spec = pl.BlockSpec((tk, tn), lambda m, n, s: (m, n))
        out_shape = jax.ShapeDtypeStruct((K1, N), F32)

    def body(a_ref, b_ref, o_ref):
        @pl.when(pl.program_id(2) == 0)
        def _():
            o_ref[...] = jnp.zeros_like(o_ref)

        o_ref[...] += _dot_tn(a_ref[...], b_ref[...].astype(MXU_DT))

    return _call(
        body, name=name, grid=(K1 // tk, N // tn, S // ts),
        in_specs=[pl.BlockSpec((ts, tk), lambda m, n, s: (s, m)),
                  pl.BlockSpec((ts, tn), lambda m, n, s: (s, n))],
        out_specs=out_spec, out_shape=out_shape,
        compiler_params=_params("parallel", "parallel", "arbitrary"),
    )(a, b)


def _mm_nt_rmsbwd(pairs, x, g, dres, name):
    S, D = x.shape
    tm = TM_MM
    n = len(pairs)

    def body(*refs):
        a_refs = refs[:n]
        w_refs = refs[n:2 * n]
        x_ref, g_ref, r_ref, dx_ref, dg_ref = refs[2 * n:]
        dh = _dot_nt(a_refs[0][...], w_refs[0][...])
        for k in range(1, n):
            dh += _dot_nt(a_refs[k][...], w_refs[k][...])
        xv = x_ref[...]
        r = lax.rsqrt(jnp.mean(xv * xv, axis=-1, keepdims=True) + EPS)
        xhat = xv * r

        @pl.when(pl.program_id(0) == 0)
        def _():
            dg_ref[...] = jnp.zeros_like(dg_ref)

        dg_ref[...] += jnp.sum(dh * xhat, axis=0, keepdims=True)
        dxh = dh * g_ref[...]
        dx_ref[...] = r_ref[...] + r * (dxh - xhat * jnp.mean(dxh * xhat, axis=-1, keepdims=True))

    in_specs = ([_row_spec(tm, a.shape[1]) for a, _ in pairs] + [_full_spec(w.shape) for _, w in pairs]
                + [_row_spec(tm, D), _full_spec((1, D)), _row_spec(tm, D)])
    return _call(
        body, name=name, grid=(S // tm,), in_specs=in_specs,
        out_specs=[_row_spec(tm, D), _full_spec((1, D))],
        out_shape=[jax.ShapeDtypeStruct((S, D), F32), jax.ShapeDtypeStruct((1, D), F32)],
        compiler_params=_params("arbitrary"),
    )(*[a for a, _ in pairs], *[w for _, w in pairs], x, g, dres)


def _down_proj_loss(a, w, res, g, tgt):
    S, D = res.shape
    K = a.shape[1]
    tm = TM

    def body(a_ref, w_ref, res_ref, g_ref, t_ref, dx_ref, dg_ref, l_ref):
        xv = res_ref[...] + _dot(a_ref[...], w_ref[...])
        r = lax.rsqrt(jnp.mean(xv * xv, axis=-1, keepdims=True) + EPS)
        xhat = xv * r
        diff = xhat * g_ref[...] - t_ref[...]

        @pl.when(pl.program_id(0) == 0)
        def _():
            dg_ref[...] = jnp.zeros_like(dg_ref)
            l_ref[...] = jnp.zeros_like(l_ref)

        l_ref[...] += jnp.full(l_ref.shape, 0.5 * jnp.sum(jnp.mean(diff * diff, axis=-1, keepdims=True)), F32)
        dout = diff * (1.0 / D)
        dg_ref[...] += jnp.sum(dout * xhat, axis=0, keepdims=True)
        dxh = dout * g_ref[...]
        dx_ref[...] = r * (dxh - xhat * jnp.mean(dxh * xhat, axis=-1, keepdims=True))

    return _call(
        body, name="down_proj_loss", grid=(S // tm,),
        in_specs=[_row_spec(tm, K), _full_spec((K, D)), _row_spec(tm, D), _full_spec((1, D)), _row_spec(tm, D)],
        out_specs=[_row_spec(tm, D), _full_spec((1, D)), _full_spec((8, LANES))],
        out_shape=[jax.ShapeDtypeStruct((S, D), F32), jax.ShapeDtypeStruct((1, D), F32),
                   jax.ShapeDtypeStruct((8, LANES), F32)],
        compiler_params=_params("arbitrary"),
    )(a, w, res, g, tgt)


def _mix_out(ya, yb, yc, gm, wo, x, gmat):
    S = x.shape[0]
    tm = TM

    def body(ya_ref, yb_ref, yc_ref, gm_ref, wo_ref, x_ref, gmat_ref, x2_ref, yn_ref):
        y = jnp.concatenate([ya_ref[...], yb_ref[...], yc_ref[...]], axis=-1)
        r = lax.rsqrt(_group_mean(y * y, gmat_ref[...]) + EPS)
        yn = (y * r * gm_ref[...]).astype(MXU_DT)
        yn_ref[...] = yn
        x2_ref[...] = x_ref[...] + _dot(yn, wo_ref[...])

    return _call(
        body, name="mix_out", grid=(S // tm,),
        in_specs=[_row_spec(tm, W_A), _row_spec(tm, W_B), _row_spec(tm, W_C), _full_spec((1, D_MODEL)),
                  _full_spec((D_MODEL, D_MODEL)), _row_spec(tm, D_MODEL), _full_spec((LANES, LANES))],
        out_specs=[_row_spec(tm, D_MODEL), _row_spec(tm, D_MODEL)],
        out_shape=[jax.ShapeDtypeStruct((S, D_MODEL), F32), jax.ShapeDtypeStruct((S, D_MODEL), MXU_DT)],
        compiler_params=_params("parallel"),
    )(ya, yb, yc, gm, wo, x, gmat)


def _mix_out_bwd(dx2, wo, ya, yb, yc, gm, gmat, carry_swap=None):
    S = dx2.shape[0]
    tm = TM
    nb = S // tm
    items = list(carry_swap or [])
    ns = len(items)

    def body(*refs):
        dx2_ref, wo_ref, ya_ref, yb_ref, yc_ref, gm_ref, gmat_ref = refs[:7]
        dya_ref, dyb_ref, dyc_ref, dgm_ref = refs[7 + 2 * ns:11 + 2 * ns]
        if ns:
            start, finish = _swap_plan(refs[7:7 + ns], refs[7 + ns:7 + 2 * ns], refs[11 + 2 * ns:11 + 3 * ns],
                                       refs[11 + 3 * ns], refs[12 + 3 * ns])

            @pl.when(pl.program_id(0) == 0)
            def _():
                start()

        dyn = _dot_nt(dx2_ref[...].astype(MXU_DT), wo_ref[...])
        y = jnp.concatenate([ya_ref[...], yb_ref[...], yc_ref[...]], axis=-1)
        r = lax.rsqrt(_group_mean(y * y, gmat_ref[...]) + EPS)
        yhat = y * r

        @pl.when(pl.program_id(0) == 0)
        def _():
            dgm_ref[...] = jnp.zeros_like(dgm_ref)

        dgm_ref[...] += jnp.sum(dyn * yhat, axis=0, keepdims=True)
        dyh = dyn * gm_ref[...]
        dy = r * (dyh - yhat * _group_mean(dyh * yhat, gmat_ref[...]))
        dya_ref[...] = dy[:, :W_A]
        dyb_ref[...] = dy[:, W_A:W_A + W_B]
        dyc_ref[...] = dy[:, W_A + W_B:]

        if ns:
            @pl.when(pl.program_id(0) == nb - 1)
            def _():
                finish()

    swapped, sems = _swap_shapes(items) if ns else ([], [])
    return _call(
        body, name="mix_out_bwd_swap" if ns else "mix_out_bwd", grid=(nb,),
        in_specs=[_row_spec(tm, D_MODEL), _full_spec((D_MODEL, D_MODEL)), _row_spec(tm, W_A), _row_spec(tm, W_B),
                  _row_spec(tm, W_C), _full_spec((1, D_MODEL)), _full_spec((LANES, LANES))] + [ANY] * (2 * ns),
        out_specs=[_row_spec(tm, W_A), _row_spec(tm, W_B), _row_spec(tm, W_C), _full_spec((1, D_MODEL))]
        + [ANY] * ns,
        out_shape=[jax.ShapeDtypeStruct((S, W_A), F32), jax.ShapeDtypeStruct((S, W_B), F32),
                   jax.ShapeDtypeStruct((S, W_C), F32), jax.ShapeDtypeStruct((1, D_MODEL), F32)] + swapped,
        scratch_shapes=sems,
        compiler_params=_params("arbitrary"),
    )(dx2, wo, ya, yb, yc, gm, gmat, *[a0 for a0, _ in items], *[a1 for _, a1 in items])


_SQRT_HALF = 0.7071067811865476
_INV_SQRT_2PI = 0.3989422804014327


def _sgu_common(a, sng, wm_ref, bias, gmat):
    phi = 0.5 * (1.0 + lax.erf(a * _SQRT_HALF))
    ga = a * phi
    u = ga[:, :W_A]
    v = ga[:, W_A:]
    r = lax.rsqrt(_group_mean(v * v, gmat) + EPS)
    vhat = v * r
    vn = (vhat * sng).astype(MXU_DT)
    head = lax.broadcasted_iota(jnp.int32, (CHUNK, W_A), 1) // HEAD_DIM
    rows = []
    for c in range(a.shape[0] // CHUNK):
        vc = vn[c * CHUNK:(c + 1) * CHUNK]
        s = bias
        for h in range(4):
            s = s + jnp.where(head == h, _dot(wm_ref[h], vc), 0.0)
        rows.append(s)
    s = jnp.concatenate(rows, axis=0)
    return phi, u, r, vhat, vn, s


def _tril_weights(sgu_w_l):
    t = jnp.arange(CHUNK)
    return jnp.where((t[None, :] <= t[:, None])[None], sgu_w_l, 0.0)


def _sgu_bwd(proj, dy, sng, wm, wmt, bias, gmat):
    S = proj.shape[0]
    tm = TM

    def body(a_ref, dy_ref, sng_ref, wm_ref, wmt_ref, b_ref, gmat_ref, da_ref, dw_ref, db_ref, dsng_ref):
        a = a_ref[...]
        dy = dy_ref[...]
        gmat = gmat_ref[...]
        sng = sng_ref[...]
        phi, u, r, vhat, vn, s = _sgu_common(a, sng, wm_ref, b_ref[...], gmat)
        du = dy * s
        ds = dy * u

        @pl.when(pl.program_id(0) == 0)
        def _():
            dw_ref[...] = jnp.zeros_like(dw_ref)
            db_ref[...] = jnp.zeros_like(db_ref)
            dsng_ref[...] = jnp.zeros_like(dsng_ref)

        head = lax.broadcasted_iota(jnp.int32, (CHUNK, W_A), 1) // HEAD_DIM
        tt = lax.broadcasted_iota(jnp.int32, (CHUNK, CHUNK), 0)
        ss = lax.broadcasted_iota(jnp.int32, (CHUNK, CHUNK), 1)
        rows = []
        for c in range(tm // CHUNK):
            dsc = ds[c * CHUNK:(c + 1) * CHUNK]
            vc = vn[c * CHUNK:(c + 1) * CHUNK]
            db_ref[...] += dsc
            dsb = dsc.astype(MXU_DT)
            dvn = jnp.zeros((CHUNK, W_A), F32)
            for h in range(4):
                dvn = dvn + jnp.where(head == h, _dot(wmt_ref[h], dsb), 0.0)
                dsh = jnp.where(head == h, dsc, 0.0).astype(MXU_DT)
                dw_ref[h] += jnp.where(ss <= tt, _dot_nt(dsh, vc), 0.0)
            rows.append(dvn)
        dvn = jnp.concatenate(rows, axis=0)
        dsng_ref[...] += jnp.sum(dvn * vhat, axis=0, keepdims=True)
        dvh = dvn * sng
        dv = r * (dvh - vhat * _group_mean(dvh * vhat, gmat))
        dga = jnp.concatenate([du, dv], axis=-1)
        dgelu = phi + a * (_INV_SQRT_2PI * jnp.exp(-0.5 * a * a))
        da_ref[...] = (dga * dgelu).astype(da_ref.dtype)

    return _call(
        body, name="sgu_bwd", grid=(S // tm,),
        in_specs=[_row_spec(tm, 2 * W_A), _row_spec(tm, W_A), _full_spec((1, W_A)), _full_spec((4, CHUNK, CHUNK)),
                  _full_spec((4, CHUNK, CHUNK)), _full_spec((CHUNK, W_A)), _full_spec((LANES, LANES))],
        out_specs=[_row_spec(tm, 2 * W_A), _full_spec((4, CHUNK, CHUNK)), _full_spec((CHUNK, W_A)),
                   _full_spec((1, W_A))],
        out_shape=[jax.ShapeDtypeStruct((S, 2 * W_A), MXU_DT), jax.ShapeDtypeStruct((4, CHUNK, CHUNK), F32),
                   jax.ShapeDtypeStruct((CHUNK, W_A), F32), jax.ShapeDtypeStruct((1, W_A), F32)],
        compiler_params=_params("arbitrary"),
    )(proj, dy, sng, wm, wmt, bias, gmat)


HG = 4
LW = HG * HEAD_DIM
Q_BLK0 = (2 * W_A) // LW
K_BLK0 = Q_BLK0 + W_B // LW
V_BLK0 = K_BLK0 + W_B // LW
N_GROUPS = W_B // LW
EXP_IS_ZERO_BELOW = -120.0


def _tri_matrix():
    r = jnp.arange(TK)
    return (r[:, None] > r[None, :]).astype(MXU_DT)


def _stack_heads(a):
    head = lax.broadcasted_iota(jnp.int32, a.shape, 1) // HEAD_DIM
    return jnp.concatenate([jnp.where(head == h, a, 0.0) for h in range(HG)], axis=0).astype(MXU_DT)


def _unstack_heads(a):
    head = lax.broadcasted_iota(jnp.int32, (TQ, LW), 1) // HEAD_DIM
    out = a[:TQ]
    for h in range(1, HG):
        out = jnp.where(head == h, a[h * TQ:(h + 1) * TQ], out)
    return out


def _sb_scores(q2, kj, tri, key_offset):
    z = _dot_nt(q2, kj)
    sp = jnp.log(1.0 + jnp.exp(-jnp.abs(z)))
    lsp = jnp.minimum(z, 0.0) - sp
    lsm = lsp - z
    msk = None
    if key_offset is not None:
        row = lax.broadcasted_iota(jnp.int32, z.shape, 0) & (TQ - 1)
        col = lax.broadcasted_iota(jnp.int32, z.shape, 1) + key_offset
        msk = col < row
        lsm = jnp.where(msk, lsm, 0.0)
    tail = _dot(lsm.astype(MXU_DT), tri)
    return lsp, lsm, tail, msk


def _sb_fwd(proj_b, tri, carry_gather=None):
    S = proj_b.shape[0]
    nq = S // TQ
    kpq = TQ // TK
    assert S // TK < LANES
    shards = list(carry_gather or [])
    ng = len(shards)

    def body(*refs):
        q_ref, k_ref, v_ref, tri_ref = refs[:4]
        o_ref, rb_ref = refs[4 + ng:6 + ng]
        acc_ref = refs[6 + 2 * ng]
        i = pl.program_id(1)
        if ng:
            step = pl.program_id(0) * nq + i
            sender = 1
            c, sends, arrivals, forwards = _gather_plan(refs[4:4 + ng], refs[6 + ng:6 + 2 * ng], refs[7 + 2 * ng],
                                                        refs[8 + 2 * ng], 1, sender)

            @pl.when(jnp.logical_and(step == 0, c == sender))
            def _():
                for cp in sends:
                    cp.start()

            @pl.when(jnp.logical_and(step == N_GROUPS * nq - max(nq // 8, 1), c == sender))
            def _():
                for arrived, onward in zip(arrivals, forwards):
                    arrived.wait_recv()
                    onward.start()


        lane2 = lax.broadcasted_iota(jnp.int32, (HG * TQ, LANES), 1)
        q2 = _stack_heads(q_ref[...].astype(F32) * (HEAD_DIM ** -0.5))
        tri = tri_ref[...]
        rb_ref[...] = jnp.zeros_like(rb_ref)

        def block(j, run, key_offset=None, first=False):
            start = pl.multiple_of(j * TK, TK)
            kj = k_ref[pl.ds(start, TK), :]
            vj = v_ref[pl.ds(start, TK), :]
            lsp, lsm, tail, msk = _sb_scores(q2, kj, tri, key_offset)
            rb_ref[...] = jnp.where(lane2 == j, run, rb_ref[...])
            att = jnp.exp(lsp + tail + run)
            if msk is not None:
                att = jnp.where(msk, att, 0.0)
            pv = _dot(att.astype(MXU_DT), vj)
            if first:
                acc_ref[...] = pv
            else:
                acc_ref[...] += pv
            return run + tail[:, :1] + lsm[:, :1]

        past = i * kpq

        def overlapping():
            run = jnp.zeros((HG * TQ, 1), F32)
            for d in reversed(range(kpq)):
                run = block(i * kpq + d, run, key_offset=d * TK, first=(d == kpq - 1))
            return run

        def alive(run):
            return (jnp.max(run) > EXP_IS_ZERO_BELOW).astype(jnp.int32)

        def walk(carry):
            n, run, _ = carry
            run = block(past - 1 - n, run)
            return n + 1, run, alive(run)

        n, run = lax.cond(past > 0, lambda: (jnp.int32(1), block(past - 1, overlapping())),
                          lambda: (jnp.int32(0), overlapping()))
        n, _, _ = lax.while_loop(lambda s: jnp.logical_and(s[0] < past, s[2] > 0), walk, (n, run, alive(run)))
        rb_ref[...] = jnp.where(lane2 == LANES - 1, n.astype(F32), rb_ref[...])
        o_ref[...] = _unstack_heads(acc_ref[...])

        if ng:
            @pl.when(jnp.logical_and(step == N_GROUPS * nq - 1, c == sender))
            def _():
                for cp in sends + forwards:
                    cp.wait_send()

            @pl.when(jnp.logical_and(step == N_GROUPS * nq - 1, c != sender))
            def _():
                for cp in forwards:
                    cp.wait_recv()

    once = pl.Buffered(1)
    gathered, sems = _gather_shapes(shards) if ng else ([], [])
    return _call(
        body, name="sb_fwd_gather" if ng else "sb_fwd", grid=(N_GROUPS, nq),
        in_specs=[pl.BlockSpec((TQ, LW), lambda p, i: (i, Q_BLK0 + p)),
                  pl.BlockSpec((S, LW), lambda p, i: (0, K_BLK0 + p), pipeline_mode=once),
                  pl.BlockSpec((S, LW), lambda p, i: (0, V_BLK0 + p), pipeline_mode=once),
                  pl.BlockSpec((TK, TK), lambda p, i: (0, 0))] + [ANY] * ng,
        out_specs=[pl.BlockSpec((TQ, LW), lambda p, i: (i, p)),
                   pl.BlockSpec((None, None, HG * TQ, LANES), lambda p, i: (p, i, 0, 0))] + [ANY] * ng,
        out_shape=[jax.ShapeDtypeStruct((S, W_B), F32),
                   jax.ShapeDtypeStruct((N_GROUPS, nq, HG * TQ, LANES), F32)] + gathered,
        scratch_shapes=[pltpu.VMEM((HG * TQ, LW), F32)] + sems,
        compiler_params=_params("arbitrary", "arbitrary"),
    )(proj_b, proj_b, proj_b, tri, *shards)


def _sb_bwd(proj_b, dyb, rb, tri, trit, carry_exchange=None):
    S = proj_b.shape[0]
    nq = S // TQ
    kpq = TQ // TK
    hs = list(carry_exchange or [])
    ne = len(hs)

    def body(*refs):
        q_ref, k_ref, v_ref, do_ref, rb_ref, tri_ref, trit_ref = refs[:7]
        dq_ref, dk_acc, dv_acc = refs[7 + ne:10 + ne]
        dq_acc = refs[10 + 2 * ne]
        i = pl.program_id(1)
        if ne:
            tick = pl.program_id(0) * nq + i
            sends, arrivals = _exchange_plan(refs[7:7 + ne], refs[10 + ne:10 + 2 * ne], refs[11 + 2 * ne],
                                             refs[12 + 2 * ne])

            @pl.when(tick == 0)
            def _():
                for cp in sends:
                    cp.start()

        lane2 = lax.broadcasted_iota(jnp.int32, (HG * TQ, LANES), 1)
        scale = HEAD_DIM ** -0.5
        q2 = _stack_heads(q_ref[...].astype(F32) * scale)
        do2 = _stack_heads(do_ref[...])
        tri = tri_ref[...]
        trit = trit_ref[...]

        @pl.when(i == 0)
        def _():
            dk_acc[...] = jnp.zeros_like(dk_acc)
            dv_acc[...] = jnp.zeros_like(dv_acc)

        dq_acc[...] = jnp.zeros_like(dq_acc)

        def block(j, pre, key_offset=None):
            start = pl.multiple_of(j * TK, TK)
            kj = k_ref[pl.ds(start, TK), :]
            vj = v_ref[pl.ds(start, TK), :]
            lsp, lsm, tail, msk = _sb_scores(q2, kj, tri, key_offset)
            run = jnp.sum(jnp.where(lane2 == j, rb_ref[...], 0.0), axis=-1, keepdims=True)
            att = jnp.exp(lsp + tail + run)
            if msk is not None:
                att = jnp.where(msk, att, 0.0)
            beta = jnp.exp(lsp)
            dl = _dot_nt(do2, vj) * att
            cin = _dot(dl.astype(MXU_DT), trit)
            dz = dl * (1.0 - beta) - beta * (pre + cin)
            if msk is not None:
                dz = jnp.where(msk, dz, 0.0)
            dzb = dz.astype(MXU_DT)
            dq_acc[...] += _dot(dzb, kj)
            dk_acc[pl.ds(start, TK), :] += _dot_tn(dzb, q2)
            dv_acc[pl.ds(start, TK), :] += _dot_tn(att.astype(MXU_DT), do2)
            return pre + cin[:, TK - 1:] + dl[:, TK - 1:]

        past = i * kpq
        walked = jnp.max(jnp.where(lane2[:8] == LANES - 1, rb_ref[pl.ds(0, 8), :], 0.0)).astype(jnp.int32)
        walked = jnp.clip(walked, 0, past)
        def overlapping(pre):
            for d in range(kpq):
                pre = block(i * kpq + d, pre, key_offset=d * TK)
            return jnp.int32(0)

        def with_past():
            pre = lax.fori_loop(past - walked, past - 1, lambda j, pre: block(j, pre),
                                jnp.zeros((HG * TQ, 1), F32))
            return overlapping(block(past - 1, pre))

        lax.cond(walked > 0, with_past, lambda: overlapping(jnp.zeros((HG * TQ, 1), F32)))
        dq_ref[...] = (_unstack_heads(dq_acc[...]) * scale).astype(dq_ref.dtype)

        if ne:
            @pl.when(tick == N_GROUPS * nq - 1)
            def _():
                for cp in arrivals:
                    cp.wait_recv()
                for cp in sends:
                    cp.wait_send()

    once = pl.Buffered(1)
    exchanged, sems = _exchange_shapes(hs) if ne else ([], [])
    return _call(
        body, name="sb_bwd_exchange" if ne else "sb_bwd", grid=(N_GROUPS, nq),
        in_specs=[pl.BlockSpec((TQ, LW), lambda p, i: (i, Q_BLK0 + p)),
                  pl.BlockSpec((S, LW), lambda p, i: (0, K_BLK0 + p), pipeline_mode=once),
                  pl.BlockSpec((S, LW), lambda p, i: (0, V_BLK0 + p), pipeline_mode=once),
                  pl.BlockSpec((TQ, LW), lambda p, i: (i, p)),
                  pl.BlockSpec((None, None, HG * TQ, LANES), lambda p, i: (p, i, 0, 0)),
                  pl.BlockSpec((TK, TK), lambda p, i: (0, 0)),
                  pl.BlockSpec((TK, TK), lambda p, i: (0, 0))] + [ANY] * ne,
        out_specs=[pl.BlockSpec((TQ, LW), lambda p, i: (i, p)),
                   pl.BlockSpec((S, LW), lambda p, i: (0, p), pipeline_mode=once),
                   pl.BlockSpec((S, LW), lambda p, i: (0, p), pipeline_mode=once)] + [ANY] * ne,
        out_shape=[jax.ShapeDtypeStruct((S, W_B), MXU_DT), jax.ShapeDtypeStruct((S, W_B), F32),
                   jax.ShapeDtypeStruct((S, W_B), F32)] + exchanged,
        scratch_shapes=[pltpu.VMEM((HG * TQ, LW), F32)] + sems,
        compiler_params=_params("arbitrary", "arbitrary"),
    )(proj_b, proj_b, proj_b, dyb, rb, tri, trit, *hs)


P_BLK = (2 * W_A + 3 * W_B) // W_C


def _window_lanes():
    g = lax.broadcasted_iota(jnp.int32, (1, W_C), 1) // (W_C // 4)
    w = jnp.where(g == 0, POOL_WINDOWS[0], jnp.where(g == 1, POOL_WINDOWS[1],
                  jnp.where(g == 2, POOL_WINDOWS[2], POOL_WINDOWS[3])))
    return g, w


def _shift_rows(ext, k, tm, lead):
    n = ext.shape[0]
    return pltpu.roll(ext, shift=k % n, axis=0)[lead:lead + tm]


def _pool_diff(p_cur, p_halo, row0, tm):
    ext = jnp.concatenate([p_halo, p_cur], axis=0)
    g, w = _window_lanes()
    acc = ext
    sums = []
    for sh in (1, 2, 4, 8):
        acc = acc + pltpu.roll(acc, shift=sh, axis=0)
        sums.append(acc[HALO:HALO + tm])
    wsum = jnp.where(g == 0, sums[0], jnp.where(g == 1, sums[1], jnp.where(g == 2, sums[2], sums[3])))
    pos = (row0 + 1 + lax.broadcasted_iota(jnp.int32, (tm, W_C), 0)).astype(F32)
    cnt = jnp.minimum(pos, w.astype(F32))
    return wsum / cnt - p_cur, cnt


def _pool_specs(tm, nrow_blocks_halo):
    cur = pl.BlockSpec((tm, W_C), lambda i: (i, P_BLK))
    prev = pl.BlockSpec((HALO, W_C), lambda i: (jnp.maximum(i * (tm // HALO) - 1, 0), P_BLK))
    return cur, prev


def _in_proj_groups(x, g, w, sng, wm, bias, gmat, wbd, scale, carry_gather=None):
    S, D = x.shape
    tm = TM_MM
    nb = S // tm
    p0 = 2 * W_A + 3 * W_B
    qkv_chunk = 3 * W_B // 2
    shards = list(carry_gather or [])
    ng = len(shards)

    def body(*refs):
        x_ref, g_ref, w_ref, sng_ref, wm_ref, b_ref, gmat_ref, wbd_ref, sc_ref = refs[:9]
        h_ref, o_ref, ob_ref, ya_ref, yc_ref = refs[9 + ng:14 + ng]
        tail_ref = refs[14 + 2 * ng]
        i = pl.program_id(0)
        if ng:
            sender = 0
            c, sends, arrivals, forwards = _gather_plan(refs[9:9 + ng], refs[14 + ng:14 + 2 * ng], refs[15 + 2 * ng],
                                                        refs[16 + 2 * ng], 0, sender)

            @pl.when(jnp.logical_and(i == 0, c == sender))
            def _():
                for cp in sends:
                    cp.start()
        xv = x_ref[...]
        r = lax.rsqrt(jnp.mean(xv * xv, axis=-1, keepdims=True) + EPS)
        h = (xv * r * g_ref[...]).astype(h_ref.dtype)
        h_ref[...] = h

        def project(c0, c1):
            acc = _dot(h, w_ref[:, c0:c1])
            o_ref[:, c0:c1] = acc
            ob_ref[:, c0:c1] = acc.astype(ob_ref.dtype)
            return acc

        a = project(0, 2 * W_A)
        _, u, _, _, _, s = _sgu_common(a, sng_ref[...], wm_ref, b_ref[...], gmat_ref[...])
        ya_ref[...] = u * s
        for c0 in range(2 * W_A, p0, qkv_chunk):
            project(c0, c0 + qkv_chunk)
        p = project(p0, p0 + W_C)
        halo = jnp.where(i > 0, tail_ref[...], 0.0)
        tail_ref[...] = p[tm - HALO:]
        d, _ = _pool_diff(p, halo, i * tm, tm)
        yc_ref[...] = _dot(d.astype(MXU_DT), wbd_ref[...]) * sc_ref[...]

        if ng:
            @pl.when(jnp.logical_and(i == nb - 1, c == sender))
            def _():
                for arrived, onward in zip(arrivals, forwards):
                    arrived.wait_recv()
                    onward.start()
                for cp in sends + forwards:
                    cp.wait_send()

            @pl.when(jnp.logical_and(i == nb - 1, c != sender))
            def _():
                for cp in forwards:
                    cp.wait_recv()

    gathered, sems = _gather_shapes(shards) if ng else ([], [])
    return _call(
        body, name="in_proj_groups_gather" if ng else "in_proj_groups", grid=(nb,),
        in_specs=[_row_spec(tm, D), _full_spec((1, D)),
                  pl.BlockSpec((D, IN_COLS), lambda i: (0, 0), pipeline_mode=pl.Buffered(1)),
                  _full_spec((1, W_A)), _full_spec((4, CHUNK, CHUNK)), _full_spec((CHUNK, W_A)),
                  _full_spec((LANES, LANES)), _full_spec((W_C, W_C)), _full_spec((1, W_C))] + [ANY] * ng,
        out_specs=[_row_spec(tm, D), _row_spec(tm, IN_COLS), _row_spec(tm, IN_COLS), _row_spec(tm, W_A),
                   _row_spec(tm, W_C)] + [ANY] * ng,
        out_shape=[jax.ShapeDtypeStruct((S, D), MXU_DT), jax.ShapeDtypeStruct((S, IN_COLS), F32),
                   jax.ShapeDtypeStruct((S, IN_COLS), MXU_DT), jax.ShapeDtypeStruct((S, W_A), F32),
                   jax.ShapeDtypeStruct((S, W_C), F32)] + gathered,
        scratch_shapes=[pltpu.VMEM((HALO, W_C), F32)] + sems,
        compiler_params=_params("arbitrary"),
    )(x, g, w, sng, wm, bias, gmat, wbd, scale, *shards)


def _pool_bwd_a(proj, dy, wbd, scale):
    S = proj.shape[0]
    tm = TM

    def body(p_ref, ph_ref, dy_ref, w_ref, sc_ref, dd_ref, e_ref, dw_ref, dsc_ref):
        i = pl.program_id(0)
        halo = jnp.where(i > 0, ph_ref[...], 0.0)
        d, cnt = _pool_diff(p_ref[...], halo, i * tm, tm)
        db = d.astype(MXU_DT)
        dy = dy_ref[...]

        @pl.when(i == 0)
        def _():
            dw_ref[...] = jnp.zeros_like(dw_ref)
            dsc_ref[...] = jnp.zeros_like(dsc_ref)

        dsc_ref[...] += jnp.sum(dy * _dot(db, w_ref[...]), axis=0, keepdims=True)
        dys = (dy * sc_ref[...]).astype(MXU_DT)
        dw_ref[...] += _dot_tn(db, dys)
        dd = _dot_nt(dys, w_ref[...])
        dd_ref[...] = dd
        e_ref[...] = dd / cnt

    cur, prev = _pool_specs(tm, S // HALO)
    return _call(
        body, name="pool_bwd_a", grid=(S // tm,),
        in_specs=[cur, prev, _row_spec(tm, W_C), _full_spec((W_C, W_C)), _full_spec((1, W_C))],
        out_specs=[_row_spec(tm, W_C), _row_spec(tm, W_C), _full_spec((W_C, W_C)), _full_spec((1, W_C))],
        out_shape=[jax.ShapeDtypeStruct((S, W_C), F32), jax.ShapeDtypeStruct((S, W_C), F32),
                   jax.ShapeDtypeStruct((W_C, W_C), F32), jax.ShapeDtypeStruct((1, W_C), F32)],
        compiler_params=_params("arbitrary"),
    )(proj, proj, dy, wbd, scale)


def _pool_bwd_b(dd, e):
    S = dd.shape[0]
    tm = TM
    nb = S // tm

    def body(dd_ref, e_ref, en_ref, dp_ref):
        i = pl.program_id(0)
        halo = jnp.where(i < nb - 1, en_ref[...], 0.0)
        ext = jnp.concatenate([e_ref[...], halo], axis=0)
        n = ext.shape[0]
        g, _ = _window_lanes()
        acc = ext
        sums = []
        for sh in (1, 2, 4, 8):
            acc = acc + pltpu.roll(acc, shift=n - sh, axis=0)
            sums.append(acc[:tm])
        wsum = jnp.where(g == 0, sums[0], jnp.where(g == 1, sums[1], jnp.where(g == 2, sums[2], sums[3])))
        dp_ref[...] = (wsum - dd_ref[...]).astype(dp_ref.dtype)

    nxt = pl.BlockSpec((HALO, W_C), lambda i: (jnp.minimum((i + 1) * (tm // HALO), S // HALO - 1), 0))
    return _call(
        body, name="pool_bwd_b", grid=(nb,),
        in_specs=[_row_spec(tm, W_C), _row_spec(tm, W_C), nxt],
        out_specs=_row_spec(tm, W_C),
        out_shape=jax.ShapeDtypeStruct((S, W_C), MXU_DT),
        compiler_params=_params("parallel"),
    )(dd, e, e)


TN_FF = 1408
NB_FF = D_FF // TN_FF
CONV_ROWS = 8


def _conv(z_cur, z_halo, cwb, tm):
    ext = jnp.concatenate([z_halo, z_cur], axis=0)
    z2 = _shift_rows(ext, 2, tm, HALO)
    z1 = _shift_rows(ext, 1, tm, HALO)
    zc = cwb[3:4] + z2 * cwb[0:1] + z1 * cwb[1:2] + z_cur * cwb[2:3]
    return zc, z2, z1


def _up_proj_gate(x, g, w, cwb):
    S, D = x.shape
    tm = TM

    def body(x_ref, g_ref, w_ref, c_ref, h_ref, z_ref, zc_ref, f_ref, tail_ref):
        first = pl.program_id(0) == 0
        xv = x_ref[...]
        r = lax.rsqrt(jnp.mean(xv * xv, axis=-1, keepdims=True) + EPS)
        h = (xv * r * g_ref[...]).astype(h_ref.dtype)
        h_ref[...] = h
        for j in range(NB_FF):
            halves = []
            for col0 in (j * TN_FF, D_FF + j * TN_FF):
                zb = _dot(h, w_ref[:, col0:col0 + TN_FF]).astype(z_ref.dtype)
                z_ref[:, col0:col0 + TN_FF] = zb
                zf = zb.astype(F32)
                prev = jnp.where(first, 0.0, tail_ref[:, col0:col0 + TN_FF])
                tail_ref[:, col0:col0 + TN_FF] = zf[tm - HALO:]
                zc = _conv(zf, prev, c_ref[:, col0:col0 + TN_FF], tm)[0]
                zc_ref[:, col0:col0 + TN_FF] = zc.astype(zc_ref.dtype)
                halves.append(zc)
            gate, value = halves
            f_ref[:, j * TN_FF:(j + 1) * TN_FF] = (gate * jax.nn.sigmoid(gate) * value).astype(f_ref.dtype)

    return _call(
        body, name="up_proj_gate", grid=(S // tm,),
        in_specs=[_row_spec(tm, D), _full_spec((1, D)),
                  pl.BlockSpec((D, 2 * D_FF), lambda i: (0, 0), pipeline_mode=pl.Buffered(1)),
                  _full_spec((CONV_ROWS, 2 * D_FF))],
        out_specs=[_row_spec(tm, D), _row_spec(tm, 2 * D_FF), _row_spec(tm, 2 * D_FF), _row_spec(tm, D_FF)],
        out_shape=[jax.ShapeDtypeStruct((S, D), MXU_DT), jax.ShapeDtypeStruct((S, 2 * D_FF), MXU_DT),
                   jax.ShapeDtypeStruct((S, 2 * D_FF), MXU_DT), jax.ShapeDtypeStruct((S, D_FF), MXU_DT)],
        scratch_shapes=[pltpu.VMEM((HALO, 2 * D_FF), F32)],
        compiler_params=_params("arbitrary"),
    )(x, g, w, cwb)


def _gate_up_bwd(z, zc, cwb, w, wd, x, g, dres):
    S, D = x.shape
    tm = TM
    nb = S // tm

    def body(z_ref, zc_ref, zcn_ref, c_ref, w_ref, wd_ref, x_ref, g_ref, r_ref, rn_ref,
             dz_ref, dc_ref, dx_ref, dg_ref):
        i = pl.program_id(0)
        first = i == 0
        last = i == nb - 1
        dxe = jnp.concatenate([r_ref[...], jnp.where(last, 0.0, rn_ref[...])], axis=0).astype(MXU_DT)

        @pl.when(first)
        def _():
            dc_ref[...] = jnp.zeros_like(dc_ref)
            dg_ref[...] = jnp.zeros_like(dg_ref)

        rid = lax.broadcasted_iota(jnp.int32, (CONV_ROWS, TN_FF), 0)

        def conv_out(cols):
            return jnp.concatenate([zc_ref[:, cols].astype(F32), zcn_ref[:, cols].astype(F32)], axis=0)

        def conv_bwd(d, z0, c):
            d0 = d[:tm]
            d1 = _shift_rows(d, -1, tm, 0)
            d2 = _shift_rows(d, -2, tm, 0)
            sums = [jnp.sum(d2 * z0, axis=0, keepdims=True), jnp.sum(d1 * z0, axis=0, keepdims=True),
                    jnp.sum(d0 * z0, axis=0, keepdims=True), jnp.sum(d0, axis=0, keepdims=True)]
            dtaps = jnp.zeros((CONV_ROWS, TN_FF), F32)
            for k, v in enumerate(sums):
                dtaps = jnp.where(rid == k, v, dtaps)
            return d0 * c[2:3] + d1 * c[1:2] + d2 * c[0:1], dtaps

        dh = jnp.zeros((tm, D), F32)
        for j in range(NB_FF):
            gc = slice(j * TN_FF, (j + 1) * TN_FF)
            uc = slice(D_FF + j * TN_FF, D_FF + (j + 1) * TN_FF)
            gt = conv_out(gc)
            ut = conv_out(uc)
            df = _dot_nt(dxe, wd_ref[gc, :])
            sg = jax.nn.sigmoid(gt)
            dzg, dtg = conv_bwd(df * ut * (sg * (1.0 + gt * (1.0 - sg))), z_ref[:, gc].astype(F32), c_ref[:, gc])
            dzu, dtu = conv_bwd(df * (gt * sg), z_ref[:, uc].astype(F32), c_ref[:, uc])
            dzg = dzg.astype(dz_ref.dtype)
            dzu = dzu.astype(dz_ref.dtype)
            dz_ref[:, gc] = dzg
            dz_ref[:, uc] = dzu
            dc_ref[:, gc] += dtg
            dc_ref[:, uc] += dtu
            dh += _dot_nt(dzg, w_ref[:, gc]) + _dot_nt(dzu, w_ref[:, uc])

        xv = x_ref[...]
        r = lax.rsqrt(jnp.mean(xv * xv, axis=-1, keepdims=True) + EPS)
        xhat = xv * r
        dg_ref[...] += jnp.sum(dh * xhat, axis=0, keepdims=True)
        dxh = dh * g_ref[...]
        dx_ref[...] = r_ref[...] + r * (dxh - xhat * jnp.mean(dxh * xhat, axis=-1, keepdims=True))

    hb = tm // HALO
    last_halo = S // HALO - 1
    return _call(
        body, name="gate_up_bwd", grid=(nb,),
        in_specs=[_row_spec(tm, 2 * D_FF), _row_spec(tm, 2 * D_FF),
                  pl.BlockSpec((HALO, 2 * D_FF), lambda i: (jnp.minimum((i + 1) * hb, last_halo), 0)),
                  _full_spec((CONV_ROWS, 2 * D_FF)),
                  pl.BlockSpec((D, 2 * D_FF), lambda i: (0, 0), pipeline_mode=pl.Buffered(1)),
                  pl.BlockSpec((D_FF, D), lambda i: (0, 0), pipeline_mode=pl.Buffered(1)),
                  _row_spec(tm, D), _full_spec((1, D)), _row_spec(tm, D),
                  pl.BlockSpec((HALO, D), lambda i: (jnp.minimum((i + 1) * hb, last_halo), 0))],
        out_specs=[_row_spec(tm, 2 * D_FF), _full_spec((CONV_ROWS, 2 * D_FF)), _row_spec(tm, D), _full_spec((1, D))],
        out_shape=[jax.ShapeDtypeStruct((S, 2 * D_FF), MXU_DT), jax.ShapeDtypeStruct((CONV_ROWS, 2 * D_FF), F32),
                   jax.ShapeDtypeStruct((S, D), F32), jax.ShapeDtypeStruct((1, D), F32)],
        compiler_params=_params("arbitrary"),
    )(z, zc, zc, cwb, w, wd, x, g, dres, dres)


def _layer_consts(w, l):
    wm = _tril_weights(w["sgu_w"][l])
    eye = jnp.eye(4, dtype=F32)
    wbd = (w["pool_w"][l][:, :, None, :] * eye[:, None, :, None]).reshape(W_C, W_C)
    cwb = jnp.concatenate([w["conv_w"][l], w["conv_b"][l][None], jnp.zeros((CONV_ROWS - 4, 2 * D_FF), F32)], axis=0)
    return dict(
        g1=w["norm1_g"][l][None], g2=w["norm2_g"][l][None], gm=w["mix_norm_g"][l][None],
        sng=w["sgu_norm_g"][l][None], wm=wm.astype(MXU_DT), wmt=jnp.swapaxes(wm, 1, 2).astype(MXU_DT),
        bias=jnp.repeat(jnp.transpose(w["sgu_b"][l]), HEAD_DIM, axis=1),
        wbd=wbd.astype(MXU_DT), scale=w["pool_scale"][l][None], cwb=cwb,
    )


def _local_step(x, tgt, w, late=None, early_exchange=None):
    gmat = _group_matrix()
    tri = _tri_matrix()
    trit = jnp.transpose(tri)
    saved = []
    early = None
    big = {n: [w[n][l] for l in range(DEPTH)] for n in BIG_NAMES[:4]}
    for l in range(DEPTH):
        c = _layer_consts(w, l)
        if l == 0 and late is not None:
            assert DEPTH == 2
            shards = late["shards"]
            h1, proj, proj_b, ya, yc, *gathered = _in_proj_groups(
                x, c["g1"], big["w_in"][l], c["sng"], c["wm"], c["bias"], gmat, c["wbd"], c["scale"],
                carry_gather=shards[1:])
            for name, arr in late["assemble"](gathered, shards[1:], 0, BIG_NAMES[1:4]).items():
                big[name][0] = arr
            yb, rb, *gathered = _sb_fwd(proj_b, tri, carry_gather=shards)
            for name, arr in late["assemble"](gathered, shards, 1, BIG_NAMES[:4]).items():
                big[name][1] = arr
        else:
            h1, proj, proj_b, ya, yc = _in_proj_groups(x, c["g1"], big["w_in"][l], c["sng"], c["wm"], c["bias"],
                                                       gmat, c["wbd"], c["scale"])
            yb, rb = _sb_fwd(proj_b, tri)
        x2, yn = _mix_out(ya, yb, yc, c["gm"], big["w_o"][l], x, gmat)
        h2, z, zc, f = _up_proj_gate(x2, c["g2"], big["w_up"][l], c["cwb"])
        saved.append(dict(c=c, x=x, proj=proj, proj_b=proj_b, h1=h1, ya=ya, yb=yb, yc=yc, rb=rb, x2=x2, yn=yn,
                          z=z, zc=zc, h2=h2, f=f))
        if l < DEPTH - 1:
            x = _mm_res(f, big["w_down"][l], x2, "down_proj")
    for l in range(DEPTH):
        saved[l]["c"] = dict(saved[l]["c"], **{n: big[n][l] for n in BIG_NAMES[:4]})

    last = saved[-1]
    dx, d_final_g, loss8 = _down_proj_loss(last["f"], last["c"]["w_down"], last["x2"], w["final_g"][None], tgt)
    grads = {n: [None] * DEPTH for n in ("norm1_g", "w_in", "sgu_norm_g", "sgu_w", "sgu_b", "pool_w", "pool_scale",
                                         "mix_norm_g", "w_o", "norm2_g", "w_up", "conv_w", "conv_b", "w_down")}
    for l in reversed(range(DEPTH)):
        s = saved[l]
        c = s["c"]
        grads["w_down"][l] = _mm_tn(s["f"], dx, "down_proj_wgrad").reshape(N_CHIPS, D_FF // N_CHIPS, D_MODEL)
        dz, dcwb, dx2, dg2 = _gate_up_bwd(s["z"], s["zc"], c["cwb"], c["w_up"], c["w_down"], s["x2"], c["g2"], dx)
        grads["conv_w"][l] = dcwb[:3]
        grads["conv_b"][l] = dcwb[3]
        grads["w_up"][l] = _mm_tn(s["h2"], dz, "up_proj_wgrad", col_tiles=True)
        grads["norm2_g"][l] = dg2[0]
        grads["w_o"][l] = _mm_tn(s["yn"], dx2, "out_proj_wgrad").reshape(N_CHIPS, D_MODEL // N_CHIPS, D_MODEL)
        if l == 0 and early_exchange is not None:
            early_items = early_exchange[0](grads)
            dya, dyb, dyc, dgm, *swapped = _mix_out_bwd(dx2, c["w_o"], s["ya"], s["yb"], s["yc"], c["gm"], gmat,
                                                        carry_swap=early_items)
        else:
            dya, dyb, dyc, dgm = _mix_out_bwd(dx2, c["w_o"], s["ya"], s["yb"], s["yc"], c["gm"], gmat)
        grads["mix_norm_g"][l] = dgm[0]
        dd, e, dwbd, dscale = _pool_bwd_a(s["proj"], dyc, c["wbd"], c["scale"])
        dp = _pool_bwd_b(dd, e)
        grads["pool_w"][l] = jnp.stack([dwbd[g * 64:(g + 1) * 64, g * 64:(g + 1) * 64] for g in range(4)])
        grads["pool_scale"][l] = dscale[0]
        if l == 0 and early_exchange is not None:
            sent = early_exchange[1](early_items, swapped)
            dq, dk, dv, *parts = _sb_bwd(s["proj_b"], dyb, s["rb"], tri, trit, carry_exchange=sent)
            early = (sent, parts)
        else:
            dq, dk, dv = _sb_bwd(s["proj_b"], dyb, s["rb"], tri, trit)
        da, dwm, dbias, dsng = _sgu_bwd(s["proj"], dya, c["sng"], c["wm"], c["wmt"], c["bias"], gmat)
        grads["sgu_w"][l] = dwm
        grads["sgu_b"][l] = jnp.transpose(jnp.sum(dbias.reshape(CHUNK, 4, HEAD_DIM), axis=-1))
        grads["sgu_norm_g"][l] = dsng[0]
        dproj = jnp.concatenate([da, dq, dk.astype(MXU_DT), dv.astype(MXU_DT), dp], axis=1)
        dw_in = _mm_tn(s["h1"], dproj, "in_proj_wgrad")
        grads["w_in"][l] = jnp.transpose(dw_in.reshape(D_MODEL, N_CHIPS, IN_COLS // N_CHIPS), (1, 0, 2))
        dx, dg1 = _mm_nt_rmsbwd([(dproj, c["w_in"])], s["x"], c["g1"], dx2, "in_proj_bwd")
        grads["norm1_g"][l] = dg1[0]

    out = {n: (v if n in BIG_NAMES[:4] else jnp.stack(v)) for n, v in grads.items()}
    out["final_g"] = d_final_g[0]
    return loss8[0, 0], dx, out, early


MESH = pl.DeviceIdType.MESH
ANY = pl.BlockSpec(memory_space=pl.ANY)


def _gather_plan(ins, outs, send_sems, recv_sems, layer, sender):
    n = len(ins)
    x, y, c = lax.axis_index("x"), lax.axis_index("y"), lax.axis_index("c")
    sibling = (x, y, 1 - c)
    my_chip = 2 * x + y
    chips = [(1 - x, y), (x, 1 - y), (1 - x, 1 - y)]
    ids = [2 * px + py for px, py in chips]

    def copy(a, k, chip, to, own=False):
        dst = outs[a].at[chip]
        return pltpu.make_async_remote_copy(
            src_ref=ins[a].at[layer] if own else dst, dst_ref=dst,
            send_sem=send_sems.at[a, k], recv_sem=recv_sems.at[a, k], device_id=to, device_id_type=MESH)

    sends = [copy(a, j, my_chip, (*chips[j], sender), own=True) for j in range(3) for a in range(n)]
    arrivals = [copy(a, j, ids[j], sibling) for j in range(3) for a in range(n)]
    forwards = [copy(a, 3 + j, ids[j], sibling) for j in range(3) for a in range(n)]
    return c, sends, arrivals, forwards


def _gather_shapes(shards):
    n = len(shards)
    return ([jax.ShapeDtypeStruct((N_CHIPS,) + s.shape[1:], s.dtype) for s in shards],
            [pltpu.SemaphoreType.DMA((n, 6)), pltpu.SemaphoreType.DMA((n, 6))])


def _all_gather(shards, layer, sender):
    n = len(shards)

    def body(*refs):
        c, sends, arrivals, forwards = _gather_plan(refs[:n], refs[n:2 * n], refs[2 * n], refs[2 * n + 1],
                                                    layer, sender)

        @pl.when(c == sender)
        def _():
            for cp in sends:
                cp.start()
            for arrived, onward in zip(arrivals, forwards):
                arrived.wait_recv()
                onward.start()
            for cp in sends + forwards:
                cp.wait_send()

        @pl.when(c != sender)
        def _():
            for cp in forwards:
                cp.wait_recv()

    out_shape, sems = _gather_shapes(shards)
    return _call(body, name="weight_all_gather", out_shape=out_shape, in_specs=[ANY] * n, out_specs=[ANY] * n,
                 scratch_shapes=sems)(*shards)


def _row_tile(r):
    return r if r <= 704 else 256


def _grad_swap(items, name):
    n = len(items)

    def body(*refs):
        start, finish = _swap_plan(refs[:n], refs[n:2 * n], refs[2 * n:3 * n], refs[3 * n], refs[3 * n + 1])
        start()
        finish()

    out_shape, sems = _swap_shapes(items)
    return _call(body, name=name, out_shape=out_shape, in_specs=[ANY] * (2 * n), out_specs=[ANY] * n,
                 scratch_shapes=sems)(*[a0 for a0, _ in items], *[a1 for _, a1 in items])


def _swap_plan(firsts, seconds, outs, send_sems, recv_sems):
    n = len(firsts)
    x, y, c = lax.axis_index("x"), lax.axis_index("y"), lax.axis_index("c")

    def copies(srcs):
        return [pltpu.make_async_remote_copy(src_ref=srcs[a], dst_ref=outs[a], send_sem=send_sems.at[a],
                                             recv_sem=recv_sems.at[a], device_id=(x, y, 1 - c),
                                             device_id_type=MESH) for a in range(n)]

    def start():
        @pl.when(c == 0)
        def _():
            for cp in copies(seconds):
                cp.start()

        @pl.when(c == 1)
        def _():
            for cp in copies(firsts):
                cp.start()

    def finish():
        for cp in copies(firsts):
            cp.wait()

    return start, finish


def _swap_shapes(items):
    n = len(items)
    return ([jax.ShapeDtypeStruct(a0.shape, a0.dtype) for a0, _ in items],
            [pltpu.SemaphoreType.DMA((n,)), pltpu.SemaphoreType.DMA((n,))])


def _pair_add(a0, a1, r, c_arr, name, out_dtype):
    k, rr, cc = r.shape
    tr = _row_tile(rr)

    def body(c_ref, a0_ref, a1_ref, r_ref, o_ref):
        mine = jnp.where(c_ref[0] == 0, a0_ref[...], a1_ref[...])
        o_ref[...] = (mine + r_ref[...]).astype(o_ref.dtype)

    def member(which):
        def index(kk, i, c_ref):
            used = (c_ref[0] == which).astype(jnp.int32)
            return (kk * used, i * used, 0)
        return pl.BlockSpec((1, tr, cc), index)

    spec = pl.BlockSpec((1, tr, cc), lambda kk, i, c_ref: (kk, i, 0))
    grid_spec = pltpu.PrefetchScalarGridSpec(num_scalar_prefetch=1, grid=(k, rr // tr),
                                             in_specs=[member(0), member(1), spec], out_specs=spec)
    return _call(body, name=name, grid_spec=grid_spec, out_shape=jax.ShapeDtypeStruct((k, rr, cc), out_dtype),
                 compiler_params=_params("parallel", "parallel"))(c_arr, a0, a1, r)


def _exchange_plan(ins, outs, send_sems, recv_sems):
    n = len(ins)
    x, y, c = lax.axis_index("x"), lax.axis_index("y"), lax.axis_index("c")
    my_chip = 2 * x + y
    chips = [(1 - x, y), (x, 1 - y), (1 - x, 1 - y)]

    def copy(a, k, src_chip, dst_chip):
        px, py = chips[k]
        return pltpu.make_async_remote_copy(
            src_ref=ins[a].at[src_chip], dst_ref=outs[a].at[dst_chip], send_sem=send_sems.at[a, k],
            recv_sem=recv_sems.at[a, k], device_id=(px, py, c), device_id_type=MESH)

    sends = [copy(a, k, 2 * chips[k][0] + chips[k][1], my_chip) for k in range(3) for a in range(n)]
    arrivals = [copy(a, k, my_chip, 2 * chips[k][0] + chips[k][1]) for k in range(3) for a in range(n)]
    return sends, arrivals


def _exchange_shapes(hs):
    n = len(hs)
    return ([jax.ShapeDtypeStruct(h.shape, h.dtype) for h in hs],
            [pltpu.SemaphoreType.DMA((n, 3)), pltpu.SemaphoreType.DMA((n, 3))])


def _grad_exchange(hs):
    n = len(hs)

    def body(*refs):
        sends, arrivals = _exchange_plan(refs[:n], refs[n:2 * n], refs[2 * n], refs[2 * n + 1])
        for cp in sends:
            cp.start()
        for cp in arrivals:
            cp.wait_recv()
        for cp in sends:
            cp.wait_send()

    out_shape, sems = _exchange_shapes(hs)
    return _call(body, name="grad_exchange_chips", out_shape=out_shape, in_specs=[ANY] * n, out_specs=[ANY] * n,
                 scratch_shapes=sems)(*hs)


def _sum_chips(a, c_arr, name):
    _, r, cc = a.shape
    tr = _row_tile(r)

    def body(c_ref, a_ref, o_ref):
        o_ref[...] = ((a_ref[0].astype(F32) + a_ref[1].astype(F32)) + a_ref[2].astype(F32)) + a_ref[3].astype(F32)

    grid_spec = pltpu.PrefetchScalarGridSpec(
        num_scalar_prefetch=1, grid=(r // tr,),
        in_specs=[pl.BlockSpec((N_CHIPS, tr, cc), lambda i, c_ref: (0, i, 0))],
        out_specs=pl.BlockSpec((None, tr, cc), lambda i, c_ref: (c_ref[0], i, 0)))
    return _call(body, name=name, grid_spec=grid_spec, out_shape=jax.ShapeDtypeStruct((2, r, cc), F32),
                 compiler_params=_params("parallel"))(c_arr, a)


def _grad_share(bufs):
    n = len(bufs)

    def body(*refs):
        outs = refs[n:2 * n]
        send_sems, recv_sems = refs[2 * n:]
        x, y, c = lax.axis_index("x"), lax.axis_index("y"), lax.axis_index("c")
        copies = [pltpu.make_async_remote_copy(src_ref=outs[a].at[c], dst_ref=outs[a].at[c], send_sem=send_sems.at[a],
                                               recv_sem=recv_sems.at[a], device_id=(x, y, 1 - c),
                                               device_id_type=MESH) for a in range(n)]
        for cp in copies:
            cp.start()
        for a in range(n):
            pltpu.make_async_remote_copy(src_ref=outs[a].at[c], dst_ref=outs[a].at[1 - c], send_sem=send_sems.at[a],
                                         recv_sem=recv_sems.at[a], device_id=(x, y, 1 - c),
                                         device_id_type=MESH).wait_recv()
        for cp in copies:
            cp.wait_send()

    return _call(
        body, name="grad_share_cores", out_shape=[jax.ShapeDtypeStruct(b.shape, b.dtype) for b in bufs],
        in_specs=[ANY] * n, out_specs=[ANY] * n, input_output_aliases={a: a for a in range(n)},
        scratch_shapes=[pltpu.SemaphoreType.DMA((n,)), pltpu.SemaphoreType.DMA((n,))],
    )(*bufs)


def _adamw_math(g_ref, w_ref, m_ref, v_ref, d_ref, nm_ref, nv_ref):
    gv = g_ref[...]
    nm = ADAM_B1 * m_ref[...] + (1.0 - ADAM_B1) * gv
    nv = ADAM_B2 * v_ref[...] + (1.0 - ADAM_B2) * (gv * gv)
    m_hat = nm / (1.0 - ADAM_B1 ** ADAM_STEP)
    v_hat = nv / (1.0 - ADAM_B2 ** ADAM_STEP)
    d_ref[...] = -ADAM_LR * (m_hat / (jnp.sqrt(v_hat) + ADAM_EPS) + ADAM_WD * w_ref[...])
    nm_ref[...] = nm
    nv_ref[...] = nv


def _adamw_big(g, w, m, v, name):
    d, r, c = g.shape
    tr = r if r <= 704 else 256
    spec = pl.BlockSpec((1, tr, c), lambda l, i: (l, i, 0))

    def body(*refs):
        _adamw_math(*refs)

    shp = jax.ShapeDtypeStruct(g.shape, F32)
    return _call(body, name=name, grid=(d, r // tr), in_specs=[spec] * 4, out_specs=[spec] * 3,
                 out_shape=[shp, shp, shp], compiler_params=_params("parallel", "parallel"))(g, w, m, v)


def _adamw_small(gs, ws, ms, vs):
    n = len(gs)

    def body(*refs):
        ins, outs = refs[:4 * n], refs[4 * n:]
        for k in range(n):
            _adamw_math(ins[k], ins[n + k], ins[2 * n + k], ins[3 * n + k], outs[k], outs[n + k], outs[2 * n + k])

    shp = [jax.ShapeDtypeStruct(g.shape, F32) for g in gs]
    res = _call(body, name="adamw_small", out_shape=shp * 3)(*gs, *ws, *ms, *vs)
    return res[:n], res[n:2 * n], res[2 * n:]


def _rows(a, rows):
    flat = a.reshape(-1)
    return jnp.pad(flat, (0, rows * D_MODEL - flat.shape[0])).reshape(rows, D_MODEL)


def _small_rows(p, extra=None):
    parts = [p[n].reshape(-1) for n in SMALL_NAMES]
    if extra is not None:
        parts.append(extra.reshape(-1))
    flat = jnp.concatenate(parts)
    return jnp.pad(flat, (0, ROWS_SMALL * D_MODEL - flat.shape[0])).reshape(ROWS_SMALL, D_MODEL)


CONV_SHARD = (DEPTH, 3, 2 * D_FF // N_CHIPS)
N_CONV_SHARD = DEPTH * 3 * (2 * D_FF // N_CHIPS)


def _small_pack(g, loss):
    conv = jnp.transpose(g["conv_w"].reshape(DEPTH, 3, N_CHIPS, 2 * D_FF // N_CHIPS), (2, 0, 1, 3))
    conv = jnp.stack([_rows(conv[k], ROWS_CONV) for k in range(N_CHIPS)])
    small = jnp.broadcast_to(_small_rows(g, loss), (N_CHIPS, ROWS_SMALL, D_MODEL))
    return jnp.concatenate([conv, small], axis=1)


def _unpack_small(pack):
    out = {"conv_w": pack[:ROWS_CONV].reshape(-1)[:N_CONV_SHARD].reshape(CONV_SHARD)}
    flat = pack[ROWS_CONV:].reshape(-1)
    k = 0
    for name in SMALL_NAMES:
        shape = SMALL_SHAPES[name]
        n = 1
        for d in shape:
            n *= d
        out[name] = flat[k:k + n].reshape(shape)
        k += n
    out["extra"] = flat[k]
    return out


def _assemble_layer(gathered, shards, layer, names):
    my_chip = 2 * lax.axis_index("x") + lax.axis_index("y")
    out = {}
    for name, got, own in zip(names, gathered, shards):
        full = lax.dynamic_update_index_in_dim(got, own[layer], my_chip, 0)
        if name in ("w_in", "w_up"):
            k, r, wd = full.shape
            out[name] = jnp.transpose(full, (1, 0, 2)).reshape(r, k * wd)
        else:
            out[name] = full.reshape(-1, D_MODEL)
    return out


def _gather_weights(p):
    shards = [p[n].astype(jnp.bfloat16) for n in BIG_NAMES[:4]]
    conv_all = p["conv_w"].reshape(1, -1, p["conv_w"].shape[-1])
    got = _all_gather([shards[0], conv_all], 0, 0)
    my_chip = 2 * lax.axis_index("x") + lax.axis_index("y")
    conv = lax.dynamic_update_index_in_dim(got[1], conv_all[0], my_chip, 0)
    conv = jnp.transpose(conv.reshape((N_CHIPS,) + CONV_SHARD), (1, 2, 0, 3)).reshape(DEPTH, 3, 2 * D_FF)
    full = {n: [None, None] for n in BIG_NAMES[:4]}
    full["w_in"][0] = _assemble_layer(got[:1], shards[:1], 0, ("w_in",))["w_in"]
    full["conv_w"] = conv
    return full, dict(shards=shards, assemble=_assemble_layer)


def _halves(a):
    r = a.shape[1] // 2
    return a[:, :r], a[:, r:]


def _reduce_begin(items, names, dtypes, c_arr, tag):
    return _pair_adds(items, _grad_swap(items, "grad_swap_cores_" + tag), names, dtypes, c_arr)


def _pair_adds(items, got, names, dtypes, c_arr):
    return [_pair_add(a0, a1, r, c_arr, "grad_add_cores_" + nm, dt)
            for (a0, a1), r, nm, dt in zip(items, got, names, dtypes)]


def _reduce_end(parts, sent, names, c_arr):
    my_chip = 2 * lax.axis_index("x") + lax.axis_index("y")
    full = [lax.dynamic_update_index_in_dim(p, lax.dynamic_index_in_dim(own, my_chip, 0, keepdims=False), my_chip, 0)
            for p, own in zip(parts, sent)]
    return [_sum_chips(f, c_arr, "grad_sum_chips_" + nm) for f, nm in zip(full, names)]


EARLY_NAMES = ("w_o", "w_up", "w_down", "w_in_1")


def _early_items(grads):
    return [tuple(grads[n]) for n in ("w_o", "w_up", "w_down")] + [_halves(grads["w_in"][1])]


def kernel(x, norm1_g, w_in, sgu_norm_g, sgu_w, sgu_b, pool_w, pool_scale, mix_norm_g, w_o, norm2_g, w_up, conv_w, conv_b, w_down, final_g, loss_target, m_norm1_g, m_w_in, m_sgu_norm_g, m_sgu_w, m_sgu_b, m_pool_w, m_pool_scale, m_mix_norm_g, m_w_o, m_norm2_g, m_w_up, m_conv_w, m_conv_b, m_w_down, m_final_g, v_norm1_g, v_w_in, v_sgu_norm_g, v_sgu_w, v_sgu_b, v_pool_w, v_pool_scale, v_mix_norm_g, v_w_o, v_norm2_g, v_w_up, v_conv_w, v_conv_b, v_w_down, v_final_g):
    names = ("norm1_g", "w_in", "sgu_norm_g", "sgu_w", "sgu_b", "pool_w", "pool_scale", "mix_norm_g", "w_o",
             "norm2_g", "w_up", "conv_w", "conv_b", "w_down", "final_g")
    p = dict(zip(names, (norm1_g, w_in, sgu_norm_g, sgu_w, sgu_b, pool_w, pool_scale, mix_norm_g, w_o, norm2_g,
                         w_up, conv_w, conv_b, w_down, final_g)))
    pm = dict(zip(names, (m_norm1_g, m_w_in, m_sgu_norm_g, m_sgu_w, m_sgu_b, m_pool_w, m_pool_scale, m_mix_norm_g,
                          m_w_o, m_norm2_g, m_w_up, m_conv_w, m_conv_b, m_w_down, m_final_g)))
    pv = dict(zip(names, (v_norm1_g, v_w_in, v_sgu_norm_g, v_sgu_w, v_sgu_b, v_pool_w, v_pool_scale, v_mix_norm_g,
                          v_w_o, v_norm2_g, v_w_up, v_conv_w, v_conv_b, v_w_down, v_final_g)))
    c = lax.axis_index("c")
    gathered, late = _gather_weights(p)
    full = dict(p)
    full.update(gathered)

    c_arr = jnp.reshape(c, (1,)).astype(jnp.int32)
    early_types = [ICI_DT] * len(EARLY_NAMES)
    loss, dx, grads, (sent, received) = _local_step(
        x[0], loss_target[0], full, late,
        (_early_items, lambda items, swapped: _pair_adds(items, swapped, EARLY_NAMES, early_types, c_arr)))
    early_sums = _reduce_end(received, sent, EARLY_NAMES, c_arr)
    small_pack = _small_pack(grads, loss)
    late_names = ("w_in_0", "small")
    late_sent = _reduce_begin([_halves(grads["w_in"][0]), _halves(small_pack)], late_names, [ICI_DT, F32], c_arr, "late")
    late_sums = _reduce_end(_grad_exchange(late_sent), late_sent, late_names, c_arr)
    r_o, r_up, r_down, r_in1, r_in0, r_small = _grad_share(early_sums + late_sums)
    g = dict(w_o=r_o, w_up=r_up, w_down=r_down,
             w_in=jnp.stack([r_in0.reshape(D_MODEL, -1), r_in1.reshape(D_MODEL, -1)]))
    g.update(_unpack_small(r_small.reshape(2 * SP_HALF, D_MODEL)))
    d, nm, nv = {}, {}, {}
    for n in BIG_NAMES:
        d[n], nm[n], nv[n] = _adamw_big(g[n], p[n], pm[n], pv[n], "adamw_" + n)

    def two_d(a):
        return a.reshape(1, -1) if a.ndim == 1 else a

    ds, ms, vs = _adamw_small([two_d(g[n]) for n in SMALL_NAMES], [two_d(p[n]) for n in SMALL_NAMES],
                              [two_d(pm[n]) for n in SMALL_NAMES], [two_d(pv[n]) for n in SMALL_NAMES])
    for k, n in enumerate(SMALL_NAMES):
        d[n], nm[n], nv[n] = (a.reshape(p[n].shape) for a in (ds[k], ms[k], vs[k]))
    return (g["extra"], dx[None], *[g[n] for n in names], *[d[n] for n in names], *[nm[n] for n in names],
            *[nv[n] for n in names])
```

```python
import jax
import jax.numpy as jnp
from jax import lax
from jax.experimental import pallas as pl
from jax.experimental.pallas import tpu as pltpu

F32 = jnp.float32
MXU_DT = jnp.bfloat16

D_MODEL = 1024
DEPTH = 2
HEAD_DIM = 64
W_A = 256
W_B = 512
W_C = 256
IN_COLS = 2 * W_A + 3 * W_B + W_C
CHUNK = 128
POOL_WINDOWS = (2, 4, 8, 16)
D_FF = 2816
EPS = 1e-6
N_CHIPS = 4

ADAM_LR = 0.001
ADAM_B1 = 0.9
ADAM_B2 = 0.999
ADAM_EPS = 1e-08
ADAM_WD = 0.01
ADAM_STEP = 10

LANES = 128
TQ = 256
TK = 256
TM = 256
TM_MM = 512
HALO = 16
VMEM_LIMIT = 56 * 1024 * 1024

ROWS_CONV = 16
ROWS_SMALL = 240
SP_HALF = (ROWS_CONV + ROWS_SMALL) // 2
ICI_DT = jnp.bfloat16

BIG_NAMES = ("w_in", "w_o", "w_up", "w_down", "conv_w")
SMALL_NAMES = ("norm1_g", "sgu_norm_g", "sgu_w", "sgu_b", "pool_w", "pool_scale",
               "mix_norm_g", "norm2_g", "conv_b", "final_g")
SMALL_SHAPES = {
    "norm1_g": (DEPTH, D_MODEL), "sgu_norm_g": (DEPTH, W_A), "sgu_w": (DEPTH, 4, CHUNK, CHUNK),
    "sgu_b": (DEPTH, 4, CHUNK), "pool_w": (DEPTH, 4, 64, 64), "pool_scale": (DEPTH, W_C),
    "mix_norm_g": (DEPTH, D_MODEL), "norm2_g": (DEPTH, D_MODEL), "conv_b": (DEPTH, 2 * D_FF),
    "final_g": (D_MODEL,),
}


def _call(body, **kw):
    return pl.pallas_call(body, **kw)


def _params(*sem):
    return pltpu.CompilerParams(dimension_semantics=sem, vmem_limit_bytes=VMEM_LIMIT)


def _dot(a, b):
    return jnp.dot(a, b, preferred_element_type=F32)


def _dot_nt(a, b):
    return lax.dot_general(a, b, (((1,), (1,)), ((), ())), preferred_element_type=F32)


def _dot_tn(a, b):
    return lax.dot_general(a, b, (((0,), (0,)), ((), ())), preferred_element_type=F32)


def _group_mean(sq, gmat):
    sqb = sq.astype(MXU_DT)
    cols = [_dot(sqb[:, b * LANES:(b + 1) * LANES], gmat) for b in range(sq.shape[1] // LANES)]
    return cols[0] if len(cols) == 1 else jnp.concatenate(cols, axis=-1)


def _group_matrix():
    r = jnp.arange(LANES)
    return jnp.where((r[:, None] // HEAD_DIM) == (r[None, :] // HEAD_DIM), 1.0 / HEAD_DIM, 0.0).astype(MXU_DT)


def _tile(n):
    return max(t for t in range(LANES, 1536 + 1, LANES) if n % t == 0)


def _row_spec(tm, cols, col_block=0):
    return pl.BlockSpec((tm, cols), lambda i, cb=col_block: (i, cb))


def _full_spec(shape):
    nd = len(shape)
    return pl.BlockSpec(shape, lambda *_: (0,) * nd)


def _mm_res(a, w, res, name):
    S, K = a.shape
    N = w.shape[1]
    tm = TM_MM

    def body(a_ref, w_ref, r_ref, o_ref):
        o_ref[...] = r_ref[...] + _dot(a_ref[...], w_ref[...])

    return _call(
        body, name=name, grid=(S // tm,),
        in_specs=[_row_spec(tm, K), _full_spec((K, N)), _row_spec(tm, N)],
        out_specs=_row_spec(tm, N),
        out_shape=jax.ShapeDtypeStruct((S, N), F32),
        compiler_params=_params("parallel"),
    )(a, w, res)


def _mm_tn(a, b, name, col_tiles=False):
    S, K1 = a.shape
    N = b.shape[1]
    ts = min(4 * TM_MM, S)
    tk = _tile(K1)
    tn = _tile(N)
    if col_tiles:
        out_spec = pl.BlockSpec((None, tk, tn), lambda m, n, s: (n, m, 0))
        out_shape = jax.ShapeDtypeStruct((N // tn, K1, tn), F32)
    else:
        out_spec = pl.BlockSpec((tk, tn), lambda m, n, s: (m, n))
        out_shape = jax.ShapeDtypeStruct((K1, N), F32)

    def body(a_ref, b_ref, o_ref):
        @pl.when(pl.program_id(2) == 0)
        def _():
            o_ref[...] = jnp.zeros_like(o_ref)

        o_ref[...] += _dot_tn(a_ref[...], b_ref[...].astype(MXU_DT))

    return _call(
        body, name=name, grid=(K1 // tk, N // tn, S // ts),
        in_specs=[pl.BlockSpec((ts, tk), lambda m, n, s: (s, m)),
                  pl.BlockSpec((ts, tn), lambda m, n, s: (s, n))],
        out_specs=out_spec, out_shape=out_shape,
        compiler_params=_params("parallel", "parallel", "arbitrary"),
    )(a, b)


def _down_proj_loss(a, w, res, g, tgt):
    S, D = res.shape
    K = a.shape[1]
    tm = TM

    def body(a_ref, w_ref, res_ref, g_ref, t_ref, dx_ref, dg_ref, l_ref):
        xv = res_ref[...] + _dot(a_ref[...], w_ref[...])
        r = lax.rsqrt(jnp.mean(xv * xv, axis=-1, keepdims=True) + EPS)
        xhat = xv * r
        diff = xhat * g_ref[...] - t_ref[...]

        @pl.when(pl.program_id(0) == 0)
        def _():
            dg_ref[...] = jnp.zeros_like(dg_ref)
            l_ref[...] = jnp.zeros_like(l_ref)

        l_ref[...] += jnp.full(l_ref.shape, 0.5 * jnp.sum(jnp.mean(diff * diff, axis=-1, keepdims=True)), F32)
        dout = diff * (1.0 / D)
        dg_ref[...] += jnp.sum(dout * xhat, axis=0, keepdims=True)
        dxh = dout * g_ref[...]
        dx_ref[...] = r * (dxh - xhat * jnp.mean(dxh * xhat, axis=-1, keepdims=True))

    return _call(
        body, name="down_proj_loss", grid=(S // tm,),
        in_specs=[_row_spec(tm, K), _full_spec((K, D)), _row_spec(tm, D), _full_spec((1, D)), _row_spec(tm, D)],
        out_specs=[_row_spec(tm, D), _full_spec((1, D)), _full_spec((8, LANES))],
        out_shape=[jax.ShapeDtypeStruct((S, D), F32), jax.ShapeDtypeStruct((1, D), F32),
                   jax.ShapeDtypeStruct((8, LANES), F32)],
        compiler_params=_params("arbitrary"),
    )(a, w, res, g, tgt)


def _mix_out(ya, yb, yc, gm, wo, x, gmat):
    S = x.shape[0]
    tm = TM

    def body(ya_ref, yb_ref, yc_ref, gm_ref, wo_ref, x_ref, gmat_ref, x2_ref, yn_ref):
        y = jnp.concatenate([ya_ref[...], yb_ref[...], yc_ref[...]], axis=-1)
        r = lax.rsqrt(_group_mean(y * y, gmat_ref[...]) + EPS)
        yn = (y * r * gm_ref[...]).astype(MXU_DT)
        yn_ref[...] = yn
        x2_ref[...] = x_ref[...] + _dot(yn, wo_ref[...])

    return _call(
        body, name="mix_out", grid=(S // tm,),
        in_specs=[_row_spec(tm, W_A), _row_spec(tm, W_B), _row_spec(tm, W_C), _full_spec((1, D_MODEL)),
                  _full_spec((D_MODEL, D_MODEL)), _row_spec(tm, D_MODEL), _full_spec((LANES, LANES))],
        out_specs=[_row_spec(tm, D_MODEL), _row_spec(tm, D_MODEL)],
        out_shape=[jax.ShapeDtypeStruct((S, D_MODEL), F32), jax.ShapeDtypeStruct((S, D_MODEL), MXU_DT)],
        compiler_params=_params("parallel"),
    )(ya, yb, yc, gm, wo, x, gmat)


def _mix_out_bwd(dx2, wo, ya, yb, yc, gm, gmat, carry_swap=None):
    S = dx2.shape[0]
    tm = TM
    nb = S // tm
    items = list(carry_swap or [])
    ns = len(items)

    def body(*refs):
        dx2_ref, wo_ref, ya_ref, yb_ref, yc_ref, gm_ref, gmat_ref = refs[:7]
        dya_ref, dyb_ref, dyc_ref, dgm_ref = refs[7 + 2 * ns:11 + 2 * ns]
        if ns:
            start, finish = _swap_plan(refs[7:7 + ns], refs[7 + ns:7 + 2 * ns], refs[11 + 2 * ns:11 + 3 * ns],
                                       refs[11 + 3 * ns], refs[12 + 3 * ns])

            @pl.when(pl.program_id(0) == 0)
            def _():
                start()

        dyn = _dot_nt(dx2_ref[...].astype(MXU_DT), wo_ref[...])
        y = jnp.concatenate([ya_ref[...], yb_ref[...], yc_ref[...]], axis=-1)
        r = lax.rsqrt(_group_mean(y * y, gmat_ref[...]) + EPS)
        yhat = y * r

        @pl.when(pl.program_id(0) == 0)
        def _():
            dgm_ref[...] = jnp.zeros_like(dgm_ref)

        dgm_ref[...] += jnp.sum(dyn * yhat, axis=0, keepdims=True)
        dyh = dyn * gm_ref[...]
        dy = r * (dyh - yhat * _group_mean(dyh * yhat, gmat_ref[...]))
        dya_ref[...] = dy[:, :W_A]
        dyb_ref[...] = dy[:, W_A:W_A + W_B]
        dyc_ref[...] = dy[:, W_A + W_B:]

        if ns:
            @pl.when(pl.program_id(0) == nb - 1)
            def _():
                finish()

    swapped, sems = _swap_shapes(items) if ns else ([], [])
    return _call(
        body, name="mix_out_bwd_swap" if ns else "mix_out_bwd", grid=(nb,),
        in_specs=[_row_spec(tm, D_MODEL), _full_spec((D_MODEL, D_MODEL)), _row_spec(tm, W_A), _row_spec(tm, W_B),
                  _row_spec(tm, W_C), _full_spec((1, D_MODEL)), _full_spec((LANES, LANES))] + [ANY] * (2 * ns),
        out_specs=[_row_spec(tm, W_A), _row_spec(tm, W_B), _row_spec(tm, W_C), _full_spec((1, D_MODEL))]
        + [ANY] * ns,
        out_shape=[jax.ShapeDtypeStruct((S, W_A), F32), jax.ShapeDtypeStruct((S, W_B), F32),
                   jax.ShapeDtypeStruct((S, W_C), F32), jax.ShapeDtypeStruct((1, D_MODEL), F32)] + swapped,
        scratch_shapes=sems,
        compiler_params=_params("arbitrary"),
    )(dx2, wo, ya, yb, yc, gm, gmat, *[a0 for a0, _ in items], *[a1 for _, a1 in items])


_SQRT_HALF = 0.7071067811865476
_INV_SQRT_2PI = 0.3989422804014327


def _sgu_common(a, sng, wm_ref, bias, gmat):
    phi = 0.5 * (1.0 + lax.erf(a * _SQRT_HALF))
    ga = a * phi
    u = ga[:, :W_A]
    v = ga[:, W_A:]
    r = lax.rsqrt(_group_mean(v * v, gmat) + EPS)
    vhat = v * r
    vn = (vhat * sng).astype(MXU_DT)
    head = lax.broadcasted_iota(jnp.int32, (CHUNK, W_A), 1) // HEAD_DIM
    rows = []
    for c in range(a.shape[0] // CHUNK):
        vc = vn[c * CHUNK:(c + 1) * CHUNK]
        s = bias
        for h in range(4):
            s = s + jnp.where(head == h, _dot(wm_ref[h], vc), 0.0)
        rows.append(s)
    s = jnp.concatenate(rows, axis=0)
    return phi, u, r, vhat, vn, s


def _tril_weights(sgu_w_l):
    t = jnp.arange(CHUNK)
    return jnp.where((t[None, :] <= t[:, None])[None], sgu_w_l, 0.0)


def _sgu_in_proj_bwd(proj, dy, sng, wm, wmt, bias, gmat, rest, w, x, g, dres):
    S, D = x.shape
    tm = TM_MM

    def body(a_ref, dy_ref, sng_ref, wm_ref, wmt_ref, b_ref, gmat_ref, rest_ref, w_ref, x_ref, g_ref, r_ref,
             da_ref, dw_ref, db_ref, dsng_ref, dx_ref, dg_ref):
        a = a_ref[...]
        dy = dy_ref[...]
        gmat = gmat_ref[...]
        sng = sng_ref[...]
        phi, u, r, vhat, vn, s = _sgu_common(a, sng, wm_ref, b_ref[...], gmat)
        du = dy * s
        ds = dy * u

        @pl.when(pl.program_id(0) == 0)
        def _():
            dw_ref[...] = jnp.zeros_like(dw_ref)
            db_ref[...] = jnp.zeros_like(db_ref)
            dsng_ref[...] = jnp.zeros_like(dsng_ref)
            dg_ref[...] = jnp.zeros_like(dg_ref)

        head = lax.broadcasted_iota(jnp.int32, (CHUNK, W_A), 1) // HEAD_DIM
        tt = lax.broadcasted_iota(jnp.int32, (CHUNK, CHUNK), 0)
        ss = lax.broadcasted_iota(jnp.int32, (CHUNK, CHUNK), 1)
        rows = []
        for c in range(tm // CHUNK):
            dsc = ds[c * CHUNK:(c + 1) * CHUNK]
            vc = vn[c * CHUNK:(c + 1) * CHUNK]
            db_ref[...] += dsc
            dsb = dsc.astype(MXU_DT)
            dvn = jnp.zeros((CHUNK, W_A), F32)
            for h in range(4):
                dvn = dvn + jnp.where(head == h, _dot(wmt_ref[h], dsb), 0.0)
                dsh = jnp.where(head == h, dsc, 0.0).astype(MXU_DT)
                dw_ref[h] += jnp.where(ss <= tt, _dot_nt(dsh, vc), 0.0)
            rows.append(dvn)
        dvn = jnp.concatenate(rows, axis=0)
        dsng_ref[...] += jnp.sum(dvn * vhat, axis=0, keepdims=True)
        dvh = dvn * sng
        dv = r * (dvh - vhat * _group_mean(dvh * vhat, gmat))
        dga = jnp.concatenate([du, dv], axis=-1)
        dgelu = phi + a * (_INV_SQRT_2PI * jnp.exp(-0.5 * a * a))
        dab = (dga * dgelu).astype(da_ref.dtype)
        da_ref[...] = dab

        dh = _dot_nt(dab, w_ref[:, :2 * W_A]) + _dot_nt(rest_ref[...], w_ref[:, 2 * W_A:])
        xv = x_ref[...]
        rx = lax.rsqrt(jnp.mean(xv * xv, axis=-1, keepdims=True) + EPS)
        xhat = xv * rx
        dg_ref[...] += jnp.sum(dh * xhat, axis=0, keepdims=True)
        dxh = dh * g_ref[...]
        dx_ref[...] = r_ref[...] + rx * (dxh - xhat * jnp.mean(dxh * xhat, axis=-1, keepdims=True))

    n_rest = IN_COLS - 2 * W_A
    return _call(
        body, name="sgu_in_proj_bwd", grid=(S // tm,),
        in_specs=[_row_spec(tm, 2 * W_A), _row_spec(tm, W_A), _full_spec((1, W_A)), _full_spec((4, CHUNK, CHUNK)),
                  _full_spec((4, CHUNK, CHUNK)), _full_spec((CHUNK, W_A)), _full_spec((LANES, LANES)),
                  _row_spec(tm, n_rest),
                  pl.BlockSpec((D, IN_COLS), lambda i: (0, 0), pipeline_mode=pl.Buffered(1)),
                  _row_spec(tm, D), _full_spec((1, D)), _row_spec(tm, D)],
        out_specs=[_row_spec(tm, 2 * W_A), _full_spec((4, CHUNK, CHUNK)), _full_spec((CHUNK, W_A)),
                   _full_spec((1, W_A)), _row_spec(tm, D), _full_spec((1, D))],
        out_shape=[jax.ShapeDtypeStruct((S, 2 * W_A), MXU_DT), jax.ShapeDtypeStruct((4, CHUNK, CHUNK), F32),
                   jax.ShapeDtypeStruct((CHUNK, W_A), F32), jax.ShapeDtypeStruct((1, W_A), F32),
                   jax.ShapeDtypeStruct((S, D), F32), jax.ShapeDtypeStruct((1, D), F32)],
        compiler_params=_params("arbitrary"),
    )(proj, dy, sng, wm, wmt, bias, gmat, rest, w, x, g, dres)


HG = 4
LW = HG * HEAD_DIM
Q_BLK0 = (2 * W_A) // LW
K_BLK0 = Q_BLK0 + W_B // LW
V_BLK0 = K_BLK0 + W_B // LW
N_GROUPS = W_B // LW
EXP_IS_ZERO_BELOW = -120.0


def _tri_matrix():
    r = jnp.arange(TK)
    return (r[:, None] > r[None, :]).astype(MXU_DT)


def _stack_heads(a):
    head = lax.broadcasted_iota(jnp.int32, a.shape, 1) // HEAD_DIM
    return jnp.concatenate([jnp.where(head == h, a, 0.0) for h in range(HG)], axis=0).astype(MXU_DT)


def _unstack_heads(a):
    head = lax.broadcasted_iota(jnp.int32, (TQ, LW), 1) // HEAD_DIM
    out = a[:TQ]
    for h in range(1, HG):
        out = jnp.where(head == h, a[h * TQ:(h + 1) * TQ], out)
    return out


def _sb_scores(q2, kj, tri, key_offset):
    z = _dot_nt(q2, kj)
    sp = jnp.log(1.0 + jnp.exp(-jnp.abs(z)))
    lsp = jnp.minimum(z, 0.0) - sp
    lsm = lsp - z
    msk = None
    if key_offset is not None:
        row = lax.broadcasted_iota(jnp.int32, z.shape, 0) & (TQ - 1)
        col = lax.broadcasted_iota(jnp.int32, z.shape, 1) + key_offset
        msk = col < row
        lsm = jnp.where(msk, lsm, 0.0)
    tail = _dot(lsm.astype(MXU_DT), tri)
    return lsp, lsm, tail, msk


def _sb_fwd(proj_b, tri, carry_gather=None):
    S = proj_b.shape[0]
    nq = S // TQ
    kpq = TQ // TK
    assert S // TK < LANES
    shards = list(carry_gather or [])
    ng = len(shards)

    def body(*refs):
        q_ref, k_ref, v_ref, tri_ref = refs[:4]
        o_ref, rb_ref = refs[4 + ng:6 + ng]
        acc_ref = refs[6 + 2 * ng]
        i = pl.program_id(1)
        if ng:
            step = pl.program_id(0) * nq + i
            sender = 1
            c, sends, arrivals, forwards = _gather_plan(refs[4:4 + ng], refs[6 + ng:6 + 2 * ng], refs[7 + 2 * ng],
                                                        refs[8 + 2 * ng], 1, sender)

            @pl.when(jnp.logical_and(step == 0, c == sender))
            def _():
                for cp in sends:
                    cp.start()

            @pl.when(jnp.logical_and(step == N_GROUPS * nq - max(nq // 8, 1), c == sender))
            def _():
                for arrived, onward in zip(arrivals, forwards):
                    arrived.wait_recv()
                    onward.start()


        lane2 = lax.broadcasted_iota(jnp.int32, (HG * TQ, LANES), 1)
        q2 = _stack_heads(q_ref[...].astype(F32) * (HEAD_DIM ** -0.5))
        tri = tri_ref[...]
        rb_ref[...] = jnp.zeros_like(rb_ref)

        def block(j, run, key_offset=None, first=False):
            start = pl.multiple_of(j * TK, TK)
            kj = k_ref[pl.ds(start, TK), :]
            vj = v_ref[pl.ds(start, TK), :]
            lsp, lsm, tail, msk = _sb_scores(q2, kj, tri, key_offset)
            rb_ref[...] = jnp.where(lane2 == j, run, rb_ref[...])
            att = jnp.exp(lsp + tail + run)
            if msk is not None:
                att = jnp.where(msk, att, 0.0)
            pv = _dot(att.astype(MXU_DT), vj)
            if first:
                acc_ref[...] = pv
            else:
                acc_ref[...] += pv
            return run + tail[:, :1] + lsm[:, :1]

        past = i * kpq

        def overlapping():
            run = jnp.zeros((HG * TQ, 1), F32)
            for d in reversed(range(kpq)):
                run = block(i * kpq + d, run, key_offset=d * TK, first=(d == kpq - 1))
            return run

        def alive(run):
            return (jnp.max(run) > EXP_IS_ZERO_BELOW).astype(jnp.int32)

        def walk(carry):
            n, run, _ = carry
            run = block(past - 1 - n, run)
            return n + 1, run, alive(run)

        n, run = lax.cond(past > 0, lambda: (jnp.int32(1), block(past - 1, overlapping())),
                          lambda: (jnp.int32(0), overlapping()))
        n, _, _ = lax.while_loop(lambda s: jnp.logical_and(s[0] < past, s[2] > 0), walk, (n, run, alive(run)))
        rb_ref[...] = jnp.where(lane2 == LANES - 1, n.astype(F32), rb_ref[...])
        o_ref[...] = _unstack_heads(acc_ref[...])

        if ng:
            @pl.when(jnp.logical_and(step == N_GROUPS * nq - 1, c == sender))
            def _():
                for cp in sends + forwards:
                    cp.wait_send()

            @pl.when(jnp.logical_and(step == N_GROUPS * nq - 1, c != sender))
            def _():
                for cp in forwards:
                    cp.wait_recv()

    once = pl.Buffered(1)
    gathered, sems = _gather_shapes(shards) if ng else ([], [])
    return _call(
        body, name="sb_fwd_gather" if ng else "sb_fwd", grid=(N_GROUPS, nq),
        in_specs=[pl.BlockSpec((TQ, LW), lambda p, i: (i, Q_BLK0 + p)),
                  pl.BlockSpec((S, LW), lambda p, i: (0, K_BLK0 + p), pipeline_mode=once),
                  pl.BlockSpec((S, LW), lambda p, i: (0, V_BLK0 + p), pipeline_mode=once),
                  pl.BlockSpec((TK, TK), lambda p, i: (0, 0))] + [ANY] * ng,
        out_specs=[pl.BlockSpec((TQ, LW), lambda p, i: (i, p)),
                   pl.BlockSpec((None, None, HG * TQ, LANES), lambda p, i: (p, i, 0, 0))] + [ANY] * ng,
        out_shape=[jax.ShapeDtypeStruct((S, W_B), F32),
                   jax.ShapeDtypeStruct((N_GROUPS, nq, HG * TQ, LANES), F32)] + gathered,
        scratch_shapes=[pltpu.VMEM((HG * TQ, LW), F32)] + sems,
        compiler_params=_params("arbitrary", "arbitrary"),
    )(proj_b, proj_b, proj_b, tri, *shards)


def _sb_bwd(proj_b, dyb, rb, tri, trit, carry_exchange=None):
    S = proj_b.shape[0]
    nq = S // TQ
    kpq = TQ // TK
    hs = list(carry_exchange or [])
    ne = len(hs)

    def body(*refs):
        q_ref, k_ref, v_ref, do_ref, rb_ref, tri_ref, trit_ref = refs[:7]
        dq_ref, dk_acc, dv_acc = refs[7 + ne:10 + ne]
        dq_acc = refs[10 + 2 * ne]
        i = pl.program_id(1)
        if ne:
            tick = pl.program_id(0) * nq + i
            sends, arrivals = _exchange_plan(refs[7:7 + ne], refs[10 + ne:10 + 2 * ne], refs[11 + 2 * ne],
                                             refs[12 + 2 * ne])

            @pl.when(tick == 0)
            def _():
                for cp in sends:
                    cp.start()

        lane2 = lax.broadcasted_iota(jnp.int32, (HG * TQ, LANES), 1)
        scale = HEAD_DIM ** -0.5
        q2 = _stack_heads(q_ref[...].astype(F32) * scale)
        do2 = _stack_heads(do_ref[...])
        tri = tri_ref[...]
        trit = trit_ref[...]

        @pl.when(i == 0)
        def _():
            dk_acc[...] = jnp.zeros_like(dk_acc)
            dv_acc[...] = jnp.zeros_like(dv_acc)

        dq_acc[...] = jnp.zeros_like(dq_acc)

        def block(j, pre, key_offset=None):
            start = pl.multiple_of(j * TK, TK)
            kj = k_ref[pl.ds(start, TK), :]
            vj = v_ref[pl.ds(start, TK), :]
            lsp, lsm, tail, msk = _sb_scores(q2, kj, tri, key_offset)
            run = jnp.sum(jnp.where(lane2 == j, rb_ref[...], 0.0), axis=-1, keepdims=True)
            att = jnp.exp(lsp + tail + run)
            if msk is not None:
                att = jnp.where(msk, att, 0.0)
            beta = jnp.exp(lsp)
            dl = _dot_nt(do2, vj) * att
            cin = _dot(dl.astype(MXU_DT), trit)
            dz = dl * (1.0 - beta) - beta * (pre + cin)
            if msk is not None:
                dz = jnp.where(msk, dz, 0.0)
            dzb = dz.astype(MXU_DT)
            dq_acc[...] += _dot(dzb, kj)
            dk_acc[pl.ds(start, TK), :] += _dot_tn(dzb, q2)
            dv_acc[pl.ds(start, TK), :] += _dot_tn(att.astype(MXU_DT), do2)
            return pre + cin[:, TK - 1:] + dl[:, TK - 1:]

        past = i * kpq
        walked = jnp.max(jnp.where(lane2[:8] == LANES - 1, rb_ref[pl.ds(0, 8), :], 0.0)).astype(jnp.int32)
        walked = jnp.clip(walked, 0, past)
        def overlapping(pre):
            for d in range(kpq):
                pre = block(i * kpq + d, pre, key_offset=d * TK)
            return jnp.int32(0)

        def with_past():
            pre = lax.fori_loop(past - walked, past - 1, lambda j, pre: block(j, pre),
                                jnp.zeros((HG * TQ, 1), F32))
            return overlapping(block(past - 1, pre))

        lax.cond(walked > 0, with_past, lambda: overlapping(jnp.zeros((HG * TQ, 1), F32)))
        dq_ref[...] = (_unstack_heads(dq_acc[...]) * scale).astype(dq_ref.dtype)

        if ne:
            @pl.when(tick == N_GROUPS * nq - 1)
            def _():
                for cp in arrivals:
                    cp.wait_recv()
                for cp in sends:
                    cp.wait_send()

    once = pl.Buffered(1)
    exchanged, sems = _exchange_shapes(hs) if ne else ([], [])
    return _call(
        body, name="sb_bwd_exchange" if ne else "sb_bwd", grid=(N_GROUPS, nq),
        in_specs=[pl.BlockSpec((TQ, LW), lambda p, i: (i, Q_BLK0 + p)),
                  pl.BlockSpec((S, LW), lambda p, i: (0, K_BLK0 + p), pipeline_mode=once),
                  pl.BlockSpec((S, LW), lambda p, i: (0, V_BLK0 + p), pipeline_mode=once),
                  pl.BlockSpec((TQ, LW), lambda p, i: (i, p)),
                  pl.BlockSpec((None, None, HG * TQ, LANES), lambda p, i: (p, i, 0, 0)),
                  pl.BlockSpec((TK, TK), lambda p, i: (0, 0)),
                  pl.BlockSpec((TK, TK), lambda p, i: (0, 0))] + [ANY] * ne,
        out_specs=[pl.BlockSpec((TQ, LW), lambda p, i: (i, p)),
                   pl.BlockSpec((S, LW), lambda p, i: (0, p), pipeline_mode=once),
                   pl.BlockSpec((S, LW), lambda p, i: (0, p), pipeline_mode=once)] + [ANY] * ne,
        out_shape=[jax.ShapeDtypeStruct((S, W_B), MXU_DT), jax.ShapeDtypeStruct((S, W_B), F32),
                   jax.ShapeDtypeStruct((S, W_B), F32)] + exchanged,
        scratch_shapes=[pltpu.VMEM((HG * TQ, LW), F32)] + sems,
        compiler_params=_params("arbitrary", "arbitrary"),
    )(proj_b, proj_b, proj_b, dyb, rb, tri, trit, *hs)


P_BLK = (2 * W_A + 3 * W_B) // W_C


def _window_lanes():
    g = lax.broadcasted_iota(jnp.int32, (1, W_C), 1) // (W_C // 4)
    w = jnp.where(g == 0, POOL_WINDOWS[0], jnp.where(g == 1, POOL_WINDOWS[1],
                  jnp.where(g == 2, POOL_WINDOWS[2], POOL_WINDOWS[3])))
    return g, w


def _shift_rows(ext, k, tm, lead):
    n = ext.shape[0]
    return pltpu.roll(ext, shift=k % n, axis=0)[lead:lead + tm]


def _pool_diff(p_cur, p_halo, row0, tm):
    ext = jnp.concatenate([p_halo, p_cur], axis=0)
    g, w = _window_lanes()
    acc = ext
    sums = []
    for sh in (1, 2, 4, 8):
        acc = acc + pltpu.roll(acc, shift=sh, axis=0)
        sums.append(acc[HALO:HALO + tm])
    wsum = jnp.where(g == 0, sums[0], jnp.where(g == 1, sums[1], jnp.where(g == 2, sums[2], sums[3])))
    pos = (row0 + 1 + lax.broadcasted_iota(jnp.int32, (tm, W_C), 0)).astype(F32)
    cnt = jnp.minimum(pos, w.astype(F32))
    return wsum / cnt - p_cur, cnt


def _pool_specs(tm, nrow_blocks_halo):
    cur = pl.BlockSpec((tm, W_C), lambda i: (i, P_BLK))
    prev = pl.BlockSpec((HALO, W_C), lambda i: (jnp.maximum(i * (tm // HALO) - 1, 0), P_BLK))
    return cur, prev


def _in_proj_groups(x, g, w, sng, wm, bias, gmat, wbd, scale, carry_gather=None):
    S, D = x.shape
    tm = TM_MM
    nb = S // tm
    p0 = 2 * W_A + 3 * W_B
    qkv_chunk = 3 * W_B // 2
    shards = list(carry_gather or [])
    ng = len(shards)

    def body(*refs):
        x_ref, g_ref, w_ref, sng_ref, wm_ref, b_ref, gmat_ref, wbd_ref, sc_ref = refs[:9]
        h_ref, o_ref, ob_ref, ya_ref, yc_ref = refs[9 + ng:14 + ng]
        tail_ref = refs[14 + 2 * ng]
        i = pl.program_id(0)
        if ng:
            sender = 0
            c, sends, arrivals, forwards = _gather_plan(refs[9:9 + ng], refs[14 + ng:14 + 2 * ng], refs[15 + 2 * ng],
                                                        refs[16 + 2 * ng], 0, sender)

            @pl.when(jnp.logical_and(i == 0, c == sender))
            def _():
                for cp in sends:
                    cp.start()
        xv = x_ref[...]
        r = lax.rsqrt(jnp.mean(xv * xv, axis=-1, keepdims=True) + EPS)
        h = (xv * r * g_ref[...]).astype(h_ref.dtype)
        h_ref[...] = h

        def project(c0, c1):
            acc = _dot(h, w_ref[:, c0:c1])
            o_ref[:, c0:c1] = acc
            ob_ref[:, c0:c1] = acc.astype(ob_ref.dtype)
            return acc

        a = project(0, 2 * W_A)
        _, u, _, _, _, s = _sgu_common(a, sng_ref[...], wm_ref, b_ref[...], gmat_ref[...])
        ya_ref[...] = u * s
        for c0 in range(2 * W_A, p0, qkv_chunk):
            project(c0, c0 + qkv_chunk)
        p = project(p0, p0 + W_C)
        halo = jnp.where(i > 0, tail_ref[...], 0.0)
        tail_ref[...] = p[tm - HALO:]
        d, _ = _pool_diff(p, halo, i * tm, tm)
        yc_ref[...] = _dot(d.astype(MXU_DT), wbd_ref[...]) * sc_ref[...]

        if ng:
            @pl.when(jnp.logical_and(i == nb - 1, c == sender))
            def _():
                for arrived, onward in zip(arrivals, forwards):
                    arrived.wait_recv()
                    onward.start()
                for cp in sends + forwards:
                    cp.wait_send()

            @pl.when(jnp.logical_and(i == nb - 1, c != sender))
            def _():
                for cp in forwards:
                    cp.wait_recv()

    gathered, sems = _gather_shapes(shards) if ng else ([], [])
    return _call(
        body, name="in_proj_groups_gather" if ng else "in_proj_groups", grid=(nb,),
        in_specs=[_row_spec(tm, D), _full_spec((1, D)),
                  pl.BlockSpec((D, IN_COLS), lambda i: (0, 0), pipeline_mode=pl.Buffered(1)),
                  _full_spec((1, W_A)), _full_spec((4, CHUNK, CHUNK)), _full_spec((CHUNK, W_A)),
                  _full_spec((LANES, LANES)), _full_spec((W_C, W_C)), _full_spec((1, W_C))] + [ANY] * ng,
        out_specs=[_row_spec(tm, D), _row_spec(tm, IN_COLS), _row_spec(tm, IN_COLS), _row_spec(tm, W_A),
                   _row_spec(tm, W_C)] + [ANY] * ng,
        out_shape=[jax.ShapeDtypeStruct((S, D), MXU_DT), jax.ShapeDtypeStruct((S, IN_COLS), F32),
                   jax.ShapeDtypeStruct((S, IN_COLS), MXU_DT), jax.ShapeDtypeStruct((S, W_A), F32),
                   jax.ShapeDtypeStruct((S, W_C), F32)] + gathered,
        scratch_shapes=[pltpu.VMEM((HALO, W_C), F32)] + sems,
        compiler_params=_params("arbitrary"),
    )(x, g, w, sng, wm, bias, gmat, wbd, scale, *shards)


def _pool_bwd_a(proj, dy, wbd, scale):
    S = proj.shape[0]
    tm = TM

    def body(p_ref, ph_ref, dy_ref, w_ref, sc_ref, dd_ref, e_ref, dw_ref, dsc_ref):
        i = pl.program_id(0)
        halo = jnp.where(i > 0, ph_ref[...], 0.0)
        d, cnt = _pool_diff(p_ref[...], halo, i * tm, tm)
        db = d.astype(MXU_DT)
        dy = dy_ref[...]

        @pl.when(i == 0)
        def _():
            dw_ref[...] = jnp.zeros_like(dw_ref)
            dsc_ref[...] = jnp.zeros_like(dsc_ref)

        dsc_ref[...] += jnp.sum(dy * _dot(db, w_ref[...]), axis=0, keepdims=True)
        dys = (dy * sc_ref[...]).astype(MXU_DT)
        dw_ref[...] += _dot_tn(db, dys)
        dd = _dot_nt(dys, w_ref[...])
        dd_ref[...] = dd
        e_ref[...] = dd / cnt

    cur, prev = _pool_specs(tm, S // HALO)
    return _call(
        body, name="pool_bwd_a", grid=(S // tm,),
        in_specs=[cur, prev, _row_spec(tm, W_C), _full_spec((W_C, W_C)), _full_spec((1, W_C))],
        out_specs=[_row_spec(tm, W_C), _row_spec(tm, W_C), _full_spec((W_C, W_C)), _full_spec((1, W_C))],
        out_shape=[jax.ShapeDtypeStruct((S, W_C), F32), jax.ShapeDtypeStruct((S, W_C), F32),
                   jax.ShapeDtypeStruct((W_C, W_C), F32), jax.ShapeDtypeStruct((1, W_C), F32)],
        compiler_params=_params("arbitrary"),
    )(proj, proj, dy, wbd, scale)


def _pool_bwd_b(dd, e):
    S = dd.shape[0]
    tm = TM
    nb = S // tm

    def body(dd_ref, e_ref, en_ref, dp_ref):
        i = pl.program_id(0)
        halo = jnp.where(i < nb - 1, en_ref[...], 0.0)
        ext = jnp.concatenate([e_ref[...], halo], axis=0)
        n = ext.shape[0]
        g, _ = _window_lanes()
        acc = ext
        sums = []
        for sh in (1, 2, 4, 8):
            acc = acc + pltpu.roll(acc, shift=n - sh, axis=0)
            sums.append(acc[:tm])
        wsum = jnp.where(g == 0, sums[0], jnp.where(g == 1, sums[1], jnp.where(g == 2, sums[2], sums[3])))
        dp_ref[...] = (wsum - dd_ref[...]).astype(dp_ref.dtype)

    nxt = pl.BlockSpec((HALO, W_C), lambda i: (jnp.minimum((i + 1) * (tm // HALO), S // HALO - 1), 0))
    return _call(
        body, name="pool_bwd_b", grid=(nb,),
        in_specs=[_row_spec(tm, W_C), _row_spec(tm, W_C), nxt],
        out_specs=_row_spec(tm, W_C),
        out_shape=jax.ShapeDtypeStruct((S, W_C), MXU_DT),
        compiler_params=_params("parallel"),
    )(dd, e, e)


TN_FF = 1408
NB_FF = D_FF // TN_FF
CONV_ROWS = 8


def _conv(z_cur, z_halo, cwb, tm):
    ext = jnp.concatenate([z_halo, z_cur], axis=0)
    z2 = _shift_rows(ext, 2, tm, HALO)
    z1 = _shift_rows(ext, 1, tm, HALO)
    zc = cwb[3:4] + z2 * cwb[0:1] + z1 * cwb[1:2] + z_cur * cwb[2:3]
    return zc, z2, z1


def _up_proj_gate(x, g, w, cwb):
    S, D = x.shape
    tm = TM

    def body(x_ref, g_ref, w_ref, c_ref, h_ref, z_ref, zc_ref, f_ref, tail_ref):
        first = pl.program_id(0) == 0
        xv = x_ref[...]
        r = lax.rsqrt(jnp.mean(xv * xv, axis=-1, keepdims=True) + EPS)
        h = (xv * r * g_ref[...]).astype(h_ref.dtype)
        h_ref[...] = h
        for j in range(NB_FF):
            halves = []
            for col0 in (j * TN_FF, D_FF + j * TN_FF):
                zb = _dot(h, w_ref[:, col0:col0 + TN_FF]).astype(z_ref.dtype)
                z_ref[:, col0:col0 + TN_FF] = zb
                zf = zb.astype(F32)
                prev = jnp.where(first, 0.0, tail_ref[:, col0:col0 + TN_FF])
                tail_ref[:, col0:col0 + TN_FF] = zf[tm - HALO:]
                zc = _conv(zf, prev, c_ref[:, col0:col0 + TN_FF], tm)[0]
                zc_ref[:, col0:col0 + TN_FF] = zc.astype(zc_ref.dtype)
                halves.append(zc)
            gate, value = halves
            f_ref[:, j * TN_FF:(j + 1) * TN_FF] = (gate * jax.nn.sigmoid(gate) * value).astype(f_ref.dtype)

    return _call(
        body, name="up_proj_gate", grid=(S // tm,),
        in_specs=[_row_spec(tm, D), _full_spec((1, D)),
                  pl.BlockSpec((D, 2 * D_FF), lambda i: (0, 0), pipeline_mode=pl.Buffered(1)),
                  _full_spec((CONV_ROWS, 2 * D_FF))],
        out_specs=[_row_spec(tm, D), _row_spec(tm, 2 * D_FF), _row_spec(tm, 2 * D_FF), _row_spec(tm, D_FF)],
        out_shape=[jax.ShapeDtypeStruct((S, D), MXU_DT), jax.ShapeDtypeStruct((S, 2 * D_FF), MXU_DT),
                   jax.ShapeDtypeStruct((S, 2 * D_FF), MXU_DT), jax.ShapeDtypeStruct((S, D_FF), MXU_DT)],
        scratch_shapes=[pltpu.VMEM((HALO, 2 * D_FF), F32)],
        compiler_params=_params("arbitrary"),
    )(x, g, w, cwb)


def _gate_up_bwd(z, zc, cwb, w, wd, x, g, dres):
    S, D = x.shape
    tm = TM
    nb = S // tm

    def body(z_ref, zc_ref, zcn_ref, c_ref, w_ref, wd_ref, x_ref, g_ref, r_ref, rn_ref,
             dz_ref, dc_ref, dx_ref, dg_ref):
        i = pl.program_id(0)
        first = i == 0
        last = i == nb - 1
        dxe = jnp.concatenate([r_ref[...], jnp.where(last, 0.0, rn_ref[...])], axis=0).astype(MXU_DT)

        @pl.when(first)
        def _():
            dc_ref[...] = jnp.zeros_like(dc_ref)
            dg_ref[...] = jnp.zeros_like(dg_ref)

        rid = lax.broadcasted_iota(jnp.int32, (CONV_ROWS, TN_FF), 0)

        def conv_out(cols):
            return jnp.concatenate([zc_ref[:, cols].astype(F32), zcn_ref[:, cols].astype(F32)], axis=0)

        def conv_bwd(d, z0, c):
            d0 = d[:tm]
            d1 = _shift_rows(d, -1, tm, 0)
            d2 = _shift_rows(d, -2, tm, 0)
            sums = [jnp.sum(d2 * z0, axis=0, keepdims=True), jnp.sum(d1 * z0, axis=0, keepdims=True),
                    jnp.sum(d0 * z0, axis=0, keepdims=True), jnp.sum(d0, axis=0, keepdims=True)]
            dtaps = jnp.zeros((CONV_ROWS, TN_FF), F32)
            for k, v in enumerate(sums):
                dtaps = jnp.where(rid == k, v, dtaps)
            return d0 * c[2:3] + d1 * c[1:2] + d2 * c[0:1], dtaps

        dh = jnp.zeros((tm, D), F32)
        for j in range(NB_FF):
            gc = slice(j * TN_FF, (j + 1) * TN_FF)
            uc = slice(D_FF + j * TN_FF, D_FF + (j + 1) * TN_FF)
            gt = conv_out(gc)
            ut = conv_out(uc)
            df = _dot_nt(dxe, wd_ref[gc, :])
            sg = jax.nn.sigmoid(gt)
            dzg, dtg = conv_bwd(df * ut * (sg * (1.0 + gt * (1.0 - sg))), z_ref[:, gc].astype(F32), c_ref[:, gc])
            dzu, dtu = conv_bwd(df * (gt * sg), z_ref[:, uc].astype(F32), c_ref[:, uc])
            dzg = dzg.astype(dz_ref.dtype)
            dzu = dzu.astype(dz_ref.dtype)
            dz_ref[:, gc] = dzg
            dz_ref[:, uc] = dzu
            dc_ref[:, gc] += dtg
            dc_ref[:, uc] += dtu
            dh += _dot_nt(dzg, w_ref[:, gc]) + _dot_nt(dzu, w_ref[:, uc])

        xv = x_ref[...]
        r = lax.rsqrt(jnp.mean(xv * xv, axis=-1, keepdims=True) + EPS)
        xhat = xv * r
        dg_ref[...] += jnp.sum(dh * xhat, axis=0, keepdims=True)
        dxh = dh * g_ref[...]
        dx_ref[...] = r_ref[...] + r * (dxh - xhat * jnp.mean(dxh * xhat, axis=-1, keepdims=True))

    hb = tm // HALO
    last_halo = S // HALO - 1
    return _call(
        body, name="gate_up_bwd", grid=(nb,),
        in_specs=[_row_spec(tm, 2 * D_FF), _row_spec(tm, 2 * D_FF),
                  pl.BlockSpec((HALO, 2 * D_FF), lambda i: (jnp.minimum((i + 1) * hb, last_halo), 0)),
                  _full_spec((CONV_ROWS, 2 * D_FF)),
                  pl.BlockSpec((D, 2 * D_FF), lambda i: (0, 0), pipeline_mode=pl.Buffered(1)),
                  pl.BlockSpec((D_FF, D), lambda i: (0, 0), pipeline_mode=pl.Buffered(1)),
                  _row_spec(tm, D), _full_spec((1, D)), _row_spec(tm, D),
                  pl.BlockSpec((HALO, D), lambda i: (jnp.minimum((i + 1) * hb, last_halo), 0))],
        out_specs=[_row_spec(tm, 2 * D_FF), _full_spec((CONV_ROWS, 2 * D_FF)), _row_spec(tm, D), _full_spec((1, D))],
        out_shape=[jax.ShapeDtypeStruct((S, 2 * D_FF), MXU_DT), jax.ShapeDtypeStruct((CONV_ROWS, 2 * D_FF), F32),
                   jax.ShapeDtypeStruct((S, D), F32), jax.ShapeDtypeStruct((1, D), F32)],
        compiler_params=_params("arbitrary"),
    )(z, zc, zc, cwb, w, wd, x, g, dres, dres)


def _layer_consts(w, l):
    wm = _tril_weights(w["sgu_w"][l])
    eye = jnp.eye(4, dtype=F32)
    wbd = (w["pool_w"][l][:, :, None, :] * eye[:, None, :, None]).reshape(W_C, W_C)
    cwb = jnp.concatenate([w["conv_w"][l], w["conv_b"][l][None], jnp.zeros((CONV_ROWS - 4, 2 * D_FF), F32)], axis=0)
    return dict(
        g1=w["norm1_g"][l][None], g2=w["norm2_g"][l][None], gm=w["mix_norm_g"][l][None],
        sng=w["sgu_norm_g"][l][None], wm=wm.astype(MXU_DT), wmt=jnp.swapaxes(wm, 1, 2).astype(MXU_DT),
        bias=jnp.repeat(jnp.transpose(w["sgu_b"][l]), HEAD_DIM, axis=1),
        wbd=wbd.astype(MXU_DT), scale=w["pool_scale"][l][None], cwb=cwb,
    )


def _local_step(x, tgt, w, late=None, early_exchange=None):
    gmat = _group_matrix()
    tri = _tri_matrix()
    trit = jnp.transpose(tri)
    saved = []
    early = None
    big = {n: [w[n][l] for l in range(DEPTH)] for n in BIG_NAMES[:4]}
    for l in range(DEPTH):
        c = _layer_consts(w, l)
        if l == 0 and late is not None:
            assert DEPTH == 2
            shards = late["shards"]
            h1, proj, proj_b, ya, yc, *gathered = _in_proj_groups(
                x, c["g1"], big["w_in"][l], c["sng"], c["wm"], c["bias"], gmat, c["wbd"], c["scale"],
                carry_gather=shards[1:])
            for name, arr in late["assemble"](gathered, shards[1:], 0, BIG_NAMES[1:4]).items():
                big[name][0] = arr
            yb, rb, *gathered = _sb_fwd(proj_b, tri, carry_gather=shards)
            for name, arr in late["assemble"](gathered, shards, 1, BIG_NAMES[:4]).items():
                big[name][1] = arr
        else:
            h1, proj, proj_b, ya, yc = _in_proj_groups(x, c["g1"], big["w_in"][l], c["sng"], c["wm"], c["bias"],
                                                       gmat, c["wbd"], c["scale"])
            yb, rb = _sb_fwd(proj_b, tri)
        x2, yn = _mix_out(ya, yb, yc, c["gm"], big["w_o"][l], x, gmat)
        h2, z, zc, f = _up_proj_gate(x2, c["g2"], big["w_up"][l], c["cwb"])
        saved.append(dict(c=c, x=x, proj=proj, proj_b=proj_b, h1=h1, ya=ya, yb=yb, yc=yc, rb=rb, x2=x2, yn=yn,
                          z=z, zc=zc, h2=h2, f=f))
        if l < DEPTH - 1:
            x = _mm_res(f, big["w_down"][l], x2, "down_proj")
    for l in range(DEPTH):
        saved[l]["c"] = dict(saved[l]["c"], **{n: big[n][l] for n in BIG_NAMES[:4]})

    last = saved[-1]
    dx, d_final_g, loss8 = _down_proj_loss(last["f"], last["c"]["w_down"], last["x2"], w["final_g"][None], tgt)
    grads = {n: [None] * DEPTH for n in ("norm1_g", "w_in", "sgu_norm_g", "sgu_w", "sgu_b", "pool_w", "pool_scale",
                                         "mix_norm_g", "w_o", "norm2_g", "w_up", "conv_w", "conv_b", "w_down")}
    for l in reversed(range(DEPTH)):
        s = saved[l]
        c = s["c"]
        grads["w_down"][l] = _mm_tn(s["f"], dx, "down_proj_wgrad").reshape(N_CHIPS, D_FF // N_CHIPS, D_MODEL)
        dz, dcwb, dx2, dg2 = _gate_up_bwd(s["z"], s["zc"], c["cwb"], c["w_up"], c["w_down"], s["x2"], c["g2"], dx)
        grads["conv_w"][l] = dcwb[:3]
        grads["conv_b"][l] = dcwb[3]
        grads["w_up"][l] = _mm_tn(s["h2"], dz, "up_proj_wgrad", col_tiles=True)
        grads["norm2_g"][l] = dg2[0]
        grads["w_o"][l] = _mm_tn(s["yn"], dx2, "out_proj_wgrad").reshape(N_CHIPS, D_MODEL // N_CHIPS, D_MODEL)
        if l == 0 and early_exchange is not None:
            early_items = early_exchange[0](grads)
            dya, dyb, dyc, dgm, *swapped = _mix_out_bwd(dx2, c["w_o"], s["ya"], s["yb"], s["yc"], c["gm"], gmat,
                                                        carry_swap=early_items)
        else:
            dya, dyb, dyc, dgm = _mix_out_bwd(dx2, c["w_o"], s["ya"], s["yb"], s["yc"], c["gm"], gmat)
        grads["mix_norm_g"][l] = dgm[0]
        dd, e, dwbd, dscale = _pool_bwd_a(s["proj"], dyc, c["wbd"], c["scale"])
        dp = _pool_bwd_b(dd, e)
        grads["pool_w"][l] = jnp.stack([dwbd[g * 64:(g + 1) * 64, g * 64:(g + 1) * 64] for g in range(4)])
        grads["pool_scale"][l] = dscale[0]
        if l == 0 and early_exchange is not None:
            sent = early_exchange[1](early_items, swapped)
            dq, dk, dv, *parts = _sb_bwd(s["proj_b"], dyb, s["rb"], tri, trit, carry_exchange=sent)
            early = (sent, parts)
        else:
            dq, dk, dv = _sb_bwd(s["proj_b"], dyb, s["rb"], tri, trit)
        rest = jnp.concatenate([dq, dk.astype(MXU_DT), dv.astype(MXU_DT), dp], axis=1)
        da, dwm, dbias, dsng, dx, dg1 = _sgu_in_proj_bwd(s["proj"], dya, c["sng"], c["wm"], c["wmt"], c["bias"],
                                                         gmat, rest, c["w_in"], s["x"], c["g1"], dx2)
        grads["sgu_w"][l] = dwm
        grads["sgu_b"][l] = jnp.transpose(jnp.sum(dbias.reshape(CHUNK, 4, HEAD_DIM), axis=-1))
        grads["sgu_norm_g"][l] = dsng[0]
        grads["norm1_g"][l] = dg1[0]
        dw_in = jnp.concatenate([_mm_tn(s["h1"], da, "in_proj_wgrad_unit"),
                                 _mm_tn(s["h1"], rest, "in_proj_wgrad_rest")], axis=1)
        grads["w_in"][l] = jnp.transpose(dw_in.reshape(D_MODEL, N_CHIPS, IN_COLS // N_CHIPS), (1, 0, 2))

    out = {n: (v if n in BIG_NAMES[:4] else jnp.stack(v)) for n, v in grads.items()}
    out["final_g"] = d_final_g[0]
    return loss8[0, 0], dx, out, early


MESH = pl.DeviceIdType.MESH
ANY = pl.BlockSpec(memory_space=pl.ANY)


def _gather_plan(ins, outs, send_sems, recv_sems, layer, sender):
    n = len(ins)
    x, y, c = lax.axis_index("x"), lax.axis_index("y"), lax.axis_index("c")
    sibling = (x, y, 1 - c)
    my_chip = 2 * x + y
    chips = [(1 - x, y), (x, 1 - y), (1 - x, 1 - y)]
    ids = [2 * px + py for px, py in chips]

    def copy(a, k, chip, to, own=False):
        dst = outs[a].at[chip]
        return pltpu.make_async_remote_copy(
            src_ref=ins[a].at[layer] if own else dst, dst_ref=dst,
            send_sem=send_sems.at[a, k], recv_sem=recv_sems.at[a, k], device_id=to, device_id_type=MESH)

    sends = [copy(a, j, my_chip, (*chips[j], sender), own=True) for j in range(3) for a in range(n)]
    arrivals = [copy(a, j, ids[j], sibling) for j in range(3) for a in range(n)]
    forwards = [copy(a, 3 + j, ids[j], sibling) for j in range(3) for a in range(n)]
    return c, sends, arrivals, forwards


def _gather_shapes(shards):
    n = len(shards)
    return ([jax.ShapeDtypeStruct((N_CHIPS,) + s.shape[1:], s.dtype) for s in shards],
            [pltpu.SemaphoreType.DMA((n, 6)), pltpu.SemaphoreType.DMA((n, 6))])


def _all_gather(shards, layer, sender):
    n = len(shards)

    def body(*refs):
        c, sends, arrivals, forwards = _gather_plan(refs[:n], refs[n:2 * n], refs[2 * n], refs[2 * n + 1],
                                                    layer, sender)

        @pl.when(c == sender)
        def _():
            for cp in sends:
                cp.start()
            for arrived, onward in zip(arrivals, forwards):
                arrived.wait_recv()
                onward.start()
            for cp in sends + forwards:
                cp.wait_send()

        @pl.when(c != sender)
        def _():
            for cp in forwards:
                cp.wait_recv()

    out_shape, sems = _gather_shapes(shards)
    return _call(body, name="weight_all_gather", out_shape=out_shape, in_specs=[ANY] * n, out_specs=[ANY] * n,
                 scratch_shapes=sems)(*shards)


def _row_tile(r):
    return r if r <= 704 else 256


def _grad_swap(items, name):
    n = len(items)

    def body(*refs):
        start, finish = _swap_plan(refs[:n], refs[n:2 * n], refs[2 * n:3 * n], refs[3 * n], refs[3 * n + 1])
        start()
        finish()

    out_shape, sems = _swap_shapes(items)
    return _call(body, name=name, out_shape=out_shape, in_specs=[ANY] * (2 * n), out_specs=[ANY] * n,
                 scratch_shapes=sems)(*[a0 for a0, _ in items], *[a1 for _, a1 in items])


def _swap_plan(firsts, seconds, outs, send_sems, recv_sems):
    n = len(firsts)
    x, y, c = lax.axis_index("x"), lax.axis_index("y"), lax.axis_index("c")

    def copies(srcs):
        return [pltpu.make_async_remote_copy(src_ref=srcs[a], dst_ref=outs[a], send_sem=send_sems.at[a],
                                             recv_sem=recv_sems.at[a], device_id=(x, y, 1 - c),
                                             device_id_type=MESH) for a in range(n)]

    def start():
        @pl.when(c == 0)
        def _():
            for cp in copies(seconds):
                cp.start()

        @pl.when(c == 1)
        def _():
            for cp in copies(firsts):
                cp.start()

    def finish():
        for cp in copies(firsts):
            cp.wait()

    return start, finish


def _swap_shapes(items):
    n = len(items)
    return ([jax.ShapeDtypeStruct(a0.shape, a0.dtype) for a0, _ in items],
            [pltpu.SemaphoreType.DMA((n,)), pltpu.SemaphoreType.DMA((n,))])


def _pair_add(a0, a1, r, c_arr, name, out_dtype):
    k, rr, cc = r.shape
    tr = _row_tile(rr)

    def body(c_ref, a0_ref, a1_ref, r_ref, o_ref):
        mine = jnp.where(c_ref[0] == 0, a0_ref[...], a1_ref[...])
        o_ref[...] = (mine + r_ref[...]).astype(o_ref.dtype)

    def member(which):
        def index(kk, i, c_ref):
            used = (c_ref[0] == which).astype(jnp.int32)
            return (kk * used, i * used, 0)
        return pl.BlockSpec((1, tr, cc), index)

    spec = pl.BlockSpec((1, tr, cc), lambda kk, i, c_ref: (kk, i, 0))
    grid_spec = pltpu.PrefetchScalarGridSpec(num_scalar_prefetch=1, grid=(k, rr // tr),
                                             in_specs=[member(0), member(1), spec], out_specs=spec)
    return _call(body, name=name, grid_spec=grid_spec, out_shape=jax.ShapeDtypeStruct((k, rr, cc), out_dtype),
                 compiler_params=_params("parallel", "parallel"))(c_arr, a0, a1, r)


def _exchange_plan(ins, outs, send_sems, recv_sems):
    n = len(ins)
    x, y, c = lax.axis_index("x"), lax.axis_index("y"), lax.axis_index("c")
    my_chip = 2 * x + y
    chips = [(1 - x, y), (x, 1 - y), (1 - x, 1 - y)]

    def copy(a, k, src_chip, dst_chip):
        px, py = chips[k]
        return pltpu.make_async_remote_copy(
            src_ref=ins[a].at[src_chip], dst_ref=outs[a].at[dst_chip], send_sem=send_sems.at[a, k],
            recv_sem=recv_sems.at[a, k], device_id=(px, py, c), device_id_type=MESH)

    sends = [copy(a, k, 2 * chips[k][0] + chips[k][1], my_chip) for k in range(3) for a in range(n)]
    arrivals = [copy(a, k, my_chip, 2 * chips[k][0] + chips[k][1]) for k in range(3) for a in range(n)]
    return sends, arrivals


def _exchange_shapes(hs):
    n = len(hs)
    return ([jax.ShapeDtypeStruct(h.shape, h.dtype) for h in hs],
            [pltpu.SemaphoreType.DMA((n, 3)), pltpu.SemaphoreType.DMA((n, 3))])


def _grad_exchange(hs):
    n = len(hs)

    def body(*refs):
        sends, arrivals = _exchange_plan(refs[:n], refs[n:2 * n], refs[2 * n], refs[2 * n + 1])
        for cp in sends:
            cp.start()
        for cp in arrivals:
            cp.wait_recv()
        for cp in sends:
            cp.wait_send()

    out_shape, sems = _exchange_shapes(hs)
    return _call(body, name="grad_exchange_chips", out_shape=out_shape, in_specs=[ANY] * n, out_specs=[ANY] * n,
                 scratch_shapes=sems)(*hs)


def _sum_chips(a, c_arr, name):
    _, r, cc = a.shape
    tr = _row_tile(r)

    def body(c_ref, a_ref, o_ref):
        o_ref[...] = ((a_ref[0].astype(F32) + a_ref[1].astype(F32)) + a_ref[2].astype(F32)) + a_ref[3].astype(F32)

    grid_spec = pltpu.PrefetchScalarGridSpec(
        num_scalar_prefetch=1, grid=(r // tr,),
        in_specs=[pl.BlockSpec((N_CHIPS, tr, cc), lambda i, c_ref: (0, i, 0))],
        out_specs=pl.BlockSpec((None, tr, cc), lambda i, c_ref: (c_ref[0], i, 0)))
    return _call(body, name=name, grid_spec=grid_spec, out_shape=jax.ShapeDtypeStruct((2, r, cc), F32),
                 compiler_params=_params("parallel"))(c_arr, a)


def _grad_share(bufs):
    n = len(bufs)

    def body(*refs):
        outs = refs[n:2 * n]
        send_sems, recv_sems = refs[2 * n:]
        x, y, c = lax.axis_index("x"), lax.axis_index("y"), lax.axis_index("c")
        copies = [pltpu.make_async_remote_copy(src_ref=outs[a].at[c], dst_ref=outs[a].at[c], send_sem=send_sems.at[a],
                                               recv_sem=recv_sems.at[a], device_id=(x, y, 1 - c),
                                               device_id_type=MESH) for a in range(n)]
        for cp in copies:
            cp.start()
        for a in range(n):
            pltpu.make_async_remote_copy(src_ref=outs[a].at[c], dst_ref=outs[a].at[1 - c], send_sem=send_sems.at[a],
                                         recv_sem=recv_sems.at[a], device_id=(x, y, 1 - c),
                                         device_id_type=MESH).wait_recv()
        for cp in copies:
            cp.wait_send()

    return _call(
        body, name="grad_share_cores", out_shape=[jax.ShapeDtypeStruct(b.shape, b.dtype) for b in bufs],
        in_specs=[ANY] * n, out_specs=[ANY] * n, input_output_aliases={a: a for a in range(n)},
        scratch_shapes=[pltpu.SemaphoreType.DMA((n,)), pltpu.SemaphoreType.DMA((n,))],
    )(*bufs)


def _adamw_math(g_ref, w_ref, m_ref, v_ref, d_ref, nm_ref, nv_ref):
    gv = g_ref[...]
    nm = ADAM_B1 * m_ref[...] + (1.0 - ADAM_B1) * gv
    nv = ADAM_B2 * v_ref[...] + (1.0 - ADAM_B2) * (gv * gv)
    m_hat = nm / (1.0 - ADAM_B1 ** ADAM_STEP)
    v_hat = nv / (1.0 - ADAM_B2 ** ADAM_STEP)
    d_ref[...] = -ADAM_LR * (m_hat / (jnp.sqrt(v_hat) + ADAM_EPS) + ADAM_WD * w_ref[...])
    nm_ref[...] = nm
    nv_ref[...] = nv


def _adamw_big(g, w, m, v, name):
    d, r, c = g.shape
    tr = r if r <= 704 else 256
    spec = pl.BlockSpec((1, tr, c), lambda l, i: (l, i, 0))

    def body(*refs):
        _adamw_math(*refs)

    shp = jax.ShapeDtypeStruct(g.shape, F32)
    return _call(body, name=name, grid=(d, r // tr), in_specs=[spec] * 4, out_specs=[spec] * 3,
                 out_shape=[shp, shp, shp], compiler_params=_params("parallel", "parallel"))(g, w, m, v)


def _adamw_small(gs, ws, ms, vs):
    n = len(gs)

    def body(*refs):
        ins, outs = refs[:4 * n], refs[4 * n:]
        for k in range(n):
            _adamw_math(ins[k], ins[n + k], ins[2 * n + k], ins[3 * n + k], outs[k], outs[n + k], outs[2 * n + k])

    shp = [jax.ShapeDtypeStruct(g.shape, F32) for g in gs]
    res = _call(body, name="adamw_small", out_shape=shp * 3)(*gs, *ws, *ms, *vs)
    return res[:n], res[n:2 * n], res[2 * n:]


def _rows(a, rows):
    flat = a.reshape(-1)
    return jnp.pad(flat, (0, rows * D_MODEL - flat.shape[0])).reshape(rows, D_MODEL)


def _small_rows(p, extra=None):
    parts = [p[n].reshape(-1) for n in SMALL_NAMES]
    if extra is not None:
        parts.append(extra.reshape(-1))
    flat = jnp.concatenate(parts)
    return jnp.pad(flat, (0, ROWS_SMALL * D_MODEL - flat.shape[0])).reshape(ROWS_SMALL, D_MODEL)


CONV_SHARD = (DEPTH, 3, 2 * D_FF // N_CHIPS)
N_CONV_SHARD = DEPTH * 3 * (2 * D_FF // N_CHIPS)


def _small_pack(g, loss):
    conv = jnp.transpose(g["conv_w"].reshape(DEPTH, 3, N_CHIPS, 2 * D_FF // N_CHIPS), (2, 0, 1, 3))
    conv = jnp.stack([_rows(conv[k], ROWS_CONV) for k in range(N_CHIPS)])
    small = jnp.broadcast_to(_small_rows(g, loss), (N_CHIPS, ROWS_SMALL, D_MODEL))
    return jnp.concatenate([conv, small], axis=1)


def _unpack_small(pack):
    out = {"conv_w": pack[:ROWS_CONV].reshape(-1)[:N_CONV_SHARD].reshape(CONV_SHARD)}
    flat = pack[ROWS_CONV:].reshape(-1)
    k = 0
    for name in SMALL_NAMES:
        shape = SMALL_SHAPES[name]
        n = 1
        for d in shape:
            n *= d
        out[name] = flat[k:k + n].reshape(shape)
        k += n
    out["extra"] = flat[k]
    return out


def _assemble_layer(gathered, shards, layer, names):
    my_chip = 2 * lax.axis_index("x") + lax.axis_index("y")
    out = {}
    for name, got, own in zip(names, gathered, shards):
        full = lax.dynamic_update_index_in_dim(got, own[layer], my_chip, 0)
        if name in ("w_in", "w_up"):
            k, r, wd = full.shape
            out[name] = jnp.transpose(full, (1, 0, 2)).reshape(r, k * wd)
        else:
            out[name] = full.reshape(-1, D_MODEL)
    return out


def _gather_weights(p):
    shards = [p[n].astype(jnp.bfloat16) for n in BIG_NAMES[:4]]
    conv_all = p["conv_w"].reshape(1, -1, p["conv_w"].shape[-1])
    got = _all_gather([shards[0], conv_all], 0, 0)
    my_chip = 2 * lax.axis_index("x") + lax.axis_index("y")
    conv = lax.dynamic_update_index_in_dim(got[1], conv_all[0], my_chip, 0)
    conv = jnp.transpose(conv.reshape((N_CHIPS,) + CONV_SHARD), (1, 2, 0, 3)).reshape(DEPTH, 3, 2 * D_FF)
    full = {n: [None, None] for n in BIG_NAMES[:4]}
    full["w_in"][0] = _assemble_layer(got[:1], shards[:1], 0, ("w_in",))["w_in"]
    full["conv_w"] = conv
    return full, dict(shards=shards, assemble=_assemble_layer)


def _halves(a):
    r = a.shape[1] // 2
    return a[:, :r], a[:, r:]


def _reduce_begin(items, names, dtypes, c_arr, tag):
    return _pair_adds(items, _grad_swap(items, "grad_swap_cores_" + tag), names, dtypes, c_arr)


def _pair_adds(items, got, names, dtypes, c_arr):
    return [_pair_add(a0, a1, r, c_arr, "grad_add_cores_" + nm, dt)
            for (a0, a1), r, nm, dt in zip(items, got, names, dtypes)]


def _reduce_end(parts, sent, names, c_arr):
    my_chip = 2 * lax.axis_index("x") + lax.axis_index("y")
    full = [lax.dynamic_update_index_in_dim(p, lax.dynamic_index_in_dim(own, my_chip, 0, keepdims=False), my_chip, 0)
            for p, own in zip(parts, sent)]
    return [_sum_chips(f, c_arr, "grad_sum_chips_" + nm) for f, nm in zip(full, names)]


EARLY_NAMES = ("w_o", "w_up", "w_down", "w_in_1")


def _early_items(grads):
    return [tuple(grads[n]) for n in ("w_o", "w_up", "w_down")] + [_halves(grads["w_in"][1])]


def kernel(x, norm1_g, w_in, sgu_norm_g, sgu_w, sgu_b, pool_w, pool_scale, mix_norm_g, w_o, norm2_g, w_up, conv_w, conv_b, w_down, final_g, loss_target, m_norm1_g, m_w_in, m_sgu_norm_g, m_sgu_w, m_sgu_b, m_pool_w, m_pool_scale, m_mix_norm_g, m_w_o, m_norm2_g, m_w_up, m_conv_w, m_conv_b, m_w_down, m_final_g, v_norm1_g, v_w_in, v_sgu_norm_g, v_sgu_w, v_sgu_b, v_pool_w, v_pool_scale, v_mix_norm_g, v_w_o, v_norm2_g, v_w_up, v_conv_w, v_conv_b, v_w_down, v_final_g):
    names = ("norm1_g", "w_in", "sgu_norm_g", "sgu_w", "sgu_b", "pool_w", "pool_scale", "mix_norm_g", "w_o",
             "norm2_g", "w_up", "conv_w", "conv_b", "w_down", "final_g")
    p = dict(zip(names, (norm1_g, w_in, sgu_norm_g, sgu_w, sgu_b, pool_w, pool_scale, mix_norm_g, w_o, norm2_g,
                         w_up, conv_w, conv_b, w_down, final_g)))
    pm = dict(zip(names, (m_norm1_g, m_w_in, m_sgu_norm_g, m_sgu_w, m_sgu_b, m_pool_w, m_pool_scale, m_mix_norm_g,
                          m_w_o, m_norm2_g, m_w_up, m_conv_w, m_conv_b, m_w_down, m_final_g)))
    pv = dict(zip(names, (v_norm1_g, v_w_in, v_sgu_norm_g, v_sgu_w, v_sgu_b, v_pool_w, v_pool_scale, v_mix_norm_g,
                          v_w_o, v_norm2_g, v_w_up, v_conv_w, v_conv_b, v_w_down, v_final_g)))
    c = lax.axis_index("c")
    gathered, late = _gather_weights(p)
    full = dict(p)
    full.update(gathered)

    c_arr = jnp.reshape(c, (1,)).astype(jnp.int32)
    early_types = [ICI_DT] * len(EARLY_NAMES)
    loss, dx, grads, (sent, received) = _local_step(
        x[0], loss_target[0], full, late,
        (_early_items, lambda items, swapped: _pair_adds(items, swapped, EARLY_NAMES, early_types, c_arr)))
    early_sums = _reduce_end(received, sent, EARLY_NAMES, c_arr)
    small_pack = _small_pack(grads, loss)
    late_names = ("w_in_0", "small")
    late_sent = _reduce_begin([_halves(grads["w_in"][0]), _halves(small_pack)], late_names, [ICI_DT, F32], c_arr, "late")
    late_sums = _reduce_end(_grad_exchange(late_sent), late_sent, late_names, c_arr)
    r_o, r_up, r_down, r_in1, r_in0, r_small = _grad_share(early_sums + late_sums)
    g = dict(w_o=r_o, w_up=r_up, w_down=r_down,
             w_in=jnp.stack([r_in0.reshape(D_MODEL, -1), r_in1.reshape(D_MODEL, -1)]))
    g.update(_unpack_small(r_small.reshape(2 * SP_HALF, D_MODEL)))
    d, nm, nv = {}, {}, {}
    for n in BIG_NAMES:
        d[n], nm[n], nv[n] = _adamw_big(g[n], p[n], pm[n], pv[n], "adamw_" + n)

    def two_d(a):
        return a.reshape(1, -1) if a.ndim == 1 else a

    ds, ms, vs = _adamw_small([two_d(g[n]) for n in SMALL_NAMES], [two_d(p[n]) for n in SMALL_NAMES],
                              [two_d(pm[n]) for n in SMALL_NAMES], [two_d(pv[n]) for n in SMALL_NAMES])
    for k, n in enumerate(SMALL_NAMES):
        d[n], nm[n], nv[n] = (a.reshape(p[n].shape) for a in (ds[k], ms[k], vs[k]))
    return (g["extra"], dx[None], *[g[n] for n in names], *[d[n] for n in names], *[nm[n] for n in names],
            *[nv[n] for n in names])
```

```python
import jax
import jax.numpy as jnp
from jax import lax
from jax.experimental import pallas as pl
from jax.experimental.pallas import tpu as pltpu

F32 = jnp.float32
MXU_DT = jnp.bfloat16

D_MODEL = 1024
DEPTH = 2
HEAD_DIM = 64
W_A = 256
W_B = 512
W_C = 256
IN_COLS = 2 * W_A + 3 * W_B + W_C
CHUNK = 128
POOL_WINDOWS = (2, 4, 8, 16)
D_FF = 2816
EPS = 1e-6
N_CHIPS = 4

ADAM_LR = 0.001
ADAM_B1 = 0.9
ADAM_B2 = 0.999
ADAM_EPS = 1e-08
ADAM_WD = 0.01
ADAM_STEP = 10

LANES = 128
TQ = 256
TK = 256
TM = 256
TM_MM = 512
HALO = 16
VMEM_LIMIT = 56 * 1024 * 1024

ROWS_CONV = 16
ROWS_SMALL = 240
SP_HALF = (ROWS_CONV + ROWS_SMALL) // 2
ICI_DT = jnp.bfloat16

BIG_NAMES = ("w_in", "w_o", "w_up", "w_down", "conv_w")
SMALL_NAMES = ("norm1_g", "sgu_norm_g", "sgu_w", "sgu_b", "pool_w", "pool_scale",
               "mix_norm_g", "norm2_g", "conv_b", "final_g")
SMALL_SHAPES = {
    "norm1_g": (DEPTH, D_MODEL), "sgu_norm_g": (DEPTH, W_A), "sgu_w": (DEPTH, 4, CHUNK, CHUNK),
    "sgu_b": (DEPTH, 4, CHUNK), "pool_w": (DEPTH, 4, 64, 64), "pool_scale": (DEPTH, W_C),
    "mix_norm_g": (DEPTH, D_MODEL), "norm2_g": (DEPTH, D_MODEL), "conv_b": (DEPTH, 2 * D_FF),
    "final_g": (D_MODEL,),
}


def _call(body, **kw):
    return pl.pallas_call(body, **kw)


def _params(*sem):
    return pltpu.CompilerParams(dimension_semantics=sem, vmem_limit_bytes=VMEM_LIMIT)


def _dot(a, b):
    return jnp.dot(a, b, preferred_element_type=F32)


def _dot_nt(a, b):
    return lax.dot_general(a, b, (((1,), (1,)), ((), ())), preferred_element_type=F32)


def _dot_tn(a, b):
    return lax.dot_general(a, b, (((0,), (0,)), ((), ())), preferred_element_type=F32)


def _group_mean(sq, gmat):
    sqb = sq.astype(MXU_DT)
    cols = [_dot(sqb[:, b * LANES:(b + 1) * LANES], gmat) for b in range(sq.shape[1] // LANES)]
    return cols[0] if len(cols) == 1 else jnp.concatenate(cols, axis=-1)


def _group_matrix():
    r = jnp.arange(LANES)
    return jnp.where((r[:, None] // HEAD_DIM) == (r[None, :] // HEAD_DIM), 1.0 / HEAD_DIM, 0.0).astype(MXU_DT)


def _tile(n):
    return max(t for t in range(LANES, 1536 + 1, LANES) if n % t == 0)


def _row_spec(tm, cols, col_block=0):
    return pl.BlockSpec((tm, cols), lambda i, cb=col_block: (i, cb))


def _full_spec(shape):
    nd = len(shape)
    return pl.BlockSpec(shape, lambda *_: (0,) * nd)


def _mm_res(a, w, res, name):
    S, K = a.shape
    N = w.shape[1]
    tm = TM_MM

    def body(a_ref, w_ref, r_ref, o_ref):
        o_ref[...] = r_ref[...] + _dot(a_ref[...], w_ref[...])

    return _call(
        body, name=name, grid=(S // tm,),
        in_specs=[_row_spec(tm, K), _full_spec((K, N)), _row_spec(tm, N)],
        out_specs=_row_spec(tm, N),
        out_shape=jax.ShapeDtypeStruct((S, N), F32),
        compiler_params=_params("parallel"),
    )(a, w, res)


def _mm_tn(a, b, name, col_tiles=False):
    S, K1 = a.shape
    N = b.shape[1]
    ts = min(4 * TM_MM, S)
    tk = _tile(K1)
    tn = _tile(N)
    if col_tiles:
        out_spec = pl.BlockSpec((None, tk, tn), lambda m, n, s: (n, m, 0))
        out_shape = jax.ShapeDtypeStruct((N // tn, K1, tn), F32)
    else:
        out_spec = pl.BlockSpec((tk, tn), lambda m, n, s: (m, n))
        out_shape = jax.ShapeDtypeStruct((K1, N), F32)

    def body(a_ref, b_ref, o_ref):
        @pl.when(pl.program_id(2) == 0)
        def _():
            o_ref[...] = jnp.zeros_like(o_ref)

        o_ref[...] += _dot_tn(a_ref[...], b_ref[...].astype(MXU_DT))

    return _call(
        body, name=name, grid=(K1 // tk, N // tn, S // ts),
        in_specs=[pl.BlockSpec((ts, tk), lambda m, n, s: (s, m)),
                  pl.BlockSpec((ts, tn), lambda m, n, s: (s, n))],
        out_specs=out_spec, out_shape=out_shape,
        compiler_params=_params("parallel", "parallel", "arbitrary"),
    )(a, b)


def _down_proj_loss(a, w, res, g, tgt):
    S, D = res.shape
    K = a.shape[1]
    tm = TM

    def body(a_ref, w_ref, res_ref, g_ref, t_ref, dx_ref, dg_ref, l_ref):
        xv = res_ref[...] + _dot(a_ref[...], w_ref[...])
        r = lax.rsqrt(jnp.mean(xv * xv, axis=-1, keepdims=True) + EPS)
        xhat = xv * r
        diff = xhat * g_ref[...] - t_ref[...]

        @pl.when(pl.program_id(0) == 0)
        def _():
            dg_ref[...] = jnp.zeros_like(dg_ref)
            l_ref[...] = jnp.zeros_like(l_ref)

        l_ref[...] += jnp.full(l_ref.shape, 0.5 * jnp.sum(jnp.mean(diff * diff, axis=-1, keepdims=True)), F32)
        dout = diff * (1.0 / D)
        dg_ref[...] += jnp.sum(dout * xhat, axis=0, keepdims=True)
        dxh = dout * g_ref[...]
        dx_ref[...] = r * (dxh - xhat * jnp.mean(dxh * xhat, axis=-1, keepdims=True))

    return _call(
        body, name="down_proj_loss", grid=(S // tm,),
        in_specs=[_row_spec(tm, K), _full_spec((K, D)), _row_spec(tm, D), _full_spec((1, D)), _row_spec(tm, D)],
        out_specs=[_row_spec(tm, D), _full_spec((1, D)), _full_spec((8, LANES))],
        out_shape=[jax.ShapeDtypeStruct((S, D), F32), jax.ShapeDtypeStruct((1, D), F32),
                   jax.ShapeDtypeStruct((8, LANES), F32)],
        compiler_params=_params("arbitrary"),
    )(a, w, res, g, tgt)


def _mix_out(ya, yb, yc, gm, wo, x, gmat):
    S = x.shape[0]
    tm = TM_MM

    def body(ya_ref, yb_ref, yc_ref, gm_ref, wo_ref, x_ref, gmat_ref, x2_ref, yn_ref):
        y = jnp.concatenate([ya_ref[...], yb_ref[...], yc_ref[...]], axis=-1)
        r = lax.rsqrt(_group_mean(y * y, gmat_ref[...]) + EPS)
        yn = (y * r * gm_ref[...]).astype(MXU_DT)
        yn_ref[...] = yn
        x2_ref[...] = x_ref[...] + _dot(yn, wo_ref[...])

    return _call(
        body, name="mix_out", grid=(S // tm,),
        in_specs=[_row_spec(tm, W_A), _row_spec(tm, W_B), _row_spec(tm, W_C), _full_spec((1, D_MODEL)),
                  _full_spec((D_MODEL, D_MODEL)), _row_spec(tm, D_MODEL), _full_spec((LANES, LANES))],
        out_specs=[_row_spec(tm, D_MODEL), _row_spec(tm, D_MODEL)],
        out_shape=[jax.ShapeDtypeStruct((S, D_MODEL), F32), jax.ShapeDtypeStruct((S, D_MODEL), MXU_DT)],
        compiler_params=_params("parallel"),
    )(ya, yb, yc, gm, wo, x, gmat)


def _mix_out_bwd(dx2, wo, ya, yb, yc, gm, gmat, carry_swap=None):
    S = dx2.shape[0]
    tm = TM_MM
    nb = S // tm
    items = list(carry_swap or [])
    ns = len(items)

    def body(*refs):
        dx2_ref, wo_ref, ya_ref, yb_ref, yc_ref, gm_ref, gmat_ref = refs[:7]
        dya_ref, dyb_ref, dyc_ref, dgm_ref = refs[7 + 2 * ns:11 + 2 * ns]
        if ns:
            start, finish = _swap_plan(refs[7:7 + ns], refs[7 + ns:7 + 2 * ns], refs[11 + 2 * ns:11 + 3 * ns],
                                       refs[11 + 3 * ns], refs[12 + 3 * ns])

            @pl.when(pl.program_id(0) == 0)
            def _():
                start()

        dyn = _dot_nt(dx2_ref[...].astype(MXU_DT), wo_ref[...])
        y = jnp.concatenate([ya_ref[...], yb_ref[...], yc_ref[...]], axis=-1)
        r = lax.rsqrt(_group_mean(y * y, gmat_ref[...]) + EPS)
        yhat = y * r

        @pl.when(pl.program_id(0) == 0)
        def _():
            dgm_ref[...] = jnp.zeros_like(dgm_ref)

        dgm_ref[...] += jnp.sum(dyn * yhat, axis=0, keepdims=True)
        dyh = dyn * gm_ref[...]
        dy = r * (dyh - yhat * _group_mean(dyh * yhat, gmat_ref[...]))
        dya_ref[...] = dy[:, :W_A]
        dyb_ref[...] = dy[:, W_A:W_A + W_B]
        dyc_ref[...] = dy[:, W_A + W_B:]

        if ns:
            @pl.when(pl.program_id(0) == nb - 1)
            def _():
                finish()

    swapped, sems = _swap_shapes(items) if ns else ([], [])
    return _call(
        body, name="mix_out_bwd_swap" if ns else "mix_out_bwd", grid=(nb,),
        in_specs=[_row_spec(tm, D_MODEL), _full_spec((D_MODEL, D_MODEL)), _row_spec(tm, W_A), _row_spec(tm, W_B),
                  _row_spec(tm, W_C), _full_spec((1, D_MODEL)), _full_spec((LANES, LANES))] + [ANY] * (2 * ns),
        out_specs=[_row_spec(tm, W_A), _row_spec(tm, W_B), _row_spec(tm, W_C), _full_spec((1, D_MODEL))]
        + [ANY] * ns,
        out_shape=[jax.ShapeDtypeStruct((S, W_A), F32), jax.ShapeDtypeStruct((S, W_B), F32),
                   jax.ShapeDtypeStruct((S, W_C), F32), jax.ShapeDtypeStruct((1, D_MODEL), F32)] + swapped,
        scratch_shapes=sems,
        compiler_params=_params("arbitrary"),
    )(dx2, wo, ya, yb, yc, gm, gmat, *[a0 for a0, _ in items], *[a1 for _, a1 in items])


_SQRT_HALF = 0.7071067811865476
_INV_SQRT_2PI = 0.3989422804014327


def _sgu_common(a, sng, wm_ref, bias, gmat):
    phi = 0.5 * (1.0 + lax.erf(a * _SQRT_HALF))
    ga = a * phi
    u = ga[:, :W_A]
    v = ga[:, W_A:]
    r = lax.rsqrt(_group_mean(v * v, gmat) + EPS)
    vhat = v * r
    vn = (vhat * sng).astype(MXU_DT)
    head = lax.broadcasted_iota(jnp.int32, (CHUNK, W_A), 1) // HEAD_DIM
    rows = []
    for c in range(a.shape[0] // CHUNK):
        vc = vn[c * CHUNK:(c + 1) * CHUNK]
        s = bias
        for h in range(4):
            s = s + jnp.where(head == h, _dot(wm_ref[h], vc), 0.0)
        rows.append(s)
    s = jnp.concatenate(rows, axis=0)
    return phi, u, r, vhat, vn, s


def _tril_weights(sgu_w_l):
    t = jnp.arange(CHUNK)
    return jnp.where((t[None, :] <= t[:, None])[None], sgu_w_l, 0.0)


def _sgu_in_proj_bwd(proj, dy, sng, wm, wmt, bias, gmat, rest, w, x, g, dres):
    S, D = x.shape
    tm = TM_MM

    def body(a_ref, dy_ref, sng_ref, wm_ref, wmt_ref, b_ref, gmat_ref, rest_ref, w_ref, x_ref, g_ref, r_ref,
             da_ref, dw_ref, db_ref, dsng_ref, dx_ref, dg_ref):
        a = a_ref[...]
        dy = dy_ref[...]
        gmat = gmat_ref[...]
        sng = sng_ref[...]
        phi, u, r, vhat, vn, s = _sgu_common(a, sng, wm_ref, b_ref[...], gmat)
        du = dy * s
        ds = dy * u

        @pl.when(pl.program_id(0) == 0)
        def _():
            dw_ref[...] = jnp.zeros_like(dw_ref)
            db_ref[...] = jnp.zeros_like(db_ref)
            dsng_ref[...] = jnp.zeros_like(dsng_ref)
            dg_ref[...] = jnp.zeros_like(dg_ref)

        head = lax.broadcasted_iota(jnp.int32, (CHUNK, W_A), 1) // HEAD_DIM
        tt = lax.broadcasted_iota(jnp.int32, (CHUNK, CHUNK), 0)
        ss = lax.broadcasted_iota(jnp.int32, (CHUNK, CHUNK), 1)
        rows = []
        for c in range(tm // CHUNK):
            dsc = ds[c * CHUNK:(c + 1) * CHUNK]
            vc = vn[c * CHUNK:(c + 1) * CHUNK]
            db_ref[...] += dsc
            dsb = dsc.astype(MXU_DT)
            dvn = jnp.zeros((CHUNK, W_A), F32)
            for h in range(4):
                dvn = dvn + jnp.where(head == h, _dot(wmt_ref[h], dsb), 0.0)
                dsh = jnp.where(head == h, dsc, 0.0).astype(MXU_DT)
                dw_ref[h] += jnp.where(ss <= tt, _dot_nt(dsh, vc), 0.0)
            rows.append(dvn)
        dvn = jnp.concatenate(rows, axis=0)
        dsng_ref[...] += jnp.sum(dvn * vhat, axis=0, keepdims=True)
        dvh = dvn * sng
        dv = r * (dvh - vhat * _group_mean(dvh * vhat, gmat))
        dga = jnp.concatenate([du, dv], axis=-1)
        dgelu = phi + a * (_INV_SQRT_2PI * jnp.exp(-0.5 * a * a))
        dab = (dga * dgelu).astype(da_ref.dtype)
        da_ref[...] = dab

        dh = _dot_nt(dab, w_ref[:, :2 * W_A]) + _dot_nt(rest_ref[...], w_ref[:, 2 * W_A:])
        xv = x_ref[...]
        rx = lax.rsqrt(jnp.mean(xv * xv, axis=-1, keepdims=True) + EPS)
        xhat = xv * rx
        dg_ref[...] += jnp.sum(dh * xhat, axis=0, keepdims=True)
        dxh = dh * g_ref[...]
        dx_ref[...] = r_ref[...] + rx * (dxh - xhat * jnp.mean(dxh * xhat, axis=-1, keepdims=True))

    n_rest = IN_COLS - 2 * W_A
    return _call(
        body, name="sgu_in_proj_bwd", grid=(S // tm,),
        in_specs=[_row_spec(tm, 2 * W_A), _row_spec(tm, W_A), _full_spec((1, W_A)), _full_spec((4, CHUNK, CHUNK)),
                  _full_spec((4, CHUNK, CHUNK)), _full_spec((CHUNK, W_A)), _full_spec((LANES, LANES)),
                  _row_spec(tm, n_rest),
                  pl.BlockSpec((D, IN_COLS), lambda i: (0, 0), pipeline_mode=pl.Buffered(1)),
                  _row_spec(tm, D), _full_spec((1, D)), _row_spec(tm, D)],
        out_specs=[_row_spec(tm, 2 * W_A), _full_spec((4, CHUNK, CHUNK)), _full_spec((CHUNK, W_A)),
                   _full_spec((1, W_A)), _row_spec(tm, D), _full_spec((1, D))],
        out_shape=[jax.ShapeDtypeStruct((S, 2 * W_A), MXU_DT), jax.ShapeDtypeStruct((4, CHUNK, CHUNK), F32),
                   jax.ShapeDtypeStruct((CHUNK, W_A), F32), jax.ShapeDtypeStruct((1, W_A), F32),
                   jax.ShapeDtypeStruct((S, D), F32), jax.ShapeDtypeStruct((1, D), F32)],
        compiler_params=_params("arbitrary"),
    )(proj, dy, sng, wm, wmt, bias, gmat, rest, w, x, g, dres)


HG = 4
LW = HG * HEAD_DIM
Q_BLK0 = (2 * W_A) // LW
K_BLK0 = Q_BLK0 + W_B // LW
V_BLK0 = K_BLK0 + W_B // LW
N_GROUPS = W_B // LW
EXP_IS_ZERO_BELOW = -120.0


def _tri_matrix():
    r = jnp.arange(TK)
    return (r[:, None] > r[None, :]).astype(MXU_DT)


def _stack_heads(a):
    head = lax.broadcasted_iota(jnp.int32, a.shape, 1) // HEAD_DIM
    return jnp.concatenate([jnp.where(head == h, a, 0.0) for h in range(HG)], axis=0).astype(MXU_DT)


def _unstack_heads(a):
    head = lax.broadcasted_iota(jnp.int32, (TQ, LW), 1) // HEAD_DIM
    out = a[:TQ]
    for h in range(1, HG):
        out = jnp.where(head == h, a[h * TQ:(h + 1) * TQ], out)
    return out


def _sb_scores(q2, kj, tri, key_offset):
    z = _dot_nt(q2, kj)
    sp = jnp.log(1.0 + jnp.exp(-jnp.abs(z)))
    lsp = jnp.minimum(z, 0.0) - sp
    lsm = lsp - z
    msk = None
    if key_offset is not None:
        row = lax.broadcasted_iota(jnp.int32, z.shape, 0) & (TQ - 1)
        col = lax.broadcasted_iota(jnp.int32, z.shape, 1) + key_offset
        msk = col < row
        lsm = jnp.where(msk, lsm, 0.0)
    tail = _dot(lsm.astype(MXU_DT), tri)
    return lsp, lsm, tail, msk


def _sb_fwd(proj_b, tri, carry_gather=None):
    S = proj_b.shape[0]
    nq = S // TQ
    kpq = TQ // TK
    assert S // TK < LANES
    shards = list(carry_gather or [])
    ng = len(shards)

    def body(*refs):
        q_ref, k_ref, v_ref, tri_ref = refs[:4]
        o_ref, rb_ref = refs[4 + ng:6 + ng]
        acc_ref = refs[6 + 2 * ng]
        i = pl.program_id(1)
        if ng:
            step = pl.program_id(0) * nq + i
            sender = 1
            c, sends, arrivals, forwards = _gather_plan(refs[4:4 + ng], refs[6 + ng:6 + 2 * ng], refs[7 + 2 * ng],
                                                        refs[8 + 2 * ng], 1, sender)

            @pl.when(jnp.logical_and(step == 0, c == sender))
            def _():
                for cp in sends:
                    cp.start()

            @pl.when(jnp.logical_and(step == N_GROUPS * nq - max(nq // 8, 1), c == sender))
            def _():
                for arrived, onward in zip(arrivals, forwards):
                    arrived.wait_recv()
                    onward.start()


        lane2 = lax.broadcasted_iota(jnp.int32, (HG * TQ, LANES), 1)
        q2 = _stack_heads(q_ref[...].astype(F32) * (HEAD_DIM ** -0.5))
        tri = tri_ref[...]
        rb_ref[...] = jnp.zeros_like(rb_ref)

        def block(j, run, key_offset=None, first=False):
            start = pl.multiple_of(j * TK, TK)
            kj = k_ref[pl.ds(start, TK), :]
            vj = v_ref[pl.ds(start, TK), :]
            lsp, lsm, tail, msk = _sb_scores(q2, kj, tri, key_offset)
            rb_ref[...] = jnp.where(lane2 == j, run, rb_ref[...])
            att = jnp.exp(lsp + tail + run)
            if msk is not None:
                att = jnp.where(msk, att, 0.0)
            pv = _dot(att.astype(MXU_DT), vj)
            if first:
                acc_ref[...] = pv
            else:
                acc_ref[...] += pv
            return run + tail[:, :1] + lsm[:, :1]

        past = i * kpq

        def overlapping():
            run = jnp.zeros((HG * TQ, 1), F32)
            for d in reversed(range(kpq)):
                run = block(i * kpq + d, run, key_offset=d * TK, first=(d == kpq - 1))
            return run

        def alive(run):
            return (jnp.max(run) > EXP_IS_ZERO_BELOW).astype(jnp.int32)

        def walk(carry):
            n, run, _ = carry
            run = block(past - 1 - n, run)
            return n + 1, run, alive(run)

        n, run = lax.cond(past > 0, lambda: (jnp.int32(1), block(past - 1, overlapping())),
                          lambda: (jnp.int32(0), overlapping()))
        n, _, _ = lax.while_loop(lambda s: jnp.logical_and(s[0] < past, s[2] > 0), walk, (n, run, alive(run)))
        rb_ref[...] = jnp.where(lane2 == LANES - 1, n.astype(F32), rb_ref[...])
        o_ref[...] = _unstack_heads(acc_ref[...])

        if ng:
            @pl.when(jnp.logical_and(step == N_GROUPS * nq - 1, c == sender))
            def _():
                for cp in sends + forwards:
                    cp.wait_send()

            @pl.when(jnp.logical_and(step == N_GROUPS * nq - 1, c != sender))
            def _():
                for cp in forwards:
                    cp.wait_recv()

    once = pl.Buffered(1)
    gathered, sems = _gather_shapes(shards) if ng else ([], [])
    return _call(
        body, name="sb_fwd_gather" if ng else "sb_fwd", grid=(N_GROUPS, nq),
        in_specs=[pl.BlockSpec((TQ, LW), lambda p, i: (i, Q_BLK0 + p)),
                  pl.BlockSpec((S, LW), lambda p, i: (0, K_BLK0 + p), pipeline_mode=once),
                  pl.BlockSpec((S, LW), lambda p, i: (0, V_BLK0 + p), pipeline_mode=once),
                  pl.BlockSpec((TK, TK), lambda p, i: (0, 0))] + [ANY] * ng,
        out_specs=[pl.BlockSpec((TQ, LW), lambda p, i: (i, p)),
                   pl.BlockSpec((None, None, HG * TQ, LANES), lambda p, i: (p, i, 0, 0))] + [ANY] * ng,
        out_shape=[jax.ShapeDtypeStruct((S, W_B), F32),
                   jax.ShapeDtypeStruct((N_GROUPS, nq, HG * TQ, LANES), F32)] + gathered,
        scratch_shapes=[pltpu.VMEM((HG * TQ, LW), F32)] + sems,
        compiler_params=_params("arbitrary", "arbitrary"),
    )(proj_b, proj_b, proj_b, tri, *shards)


def _sb_bwd(proj_b, dyb, rb, tri, trit, carry_exchange=None):
    S = proj_b.shape[0]
    nq = S // TQ
    kpq = TQ // TK
    hs = list(carry_exchange or [])
    ne = len(hs)

    def body(*refs):
        q_ref, k_ref, v_ref, do_ref, rb_ref, tri_ref, trit_ref = refs[:7]
        dq_ref, dk_acc, dv_acc = refs[7 + ne:10 + ne]
        dq_acc = refs[10 + 2 * ne]
        i = pl.program_id(1)
        if ne:
            tick = pl.program_id(0) * nq + i
            sends, arrivals = _exchange_plan(refs[7:7 + ne], refs[10 + ne:10 + 2 * ne], refs[11 + 2 * ne],
                                             refs[12 + 2 * ne])

            @pl.when(tick == 0)
            def _():
                for cp in sends:
                    cp.start()

        lane2 = lax.broadcasted_iota(jnp.int32, (HG * TQ, LANES), 1)
        scale = HEAD_DIM ** -0.5
        q2 = _stack_heads(q_ref[...].astype(F32) * scale)
        do2 = _stack_heads(do_ref[...])
        tri = tri_ref[...]
        trit = trit_ref[...]

        @pl.when(i == 0)
        def _():
            dk_acc[...] = jnp.zeros_like(dk_acc)
            dv_acc[...] = jnp.zeros_like(dv_acc)

        dq_acc[...] = jnp.zeros_like(dq_acc)

        def block(j, pre, key_offset=None):
            start = pl.multiple_of(j * TK, TK)
            kj = k_ref[pl.ds(start, TK), :]
            vj = v_ref[pl.ds(start, TK), :]
            lsp, lsm, tail, msk = _sb_scores(q2, kj, tri, key_offset)
            run = jnp.sum(jnp.where(lane2 == j, rb_ref[...], 0.0), axis=-1, keepdims=True)
            att = jnp.exp(lsp + tail + run)
            if msk is not None:
                att = jnp.where(msk, att, 0.0)
            beta = jnp.exp(lsp)
            dl = _dot_nt(do2, vj) * att
            cin = _dot(dl.astype(MXU_DT), trit)
            dz = dl * (1.0 - beta) - beta * (pre + cin)
            if msk is not None:
                dz = jnp.where(msk, dz, 0.0)
            dzb = dz.astype(MXU_DT)
            dq_acc[...] += _dot(dzb, kj)
            dk_acc[pl.ds(start, TK), :] += _dot_tn(dzb, q2)
            dv_acc[pl.ds(start, TK), :] += _dot_tn(att.astype(MXU_DT), do2)
            return pre + cin[:, TK - 1:] + dl[:, TK - 1:]

        past = i * kpq
        walked = jnp.max(jnp.where(lane2[:8] == LANES - 1, rb_ref[pl.ds(0, 8), :], 0.0)).astype(jnp.int32)
        walked = jnp.clip(walked, 0, past)
        def overlapping(pre):
            for d in range(kpq):
                pre = block(i * kpq + d, pre, key_offset=d * TK)
            return jnp.int32(0)

        def with_past():
            pre = lax.fori_loop(past - walked, past - 1, lambda j, pre: block(j, pre),
                                jnp.zeros((HG * TQ, 1), F32))
            return overlapping(block(past - 1, pre))

        lax.cond(walked > 0, with_past, lambda: overlapping(jnp.zeros((HG * TQ, 1), F32)))
        dq_ref[...] = (_unstack_heads(dq_acc[...]) * scale).astype(dq_ref.dtype)

        if ne:
            @pl.when(tick == N_GROUPS * nq - 1)
            def _():
                for cp in arrivals:
                    cp.wait_recv()
                for cp in sends:
                    cp.wait_send()

    once = pl.Buffered(1)
    exchanged, sems = _exchange_shapes(hs) if ne else ([], [])
    return _call(
        body, name="sb_bwd_exchange" if ne else "sb_bwd", grid=(N_GROUPS, nq),
        in_specs=[pl.BlockSpec((TQ, LW), lambda p, i: (i, Q_BLK0 + p)),
                  pl.BlockSpec((S, LW), lambda p, i: (0, K_BLK0 + p), pipeline_mode=once),
                  pl.BlockSpec((S, LW), lambda p, i: (0, V_BLK0 + p), pipeline_mode=once),
                  pl.BlockSpec((TQ, LW), lambda p, i: (i, p)),
                  pl.BlockSpec((None, None, HG * TQ, LANES), lambda p, i: (p, i, 0, 0)),
                  pl.BlockSpec((TK, TK), lambda p, i: (0, 0)),
                  pl.BlockSpec((TK, TK), lambda p, i: (0, 0))] + [ANY] * ne,
        out_specs=[pl.BlockSpec((TQ, LW), lambda p, i: (i, p)),
                   pl.BlockSpec((S, LW), lambda p, i: (0, p), pipeline_mode=once),
                   pl.BlockSpec((S, LW), lambda p, i: (0, p), pipeline_mode=once)] + [ANY] * ne,
        out_shape=[jax.ShapeDtypeStruct((S, W_B), MXU_DT), jax.ShapeDtypeStruct((S, W_B), F32),
                   jax.ShapeDtypeStruct((S, W_B), F32)] + exchanged,
        scratch_shapes=[pltpu.VMEM((HG * TQ, LW), F32)] + sems,
        compiler_params=_params("arbitrary", "arbitrary"),
    )(proj_b, proj_b, proj_b, dyb, rb, tri, trit, *hs)


P_BLK = (2 * W_A + 3 * W_B) // W_C


def _window_lanes():
    g = lax.broadcasted_iota(jnp.int32, (1, W_C), 1) // (W_C // 4)
    w = jnp.where(g == 0, POOL_WINDOWS[0], jnp.where(g == 1, POOL_WINDOWS[1],
                  jnp.where(g == 2, POOL_WINDOWS[2], POOL_WINDOWS[3])))
    return g, w


def _shift_rows(ext, k, tm, lead):
    n = ext.shape[0]
    return pltpu.roll(ext, shift=k % n, axis=0)[lead:lead + tm]


def _pool_diff(p_cur, p_halo, row0, tm):
    ext = jnp.concatenate([p_halo, p_cur], axis=0)
    g, w = _window_lanes()
    acc = ext
    sums = []
    for sh in (1, 2, 4, 8):
        acc = acc + pltpu.roll(acc, shift=sh, axis=0)
        sums.append(acc[HALO:HALO + tm])
    wsum = jnp.where(g == 0, sums[0], jnp.where(g == 1, sums[1], jnp.where(g == 2, sums[2], sums[3])))
    pos = (row0 + 1 + lax.broadcasted_iota(jnp.int32, (tm, W_C), 0)).astype(F32)
    cnt = jnp.minimum(pos, w.astype(F32))
    return wsum / cnt - p_cur, cnt


def _pool_specs(tm, nrow_blocks_halo):
    cur = pl.BlockSpec((tm, W_C), lambda i: (i, P_BLK))
    prev = pl.BlockSpec((HALO, W_C), lambda i: (jnp.maximum(i * (tm // HALO) - 1, 0), P_BLK))
    return cur, prev


def _in_proj_groups(x, g, w, sng, wm, bias, gmat, wbd, scale, carry_gather=None):
    S, D = x.shape
    tm = TM_MM
    nb = S // tm
    p0 = 2 * W_A + 3 * W_B
    qkv_chunk = 3 * W_B // 2
    shards = list(carry_gather or [])
    ng = len(shards)

    def body(*refs):
        x_ref, g_ref, w_ref, sng_ref, wm_ref, b_ref, gmat_ref, wbd_ref, sc_ref = refs[:9]
        h_ref, o_ref, ob_ref, ya_ref, yc_ref = refs[9 + ng:14 + ng]
        tail_ref = refs[14 + 2 * ng]
        i = pl.program_id(0)
        if ng:
            sender = 0
            c, sends, arrivals, forwards = _gather_plan(refs[9:9 + ng], refs[14 + ng:14 + 2 * ng], refs[15 + 2 * ng],
                                                        refs[16 + 2 * ng], 0, sender)

            @pl.when(jnp.logical_and(i == 0, c == sender))
            def _():
                for cp in sends:
                    cp.start()
        xv = x_ref[...]
        r = lax.rsqrt(jnp.mean(xv * xv, axis=-1, keepdims=True) + EPS)
        h = (xv * r * g_ref[...]).astype(h_ref.dtype)
        h_ref[...] = h

        def project(c0, c1):
            acc = _dot(h, w_ref[:, c0:c1])
            o_ref[:, c0:c1] = acc
            ob_ref[:, c0:c1] = acc.astype(ob_ref.dtype)
            return acc

        a = project(0, 2 * W_A)
        _, u, _, _, _, s = _sgu_common(a, sng_ref[...], wm_ref, b_ref[...], gmat_ref[...])
        ya_ref[...] = u * s
        for c0 in range(2 * W_A, p0, qkv_chunk):
            project(c0, c0 + qkv_chunk)
        p = project(p0, p0 + W_C)
        halo = jnp.where(i > 0, tail_ref[...], 0.0)
        tail_ref[...] = p[tm - HALO:]
        d, _ = _pool_diff(p, halo, i * tm, tm)
        yc_ref[...] = _dot(d.astype(MXU_DT), wbd_ref[...]) * sc_ref[...]

        if ng:
            @pl.when(jnp.logical_and(i == nb - 1, c == sender))
            def _():
                for arrived, onward in zip(arrivals, forwards):
                    arrived.wait_recv()
                    onward.start()
                for cp in sends + forwards:
                    cp.wait_send()

            @pl.when(jnp.logical_and(i == nb - 1, c != sender))
            def _():
                for cp in forwards:
                    cp.wait_recv()

    gathered, sems = _gather_shapes(shards) if ng else ([], [])
    return _call(
        body, name="in_proj_groups_gather" if ng else "in_proj_groups", grid=(nb,),
        in_specs=[_row_spec(tm, D), _full_spec((1, D)),
                  pl.BlockSpec((D, IN_COLS), lambda i: (0, 0), pipeline_mode=pl.Buffered(1)),
                  _full_spec((1, W_A)), _full_spec((4, CHUNK, CHUNK)), _full_spec((CHUNK, W_A)),
                  _full_spec((LANES, LANES)), _full_spec((W_C, W_C)), _full_spec((1, W_C))] + [ANY] * ng,
        out_specs=[_row_spec(tm, D), _row_spec(tm, IN_COLS), _row_spec(tm, IN_COLS), _row_spec(tm, W_A),
                   _row_spec(tm, W_C)] + [ANY] * ng,
        out_shape=[jax.ShapeDtypeStruct((S, D), MXU_DT), jax.ShapeDtypeStruct((S, IN_COLS), F32),
                   jax.ShapeDtypeStruct((S, IN_COLS), MXU_DT), jax.ShapeDtypeStruct((S, W_A), F32),
                   jax.ShapeDtypeStruct((S, W_C), F32)] + gathered,
        scratch_shapes=[pltpu.VMEM((HALO, W_C), F32)] + sems,
        compiler_params=_params("arbitrary"),
    )(x, g, w, sng, wm, bias, gmat, wbd, scale, *shards)


def _pool_bwd_a(proj, dy, wbd, scale):
    S = proj.shape[0]
    tm = TM_MM

    def body(p_ref, ph_ref, dy_ref, w_ref, sc_ref, dd_ref, e_ref, dw_ref, dsc_ref):
        i = pl.program_id(0)
        halo = jnp.where(i > 0, ph_ref[...], 0.0)
        d, cnt = _pool_diff(p_ref[...], halo, i * tm, tm)
        db = d.astype(MXU_DT)
        dy = dy_ref[...]

        @pl.when(i == 0)
        def _():
            dw_ref[...] = jnp.zeros_like(dw_ref)
            dsc_ref[...] = jnp.zeros_like(dsc_ref)

        dsc_ref[...] += jnp.sum(dy * _dot(db, w_ref[...]), axis=0, keepdims=True)
        dys = (dy * sc_ref[...]).astype(MXU_DT)
        dw_ref[...] += _dot_tn(db, dys)
        dd = _dot_nt(dys, w_ref[...])
        dd_ref[...] = dd
        e_ref[...] = dd / cnt

    cur, prev = _pool_specs(tm, S // HALO)
    return _call(
        body, name="pool_bwd_a", grid=(S // tm,),
        in_specs=[cur, prev, _row_spec(tm, W_C), _full_spec((W_C, W_C)), _full_spec((1, W_C))],
        out_specs=[_row_spec(tm, W_C), _row_spec(tm, W_C), _full_spec((W_C, W_C)), _full_spec((1, W_C))],
        out_shape=[jax.ShapeDtypeStruct((S, W_C), F32), jax.ShapeDtypeStruct((S, W_C), F32),
                   jax.ShapeDtypeStruct((W_C, W_C), F32), jax.ShapeDtypeStruct((1, W_C), F32)],
        compiler_params=_params("arbitrary"),
    )(proj, proj, dy, wbd, scale)


def _pool_bwd_b(dd, e):
    S = dd.shape[0]
    tm = TM_MM
    nb = S // tm

    def body(dd_ref, e_ref, en_ref, dp_ref):
        i = pl.program_id(0)
        halo = jnp.where(i < nb - 1, en_ref[...], 0.0)
        ext = jnp.concatenate([e_ref[...], halo], axis=0)
        n = ext.shape[0]
        g, _ = _window_lanes()
        acc = ext
        sums = []
        for sh in (1, 2, 4, 8):
            acc = acc + pltpu.roll(acc, shift=n - sh, axis=0)
            sums.append(acc[:tm])
        wsum = jnp.where(g == 0, sums[0], jnp.where(g == 1, sums[1], jnp.where(g == 2, sums[2], sums[3])))
        dp_ref[...] = (wsum - dd_ref[...]).astype(dp_ref.dtype)

    nxt = pl.BlockSpec((HALO, W_C), lambda i: (jnp.minimum((i + 1) * (tm // HALO), S // HALO - 1), 0))
    return _call(
        body, name="pool_bwd_b", grid=(nb,),
        in_specs=[_row_spec(tm, W_C), _row_spec(tm, W_C), nxt],
        out_specs=_row_spec(tm, W_C),
        out_shape=jax.ShapeDtypeStruct((S, W_C), MXU_DT),
        compiler_params=_params("parallel"),
    )(dd, e, e)


TN_FF = 1408
NB_FF = D_FF // TN_FF
CONV_ROWS = 8


def _conv(z_cur, z_halo, cwb, tm):
    ext = jnp.concatenate([z_halo, z_cur], axis=0)
    z2 = _shift_rows(ext, 2, tm, HALO)
    z1 = _shift_rows(ext, 1, tm, HALO)
    zc = cwb[3:4] + z2 * cwb[0:1] + z1 * cwb[1:2] + z_cur * cwb[2:3]
    return zc, z2, z1


def _up_proj_gate(x, g, w, cwb):
    S, D = x.shape
    tm = TM

    def body(x_ref, g_ref, w_ref, c_ref, h_ref, z_ref, zc_ref, f_ref, tail_ref):
        first = pl.program_id(0) == 0
        xv = x_ref[...]
        r = lax.rsqrt(jnp.mean(xv * xv, axis=-1, keepdims=True) + EPS)
        h = (xv * r * g_ref[...]).astype(h_ref.dtype)
        h_ref[...] = h
        for j in range(NB_FF):
            halves = []
            for col0 in (j * TN_FF, D_FF + j * TN_FF):
                zb = _dot(h, w_ref[:, col0:col0 + TN_FF]).astype(z_ref.dtype)
                z_ref[:, col0:col0 + TN_FF] = zb
                zf = zb.astype(F32)
                prev = jnp.where(first, 0.0, tail_ref[:, col0:col0 + TN_FF])
                tail_ref[:, col0:col0 + TN_FF] = zf[tm - HALO:]
                zc = _conv(zf, prev, c_ref[:, col0:col0 + TN_FF], tm)[0]
                zc_ref[:, col0:col0 + TN_FF] = zc.astype(zc_ref.dtype)
                halves.append(zc)
            gate, value = halves
            f_ref[:, j * TN_FF:(j + 1) * TN_FF] = (gate * jax.nn.sigmoid(gate) * value).astype(f_ref.dtype)

    return _call(
        body, name="up_proj_gate", grid=(S // tm,),
        in_specs=[_row_spec(tm, D), _full_spec((1, D)),
                  pl.BlockSpec((D, 2 * D_FF), lambda i: (0, 0), pipeline_mode=pl.Buffered(1)),
                  _full_spec((CONV_ROWS, 2 * D_FF))],
        out_specs=[_row_spec(tm, D), _row_spec(tm, 2 * D_FF), _row_spec(tm, 2 * D_FF), _row_spec(tm, D_FF)],
        out_shape=[jax.ShapeDtypeStruct((S, D), MXU_DT), jax.ShapeDtypeStruct((S, 2 * D_FF), MXU_DT),
                   jax.ShapeDtypeStruct((S, 2 * D_FF), MXU_DT), jax.ShapeDtypeStruct((S, D_FF), MXU_DT)],
        scratch_shapes=[pltpu.VMEM((HALO, 2 * D_FF), F32)],
        compiler_params=_params("arbitrary"),
    )(x, g, w, cwb)


def _gate_up_bwd(z, zc, cwb, w, wd, x, g, dres):
    S, D = x.shape
    tm = TM
    nb = S // tm

    def body(z_ref, zc_ref, zcn_ref, c_ref, w_ref, wd_ref, x_ref, g_ref, r_ref, rn_ref,
             dz_ref, dc_ref, dx_ref, dg_ref):
        i = pl.program_id(0)
        first = i == 0
        last = i == nb - 1
        dxe = jnp.concatenate([r_ref[...], jnp.where(last, 0.0, rn_ref[...])], axis=0).astype(MXU_DT)

        @pl.when(first)
        def _():
            dc_ref[...] = jnp.zeros_like(dc_ref)
            dg_ref[...] = jnp.zeros_like(dg_ref)

        rid = lax.broadcasted_iota(jnp.int32, (CONV_ROWS, TN_FF), 0)

        def conv_out(cols):
            return jnp.concatenate([zc_ref[:, cols].astype(F32), zcn_ref[:, cols].astype(F32)], axis=0)

        def conv_bwd(d, z0, c):
            d0 = d[:tm]
            d1 = _shift_rows(d, -1, tm, 0)
            d2 = _shift_rows(d, -2, tm, 0)
            sums = [jnp.sum(d2 * z0, axis=0, keepdims=True), jnp.sum(d1 * z0, axis=0, keepdims=True),
                    jnp.sum(d0 * z0, axis=0, keepdims=True), jnp.sum(d0, axis=0, keepdims=True)]
            dtaps = jnp.zeros((CONV_ROWS, TN_FF), F32)
            for k, v in enumerate(sums):
                dtaps = jnp.where(rid == k, v, dtaps)
            return d0 * c[2:3] + d1 * c[1:2] + d2 * c[0:1], dtaps

        dh = jnp.zeros((tm, D), F32)
        for j in range(NB_FF):
            gc = slice(j * TN_FF, (j + 1) * TN_FF)
            uc = slice(D_FF + j * TN_FF, D_FF + (j + 1) * TN_FF)
            gt = conv_out(gc)
            ut = conv_out(uc)
            df = _dot_nt(dxe, wd_ref[gc, :])
            sg = jax.nn.sigmoid(gt)
            dzg, dtg = conv_bwd(df * ut * (sg * (1.0 + gt * (1.0 - sg))), z_ref[:, gc].astype(F32), c_ref[:, gc])
            dzu, dtu = conv_bwd(df * (gt * sg), z_ref[:, uc].astype(F32), c_ref[:, uc])
            dzg = dzg.astype(dz_ref.dtype)
            dzu = dzu.astype(dz_ref.dtype)
            dz_ref[:, gc] = dzg
            dz_ref[:, uc] = dzu
            dc_ref[:, gc] += dtg
            dc_ref[:, uc] += dtu
            dh += _dot_nt(dzg, w_ref[:, gc]) + _dot_nt(dzu, w_ref[:, uc])

        xv = x_ref[...]
        r = lax.rsqrt(jnp.mean(xv * xv, axis=-1, keepdims=True) + EPS)
        xhat = xv * r
        dg_ref[...] += jnp.sum(dh * xhat, axis=0, keepdims=True)
        dxh = dh * g_ref[...]
        dx_ref[...] = r_ref[...] + r * (dxh - xhat * jnp.mean(dxh * xhat, axis=-1, keepdims=True))

    hb = tm // HALO
    last_halo = S // HALO - 1
    return _call(
        body, name="gate_up_bwd", grid=(nb,),
        in_specs=[_row_spec(tm, 2 * D_FF), _row_spec(tm, 2 * D_FF),
                  pl.BlockSpec((HALO, 2 * D_FF), lambda i: (jnp.minimum((i + 1) * hb, last_halo), 0)),
                  _full_spec((CONV_ROWS, 2 * D_FF)),
                  pl.BlockSpec((D, 2 * D_FF), lambda i: (0, 0), pipeline_mode=pl.Buffered(1)),
                  pl.BlockSpec((D_FF, D), lambda i: (0, 0), pipeline_mode=pl.Buffered(1)),
                  _row_spec(tm, D), _full_spec((1, D)), _row_spec(tm, D),
                  pl.BlockSpec((HALO, D), lambda i: (jnp.minimum((i + 1) * hb, last_halo), 0))],
        out_specs=[_row_spec(tm, 2 * D_FF), _full_spec((CONV_ROWS, 2 * D_FF)), _row_spec(tm, D), _full_spec((1, D))],
        out_shape=[jax.ShapeDtypeStruct((S, 2 * D_FF), MXU_DT), jax.ShapeDtypeStruct((CONV_ROWS, 2 * D_FF), F32),
                   jax.ShapeDtypeStruct((S, D), F32), jax.ShapeDtypeStruct((1, D), F32)],
        compiler_params=_params("arbitrary"),
    )(z, zc, zc, cwb, w, wd, x, g, dres, dres)


def _layer_consts(w, l):
    wm = _tril_weights(w["sgu_w"][l])
    eye = jnp.eye(4, dtype=F32)
    wbd = (w["pool_w"][l][:, :, None, :] * eye[:, None, :, None]).reshape(W_C, W_C)
    cwb = jnp.concatenate([w["conv_w"][l], w["conv_b"][l][None], jnp.zeros((CONV_ROWS - 4, 2 * D_FF), F32)], axis=0)
    return dict(
        g1=w["norm1_g"][l][None], g2=w["norm2_g"][l][None], gm=w["mix_norm_g"][l][None],
        sng=w["sgu_norm_g"][l][None], wm=wm.astype(MXU_DT), wmt=jnp.swapaxes(wm, 1, 2).astype(MXU_DT),
        bias=jnp.repeat(jnp.transpose(w["sgu_b"][l]), HEAD_DIM, axis=1),
        wbd=wbd.astype(MXU_DT), scale=w["pool_scale"][l][None], cwb=cwb,
    )


def _local_step(x, tgt, w, late=None, early_exchange=None):
    gmat = _group_matrix()
    tri = _tri_matrix()
    trit = jnp.transpose(tri)
    saved = []
    early = None
    big = {n: [w[n][l] for l in range(DEPTH)] for n in BIG_NAMES[:4]}
    for l in range(DEPTH):
        c = _layer_consts(w, l)
        if l == 0 and late is not None:
            assert DEPTH == 2
            shards = late["shards"]
            h1, proj, proj_b, ya, yc, *gathered = _in_proj_groups(
                x, c["g1"], big["w_in"][l], c["sng"], c["wm"], c["bias"], gmat, c["wbd"], c["scale"],
                carry_gather=shards[1:])
            for name, arr in late["assemble"](gathered, shards[1:], 0, BIG_NAMES[1:4]).items():
                big[name][0] = arr
            yb, rb, *gathered = _sb_fwd(proj_b, tri, carry_gather=shards)
            for name, arr in late["assemble"](gathered, shards, 1, BIG_NAMES[:4]).items():
                big[name][1] = arr
        else:
            h1, proj, proj_b, ya, yc = _in_proj_groups(x, c["g1"], big["w_in"][l], c["sng"], c["wm"], c["bias"],
                                                       gmat, c["wbd"], c["scale"])
            yb, rb = _sb_fwd(proj_b, tri)
        x2, yn = _mix_out(ya, yb, yc, c["gm"], big["w_o"][l], x, gmat)
        h2, z, zc, f = _up_proj_gate(x2, c["g2"], big["w_up"][l], c["cwb"])
        saved.append(dict(c=c, x=x, proj=proj, proj_b=proj_b, h1=h1, ya=ya, yb=yb, yc=yc, rb=rb, x2=x2, yn=yn,
                          z=z, zc=zc, h2=h2, f=f))
        if l < DEPTH - 1:
            x = _mm_res(f, big["w_down"][l], x2, "down_proj")
    for l in range(DEPTH):
        saved[l]["c"] = dict(saved[l]["c"], **{n: big[n][l] for n in BIG_NAMES[:4]})

    last = saved[-1]
    dx, d_final_g, loss8 = _down_proj_loss(last["f"], last["c"]["w_down"], last["x2"], w["final_g"][None], tgt)
    grads = {n: [None] * DEPTH for n in ("norm1_g", "w_in", "sgu_norm_g", "sgu_w", "sgu_b", "pool_w", "pool_scale",
                                         "mix_norm_g", "w_o", "norm2_g", "w_up", "conv_w", "conv_b", "w_down")}
    for l in reversed(range(DEPTH)):
        s = saved[l]
        c = s["c"]
        grads["w_down"][l] = _mm_tn(s["f"], dx, "down_proj_wgrad").reshape(N_CHIPS, D_FF // N_CHIPS, D_MODEL)
        dz, dcwb, dx2, dg2 = _gate_up_bwd(s["z"], s["zc"], c["cwb"], c["w_up"], c["w_down"], s["x2"], c["g2"], dx)
        grads["conv_w"][l] = dcwb[:3]
        grads["conv_b"][l] = dcwb[3]
        grads["w_up"][l] = _mm_tn(s["h2"], dz, "up_proj_wgrad", col_tiles=True)
        grads["norm2_g"][l] = dg2[0]
        grads["w_o"][l] = _mm_tn(s["yn"], dx2, "out_proj_wgrad").reshape(N_CHIPS, D_MODEL // N_CHIPS, D_MODEL)
        if l == 0 and early_exchange is not None:
            early_items = early_exchange[0](grads)
            dya, dyb, dyc, dgm, *swapped = _mix_out_bwd(dx2, c["w_o"], s["ya"], s["yb"], s["yc"], c["gm"], gmat,
                                                        carry_swap=early_items)
        else:
            dya, dyb, dyc, dgm = _mix_out_bwd(dx2, c["w_o"], s["ya"], s["yb"], s["yc"], c["gm"], gmat)
        grads["mix_norm_g"][l] = dgm[0]
        dd, e, dwbd, dscale = _pool_bwd_a(s["proj"], dyc, c["wbd"], c["scale"])
        dp = _pool_bwd_b(dd, e)
        grads["pool_w"][l] = jnp.stack([dwbd[g * 64:(g + 1) * 64, g * 64:(g + 1) * 64] for g in range(4)])
        grads["pool_scale"][l] = dscale[0]
        if l == 0 and early_exchange is not None:
            sent = early_exchange[1](early_items, swapped)
            dq, dk, dv, *parts = _sb_bwd(s["proj_b"], dyb, s["rb"], tri, trit, carry_exchange=sent)
            early = (sent, parts)
        else:
            dq, dk, dv = _sb_bwd(s["proj_b"], dyb, s["rb"], tri, trit)
        rest = jnp.concatenate([dq, dk.astype(MXU_DT), dv.astype(MXU_DT), dp], axis=1)
        da, dwm, dbias, dsng, dx, dg1 = _sgu_in_proj_bwd(s["proj"], dya, c["sng"], c["wm"], c["wmt"], c["bias"],
                                                         gmat, rest, c["w_in"], s["x"], c["g1"], dx2)
        grads["sgu_w"][l] = dwm
        grads["sgu_b"][l] = jnp.transpose(jnp.sum(dbias.reshape(CHUNK, 4, HEAD_DIM), axis=-1))
        grads["sgu_norm_g"][l] = dsng[0]
        grads["norm1_g"][l] = dg1[0]
        dw_in = jnp.concatenate([_mm_tn(s["h1"], da, "in_proj_wgrad_unit"),
                                 _mm_tn(s["h1"], rest, "in_proj_wgrad_rest")], axis=1)
        grads["w_in"][l] = jnp.transpose(dw_in.reshape(D_MODEL, N_CHIPS, IN_COLS // N_CHIPS), (1, 0, 2))

    out = {n: (v if n in BIG_NAMES[:4] else jnp.stack(v)) for n, v in grads.items()}
    out["final_g"] = d_final_g[0]
    return loss8[0, 0], dx, out, early


MESH = pl.DeviceIdType.MESH
ANY = pl.BlockSpec(memory_space=pl.ANY)


def _gather_plan(ins, outs, send_sems, recv_sems, layer, sender):
    n = len(ins)
    x, y, c = lax.axis_index("x"), lax.axis_index("y"), lax.axis_index("c")
    sibling = (x, y, 1 - c)
    my_chip = 2 * x + y
    chips = [(1 - x, y), (x, 1 - y), (1 - x, 1 - y)]
    ids = [2 * px + py for px, py in chips]

    def copy(a, k, chip, to, own=False):
        dst = outs[a].at[chip]
        return pltpu.make_async_remote_copy(
            src_ref=ins[a].at[layer] if own else dst, dst_ref=dst,
            send_sem=send_sems.at[a, k], recv_sem=recv_sems.at[a, k], device_id=to, device_id_type=MESH)

    sends = [copy(a, j, my_chip, (*chips[j], sender), own=True) for j in range(3) for a in range(n)]
    arrivals = [copy(a, j, ids[j], sibling) for j in range(3) for a in range(n)]
    forwards = [copy(a, 3 + j, ids[j], sibling) for j in range(3) for a in range(n)]
    return c, sends, arrivals, forwards


def _gather_shapes(shards):
    n = len(shards)
    return ([jax.ShapeDtypeStruct((N_CHIPS,) + s.shape[1:], s.dtype) for s in shards],
            [pltpu.SemaphoreType.DMA((n, 6)), pltpu.SemaphoreType.DMA((n, 6))])


def _all_gather(shards, layer, sender):
    n = len(shards)

    def body(*refs):
        c, sends, arrivals, forwards = _gather_plan(refs[:n], refs[n:2 * n], refs[2 * n], refs[2 * n + 1],
                                                    layer, sender)

        @pl.when(c == sender)
        def _():
            for cp in sends:
                cp.start()
            for arrived, onward in zip(arrivals, forwards):
                arrived.wait_recv()
                onward.start()
            for cp in sends + forwards:
                cp.wait_send()

        @pl.when(c != sender)
        def _():
            for cp in forwards:
                cp.wait_recv()

    out_shape, sems = _gather_shapes(shards)
    return _call(body, name="weight_all_gather", out_shape=out_shape, in_specs=[ANY] * n, out_specs=[ANY] * n,
                 scratch_shapes=sems)(*shards)


def _row_tile(r):
    return r if r <= 704 else 512


def _grad_swap(items, name):
    n = len(items)

    def body(*refs):
        start, finish = _swap_plan(refs[:n], refs[n:2 * n], refs[2 * n:3 * n], refs[3 * n], refs[3 * n + 1])
        start()
        finish()

    out_shape, sems = _swap_shapes(items)
    return _call(body, name=name, out_shape=out_shape, in_specs=[ANY] * (2 * n), out_specs=[ANY] * n,
                 scratch_shapes=sems)(*[a0 for a0, _ in items], *[a1 for _, a1 in items])


def _swap_plan(firsts, seconds, outs, send_sems, recv_sems):
    n = len(firsts)
    x, y, c = lax.axis_index("x"), lax.axis_index("y"), lax.axis_index("c")

    def copies(srcs):
        return [pltpu.make_async_remote_copy(src_ref=srcs[a], dst_ref=outs[a], send_sem=send_sems.at[a],
                                             recv_sem=recv_sems.at[a], device_id=(x, y, 1 - c),
                                             device_id_type=MESH) for a in range(n)]

    def start():
        @pl.when(c == 0)
        def _():
            for cp in copies(seconds):
                cp.start()

        @pl.when(c == 1)
        def _():
            for cp in copies(firsts):
                cp.start()

    def finish():
        for cp in copies(firsts):
            cp.wait()

    return start, finish


def _swap_shapes(items):
    n = len(items)
    return ([jax.ShapeDtypeStruct(a0.shape, a0.dtype) for a0, _ in items],
            [pltpu.SemaphoreType.DMA((n,)), pltpu.SemaphoreType.DMA((n,))])


def _pair_add(a0, a1, r, c_arr, name, out_dtype):
    k, rr, cc = r.shape
    tr = _row_tile(rr)

    def body(c_ref, a0_ref, a1_ref, r_ref, o_ref):
        mine = jnp.where(c_ref[0] == 0, a0_ref[...], a1_ref[...])
        o_ref[...] = (mine + r_ref[...]).astype(o_ref.dtype)

    def member(which):
        def index(kk, i, c_ref):
            used = (c_ref[0] == which).astype(jnp.int32)
            return (kk * used, i * used, 0)
        return pl.BlockSpec((1, tr, cc), index)

    spec = pl.BlockSpec((1, tr, cc), lambda kk, i, c_ref: (kk, i, 0))
    grid_spec = pltpu.PrefetchScalarGridSpec(num_scalar_prefetch=1, grid=(k, rr // tr),
                                             in_specs=[member(0), member(1), spec], out_specs=spec)
    return _call(body, name=name, grid_spec=grid_spec, out_shape=jax.ShapeDtypeStruct((k, rr, cc), out_dtype),
                 compiler_params=_params("parallel", "parallel"))(c_arr, a0, a1, r)


def _exchange_plan(ins, outs, send_sems, recv_sems):
    n = len(ins)
    x, y, c = lax.axis_index("x"), lax.axis_index("y"), lax.axis_index("c")
    my_chip = 2 * x + y
    chips = [(1 - x, y), (x, 1 - y), (1 - x, 1 - y)]

    def copy(a, k, src_chip, dst_chip):
        px, py = chips[k]
        return pltpu.make_async_remote_copy(
            src_ref=ins[a].at[src_chip], dst_ref=outs[a].at[dst_chip], send_sem=send_sems.at[a, k],
            recv_sem=recv_sems.at[a, k], device_id=(px, py, c), device_id_type=MESH)

    sends = [copy(a, k, 2 * chips[k][0] + chips[k][1], my_chip) for k in range(3) for a in range(n)]
    arrivals = [copy(a, k, my_chip, 2 * chips[k][0] + chips[k][1]) for k in range(3) for a in range(n)]
    return sends, arrivals


def _exchange_shapes(hs):
    n = len(hs)
    return ([jax.ShapeDtypeStruct(h.shape, h.dtype) for h in hs],
            [pltpu.SemaphoreType.DMA((n, 3)), pltpu.SemaphoreType.DMA((n, 3))])


def _grad_exchange(hs):
    n = len(hs)

    def body(*refs):
        sends, arrivals = _exchange_plan(refs[:n], refs[n:2 * n], refs[2 * n], refs[2 * n + 1])
        for cp in sends:
            cp.start()
        for cp in arrivals:
            cp.wait_recv()
        for cp in sends:
            cp.wait_send()

    out_shape, sems = _exchange_shapes(hs)
    return _call(body, name="grad_exchange_chips", out_shape=out_shape, in_specs=[ANY] * n, out_specs=[ANY] * n,
                 scratch_shapes=sems)(*hs)


def _sum_chips(a, c_arr, name):
    _, r, cc = a.shape
    tr = _row_tile(r)

    def body(c_ref, a_ref, o_ref):
        o_ref[...] = ((a_ref[0].astype(F32) + a_ref[1].astype(F32)) + a_ref[2].astype(F32)) + a_ref[3].astype(F32)

    grid_spec = pltpu.PrefetchScalarGridSpec(
        num_scalar_prefetch=1, grid=(r // tr,),
        in_specs=[pl.BlockSpec((N_CHIPS, tr, cc), lambda i, c_ref: (0, i, 0))],
        out_specs=pl.BlockSpec((None, tr, cc), lambda i, c_ref: (c_ref[0], i, 0)))
    return _call(body, name=name, grid_spec=grid_spec, out_shape=jax.ShapeDtypeStruct((2, r, cc), F32),
                 compiler_params=_params("parallel"))(c_arr, a)


def _grad_share(bufs):
    n = len(bufs)

    def body(*refs):
        outs = refs[n:2 * n]
        send_sems, recv_sems = refs[2 * n:]
        x, y, c = lax.axis_index("x"), lax.axis_index("y"), lax.axis_index("c")
        copies = [pltpu.make_async_remote_copy(src_ref=outs[a].at[c], dst_ref=outs[a].at[c], send_sem=send_sems.at[a],
                                               recv_sem=recv_sems.at[a], device_id=(x, y, 1 - c),
                                               device_id_type=MESH) for a in range(n)]
        for cp in copies:
            cp.start()
        for a in range(n):
            pltpu.make_async_remote_copy(src_ref=outs[a].at[c], dst_ref=outs[a].at[1 - c], send_sem=send_sems.at[a],
                                         recv_sem=recv_sems.at[a], device_id=(x, y, 1 - c),
                                         device_id_type=MESH).wait_recv()
        for cp in copies:
            cp.wait_send()

    return _call(
        body, name="grad_share_cores", out_shape=[jax.ShapeDtypeStruct(b.shape, b.dtype) for b in bufs],
        in_specs=[ANY] * n, out_specs=[ANY] * n, input_output_aliases={a: a for a in range(n)},
        scratch_shapes=[pltpu.SemaphoreType.DMA((n,)), pltpu.SemaphoreType.DMA((n,))],
    )(*bufs)


def _adamw_math(g_ref, w_ref, m_ref, v_ref, d_ref, nm_ref, nv_ref):
    gv = g_ref[...]
    nm = ADAM_B1 * m_ref[...] + (1.0 - ADAM_B1) * gv
    nv = ADAM_B2 * v_ref[...] + (1.0 - ADAM_B2) * (gv * gv)
    m_hat = nm / (1.0 - ADAM_B1 ** ADAM_STEP)
    v_hat = nv / (1.0 - ADAM_B2 ** ADAM_STEP)
    d_ref[...] = -ADAM_LR * (m_hat / (jnp.sqrt(v_hat) + ADAM_EPS) + ADAM_WD * w_ref[...])
    nm_ref[...] = nm
    nv_ref[...] = nv


def _adamw_big(g, w, m, v, name):
    d, r, c = g.shape
    tr = r if r <= 704 else 256
    spec = pl.BlockSpec((1, tr, c), lambda l, i: (l, i, 0))

    def body(*refs):
        _adamw_math(*refs)

    shp = jax.ShapeDtypeStruct(g.shape, F32)
    return _call(body, name=name, grid=(d, r // tr), in_specs=[spec] * 4, out_specs=[spec] * 3,
                 out_shape=[shp, shp, shp], compiler_params=_params("parallel", "parallel"))(g, w, m, v)


def _adamw_small(gs, ws, ms, vs):
    n = len(gs)

    def body(*refs):
        ins, outs = refs[:4 * n], refs[4 * n:]
        for k in range(n):
            _adamw_math(ins[k], ins[n + k], ins[2 * n + k], ins[3 * n + k], outs[k], outs[n + k], outs[2 * n + k])

    shp = [jax.ShapeDtypeStruct(g.shape, F32) for g in gs]
    res = _call(body, name="adamw_small", out_shape=shp * 3)(*gs, *ws, *ms, *vs)
    return res[:n], res[n:2 * n], res[2 * n:]


def _rows(a, rows):
    flat = a.reshape(-1)
    return jnp.pad(flat, (0, rows * D_MODEL - flat.shape[0])).reshape(rows, D_MODEL)


def _small_rows(p, extra=None):
    parts = [p[n].reshape(-1) for n in SMALL_NAMES]
    if extra is not None:
        parts.append(extra.reshape(-1))
    flat = jnp.concatenate(parts)
    return jnp.pad(flat, (0, ROWS_SMALL * D_MODEL - flat.shape[0])).reshape(ROWS_SMALL, D_MODEL)


CONV_SHARD = (DEPTH, 3, 2 * D_FF // N_CHIPS)
N_CONV_SHARD = DEPTH * 3 * (2 * D_FF // N_CHIPS)


def _small_pack(g, loss):
    conv = jnp.transpose(g["conv_w"].reshape(DEPTH, 3, N_CHIPS, 2 * D_FF // N_CHIPS), (2, 0, 1, 3))
    conv = jnp.stack([_rows(conv[k], ROWS_CONV) for k in range(N_CHIPS)])
    small = jnp.broadcast_to(_small_rows(g, loss), (N_CHIPS, ROWS_SMALL, D_MODEL))
    return jnp.concatenate([conv, small], axis=1)


def _unpack_small(pack):
    out = {"conv_w": pack[:ROWS_CONV].reshape(-1)[:N_CONV_SHARD].reshape(CONV_SHARD)}
    flat = pack[ROWS_CONV:].reshape(-1)
    k = 0
    for name in SMALL_NAMES:
        shape = SMALL_SHAPES[name]
        n = 1
        for d in shape:
            n *= d
        out[name] = flat[k:k + n].reshape(shape)
        k += n
    out["extra"] = flat[k]
    return out


def _assemble_layer(gathered, shards, layer, names):
    my_chip = 2 * lax.axis_index("x") + lax.axis_index("y")
    out = {}
    for name, got, own in zip(names, gathered, shards):
        full = lax.dynamic_update_index_in_dim(got, own[layer], my_chip, 0)
        if name in ("w_in", "w_up"):
            k, r, wd = full.shape
            out[name] = jnp.transpose(full, (1, 0, 2)).reshape(r, k * wd)
        else:
            out[name] = full.reshape(-1, D_MODEL)
    return out


def _gather_weights(p):
    shards = [p[n].astype(jnp.bfloat16) for n in BIG_NAMES[:4]]
    conv_all = p["conv_w"].reshape(1, -1, p["conv_w"].shape[-1])
    got = _all_gather([shards[0], conv_all], 0, 0)
    my_chip = 2 * lax.axis_index("x") + lax.axis_index("y")
    conv = lax.dynamic_update_index_in_dim(got[1], conv_all[0], my_chip, 0)
    conv = jnp.transpose(conv.reshape((N_CHIPS,) + CONV_SHARD), (1, 2, 0, 3)).reshape(DEPTH, 3, 2 * D_FF)
    full = {n: [None, None] for n in BIG_NAMES[:4]}
    full["w_in"][0] = _assemble_layer(got[:1], shards[:1], 0, ("w_in",))["w_in"]
    full["conv_w"] = conv
    return full, dict(shards=shards, assemble=_assemble_layer)


def _halves(a):
    r = a.shape[1] // 2
    return a[:, :r], a[:, r:]


def _reduce_begin(items, names, dtypes, c_arr, tag):
    return _pair_adds(items, _grad_swap(items, "grad_swap_cores_" + tag), names, dtypes, c_arr)


def _pair_adds(items, got, names, dtypes, c_arr):
    return [_pair_add(a0, a1, r, c_arr, "grad_add_cores_" + nm, dt)
            for (a0, a1), r, nm, dt in zip(items, got, names, dtypes)]


def _reduce_end(parts, sent, names, c_arr):
    my_chip = 2 * lax.axis_index("x") + lax.axis_index("y")
    full = [lax.dynamic_update_index_in_dim(p, lax.dynamic_index_in_dim(own, my_chip, 0, keepdims=False), my_chip, 0)
            for p, own in zip(parts, sent)]
    return [_sum_chips(f, c_arr, "grad_sum_chips_" + nm) for f, nm in zip(full, names)]


EARLY_NAMES = ("w_o", "w_up", "w_down", "w_in_1")


def _early_items(grads):
    return [tuple(grads[n]) for n in ("w_o", "w_up", "w_down")] + [_halves(grads["w_in"][1])]


def kernel(x, norm1_g, w_in, sgu_norm_g, sgu_w, sgu_b, pool_w, pool_scale, mix_norm_g, w_o, norm2_g, w_up, conv_w, conv_b, w_down, final_g, loss_target, m_norm1_g, m_w_in, m_sgu_norm_g, m_sgu_w, m_sgu_b, m_pool_w, m_pool_scale, m_mix_norm_g, m_w_o, m_norm2_g, m_w_up, m_conv_w, m_conv_b, m_w_down, m_final_g, v_norm1_g, v_w_in, v_sgu_norm_g, v_sgu_w, v_sgu_b, v_pool_w, v_pool_scale, v_mix_norm_g, v_w_o, v_norm2_g, v_w_up, v_conv_w, v_conv_b, v_w_down, v_final_g):
    names = ("norm1_g", "w_in", "sgu_norm_g", "sgu_w", "sgu_b", "pool_w", "pool_scale", "mix_norm_g", "w_o",
             "norm2_g", "w_up", "conv_w", "conv_b", "w_down", "final_g")
    p = dict(zip(names, (norm1_g, w_in, sgu_norm_g, sgu_w, sgu_b, pool_w, pool_scale, mix_norm_g, w_o, norm2_g,
                         w_up, conv_w, conv_b, w_down, final_g)))
    pm = dict(zip(names, (m_norm1_g, m_w_in, m_sgu_norm_g, m_sgu_w, m_sgu_b, m_pool_w, m_pool_scale, m_mix_norm_g,
                          m_w_o, m_norm2_g, m_w_up, m_conv_w, m_conv_b, m_w_down, m_final_g)))
    pv = dict(zip(names, (v_norm1_g, v_w_in, v_sgu_norm_g, v_sgu_w, v_sgu_b, v_pool_w, v_pool_scale, v_mix_norm_g,
                          v_w_o, v_norm2_g, v_w_up, v_conv_w, v_conv_b, v_w_down, v_final_g)))
    c = lax.axis_index("c")
    gathered, late = _gather_weights(p)
    full = dict(p)
    full.update(gathered)

    c_arr = jnp.reshape(c, (1,)).astype(jnp.int32)
    early_types = [ICI_DT] * len(EARLY_NAMES)
    loss, dx, grads, (sent, received) = _local_step(
        x[0], loss_target[0], full, late,
        (_early_items, lambda items, swapped: _pair_adds(items, swapped, EARLY_NAMES, early_types, c_arr)))
    early_sums = _reduce_end(received, sent, EARLY_NAMES, c_arr)
    small_pack = _small_pack(grads, loss)
    late_names = ("w_in_0", "small")
    late_sent = _reduce_begin([_halves(grads["w_in"][0]), _halves(small_pack)], late_names, [ICI_DT, F32], c_arr, "late")
    late_sums = _reduce_end(_grad_exchange(late_sent), late_sent, late_names, c_arr)
    r_o, r_up, r_down, r_in1, r_in0, r_small = _grad_share(early_sums + late_sums)
    g = dict(w_o=r_o, w_up=r_up, w_down=r_down,
             w_in=jnp.stack([r_in0.reshape(D_MODEL, -1), r_in1.reshape(D_MODEL, -1)]))
    g.update(_unpack_small(r_small.reshape(2 * SP_HALF, D_MODEL)))
    d, nm, nv = {}, {}, {}
    for n in BIG_NAMES:
        d[n], nm[n], nv[n] = _adamw_big(g[n], p[n], pm[n], pv[n], "adamw_" + n)

    def two_d(a):
        return a.reshape(1, -1) if a.ndim == 1 else a

    ds, ms, vs = _adamw_small([two_d(g[n]) for n in SMALL_NAMES], [two_d(p[n]) for n in SMALL_NAMES],
                              [two_d(pm[n]) for n in SMALL_NAMES], [two_d(pv[n]) for n in SMALL_NAMES])
    for k, n in enumerate(SMALL_NAMES):
        d[n], nm[n], nv[n] = (a.reshape(p[n].shape) for a in (ds[k], ms[k], vs[k]))
    return (g["extra"], dx[None], *[g[n] for n in names], *[d[n] for n in names], *[nm[n] for n in names],
            *[nv[n] for n in names])
```

```python
import jax
import jax.numpy as jnp
from jax import lax
from jax.experimental import pallas as pl
from jax.experimental.pallas import tpu as pltpu

F32 = jnp.float32
MXU_DT = jnp.bfloat16

D_MODEL = 1024
DEPTH = 2
HEAD_DIM = 64
W_A = 256
W_B = 512
W_C = 256
IN_COLS = 2 * W_A + 3 * W_B + W_C
CHUNK = 128
POOL_WINDOWS = (2, 4, 8, 16)
D_FF = 2816
EPS = 1e-6
N_CHIPS = 4

ADAM_LR = 0.001
ADAM_B1 = 0.9
ADAM_B2 = 0.999
ADAM_EPS = 1e-08
ADAM_WD = 0.01
ADAM_STEP = 10

LANES = 128
TQ = 256
TK = 256
TM = 256
TM_MM = 512
HALO = 16
VMEM_LIMIT = 56 * 1024 * 1024

ROWS_CONV = 16
ROWS_SMALL = 240
SP_HALF = (ROWS_CONV + ROWS_SMALL) // 2
ICI_DT = jnp.bfloat16

BIG_NAMES = ("w_in", "w_o", "w_up", "w_down", "conv_w")
SMALL_NAMES = ("norm1_g", "sgu_norm_g", "sgu_w", "sgu_b", "pool_w", "pool_scale",
               "mix_norm_g", "norm2_g", "conv_b", "final_g")
SMALL_SHAPES = {
    "norm1_g": (DEPTH, D_MODEL), "sgu_norm_g": (DEPTH, W_A), "sgu_w": (DEPTH, 4, CHUNK, CHUNK),
    "sgu_b": (DEPTH, 4, CHUNK), "pool_w": (DEPTH, 4, 64, 64), "pool_scale": (DEPTH, W_C),
    "mix_norm_g": (DEPTH, D_MODEL), "norm2_g": (DEPTH, D_MODEL), "conv_b": (DEPTH, 2 * D_FF),
    "final_g": (D_MODEL,),
}


def _call(body, **kw):
    return pl.pallas_call(body, **kw)


def _params(*sem):
    return pltpu.CompilerParams(dimension_semantics=sem, vmem_limit_bytes=VMEM_LIMIT)


def _dot(a, b):
    return jnp.dot(a, b, preferred_element_type=F32)


def _dot_nt(a, b):
    return lax.dot_general(a, b, (((1,), (1,)), ((), ())), preferred_element_type=F32)


def _dot_tn(a, b):
    return lax.dot_general(a, b, (((0,), (0,)), ((), ())), preferred_element_type=F32)


def _group_mean(sq, gmat):
    sqb = sq.astype(MXU_DT)
    cols = [_dot(sqb[:, b * LANES:(b + 1) * LANES], gmat) for b in range(sq.shape[1] // LANES)]
    return cols[0] if len(cols) == 1 else jnp.concatenate(cols, axis=-1)


def _group_matrix():
    r = jnp.arange(LANES)
    return jnp.where((r[:, None] // HEAD_DIM) == (r[None, :] // HEAD_DIM), 1.0 / HEAD_DIM, 0.0).astype(MXU_DT)


def _tile(n):
    return max(t for t in range(LANES, 1536 + 1, LANES) if n % t == 0)


def _row_spec(tm, cols, col_block=0):
    return pl.BlockSpec((tm, cols), lambda i, cb=col_block: (i, cb))


def _full_spec(shape):
    nd = len(shape)
    return pl.BlockSpec(shape, lambda *_: (0,) * nd)


def _mm_res(a, w, res, name):
    S, K = a.shape
    N = w.shape[1]
    tm = TM_MM

    def body(a_ref, w_ref, r_ref, o_ref):
        o_ref[...] = r_ref[...] + _dot(a_ref[...], w_ref[...])

    return _call(
        body, name=name, grid=(S // tm,),
        in_specs=[_row_spec(tm, K), _full_spec((K, N)), _row_spec(tm, N)],
        out_specs=_row_spec(tm, N),
        out_shape=jax.ShapeDtypeStruct((S, N), F32),
        compiler_params=_params("parallel"),
    )(a, w, res)


def _mm_tn(a, b, name, col_tiles=False):
    S, K1 = a.shape
    N = b.shape[1]
    ts = min(4 * TM_MM, S)
    tk = _tile(K1)
    tn = _tile(N)
    if col_tiles:
        out_spec = pl.BlockSpec((None, tk, tn), lambda m, n, s: (n, m, 0))
        out_shape = jax.ShapeDtypeStruct((N // tn, K1, tn), F32)
    else:
        out_spec = pl.BlockSpec((tk, tn), lambda m, n, s: (m, n))
        out_shape = jax.ShapeDtypeStruct((K1, N), F32)

    def body(a_ref, b_ref, o_ref):
        @pl.when(pl.program_id(2) == 0)
        def _():
            o_ref[...] = jnp.zeros_like(o_ref)

        o_ref[...] += _dot_tn(a_ref[...], b_ref[...].astype(MXU_DT))

    return _call(
        body, name=name, grid=(K1 // tk, N // tn, S // ts),
        in_specs=[pl.BlockSpec((ts, tk), lambda m, n, s: (s, m)),
                  pl.BlockSpec((ts, tn), lambda m, n, s: (s, n))],
        out_specs=out_spec, out_shape=out_shape,
        compiler_params=_params("parallel", "parallel", "arbitrary"),
    )(a, b)


def _down_proj_loss(a, w, res, g, tgt):
    S, D = res.shape
    K = a.shape[1]
    tm = TM

    def body(a_ref, w_ref, res_ref, g_ref, t_ref, dx_ref, dg_ref, l_ref):
        xv = res_ref[...] + _dot(a_ref[...], w_ref[...])
        r = lax.rsqrt(jnp.mean(xv * xv, axis=-1, keepdims=True) + EPS)
        xhat = xv * r
        diff = xhat * g_ref[...] - t_ref[...]

        @pl.when(pl.program_id(0) == 0)
        def _():
            dg_ref[...] = jnp.zeros_like(dg_ref)
            l_ref[...] = jnp.zeros_like(l_ref)

        l_ref[...] += jnp.full(l_ref.shape, 0.5 * jnp.sum(jnp.mean(diff * diff, axis=-1, keepdims=True)), F32)
        dout = diff * (1.0 / D)
        dg_ref[...] += jnp.sum(dout * xhat, axis=0, keepdims=True)
        dxh = dout * g_ref[...]
        dx_ref[...] = r * (dxh - xhat * jnp.mean(dxh * xhat, axis=-1, keepdims=True))

    return _call(
        body, name="down_proj_loss", grid=(S // tm,),
        in_specs=[_row_spec(tm, K), _full_spec((K, D)), _row_spec(tm, D), _full_spec((1, D)), _row_spec(tm, D)],
        out_specs=[_row_spec(tm, D), _full_spec((1, D)), _full_spec((8, LANES))],
        out_shape=[jax.ShapeDtypeStruct((S, D), F32), jax.ShapeDtypeStruct((1, D), F32),
                   jax.ShapeDtypeStruct((8, LANES), F32)],
        compiler_params=_params("arbitrary"),
    )(a, w, res, g, tgt)


def _mix_out(ya, yb, yc, gm, wo, x, gmat):
    S = x.shape[0]
    tm = TM_MM

    def body(ya_ref, yb_ref, yc_ref, gm_ref, wo_ref, x_ref, gmat_ref, x2_ref, yn_ref):
        y = jnp.concatenate([ya_ref[...], yb_ref[...], yc_ref[...]], axis=-1)
        r = lax.rsqrt(_group_mean(y * y, gmat_ref[...]) + EPS)
        yn = (y * r * gm_ref[...]).astype(MXU_DT)
        yn_ref[...] = yn
        x2_ref[...] = x_ref[...] + _dot(yn, wo_ref[...])

    return _call(
        body, name="mix_out", grid=(S // tm,),
        in_specs=[_row_spec(tm, W_A), _row_spec(tm, W_B), _row_spec(tm, W_C), _full_spec((1, D_MODEL)),
                  _full_spec((D_MODEL, D_MODEL)), _row_spec(tm, D_MODEL), _full_spec((LANES, LANES))],
        out_specs=[_row_spec(tm, D_MODEL), _row_spec(tm, D_MODEL)],
        out_shape=[jax.ShapeDtypeStruct((S, D_MODEL), F32), jax.ShapeDtypeStruct((S, D_MODEL), MXU_DT)],
        compiler_params=_params("parallel"),
    )(ya, yb, yc, gm, wo, x, gmat)


def _mix_out_bwd(dx2, wo, ya, yb, yc, gm, gmat, carry_swap=None):
    S = dx2.shape[0]
    tm = TM_MM
    nb = S // tm
    items = list(carry_swap or [])
    ns = len(items)

    def body(*refs):
        dx2_ref, wo_ref, ya_ref, yb_ref, yc_ref, gm_ref, gmat_ref = refs[:7]
        dya_ref, dyb_ref, dyc_ref, dgm_ref = refs[7 + 2 * ns:11 + 2 * ns]
        if ns:
            start, finish = _swap_plan(refs[7:7 + ns], refs[7 + ns:7 + 2 * ns], refs[11 + 2 * ns:11 + 3 * ns],
                                       refs[11 + 3 * ns], refs[12 + 3 * ns])

            @pl.when(pl.program_id(0) == 0)
            def _():
                start()

        dyn = _dot_nt(dx2_ref[...].astype(MXU_DT), wo_ref[...])
        y = jnp.concatenate([ya_ref[...], yb_ref[...], yc_ref[...]], axis=-1)
        r = lax.rsqrt(_group_mean(y * y, gmat_ref[...]) + EPS)
        yhat = y * r

        @pl.when(pl.program_id(0) == 0)
        def _():
            dgm_ref[...] = jnp.zeros_like(dgm_ref)

        dgm_ref[...] += jnp.sum(dyn * yhat, axis=0, keepdims=True)
        dyh = dyn * gm_ref[...]
        dy = r * (dyh - yhat * _group_mean(dyh * yhat, gmat_ref[...]))
        dya_ref[...] = dy[:, :W_A]
        dyb_ref[...] = dy[:, W_A:W_A + W_B]
        dyc_ref[...] = dy[:, W_A + W_B:]

        if ns:
            @pl.when(pl.program_id(0) == nb - 1)
            def _():
                finish()

    swapped, sems = _swap_shapes(items) if ns else ([], [])
    return _call(
        body, name="mix_out_bwd_swap" if ns else "mix_out_bwd", grid=(nb,),
        in_specs=[_row_spec(tm, D_MODEL), _full_spec((D_MODEL, D_MODEL)), _row_spec(tm, W_A), _row_spec(tm, W_B),
                  _row_spec(tm, W_C), _full_spec((1, D_MODEL)), _full_spec((LANES, LANES))] + [ANY] * (2 * ns),
        out_specs=[_row_spec(tm, W_A), _row_spec(tm, W_B), _row_spec(tm, W_C), _full_spec((1, D_MODEL))]
        + [ANY] * ns,
        out_shape=[jax.ShapeDtypeStruct((S, W_A), F32), jax.ShapeDtypeStruct((S, W_B), F32),
                   jax.ShapeDtypeStruct((S, W_C), F32), jax.ShapeDtypeStruct((1, D_MODEL), F32)] + swapped,
        scratch_shapes=sems,
        compiler_params=_params("arbitrary"),
    )(dx2, wo, ya, yb, yc, gm, gmat, *[a0 for a0, _ in items], *[a1 for _, a1 in items])


_SQRT_HALF = 0.7071067811865476
_INV_SQRT_2PI = 0.3989422804014327


def _sgu_common(a, sng, wm_ref, bias, gmat):
    phi = 0.5 * (1.0 + lax.erf(a * _SQRT_HALF))
    ga = a * phi
    u = ga[:, :W_A]
    v = ga[:, W_A:]
    r = lax.rsqrt(_group_mean(v * v, gmat) + EPS)
    vhat = v * r
    vn = (vhat * sng).astype(MXU_DT)
    head = lax.broadcasted_iota(jnp.int32, (CHUNK, W_A), 1) // HEAD_DIM
    rows = []
    for c in range(a.shape[0] // CHUNK):
        vc = vn[c * CHUNK:(c + 1) * CHUNK]
        s = bias
        for h in range(4):
            s = s + jnp.where(head == h, _dot(wm_ref[h], vc), 0.0)
        rows.append(s)
    s = jnp.concatenate(rows, axis=0)
    return phi, u, r, vhat, vn, s


def _tril_weights(sgu_w_l):
    t = jnp.arange(CHUNK)
    return jnp.where((t[None, :] <= t[:, None])[None], sgu_w_l, 0.0)


def _sgu_in_proj_bwd(proj, dy, sng, wm, wmt, bias, gmat, rest, w, x, g, dres):
    S, D = x.shape
    tm = TM_MM

    def body(a_ref, dy_ref, sng_ref, wm_ref, wmt_ref, b_ref, gmat_ref, rest_ref, w_ref, x_ref, g_ref, r_ref,
             da_ref, dw_ref, db_ref, dsng_ref, dx_ref, dg_ref):
        a = a_ref[...]
        dy = dy_ref[...]
        gmat = gmat_ref[...]
        sng = sng_ref[...]
        phi, u, r, vhat, vn, s = _sgu_common(a, sng, wm_ref, b_ref[...], gmat)
        du = dy * s
        ds = dy * u

        @pl.when(pl.program_id(0) == 0)
        def _():
            dw_ref[...] = jnp.zeros_like(dw_ref)
            db_ref[...] = jnp.zeros_like(db_ref)
            dsng_ref[...] = jnp.zeros_like(dsng_ref)
            dg_ref[...] = jnp.zeros_like(dg_ref)

        head = lax.broadcasted_iota(jnp.int32, (CHUNK, W_A), 1) // HEAD_DIM
        tt = lax.broadcasted_iota(jnp.int32, (CHUNK, CHUNK), 0)
        ss = lax.broadcasted_iota(jnp.int32, (CHUNK, CHUNK), 1)
        rows = []
        for c in range(tm // CHUNK):
            dsc = ds[c * CHUNK:(c + 1) * CHUNK]
            vc = vn[c * CHUNK:(c + 1) * CHUNK]
            db_ref[...] += dsc
            dsb = dsc.astype(MXU_DT)
            dvn = jnp.zeros((CHUNK, W_A), F32)
            for h in range(4):
                dvn = dvn + jnp.where(head == h, _dot(wmt_ref[h], dsb), 0.0)
                dsh = jnp.where(head == h, dsc, 0.0).astype(MXU_DT)
                dw_ref[h] += jnp.where(ss <= tt, _dot_nt(dsh, vc), 0.0)
            rows.append(dvn)
        dvn = jnp.concatenate(rows, axis=0)
        dsng_ref[...] += jnp.sum(dvn * vhat, axis=0, keepdims=True)
        dvh = dvn * sng
        dv = r * (dvh - vhat * _group_mean(dvh * vhat, gmat))
        dga = jnp.concatenate([du, dv], axis=-1)
        dgelu = phi + a * (_INV_SQRT_2PI * jnp.exp(-0.5 * a * a))
        dab = (dga * dgelu).astype(da_ref.dtype)
        da_ref[...] = dab

        dh = _dot_nt(dab, w_ref[:, :2 * W_A]) + _dot_nt(rest_ref[...], w_ref[:, 2 * W_A:])
        xv = x_ref[...]
        rx = lax.rsqrt(jnp.mean(xv * xv, axis=-1, keepdims=True) + EPS)
        xhat = xv * rx
        dg_ref[...] += jnp.sum(dh * xhat, axis=0, keepdims=True)
        dxh = dh * g_ref[...]
        dx_ref[...] = r_ref[...] + rx * (dxh - xhat * jnp.mean(dxh * xhat, axis=-1, keepdims=True))

    n_rest = IN_COLS - 2 * W_A
    return _call(
        body, name="sgu_in_proj_bwd", grid=(S // tm,),
        in_specs=[_row_spec(tm, 2 * W_A), _row_spec(tm, W_A), _full_spec((1, W_A)), _full_spec((4, CHUNK, CHUNK)),
                  _full_spec((4, CHUNK, CHUNK)), _full_spec((CHUNK, W_A)), _full_spec((LANES, LANES)),
                  _row_spec(tm, n_rest),
                  pl.BlockSpec((D, IN_COLS), lambda i: (0, 0), pipeline_mode=pl.Buffered(1)),
                  _row_spec(tm, D), _full_spec((1, D)), _row_spec(tm, D)],
        out_specs=[_row_spec(tm, 2 * W_A), _full_spec((4, CHUNK, CHUNK)), _full_spec((CHUNK, W_A)),
                   _full_spec((1, W_A)), _row_spec(tm, D), _full_spec((1, D))],
        out_shape=[jax.ShapeDtypeStruct((S, 2 * W_A), MXU_DT), jax.ShapeDtypeStruct((4, CHUNK, CHUNK), F32),
                   jax.ShapeDtypeStruct((CHUNK, W_A), F32), jax.ShapeDtypeStruct((1, W_A), F32),
                   jax.ShapeDtypeStruct((S, D), F32), jax.ShapeDtypeStruct((1, D), F32)],
        compiler_params=_params("arbitrary"),
    )(proj, dy, sng, wm, wmt, bias, gmat, rest, w, x, g, dres)


HG = 4
LW = HG * HEAD_DIM
Q_BLK0 = (2 * W_A) // LW
K_BLK0 = Q_BLK0 + W_B // LW
V_BLK0 = K_BLK0 + W_B // LW
N_GROUPS = W_B // LW
EXP_IS_ZERO_BELOW = -120.0


def _tri_matrix():
    r = jnp.arange(TK)
    return (r[:, None] > r[None, :]).astype(MXU_DT)


def _stack_heads(a):
    head = lax.broadcasted_iota(jnp.int32, a.shape, 1) // HEAD_DIM
    return jnp.concatenate([jnp.where(head == h, a, 0.0) for h in range(HG)], axis=0).astype(MXU_DT)


def _unstack_heads(a):
    head = lax.broadcasted_iota(jnp.int32, (TQ, LW), 1) // HEAD_DIM
    out = a[:TQ]
    for h in range(1, HG):
        out = jnp.where(head == h, a[h * TQ:(h + 1) * TQ], out)
    return out


def _sb_scores(q2, kj, tri, key_offset):
    z = _dot_nt(q2, kj)
    sp = jnp.log(1.0 + jnp.exp(-jnp.abs(z)))
    lsp = jnp.minimum(z, 0.0) - sp
    lsm = lsp - z
    msk = None
    if key_offset is not None:
        row = lax.broadcasted_iota(jnp.int32, z.shape, 0) & (TQ - 1)
        col = lax.broadcasted_iota(jnp.int32, z.shape, 1) + key_offset
        msk = col < row
        lsm = jnp.where(msk, lsm, 0.0)
    tail = _dot(lsm.astype(MXU_DT), tri)
    return lsp, lsm, tail, msk


def _sb_fwd(proj_b, tri, carry_gather=None):
    S = proj_b.shape[0]
    nq = S // TQ
    kpq = TQ // TK
    assert S // TK < LANES
    shards = list(carry_gather or [])
    ng = len(shards)

    def body(*refs):
        q_ref, k_ref, v_ref, tri_ref = refs[:4]
        o_ref, rb_ref = refs[4 + ng:6 + ng]
        acc_ref = refs[6 + 2 * ng]
        i = pl.program_id(1)
        if ng:
            step = pl.program_id(0) * nq + i
            sender = 1
            c, sends, arrivals, forwards = _gather_plan(refs[4:4 + ng], refs[6 + ng:6 + 2 * ng], refs[7 + 2 * ng],
                                                        refs[8 + 2 * ng], 1, sender)

            @pl.when(jnp.logical_and(step == 0, c == sender))
            def _():
                for cp in sends:
                    cp.start()

            @pl.when(jnp.logical_and(step == N_GROUPS * nq - max(nq // 8, 1), c == sender))
            def _():
                for arrived, onward in zip(arrivals, forwards):
                    arrived.wait_recv()
                    onward.start()


        lane2 = lax.broadcasted_iota(jnp.int32, (HG * TQ, LANES), 1)
        q2 = _stack_heads(q_ref[...].astype(F32) * (HEAD_DIM ** -0.5))
        tri = tri_ref[...]
        rb_ref[...] = jnp.zeros_like(rb_ref)

        def block(j, run, key_offset=None, first=False):
            start = pl.multiple_of(j * TK, TK)
            kj = k_ref[pl.ds(start, TK), :]
            vj = v_ref[pl.ds(start, TK), :]
            lsp, lsm, tail, msk = _sb_scores(q2, kj, tri, key_offset)
            rb_ref[...] = jnp.where(lane2 == j, run, rb_ref[...])
            att = jnp.exp(lsp + tail + run)
            if msk is not None:
                att = jnp.where(msk, att, 0.0)
            pv = _dot(att.astype(MXU_DT), vj)
            if first:
                acc_ref[...] = pv
            else:
                acc_ref[...] += pv
            return run + tail[:, :1] + lsm[:, :1]

        past = i * kpq

        def overlapping():
            run = jnp.zeros((HG * TQ, 1), F32)
            for d in reversed(range(kpq)):
                run = block(i * kpq + d, run, key_offset=d * TK, first=(d == kpq - 1))
            return run

        def alive(run):
            return (jnp.max(run) > EXP_IS_ZERO_BELOW).astype(jnp.int32)

        def walk(carry):
            n, run, _ = carry
            run = block(past - 1 - n, run)
            return n + 1, run, alive(run)

        n, run = lax.cond(past > 0, lambda: (jnp.int32(1), block(past - 1, overlapping())),
                          lambda: (jnp.int32(0), overlapping()))
        n, _, _ = lax.while_loop(lambda s: jnp.logical_and(s[0] < past, s[2] > 0), walk, (n, run, alive(run)))
        rb_ref[...] = jnp.where(lane2 == LANES - 1, n.astype(F32), rb_ref[...])
        o_ref[...] = _unstack_heads(acc_ref[...])

        if ng:
            @pl.when(jnp.logical_and(step == N_GROUPS * nq - 1, c == sender))
            def _():
                for cp in sends + forwards:
                    cp.wait_send()

            @pl.when(jnp.logical_and(step == N_GROUPS * nq - 1, c != sender))
            def _():
                for cp in forwards:
                    cp.wait_recv()

    once = pl.Buffered(1)
    gathered, sems = _gather_shapes(shards) if ng else ([], [])
    return _call(
        body, name="sb_fwd_gather" if ng else "sb_fwd", grid=(N_GROUPS, nq),
        in_specs=[pl.BlockSpec((TQ, LW), lambda p, i: (i, Q_BLK0 + p)),
                  pl.BlockSpec((S, LW), lambda p, i: (0, K_BLK0 + p), pipeline_mode=once),
                  pl.BlockSpec((S, LW), lambda p, i: (0, V_BLK0 + p), pipeline_mode=once),
                  pl.BlockSpec((TK, TK), lambda p, i: (0, 0))] + [ANY] * ng,
        out_specs=[pl.BlockSpec((TQ, LW), lambda p, i: (i, p)),
                   pl.BlockSpec((None, None, HG * TQ, LANES), lambda p, i: (p, i, 0, 0))] + [ANY] * ng,
        out_shape=[jax.ShapeDtypeStruct((S, W_B), F32),
                   jax.ShapeDtypeStruct((N_GROUPS, nq, HG * TQ, LANES), F32)] + gathered,
        scratch_shapes=[pltpu.VMEM((HG * TQ, LW), F32)] + sems,
        compiler_params=_params("arbitrary", "arbitrary"),
    )(proj_b, proj_b, proj_b, tri, *shards)


def _sb_bwd(proj_b, dyb, rb, tri, trit, carry_exchange=None):
    S = proj_b.shape[0]
    nq = S // TQ
    kpq = TQ // TK
    hs = list(carry_exchange or [])
    ne = len(hs)

    def body(*refs):
        q_ref, k_ref, v_ref, do_ref, rb_ref, tri_ref, trit_ref = refs[:7]
        dq_ref, dk_acc, dv_acc = refs[7 + ne:10 + ne]
        dq_acc = refs[10 + 2 * ne]
        i = pl.program_id(1)
        if ne:
            tick = pl.program_id(0) * nq + i
            sends, arrivals = _exchange_plan(refs[7:7 + ne], refs[10 + ne:10 + 2 * ne], refs[11 + 2 * ne],
                                             refs[12 + 2 * ne])

            @pl.when(tick == 0)
            def _():
                for cp in sends:
                    cp.start()

        lane2 = lax.broadcasted_iota(jnp.int32, (HG * TQ, LANES), 1)
        scale = HEAD_DIM ** -0.5
        q2 = _stack_heads(q_ref[...].astype(F32) * scale)
        do2 = _stack_heads(do_ref[...])
        tri = tri_ref[...]
        trit = trit_ref[...]

        @pl.when(i == 0)
        def _():
            dk_acc[...] = jnp.zeros_like(dk_acc)
            dv_acc[...] = jnp.zeros_like(dv_acc)

        dq_acc[...] = jnp.zeros_like(dq_acc)

        def block(j, pre, key_offset=None):
            start = pl.multiple_of(j * TK, TK)
            kj = k_ref[pl.ds(start, TK), :]
            vj = v_ref[pl.ds(start, TK), :]
            lsp, lsm, tail, msk = _sb_scores(q2, kj, tri, key_offset)
            run = jnp.sum(jnp.where(lane2 == j, rb_ref[...], 0.0), axis=-1, keepdims=True)
            att = jnp.exp(lsp + tail + run)
            if msk is not None:
                att = jnp.where(msk, att, 0.0)
            beta = jnp.exp(lsp)
            dl = _dot_nt(do2, vj) * att
            cin = _dot(dl.astype(MXU_DT), trit)
            dz = dl * (1.0 - beta) - beta * (pre + cin)
            if msk is not None:
                dz = jnp.where(msk, dz, 0.0)
            dzb = dz.astype(MXU_DT)
            dq_acc[...] += _dot(dzb, kj)
            dk_acc[pl.ds(start, TK), :] += _dot_tn(dzb, q2)
            dv_acc[pl.ds(start, TK), :] += _dot_tn(att.astype(MXU_DT), do2)
            return pre + cin[:, TK - 1:] + dl[:, TK - 1:]

        past = i * kpq
        walked = jnp.max(jnp.where(lane2[:8] == LANES - 1, rb_ref[pl.ds(0, 8), :], 0.0)).astype(jnp.int32)
        walked = jnp.clip(walked, 0, past)
        def overlapping(pre):
            for d in range(kpq):
                pre = block(i * kpq + d, pre, key_offset=d * TK)
            return jnp.int32(0)

        def with_past():
            pre = lax.fori_loop(past - walked, past - 1, lambda j, pre: block(j, pre),
                                jnp.zeros((HG * TQ, 1), F32))
            return overlapping(block(past - 1, pre))

        lax.cond(walked > 0, with_past, lambda: overlapping(jnp.zeros((HG * TQ, 1), F32)))
        dq_ref[...] = (_unstack_heads(dq_acc[...]) * scale).astype(dq_ref.dtype)

        if ne:
            @pl.when(tick == N_GROUPS * nq - 1)
            def _():
                for cp in arrivals:
                    cp.wait_recv()
                for cp in sends:
                    cp.wait_send()

    once = pl.Buffered(1)
    exchanged, sems = _exchange_shapes(hs) if ne else ([], [])
    return _call(
        body, name="sb_bwd_exchange" if ne else "sb_bwd", grid=(N_GROUPS, nq),
        in_specs=[pl.BlockSpec((TQ, LW), lambda p, i: (i, Q_BLK0 + p)),
                  pl.BlockSpec((S, LW), lambda p, i: (0, K_BLK0 + p), pipeline_mode=once),
                  pl.BlockSpec((S, LW), lambda p, i: (0, V_BLK0 + p), pipeline_mode=once),
                  pl.BlockSpec((TQ, LW), lambda p, i: (i, p)),
                  pl.BlockSpec((None, None, HG * TQ, LANES), lambda p, i: (p, i, 0, 0)),
                  pl.BlockSpec((TK, TK), lambda p, i: (0, 0)),
                  pl.BlockSpec((TK, TK), lambda p, i: (0, 0))] + [ANY] * ne,
        out_specs=[pl.BlockSpec((TQ, LW), lambda p, i: (i, p)),
                   pl.BlockSpec((S, LW), lambda p, i: (0, p), pipeline_mode=once),
                   pl.BlockSpec((S, LW), lambda p, i: (0, p), pipeline_mode=once)] + [ANY] * ne,
        out_shape=[jax.ShapeDtypeStruct((S, W_B), MXU_DT), jax.ShapeDtypeStruct((S, W_B), F32),
                   jax.ShapeDtypeStruct((S, W_B), F32)] + exchanged,
        scratch_shapes=[pltpu.VMEM((HG * TQ, LW), F32)] + sems,
        compiler_params=_params("arbitrary", "arbitrary"),
    )(proj_b, proj_b, proj_b, dyb, rb, tri, trit, *hs)


P_BLK = (2 * W_A + 3 * W_B) // W_C


def _window_lanes():
    g = lax.broadcasted_iota(jnp.int32, (1, W_C), 1) // (W_C // 4)
    w = jnp.where(g == 0, POOL_WINDOWS[0], jnp.where(g == 1, POOL_WINDOWS[1],
                  jnp.where(g == 2, POOL_WINDOWS[2], POOL_WINDOWS[3])))
    return g, w


def _shift_rows(ext, k, tm, lead):
    n = ext.shape[0]
    return pltpu.roll(ext, shift=k % n, axis=0)[lead:lead + tm]


def _pool_diff(p_cur, p_halo, row0, tm):
    ext = jnp.concatenate([p_halo, p_cur], axis=0)
    g, w = _window_lanes()
    acc = ext
    sums = []
    for sh in (1, 2, 4, 8):
        acc = acc + pltpu.roll(acc, shift=sh, axis=0)
        sums.append(acc[HALO:HALO + tm])
    wsum = jnp.where(g == 0, sums[0], jnp.where(g == 1, sums[1], jnp.where(g == 2, sums[2], sums[3])))
    pos = (row0 + 1 + lax.broadcasted_iota(jnp.int32, (tm, W_C), 0)).astype(F32)
    cnt = jnp.minimum(pos, w.astype(F32))
    return wsum / cnt - p_cur, cnt


def _pool_specs(tm, nrow_blocks_halo):
    cur = pl.BlockSpec((tm, W_C), lambda i: (i, P_BLK))
    prev = pl.BlockSpec((HALO, W_C), lambda i: (jnp.maximum(i * (tm // HALO) - 1, 0), P_BLK))
    return cur, prev


def _in_proj_groups(x, g, w, sng, wm, bias, gmat, wbd, scale, carry_gather=None):
    S, D = x.shape
    tm = TM_MM
    nb = S // tm
    p0 = 2 * W_A + 3 * W_B
    qkv_chunk = 3 * W_B // 2
    shards = list(carry_gather or [])
    ng = len(shards)

    def body(*refs):
        x_ref, g_ref, w_ref, sng_ref, wm_ref, b_ref, gmat_ref, wbd_ref, sc_ref = refs[:9]
        h_ref, o_ref, ob_ref, ya_ref, yc_ref = refs[9 + ng:14 + ng]
        tail_ref = refs[14 + 2 * ng]
        i = pl.program_id(0)
        if ng:
            sender = 0
            c, sends, arrivals, forwards = _gather_plan(refs[9:9 + ng], refs[14 + ng:14 + 2 * ng], refs[15 + 2 * ng],
                                                        refs[16 + 2 * ng], 0, sender)

            @pl.when(jnp.logical_and(i == 0, c == sender))
            def _():
                for cp in sends:
                    cp.start()
        xv = x_ref[...]
        r = lax.rsqrt(jnp.mean(xv * xv, axis=-1, keepdims=True) + EPS)
        h = (xv * r * g_ref[...]).astype(h_ref.dtype)
        h_ref[...] = h

        def project(c0, c1):
            acc = _dot(h, w_ref[:, c0:c1])
            o_ref[:, c0:c1] = acc
            ob_ref[:, c0:c1] = acc.astype(ob_ref.dtype)
            return acc

        a = project(0, 2 * W_A)
        _, u, _, _, _, s = _sgu_common(a, sng_ref[...], wm_ref, b_ref[...], gmat_ref[...])
        ya_ref[...] = u * s
        for c0 in range(2 * W_A, p0, qkv_chunk):
            project(c0, c0 + qkv_chunk)
        p = project(p0, p0 + W_C)
        halo = jnp.where(i > 0, tail_ref[...], 0.0)
        tail_ref[...] = p[tm - HALO:]
        d, _ = _pool_diff(p, halo, i * tm, tm)
        yc_ref[...] = _dot(d.astype(MXU_DT), wbd_ref[...]) * sc_ref[...]

        if ng:
            @pl.when(jnp.logical_and(i == nb - 1, c == sender))
            def _():
                for arrived, onward in zip(arrivals, forwards):
                    arrived.wait_recv()
                    onward.start()
                for cp in sends + forwards:
                    cp.wait_send()

            @pl.when(jnp.logical_and(i == nb - 1, c != sender))
            def _():
                for cp in forwards:
                    cp.wait_recv()

    gathered, sems = _gather_shapes(shards) if ng else ([], [])
    return _call(
        body, name="in_proj_groups_gather" if ng else "in_proj_groups", grid=(nb,),
        in_specs=[_row_spec(tm, D), _full_spec((1, D)),
                  pl.BlockSpec((D, IN_COLS), lambda i: (0, 0), pipeline_mode=pl.Buffered(1)),
                  _full_spec((1, W_A)), _full_spec((4, CHUNK, CHUNK)), _full_spec((CHUNK, W_A)),
                  _full_spec((LANES, LANES)), _full_spec((W_C, W_C)), _full_spec((1, W_C))] + [ANY] * ng,
        out_specs=[_row_spec(tm, D), _row_spec(tm, IN_COLS), _row_spec(tm, IN_COLS), _row_spec(tm, W_A),
                   _row_spec(tm, W_C)] + [ANY] * ng,
        out_shape=[jax.ShapeDtypeStruct((S, D), MXU_DT), jax.ShapeDtypeStruct((S, IN_COLS), F32),
                   jax.ShapeDtypeStruct((S, IN_COLS), MXU_DT), jax.ShapeDtypeStruct((S, W_A), F32),
                   jax.ShapeDtypeStruct((S, W_C), F32)] + gathered,
        scratch_shapes=[pltpu.VMEM((HALO, W_C), F32)] + sems,
        compiler_params=_params("arbitrary"),
    )(x, g, w, sng, wm, bias, gmat, wbd, scale, *shards)


def _pool_bwd_a(proj, dy, wbd, scale):
    S = proj.shape[0]
    tm = TM_MM

    def body(p_ref, ph_ref, dy_ref, w_ref, sc_ref, dd_ref, e_ref, dw_ref, dsc_ref):
        i = pl.program_id(0)
        halo = jnp.where(i > 0, ph_ref[...], 0.0)
        d, cnt = _pool_diff(p_ref[...], halo, i * tm, tm)
        db = d.astype(MXU_DT)
        dy = dy_ref[...]

        @pl.when(i == 0)
        def _():
            dw_ref[...] = jnp.zeros_like(dw_ref)
            dsc_ref[...] = jnp.zeros_like(dsc_ref)

        dsc_ref[...] += jnp.sum(dy * _dot(db, w_ref[...]), axis=0, keepdims=True)
        dys = (dy * sc_ref[...]).astype(MXU_DT)
        dw_ref[...] += _dot_tn(db, dys)
        dd = _dot_nt(dys, w_ref[...])
        dd_ref[...] = dd
        e_ref[...] = dd / cnt

    cur, prev = _pool_specs(tm, S // HALO)
    return _call(
        body, name="pool_bwd_a", grid=(S // tm,),
        in_specs=[cur, prev, _row_spec(tm, W_C), _full_spec((W_C, W_C)), _full_spec((1, W_C))],
        out_specs=[_row_spec(tm, W_C), _row_spec(tm, W_C), _full_spec((W_C, W_C)), _full_spec((1, W_C))],
        out_shape=[jax.ShapeDtypeStruct((S, W_C), F32), jax.ShapeDtypeStruct((S, W_C), F32),
                   jax.ShapeDtypeStruct((W_C, W_C), F32), jax.ShapeDtypeStruct((1, W_C), F32)],
        compiler_params=_params("arbitrary"),
    )(proj, proj, dy, wbd, scale)


def _pool_bwd_b(dd, e):
    S = dd.shape[0]
    tm = TM_MM
    nb = S // tm

    def body(dd_ref, e_ref, en_ref, dp_ref):
        i = pl.program_id(0)
        halo = jnp.where(i < nb - 1, en_ref[...], 0.0)
        ext = jnp.concatenate([e_ref[...], halo], axis=0)
        n = ext.shape[0]
        g, _ = _window_lanes()
        acc = ext
        sums = []
        for sh in (1, 2, 4, 8):
            acc = acc + pltpu.roll(acc, shift=n - sh, axis=0)
            sums.append(acc[:tm])
        wsum = jnp.where(g == 0, sums[0], jnp.where(g == 1, sums[1], jnp.where(g == 2, sums[2], sums[3])))
        dp_ref[...] = (wsum - dd_ref[...]).astype(dp_ref.dtype)

    nxt = pl.BlockSpec((HALO, W_C), lambda i: (jnp.minimum((i + 1) * (tm // HALO), S // HALO - 1), 0))
    return _call(
        body, name="pool_bwd_b", grid=(nb,),
        in_specs=[_row_spec(tm, W_C), _row_spec(tm, W_C), nxt],
        out_specs=_row_spec(tm, W_C),
        out_shape=jax.ShapeDtypeStruct((S, W_C), MXU_DT),
        compiler_params=_params("parallel"),
    )(dd, e, e)


TN_FF = 1408
NB_FF = D_FF // TN_FF
CONV_ROWS = 8


def _conv(z_cur, z_halo, cwb, tm):
    ext = jnp.concatenate([z_halo, z_cur], axis=0)
    z2 = _shift_rows(ext, 2, tm, HALO)
    z1 = _shift_rows(ext, 1, tm, HALO)
    zc = cwb[3:4] + z2 * cwb[0:1] + z1 * cwb[1:2] + z_cur * cwb[2:3]
    return zc, z2, z1


def _up_proj_gate(x, g, w, cwb):
    S, D = x.shape
    tm = TM_MM

    def body(x_ref, g_ref, w_ref, c_ref, h_ref, z_ref, zc_ref, f_ref, tail_ref):
        first = pl.program_id(0) == 0
        xv = x_ref[...]
        r = lax.rsqrt(jnp.mean(xv * xv, axis=-1, keepdims=True) + EPS)
        h = (xv * r * g_ref[...]).astype(h_ref.dtype)
        h_ref[...] = h
        for j in range(NB_FF):
            halves = []
            for col0 in (j * TN_FF, D_FF + j * TN_FF):
                zb = _dot(h, w_ref[:, col0:col0 + TN_FF]).astype(z_ref.dtype)
                z_ref[:, col0:col0 + TN_FF] = zb
                zf = zb.astype(F32)
                prev = jnp.where(first, 0.0, tail_ref[:, col0:col0 + TN_FF])
                tail_ref[:, col0:col0 + TN_FF] = zf[tm - HALO:]
                zc = _conv(zf, prev, c_ref[:, col0:col0 + TN_FF], tm)[0]
                zc_ref[:, col0:col0 + TN_FF] = zc.astype(zc_ref.dtype)
                halves.append(zc)
            gate, value = halves
            f_ref[:, j * TN_FF:(j + 1) * TN_FF] = (gate * jax.nn.sigmoid(gate) * value).astype(f_ref.dtype)

    return _call(
        body, name="up_proj_gate", grid=(S // tm,),
        in_specs=[_row_spec(tm, D), _full_spec((1, D)),
                  pl.BlockSpec((D, 2 * D_FF), lambda i: (0, 0), pipeline_mode=pl.Buffered(1)),
                  _full_spec((CONV_ROWS, 2 * D_FF))],
        out_specs=[_row_spec(tm, D), _row_spec(tm, 2 * D_FF), _row_spec(tm, 2 * D_FF), _row_spec(tm, D_FF)],
        out_shape=[jax.ShapeDtypeStruct((S, D), MXU_DT), jax.ShapeDtypeStruct((S, 2 * D_FF), MXU_DT),
                   jax.ShapeDtypeStruct((S, 2 * D_FF), MXU_DT), jax.ShapeDtypeStruct((S, D_FF), MXU_DT)],
        scratch_shapes=[pltpu.VMEM((HALO, 2 * D_FF), F32)],
        compiler_params=_params("arbitrary"),
    )(x, g, w, cwb)


def _gate_up_bwd(z, zc, cwb, w, wd, x, g, dres):
    S, D = x.shape
    tm = TM
    nb = S // tm

    def body(z_ref, zc_ref, zcn_ref, c_ref, w_ref, wd_ref, x_ref, g_ref, r_ref, rn_ref,
             dz_ref, dc_ref, dx_ref, dg_ref):
        i = pl.program_id(0)
        first = i == 0
        last = i == nb - 1
        dxe = jnp.concatenate([r_ref[...], jnp.where(last, 0.0, rn_ref[...])], axis=0).astype(MXU_DT)

        @pl.when(first)
        def _():
            dc_ref[...] = jnp.zeros_like(dc_ref)
            dg_ref[...] = jnp.zeros_like(dg_ref)

        rid = lax.broadcasted_iota(jnp.int32, (CONV_ROWS, TN_FF), 0)

        def conv_out(cols):
            return jnp.concatenate([zc_ref[:, cols].astype(F32), zcn_ref[:, cols].astype(F32)], axis=0)

        def conv_bwd(d, z0, c):
            d0 = d[:tm]
            d1 = _shift_rows(d, -1, tm, 0)
            d2 = _shift_rows(d, -2, tm, 0)
            sums = [jnp.sum(d2 * z0, axis=0, keepdims=True), jnp.sum(d1 * z0, axis=0, keepdims=True),
                    jnp.sum(d0 * z0, axis=0, keepdims=True), jnp.sum(d0, axis=0, keepdims=True)]
            dtaps = jnp.zeros((CONV_ROWS, TN_FF), F32)
            for k, v in enumerate(sums):
                dtaps = jnp.where(rid == k, v, dtaps)
            return d0 * c[2:3] + d1 * c[1:2] + d2 * c[0:1], dtaps

        dh = jnp.zeros((tm, D), F32)
        for j in range(NB_FF):
            gc = slice(j * TN_FF, (j + 1) * TN_FF)
            uc = slice(D_FF + j * TN_FF, D_FF + (j + 1) * TN_FF)
            gt = conv_out(gc)
            ut = conv_out(uc)
            df = _dot_nt(dxe, wd_ref[gc, :])
            sg = jax.nn.sigmoid(gt)
            dzg, dtg = conv_bwd(df * ut * (sg * (1.0 + gt * (1.0 - sg))), z_ref[:, gc].astype(F32), c_ref[:, gc])
            dzu, dtu = conv_bwd(df * (gt * sg), z_ref[:, uc].astype(F32), c_ref[:, uc])
            dzg = dzg.astype(dz_ref.dtype)
            dzu = dzu.astype(dz_ref.dtype)
            dz_ref[:, gc] = dzg
            dz_ref[:, uc] = dzu
            dc_ref[:, gc] += dtg
            dc_ref[:, uc] += dtu
            dh += _dot_nt(dzg, w_ref[:, gc]) + _dot_nt(dzu, w_ref[:, uc])

        xv = x_ref[...]
        r = lax.rsqrt(jnp.mean(xv * xv, axis=-1, keepdims=True) + EPS)
        xhat = xv * r
        dg_ref[...] += jnp.sum(dh * xhat, axis=0, keepdims=True)
        dxh = dh * g_ref[...]
        dx_ref[...] = r_ref[...] + r * (dxh - xhat * jnp.mean(dxh * xhat, axis=-1, keepdims=True))

    hb = tm // HALO
    last_halo = S // HALO - 1
    return _call(
        body, name="gate_up_bwd", grid=(nb,),
        in_specs=[_row_spec(tm, 2 * D_FF), _row_spec(tm, 2 * D_FF),
                  pl.BlockSpec((HALO, 2 * D_FF), lambda i: (jnp.minimum((i + 1) * hb, last_halo), 0)),
                  _full_spec((CONV_ROWS, 2 * D_FF)),
                  pl.BlockSpec((D, 2 * D_FF), lambda i: (0, 0), pipeline_mode=pl.Buffered(1)),
                  pl.BlockSpec((D_FF, D), lambda i: (0, 0), pipeline_mode=pl.Buffered(1)),
                  _row_spec(tm, D), _full_spec((1, D)), _row_spec(tm, D),
                  pl.BlockSpec((HALO, D), lambda i: (jnp.minimum((i + 1) * hb, last_halo), 0))],
        out_specs=[_row_spec(tm, 2 * D_FF), _full_spec((CONV_ROWS, 2 * D_FF)), _row_spec(tm, D), _full_spec((1, D))],
        out_shape=[jax.ShapeDtypeStruct((S, 2 * D_FF), MXU_DT), jax.ShapeDtypeStruct((CONV_ROWS, 2 * D_FF), F32),
                   jax.ShapeDtypeStruct((S, D), F32), jax.ShapeDtypeStruct((1, D), F32)],
        compiler_params=_params("arbitrary"),
    )(z, zc, zc, cwb, w, wd, x, g, dres, dres)


def _layer_consts(w, l):
    wm = _tril_weights(w["sgu_w"][l])
    eye = jnp.eye(4, dtype=F32)
    wbd = (w["pool_w"][l][:, :, None, :] * eye[:, None, :, None]).reshape(W_C, W_C)
    cwb = jnp.concatenate([w["conv_w"][l], w["conv_b"][l][None], jnp.zeros((CONV_ROWS - 4, 2 * D_FF), F32)], axis=0)
    return dict(
        g1=w["norm1_g"][l][None], g2=w["norm2_g"][l][None], gm=w["mix_norm_g"][l][None],
        sng=w["sgu_norm_g"][l][None], wm=wm.astype(MXU_DT), wmt=jnp.swapaxes(wm, 1, 2).astype(MXU_DT),
        bias=jnp.repeat(jnp.transpose(w["sgu_b"][l]), HEAD_DIM, axis=1),
        wbd=wbd.astype(MXU_DT), scale=w["pool_scale"][l][None], cwb=cwb,
    )


def _local_step(x, tgt, w, late=None, early_exchange=None):
    gmat = _group_matrix()
    tri = _tri_matrix()
    trit = jnp.transpose(tri)
    saved = []
    early = None
    big = {n: [w[n][l] for l in range(DEPTH)] for n in BIG_NAMES[:4]}
    for l in range(DEPTH):
        c = _layer_consts(w, l)
        if l == 0 and late is not None:
            assert DEPTH == 2
            shards = late["shards"]
            h1, proj, proj_b, ya, yc, *gathered = _in_proj_groups(
                x, c["g1"], big["w_in"][l], c["sng"], c["wm"], c["bias"], gmat, c["wbd"], c["scale"],
                carry_gather=shards[1:])
            for name, arr in late["assemble"](gathered, shards[1:], 0, BIG_NAMES[1:4]).items():
                big[name][0] = arr
            yb, rb, *gathered = _sb_fwd(proj_b, tri, carry_gather=shards)
            for name, arr in late["assemble"](gathered, shards, 1, BIG_NAMES[:4]).items():
                big[name][1] = arr
        else:
            h1, proj, proj_b, ya, yc = _in_proj_groups(x, c["g1"], big["w_in"][l], c["sng"], c["wm"], c["bias"],
                                                       gmat, c["wbd"], c["scale"])
            yb, rb = _sb_fwd(proj_b, tri)
        x2, yn = _mix_out(ya, yb, yc, c["gm"], big["w_o"][l], x, gmat)
        h2, z, zc, f = _up_proj_gate(x2, c["g2"], big["w_up"][l], c["cwb"])
        saved.append(dict(c=c, x=x, proj=proj, proj_b=proj_b, h1=h1, ya=ya, yb=yb, yc=yc, rb=rb, x2=x2, yn=yn,
                          z=z, zc=zc, h2=h2, f=f))
        if l < DEPTH - 1:
            x = _mm_res(f, big["w_down"][l], x2, "down_proj")
    for l in range(DEPTH):
        saved[l]["c"] = dict(saved[l]["c"], **{n: big[n][l] for n in BIG_NAMES[:4]})

    last = saved[-1]
    dx, d_final_g, loss8 = _down_proj_loss(last["f"], last["c"]["w_down"], last["x2"], w["final_g"][None], tgt)
    grads = {n: [None] * DEPTH for n in ("norm1_g", "w_in", "sgu_norm_g", "sgu_w", "sgu_b", "pool_w", "pool_scale",
                                         "mix_norm_g", "w_o", "norm2_g", "w_up", "conv_w", "conv_b", "w_down")}
    for l in reversed(range(DEPTH)):
        s = saved[l]
        c = s["c"]
        grads["w_down"][l] = _mm_tn(s["f"], dx, "down_proj_wgrad").reshape(N_CHIPS, D_FF // N_CHIPS, D_MODEL)
        dz, dcwb, dx2, dg2 = _gate_up_bwd(s["z"], s["zc"], c["cwb"], c["w_up"], c["w_down"], s["x2"], c["g2"], dx)
        grads["conv_w"][l] = dcwb[:3]
        grads["conv_b"][l] = dcwb[3]
        grads["w_up"][l] = _mm_tn(s["h2"], dz, "up_proj_wgrad", col_tiles=True)
        grads["norm2_g"][l] = dg2[0]
        grads["w_o"][l] = _mm_tn(s["yn"], dx2, "out_proj_wgrad").reshape(N_CHIPS, D_MODEL // N_CHIPS, D_MODEL)
        if l == 0 and early_exchange is not None:
            early_items = early_exchange[0](grads)
            dya, dyb, dyc, dgm, *swapped = _mix_out_bwd(dx2, c["w_o"], s["ya"], s["yb"], s["yc"], c["gm"], gmat,
                                                        carry_swap=early_items)
        else:
            dya, dyb, dyc, dgm = _mix_out_bwd(dx2, c["w_o"], s["ya"], s["yb"], s["yc"], c["gm"], gmat)
        grads["mix_norm_g"][l] = dgm[0]
        dd, e, dwbd, dscale = _pool_bwd_a(s["proj"], dyc, c["wbd"], c["scale"])
        dp = _pool_bwd_b(dd, e)
        grads["pool_w"][l] = jnp.stack([dwbd[g * 64:(g + 1) * 64, g * 64:(g + 1) * 64] for g in range(4)])
        grads["pool_scale"][l] = dscale[0]
        if l == 0 and early_exchange is not None:
            sent = early_exchange[1](early_items, swapped)
            dq, dk, dv, *parts = _sb_bwd(s["proj_b"], dyb, s["rb"], tri, trit, carry_exchange=sent)
            early = (sent, parts)
        else:
            dq, dk, dv = _sb_bwd(s["proj_b"], dyb, s["rb"], tri, trit)
        rest = jnp.concatenate([dq, dk.astype(MXU_DT), dv.astype(MXU_DT), dp], axis=1)
        da, dwm, dbias, dsng, dx, dg1 = _sgu_in_proj_bwd(s["proj"], dya, c["sng"], c["wm"], c["wmt"], c["bias"],
                                                         gmat, rest, c["w_in"], s["x"], c["g1"], dx2)
        grads["sgu_w"][l] = dwm
        grads["sgu_b"][l] = jnp.transpose(jnp.sum(dbias.reshape(CHUNK, 4, HEAD_DIM), axis=-1))
        grads["sgu_norm_g"][l] = dsng[0]
        grads["norm1_g"][l] = dg1[0]
        dw_in = jnp.concatenate([_mm_tn(s["h1"], da, "in_proj_wgrad_unit"),
                                 _mm_tn(s["h1"], rest, "in_proj_wgrad_rest")], axis=1)
        grads["w_in"][l] = jnp.transpose(dw_in.reshape(D_MODEL, N_CHIPS, IN_COLS // N_CHIPS), (1, 0, 2))

    out = {n: (v if n in BIG_NAMES[:4] else jnp.stack(v)) for n, v in grads.items()}
    out["final_g"] = d_final_g[0]
    return loss8[0, 0], dx, out, early


MESH = pl.DeviceIdType.MESH
ANY = pl.BlockSpec(memory_space=pl.ANY)


def _gather_plan(ins, outs, send_sems, recv_sems, layer, sender):
    n = len(ins)
    x, y, c = lax.axis_index("x"), lax.axis_index("y"), lax.axis_index("c")
    sibling = (x, y, 1 - c)
    my_chip = 2 * x + y
    chips = [(1 - x, y), (x, 1 - y), (1 - x, 1 - y)]
    ids = [2 * px + py for px, py in chips]

    def copy(a, k, chip, to, own=False):
        dst = outs[a].at[chip]
        return pltpu.make_async_remote_copy(
            src_ref=ins[a].at[layer] if own else dst, dst_ref=dst,
            send_sem=send_sems.at[a, k], recv_sem=recv_sems.at[a, k], device_id=to, device_id_type=MESH)

    sends = [copy(a, j, my_chip, (*chips[j], sender), own=True) for j in range(3) for a in range(n)]
    arrivals = [copy(a, j, ids[j], sibling) for j in range(3) for a in range(n)]
    forwards = [copy(a, 3 + j, ids[j], sibling) for j in range(3) for a in range(n)]
    return c, sends, arrivals, forwards


def _gather_shapes(shards):
    n = len(shards)
    return ([jax.ShapeDtypeStruct((N_CHIPS,) + s.shape[1:], s.dtype) for s in shards],
            [pltpu.SemaphoreType.DMA((n, 6)), pltpu.SemaphoreType.DMA((n, 6))])


def _all_gather(shards, layer, sender):
    n = len(shards)

    def body(*refs):
        c, sends, arrivals, forwards = _gather_plan(refs[:n], refs[n:2 * n], refs[2 * n], refs[2 * n + 1],
                                                    layer, sender)

        @pl.when(c == sender)
        def _():
            for cp in sends:
                cp.start()
            for arrived, onward in zip(arrivals, forwards):
                arrived.wait_recv()
                onward.start()
            for cp in sends + forwards:
                cp.wait_send()

        @pl.when(c != sender)
        def _():
            for cp in forwards:
                cp.wait_recv()

    out_shape, sems = _gather_shapes(shards)
    return _call(body, name="weight_all_gather", out_shape=out_shape, in_specs=[ANY] * n, out_specs=[ANY] * n,
                 scratch_shapes=sems)(*shards)


def _row_tile(r):
    return r if r <= 704 else 512


def _grad_swap(items, name):
    n = len(items)

    def body(*refs):
        start, finish = _swap_plan(refs[:n], refs[n:2 * n], refs[2 * n:3 * n], refs[3 * n], refs[3 * n + 1])
        start()
        finish()

    out_shape, sems = _swap_shapes(items)
    return _call(body, name=name, out_shape=out_shape, in_specs=[ANY] * (2 * n), out_specs=[ANY] * n,
                 scratch_shapes=sems)(*[a0 for a0, _ in items], *[a1 for _, a1 in items])


def _swap_plan(firsts, seconds, outs, send_sems, recv_sems):
    n = len(firsts)
    x, y, c = lax.axis_index("x"), lax.axis_index("y"), lax.axis_index("c")

    def copies(srcs):
        return [pltpu.make_async_remote_copy(src_ref=srcs[a], dst_ref=outs[a], send_sem=send_sems.at[a],
                                             recv_sem=recv_sems.at[a], device_id=(x, y, 1 - c),
                                             device_id_type=MESH) for a in range(n)]

    def start():
        @pl.when(c == 0)
        def _():
            for cp in copies(seconds):
                cp.start()

        @pl.when(c == 1)
        def _():
            for cp in copies(firsts):
                cp.start()

    def finish():
        for cp in copies(firsts):
            cp.wait()

    return start, finish


def _swap_shapes(items):
    n = len(items)
    return ([jax.ShapeDtypeStruct(a0.shape, a0.dtype) for a0, _ in items],
            [pltpu.SemaphoreType.DMA((n,)), pltpu.SemaphoreType.DMA((n,))])


def _pair_add(a0, a1, r, c_arr, name, out_dtype):
    k, rr, cc = r.shape
    tr = _row_tile(rr)

    def body(c_ref, a0_ref, a1_ref, r_ref, o_ref):
        mine = jnp.where(c_ref[0] == 0, a0_ref[...], a1_ref[...])
        o_ref[...] = (mine + r_ref[...]).astype(o_ref.dtype)

    def member(which):
        def index(kk, i, c_ref):
            used = (c_ref[0] == which).astype(jnp.int32)
            return (kk * used, i * used, 0)
        return pl.BlockSpec((1, tr, cc), index)

    spec = pl.BlockSpec((1, tr, cc), lambda kk, i, c_ref: (kk, i, 0))
    grid_spec = pltpu.PrefetchScalarGridSpec(num_scalar_prefetch=1, grid=(k, rr // tr),
                                             in_specs=[member(0), member(1), spec], out_specs=spec)
    return _call(body, name=name, grid_spec=grid_spec, out_shape=jax.ShapeDtypeStruct((k, rr, cc), out_dtype),
                 compiler_params=_params("parallel", "parallel"))(c_arr, a0, a1, r)


def _exchange_plan(ins, outs, send_sems, recv_sems):
    n = len(ins)
    x, y, c = lax.axis_index("x"), lax.axis_index("y"), lax.axis_index("c")
    my_chip = 2 * x + y
    chips = [(1 - x, y), (x, 1 - y), (1 - x, 1 - y)]

    def copy(a, k, src_chip, dst_chip):
        px, py = chips[k]
        return pltpu.make_async_remote_copy(
            src_ref=ins[a].at[src_chip], dst_ref=outs[a].at[dst_chip], send_sem=send_sems.at[a, k],
            recv_sem=recv_sems.at[a, k], device_id=(px, py, c), device_id_type=MESH)

    sends = [copy(a, k, 2 * chips[k][0] + chips[k][1], my_chip) for k in range(3) for a in range(n)]
    arrivals = [copy(a, k, my_chip, 2 * chips[k][0] + chips[k][1]) for k in range(3) for a in range(n)]
    return sends, arrivals


def _exchange_shapes(hs):
    n = len(hs)
    return ([jax.ShapeDtypeStruct(h.shape, h.dtype) for h in hs],
            [pltpu.SemaphoreType.DMA((n, 3)), pltpu.SemaphoreType.DMA((n, 3))])


def _grad_exchange(hs):
    n = len(hs)

    def body(*refs):
        sends, arrivals = _exchange_plan(refs[:n], refs[n:2 * n], refs[2 * n], refs[2 * n + 1])
        for cp in sends:
            cp.start()
        for cp in arrivals:
            cp.wait_recv()
        for cp in sends:
            cp.wait_send()

    out_shape, sems = _exchange_shapes(hs)
    return _call(body, name="grad_exchange_chips", out_shape=out_shape, in_specs=[ANY] * n, out_specs=[ANY] * n,
                 scratch_shapes=sems)(*hs)


def _sum_chips(a, c_arr, name):
    _, r, cc = a.shape
    tr = _row_tile(r)

    def body(c_ref, a_ref, o_ref):
        o_ref[...] = ((a_ref[0].astype(F32) + a_ref[1].astype(F32)) + a_ref[2].astype(F32)) + a_ref[3].astype(F32)

    grid_spec = pltpu.PrefetchScalarGridSpec(
        num_scalar_prefetch=1, grid=(r // tr,),
        in_specs=[pl.BlockSpec((N_CHIPS, tr, cc), lambda i, c_ref: (0, i, 0))],
        out_specs=pl.BlockSpec((None, tr, cc), lambda i, c_ref: (c_ref[0], i, 0)))
    return _call(body, name=name, grid_spec=grid_spec, out_shape=jax.ShapeDtypeStruct((2, r, cc), F32),
                 compiler_params=_params("parallel"))(c_arr, a)


def _grad_share(bufs):
    n = len(bufs)

    def body(*refs):
        outs = refs[n:2 * n]
        send_sems, recv_sems = refs[2 * n:]
        x, y, c = lax.axis_index("x"), lax.axis_index("y"), lax.axis_index("c")
        copies = [pltpu.make_async_remote_copy(src_ref=outs[a].at[c], dst_ref=outs[a].at[c], send_sem=send_sems.at[a],
                                               recv_sem=recv_sems.at[a], device_id=(x, y, 1 - c),
                                               device_id_type=MESH) for a in range(n)]
        for cp in copies:
            cp.start()
        for a in range(n):
            pltpu.make_async_remote_copy(src_ref=outs[a].at[c], dst_ref=outs[a].at[1 - c], send_sem=send_sems.at[a],
                                         recv_sem=recv_sems.at[a], device_id=(x, y, 1 - c),
                                         device_id_type=MESH).wait_recv()
        for cp in copies:
            cp.wait_send()

    return _call(
        body, name="grad_share_cores", out_shape=[jax.ShapeDtypeStruct(b.shape, b.dtype) for b in bufs],
        in_specs=[ANY] * n, out_specs=[ANY] * n, input_output_aliases={a: a for a in range(n)},
        scratch_shapes=[pltpu.SemaphoreType.DMA((n,)), pltpu.SemaphoreType.DMA((n,))],
    )(*bufs)


def _adamw_math(g_ref, w_ref, m_ref, v_ref, d_ref, nm_ref, nv_ref):
    gv = g_ref[...]
    nm = ADAM_B1 * m_ref[...] + (1.0 - ADAM_B1) * gv
    nv = ADAM_B2 * v_ref[...] + (1.0 - ADAM_B2) * (gv * gv)
    m_hat = nm / (1.0 - ADAM_B1 ** ADAM_STEP)
    v_hat = nv / (1.0 - ADAM_B2 ** ADAM_STEP)
    d_ref[...] = -ADAM_LR * (m_hat / (jnp.sqrt(v_hat) + ADAM_EPS) + ADAM_WD * w_ref[...])
    nm_ref[...] = nm
    nv_ref[...] = nv


def _adamw_big(g, w, m, v, name):
    d, r, c = g.shape
    tr = r if r <= 704 else 256
    spec = pl.BlockSpec((1, tr, c), lambda l, i: (l, i, 0))

    def body(*refs):
        _adamw_math(*refs)

    shp = jax.ShapeDtypeStruct(g.shape, F32)
    return _call(body, name=name, grid=(d, r // tr), in_specs=[spec] * 4, out_specs=[spec] * 3,
                 out_shape=[shp, shp, shp], compiler_params=_params("parallel", "parallel"))(g, w, m, v)


def _adamw_small(gs, ws, ms, vs):
    n = len(gs)

    def body(*refs):
        ins, outs = refs[:4 * n], refs[4 * n:]
        for k in range(n):
            _adamw_math(ins[k], ins[n + k], ins[2 * n + k], ins[3 * n + k], outs[k], outs[n + k], outs[2 * n + k])

    shp = [jax.ShapeDtypeStruct(g.shape, F32) for g in gs]
    res = _call(body, name="adamw_small", out_shape=shp * 3)(*gs, *ws, *ms, *vs)
    return res[:n], res[n:2 * n], res[2 * n:]


def _rows(a, rows):
    flat = a.reshape(-1)
    return jnp.pad(flat, (0, rows * D_MODEL - flat.shape[0])).reshape(rows, D_MODEL)


def _small_rows(p, extra=None):
    parts = [p[n].reshape(-1) for n in SMALL_NAMES]
    if extra is not None:
        parts.append(extra.reshape(-1))
    flat = jnp.concatenate(parts)
    return jnp.pad(flat, (0, ROWS_SMALL * D_MODEL - flat.shape[0])).reshape(ROWS_SMALL, D_MODEL)


CONV_SHARD = (DEPTH, 3, 2 * D_FF // N_CHIPS)
N_CONV_SHARD = DEPTH * 3 * (2 * D_FF // N_CHIPS)


def _small_pack(g, loss):
    conv = jnp.transpose(g["conv_w"].reshape(DEPTH, 3, N_CHIPS, 2 * D_FF // N_CHIPS), (2, 0, 1, 3))
    conv = jnp.stack([_rows(conv[k], ROWS_CONV) for k in range(N_CHIPS)])
    small = jnp.broadcast_to(_small_rows(g, loss), (N_CHIPS, ROWS_SMALL, D_MODEL))
    return jnp.concatenate([conv, small], axis=1)


def _unpack_small(pack):
    out = {"conv_w": pack[:ROWS_CONV].reshape(-1)[:N_CONV_SHARD].reshape(CONV_SHARD)}
    flat = pack[ROWS_CONV:].reshape(-1)
    k = 0
    for name in SMALL_NAMES:
        shape = SMALL_SHAPES[name]
        n = 1
        for d in shape:
            n *= d
        out[name] = flat[k:k + n].reshape(shape)
        k += n
    out["extra"] = flat[k]
    return out


def _assemble_layer(gathered, shards, layer, names):
    my_chip = 2 * lax.axis_index("x") + lax.axis_index("y")
    out = {}
    for name, got, own in zip(names, gathered, shards):
        full = lax.dynamic_update_index_in_dim(got, own[layer], my_chip, 0)
        if name in ("w_in", "w_up"):
            k, r, wd = full.shape
            out[name] = jnp.transpose(full, (1, 0, 2)).reshape(r, k * wd)
        else:
            out[name] = full.reshape(-1, D_MODEL)
    return out


def _gather_weights(p):
    shards = [p[n].astype(jnp.bfloat16) for n in BIG_NAMES[:4]]
    conv_all = p["conv_w"].reshape(1, -1, p["conv_w"].shape[-1])
    got = _all_gather([shards[0], conv_all], 0, 0)
    my_chip = 2 * lax.axis_index("x") + lax.axis_index("y")
    conv = lax.dynamic_update_index_in_dim(got[1], conv_all[0], my_chip, 0)
    conv = jnp.transpose(conv.reshape((N_CHIPS,) + CONV_SHARD), (1, 2, 0, 3)).reshape(DEPTH, 3, 2 * D_FF)
    full = {n: [None, None] for n in BIG_NAMES[:4]}
    full["w_in"][0] = _assemble_layer(got[:1], shards[:1], 0, ("w_in",))["w_in"]
    full["conv_w"] = conv
    return full, dict(shards=shards, assemble=_assemble_layer)


def _halves(a):
    r = a.shape[1] // 2
    return a[:, :r], a[:, r:]


def _reduce_begin(items, names, dtypes, c_arr, tag):
    return _pair_adds(items, _grad_swap(items, "grad_swap_cores_" + tag), names, dtypes, c_arr)


def _pair_adds(items, got, names, dtypes, c_arr):
    return [_pair_add(a0, a1, r, c_arr, "grad_add_cores_" + nm, dt)
            for (a0, a1), r, nm, dt in zip(items, got, names, dtypes)]


def _reduce_end(parts, sent, names, c_arr):
    my_chip = 2 * lax.axis_index("x") + lax.axis_index("y")
    full = [lax.dynamic_update_index_in_dim(p, lax.dynamic_index_in_dim(own, my_chip, 0, keepdims=False), my_chip, 0)
            for p, own in zip(parts, sent)]
    return [_sum_chips(f, c_arr, "grad_sum_chips_" + nm) for f, nm in zip(full, names)]


EARLY_NAMES = ("w_o", "w_up", "w_down", "w_in_1")


def _early_items(grads):
    return [tuple(grads[n]) for n in ("w_o", "w_up", "w_down")] + [_halves(grads["w_in"][1])]


def kernel(x, norm1_g, w_in, sgu_norm_g, sgu_w, sgu_b, pool_w, pool_scale, mix_norm_g, w_o, norm2_g, w_up, conv_w, conv_b, w_down, final_g, loss_target, m_norm1_g, m_w_in, m_sgu_norm_g, m_sgu_w, m_sgu_b, m_pool_w, m_pool_scale, m_mix_norm_g, m_w_o, m_norm2_g, m_w_up, m_conv_w, m_conv_b, m_w_down, m_final_g, v_norm1_g, v_w_in, v_sgu_norm_g, v_sgu_w, v_sgu_b, v_pool_w, v_pool_scale, v_mix_norm_g, v_w_o, v_norm2_g, v_w_up, v_conv_w, v_conv_b, v_w_down, v_final_g):
    names = ("norm1_g", "w_in", "sgu_norm_g", "sgu_w", "sgu_b", "pool_w", "pool_scale", "mix_norm_g", "w_o",
             "norm2_g", "w_up", "conv_w", "conv_b", "w_down", "final_g")
    p = dict(zip(names, (norm1_g, w_in, sgu_norm_g, sgu_w, sgu_b, pool_w, pool_scale, mix_norm_g, w_o, norm2_g,
                         w_up, conv_w, conv_b, w_down, final_g)))
    pm = dict(zip(names, (m_norm1_g, m_w_in, m_sgu_norm_g, m_sgu_w, m_sgu_b, m_pool_w, m_pool_scale, m_mix_norm_g,
                          m_w_o, m_norm2_g, m_w_up, m_conv_w, m_conv_b, m_w_down, m_final_g)))
    pv = dict(zip(names, (v_norm1_g, v_w_in, v_sgu_norm_g, v_sgu_w, v_sgu_b, v_pool_w, v_pool_scale, v_mix_norm_g,
                          v_w_o, v_norm2_g, v_w_up, v_conv_w, v_conv_b, v_w_down, v_final_g)))
    c = lax.axis_index("c")
    gathered, late = _gather_weights(p)
    full = dict(p)
    full.update(gathered)

    c_arr = jnp.reshape(c, (1,)).astype(jnp.int32)
    early_types = [ICI_DT] * len(EARLY_NAMES)
    loss, dx, grads, (sent, received) = _local_step(
        x[0], loss_target[0], full, late,
        (_early_items, lambda items, swapped: _pair_adds(items, swapped, EARLY_NAMES, early_types, c_arr)))
    early_sums = _reduce_end(received, sent, EARLY_NAMES, c_arr)
    small_pack = _small_pack(grads, loss)
    late_names = ("w_in_0", "small")
    late_sent = _reduce_begin([_halves(grads["w_in"][0]), _halves(small_pack)], late_names, [ICI_DT, F32], c_arr, "late")
    late_sums = _reduce_end(_grad_exchange(late_sent), late_sent, late_names, c_arr)
    r_o, r_up, r_down, r_in1, r_in0, r_small = _grad_share(early_sums + late_sums)
    g = dict(w_o=r_o, w_up=r_up, w_down=r_down,
             w_in=jnp.stack([r_in0.reshape(D_MODEL, -1), r_in1.reshape(D_MODEL, -1)]))
    g.update(_unpack_small(r_small.reshape(2 * SP_HALF, D_MODEL)))
    d, nm, nv = {}, {}, {}
    for n in BIG_NAMES:
        d[n], nm[n], nv[n] = _adamw_big(g[n], p[n], pm[n], pv[n], "adamw_" + n)

    def two_d(a):
        return a.reshape(1, -1) if a.ndim == 1 else a

    ds, ms, vs = _adamw_small([two_d(g[n]) for n in SMALL_NAMES], [two_d(p[n]) for n in SMALL_NAMES],
                              [two_d(pm[n]) for n in SMALL_NAMES], [two_d(pv[n]) for n in SMALL_NAMES])
    for k, n in enumerate(SMALL_NAMES):
        d[n], nm[n], nv[n] = (a.reshape(p[n].shape) for a in (ds[k], ms[k], vs[k]))
    return (g["extra"], dx[None], *[g[n] for n in names], *[d[n] for n in names], *[nm[n] for n in names],
            *[nv[n] for n in names])
```

```python
import jax
import jax.numpy as jnp
from jax import lax
from jax.experimental import pallas as pl
from jax.experimental.pallas import tpu as pltpu

F32 = jnp.float32
MXU_DT = jnp.bfloat16

D_MODEL = 1024
DEPTH = 2
HEAD_DIM = 64
W_A = 256
W_B = 512
W_C = 256
IN_COLS = 2 * W_A + 3 * W_B + W_C
CHUNK = 128
POOL_WINDOWS = (2, 4, 8, 16)
D_FF = 2816
EPS = 1e-6
N_CHIPS = 4

ADAM_LR = 0.001
ADAM_B1 = 0.9
ADAM_B2 = 0.999
ADAM_EPS = 1e-08
ADAM_WD = 0.01
ADAM_STEP = 10

LANES = 128
TQ = 256
TK = 256
TM = 256
TM_MM = 512
HALO = 16
VMEM_LIMIT = 56 * 1024 * 1024

ROWS_CONV = 16
ROWS_SMALL = 240
SP_HALF = (ROWS_CONV + ROWS_SMALL) // 2
ICI_DT = jnp.bfloat16

BIG_NAMES = ("w_in", "w_o", "w_up", "w_down", "conv_w")
SMALL_NAMES = ("norm1_g", "sgu_norm_g", "sgu_w", "sgu_b", "pool_w", "pool_scale",
               "mix_norm_g", "norm2_g", "conv_b", "final_g")
SMALL_SHAPES = {
    "norm1_g": (DEPTH, D_MODEL), "sgu_norm_g": (DEPTH, W_A), "sgu_w": (DEPTH, 4, CHUNK, CHUNK),
    "sgu_b": (DEPTH, 4, CHUNK), "pool_w": (DEPTH, 4, 64, 64), "pool_scale": (DEPTH, W_C),
    "mix_norm_g": (DEPTH, D_MODEL), "norm2_g": (DEPTH, D_MODEL), "conv_b": (DEPTH, 2 * D_FF),
    "final_g": (D_MODEL,),
}


def _call(body, **kw):
    return pl.pallas_call(body, **kw)


def _params(*sem):
    return pltpu.CompilerParams(dimension_semantics=sem, vmem_limit_bytes=VMEM_LIMIT)


def _dot(a, b):
    return jnp.dot(a, b, preferred_element_type=F32)


def _dot_nt(a, b):
    return lax.dot_general(a, b, (((1,), (1,)), ((), ())), preferred_element_type=F32)


def _dot_tn(a, b):
    return lax.dot_general(a, b, (((0,), (0,)), ((), ())), preferred_element_type=F32)


def _group_mean(sq, gmat):
    sqb = sq.astype(MXU_DT)
    cols = [_dot(sqb[:, b * LANES:(b + 1) * LANES], gmat) for b in range(sq.shape[1] // LANES)]
    return cols[0] if len(cols) == 1 else jnp.concatenate(cols, axis=-1)


def _group_matrix():
    r = jnp.arange(LANES)
    return jnp.where((r[:, None] // HEAD_DIM) == (r[None, :] // HEAD_DIM), 1.0 / HEAD_DIM, 0.0).astype(MXU_DT)


def _tile(n):
    return max(t for t in range(LANES, 1536 + 1, LANES) if n % t == 0)


def _row_spec(tm, cols, col_block=0):
    return pl.BlockSpec((tm, cols), lambda i, cb=col_block: (i, cb))


def _full_spec(shape):
    nd = len(shape)
    return pl.BlockSpec(shape, lambda *_: (0,) * nd)


def _mm_res(a, w, res, name):
    S, K = a.shape
    N = w.shape[1]
    tm = TM_MM

    def body(a_ref, w_ref, r_ref, o_ref):
        o_ref[...] = r_ref[...] + _dot(a_ref[...], w_ref[...])

    return _call(
        body, name=name, grid=(S // tm,),
        in_specs=[_row_spec(tm, K), _full_spec((K, N)), _row_spec(tm, N)],
        out_specs=_row_spec(tm, N),
        out_shape=jax.ShapeDtypeStruct((S, N), F32),
        compiler_params=_params("parallel"),
    )(a, w, res)


def _mm_tn(a, b, name, col_tiles=False):
    S, K1 = a.shape
    N = b.shape[1]
    ts = min(4 * TM_MM, S)
    tk = _tile(K1)
    tn = _tile(N)
    if col_tiles:
        out_spec = pl.BlockSpec((None, tk, tn), lambda m, n, s: (n, m, 0))
        out_shape = jax.ShapeDtypeStruct((N // tn, K1, tn), F32)
    else:
        out_spec = pl.BlockSpec((tk, tn), lambda m, n, s: (m, n))
        out_shape = jax.ShapeDtypeStruct((K1, N), F32)

    def body(a_ref, b_ref, o_ref):
        @pl.when(pl.program_id(2) == 0)
        def _():
            o_ref[...] = jnp.zeros_like(o_ref)

        o_ref[...] += _dot_tn(a_ref[...], b_ref[...].astype(MXU_DT))

    return _call(
        body, name=name, grid=(K1 // tk, N // tn, S // ts),
        in_specs=[pl.BlockSpec((ts, tk), lambda m, n, s: (s, m)),
                  pl.BlockSpec((ts, tn), lambda m, n, s: (s, n))],
        out_specs=out_spec, out_shape=out_shape,
        compiler_params=_params("parallel", "parallel", "arbitrary"),
    )(a, b)


def _down_proj_loss(a, w, res, g, tgt):
    S, D = res.shape
    K = a.shape[1]
    tm = TM_MM

    def body(a_ref, w_ref, res_ref, g_ref, t_ref, dx_ref, dg_ref, l_ref):
        xv = res_ref[...] + _dot(a_ref[...], w_ref[...])
        r = lax.rsqrt(jnp.mean(xv * xv, axis=-1, keepdims=True) + EPS)
        xhat = xv * r
        diff = xhat * g_ref[...] - t_ref[...]

        @pl.when(pl.program_id(0) == 0)
        def _():
            dg_ref[...] = jnp.zeros_like(dg_ref)
            l_ref[...] = jnp.zeros_like(l_ref)

        l_ref[...] += jnp.full(l_ref.shape, 0.5 * jnp.sum(jnp.mean(diff * diff, axis=-1, keepdims=True)), F32)
        dout = diff * (1.0 / D)
        dg_ref[...] += jnp.sum(dout * xhat, axis=0, keepdims=True)
        dxh = dout * g_ref[...]
        dx_ref[...] = r * (dxh - xhat * jnp.mean(dxh * xhat, axis=-1, keepdims=True))

    return _call(
        body, name="down_proj_loss", grid=(S // tm,),
        in_specs=[_row_spec(tm, K), _full_spec((K, D)), _row_spec(tm, D), _full_spec((1, D)), _row_spec(tm, D)],
        out_specs=[_row_spec(tm, D), _full_spec((1, D)), _full_spec((8, LANES))],
        out_shape=[jax.ShapeDtypeStruct((S, D), F32), jax.ShapeDtypeStruct((1, D), F32),
                   jax.ShapeDtypeStruct((8, LANES), F32)],
        compiler_params=_params("arbitrary"),
    )(a, w, res, g, tgt)


def _mix_out(ya, yb, yc, gm, wo, x, gmat):
    S = x.shape[0]
    tm = TM_MM

    def body(ya_ref, yb_ref, yc_ref, gm_ref, wo_ref, x_ref, gmat_ref, x2_ref, yn_ref):
        y = jnp.concatenate([ya_ref[...], yb_ref[...], yc_ref[...]], axis=-1)
        r = lax.rsqrt(_group_mean(y * y, gmat_ref[...]) + EPS)
        yn = (y * r * gm_ref[...]).astype(MXU_DT)
        yn_ref[...] = yn
        x2_ref[...] = x_ref[...] + _dot(yn, wo_ref[...])

    return _call(
        body, name="mix_out", grid=(S // tm,),
        in_specs=[_row_spec(tm, W_A), _row_spec(tm, W_B), _row_spec(tm, W_C), _full_spec((1, D_MODEL)),
                  _full_spec((D_MODEL, D_MODEL)), _row_spec(tm, D_MODEL), _full_spec((LANES, LANES))],
        out_specs=[_row_spec(tm, D_MODEL), _row_spec(tm, D_MODEL)],
        out_shape=[jax.ShapeDtypeStruct((S, D_MODEL), F32), jax.ShapeDtypeStruct((S, D_MODEL), MXU_DT)],
        compiler_params=_params("parallel"),
    )(ya, yb, yc, gm, wo, x, gmat)


def _mix_out_bwd(dx2, wo, ya, yb, yc, gm, gmat, carry_swap=None):
    S = dx2.shape[0]
    tm = TM_MM
    nb = S // tm
    items = list(carry_swap or [])
    ns = len(items)

    def body(*refs):
        dx2_ref, wo_ref, ya_ref, yb_ref, yc_ref, gm_ref, gmat_ref = refs[:7]
        dya_ref, dyb_ref, dyc_ref, dgm_ref = refs[7 + 2 * ns:11 + 2 * ns]
        if ns:
            start, finish = _swap_plan(refs[7:7 + ns], refs[7 + ns:7 + 2 * ns], refs[11 + 2 * ns:11 + 3 * ns],
                                       refs[11 + 3 * ns], refs[12 + 3 * ns])

            @pl.when(pl.program_id(0) == 0)
            def _():
                start()

        dyn = _dot_nt(dx2_ref[...].astype(MXU_DT), wo_ref[...])
        y = jnp.concatenate([ya_ref[...], yb_ref[...], yc_ref[...]], axis=-1)
        r = lax.rsqrt(_group_mean(y * y, gmat_ref[...]) + EPS)
        yhat = y * r

        @pl.when(pl.program_id(0) == 0)
        def _():
            dgm_ref[...] = jnp.zeros_like(dgm_ref)

        dgm_ref[...] += jnp.sum(dyn * yhat, axis=0, keepdims=True)
        dyh = dyn * gm_ref[...]
        dy = r * (dyh - yhat * _group_mean(dyh * yhat, gmat_ref[...]))
        dya_ref[...] = dy[:, :W_A]
        dyb_ref[...] = dy[:, W_A:W_A + W_B]
        dyc_ref[...] = dy[:, W_A + W_B:]

        if ns:
            @pl.when(pl.program_id(0) == nb - 1)
            def _():
                finish()

    swapped, sems = _swap_shapes(items) if ns else ([], [])
    return _call(
        body, name="mix_out_bwd_swap" if ns else "mix_out_bwd", grid=(nb,),
        in_specs=[_row_spec(tm, D_MODEL), _full_spec((D_MODEL, D_MODEL)), _row_spec(tm, W_A), _row_spec(tm, W_B),
                  _row_spec(tm, W_C), _full_spec((1, D_MODEL)), _full_spec((LANES, LANES))] + [ANY] * (2 * ns),
        out_specs=[_row_spec(tm, W_A), _row_spec(tm, W_B), _row_spec(tm, W_C), _full_spec((1, D_MODEL))]
        + [ANY] * ns,
        out_shape=[jax.ShapeDtypeStruct((S, W_A), F32), jax.ShapeDtypeStruct((S, W_B), F32),
                   jax.ShapeDtypeStruct((S, W_C), F32), jax.ShapeDtypeStruct((1, D_MODEL), F32)] + swapped,
        scratch_shapes=sems,
        compiler_params=_params("arbitrary"),
    )(dx2, wo, ya, yb, yc, gm, gmat, *[a0 for a0, _ in items], *[a1 for _, a1 in items])


_SQRT_HALF = 0.7071067811865476
_INV_SQRT_2PI = 0.3989422804014327


def _sgu_common(a, sng, wm_ref, bias, gmat):
    phi = 0.5 * (1.0 + lax.erf(a * _SQRT_HALF))
    ga = a * phi
    u = ga[:, :W_A]
    v = ga[:, W_A:]
    r = lax.rsqrt(_group_mean(v * v, gmat) + EPS)
    vhat = v * r
    vn = (vhat * sng).astype(MXU_DT)
    head = lax.broadcasted_iota(jnp.int32, (CHUNK, W_A), 1) // HEAD_DIM
    rows = []
    for c in range(a.shape[0] // CHUNK):
        vc = vn[c * CHUNK:(c + 1) * CHUNK]
        s = bias
        for h in range(4):
            s = s + jnp.where(head == h, _dot(wm_ref[h], vc), 0.0)
        rows.append(s)
    s = jnp.concatenate(rows, axis=0)
    return phi, u, r, vhat, vn, s


def _tril_weights(sgu_w_l):
    t = jnp.arange(CHUNK)
    return jnp.where((t[None, :] <= t[:, None])[None], sgu_w_l, 0.0)


def _sgu_in_proj_bwd(proj, dy, sng, wm, wmt, bias, gmat, rest, w, x, g, dres):
    S, D = x.shape
    tm = TM_MM

    def body(a_ref, dy_ref, sng_ref, wm_ref, wmt_ref, b_ref, gmat_ref, rest_ref, w_ref, x_ref, g_ref, r_ref,
             da_ref, dw_ref, db_ref, dsng_ref, dx_ref, dg_ref):
        a = a_ref[...]
        dy = dy_ref[...]
        gmat = gmat_ref[...]
        sng = sng_ref[...]
        phi, u, r, vhat, vn, s = _sgu_common(a, sng, wm_ref, b_ref[...], gmat)
        du = dy * s
        ds = dy * u

        @pl.when(pl.program_id(0) == 0)
        def _():
            dw_ref[...] = jnp.zeros_like(dw_ref)
            db_ref[...] = jnp.zeros_like(db_ref)
            dsng_ref[...] = jnp.zeros_like(dsng_ref)
            dg_ref[...] = jnp.zeros_like(dg_ref)

        head = lax.broadcasted_iota(jnp.int32, (CHUNK, W_A), 1) // HEAD_DIM
        tt = lax.broadcasted_iota(jnp.int32, (CHUNK, CHUNK), 0)
        ss = lax.broadcasted_iota(jnp.int32, (CHUNK, CHUNK), 1)
        rows = []
        for c in range(tm // CHUNK):
            dsc = ds[c * CHUNK:(c + 1) * CHUNK]
            vc = vn[c * CHUNK:(c + 1) * CHUNK]
            db_ref[...] += dsc
            dsb = dsc.astype(MXU_DT)
            dvn = jnp.zeros((CHUNK, W_A), F32)
            for h in range(4):
                dvn = dvn + jnp.where(head == h, _dot(wmt_ref[h], dsb), 0.0)
                dsh = jnp.where(head == h, dsc, 0.0).astype(MXU_DT)
                dw_ref[h] += jnp.where(ss <= tt, _dot_nt(dsh, vc), 0.0)
            rows.append(dvn)
        dvn = jnp.concatenate(rows, axis=0)
        dsng_ref[...] += jnp.sum(dvn * vhat, axis=0, keepdims=True)
        dvh = dvn * sng
        dv = r * (dvh - vhat * _group_mean(dvh * vhat, gmat))
        dga = jnp.concatenate([du, dv], axis=-1)
        dgelu = phi + a * (_INV_SQRT_2PI * jnp.exp(-0.5 * a * a))
        dab = (dga * dgelu).astype(da_ref.dtype)
        da_ref[...] = dab

        dh = _dot_nt(dab, w_ref[:, :2 * W_A]) + _dot_nt(rest_ref[...], w_ref[:, 2 * W_A:])
        xv = x_ref[...]
        rx = lax.rsqrt(jnp.mean(xv * xv, axis=-1, keepdims=True) + EPS)
        xhat = xv * rx
        dg_ref[...] += jnp.sum(dh * xhat, axis=0, keepdims=True)
        dxh = dh * g_ref[...]
        dx_ref[...] = r_ref[...] + rx * (dxh - xhat * jnp.mean(dxh * xhat, axis=-1, keepdims=True))

    n_rest = IN_COLS - 2 * W_A
    return _call(
        body, name="sgu_in_proj_bwd", grid=(S // tm,),
        in_specs=[_row_spec(tm, 2 * W_A), _row_spec(tm, W_A), _full_spec((1, W_A)), _full_spec((4, CHUNK, CHUNK)),
                  _full_spec((4, CHUNK, CHUNK)), _full_spec((CHUNK, W_A)), _full_spec((LANES, LANES)),
                  _row_spec(tm, n_rest),
                  pl.BlockSpec((D, IN_COLS), lambda i: (0, 0), pipeline_mode=pl.Buffered(1)),
                  _row_spec(tm, D), _full_spec((1, D)), _row_spec(tm, D)],
        out_specs=[_row_spec(tm, 2 * W_A), _full_spec((4, CHUNK, CHUNK)), _full_spec((CHUNK, W_A)),
                   _full_spec((1, W_A)), _row_spec(tm, D), _full_spec((1, D))],
        out_shape=[jax.ShapeDtypeStruct((S, 2 * W_A), MXU_DT), jax.ShapeDtypeStruct((4, CHUNK, CHUNK), F32),
                   jax.ShapeDtypeStruct((CHUNK, W_A), F32), jax.ShapeDtypeStruct((1, W_A), F32),
                   jax.ShapeDtypeStruct((S, D), F32), jax.ShapeDtypeStruct((1, D), F32)],
        compiler_params=_params("arbitrary"),
    )(proj, dy, sng, wm, wmt, bias, gmat, rest, w, x, g, dres)


HG = 4
LW = HG * HEAD_DIM
Q_BLK0 = (2 * W_A) // LW
K_BLK0 = Q_BLK0 + W_B // LW
V_BLK0 = K_BLK0 + W_B // LW
N_GROUPS = W_B // LW
EXP_IS_ZERO_BELOW = -120.0


def _tri_matrix():
    r = jnp.arange(TK)
    return (r[:, None] > r[None, :]).astype(MXU_DT)


def _stack_heads(a):
    head = lax.broadcasted_iota(jnp.int32, a.shape, 1) // HEAD_DIM
    return jnp.concatenate([jnp.where(head == h, a, 0.0) for h in range(HG)], axis=0).astype(MXU_DT)


def _unstack_heads(a):
    head = lax.broadcasted_iota(jnp.int32, (TQ, LW), 1) // HEAD_DIM
    out = a[:TQ]
    for h in range(1, HG):
        out = jnp.where(head == h, a[h * TQ:(h + 1) * TQ], out)
    return out


def _sb_scores(q2, kj, tri, key_offset):
    z = _dot_nt(q2, kj)
    sp = jnp.log(1.0 + jnp.exp(-jnp.abs(z)))
    lsp = jnp.minimum(z, 0.0) - sp
    lsm = lsp - z
    msk = None
    if key_offset is not None:
        row = lax.broadcasted_iota(jnp.int32, z.shape, 0) & (TQ - 1)
        col = lax.broadcasted_iota(jnp.int32, z.shape, 1) + key_offset
        msk = col < row
        lsm = jnp.where(msk, lsm, 0.0)
    tail = _dot(lsm.astype(MXU_DT), tri)
    return lsp, lsm, tail, msk


def _sb_fwd(proj_b, tri, carry_gather=None):
    S = proj_b.shape[0]
    nq = S // TQ
    kpq = TQ // TK
    assert S // TK < LANES
    shards = list(carry_gather or [])
    ng = len(shards)

    def body(*refs):
        q_ref, k_ref, v_ref, tri_ref = refs[:4]
        o_ref, rb_ref = refs[4 + ng:6 + ng]
        acc_ref = refs[6 + 2 * ng]
        i = pl.program_id(1)
        if ng:
            step = pl.program_id(0) * nq + i
            sender = 1
            c, sends, arrivals, forwards = _gather_plan(refs[4:4 + ng], refs[6 + ng:6 + 2 * ng], refs[7 + 2 * ng],
                                                        refs[8 + 2 * ng], 1, sender)

            @pl.when(jnp.logical_and(step == 0, c == sender))
            def _():
                for cp in sends:
                    cp.start()

            @pl.when(jnp.logical_and(step == N_GROUPS * nq - max(nq // 8, 1), c == sender))
            def _():
                for arrived, onward in zip(arrivals, forwards):
                    arrived.wait_recv()
                    onward.start()


        lane2 = lax.broadcasted_iota(jnp.int32, (HG * TQ, LANES), 1)
        q2 = _stack_heads(q_ref[...].astype(F32) * (HEAD_DIM ** -0.5))
        tri = tri_ref[...]
        rb_ref[...] = jnp.zeros_like(rb_ref)

        def block(j, run, key_offset=None, first=False):
            start = pl.multiple_of(j * TK, TK)
            kj = k_ref[pl.ds(start, TK), :]
            vj = v_ref[pl.ds(start, TK), :]
            lsp, lsm, tail, msk = _sb_scores(q2, kj, tri, key_offset)
            rb_ref[...] = jnp.where(lane2 == j, run, rb_ref[...])
            att = jnp.exp(lsp + tail + run)
            if msk is not None:
                att = jnp.where(msk, att, 0.0)
            pv = _dot(att.astype(MXU_DT), vj)
            if first:
                acc_ref[...] = pv
            else:
                acc_ref[...] += pv
            return run + tail[:, :1] + lsm[:, :1]

        past = i * kpq

        def overlapping():
            run = jnp.zeros((HG * TQ, 1), F32)
            for d in reversed(range(kpq)):
                run = block(i * kpq + d, run, key_offset=d * TK, first=(d == kpq - 1))
            return run

        def alive(run):
            return (jnp.max(run) > EXP_IS_ZERO_BELOW).astype(jnp.int32)

        def walk(carry):
            n, run, _ = carry
            run = block(past - 1 - n, run)
            return n + 1, run, alive(run)

        n, run = lax.cond(past > 0, lambda: (jnp.int32(1), block(past - 1, overlapping())),
                          lambda: (jnp.int32(0), overlapping()))
        n, _, _ = lax.while_loop(lambda s: jnp.logical_and(s[0] < past, s[2] > 0), walk, (n, run, alive(run)))
        rb_ref[...] = jnp.where(lane2 == LANES - 1, n.astype(F32), rb_ref[...])
        o_ref[...] = _unstack_heads(acc_ref[...])

        if ng:
            @pl.when(jnp.logical_and(step == N_GROUPS * nq - 1, c == sender))
            def _():
                for cp in sends + forwards:
                    cp.wait_send()

            @pl.when(jnp.logical_and(step == N_GROUPS * nq - 1, c != sender))
            def _():
                for cp in forwards:
                    cp.wait_recv()

    once = pl.Buffered(1)
    gathered, sems = _gather_shapes(shards) if ng else ([], [])
    return _call(
        body, name="sb_fwd_gather" if ng else "sb_fwd", grid=(N_GROUPS, nq),
        in_specs=[pl.BlockSpec((TQ, LW), lambda p, i: (i, Q_BLK0 + p)),
                  pl.BlockSpec((S, LW), lambda p, i: (0, K_BLK0 + p), pipeline_mode=once),
                  pl.BlockSpec((S, LW), lambda p, i: (0, V_BLK0 + p), pipeline_mode=once),
                  pl.BlockSpec((TK, TK), lambda p, i: (0, 0))] + [ANY] * ng,
        out_specs=[pl.BlockSpec((TQ, LW), lambda p, i: (i, p)),
                   pl.BlockSpec((None, None, HG * TQ, LANES), lambda p, i: (p, i, 0, 0))] + [ANY] * ng,
        out_shape=[jax.ShapeDtypeStruct((S, W_B), F32),
                   jax.ShapeDtypeStruct((N_GROUPS, nq, HG * TQ, LANES), F32)] + gathered,
        scratch_shapes=[pltpu.VMEM((HG * TQ, LW), F32)] + sems,
        compiler_params=_params("arbitrary", "arbitrary"),
    )(proj_b, proj_b, proj_b, tri, *shards)


def _sb_bwd(proj_b, dyb, rb, tri, trit, carry_exchange=None):
    S = proj_b.shape[0]
    nq = S // TQ
    kpq = TQ // TK
    hs = list(carry_exchange or [])
    ne = len(hs)

    def body(*refs):
        q_ref, k_ref, v_ref, do_ref, rb_ref, tri_ref, trit_ref = refs[:7]
        dq_ref, dk_acc, dv_acc = refs[7 + ne:10 + ne]
        dq_acc = refs[10 + 2 * ne]
        i = pl.program_id(1)
        if ne:
            tick = pl.program_id(0) * nq + i
            sends, arrivals = _exchange_plan(refs[7:7 + ne], refs[10 + ne:10 + 2 * ne], refs[11 + 2 * ne],
                                             refs[12 + 2 * ne])

            @pl.when(tick == 0)
            def _():
                for cp in sends:
                    cp.start()

        lane2 = lax.broadcasted_iota(jnp.int32, (HG * TQ, LANES), 1)
        scale = HEAD_DIM ** -0.5
        q2 = _stack_heads(q_ref[...].astype(F32) * scale)
        do2 = _stack_heads(do_ref[...])
        tri = tri_ref[...]
        trit = trit_ref[...]

        @pl.when(i == 0)
        def _():
            dk_acc[...] = jnp.zeros_like(dk_acc)
            dv_acc[...] = jnp.zeros_like(dv_acc)

        dq_acc[...] = jnp.zeros_like(dq_acc)

        def block(j, pre, key_offset=None):
            start = pl.multiple_of(j * TK, TK)
            kj = k_ref[pl.ds(start, TK), :]
            vj = v_ref[pl.ds(start, TK), :]
            lsp, lsm, tail, msk = _sb_scores(q2, kj, tri, key_offset)
            run = jnp.sum(jnp.where(lane2 == j, rb_ref[...], 0.0), axis=-1, keepdims=True)
            att = jnp.exp(lsp + tail + run)
            if msk is not None:
                att = jnp.where(msk, att, 0.0)
            beta = jnp.exp(lsp)
            dl = _dot_nt(do2, vj) * att
            cin = _dot(dl.astype(MXU_DT), trit)
            dz = dl * (1.0 - beta) - beta * (pre + cin)
            if msk is not None:
                dz = jnp.where(msk, dz, 0.0)
            dzb = dz.astype(MXU_DT)
            dq_acc[...] += _dot(dzb, kj)
            dk_acc[pl.ds(start, TK), :] += _dot_tn(dzb, q2)
            dv_acc[pl.ds(start, TK), :] += _dot_tn(att.astype(MXU_DT), do2)
            return pre + cin[:, TK - 1:] + dl[:, TK - 1:]

        past = i * kpq
        walked = jnp.max(jnp.where(lane2[:8] == LANES - 1, rb_ref[pl.ds(0, 8), :], 0.0)).astype(jnp.int32)
        walked = jnp.clip(walked, 0, past)
        def overlapping(pre):
            for d in range(kpq):
                pre = block(i * kpq + d, pre, key_offset=d * TK)
            return jnp.int32(0)

        def with_past():
            pre = lax.fori_loop(past - walked, past - 1, lambda j, pre: block(j, pre),
                                jnp.zeros((HG * TQ, 1), F32))
            return overlapping(block(past - 1, pre))

        lax.cond(walked > 0, with_past, lambda: overlapping(jnp.zeros((HG * TQ, 1), F32)))
        dq_ref[...] = (_unstack_heads(dq_acc[...]) * scale).astype(dq_ref.dtype)

        if ne:
            @pl.when(tick == N_GROUPS * nq - 1)
            def _():
                for cp in arrivals:
                    cp.wait_recv()
                for cp in sends:
                    cp.wait_send()

    once = pl.Buffered(1)
    exchanged, sems = _exchange_shapes(hs) if ne else ([], [])
    return _call(
        body, name="sb_bwd_exchange" if ne else "sb_bwd", grid=(N_GROUPS, nq),
        in_specs=[pl.BlockSpec((TQ, LW), lambda p, i: (i, Q_BLK0 + p)),
                  pl.BlockSpec((S, LW), lambda p, i: (0, K_BLK0 + p), pipeline_mode=once),
                  pl.BlockSpec((S, LW), lambda p, i: (0, V_BLK0 + p), pipeline_mode=once),
                  pl.BlockSpec((TQ, LW), lambda p, i: (i, p)),
                  pl.BlockSpec((None, None, HG * TQ, LANES), lambda p, i: (p, i, 0, 0)),
                  pl.BlockSpec((TK, TK), lambda p, i: (0, 0)),
                  pl.BlockSpec((TK, TK), lambda p, i: (0, 0))] + [ANY] * ne,
        out_specs=[pl.BlockSpec((TQ, LW), lambda p, i: (i, p)),
                   pl.BlockSpec((S, LW), lambda p, i: (0, p), pipeline_mode=once),
                   pl.BlockSpec((S, LW), lambda p, i: (0, p), pipeline_mode=once)] + [ANY] * ne,
        out_shape=[jax.ShapeDtypeStruct((S, W_B), MXU_DT), jax.ShapeDtypeStruct((S, W_B), F32),
                   jax.ShapeDtypeStruct((S, W_B), F32)] + exchanged,
        scratch_shapes=[pltpu.VMEM((HG * TQ, LW), F32)] + sems,
        compiler_params=_params("arbitrary", "arbitrary"),
    )(proj_b, proj_b, proj_b, dyb, rb, tri, trit, *hs)


P_BLK = (2 * W_A + 3 * W_B) // W_C


def _window_lanes():
    g = lax.broadcasted_iota(jnp.int32, (1, W_C), 1) // (W_C // 4)
    w = jnp.where(g == 0, POOL_WINDOWS[0], jnp.where(g == 1, POOL_WINDOWS[1],
                  jnp.where(g == 2, POOL_WINDOWS[2], POOL_WINDOWS[3])))
    return g, w


def _shift_rows(ext, k, tm, lead):
    n = ext.shape[0]
    return pltpu.roll(ext, shift=k % n, axis=0)[lead:lead + tm]


def _pool_diff(p_cur, p_halo, row0, tm):
    ext = jnp.concatenate([p_halo, p_cur], axis=0)
    g, w = _window_lanes()
    acc = ext
    sums = []
    for sh in (1, 2, 4, 8):
        acc = acc + pltpu.roll(acc, shift=sh, axis=0)
        sums.append(acc[HALO:HALO + tm])
    wsum = jnp.where(g == 0, sums[0], jnp.where(g == 1, sums[1], jnp.where(g == 2, sums[2], sums[3])))
    pos = (row0 + 1 + lax.broadcasted_iota(jnp.int32, (tm, W_C), 0)).astype(F32)
    cnt = jnp.minimum(pos, w.astype(F32))
    return wsum / cnt - p_cur, cnt


def _pool_specs(tm, nrow_blocks_halo):
    cur = pl.BlockSpec((tm, W_C), lambda i: (i, P_BLK))
    prev = pl.BlockSpec((HALO, W_C), lambda i: (jnp.maximum(i * (tm // HALO) - 1, 0), P_BLK))
    return cur, prev


def _in_proj_groups(x, g, w, sng, wm, bias, gmat, wbd, scale, carry_gather=None):
    S, D = x.shape
    tm = TM_MM
    nb = S // tm
    p0 = 2 * W_A + 3 * W_B
    qkv_chunk = 3 * W_B // 2
    shards = list(carry_gather or [])
    ng = len(shards)

    def body(*refs):
        x_ref, g_ref, w_ref, sng_ref, wm_ref, b_ref, gmat_ref, wbd_ref, sc_ref = refs[:9]
        h_ref, o_ref, ob_ref, ya_ref, yc_ref = refs[9 + ng:14 + ng]
        tail_ref = refs[14 + 2 * ng]
        i = pl.program_id(0)
        if ng:
            sender = 0
            c, sends, arrivals, forwards = _gather_plan(refs[9:9 + ng], refs[14 + ng:14 + 2 * ng], refs[15 + 2 * ng],
                                                        refs[16 + 2 * ng], 0, sender)

            @pl.when(jnp.logical_and(i == 0, c == sender))
            def _():
                for cp in sends:
                    cp.start()
        xv = x_ref[...]
        r = lax.rsqrt(jnp.mean(xv * xv, axis=-1, keepdims=True) + EPS)
        h = (xv * r * g_ref[...]).astype(h_ref.dtype)
        h_ref[...] = h

        def project(c0, c1):
            acc = _dot(h, w_ref[:, c0:c1])
            o_ref[:, c0:c1] = acc
            ob_ref[:, c0:c1] = acc.astype(ob_ref.dtype)
            return acc

        a = project(0, 2 * W_A)
        _, u, _, _, _, s = _sgu_common(a, sng_ref[...], wm_ref, b_ref[...], gmat_ref[...])
        ya_ref[...] = u * s
        for c0 in range(2 * W_A, p0, qkv_chunk):
            project(c0, c0 + qkv_chunk)
        p = project(p0, p0 + W_C)
        halo = jnp.where(i > 0, tail_ref[...], 0.0)
        tail_ref[...] = p[tm - HALO:]
        d, _ = _pool_diff(p, halo, i * tm, tm)
        yc_ref[...] = _dot(d.astype(MXU_DT), wbd_ref[...]) * sc_ref[...]

        if ng:
            @pl.when(jnp.logical_and(i == nb - 1, c == sender))
            def _():
                for arrived, onward in zip(arrivals, forwards):
                    arrived.wait_recv()
                    onward.start()
                for cp in sends + forwards:
                    cp.wait_send()

            @pl.when(jnp.logical_and(i == nb - 1, c != sender))
            def _():
                for cp in forwards:
                    cp.wait_recv()

    gathered, sems = _gather_shapes(shards) if ng else ([], [])
    return _call(
        body, name="in_proj_groups_gather" if ng else "in_proj_groups", grid=(nb,),
        in_specs=[_row_spec(tm, D), _full_spec((1, D)),
                  pl.BlockSpec((D, IN_COLS), lambda i: (0, 0), pipeline_mode=pl.Buffered(1)),
                  _full_spec((1, W_A)), _full_spec((4, CHUNK, CHUNK)), _full_spec((CHUNK, W_A)),
                  _full_spec((LANES, LANES)), _full_spec((W_C, W_C)), _full_spec((1, W_C))] + [ANY] * ng,
        out_specs=[_row_spec(tm, D), _row_spec(tm, IN_COLS), _row_spec(tm, IN_COLS), _row_spec(tm, W_A),
                   _row_spec(tm, W_C)] + [ANY] * ng,
        out_shape=[jax.ShapeDtypeStruct((S, D), MXU_DT), jax.ShapeDtypeStruct((S, IN_COLS), F32),
                   jax.ShapeDtypeStruct((S, IN_COLS), MXU_DT), jax.ShapeDtypeStruct((S, W_A), F32),
                   jax.ShapeDtypeStruct((S, W_C), F32)] + gathered,
        scratch_shapes=[pltpu.VMEM((HALO, W_C), F32)] + sems,
        compiler_params=_params("arbitrary"),
    )(x, g, w, sng, wm, bias, gmat, wbd, scale, *shards)


def _pool_bwd_a(proj, dy, wbd, scale):
    S = proj.shape[0]
    tm = TM_MM

    def body(p_ref, ph_ref, dy_ref, w_ref, sc_ref, dd_ref, e_ref, dw_ref, dsc_ref):
        i = pl.program_id(0)
        halo = jnp.where(i > 0, ph_ref[...], 0.0)
        d, cnt = _pool_diff(p_ref[...], halo, i * tm, tm)
        db = d.astype(MXU_DT)
        dy = dy_ref[...]

        @pl.when(i == 0)
        def _():
            dw_ref[...] = jnp.zeros_like(dw_ref)
            dsc_ref[...] = jnp.zeros_like(dsc_ref)

        dsc_ref[...] += jnp.sum(dy * _dot(db, w_ref[...]), axis=0, keepdims=True)
        dys = (dy * sc_ref[...]).astype(MXU_DT)
        dw_ref[...] += _dot_tn(db, dys)
        dd = _dot_nt(dys, w_ref[...])
        dd_ref[...] = dd
        e_ref[...] = dd / cnt

    cur, prev = _pool_specs(tm, S // HALO)
    return _call(
        body, name="pool_bwd_a", grid=(S // tm,),
        in_specs=[cur, prev, _row_spec(tm, W_C), _full_spec((W_C, W_C)), _full_spec((1, W_C))],
        out_specs=[_row_spec(tm, W_C), _row_spec(tm, W_C), _full_spec((W_C, W_C)), _full_spec((1, W_C))],
        out_shape=[jax.ShapeDtypeStruct((S, W_C), F32), jax.ShapeDtypeStruct((S, W_C), F32),
                   jax.ShapeDtypeStruct((W_C, W_C), F32), jax.ShapeDtypeStruct((1, W_C), F32)],
        compiler_params=_params("arbitrary"),
    )(proj, proj, dy, wbd, scale)


def _pool_bwd_b(dd, e):
    S = dd.shape[0]
    tm = TM_MM
    nb = S // tm

    def body(dd_ref, e_ref, en_ref, dp_ref):
        i = pl.program_id(0)
        halo = jnp.where(i < nb - 1, en_ref[...], 0.0)
        ext = jnp.concatenate([e_ref[...], halo], axis=0)
        n = ext.shape[0]
        g, _ = _window_lanes()
        acc = ext
        sums = []
        for sh in (1, 2, 4, 8):
            acc = acc + pltpu.roll(acc, shift=n - sh, axis=0)
            sums.append(acc[:tm])
        wsum = jnp.where(g == 0, sums[0], jnp.where(g == 1, sums[1], jnp.where(g == 2, sums[2], sums[3])))
        dp_ref[...] = (wsum - dd_ref[...]).astype(dp_ref.dtype)

    nxt = pl.BlockSpec((HALO, W_C), lambda i: (jnp.minimum((i + 1) * (tm // HALO), S // HALO - 1), 0))
    return _call(
        body, name="pool_bwd_b", grid=(nb,),
        in_specs=[_row_spec(tm, W_C), _row_spec(tm, W_C), nxt],
        out_specs=_row_spec(tm, W_C),
        out_shape=jax.ShapeDtypeStruct((S, W_C), MXU_DT),
        compiler_params=_params("parallel"),
    )(dd, e, e)


TN_FF = 1408
NB_FF = D_FF // TN_FF
CONV_ROWS = 8


def _conv(z_cur, z_halo, cwb, tm):
    ext = jnp.concatenate([z_halo, z_cur], axis=0)
    z2 = _shift_rows(ext, 2, tm, HALO)
    z1 = _shift_rows(ext, 1, tm, HALO)
    zc = cwb[3:4] + z2 * cwb[0:1] + z1 * cwb[1:2] + z_cur * cwb[2:3]
    return zc, z2, z1


def _up_proj_gate(x, g, w, cwb):
    S, D = x.shape
    tm = TM_MM

    def body(x_ref, g_ref, w_ref, c_ref, h_ref, z_ref, zc_ref, f_ref, tail_ref):
        first = pl.program_id(0) == 0
        xv = x_ref[...]
        r = lax.rsqrt(jnp.mean(xv * xv, axis=-1, keepdims=True) + EPS)
        h = (xv * r * g_ref[...]).astype(h_ref.dtype)
        h_ref[...] = h
        for j in range(NB_FF):
            halves = []
            for col0 in (j * TN_FF, D_FF + j * TN_FF):
                zb = _dot(h, w_ref[:, col0:col0 + TN_FF]).astype(z_ref.dtype)
                z_ref[:, col0:col0 + TN_FF] = zb
                zf = zb.astype(F32)
                prev = jnp.where(first, 0.0, tail_ref[:, col0:col0 + TN_FF])
                tail_ref[:, col0:col0 + TN_FF] = zf[tm - HALO:]
                zc = _conv(zf, prev, c_ref[:, col0:col0 + TN_FF], tm)[0]
                zc_ref[:, col0:col0 + TN_FF] = zc.astype(zc_ref.dtype)
                halves.append(zc)
            gate, value = halves
            f_ref[:, j * TN_FF:(j + 1) * TN_FF] = (gate * jax.nn.sigmoid(gate) * value).astype(f_ref.dtype)

    return _call(
        body, name="up_proj_gate", grid=(S // tm,),
        in_specs=[_row_spec(tm, D), _full_spec((1, D)),
                  pl.BlockSpec((D, 2 * D_FF), lambda i: (0, 0), pipeline_mode=pl.Buffered(1)),
                  _full_spec((CONV_ROWS, 2 * D_FF))],
        out_specs=[_row_spec(tm, D), _row_spec(tm, 2 * D_FF), _row_spec(tm, 2 * D_FF), _row_spec(tm, D_FF)],
        out_shape=[jax.ShapeDtypeStruct((S, D), MXU_DT), jax.ShapeDtypeStruct((S, 2 * D_FF), MXU_DT),
                   jax.ShapeDtypeStruct((S, 2 * D_FF), MXU_DT), jax.ShapeDtypeStruct((S, D_FF), MXU_DT)],
        scratch_shapes=[pltpu.VMEM((HALO, 2 * D_FF), F32)],
        compiler_params=_params("arbitrary"),
    )(x, g, w, cwb)


def _gate_up_bwd(z, zc, cwb, w, wd, x, g, dres):
    S, D = x.shape
    tm = TM
    nb = S // tm

    def body(z_ref, zc_ref, zcn_ref, c_ref, w_ref, wd_ref, x_ref, g_ref, r_ref, rn_ref,
             dz_ref, dc_ref, dx_ref, dg_ref):
        i = pl.program_id(0)
        first = i == 0
        last = i == nb - 1
        dxe = jnp.concatenate([r_ref[...], jnp.where(last, 0.0, rn_ref[...])], axis=0).astype(MXU_DT)

        @pl.when(first)
        def _():
            dc_ref[...] = jnp.zeros_like(dc_ref)
            dg_ref[...] = jnp.zeros_like(dg_ref)

        rid = lax.broadcasted_iota(jnp.int32, (CONV_ROWS, TN_FF), 0)

        def conv_out(cols):
            return jnp.concatenate([zc_ref[:, cols].astype(F32), zcn_ref[:, cols].astype(F32)], axis=0)

        def conv_bwd(d, z0, c):
            d0 = d[:tm]
            d1 = _shift_rows(d, -1, tm, 0)
            d2 = _shift_rows(d, -2, tm, 0)
            sums = [jnp.sum(d2 * z0, axis=0, keepdims=True), jnp.sum(d1 * z0, axis=0, keepdims=True),
                    jnp.sum(d0 * z0, axis=0, keepdims=True), jnp.sum(d0, axis=0, keepdims=True)]
            dtaps = jnp.zeros((CONV_ROWS, TN_FF), F32)
            for k, v in enumerate(sums):
                dtaps = jnp.where(rid == k, v, dtaps)
            return d0 * c[2:3] + d1 * c[1:2] + d2 * c[0:1], dtaps

        dh = jnp.zeros((tm, D), F32)
        for j in range(NB_FF):
            gc = slice(j * TN_FF, (j + 1) * TN_FF)
            uc = slice(D_FF + j * TN_FF, D_FF + (j + 1) * TN_FF)
            gt = conv_out(gc)
            ut = conv_out(uc)
            df = _dot_nt(dxe, wd_ref[gc, :])
            sg = jax.nn.sigmoid(gt)
            dzg, dtg = conv_bwd(df * ut * (sg * (1.0 + gt * (1.0 - sg))), z_ref[:, gc].astype(F32), c_ref[:, gc])
            dzu, dtu = conv_bwd(df * (gt * sg), z_ref[:, uc].astype(F32), c_ref[:, uc])
            dzg = dzg.astype(dz_ref.dtype)
            dzu = dzu.astype(dz_ref.dtype)
            dz_ref[:, gc] = dzg
            dz_ref[:, uc] = dzu
            dc_ref[:, gc] += dtg
            dc_ref[:, uc] += dtu
            dh += _dot_nt(dzg, w_ref[:, gc]) + _dot_nt(dzu, w_ref[:, uc])

        xv = x_ref[...]
        r = lax.rsqrt(jnp.mean(xv * xv, axis=-1, keepdims=True) + EPS)
        xhat = xv * r
        dg_ref[...] += jnp.sum(dh * xhat, axis=0, keepdims=True)
        dxh = dh * g_ref[...]
        dx_ref[...] = r_ref[...] + r * (dxh - xhat * jnp.mean(dxh * xhat, axis=-1, keepdims=True))

    hb = tm // HALO
    last_halo = S // HALO - 1
    return _call(
        body, name="gate_up_bwd", grid=(nb,),
        in_specs=[_row_spec(tm, 2 * D_FF), _row_spec(tm, 2 * D_FF),
                  pl.BlockSpec((HALO, 2 * D_FF), lambda i: (jnp.minimum((i + 1) * hb, last_halo), 0)),
                  _full_spec((CONV_ROWS, 2 * D_FF)),
                  pl.BlockSpec((D, 2 * D_FF), lambda i: (0, 0), pipeline_mode=pl.Buffered(1)),
                  pl.BlockSpec((D_FF, D), lambda i: (0, 0), pipeline_mode=pl.Buffered(1)),
                  _row_spec(tm, D), _full_spec((1, D)), _row_spec(tm, D),
                  pl.BlockSpec((HALO, D), lambda i: (jnp.minimum((i + 1) * hb, last_halo), 0))],
        out_specs=[_row_spec(tm, 2 * D_FF), _full_spec((CONV_ROWS, 2 * D_FF)), _row_spec(tm, D), _full_spec((1, D))],
        out_shape=[jax.ShapeDtypeStruct((S, 2 * D_FF), MXU_DT), jax.ShapeDtypeStruct((CONV_ROWS, 2 * D_FF), F32),
                   jax.ShapeDtypeStruct((S, D), F32), jax.ShapeDtypeStruct((1, D), F32)],
        compiler_params=_params("arbitrary"),
    )(z, zc, zc, cwb, w, wd, x, g, dres, dres)


def _layer_consts(w, l):
    wm = _tril_weights(w["sgu_w"][l])
    eye = jnp.eye(4, dtype=F32)
    wbd = (w["pool_w"][l][:, :, None, :] * eye[:, None, :, None]).reshape(W_C, W_C)
    cwb = jnp.concatenate([w["conv_w"][l], w["conv_b"][l][None], jnp.zeros((CONV_ROWS - 4, 2 * D_FF), F32)], axis=0)
    return dict(
        g1=w["norm1_g"][l][None], g2=w["norm2_g"][l][None], gm=w["mix_norm_g"][l][None],
        sng=w["sgu_norm_g"][l][None], wm=wm.astype(MXU_DT), wmt=jnp.swapaxes(wm, 1, 2).astype(MXU_DT),
        bias=jnp.repeat(jnp.transpose(w["sgu_b"][l]), HEAD_DIM, axis=1),
        wbd=wbd.astype(MXU_DT), scale=w["pool_scale"][l][None], cwb=cwb,
    )


def _local_step(x, tgt, w, late=None, early_exchange=None):
    gmat = _group_matrix()
    tri = _tri_matrix()
    trit = jnp.transpose(tri)
    saved = []
    early = None
    big = {n: [w[n][l] for l in range(DEPTH)] for n in BIG_NAMES[:4]}
    for l in range(DEPTH):
        c = _layer_consts(w, l)
        if l == 0 and late is not None:
            assert DEPTH == 2
            shards = late["shards"]
            h1, proj, proj_b, ya, yc, *gathered = _in_proj_groups(
                x, c["g1"], big["w_in"][l], c["sng"], c["wm"], c["bias"], gmat, c["wbd"], c["scale"],
                carry_gather=shards[1:])
            for name, arr in late["assemble"](gathered, shards[1:], 0, BIG_NAMES[1:4]).items():
                big[name][0] = arr
            yb, rb, *gathered = _sb_fwd(proj_b, tri, carry_gather=shards)
            for name, arr in late["assemble"](gathered, shards, 1, BIG_NAMES[:4]).items():
                big[name][1] = arr
        else:
            h1, proj, proj_b, ya, yc = _in_proj_groups(x, c["g1"], big["w_in"][l], c["sng"], c["wm"], c["bias"],
                                                       gmat, c["wbd"], c["scale"])
            yb, rb = _sb_fwd(proj_b, tri)
        x2, yn = _mix_out(ya, yb, yc, c["gm"], big["w_o"][l], x, gmat)
        h2, z, zc, f = _up_proj_gate(x2, c["g2"], big["w_up"][l], c["cwb"])
        saved.append(dict(c=c, x=x, proj=proj, proj_b=proj_b, h1=h1, ya=ya, yb=yb, yc=yc, rb=rb, x2=x2, yn=yn,
                          z=z, zc=zc, h2=h2, f=f))
        if l < DEPTH - 1:
            x = _mm_res(f, big["w_down"][l], x2, "down_proj")
    for l in range(DEPTH):
        saved[l]["c"] = dict(saved[l]["c"], **{n: big[n][l] for n in BIG_NAMES[:4]})

    last = saved[-1]
    dx, d_final_g, loss8 = _down_proj_loss(last["f"], last["c"]["w_down"], last["x2"], w["final_g"][None], tgt)
    grads = {n: [None] * DEPTH for n in ("norm1_g", "w_in", "sgu_norm_g", "sgu_w", "sgu_b", "pool_w", "pool_scale",
                                         "mix_norm_g", "w_o", "norm2_g", "w_up", "conv_w", "conv_b", "w_down")}
    for l in reversed(range(DEPTH)):
        s = saved[l]
        c = s["c"]
        grads["w_down"][l] = _mm_tn(s["f"], dx, "down_proj_wgrad").reshape(N_CHIPS, D_FF // N_CHIPS, D_MODEL)
        dz, dcwb, dx2, dg2 = _gate_up_bwd(s["z"], s["zc"], c["cwb"], c["w_up"], c["w_down"], s["x2"], c["g2"], dx)
        grads["conv_w"][l] = dcwb[:3]
        grads["conv_b"][l] = dcwb[3]
        grads["w_up"][l] = _mm_tn(s["h2"], dz, "up_proj_wgrad", col_tiles=True)
        grads["norm2_g"][l] = dg2[0]
        grads["w_o"][l] = _mm_tn(s["yn"], dx2, "out_proj_wgrad").reshape(N_CHIPS, D_MODEL // N_CHIPS, D_MODEL)
        if l == 0 and early_exchange is not None:
            early_items = early_exchange[0](grads)
            dya, dyb, dyc, dgm, *swapped = _mix_out_bwd(dx2, c["w_o"], s["ya"], s["yb"], s["yc"], c["gm"], gmat,
                                                        carry_swap=early_items)
        else:
            dya, dyb, dyc, dgm = _mix_out_bwd(dx2, c["w_o"], s["ya"], s["yb"], s["yc"], c["gm"], gmat)
        grads["mix_norm_g"][l] = dgm[0]
        dd, e, dwbd, dscale = _pool_bwd_a(s["proj"], dyc, c["wbd"], c["scale"])
        dp = _pool_bwd_b(dd, e)
        grads["pool_w"][l] = jnp.stack([dwbd[g * 64:(g + 1) * 64, g * 64:(g + 1) * 64] for g in range(4)])
        grads["pool_scale"][l] = dscale[0]
        if l == 0 and early_exchange is not None:
            sent = early_exchange[1](early_items, swapped)
            dq, dk, dv, *parts = _sb_bwd(s["proj_b"], dyb, s["rb"], tri, trit, carry_exchange=sent)
            early = (sent, parts)
        else:
            dq, dk, dv = _sb_bwd(s["proj_b"], dyb, s["rb"], tri, trit)
        rest = jnp.concatenate([dq, dk.astype(MXU_DT), dv.astype(MXU_DT), dp], axis=1)
        da, dwm, dbias, dsng, dx, dg1 = _sgu_in_proj_bwd(s["proj"], dya, c["sng"], c["wm"], c["wmt"], c["bias"],
                                                         gmat, rest, c["w_in"], s["x"], c["g1"], dx2)
        grads["sgu_w"][l] = dwm
        grads["sgu_b"][l] = jnp.transpose(jnp.sum(dbias.reshape(CHUNK, 4, HEAD_DIM), axis=-1))
        grads["sgu_norm_g"][l] = dsng[0]
        grads["norm1_g"][l] = dg1[0]
        dw_in = jnp.concatenate([_mm_tn(s["h1"], da, "in_proj_wgrad_unit"),
                                 _mm_tn(s["h1"], rest, "in_proj_wgrad_rest")], axis=1)
        grads["w_in"][l] = jnp.transpose(dw_in.reshape(D_MODEL, N_CHIPS, IN_COLS // N_CHIPS), (1, 0, 2))

    out = {n: (v if n in BIG_NAMES[:4] else jnp.stack(v)) for n, v in grads.items()}
    out["final_g"] = d_final_g[0]
    return loss8[0, 0], dx, out, early


MESH = pl.DeviceIdType.MESH
ANY = pl.BlockSpec(memory_space=pl.ANY)


def _gather_plan(ins, outs, send_sems, recv_sems, layer, sender):
    n = len(ins)
    x, y, c = lax.axis_index("x"), lax.axis_index("y"), lax.axis_index("c")
    sibling = (x, y, 1 - c)
    my_chip = 2 * x + y
    chips = [(1 - x, y), (x, 1 - y), (1 - x, 1 - y)]
    ids = [2 * px + py for px, py in chips]

    def copy(a, k, chip, to, own=False):
        dst = outs[a].at[chip]
        return pltpu.make_async_remote_copy(
            src_ref=ins[a].at[layer] if own else dst, dst_ref=dst,
            send_sem=send_sems.at[a, k], recv_sem=recv_sems.at[a, k], device_id=to, device_id_type=MESH)

    sends = [copy(a, j, my_chip, (*chips[j], sender), own=True) for j in range(3) for a in range(n)]
    arrivals = [copy(a, j, ids[j], sibling) for j in range(3) for a in range(n)]
    forwards = [copy(a, 3 + j, ids[j], sibling) for j in range(3) for a in range(n)]
    return c, sends, arrivals, forwards


def _gather_shapes(shards):
    n = len(shards)
    return ([jax.ShapeDtypeStruct((N_CHIPS,) + s.shape[1:], s.dtype) for s in shards],
            [pltpu.SemaphoreType.DMA((n, 6)), pltpu.SemaphoreType.DMA((n, 6))])


def _all_gather(shards, layer, sender):
    n = len(shards)

    def body(*refs):
        c, sends, arrivals, forwards = _gather_plan(refs[:n], refs[n:2 * n], refs[2 * n], refs[2 * n + 1],
                                                    layer, sender)

        @pl.when(c == sender)
        def _():
            for cp in sends:
                cp.start()
            for arrived, onward in zip(arrivals, forwards):
                arrived.wait_recv()
                onward.start()
            for cp in sends + forwards:
                cp.wait_send()

        @pl.when(c != sender)
        def _():
            for cp in forwards:
                cp.wait_recv()

    out_shape, sems = _gather_shapes(shards)
    return _call(body, name="weight_all_gather", out_shape=out_shape, in_specs=[ANY] * n, out_specs=[ANY] * n,
                 scratch_shapes=sems)(*shards)


def _row_tile(r):
    return r if r <= 704 else 512


def _grad_swap(items, name):
    n = len(items)

    def body(*refs):
        start, finish = _swap_plan(refs[:n], refs[n:2 * n], refs[2 * n:3 * n], refs[3 * n], refs[3 * n + 1])
        start()
        finish()

    out_shape, sems = _swap_shapes(items)
    return _call(body, name=name, out_shape=out_shape, in_specs=[ANY] * (2 * n), out_specs=[ANY] * n,
                 scratch_shapes=sems)(*[a0 for a0, _ in items], *[a1 for _, a1 in items])


def _swap_plan(firsts, seconds, outs, send_sems, recv_sems):
    n = len(firsts)
    x, y, c = lax.axis_index("x"), lax.axis_index("y"), lax.axis_index("c")

    def copies(srcs):
        return [pltpu.make_async_remote_copy(src_ref=srcs[a], dst_ref=outs[a], send_sem=send_sems.at[a],
                                             recv_sem=recv_sems.at[a], device_id=(x, y, 1 - c),
                                             device_id_type=MESH) for a in range(n)]

    def start():
        @pl.when(c == 0)
        def _():
            for cp in copies(seconds):
                cp.start()

        @pl.when(c == 1)
        def _():
            for cp in copies(firsts):
                cp.start()

    def finish():
        for cp in copies(firsts):
            cp.wait()

    return start, finish


def _swap_shapes(items):
    n = len(items)
    return ([jax.ShapeDtypeStruct(a0.shape, a0.dtype) for a0, _ in items],
            [pltpu.SemaphoreType.DMA((n,)), pltpu.SemaphoreType.DMA((n,))])


def _pair_add(a0, a1, r, c_arr, name, out_dtype):
    k, rr, cc = r.shape
    tr = _row_tile(rr)

    def body(c_ref, a0_ref, a1_ref, r_ref, o_ref):
        mine = jnp.where(c_ref[0] == 0, a0_ref[...], a1_ref[...])
        o_ref[...] = (mine + r_ref[...]).astype(o_ref.dtype)

    def member(which):
        def index(kk, i, c_ref):
            used = (c_ref[0] == which).astype(jnp.int32)
            return (kk * used, i * used, 0)
        return pl.BlockSpec((1, tr, cc), index)

    spec = pl.BlockSpec((1, tr, cc), lambda kk, i, c_ref: (kk, i, 0))
    grid_spec = pltpu.PrefetchScalarGridSpec(num_scalar_prefetch=1, grid=(k, rr // tr),
                                             in_specs=[member(0), member(1), spec], out_specs=spec)
    return _call(body, name=name, grid_spec=grid_spec, out_shape=jax.ShapeDtypeStruct((k, rr, cc), out_dtype),
                 compiler_params=_params("parallel", "parallel"))(c_arr, a0, a1, r)


def _exchange_plan(ins, outs, send_sems, recv_sems):
    n = len(ins)
    x, y, c = lax.axis_index("x"), lax.axis_index("y"), lax.axis_index("c")
    my_chip = 2 * x + y
    chips = [(1 - x, y), (x, 1 - y), (1 - x, 1 - y)]

    def copy(a, k, src_chip, dst_chip):
        px, py = chips[k]
        return pltpu.make_async_remote_copy(
            src_ref=ins[a].at[src_chip], dst_ref=outs[a].at[dst_chip], send_sem=send_sems.at[a, k],
            recv_sem=recv_sems.at[a, k], device_id=(px, py, c), device_id_type=MESH)

    sends = [copy(a, k, 2 * chips[k][0] + chips[k][1], my_chip) for k in range(3) for a in range(n)]
    arrivals = [copy(a, k, my_chip, 2 * chips[k][0] + chips[k][1]) for k in range(3) for a in range(n)]
    return sends, arrivals


def _exchange_shapes(hs):
    n = len(hs)
    return ([jax.ShapeDtypeStruct(h.shape, h.dtype) for h in hs],
            [pltpu.SemaphoreType.DMA((n, 3)), pltpu.SemaphoreType.DMA((n, 3))])


def _grad_exchange(hs):
    n = len(hs)

    def body(*refs):
        sends, arrivals = _exchange_plan(refs[:n], refs[n:2 * n], refs[2 * n], refs[2 * n + 1])
        for cp in sends:
            cp.start()
        for cp in arrivals:
            cp.wait_recv()
        for cp in sends:
            cp.wait_send()

    out_shape, sems = _exchange_shapes(hs)
    return _call(body, name="grad_exchange_chips", out_shape=out_shape, in_specs=[ANY] * n, out_specs=[ANY] * n,
                 scratch_shapes=sems)(*hs)


def _sum_chips(a, c_arr, name):
    _, r, cc = a.shape
    tr = _row_tile(r)

    def body(c_ref, a_ref, o_ref):
        o_ref[...] = ((a_ref[0].astype(F32) + a_ref[1].astype(F32)) + a_ref[2].astype(F32)) + a_ref[3].astype(F32)

    grid_spec = pltpu.PrefetchScalarGridSpec(
        num_scalar_prefetch=1, grid=(r // tr,),
        in_specs=[pl.BlockSpec((N_CHIPS, tr, cc), lambda i, c_ref: (0, i, 0))],
        out_specs=pl.BlockSpec((None, tr, cc), lambda i, c_ref: (c_ref[0], i, 0)))
    return _call(body, name=name, grid_spec=grid_spec, out_shape=jax.ShapeDtypeStruct((2, r, cc), F32),
                 compiler_params=_params("parallel"))(c_arr, a)


def _grad_share(bufs):
    n = len(bufs)

    def body(*refs):
        outs = refs[n:2 * n]
        send_sems, recv_sems = refs[2 * n:]
        x, y, c = lax.axis_index("x"), lax.axis_index("y"), lax.axis_index("c")
        copies = [pltpu.make_async_remote_copy(src_ref=outs[a].at[c], dst_ref=outs[a].at[c], send_sem=send_sems.at[a],
                                               recv_sem=recv_sems.at[a], device_id=(x, y, 1 - c),
                                               device_id_type=MESH) for a in range(n)]
        for cp in copies:
            cp.start()
        for a in range(n):
            pltpu.make_async_remote_copy(src_ref=outs[a].at[c], dst_ref=outs[a].at[1 - c], send_sem=send_sems.at[a],
                                         recv_sem=recv_sems.at[a], device_id=(x, y, 1 - c),
                                         device_id_type=MESH).wait_recv()
        for cp in copies:
            cp.wait_send()

    return _call(
        body, name="grad_share_cores", out_shape=[jax.ShapeDtypeStruct(b.shape, b.dtype) for b in bufs],
        in_specs=[ANY] * n, out_specs=[ANY] * n, input_output_aliases={a: a for a in range(n)},
        scratch_shapes=[pltpu.SemaphoreType.DMA((n,)), pltpu.SemaphoreType.DMA((n,))],
    )(*bufs)


def _adamw_math(g_ref, w_ref, m_ref, v_ref, d_ref, nm_ref, nv_ref):
    gv = g_ref[...]
    nm = ADAM_B1 * m_ref[...] + (1.0 - ADAM_B1) * gv
    nv = ADAM_B2 * v_ref[...] + (1.0 - ADAM_B2) * (gv * gv)
    m_hat = nm / (1.0 - ADAM_B1 ** ADAM_STEP)
    v_hat = nv / (1.0 - ADAM_B2 ** ADAM_STEP)
    d_ref[...] = -ADAM_LR * (m_hat / (jnp.sqrt(v_hat) + ADAM_EPS) + ADAM_WD * w_ref[...])
    nm_ref[...] = nm
    nv_ref[...] = nv


def _adamw_big(g, w, m, v, name):
    d, r, c = g.shape
    tr = _row_tile(r)
    spec = pl.BlockSpec((1, tr, c), lambda l, i: (l, i, 0))

    def body(*refs):
        _adamw_math(*refs)

    shp = jax.ShapeDtypeStruct(g.shape, F32)
    return _call(body, name=name, grid=(d, r // tr), in_specs=[spec] * 4, out_specs=[spec] * 3,
                 out_shape=[shp, shp, shp], compiler_params=_params("parallel", "parallel"))(g, w, m, v)


def _adamw_small(gs, ws, ms, vs):
    n = len(gs)

    def body(*refs):
        ins, outs = refs[:4 * n], refs[4 * n:]
        for k in range(n):
            _adamw_math(ins[k], ins[n + k], ins[2 * n + k], ins[3 * n + k], outs[k], outs[n + k], outs[2 * n + k])

    shp = [jax.ShapeDtypeStruct(g.shape, F32) for g in gs]
    res = _call(body, name="adamw_small", out_shape=shp * 3)(*gs, *ws, *ms, *vs)
    return res[:n], res[n:2 * n], res[2 * n:]


def _rows(a, rows):
    flat = a.reshape(-1)
    return jnp.pad(flat, (0, rows * D_MODEL - flat.shape[0])).reshape(rows, D_MODEL)


def _small_rows(p, extra=None):
    parts = [p[n].reshape(-1) for n in SMALL_NAMES]
    if extra is not None:
        parts.append(extra.reshape(-1))
    flat = jnp.concatenate(parts)
    return jnp.pad(flat, (0, ROWS_SMALL * D_MODEL - flat.shape[0])).reshape(ROWS_SMALL, D_MODEL)


CONV_SHARD = (DEPTH, 3, 2 * D_FF // N_CHIPS)
N_CONV_SHARD = DEPTH * 3 * (2 * D_FF // N_CHIPS)


def _small_pack(g, loss):
    conv = jnp.transpose(g["conv_w"].reshape(DEPTH, 3, N_CHIPS, 2 * D_FF // N_CHIPS), (2, 0, 1, 3))
    conv = jnp.stack([_rows(conv[k], ROWS_CONV) for k in range(N_CHIPS)])
    small = jnp.broadcast_to(_small_rows(g, loss), (N_CHIPS, ROWS_SMALL, D_MODEL))
    return jnp.concatenate([conv, small], axis=1)


def _unpack_small(pack):
    out = {"conv_w": pack[:ROWS_CONV].reshape(-1)[:N_CONV_SHARD].reshape(CONV_SHARD)}
    flat = pack[ROWS_CONV:].reshape(-1)
    k = 0
    for name in SMALL_NAMES:
        shape = SMALL_SHAPES[name]
        n = 1
        for d in shape:
            n *= d
        out[name] = flat[k:k + n].reshape(shape)
        k += n
    out["extra"] = flat[k]
    return out


def _assemble_layer(gathered, shards, layer, names):
    my_chip = 2 * lax.axis_index("x") + lax.axis_index("y")
    out = {}
    for name, got, own in zip(names, gathered, shards):
        full = lax.dynamic_update_index_in_dim(got, own[layer], my_chip, 0)
        if name in ("w_in", "w_up"):
            k, r, wd = full.shape
            out[name] = jnp.transpose(full, (1, 0, 2)).reshape(r, k * wd)
        else:
            out[name] = full.reshape(-1, D_MODEL)
    return out


def _gather_weights(p):
    shards = [p[n].astype(jnp.bfloat16) for n in BIG_NAMES[:4]]
    conv_all = p["conv_w"].reshape(1, -1, p["conv_w"].shape[-1])
    got = _all_gather([shards[0], conv_all], 0, 0)
    my_chip = 2 * lax.axis_index("x") + lax.axis_index("y")
    conv = lax.dynamic_update_index_in_dim(got[1], conv_all[0], my_chip, 0)
    conv = jnp.transpose(conv.reshape((N_CHIPS,) + CONV_SHARD), (1, 2, 0, 3)).reshape(DEPTH, 3, 2 * D_FF)
    full = {n: [None, None] for n in BIG_NAMES[:4]}
    full["w_in"][0] = _assemble_layer(got[:1], shards[:1], 0, ("w_in",))["w_in"]
    full["conv_w"] = conv
    return full, dict(shards=shards, assemble=_assemble_layer)


def _halves(a):
    r = a.shape[1] // 2
    return a[:, :r], a[:, r:]


def _reduce_begin(items, names, dtypes, c_arr, tag):
    return _pair_adds(items, _grad_swap(items, "grad_swap_cores_" + tag), names, dtypes, c_arr)


def _pair_adds(items, got, names, dtypes, c_arr):
    return [_pair_add(a0, a1, r, c_arr, "grad_add_cores_" + nm, dt)
            for (a0, a1), r, nm, dt in zip(items, got, names, dtypes)]


def _reduce_end(parts, sent, names, c_arr):
    my_chip = 2 * lax.axis_index("x") + lax.axis_index("y")
    full = [lax.dynamic_update_index_in_dim(p, lax.dynamic_index_in_dim(own, my_chip, 0, keepdims=False), my_chip, 0)
            for p, own in zip(parts, sent)]
    return [_sum_chips(f, c_arr, "grad_sum_chips_" + nm) for f, nm in zip(full, names)]


EARLY_NAMES = ("w_o", "w_up", "w_down", "w_in_1")


def _early_items(grads):
    return [tuple(grads[n]) for n in ("w_o", "w_up", "w_down")] + [_halves(grads["w_in"][1])]


def kernel(x, norm1_g, w_in, sgu_norm_g, sgu_w, sgu_b, pool_w, pool_scale, mix_norm_g, w_o, norm2_g, w_up, conv_w, conv_b, w_down, final_g, loss_target, m_norm1_g, m_w_in, m_sgu_norm_g, m_sgu_w, m_sgu_b, m_pool_w, m_pool_scale, m_mix_norm_g, m_w_o, m_norm2_g, m_w_up, m_conv_w, m_conv_b, m_w_down, m_final_g, v_norm1_g, v_w_in, v_sgu_norm_g, v_sgu_w, v_sgu_b, v_pool_w, v_pool_scale, v_mix_norm_g, v_w_o, v_norm2_g, v_w_up, v_conv_w, v_conv_b, v_w_down, v_final_g):
    names = ("norm1_g", "w_in", "sgu_norm_g", "sgu_w", "sgu_b", "pool_w", "pool_scale", "mix_norm_g", "w_o",
             "norm2_g", "w_up", "conv_w", "conv_b", "w_down", "final_g")
    p = dict(zip(names, (norm1_g, w_in, sgu_norm_g, sgu_w, sgu_b, pool_w, pool_scale, mix_norm_g, w_o, norm2_g,
                         w_up, conv_w, conv_b, w_down, final_g)))
    pm = dict(zip(names, (m_norm1_g, m_w_in, m_sgu_norm_g, m_sgu_w, m_sgu_b, m_pool_w, m_pool_scale, m_mix_norm_g,
                          m_w_o, m_norm2_g, m_w_up, m_conv_w, m_conv_b, m_w_down, m_final_g)))
    pv = dict(zip(names, (v_norm1_g, v_w_in, v_sgu_norm_g, v_sgu_w, v_sgu_b, v_pool_w, v_pool_scale, v_mix_norm_g,
                          v_w_o, v_norm2_g, v_w_up, v_conv_w, v_conv_b, v_w_down, v_final_g)))
    c = lax.axis_index("c")
    gathered, late = _gather_weights(p)
    full = dict(p)
    full.update(gathered)

    c_arr = jnp.reshape(c, (1,)).astype(jnp.int32)
    early_types = [ICI_DT] * len(EARLY_NAMES)
    loss, dx, grads, (sent, received) = _local_step(
        x[0], loss_target[0], full, late,
        (_early_items, lambda items, swapped: _pair_adds(items, swapped, EARLY_NAMES, early_types, c_arr)))
    early_sums = _reduce_end(received, sent, EARLY_NAMES, c_arr)
    small_pack = _small_pack(grads, loss)
    late_names = ("w_in_0", "small")
    late_sent = _reduce_begin([_halves(grads["w_in"][0]), _halves(small_pack)], late_names, [ICI_DT, F32], c_arr, "late")
    late_sums = _reduce_end(_grad_exchange(late_sent), late_sent, late_names, c_arr)
    r_o, r_up, r_down, r_in1, r_in0, r_small = _grad_share(early_sums + late_sums)
    g = dict(w_o=r_o, w_up=r_up, w_down=r_down,
             w_in=jnp.stack([r_in0.reshape(D_MODEL, -1), r_in1.reshape(D_MODEL, -1)]))
    g.update(_unpack_small(r_small.reshape(2 * SP_HALF, D_MODEL)))
    d, nm, nv = {}, {}, {}
    for n in BIG_NAMES:
        d[n], nm[n], nv[n] = _adamw_big(g[n], p[n], pm[n], pv[n], "adamw_" + n)

    def two_d(a):
        return a.reshape(1, -1) if a.ndim == 1 else a

    ds, ms, vs = _adamw_small([two_d(g[n]) for n in SMALL_NAMES], [two_d(p[n]) for n in SMALL_NAMES],
                              [two_d(pm[n]) for n in SMALL_NAMES], [two_d(pv[n]) for n in SMALL_NAMES])
    for k, n in enumerate(SMALL_NAMES):
        d[n], nm[n], nv[n] = (a.reshape(p[n].shape) for a in (ds[k], ms[k], vs[k]))
    return (g["extra"], dx[None], *[g[n] for n in names], *[d[n] for n in names], *[nm[n] for n in names],
            *[nv[n] for n in names])
```

```python
import jax
import jax.numpy as jnp
from jax import lax
from jax.experimental import pallas as pl
from jax.experimental.pallas import tpu as pltpu

F32 = jnp.float32
MXU_DT = jnp.bfloat16

D_MODEL = 1024
DEPTH = 2
HEAD_DIM = 64
W_A = 256
W_B = 512
W_C = 256
IN_COLS = 2 * W_A + 3 * W_B + W_C
CHUNK = 128
POOL_WINDOWS = (2, 4, 8, 16)
D_FF = 2816
EPS = 1e-6
N_CHIPS = 4

ADAM_LR = 0.001
ADAM_B1 = 0.9
ADAM_B2 = 0.999
ADAM_EPS = 1e-08
ADAM_WD = 0.01
ADAM_STEP = 10

LANES = 128
TQ = 256
TK = 256
TM = 256
TM_MM = 512
HALO = 16
VMEM_LIMIT = 56 * 1024 * 1024

ROWS_CONV = 16
ROWS_SMALL = 240
SP_HALF = (ROWS_CONV + ROWS_SMALL) // 2
ICI_DT = jnp.bfloat16

BIG_NAMES = ("w_in", "w_o", "w_up", "w_down", "conv_w")
SMALL_NAMES = ("norm1_g", "sgu_norm_g", "sgu_w", "sgu_b", "pool_w", "pool_scale",
               "mix_norm_g", "norm2_g", "conv_b", "final_g")
SMALL_SHAPES = {
    "norm1_g": (DEPTH, D_MODEL), "sgu_norm_g": (DEPTH, W_A), "sgu_w": (DEPTH, 4, CHUNK, CHUNK),
    "sgu_b": (DEPTH, 4, CHUNK), "pool_w": (DEPTH, 4, 64, 64), "pool_scale": (DEPTH, W_C),
    "mix_norm_g": (DEPTH, D_MODEL), "norm2_g": (DEPTH, D_MODEL), "conv_b": (DEPTH, 2 * D_FF),
    "final_g": (D_MODEL,),
}


def _call(body, **kw):
    return pl.pallas_call(body, **kw)


def _params(*sem):
    return pltpu.CompilerParams(dimension_semantics=sem, vmem_limit_bytes=VMEM_LIMIT)


def _dot(a, b):
    return jnp.dot(a, b, preferred_element_type=F32)


def _dot_nt(a, b):
    return lax.dot_general(a, b, (((1,), (1,)), ((), ())), preferred_element_type=F32)


def _dot_tn(a, b):
    return lax.dot_general(a, b, (((0,), (0,)), ((), ())), preferred_element_type=F32)


def _group_mean(sq, gmat):
    sqb = sq.astype(MXU_DT)
    cols = [_dot(sqb[:, b * LANES:(b + 1) * LANES], gmat) for b in range(sq.shape[1] // LANES)]
    return cols[0] if len(cols) == 1 else jnp.concatenate(cols, axis=-1)


def _group_matrix():
    r = jnp.arange(LANES)
    return jnp.where((r[:, None] // HEAD_DIM) == (r[None, :] // HEAD_DIM), 1.0 / HEAD_DIM, 0.0).astype(MXU_DT)


def _tile(n):
    return max(t for t in range(LANES, 1536 + 1, LANES) if n % t == 0)


def _row_spec(tm, cols, col_block=0):
    return pl.BlockSpec((tm, cols), lambda i, cb=col_block: (i, cb))


def _full_spec(shape):
    nd = len(shape)
    return pl.BlockSpec(shape, lambda *_: (0,) * nd)


def _mm_res(a, w, res, name):
    S, K = a.shape
    N = w.shape[1]
    tm = TM_MM

    def body(a_ref, w_ref, r_ref, o_ref):
        o_ref[...] = r_ref[...] + _dot(a_ref[...], w_ref[...])

    return _call(
        body, name=name, grid=(S // tm,),
        in_specs=[_row_spec(tm, K), _full_spec((K, N)), _row_spec(tm, N)],
        out_specs=_row_spec(tm, N),
        out_shape=jax.ShapeDtypeStruct((S, N), F32),
        compiler_params=_params("parallel"),
    )(a, w, res)


def _mm_tn(a, b, name, col_tiles=False):
    S, K1 = a.shape
    N = b.shape[1]
    ts = min(4 * TM_MM, S)
    tk = _tile(K1)
    tn = _tile(N)
    if col_tiles:
        out_spec = pl.BlockSpec((None, tk, tn), lambda m, n, s: (n, m, 0))
        out_shape = jax.ShapeDtypeStruct((N // tn, K1, tn), F32)
    else:
        out_spec = pl.BlockSpec((tk, tn), lambda m, n, s: (m, n))
        out_shape = jax.ShapeDtypeStruct((K1, N), F32)

    def body(a_ref, b_ref, o_ref):
        @pl.when(pl.program_id(2) == 0)
        def _():
            o_ref[...] = jnp.zeros_like(o_ref)

        o_ref[...] += _dot_tn(a_ref[...], b_ref[...].astype(MXU_DT))

    return _call(
        body, name=name, grid=(K1 // tk, N // tn, S // ts),
        in_specs=[pl.BlockSpec((ts, tk), lambda m, n, s: (s, m)),
                  pl.BlockSpec((ts, tn), lambda m, n, s: (s, n))],
        out_specs=out_spec, out_shape=out_shape,
        compiler_params=_params("parallel", "parallel", "arbitrary"),
    )(a, b)


def _down_proj_loss(a, w, res, g, tgt):
    S, D = res.shape
    K = a.shape[1]
    tm = TM_MM

    def body(a_ref, w_ref, res_ref, g_ref, t_ref, dx_ref, dg_ref, l_ref):
        xv = res_ref[...] + _dot(a_ref[...], w_ref[...])
        r = lax.rsqrt(jnp.mean(xv * xv, axis=-1, keepdims=True) + EPS)
        xhat = xv * r
        diff = xhat * g_ref[...] - t_ref[...]

        @pl.when(pl.program_id(0) == 0)
        def _():
            dg_ref[...] = jnp.zeros_like(dg_ref)
            l_ref[...] = jnp.zeros_like(l_ref)

        l_ref[...] += jnp.full(l_ref.shape, 0.5 * jnp.sum(jnp.mean(diff * diff, axis=-1, keepdims=True)), F32)
        dout = diff * (1.0 / D)
        dg_ref[...] += jnp.sum(dout * xhat, axis=0, keepdims=True)
        dxh = dout * g_ref[...]
        dx_ref[...] = r * (dxh - xhat * jnp.mean(dxh * xhat, axis=-1, keepdims=True))

    return _call(
        body, name="down_proj_loss", grid=(S // tm,),
        in_specs=[_row_spec(tm, K), _full_spec((K, D)), _row_spec(tm, D), _full_spec((1, D)), _row_spec(tm, D)],
        out_specs=[_row_spec(tm, D), _full_spec((1, D)), _full_spec((8, LANES))],
        out_shape=[jax.ShapeDtypeStruct((S, D), F32), jax.ShapeDtypeStruct((1, D), F32),
                   jax.ShapeDtypeStruct((8, LANES), F32)],
        compiler_params=_params("arbitrary"),
    )(a, w, res, g, tgt)


def _mix_out(ya, yb, yc, gm, wo, x, gmat):
    S = x.shape[0]
    tm = TM_MM

    def body(ya_ref, yb_ref, yc_ref, gm_ref, wo_ref, x_ref, gmat_ref, x2_ref, yn_ref):
        y = jnp.concatenate([ya_ref[...], yb_ref[...], yc_ref[...]], axis=-1)
        r = lax.rsqrt(_group_mean(y * y, gmat_ref[...]) + EPS)
        yn = (y * r * gm_ref[...]).astype(MXU_DT)
        yn_ref[...] = yn
        x2_ref[...] = x_ref[...] + _dot(yn, wo_ref[...])

    return _call(
        body, name="mix_out", grid=(S // tm,),
        in_specs=[_row_spec(tm, W_A), _row_spec(tm, W_B), _row_spec(tm, W_C), _full_spec((1, D_MODEL)),
                  _full_spec((D_MODEL, D_MODEL)), _row_spec(tm, D_MODEL), _full_spec((LANES, LANES))],
        out_specs=[_row_spec(tm, D_MODEL), _row_spec(tm, D_MODEL)],
        out_shape=[jax.ShapeDtypeStruct((S, D_MODEL), F32), jax.ShapeDtypeStruct((S, D_MODEL), MXU_DT)],
        compiler_params=_params("parallel"),
    )(ya, yb, yc, gm, wo, x, gmat)


def _mix_out_bwd(dx2, wo, ya, yb, yc, gm, gmat, carry_swap=None):
    S = dx2.shape[0]
    tm = TM_MM
    nb = S // tm
    items = list(carry_swap or [])
    ns = len(items)

    def body(*refs):
        dx2_ref, wo_ref, ya_ref, yb_ref, yc_ref, gm_ref, gmat_ref = refs[:7]
        dya_ref, dyb_ref, dyc_ref, dgm_ref = refs[7 + 2 * ns:11 + 2 * ns]
        if ns:
            start, finish = _swap_plan(refs[7:7 + ns], refs[7 + ns:7 + 2 * ns], refs[11 + 2 * ns:11 + 3 * ns],
                                       refs[11 + 3 * ns], refs[12 + 3 * ns])

            @pl.when(pl.program_id(0) == 0)
            def _():
                start()

        dyn = _dot_nt(dx2_ref[...].astype(MXU_DT), wo_ref[...])
        y = jnp.concatenate([ya_ref[...], yb_ref[...], yc_ref[...]], axis=-1)
        r = lax.rsqrt(_group_mean(y * y, gmat_ref[...]) + EPS)
        yhat = y * r

        @pl.when(pl.program_id(0) == 0)
        def _():
            dgm_ref[...] = jnp.zeros_like(dgm_ref)

        dgm_ref[...] += jnp.sum(dyn * yhat, axis=0, keepdims=True)
        dyh = dyn * gm_ref[...]
        dy = r * (dyh - yhat * _group_mean(dyh * yhat, gmat_ref[...]))
        dya_ref[...] = dy[:, :W_A]
        dyb_ref[...] = dy[:, W_A:W_A + W_B]
        dyc_ref[...] = dy[:, W_A + W_B:]

        if ns:
            @pl.when(pl.program_id(0) == nb - 1)
            def _():
                finish()

    swapped, sems = _swap_shapes(items) if ns else ([], [])
    return _call(
        body, name="mix_out_bwd_swap" if ns else "mix_out_bwd", grid=(nb,),
        in_specs=[_row_spec(tm, D_MODEL), _full_spec((D_MODEL, D_MODEL)), _row_spec(tm, W_A), _row_spec(tm, W_B),
                  _row_spec(tm, W_C), _full_spec((1, D_MODEL)), _full_spec((LANES, LANES))] + [ANY] * (2 * ns),
        out_specs=[_row_spec(tm, W_A), _row_spec(tm, W_B), _row_spec(tm, W_C), _full_spec((1, D_MODEL))]
        + [ANY] * ns,
        out_shape=[jax.ShapeDtypeStruct((S, W_A), F32), jax.ShapeDtypeStruct((S, W_B), F32),
                   jax.ShapeDtypeStruct((S, W_C), F32), jax.ShapeDtypeStruct((1, D_MODEL), F32)] + swapped,
        scratch_shapes=sems,
        compiler_params=_params("arbitrary"),
    )(dx2, wo, ya, yb, yc, gm, gmat, *[a0 for a0, _ in items], *[a1 for _, a1 in items])


_SQRT_HALF = 0.7071067811865476
_INV_SQRT_2PI = 0.3989422804014327


def _sgu_common(a, sng, wm_ref, bias, gmat):
    phi = 0.5 * (1.0 + lax.erf(a * _SQRT_HALF))
    ga = a * phi
    u = ga[:, :W_A]
    v = ga[:, W_A:]
    r = lax.rsqrt(_group_mean(v * v, gmat) + EPS)
    vhat = v * r
    vn = (vhat * sng).astype(MXU_DT)
    head = lax.broadcasted_iota(jnp.int32, (CHUNK, W_A), 1) // HEAD_DIM
    rows = []
    for c in range(a.shape[0] // CHUNK):
        vc = vn[c * CHUNK:(c + 1) * CHUNK]
        s = bias
        for h in range(4):
            s = s + jnp.where(head == h, _dot(wm_ref[h], vc), 0.0)
        rows.append(s)
    s = jnp.concatenate(rows, axis=0)
    return phi, u, r, vhat, vn, s


def _tril_weights(sgu_w_l):
    t = jnp.arange(CHUNK)
    return jnp.where((t[None, :] <= t[:, None])[None], sgu_w_l, 0.0)


def _sgu_in_proj_bwd(proj, dy, sng, wm, wmt, bias, gmat, rest, w, x, g, dres):
    S, D = x.shape
    tm = TM_MM

    def body(a_ref, dy_ref, sng_ref, wm_ref, wmt_ref, b_ref, gmat_ref, rest_ref, w_ref, x_ref, g_ref, r_ref,
             da_ref, dw_ref, db_ref, dsng_ref, dx_ref, dg_ref):
        a = a_ref[...]
        dy = dy_ref[...]
        gmat = gmat_ref[...]
        sng = sng_ref[...]
        phi, u, r, vhat, vn, s = _sgu_common(a, sng, wm_ref, b_ref[...], gmat)
        du = dy * s
        ds = dy * u

        @pl.when(pl.program_id(0) == 0)
        def _():
            dw_ref[...] = jnp.zeros_like(dw_ref)
            db_ref[...] = jnp.zeros_like(db_ref)
            dsng_ref[...] = jnp.zeros_like(dsng_ref)
            dg_ref[...] = jnp.zeros_like(dg_ref)

        head = lax.broadcasted_iota(jnp.int32, (CHUNK, W_A), 1) // HEAD_DIM
        tt = lax.broadcasted_iota(jnp.int32, (CHUNK, CHUNK), 0)
        ss = lax.broadcasted_iota(jnp.int32, (CHUNK, CHUNK), 1)
        rows = []
        for c in range(tm // CHUNK):
            dsc = ds[c * CHUNK:(c + 1) * CHUNK]
            vc = vn[c * CHUNK:(c + 1) * CHUNK]
            db_ref[...] += dsc
            dsb = dsc.astype(MXU_DT)
            dvn = jnp.zeros((CHUNK, W_A), F32)
            for h in range(4):
                dvn = dvn + jnp.where(head == h, _dot(wmt_ref[h], dsb), 0.0)
                dsh = jnp.where(head == h, dsc, 0.0).astype(MXU_DT)
                dw_ref[h] += jnp.where(ss <= tt, _dot_nt(dsh, vc), 0.0)
            rows.append(dvn)
        dvn = jnp.concatenate(rows, axis=0)
        dsng_ref[...] += jnp.sum(dvn * vhat, axis=0, keepdims=True)
        dvh = dvn * sng
        dv = r * (dvh - vhat * _group_mean(dvh * vhat, gmat))
        dga = jnp.concatenate([du, dv], axis=-1)
        dgelu = phi + a * (_INV_SQRT_2PI * jnp.exp(-0.5 * a * a))
        dab = (dga * dgelu).astype(da_ref.dtype)
        da_ref[...] = dab

        dh = _dot_nt(dab, w_ref[:, :2 * W_A]) + _dot_nt(rest_ref[...], w_ref[:, 2 * W_A:])
        xv = x_ref[...]
        rx = lax.rsqrt(jnp.mean(xv * xv, axis=-1, keepdims=True) + EPS)
        xhat = xv * rx
        dg_ref[...] += jnp.sum(dh * xhat, axis=0, keepdims=True)
        dxh = dh * g_ref[...]
        dx_ref[...] = r_ref[...] + rx * (dxh - xhat * jnp.mean(dxh * xhat, axis=-1, keepdims=True))

    n_rest = IN_COLS - 2 * W_A
    return _call(
        body, name="sgu_in_proj_bwd", grid=(S // tm,),
        in_specs=[_row_spec(tm, 2 * W_A), _row_spec(tm, W_A), _full_spec((1, W_A)), _full_spec((4, CHUNK, CHUNK)),
                  _full_spec((4, CHUNK, CHUNK)), _full_spec((CHUNK, W_A)), _full_spec((LANES, LANES)),
                  _row_spec(tm, n_rest),
                  pl.BlockSpec((D, IN_COLS), lambda i: (0, 0), pipeline_mode=pl.Buffered(1)),
                  _row_spec(tm, D), _full_spec((1, D)), _row_spec(tm, D)],
        out_specs=[_row_spec(tm, 2 * W_A), _full_spec((4, CHUNK, CHUNK)), _full_spec((CHUNK, W_A)),
                   _full_spec((1, W_A)), _row_spec(tm, D), _full_spec((1, D))],
        out_shape=[jax.ShapeDtypeStruct((S, 2 * W_A), MXU_DT), jax.ShapeDtypeStruct((4, CHUNK, CHUNK), F32),
                   jax.ShapeDtypeStruct((CHUNK, W_A), F32), jax.ShapeDtypeStruct((1, W_A), F32),
                   jax.ShapeDtypeStruct((S, D), F32), jax.ShapeDtypeStruct((1, D), F32)],
        compiler_params=_params("arbitrary"),
    )(proj, dy, sng, wm, wmt, bias, gmat, rest, w, x, g, dres)


HG = 4
LW = HG * HEAD_DIM
Q_BLK0 = (2 * W_A) // LW
K_BLK0 = Q_BLK0 + W_B // LW
V_BLK0 = K_BLK0 + W_B // LW
N_GROUPS = W_B // LW
EXP_IS_ZERO_BELOW = -120.0


def _tri_matrix():
    r = jnp.arange(TK)
    return (r[:, None] > r[None, :]).astype(MXU_DT)


def _stack_heads(a):
    head = lax.broadcasted_iota(jnp.int32, a.shape, 1) // HEAD_DIM
    return jnp.concatenate([jnp.where(head == h, a, 0.0) for h in range(HG)], axis=0).astype(MXU_DT)


def _unstack_heads(a):
    head = lax.broadcasted_iota(jnp.int32, (TQ, LW), 1) // HEAD_DIM
    out = a[:TQ]
    for h in range(1, HG):
        out = jnp.where(head == h, a[h * TQ:(h + 1) * TQ], out)
    return out


def _sb_scores(q2, kj, tri, key_offset):
    z = _dot_nt(q2, kj)
    sp = jnp.log(1.0 + jnp.exp(-jnp.abs(z)))
    lsp = jnp.minimum(z, 0.0) - sp
    lsm = lsp - z
    msk = None
    if key_offset is not None:
        row = lax.broadcasted_iota(jnp.int32, z.shape, 0) & (TQ - 1)
        col = lax.broadcasted_iota(jnp.int32, z.shape, 1) + key_offset
        msk = col < row
        lsm = jnp.where(msk, lsm, 0.0)
    tail = _dot(lsm.astype(MXU_DT), tri)
    return lsp, lsm, tail, msk


def _sb_fwd(proj_b, tri, carry_gather=None):
    S = proj_b.shape[0]
    nq = S // TQ
    kpq = TQ // TK
    assert S // TK < LANES
    shards = list(carry_gather or [])
    ng = len(shards)

    def body(*refs):
        q_ref, k_ref, v_ref, tri_ref = refs[:4]
        o_ref, rb_ref = refs[4 + ng:6 + ng]
        acc_ref = refs[6 + 2 * ng]
        i = pl.program_id(1)
        if ng:
            step = pl.program_id(0) * nq + i
            sender = 1
            c, sends, arrivals, forwards = _gather_plan(refs[4:4 + ng], refs[6 + ng:6 + 2 * ng], refs[7 + 2 * ng],
                                                        refs[8 + 2 * ng], 1, sender)

            @pl.when(jnp.logical_and(step == 0, c == sender))
            def _():
                for cp in sends:
                    cp.start()

            @pl.when(jnp.logical_and(step == N_GROUPS * nq - max(nq // 8, 1), c == sender))
            def _():
                for arrived, onward in zip(arrivals, forwards):
                    arrived.wait_recv()
                    onward.start()


        lane2 = lax.broadcasted_iota(jnp.int32, (HG * TQ, LANES), 1)
        q2 = _stack_heads(q_ref[...].astype(F32) * (HEAD_DIM ** -0.5))
        tri = tri_ref[...]
        rb_ref[...] = jnp.zeros_like(rb_ref)

        def block(j, run, key_offset=None, first=False):
            start = pl.multiple_of(j * TK, TK)
            kj = k_ref[pl.ds(start, TK), :]
            vj = v_ref[pl.ds(start, TK), :]
            half = HG * TQ // 2
            out = []
            for r0 in (0, half):
                rows = pl.ds(r0, half)
                run_h = run[r0:r0 + half]
                lsp, lsm, tail, msk = _sb_scores(q2[r0:r0 + half], kj, tri, key_offset)
                rb_ref[rows, :] = jnp.where(lane2[:half] == j, run_h, rb_ref[rows, :])
                att = jnp.exp(lsp + tail + run_h)
                if msk is not None:
                    att = jnp.where(msk, att, 0.0)
                pv = _dot(att.astype(MXU_DT), vj)
                if first:
                    acc_ref[rows, :] = pv
                else:
                    acc_ref[rows, :] += pv
                out.append(run_h + tail[:, :1] + lsm[:, :1])
            return jnp.concatenate(out, axis=0)

        past = i * kpq

        def overlapping():
            run = jnp.zeros((HG * TQ, 1), F32)
            for d in reversed(range(kpq)):
                run = block(i * kpq + d, run, key_offset=d * TK, first=(d == kpq - 1))
            return run

        def alive(run):
            return (jnp.max(run) > EXP_IS_ZERO_BELOW).astype(jnp.int32)

        def walk(carry):
            n, run, _ = carry
            run = block(past - 1 - n, run)
            return n + 1, run, alive(run)

        n, run = lax.cond(past > 0, lambda: (jnp.int32(1), block(past - 1, overlapping())),
                          lambda: (jnp.int32(0), overlapping()))
        n, _, _ = lax.while_loop(lambda s: jnp.logical_and(s[0] < past, s[2] > 0), walk, (n, run, alive(run)))
        rb_ref[...] = jnp.where(lane2 == LANES - 1, n.astype(F32), rb_ref[...])
        o_ref[...] = _unstack_heads(acc_ref[...])

        if ng:
            @pl.when(jnp.logical_and(step == N_GROUPS * nq - 1, c == sender))
            def _():
                for cp in sends + forwards:
                    cp.wait_send()

            @pl.when(jnp.logical_and(step == N_GROUPS * nq - 1, c != sender))
            def _():
                for cp in forwards:
                    cp.wait_recv()

    once = pl.Buffered(1)
    gathered, sems = _gather_shapes(shards) if ng else ([], [])
    return _call(
        body, name="sb_fwd_gather" if ng else "sb_fwd", grid=(N_GROUPS, nq),
        in_specs=[pl.BlockSpec((TQ, LW), lambda p, i: (i, Q_BLK0 + p)),
                  pl.BlockSpec((S, LW), lambda p, i: (0, K_BLK0 + p), pipeline_mode=once),
                  pl.BlockSpec((S, LW), lambda p, i: (0, V_BLK0 + p), pipeline_mode=once),
                  pl.BlockSpec((TK, TK), lambda p, i: (0, 0))] + [ANY] * ng,
        out_specs=[pl.BlockSpec((TQ, LW), lambda p, i: (i, p)),
                   pl.BlockSpec((None, None, HG * TQ, LANES), lambda p, i: (p, i, 0, 0))] + [ANY] * ng,
        out_shape=[jax.ShapeDtypeStruct((S, W_B), F32),
                   jax.ShapeDtypeStruct((N_GROUPS, nq, HG * TQ, LANES), F32)] + gathered,
        scratch_shapes=[pltpu.VMEM((HG * TQ, LW), F32)] + sems,
        compiler_params=_params("arbitrary", "arbitrary"),
    )(proj_b, proj_b, proj_b, tri, *shards)


def _sb_bwd(proj_b, dyb, rb, tri, trit, carry_exchange=None):
    S = proj_b.shape[0]
    nq = S // TQ
    kpq = TQ // TK
    hs = list(carry_exchange or [])
    ne = len(hs)

    def body(*refs):
        q_ref, k_ref, v_ref, do_ref, rb_ref, tri_ref, trit_ref = refs[:7]
        dq_ref, dk_acc, dv_acc = refs[7 + ne:10 + ne]
        dq_acc = refs[10 + 2 * ne]
        i = pl.program_id(1)
        if ne:
            tick = pl.program_id(0) * nq + i
            sends, arrivals = _exchange_plan(refs[7:7 + ne], refs[10 + ne:10 + 2 * ne], refs[11 + 2 * ne],
                                             refs[12 + 2 * ne])

            @pl.when(tick == 0)
            def _():
                for cp in sends:
                    cp.start()

        lane2 = lax.broadcasted_iota(jnp.int32, (HG * TQ, LANES), 1)
        scale = HEAD_DIM ** -0.5
        q2 = _stack_heads(q_ref[...].astype(F32) * scale)
        do2 = _stack_heads(do_ref[...])
        tri = tri_ref[...]
        trit = trit_ref[...]

        @pl.when(i == 0)
        def _():
            dk_acc[...] = jnp.zeros_like(dk_acc)
            dv_acc[...] = jnp.zeros_like(dv_acc)

        dq_acc[...] = jnp.zeros_like(dq_acc)

        def block(j, pre, key_offset=None):
            start = pl.multiple_of(j * TK, TK)
            kj = k_ref[pl.ds(start, TK), :]
            vj = v_ref[pl.ds(start, TK), :]
            lsp, lsm, tail, msk = _sb_scores(q2, kj, tri, key_offset)
            run = jnp.sum(jnp.where(lane2 == j, rb_ref[...], 0.0), axis=-1, keepdims=True)
            att = jnp.exp(lsp + tail + run)
            if msk is not None:
                att = jnp.where(msk, att, 0.0)
            beta = jnp.exp(lsp)
            dl = _dot_nt(do2, vj) * att
            cin = _dot(dl.astype(MXU_DT), trit)
            dz = dl * (1.0 - beta) - beta * (pre + cin)
            if msk is not None:
                dz = jnp.where(msk, dz, 0.0)
            dzb = dz.astype(MXU_DT)
            dq_acc[...] += _dot(dzb, kj)
            dk_acc[pl.ds(start, TK), :] += _dot_tn(dzb, q2)
            dv_acc[pl.ds(start, TK), :] += _dot_tn(att.astype(MXU_DT), do2)
            return pre + cin[:, TK - 1:] + dl[:, TK - 1:]

        past = i * kpq
        walked = jnp.max(jnp.where(lane2[:8] == LANES - 1, rb_ref[pl.ds(0, 8), :], 0.0)).astype(jnp.int32)
        walked = jnp.clip(walked, 0, past)
        def overlapping(pre):
            for d in range(kpq):
                pre = block(i * kpq + d, pre, key_offset=d * TK)
            return jnp.int32(0)

        def with_past():
            pre = lax.fori_loop(past - walked, past - 1, lambda j, pre: block(j, pre),
                                jnp.zeros((HG * TQ, 1), F32))
            return overlapping(block(past - 1, pre))

        lax.cond(walked > 0, with_past, lambda: overlapping(jnp.zeros((HG * TQ, 1), F32)))
        dq_ref[...] = (_unstack_heads(dq_acc[...]) * scale).astype(dq_ref.dtype)

        if ne:
            @pl.when(tick == N_GROUPS * nq - 1)
            def _():
                for cp in arrivals:
                    cp.wait_recv()
                for cp in sends:
                    cp.wait_send()

    once = pl.Buffered(1)
    exchanged, sems = _exchange_shapes(hs) if ne else ([], [])
    return _call(
        body, name="sb_bwd_exchange" if ne else "sb_bwd", grid=(N_GROUPS, nq),
        in_specs=[pl.BlockSpec((TQ, LW), lambda p, i: (i, Q_BLK0 + p)),
                  pl.BlockSpec((S, LW), lambda p, i: (0, K_BLK0 + p), pipeline_mode=once),
                  pl.BlockSpec((S, LW), lambda p, i: (0, V_BLK0 + p), pipeline_mode=once),
                  pl.BlockSpec((TQ, LW), lambda p, i: (i, p)),
                  pl.BlockSpec((None, None, HG * TQ, LANES), lambda p, i: (p, i, 0, 0)),
                  pl.BlockSpec((TK, TK), lambda p, i: (0, 0)),
                  pl.BlockSpec((TK, TK), lambda p, i: (0, 0))] + [ANY] * ne,
        out_specs=[pl.BlockSpec((TQ, LW), lambda p, i: (i, p)),
                   pl.BlockSpec((S, LW), lambda p, i: (0, p), pipeline_mode=once),
                   pl.BlockSpec((S, LW), lambda p, i: (0, p), pipeline_mode=once)] + [ANY] * ne,
        out_shape=[jax.ShapeDtypeStruct((S, W_B), MXU_DT), jax.ShapeDtypeStruct((S, W_B), F32),
                   jax.ShapeDtypeStruct((S, W_B), F32)] + exchanged,
        scratch_shapes=[pltpu.VMEM((HG * TQ, LW), F32)] + sems,
        compiler_params=_params("arbitrary", "arbitrary"),
    )(proj_b, proj_b, proj_b, dyb, rb, tri, trit, *hs)


P_BLK = (2 * W_A + 3 * W_B) // W_C


def _window_lanes():
    g = lax.broadcasted_iota(jnp.int32, (1, W_C), 1) // (W_C // 4)
    w = jnp.where(g == 0, POOL_WINDOWS[0], jnp.where(g == 1, POOL_WINDOWS[1],
                  jnp.where(g == 2, POOL_WINDOWS[2], POOL_WINDOWS[3])))
    return g, w


def _shift_rows(ext, k, tm, lead):
    n = ext.shape[0]
    return pltpu.roll(ext, shift=k % n, axis=0)[lead:lead + tm]


def _pool_diff(p_cur, p_halo, row0, tm):
    ext = jnp.concatenate([p_halo, p_cur], axis=0)
    g, w = _window_lanes()
    acc = ext
    sums = []
    for sh in (1, 2, 4, 8):
        acc = acc + pltpu.roll(acc, shift=sh, axis=0)
        sums.append(acc[HALO:HALO + tm])
    wsum = jnp.where(g == 0, sums[0], jnp.where(g == 1, sums[1], jnp.where(g == 2, sums[2], sums[3])))
    pos = (row0 + 1 + lax.broadcasted_iota(jnp.int32, (tm, W_C), 0)).astype(F32)
    cnt = jnp.minimum(pos, w.astype(F32))
    return wsum / cnt - p_cur, cnt


def _pool_specs(tm, nrow_blocks_halo):
    cur = pl.BlockSpec((tm, W_C), lambda i: (i, P_BLK))
    prev = pl.BlockSpec((HALO, W_C), lambda i: (jnp.maximum(i * (tm // HALO) - 1, 0), P_BLK))
    return cur, prev


def _in_proj_groups(x, g, w, sng, wm, bias, gmat, wbd, scale, carry_gather=None):
    S, D = x.shape
    tm = TM_MM
    nb = S // tm
    p0 = 2 * W_A + 3 * W_B
    qkv_chunk = 3 * W_B // 2
    shards = list(carry_gather or [])
    ng = len(shards)

    def body(*refs):
        x_ref, g_ref, w_ref, sng_ref, wm_ref, b_ref, gmat_ref, wbd_ref, sc_ref = refs[:9]
        h_ref, o_ref, ob_ref, ya_ref, yc_ref = refs[9 + ng:14 + ng]
        tail_ref = refs[14 + 2 * ng]
        i = pl.program_id(0)
        if ng:
            sender = 0
            c, sends, arrivals, forwards = _gather_plan(refs[9:9 + ng], refs[14 + ng:14 + 2 * ng], refs[15 + 2 * ng],
                                                        refs[16 + 2 * ng], 0, sender)

            @pl.when(jnp.logical_and(i == 0, c == sender))
            def _():
                for cp in sends:
                    cp.start()
        xv = x_ref[...]
        r = lax.rsqrt(jnp.mean(xv * xv, axis=-1, keepdims=True) + EPS)
        h = (xv * r * g_ref[...]).astype(h_ref.dtype)
        h_ref[...] = h

        def project(c0, c1):
            acc = _dot(h, w_ref[:, c0:c1])
            o_ref[:, c0:c1] = acc
            ob_ref[:, c0:c1] = acc.astype(ob_ref.dtype)
            return acc

        a = project(0, 2 * W_A)
        _, u, _, _, _, s = _sgu_common(a, sng_ref[...], wm_ref, b_ref[...], gmat_ref[...])
        ya_ref[...] = u * s
        for c0 in range(2 * W_A, p0, qkv_chunk):
            project(c0, c0 + qkv_chunk)
        p = project(p0, p0 + W_C)
        halo = jnp.where(i > 0, tail_ref[...], 0.0)
        tail_ref[...] = p[tm - HALO:]
        d, _ = _pool_diff(p, halo, i * tm, tm)
        yc_ref[...] = _dot(d.astype(MXU_DT), wbd_ref[...]) * sc_ref[...]

        if ng:
            @pl.when(jnp.logical_and(i == nb - 1, c == sender))
            def _():
                for arrived, onward in zip(arrivals, forwards):
                    arrived.wait_recv()
                    onward.start()
                for cp in sends + forwards:
                    cp.wait_send()

            @pl.when(jnp.logical_and(i == nb - 1, c != sender))
            def _():
                for cp in forwards:
                    cp.wait_recv()

    gathered, sems = _gather_shapes(shards) if ng else ([], [])
    return _call(
        body, name="in_proj_groups_gather" if ng else "in_proj_groups", grid=(nb,),
        in_specs=[_row_spec(tm, D), _full_spec((1, D)),
                  pl.BlockSpec((D, IN_COLS), lambda i: (0, 0), pipeline_mode=pl.Buffered(1)),
                  _full_spec((1, W_A)), _full_spec((4, CHUNK, CHUNK)), _full_spec((CHUNK, W_A)),
                  _full_spec((LANES, LANES)), _full_spec((W_C, W_C)), _full_spec((1, W_C))] + [ANY] * ng,
        out_specs=[_row_spec(tm, D), _row_spec(tm, IN_COLS), _row_spec(tm, IN_COLS), _row_spec(tm, W_A),
                   _row_spec(tm, W_C)] + [ANY] * ng,
        out_shape=[jax.ShapeDtypeStruct((S, D), MXU_DT), jax.ShapeDtypeStruct((S, IN_COLS), F32),
                   jax.ShapeDtypeStruct((S, IN_COLS), MXU_DT), jax.ShapeDtypeStruct((S, W_A), F32),
                   jax.ShapeDtypeStruct((S, W_C), F32)] + gathered,
        scratch_shapes=[pltpu.VMEM((HALO, W_C), F32)] + sems,
        compiler_params=_params("arbitrary"),
    )(x, g, w, sng, wm, bias, gmat, wbd, scale, *shards)


def _pool_bwd_a(proj, dy, wbd, scale):
    S = proj.shape[0]
    tm = TM_MM

    def body(p_ref, ph_ref, dy_ref, w_ref, sc_ref, dd_ref, e_ref, dw_ref, dsc_ref):
        i = pl.program_id(0)
        halo = jnp.where(i > 0, ph_ref[...], 0.0)
        d, cnt = _pool_diff(p_ref[...], halo, i * tm, tm)
        db = d.astype(MXU_DT)
        dy = dy_ref[...]

        @pl.when(i == 0)
        def _():
            dw_ref[...] = jnp.zeros_like(dw_ref)
            dsc_ref[...] = jnp.zeros_like(dsc_ref)

        dsc_ref[...] += jnp.sum(dy * _dot(db, w_ref[...]), axis=0, keepdims=True)
        dys = (dy * sc_ref[...]).astype(MXU_DT)
        dw_ref[...] += _dot_tn(db, dys)
        dd = _dot_nt(dys, w_ref[...])
        dd_ref[...] = dd
        e_ref[...] = dd / cnt

    cur, prev = _pool_specs(tm, S // HALO)
    return _call(
        body, name="pool_bwd_a", grid=(S // tm,),
        in_specs=[cur, prev, _row_spec(tm, W_C), _full_spec((W_C, W_C)), _full_spec((1, W_C))],
        out_specs=[_row_spec(tm, W_C), _row_spec(tm, W_C), _full_spec((W_C, W_C)), _full_spec((1, W_C))],
        out_shape=[jax.ShapeDtypeStruct((S, W_C), F32), jax.ShapeDtypeStruct((S, W_C), F32),
                   jax.ShapeDtypeStruct((W_C, W_C), F32), jax.ShapeDtypeStruct((1, W_C), F32)],
        compiler_params=_params("arbitrary"),
    )(proj, proj, dy, wbd, scale)


def _pool_bwd_b(dd, e):
    S = dd.shape[0]
    tm = TM_MM
    nb = S // tm

    def body(dd_ref, e_ref, en_ref, dp_ref):
        i = pl.program_id(0)
        halo = jnp.where(i < nb - 1, en_ref[...], 0.0)
        ext = jnp.concatenate([e_ref[...], halo], axis=0)
        n = ext.shape[0]
        g, _ = _window_lanes()
        acc = ext
        sums = []
        for sh in (1, 2, 4, 8):
            acc = acc + pltpu.roll(acc, shift=n - sh, axis=0)
            sums.append(acc[:tm])
        wsum = jnp.where(g == 0, sums[0], jnp.where(g == 1, sums[1], jnp.where(g == 2, sums[2], sums[3])))
        dp_ref[...] = (wsum - dd_ref[...]).astype(dp_ref.dtype)

    nxt = pl.BlockSpec((HALO, W_C), lambda i: (jnp.minimum((i + 1) * (tm // HALO), S // HALO - 1), 0))
    return _call(
        body, name="pool_bwd_b", grid=(nb,),
        in_specs=[_row_spec(tm, W_C), _row_spec(tm, W_C), nxt],
        out_specs=_row_spec(tm, W_C),
        out_shape=jax.ShapeDtypeStruct((S, W_C), MXU_DT),
        compiler_params=_params("parallel"),
    )(dd, e, e)


TN_FF = 1408
NB_FF = D_FF // TN_FF
CONV_ROWS = 8


def _conv(z_cur, z_halo, cwb, tm):
    ext = jnp.concatenate([z_halo, z_cur], axis=0)
    z2 = _shift_rows(ext, 2, tm, HALO)
    z1 = _shift_rows(ext, 1, tm, HALO)
    zc = cwb[3:4] + z2 * cwb[0:1] + z1 * cwb[1:2] + z_cur * cwb[2:3]
    return zc, z2, z1


def _up_proj_gate(x, g, w, cwb):
    S, D = x.shape
    tm = TM_MM

    def body(x_ref, g_ref, w_ref, c_ref, h_ref, z_ref, zc_ref, f_ref, tail_ref):
        first = pl.program_id(0) == 0
        xv = x_ref[...]
        r = lax.rsqrt(jnp.mean(xv * xv, axis=-1, keepdims=True) + EPS)
        h = (xv * r * g_ref[...]).astype(h_ref.dtype)
        h_ref[...] = h
        for j in range(NB_FF):
            halves = []
            for col0 in (j * TN_FF, D_FF + j * TN_FF):
                zb = _dot(h, w_ref[:, col0:col0 + TN_FF]).astype(z_ref.dtype)
                z_ref[:, col0:col0 + TN_FF] = zb
                zf = zb.astype(F32)
                prev = jnp.where(first, 0.0, tail_ref[:, col0:col0 + TN_FF])
                tail_ref[:, col0:col0 + TN_FF] = zf[tm - HALO:]
                zc = _conv(zf, prev, c_ref[:, col0:col0 + TN_FF], tm)[0]
                zc_ref[:, col0:col0 + TN_FF] = zc.astype(zc_ref.dtype)
                halves.append(zc)
            gate, value = halves
            f_ref[:, j * TN_FF:(j + 1) * TN_FF] = (gate * jax.nn.sigmoid(gate) * value).astype(f_ref.dtype)

    return _call(
        body, name="up_proj_gate", grid=(S // tm,),
        in_specs=[_row_spec(tm, D), _full_spec((1, D)),
                  pl.BlockSpec((D, 2 * D_FF), lambda i: (0, 0), pipeline_mode=pl.Buffered(1)),
                  _full_spec((CONV_ROWS, 2 * D_FF))],
        out_specs=[_row_spec(tm, D), _row_spec(tm, 2 * D_FF), _row_spec(tm, 2 * D_FF), _row_spec(tm, D_FF)],
        out_shape=[jax.ShapeDtypeStruct((S, D), MXU_DT), jax.ShapeDtypeStruct((S, 2 * D_FF), MXU_DT),
                   jax.ShapeDtypeStruct((S, 2 * D_FF), MXU_DT), jax.ShapeDtypeStruct((S, D_FF), MXU_DT)],
        scratch_shapes=[pltpu.VMEM((HALO, 2 * D_FF), F32)],
        compiler_params=_params("arbitrary"),
    )(x, g, w, cwb)


def _gate_up_bwd(z, zc, cwb, w, wd, x, g, dres):
    S, D = x.shape
    tm = TM
    nb = S // tm

    def body(z_ref, zc_ref, zcn_ref, c_ref, w_ref, wd_ref, x_ref, g_ref, r_ref, rn_ref,
             dz_ref, dc_ref, dx_ref, dg_ref):
        i = pl.program_id(0)
        first = i == 0
        last = i == nb - 1
        dxe = jnp.concatenate([r_ref[...], jnp.where(last, 0.0, rn_ref[...])], axis=0).astype(MXU_DT)

        @pl.when(first)
        def _():
            dc_ref[...] = jnp.zeros_like(dc_ref)
            dg_ref[...] = jnp.zeros_like(dg_ref)

        rid = lax.broadcasted_iota(jnp.int32, (CONV_ROWS, TN_FF), 0)

        def conv_out(cols):
            return jnp.concatenate([zc_ref[:, cols].astype(F32), zcn_ref[:, cols].astype(F32)], axis=0)

        def conv_bwd(d, z0, c):
            d0 = d[:tm]
            d1 = _shift_rows(d, -1, tm, 0)
            d2 = _shift_rows(d, -2, tm, 0)
            sums = [jnp.sum(d2 * z0, axis=0, keepdims=True), jnp.sum(d1 * z0, axis=0, keepdims=True),
                    jnp.sum(d0 * z0, axis=0, keepdims=True), jnp.sum(d0, axis=0, keepdims=True)]
            dtaps = jnp.zeros((CONV_ROWS, TN_FF), F32)
            for k, v in enumerate(sums):
                dtaps = jnp.where(rid == k, v, dtaps)
            return d0 * c[2:3] + d1 * c[1:2] + d2 * c[0:1], dtaps

        dh = jnp.zeros((tm, D), F32)
        for j in range(NB_FF):
            gc = slice(j * TN_FF, (j + 1) * TN_FF)
            uc = slice(D_FF + j * TN_FF, D_FF + (j + 1) * TN_FF)
            gt = conv_out(gc)
            ut = conv_out(uc)
            df = _dot_nt(dxe, wd_ref[gc, :])
            sg = jax.nn.sigmoid(gt)
            dzg, dtg = conv_bwd(df * ut * (sg * (1.0 + gt * (1.0 - sg))), z_ref[:, gc].astype(F32), c_ref[:, gc])
            dzu, dtu = conv_bwd(df * (gt * sg), z_ref[:, uc].astype(F32), c_ref[:, uc])
            dzg = dzg.astype(dz_ref.dtype)
            dzu = dzu.astype(dz_ref.dtype)
            dz_ref[:, gc] = dzg
            dz_ref[:, uc] = dzu
            dc_ref[:, gc] += dtg
            dc_ref[:, uc] += dtu
            dh += _dot_nt(dzg, w_ref[:, gc]) + _dot_nt(dzu, w_ref[:, uc])

        xv = x_ref[...]
        r = lax.rsqrt(jnp.mean(xv * xv, axis=-1, keepdims=True) + EPS)
        xhat = xv * r
        dg_ref[...] += jnp.sum(dh * xhat, axis=0, keepdims=True)
        dxh = dh * g_ref[...]
        dx_ref[...] = r_ref[...] + r * (dxh - xhat * jnp.mean(dxh * xhat, axis=-1, keepdims=True))

    hb = tm // HALO
    last_halo = S // HALO - 1
    return _call(
        body, name="gate_up_bwd", grid=(nb,),
        in_specs=[_row_spec(tm, 2 * D_FF), _row_spec(tm, 2 * D_FF),
                  pl.BlockSpec((HALO, 2 * D_FF), lambda i: (jnp.minimum((i + 1) * hb, last_halo), 0)),
                  _full_spec((CONV_ROWS, 2 * D_FF)),
                  pl.BlockSpec((D, 2 * D_FF), lambda i: (0, 0), pipeline_mode=pl.Buffered(1)),
                  pl.BlockSpec((D_FF, D), lambda i: (0, 0), pipeline_mode=pl.Buffered(1)),
                  _row_spec(tm, D), _full_spec((1, D)), _row_spec(tm, D),
                  pl.BlockSpec((HALO, D), lambda i: (jnp.minimum((i + 1) * hb, last_halo), 0))],
        out_specs=[_row_spec(tm, 2 * D_FF), _full_spec((CONV_ROWS, 2 * D_FF)), _row_spec(tm, D), _full_spec((1, D))],
        out_shape=[jax.ShapeDtypeStruct((S, 2 * D_FF), MXU_DT), jax.ShapeDtypeStruct((CONV_ROWS, 2 * D_FF), F32),
                   jax.ShapeDtypeStruct((S, D), F32), jax.ShapeDtypeStruct((1, D), F32)],
        compiler_params=_params("arbitrary"),
    )(z, zc, zc, cwb, w, wd, x, g, dres, dres)


def _layer_consts(w, l):
    wm = _tril_weights(w["sgu_w"][l])
    eye = jnp.eye(4, dtype=F32)
    wbd = (w["pool_w"][l][:, :, None, :] * eye[:, None, :, None]).reshape(W_C, W_C)
    cwb = jnp.concatenate([w["conv_w"][l], w["conv_b"][l][None], jnp.zeros((CONV_ROWS - 4, 2 * D_FF), F32)], axis=0)
    return dict(
        g1=w["norm1_g"][l][None], g2=w["norm2_g"][l][None], gm=w["mix_norm_g"][l][None],
        sng=w["sgu_norm_g"][l][None], wm=wm.astype(MXU_DT), wmt=jnp.swapaxes(wm, 1, 2).astype(MXU_DT),
        bias=jnp.repeat(jnp.transpose(w["sgu_b"][l]), HEAD_DIM, axis=1),
        wbd=wbd.astype(MXU_DT), scale=w["pool_scale"][l][None], cwb=cwb,
    )


def _local_step(x, tgt, w, late=None, early_exchange=None):
    gmat = _group_matrix()
    tri = _tri_matrix()
    trit = jnp.transpose(tri)
    saved = []
    early = None
    big = {n: [w[n][l] for l in range(DEPTH)] for n in BIG_NAMES[:4]}
    for l in range(DEPTH):
        c = _layer_consts(w, l)
        if l == 0 and late is not None:
            assert DEPTH == 2
            shards = late["shards"]
            h1, proj, proj_b, ya, yc, *gathered = _in_proj_groups(
                x, c["g1"], big["w_in"][l], c["sng"], c["wm"], c["bias"], gmat, c["wbd"], c["scale"],
                carry_gather=shards[1:])
            for name, arr in late["assemble"](gathered, shards[1:], 0, BIG_NAMES[1:4]).items():
                big[name][0] = arr
            yb, rb, *gathered = _sb_fwd(proj_b, tri, carry_gather=shards)
            for name, arr in late["assemble"](gathered, shards, 1, BIG_NAMES[:4]).items():
                big[name][1] = arr
        else:
            h1, proj, proj_b, ya, yc = _in_proj_groups(x, c["g1"], big["w_in"][l], c["sng"], c["wm"], c["bias"],
                                                       gmat, c["wbd"], c["scale"])
            yb, rb = _sb_fwd(proj_b, tri)
        x2, yn = _mix_out(ya, yb, yc, c["gm"], big["w_o"][l], x, gmat)
        h2, z, zc, f = _up_proj_gate(x2, c["g2"], big["w_up"][l], c["cwb"])
        saved.append(dict(c=c, x=x, proj=proj, proj_b=proj_b, h1=h1, ya=ya, yb=yb, yc=yc, rb=rb, x2=x2, yn=yn,
                          z=z, zc=zc, h2=h2, f=f))
        if l < DEPTH - 1:
            x = _mm_res(f, big["w_down"][l], x2, "down_proj")
    for l in range(DEPTH):
        saved[l]["c"] = dict(saved[l]["c"], **{n: big[n][l] for n in BIG_NAMES[:4]})

    last = saved[-1]
    dx, d_final_g, loss8 = _down_proj_loss(last["f"], last["c"]["w_down"], last["x2"], w["final_g"][None], tgt)
    grads = {n: [None] * DEPTH for n in ("norm1_g", "w_in", "sgu_norm_g", "sgu_w", "sgu_b", "pool_w", "pool_scale",
                                         "mix_norm_g", "w_o", "norm2_g", "w_up", "conv_w", "conv_b", "w_down")}
    for l in reversed(range(DEPTH)):
        s = saved[l]
        c = s["c"]
        grads["w_down"][l] = _mm_tn(s["f"], dx, "down_proj_wgrad").reshape(N_CHIPS, D_FF // N_CHIPS, D_MODEL)
        dz, dcwb, dx2, dg2 = _gate_up_bwd(s["z"], s["zc"], c["cwb"], c["w_up"], c["w_down"], s["x2"], c["g2"], dx)
        grads["conv_w"][l] = dcwb[:3]
        grads["conv_b"][l] = dcwb[3]
        grads["w_up"][l] = _mm_tn(s["h2"], dz, "up_proj_wgrad", col_tiles=True)
        grads["norm2_g"][l] = dg2[0]
        grads["w_o"][l] = _mm_tn(s["yn"], dx2, "out_proj_wgrad").reshape(N_CHIPS, D_MODEL // N_CHIPS, D_MODEL)
        if l == 0 and early_exchange is not None:
            early_items = early_exchange[0](grads)
            dya, dyb, dyc, dgm, *swapped = _mix_out_bwd(dx2, c["w_o"], s["ya"], s["yb"], s["yc"], c["gm"], gmat,
                                                        carry_swap=early_items)
        else:
            dya, dyb, dyc, dgm = _mix_out_bwd(dx2, c["w_o"], s["ya"], s["yb"], s["yc"], c["gm"], gmat)
        grads["mix_norm_g"][l] = dgm[0]
        dd, e, dwbd, dscale = _pool_bwd_a(s["proj"], dyc, c["wbd"], c["scale"])
        dp = _pool_bwd_b(dd, e)
        grads["pool_w"][l] = jnp.stack([dwbd[g * 64:(g + 1) * 64, g * 64:(g + 1) * 64] for g in range(4)])
        grads["pool_scale"][l] = dscale[0]
        if l == 0 and early_exchange is not None:
            sent = early_exchange[1](early_items, swapped)
            dq, dk, dv, *parts = _sb_bwd(s["proj_b"], dyb, s["rb"], tri, trit, carry_exchange=sent)
            early = (sent, parts)
        else:
            dq, dk, dv = _sb_bwd(s["proj_b"], dyb, s["rb"], tri, trit)
        rest = jnp.concatenate([dq, dk.astype(MXU_DT), dv.astype(MXU_DT), dp], axis=1)
        da, dwm, dbias, dsng, dx, dg1 = _sgu_in_proj_bwd(s["proj"], dya, c["sng"], c["wm"], c["wmt"], c["bias"],
                                                         gmat, rest, c["w_in"], s["x"], c["g1"], dx2)
        grads["sgu_w"][l] = dwm
        grads["sgu_b"][l] = jnp.transpose(jnp.sum(dbias.reshape(CHUNK, 4, HEAD_DIM), axis=-1))
        grads["sgu_norm_g"][l] = dsng[0]
        grads["norm1_g"][l] = dg1[0]
        dw_in = jnp.concatenate([_mm_tn(s["h1"], da, "in_proj_wgrad_unit"),
                                 _mm_tn(s["h1"], rest, "in_proj_wgrad_rest")], axis=1)
        grads["w_in"][l] = jnp.transpose(dw_in.reshape(D_MODEL, N_CHIPS, IN_COLS // N_CHIPS), (1, 0, 2))

    out = {n: (v if n in BIG_NAMES[:4] else jnp.stack(v)) for n, v in grads.items()}
    out["final_g"] = d_final_g[0]
    return loss8[0, 0], dx, out, early


MESH = pl.DeviceIdType.MESH
ANY = pl.BlockSpec(memory_space=pl.ANY)


def _gather_plan(ins, outs, send_sems, recv_sems, layer, sender):
    n = len(ins)
    x, y, c = lax.axis_index("x"), lax.axis_index("y"), lax.axis_index("c")
    sibling = (x, y, 1 - c)
    my_chip = 2 * x + y
    chips = [(1 - x, y), (x, 1 - y), (1 - x, 1 - y)]
    ids = [2 * px + py for px, py in chips]

    def copy(a, k, chip, to, own=False):
        dst = outs[a].at[chip]
        return pltpu.make_async_remote_copy(
            src_ref=ins[a].at[layer] if own else dst, dst_ref=dst,
            send_sem=send_sems.at[a, k], recv_sem=recv_sems.at[a, k], device_id=to, device_id_type=MESH)

    sends = [copy(a, j, my_chip, (*chips[j], sender), own=True) for j in range(3) for a in range(n)]
    arrivals = [copy(a, j, ids[j], sibling) for j in range(3) for a in range(n)]
    forwards = [copy(a, 3 + j, ids[j], sibling) for j in range(3) for a in range(n)]
    return c, sends, arrivals, forwards


def _gather_shapes(shards):
    n = len(shards)
    return ([jax.ShapeDtypeStruct((N_CHIPS,) + s.shape[1:], s.dtype) for s in shards],
            [pltpu.SemaphoreType.DMA((n, 6)), pltpu.SemaphoreType.DMA((n, 6))])


def _all_gather(shards, layer, sender):
    n = len(shards)

    def body(*refs):
        c, sends, arrivals, forwards = _gather_plan(refs[:n], refs[n:2 * n], refs[2 * n], refs[2 * n + 1],
                                                    layer, sender)

        @pl.when(c == sender)
        def _():
            for cp in sends:
                cp.start()
            for arrived, onward in zip(arrivals, forwards):
                arrived.wait_recv()
                onward.start()
            for cp in sends + forwards:
                cp.wait_send()

        @pl.when(c != sender)
        def _():
            for cp in forwards:
                cp.wait_recv()

    out_shape, sems = _gather_shapes(shards)
    return _call(body, name="weight_all_gather", out_shape=out_shape, in_specs=[ANY] * n, out_specs=[ANY] * n,
                 scratch_shapes=sems)(*shards)


def _row_tile(r):
    return r if r <= 704 else 512


def _grad_swap(items, name):
    n = len(items)

    def body(*refs):
        start, finish = _swap_plan(refs[:n], refs[n:2 * n], refs[2 * n:3 * n], refs[3 * n], refs[3 * n + 1])
        start()
        finish()

    out_shape, sems = _swap_shapes(items)
    return _call(body, name=name, out_shape=out_shape, in_specs=[ANY] * (2 * n), out_specs=[ANY] * n,
                 scratch_shapes=sems)(*[a0 for a0, _ in items], *[a1 for _, a1 in items])


def _swap_plan(firsts, seconds, outs, send_sems, recv_sems):
    n = len(firsts)
    x, y, c = lax.axis_index("x"), lax.axis_index("y"), lax.axis_index("c")

    def copies(srcs):
        return [pltpu.make_async_remote_copy(src_ref=srcs[a], dst_ref=outs[a], send_sem=send_sems.at[a],
                                             recv_sem=recv_sems.at[a], device_id=(x, y, 1 - c),
                                             device_id_type=MESH) for a in range(n)]

    def start():
        @pl.when(c == 0)
        def _():
            for cp in copies(seconds):
                cp.start()

        @pl.when(c == 1)
        def _():
            for cp in copies(firsts):
                cp.start()

    def finish():
        for cp in copies(firsts):
            cp.wait()

    return start, finish


def _swap_shapes(items):
    n = len(items)
    return ([jax.ShapeDtypeStruct(a0.shape, a0.dtype) for a0, _ in items],
            [pltpu.SemaphoreType.DMA((n,)), pltpu.SemaphoreType.DMA((n,))])


def _pair_add(a0, a1, r, c_arr, name, out_dtype):
    k, rr, cc = r.shape
    tr = _row_tile(rr)

    def body(c_ref, a0_ref, a1_ref, r_ref, o_ref):
        mine = jnp.where(c_ref[0] == 0, a0_ref[...], a1_ref[...])
        o_ref[...] = (mine + r_ref[...]).astype(o_ref.dtype)

    def member(which):
        def index(kk, i, c_ref):
            used = (c_ref[0] == which).astype(jnp.int32)
            return (kk * used, i * used, 0)
        return pl.BlockSpec((1, tr, cc), index)

    spec = pl.BlockSpec((1, tr, cc), lambda kk, i, c_ref: (kk, i, 0))
    grid_spec = pltpu.PrefetchScalarGridSpec(num_scalar_prefetch=1, grid=(k, rr // tr),
                                             in_specs=[member(0), member(1), spec], out_specs=spec)
    return _call(body, name=name, grid_spec=grid_spec, out_shape=jax.ShapeDtypeStruct((k, rr, cc), out_dtype),
                 compiler_params=_params("parallel", "parallel"))(c_arr, a0, a1, r)


def _exchange_plan(ins, outs, send_sems, recv_sems):
    n = len(ins)
    x, y, c = lax.axis_index("x"), lax.axis_index("y"), lax.axis_index("c")
    my_chip = 2 * x + y
    chips = [(1 - x, y), (x, 1 - y), (1 - x, 1 - y)]

    def copy(a, k, src_chip, dst_chip):
        px, py = chips[k]
        return pltpu.make_async_remote_copy(
            src_ref=ins[a].at[src_chip], dst_ref=outs[a].at[dst_chip], send_sem=send_sems.at[a, k],
            recv_sem=recv_sems.at[a, k], device_id=(px, py, c), device_id_type=MESH)

    sends = [copy(a, k, 2 * chips[k][0] + chips[k][1], my_chip) for k in range(3) for a in range(n)]
    arrivals = [copy(a, k, my_chip, 2 * chips[k][0] + chips[k][1]) for k in range(3) for a in range(n)]
    return sends, arrivals


def _exchange_shapes(hs):
    n = len(hs)
    return ([jax.ShapeDtypeStruct(h.shape, h.dtype) for h in hs],
            [pltpu.SemaphoreType.DMA((n, 3)), pltpu.SemaphoreType.DMA((n, 3))])


def _grad_exchange(hs):
    n = len(hs)

    def body(*refs):
        sends, arrivals = _exchange_plan(refs[:n], refs[n:2 * n], refs[2 * n], refs[2 * n + 1])
        for cp in sends:
            cp.start()
        for cp in arrivals:
            cp.wait_recv()
        for cp in sends:
            cp.wait_send()

    out_shape, sems = _exchange_shapes(hs)
    return _call(body, name="grad_exchange_chips", out_shape=out_shape, in_specs=[ANY] * n, out_specs=[ANY] * n,
                 scratch_shapes=sems)(*hs)


def _sum_chips(a, c_arr, name):
    _, r, cc = a.shape
    tr = _row_tile(r)

    def body(c_ref, a_ref, o_ref):
        o_ref[...] = ((a_ref[0].astype(F32) + a_ref[1].astype(F32)) + a_ref[2].astype(F32)) + a_ref[3].astype(F32)

    grid_spec = pltpu.PrefetchScalarGridSpec(
        num_scalar_prefetch=1, grid=(r // tr,),
        in_specs=[pl.BlockSpec((N_CHIPS, tr, cc), lambda i, c_ref: (0, i, 0))],
        out_specs=pl.BlockSpec((None, tr, cc), lambda i, c_ref: (c_ref[0], i, 0)))
    return _call(body, name=name, grid_spec=grid_spec, out_shape=jax.ShapeDtypeStruct((2, r, cc), F32),
                 compiler_params=_params("parallel"))(c_arr, a)


def _grad_share(bufs):
    n = len(bufs)

    def body(*refs):
        outs = refs[n:2 * n]
        send_sems, recv_sems = refs[2 * n:]
        x, y, c = lax.axis_index("x"), lax.axis_index("y"), lax.axis_index("c")
        copies = [pltpu.make_async_remote_copy(src_ref=outs[a].at[c], dst_ref=outs[a].at[c], send_sem=send_sems.at[a],
                                               recv_sem=recv_sems.at[a], device_id=(x, y, 1 - c),
                                               device_id_type=MESH) for a in range(n)]
        for cp in copies:
            cp.start()
        for a in range(n):
            pltpu.make_async_remote_copy(src_ref=outs[a].at[c], dst_ref=outs[a].at[1 - c], send_sem=send_sems.at[a],
                                         recv_sem=recv_sems.at[a], device_id=(x, y, 1 - c),
                                         device_id_type=MESH).wait_recv()
        for cp in copies:
            cp.wait_send()

    return _call(
        body, name="grad_share_cores", out_shape=[jax.ShapeDtypeStruct(b.shape, b.dtype) for b in bufs],
        in_specs=[ANY] * n, out_specs=[ANY] * n, input_output_aliases={a: a for a in range(n)},
        scratch_shapes=[pltpu.SemaphoreType.DMA((n,)), pltpu.SemaphoreType.DMA((n,))],
    )(*bufs)


def _adamw_math(g_ref, w_ref, m_ref, v_ref, d_ref, nm_ref, nv_ref):
    gv = g_ref[...]
    nm = ADAM_B1 * m_ref[...] + (1.0 - ADAM_B1) * gv
    nv = ADAM_B2 * v_ref[...] + (1.0 - ADAM_B2) * (gv * gv)
    m_hat = nm / (1.0 - ADAM_B1 ** ADAM_STEP)
    v_hat = nv / (1.0 - ADAM_B2 ** ADAM_STEP)
    d_ref[...] = -ADAM_LR * (m_hat / (jnp.sqrt(v_hat) + ADAM_EPS) + ADAM_WD * w_ref[...])
    nm_ref[...] = nm
    nv_ref[...] = nv


def _adamw_big(g, w, m, v, name):
    d, r, c = g.shape
    tr = _row_tile(r)
    spec = pl.BlockSpec((1, tr, c), lambda l, i: (l, i, 0))

    def body(*refs):
        _adamw_math(*refs)

    shp = jax.ShapeDtypeStruct(g.shape, F32)
    return _call(body, name=name, grid=(d, r // tr), in_specs=[spec] * 4, out_specs=[spec] * 3,
                 out_shape=[shp, shp, shp], compiler_params=_params("parallel", "parallel"))(g, w, m, v)


def _adamw_small(gs, ws, ms, vs):
    n = len(gs)

    def body(*refs):
        ins, outs = refs[:4 * n], refs[4 * n:]
        for k in range(n):
            _adamw_math(ins[k], ins[n + k], ins[2 * n + k], ins[3 * n + k], outs[k], outs[n + k], outs[2 * n + k])

    shp = [jax.ShapeDtypeStruct(g.shape, F32) for g in gs]
    res = _call(body, name="adamw_small", out_shape=shp * 3)(*gs, *ws, *ms, *vs)
    return res[:n], res[n:2 * n], res[2 * n:]


def _rows(a, rows):
    flat = a.reshape(-1)
    return jnp.pad(flat, (0, rows * D_MODEL - flat.shape[0])).reshape(rows, D_MODEL)


def _small_rows(p, extra=None):
    parts = [p[n].reshape(-1) for n in SMALL_NAMES]
    if extra is not None:
        parts.append(extra.reshape(-1))
    flat = jnp.concatenate(parts)
    return jnp.pad(flat, (0, ROWS_SMALL * D_MODEL - flat.shape[0])).reshape(ROWS_SMALL, D_MODEL)


CONV_SHARD = (DEPTH, 3, 2 * D_FF // N_CHIPS)
N_CONV_SHARD = DEPTH * 3 * (2 * D_FF // N_CHIPS)


def _small_pack(g, loss):
    conv = jnp.transpose(g["conv_w"].reshape(DEPTH, 3, N_CHIPS, 2 * D_FF // N_CHIPS), (2, 0, 1, 3))
    conv = jnp.stack([_rows(conv[k], ROWS_CONV) for k in range(N_CHIPS)])
    small = jnp.broadcast_to(_small_rows(g, loss), (N_CHIPS, ROWS_SMALL, D_MODEL))
    return jnp.concatenate([conv, small], axis=1)


def _unpack_small(pack):
    out = {"conv_w": pack[:ROWS_CONV].reshape(-1)[:N_CONV_SHARD].reshape(CONV_SHARD)}
    flat = pack[ROWS_CONV:].reshape(-1)
    k = 0
    for name in SMALL_NAMES:
        shape = SMALL_SHAPES[name]
        n = 1
        for d in shape:
            n *= d
        out[name] = flat[k:k + n].reshape(shape)
        k += n
    out["extra"] = flat[k]
    return out


def _assemble_layer(gathered, shards, layer, names):
    my_chip = 2 * lax.axis_index("x") + lax.axis_index("y")
    out = {}
    for name, got, own in zip(names, gathered, shards):
        full = lax.dynamic_update_index_in_dim(got, own[layer], my_chip, 0)
        if name in ("w_in", "w_up"):
            k, r, wd = full.shape
            out[name] = jnp.transpose(full, (1, 0, 2)).reshape(r, k * wd)
        else:
            out[name] = full.reshape(-1, D_MODEL)
    return out


def _gather_weights(p):
    shards = [p[n].astype(jnp.bfloat16) for n in BIG_NAMES[:4]]
    conv_all = p["conv_w"].reshape(1, -1, p["conv_w"].shape[-1])
    got = _all_gather([shards[0], conv_all], 0, 0)
    my_chip = 2 * lax.axis_index("x") + lax.axis_index("y")
    conv = lax.dynamic_update_index_in_dim(got[1], conv_all[0], my_chip, 0)
    conv = jnp.transpose(conv.reshape((N_CHIPS,) + CONV_SHARD), (1, 2, 0, 3)).reshape(DEPTH, 3, 2 * D_FF)
    full = {n: [None, None] for n in BIG_NAMES[:4]}
    full["w_in"][0] = _assemble_layer(got[:1], shards[:1], 0, ("w_in",))["w_in"]
    full["conv_w"] = conv
    return full, dict(shards=shards, assemble=_assemble_layer)


def _halves(a):
    r = a.shape[1] // 2
    return a[:, :r], a[:, r:]


def _reduce_begin(items, names, dtypes, c_arr, tag):
    return _pair_adds(items, _grad_swap(items, "grad_swap_cores_" + tag), names, dtypes, c_arr)


def _pair_adds(items, got, names, dtypes, c_arr):
    return [_pair_add(a0, a1, r, c_arr, "grad_add_cores_" + nm, dt)
            for (a0, a1), r, nm, dt in zip(items, got, names, dtypes)]


def _reduce_end(parts, sent, names, c_arr):
    my_chip = 2 * lax.axis_index("x") + lax.axis_index("y")
    full = [lax.dynamic_update_index_in_dim(p, lax.dynamic_index_in_dim(own, my_chip, 0, keepdims=False), my_chip, 0)
            for p, own in zip(parts, sent)]
    return [_sum_chips(f, c_arr, "grad_sum_chips_" + nm) for f, nm in zip(full, names)]


EARLY_NAMES = ("w_o", "w_up", "w_down", "w_in_1")


def _early_items(grads):
    return [tuple(grads[n]) for n in ("w_o", "w_up", "w_down")] + [_halves(grads["w_in"][1])]


def kernel(x, norm1_g, w_in, sgu_norm_g, sgu_w, sgu_b, pool_w, pool_scale, mix_norm_g, w_o, norm2_g, w_up, conv_w, conv_b, w_down, final_g, loss_target, m_norm1_g, m_w_in, m_sgu_norm_g, m_sgu_w, m_sgu_b, m_pool_w, m_pool_scale, m_mix_norm_g, m_w_o, m_norm2_g, m_w_up, m_conv_w, m_conv_b, m_w_down, m_final_g, v_norm1_g, v_w_in, v_sgu_norm_g, v_sgu_w, v_sgu_b, v_pool_w, v_pool_scale, v_mix_norm_g, v_w_o, v_norm2_g, v_w_up, v_conv_w, v_conv_b, v_w_down, v_final_g):
    names = ("norm1_g", "w_in", "sgu_norm_g", "sgu_w", "sgu_b", "pool_w", "pool_scale", "mix_norm_g", "w_o",
             "norm2_g", "w_up", "conv_w", "conv_b", "w_down", "final_g")
    p = dict(zip(names, (norm1_g, w_in, sgu_norm_g, sgu_w, sgu_b, pool_w, pool_scale, mix_norm_g, w_o, norm2_g,
                         w_up, conv_w, conv_b, w_down, final_g)))
    pm = dict(zip(names, (m_norm1_g, m_w_in, m_sgu_norm_g, m_sgu_w, m_sgu_b, m_pool_w, m_pool_scale, m_mix_norm_g,
                          m_w_o, m_norm2_g, m_w_up, m_conv_w, m_conv_b, m_w_down, m_final_g)))
    pv = dict(zip(names, (v_norm1_g, v_w_in, v_sgu_norm_g, v_sgu_w, v_sgu_b, v_pool_w, v_pool_scale, v_mix_norm_g,
                          v_w_o, v_norm2_g, v_w_up, v_conv_w, v_conv_b, v_w_down, v_final_g)))
    c = lax.axis_index("c")
    gathered, late = _gather_weights(p)
    full = dict(p)
    full.update(gathered)

    c_arr = jnp.reshape(c, (1,)).astype(jnp.int32)
    early_types = [ICI_DT] * len(EARLY_NAMES)
    loss, dx, grads, (sent, received) = _local_step(
        x[0], loss_target[0], full, late,
        (_early_items, lambda items, swapped: _pair_adds(items, swapped, EARLY_NAMES, early_types, c_arr)))
    early_sums = _reduce_end(received, sent, EARLY_NAMES, c_arr)
    small_pack = _small_pack(grads, loss)
    late_names = ("w_in_0", "small")
    late_sent = _reduce_begin([_halves(grads["w_in"][0]), _halves(small_pack)], late_names, [ICI_DT, F32], c_arr, "late")
    late_sums = _reduce_end(_grad_exchange(late_sent), late_sent, late_names, c_arr)
    r_o, r_up, r_down, r_in1, r_in0, r_small = _grad_share(early_sums + late_sums)
    g = dict(w_o=r_o, w_up=r_up, w_down=r_down,
             w_in=jnp.stack([r_in0.reshape(D_MODEL, -1), r_in1.reshape(D_MODEL, -1)]))
    g.update(_unpack_small(r_small.reshape(2 * SP_HALF, D_MODEL)))
    d, nm, nv = {}, {}, {}
    for n in BIG_NAMES:
        d[n], nm[n], nv[n] = _adamw_big(g[n], p[n], pm[n], pv[n], "adamw_" + n)

    def two_d(a):
        return a.reshape(1, -1) if a.ndim == 1 else a

    ds, ms, vs = _adamw_small([two_d(g[n]) for n in SMALL_NAMES], [two_d(p[n]) for n in SMALL_NAMES],
                              [two_d(pm[n]) for n in SMALL_NAMES], [two_d(pv[n]) for n in SMALL_NAMES])
    for k, n in enumerate(SMALL_NAMES):
        d[n], nm[n], nv[n] = (a.reshape(p[n].shape) for a in (ds[k], ms[k], vs[k]))
    return (g["extra"], dx[None], *[g[n] for n in names], *[d[n] for n in names], *[nm[n] for n in names],
            *[nv[n] for n in names])
```
